```python
import jax, jax.numpy as jnp
from jax import lax
import numpy as np

D_MODEL = 1024
BATCH = 8
SEQ = 8192
DEPTH = 1

D_MIX = D_MODEL
ATTN_WIDTH = D_MIX // 2
N_HEADS = 8
HEAD_DIM = ATTN_WIDTH // N_HEADS
DILATED_PATTERNS = ((128, 1), (512, 4), (2048, 16))
BLOCK = 128
POOL_WIDTH = D_MIX - ATTN_WIDTH
POOL_WINDOWS = (2, 4, 8, 16)
N_POOL_GROUPS = len(POOL_WINDOWS)
POOL_GROUP_DIM = POOL_WIDTH // N_POOL_GROUPS
D_FF = 2816
CONV_WIDTH = 3
EPS = 1e-6
NEG_INF = -1e30

kernel_name = "hybrid_dilated_attn_pool_convffn_sandwich"


def rms_norm(x, g):
    x32 = x.astype(jnp.float32)
    y = x32 * lax.rsqrt(jnp.mean(x32 * x32, axis=-1, keepdims=True) + EPS)
    return (y * g.astype(jnp.float32)).astype(x.dtype)


def dilated_window_attention(q, k, v, window, dilation):
    B, H, S, hd = q.shape
    span = window // dilation
    L = S // dilation
    nb = -(-L // BLOCK)
    Lp = nb * BLOCK

    def to_res(a):
        return a.reshape(B, H, L, dilation, hd).transpose(0, 1, 3, 2, 4)

    lead = ((0, 0), (0, 0), (0, 0))
    qb = jnp.pad(to_res(q), lead + ((0, Lp - L), (0, 0))).reshape(B, H, dilation, nb, BLOCK, hd)

    def key_blocks(a):
        ap = jnp.pad(to_res(a), lead + ((BLOCK, Lp - L), (0, 0)))
        prev = ap[:, :, :, :Lp].reshape(B, H, dilation, nb, BLOCK, hd)
        cur = ap[:, :, :, BLOCK:].reshape(B, H, dilation, nb, BLOCK, hd)
        return jnp.concatenate([prev, cur], axis=4)

    kb = key_blocks(k)
    vb = key_blocks(v)
    scores = jnp.einsum('bhrnqd,bhrnkd->bhrnqk', qb.astype(jnp.float32),
                        kb.astype(jnp.float32)) * (hd ** -0.5)
    qi = jnp.arange(BLOCK)[:, None]
    ki = jnp.arange(2 * BLOCK)[None, :]
    blk = jnp.arange(nb)[:, None, None]
    dist = qi + BLOCK - ki
    key_pos = blk * BLOCK - BLOCK + ki
    mask = (dist >= 0) & (dist <= span) & (key_pos >= 0)
    scores = jnp.where(mask, scores, NEG_INF)
    m = jnp.max(scores, axis=-1, keepdims=True)
    p = jnp.exp(scores - m)
    denom = jnp.sum(p, axis=-1, keepdims=True)
    out = jnp.einsum('bhrnqk,bhrnkd->bhrnqd', p, vb.astype(jnp.float32)) / denom
    lse = (m + jnp.log(denom))[..., 0]
    out = out.reshape(B, H, dilation, Lp, hd)[:, :, :, :L]
    out = out.transpose(0, 1, 3, 2, 4).reshape(B, H, S, hd)
    lse = lse.reshape(B, H, dilation, Lp)[..., :L].transpose(0, 1, 3, 2).reshape(B, H, S)
    return out, lse


def dilated_mixture_attention(q, k, v):
    outs, lses = [], []
    for window, dilation in DILATED_PATTERNS:
        o, l = dilated_window_attention(q, k, v, window, dilation)
        outs.append(o)
        lses.append(l)
    w = jax.nn.softmax(jnp.stack(lses, axis=0), axis=0)
    return jnp.sum(w[..., None] * jnp.stack(outs, axis=0), axis=0)


def multiscale_pool_mixer(u, pool_w, pool_scale):
    B, S, _ = u.shape
    ug = u.astype(jnp.float32).reshape(B, S, N_POOL_GROUPS, POOL_GROUP_DIM)
    t = jnp.arange(S)
    outs = []
    for g, w in enumerate(POOL_WINDOWS):
        xg = ug[:, :, g]
        csum = jnp.cumsum(xg, axis=1)
        lagged = jnp.pad(csum, ((0, 0), (w, 0), (0, 0)))[:, :S]
        count = jnp.minimum(t + 1, w).astype(jnp.float32)[None, :, None]
        pooled = (csum - lagged) / count - xg
        outs.append(jnp.einsum('bsc,cd->bsd', pooled, pool_w[g].astype(jnp.float32)))
    y = jnp.concatenate(outs, axis=-1) * pool_scale.astype(jnp.float32)
    return y.astype(u.dtype)


def conv_gated_mlp(h, w_up, conv_w, conv_b, w_down):
    S = h.shape[1]
    u = jnp.einsum('bsd,df->bsf', h, w_up)
    up = jnp.pad(u, ((0, 0), (CONV_WIDTH - 1, 0), (0, 0)))
    c = conv_b + sum(up[:, j:j + S] * conv_w[j] for j in range(CONV_WIDTH))
    gate, val = jnp.split(c, 2, axis=-1)
    y = jax.nn.gelu(gate, approximate=True) * val
    return jnp.einsum('bsf,fd->bsd', y, w_down)


def _fwd_setup_inputs(seed: int = 0) -> dict:
    key = jax.random.key(seed)
    ks = jax.random.split(key, 16)
    f32 = jnp.float32

    def nrm(k, shape, scale):
        return jax.random.normal(k, shape, f32) * scale

    def gain(k, n):
        return 1.0 + 0.05 * jax.random.normal(k, (DEPTH, n), f32)

    n_in = 3 * ATTN_WIDTH + POOL_WIDTH
    return {
        "x": jax.random.normal(ks[0], (BATCH, SEQ, D_MODEL), f32),
        "g_mix_pre": gain(ks[1], D_MODEL),
        "w_in": nrm(ks[2], (DEPTH, D_MODEL, n_in), D_MODEL ** -0.5),
        "pool_w": nrm(ks[3], (DEPTH, N_POOL_GROUPS, POOL_GROUP_DIM, POOL_GROUP_DIM), POOL_GROUP_DIM ** -0.5),
        "pool_scale": 1.0 + 0.1 * jax.random.normal(ks[4], (DEPTH, POOL_WIDTH), f32),
        "w_out": nrm(ks[5], (DEPTH, D_MIX, D_MODEL), D_MIX ** -0.5),
        "g_mix_post": gain(ks[6], D_MODEL),
        "g_ffn_pre": gain(ks[7], D_MODEL),
        "w_up": nrm(ks[8], (DEPTH, D_MODEL, 2 * D_FF), D_MODEL ** -0.5),
        "conv_w": nrm(ks[9], (DEPTH, CONV_WIDTH, 2 * D_FF), CONV_WIDTH ** -0.5),
        "conv_b": nrm(ks[10], (DEPTH, 2 * D_FF), 0.01),
        "w_down": nrm(ks[11], (DEPTH, D_FF, D_MODEL), D_FF ** -0.5),
        "g_ffn_post": gain(ks[12], D_MODEL),
    }


def _fwd_reference(x, g_mix_pre, w_in, pool_w, pool_scale, w_out, g_mix_post,
              g_ffn_pre, w_up, conv_w, conv_b, w_down, g_ffn_post):
    B, S, _ = x.shape
    for layer in range(DEPTH):
        h = rms_norm(x, g_mix_pre[layer])
        proj = jnp.einsum('bsd,dn->bsn', h, w_in[layer])
        q, k, v, pool_in = jnp.split(
            proj, [ATTN_WIDTH, 2 * ATTN_WIDTH, 3 * ATTN_WIDTH], axis=-1)

        def heads(a):
            return a.reshape(B, S, N_HEADS, HEAD_DIM).transpose(0, 2, 1, 3)

        attn = dilated_mixture_attention(heads(q), heads(k), heads(v))
        attn = attn.transpose(0, 2, 1, 3).reshape(B, S, ATTN_WIDTH).astype(x.dtype)
        pool = multiscale_pool_mixer(pool_in, pool_w[layer], pool_scale[layer])
        mixed = jnp.einsum('bsm,md->bsd', jnp.concatenate([attn, pool], axis=-1), w_out[layer])
        x = x + rms_norm(mixed, g_mix_post[layer])
        h = rms_norm(x, g_ffn_pre[layer])
        f = conv_gated_mlp(h, w_up[layer], conv_w[layer], conv_b[layer], w_down[layer])
        x = x + rms_norm(f, g_ffn_post[layer])
    return x


import jax as _jax
import jax.numpy as _jnp

TWIN_FORMAT = 'train_step'
FWD_PARAMS = ['x', 'g_mix_pre', 'w_in', 'pool_w', 'pool_scale', 'w_out', 'g_mix_post', 'g_ffn_pre', 'w_up', 'conv_w', 'conv_b', 'w_down', 'g_ffn_post']
TWIN_WEIGHTS = ['g_mix_pre', 'w_in', 'pool_w', 'pool_scale', 'w_out', 'g_mix_post', 'g_ffn_pre', 'w_up', 'conv_w', 'conv_b', 'w_down', 'g_ffn_post']
TWIN_DIFF_INPUT = 'x'
TWIN_INPUTS = ['x', 'g_mix_pre', 'w_in', 'pool_w', 'pool_scale', 'w_out', 'g_mix_post', 'g_ffn_pre', 'w_up', 'conv_w', 'conv_b', 'w_down', 'g_ffn_post', 'loss_target', 'm_g_mix_pre', 'm_w_in', 'm_pool_w', 'm_pool_scale', 'm_w_out', 'm_g_mix_post', 'm_g_ffn_pre', 'm_w_up', 'm_conv_w', 'm_conv_b', 'm_w_down', 'm_g_ffn_post', 'v_g_mix_pre', 'v_w_in', 'v_pool_w', 'v_pool_scale', 'v_w_out', 'v_g_mix_post', 'v_g_ffn_pre', 'v_w_up', 'v_conv_w', 'v_conv_b', 'v_w_down', 'v_g_ffn_post']
TWIN_OUTPUTS = ['loss', 'grad_x', 'grad_g_mix_pre', 'grad_w_in', 'grad_pool_w', 'grad_pool_scale', 'grad_w_out', 'grad_g_mix_post', 'grad_g_ffn_pre', 'grad_w_up', 'grad_conv_w', 'grad_conv_b', 'grad_w_down', 'grad_g_ffn_post', 'delta_g_mix_pre', 'delta_w_in', 'delta_pool_w', 'delta_pool_scale', 'delta_w_out', 'delta_g_mix_post', 'delta_g_ffn_pre', 'delta_w_up', 'delta_conv_w', 'delta_conv_b', 'delta_w_down', 'delta_g_ffn_post', 'new_m_g_mix_pre', 'new_m_w_in', 'new_m_pool_w', 'new_m_pool_scale', 'new_m_w_out', 'new_m_g_mix_post', 'new_m_g_ffn_pre', 'new_m_w_up', 'new_m_conv_w', 'new_m_conv_b', 'new_m_w_down', 'new_m_g_ffn_post', 'new_v_g_mix_pre', 'new_v_w_in', 'new_v_pool_w', 'new_v_pool_scale', 'new_v_w_out', 'new_v_g_mix_post', 'new_v_g_ffn_pre', 'new_v_w_up', 'new_v_conv_w', 'new_v_conv_b', 'new_v_w_down', 'new_v_g_ffn_post']
TWIN_LEAF_KINDS = {'loss': 'loss', 'grad_x': 'grad_x', 'grad_g_mix_pre': 'grad_w', 'grad_w_in': 'grad_w', 'grad_pool_w': 'grad_w', 'grad_pool_scale': 'grad_w', 'grad_w_out': 'grad_w', 'grad_g_mix_post': 'grad_w', 'grad_g_ffn_pre': 'grad_w', 'grad_w_up': 'grad_w', 'grad_conv_w': 'grad_w', 'grad_conv_b': 'grad_w', 'grad_w_down': 'grad_w', 'grad_g_ffn_post': 'grad_w', 'delta_g_mix_pre': 'delta_w', 'delta_w_in': 'delta_w', 'delta_pool_w': 'delta_w', 'delta_pool_scale': 'delta_w', 'delta_w_out': 'delta_w', 'delta_g_mix_post': 'delta_w', 'delta_g_ffn_pre': 'delta_w', 'delta_w_up': 'delta_w', 'delta_conv_w': 'delta_w', 'delta_conv_b': 'delta_w', 'delta_w_down': 'delta_w', 'delta_g_ffn_post': 'delta_w', 'new_m_g_mix_pre': 'new_m', 'new_m_w_in': 'new_m', 'new_m_pool_w': 'new_m', 'new_m_pool_scale': 'new_m', 'new_m_w_out': 'new_m', 'new_m_g_mix_post': 'new_m', 'new_m_g_ffn_pre': 'new_m', 'new_m_w_up': 'new_m', 'new_m_conv_w': 'new_m', 'new_m_conv_b': 'new_m', 'new_m_w_down': 'new_m', 'new_m_g_ffn_post': 'new_m', 'new_v_g_mix_pre': 'new_v', 'new_v_w_in': 'new_v', 'new_v_pool_w': 'new_v', 'new_v_pool_scale': 'new_v', 'new_v_w_out': 'new_v', 'new_v_g_mix_post': 'new_v', 'new_v_g_ffn_pre': 'new_v', 'new_v_w_up': 'new_v', 'new_v_conv_w': 'new_v', 'new_v_conv_b': 'new_v', 'new_v_w_down': 'new_v', 'new_v_g_ffn_post': 'new_v'}


def _forward(args):
    return _fwd_reference(*[args[k] for k in FWD_PARAMS])


def _output_shape():
    def fwd():
        inp = _fwd_setup_inputs(0)
        return _fwd_reference(*[inp[k] for k in FWD_PARAMS])
    out = _jax.eval_shape(fwd)
    return out.shape, out.dtype

N_MICROBATCH = 1
ADAM_LR = 0.001
ADAM_B1 = 0.9
ADAM_B2 = 0.999
ADAM_EPS = 1e-08
ADAM_WD = 0.01
ADAM_STEP = 10
PER_EXAMPLE_BATCH_AXIS = {'x': 0, 'loss_target': 0}
SHARED_INPUTS = []
_WEIGHT_DTYPES = {'g_mix_pre': _jnp.float32, 'w_in': _jnp.float32, 'pool_w': _jnp.float32, 'pool_scale': _jnp.float32, 'w_out': _jnp.float32, 'g_mix_post': _jnp.float32, 'g_ffn_pre': _jnp.float32, 'w_up': _jnp.float32, 'conv_w': _jnp.float32, 'conv_b': _jnp.float32, 'w_down': _jnp.float32, 'g_ffn_post': _jnp.float32}
MOMENT_SCALE = {'g_mix_pre': 1.485939e+00, 'w_in': 9.483659e-01, 'pool_w': 2.178811e+00, 'pool_scale': 2.269643e+00, 'w_out': 1.617443e+00, 'g_mix_post': 6.427108e+01, 'g_ffn_pre': 9.296320e-01, 'w_up': 4.039906e-01, 'conv_w': 4.797460e-01, 'conv_b': 9.431179e-01, 'w_down': 8.939045e-01, 'g_ffn_post': 6.428721e+01}


def _to_microbatches(a, axis):
    t = _jnp.moveaxis(a, axis, 0)
    t = t.reshape((N_MICROBATCH, t.shape[0] // N_MICROBATCH) + t.shape[1:])
    return _jnp.moveaxis(t, 1, axis + 1)


def setup_inputs(seed: int = 0) -> dict:
    inp = _fwd_setup_inputs(seed)
    key = _jax.random.fold_in(_jax.random.key(seed), 7919)
    shape, _ = _output_shape()
    out = dict(inp)
    out["loss_target"] = _jax.random.normal(_jax.random.fold_in(key, 0), shape, _jnp.float32)
    for i, name in enumerate(TWIN_WEIGHTS):
        w = inp[name].astype(_jnp.float32)
        if MOMENT_SCALE is None:
            s = _jnp.sqrt(_jnp.mean(_jnp.square(w)) + 1e-30)
        else:
            s = MOMENT_SCALE[name]
        km, kv = _jax.random.split(_jax.random.fold_in(key, i + 1))
        out[name] = w
        out["m_" + name] = s * _jax.random.normal(km, w.shape, _jnp.float32)
        out["v_" + name] = (s * s) * _jax.random.uniform(kv, w.shape, _jnp.float32, 0.5, 1.5)
    if N_MICROBATCH > 1:
        for name, axis in PER_EXAMPLE_BATCH_AXIS.items():
            out[name] = _to_microbatches(out[name], axis)
    return {'x': out['x'], 'g_mix_pre': out['g_mix_pre'], 'w_in': out['w_in'], 'pool_w': out['pool_w'], 'pool_scale': out['pool_scale'], 'w_out': out['w_out'], 'g_mix_post': out['g_mix_post'], 'g_ffn_pre': out['g_ffn_pre'], 'w_up': out['w_up'], 'conv_w': out['conv_w'], 'conv_b': out['conv_b'], 'w_down': out['w_down'], 'g_ffn_post': out['g_ffn_post'], 'loss_target': out['loss_target'], 'm_g_mix_pre': out['m_g_mix_pre'], 'm_w_in': out['m_w_in'], 'm_pool_w': out['m_pool_w'], 'm_pool_scale': out['m_pool_scale'], 'm_w_out': out['m_w_out'], 'm_g_mix_post': out['m_g_mix_post'], 'm_g_ffn_pre': out['m_g_ffn_pre'], 'm_w_up': out['m_w_up'], 'm_conv_w': out['m_conv_w'], 'm_conv_b': out['m_conv_b'], 'm_w_down': out['m_w_down'], 'm_g_ffn_post': out['m_g_ffn_post'], 'v_g_mix_pre': out['v_g_mix_pre'], 'v_w_in': out['v_w_in'], 'v_pool_w': out['v_pool_w'], 'v_pool_scale': out['v_pool_scale'], 'v_w_out': out['v_w_out'], 'v_g_mix_post': out['v_g_mix_post'], 'v_g_ffn_pre': out['v_g_ffn_pre'], 'v_w_up': out['v_w_up'], 'v_conv_w': out['v_conv_w'], 'v_conv_b': out['v_conv_b'], 'v_w_down': out['v_w_down'], 'v_g_ffn_post': out['v_g_ffn_post']}


def _loss(weights, diff, rest, loss_target):
    with _jax.named_scope("forward"):
        args = {**rest, TWIN_DIFF_INPUT: diff, **{k: w.astype(_WEIGHT_DTYPES[k]) for k, w in weights.items()}}
        y = _forward(args)
    with _jax.named_scope("loss_head"):
        err = _jnp.square(y.astype(_jnp.float32) - loss_target)
        return 0.5 * _jnp.sum(_jnp.mean(err, axis=-1)) if err.ndim else 0.5 * err


def _adamw(w, g, m, v):
    m = ADAM_B1 * m + (1.0 - ADAM_B1) * g
    v = ADAM_B2 * v + (1.0 - ADAM_B2) * _jnp.square(g)
    m_hat = m / (1.0 - ADAM_B1 ** ADAM_STEP)
    v_hat = v / (1.0 - ADAM_B2 ** ADAM_STEP)
    delta = -ADAM_LR * (m_hat / (_jnp.sqrt(v_hat) + ADAM_EPS) + ADAM_WD * w)
    return delta, m, v


def reference(x, g_mix_pre, w_in, pool_w, pool_scale, w_out, g_mix_post, g_ffn_pre, w_up, conv_w, conv_b, w_down, g_ffn_post, loss_target, m_g_mix_pre, m_w_in, m_pool_w, m_pool_scale, m_w_out, m_g_mix_post, m_g_ffn_pre, m_w_up, m_conv_w, m_conv_b, m_w_down, m_g_ffn_post, v_g_mix_pre, v_w_in, v_pool_w, v_pool_scale, v_w_out, v_g_mix_post, v_g_ffn_pre, v_w_up, v_conv_w, v_conv_b, v_w_down, v_g_ffn_post):
    given = dict(x=x, g_mix_pre=g_mix_pre, w_in=w_in, pool_w=pool_w, pool_scale=pool_scale, w_out=w_out, g_mix_post=g_mix_post, g_ffn_pre=g_ffn_pre, w_up=w_up, conv_w=conv_w, conv_b=conv_b, w_down=w_down, g_ffn_post=g_ffn_post, loss_target=loss_target, m_g_mix_pre=m_g_mix_pre, m_w_in=m_w_in, m_pool_w=m_pool_w, m_pool_scale=m_pool_scale, m_w_out=m_w_out, m_g_mix_post=m_g_mix_post, m_g_ffn_pre=m_g_ffn_pre, m_w_up=m_w_up, m_conv_w=m_conv_w, m_conv_b=m_conv_b, m_w_down=m_w_down, m_g_ffn_post=m_g_ffn_post, v_g_mix_pre=v_g_mix_pre, v_w_in=v_w_in, v_pool_w=v_pool_w, v_pool_scale=v_pool_scale, v_w_out=v_w_out, v_g_mix_post=v_g_mix_post, v_g_ffn_pre=v_g_ffn_pre, v_w_up=v_w_up, v_conv_w=v_conv_w, v_conv_b=v_conv_b, v_w_down=v_w_down, v_g_ffn_post=v_g_ffn_post)
    weights = {n: given[n] for n in TWIN_WEIGHTS}
    shared = {n: given[n] for n in SHARED_INPUTS}
    per_example = {n: given[n] for n in ['x']}
    grad_fn = _jax.value_and_grad(_loss, argnums=(0, 1))

    def one_microbatch(ex, loss_target):
        ex = dict(ex)
        diff = ex.pop(TWIN_DIFF_INPUT)
        return grad_fn(weights, diff, {**shared, **ex}, loss_target)

    if N_MICROBATCH == 1:
        loss, (grad_w, grad_x) = one_microbatch(per_example, given["loss_target"])
    else:
        def body(carry, xs):
            loss_sum, grad_sum = carry
            l_k, (gw_k, gx_k) = one_microbatch(xs[0], xs[1])
            with _jax.named_scope("update"):
                return (loss_sum + l_k, _jax.tree.map(_jnp.add, grad_sum, gw_k)), gx_k

        init = (_jnp.zeros((), _jnp.float32), _jax.tree.map(_jnp.zeros_like, weights))
        (loss, grad_w), grad_x = _jax.lax.scan(body, init, (per_example, given["loss_target"]))
    with _jax.named_scope("update"):
        delta_w, new_m, new_v = {}, {}, {}
        for n in TWIN_WEIGHTS:
            delta_w[n], new_m[n], new_v[n] = _adamw(weights[n], grad_w[n], given["m_" + n], given["v_" + n])
    return (loss, grad_x, *[grad_w[n] for n in TWIN_WEIGHTS], *[delta_w[n] for n in TWIN_WEIGHTS],
            *[new_m[n] for n in TWIN_WEIGHTS], *[new_v[n] for n in TWIN_WEIGHTS])
```

```python
import functools
import math

import jax
import jax.numpy as jnp
from jax import lax
from jax.experimental import pallas as pl
from jax.experimental.pallas import tpu as pltpu

F32 = jnp.float32
BF16 = jnp.bfloat16

D_MODEL = 1024
ATTN_WIDTH = 512
N_HEADS = 8
HEAD_DIM = 64
DILATIONS = (1, 4, 16)
BLOCK = 128
POOL_WIDTH = 512
POOL_WINDOWS = (2, 4, 8, 16)
POOL_GROUP_DIM = 128
D_FF = 2816
EPS = 1e-6
NEG_INF = -1e30
SCALE = HEAD_DIM ** -0.5

ADAM_LR = 0.001
ADAM_B1 = 0.9
ADAM_B2 = 0.999
ADAM_EPS = 1e-08
ADAM_WD = 0.01
ADAM_STEP = 10

N_DEV = 8
HALO = 16
V7X_VMEM_LIMIT = 56 * 1024 * 1024

MESH = pl.DeviceIdType.MESH
ANY = pl.BlockSpec(memory_space=pl.ANY)
VMEM = pl.BlockSpec(memory_space=pltpu.VMEM)

NT = (((1,), (1,)), ((), ()))
NN = (((1,), (0,)), ((), ()))
TN = (((0,), (0,)), ((), ()))


def _cp(*sem):
    return pltpu.CompilerParams(dimension_semantics=sem, vmem_limit_bytes=V7X_VMEM_LIMIT)


def _dot(a, b, dn):
    return lax.dot_general(a, b, dn, preferred_element_type=F32)


def _rms_bwd(xin, g, dy):
    r = lax.rsqrt(jnp.mean(xin * xin, axis=-1, keepdims=True) + EPS)
    xh = xin * r
    gdy = g * dy
    dx = r * (gdy - xh * jnp.mean(gdy * xh, axis=-1, keepdims=True))
    dg = jnp.sum(dy * xh, axis=0, keepdims=True)
    return dx, dg


def _rms_norm(x, g, *, name, tm=512):
    S, D = x.shape

    def body(x_ref, g_ref, o_ref):
        xv = x_ref[...]
        r = lax.rsqrt(jnp.mean(xv * xv, axis=-1, keepdims=True) + EPS)
        o_ref[...] = (xv * r * g_ref[...]).astype(BF16)

    return pl.pallas_call(
        body, name=name, grid=(S // tm,),
        in_specs=[pl.BlockSpec((tm, D), lambda i: (i, 0)), pl.BlockSpec((1, D), lambda i: (0, 0))],
        out_specs=pl.BlockSpec((tm, D), lambda i: (i, 0)),
        out_shape=jax.ShapeDtypeStruct((S, D), BF16),
        compiler_params=_cp("parallel"),
    )(x, g)


def _matmul(a, b, *, trans_b, out_dtype, tm, tn, name):
    M, K = a.shape
    N = b.shape[0] if trans_b else b.shape[1]
    dn = NT if trans_b else NN

    def body(a_ref, b_ref, o_ref):
        o_ref[...] = _dot(a_ref[...], b_ref[...], dn).astype(out_dtype)

    b_spec = (pl.BlockSpec((tn, K), lambda i, j: (j, 0)) if trans_b
              else pl.BlockSpec((K, tn), lambda i, j: (0, j)))
    return pl.pallas_call(
        body, name=name, grid=(M // tm, N // tn),
        in_specs=[pl.BlockSpec((tm, K), lambda i, j: (i, 0)), b_spec],
        out_specs=pl.BlockSpec((tm, tn), lambda i, j: (i, j)),
        out_shape=jax.ShapeDtypeStruct((M, N), out_dtype),
        compiler_params=_cp("parallel", "parallel"),
    )(a, b)


def _matmul_tn(a, b, *, ta, ts, name):
    S, Ka = a.shape
    Nb = b.shape[1]
    ns = S // ts

    def body(a_ref, b_ref, o_ref, acc):
        s = pl.program_id(1)

        @pl.when(s == 0)
        def _():
            acc[...] = jnp.zeros_like(acc)

        acc[...] += _dot(a_ref[...], b_ref[...], TN)

        @pl.when(s == ns - 1)
        def _():
            o_ref[...] = acc[...].astype(BF16)

    return pl.pallas_call(
        body, name=name, grid=(Ka // ta, ns),
        in_specs=[pl.BlockSpec((ts, ta), lambda i, s: (s, i)), pl.BlockSpec((ts, Nb), lambda i, s: (s, 0))],
        out_specs=pl.BlockSpec((ta, Nb), lambda i, s: (i, 0)),
        out_shape=jax.ShapeDtypeStruct((Ka, Nb), BF16),
        scratch_shapes=[pltpu.VMEM((ta, Nb), F32)],
        compiler_params=_cp("parallel", "arbitrary"),
    )(a, b)


def _mix_out(cat, w_out, x, g_post, g_next, *, name, tm=256):
    S, K = cat.shape
    D = w_out.shape[1]

    def body(c_ref, w_ref, x_ref, gp_ref, gn_ref, mixed_ref, x2_ref, h2_ref):
        mixed = _dot(c_ref[...], w_ref[...], NN)
        r = lax.rsqrt(jnp.mean(mixed * mixed, axis=-1, keepdims=True) + EPS)
        x2 = x_ref[...] + mixed * r * gp_ref[...]
        r2 = lax.rsqrt(jnp.mean(x2 * x2, axis=-1, keepdims=True) + EPS)
        mixed_ref[...] = mixed
        x2_ref[...] = x2
        h2_ref[...] = (x2 * r2 * gn_ref[...]).astype(BF16)

    row = lambda i: (i, 0)
    fix = lambda i: (0, 0)
    return pl.pallas_call(
        body, name=name, grid=(S // tm,),
        in_specs=[pl.BlockSpec((tm, K), row), pl.BlockSpec((K, D), fix), pl.BlockSpec((tm, D), row),
                  pl.BlockSpec((1, D), fix), pl.BlockSpec((1, D), fix)],
        out_specs=[pl.BlockSpec((tm, D), row)] * 3,
        out_shape=[jax.ShapeDtypeStruct((S, D), F32), jax.ShapeDtypeStruct((S, D), F32),
                   jax.ShapeDtypeStruct((S, D), BF16)],
        compiler_params=_cp("parallel"),
    )(cat, w_out, x, g_post, g_next)


def _ffn_out(y, w_down, x2, target, g_post, *, name, tm=256):
    S, K = y.shape
    D = w_down.shape[1]

    def body(y_ref, w_ref, x2_ref, t_ref, g_ref, df_ref, dout_ref, loss_ref, gg_ref):
        i = pl.program_id(0)

        @pl.when(i == 0)
        def _():
            loss_ref[...] = jnp.zeros_like(loss_ref)
            gg_ref[...] = jnp.zeros_like(gg_ref)

        f = _dot(y_ref[...], w_ref[...], NN)
        g = g_ref[...]
        r = lax.rsqrt(jnp.mean(f * f, axis=-1, keepdims=True) + EPS)
        out = x2_ref[...] + f * r * g
        err = out - t_ref[...]
        dy = err * (1.0 / D)
        df, dg = _rms_bwd(f, g, dy)
        df_ref[...] = df.astype(BF16)
        dout_ref[...] = dy
        gg_ref[...] += dg
        loss_ref[...] += 0.5 * jnp.sum(jnp.mean(err * err, axis=-1, keepdims=True))

    row = lambda i: (i, 0)
    fix = lambda i: (0, 0)
    return pl.pallas_call(
        body, name=name, grid=(S // tm,),
        in_specs=[pl.BlockSpec((tm, K), row), pl.BlockSpec((K, D), fix), pl.BlockSpec((tm, D), row),
                  pl.BlockSpec((tm, D), row), pl.BlockSpec((1, D), fix)],
        out_specs=[pl.BlockSpec((tm, D), row), pl.BlockSpec((tm, D), row),
                   pl.BlockSpec((8, 128), fix), pl.BlockSpec((1, D), fix)],
        out_shape=[jax.ShapeDtypeStruct((S, D), BF16), jax.ShapeDtypeStruct((S, D), F32),
                   jax.ShapeDtypeStruct((8, 128), F32), jax.ShapeDtypeStruct((1, D), F32)],
        compiler_params=_cp("arbitrary"),
    )(y, w_down, x2, target, g_post)


def _dgrad_norm(a_list, w, resid, xin, g, second, *, tk, name, tm=512):
    S, Kp = a_list[0].shape
    na = len(a_list)
    D = w.shape[1]
    kper = Kp // tk
    nk = na * kper
    two = second is not None

    def body(*refs):
        a_refs = refs[:na]
        w_ref, r_ref, x_ref, g_ref = refs[na:na + 4]
        pos = na + 4
        if two:
            x2_ref, g2_ref = refs[pos:pos + 2]
            pos += 2
        dx_ref, gg_ref = refs[pos:pos + 2]
        pos += 2
        if two:
            d2_ref, gg2_ref = refs[pos:pos + 2]
            pos += 2
        acc = refs[pos]
        i = pl.program_id(0)
        k = pl.program_id(1)

        @pl.when(k == 0)
        def _():
            acc[...] = jnp.zeros_like(acc)

        @pl.when((i == 0) & (k == 0))
        def _():
            gg_ref[...] = jnp.zeros_like(gg_ref)
            if two:
                gg2_ref[...] = jnp.zeros_like(gg2_ref)

        for q in range(na):
            @pl.when(k // kper == q)
            def _(q=q):
                acc[...] += _dot(a_refs[q][...], w_ref[...], NN)

        @pl.when(k == nk - 1)
        def _():
            d1, dg1 = _rms_bwd(x_ref[...], g_ref[...], acc[...])
            dx = r_ref[...] + d1
            dx_ref[...] = dx
            gg_ref[...] += dg1
            if two:
                d2, dg2 = _rms_bwd(x2_ref[...], g2_ref[...], dx)
                d2_ref[...] = d2.astype(BF16)
                gg2_ref[...] += dg2

    row = lambda i, k: (i, 0)
    fix = lambda i, k: (0, 0)
    a_specs = [pl.BlockSpec((tm, tk), functools.partial(
        lambda i, k, q: (i, jnp.clip(k - q * kper, 0, kper - 1)), q=q)) for q in range(na)]
    in_specs = a_specs + [pl.BlockSpec((tk, D), lambda i, k: (k, 0)), pl.BlockSpec((tm, D), row),
                          pl.BlockSpec((tm, D), row), pl.BlockSpec((1, D), fix)]
    args = list(a_list) + [w, resid, xin, g]
    out_specs = [pl.BlockSpec((tm, D), row), pl.BlockSpec((1, D), fix)]
    out_shape = [jax.ShapeDtypeStruct((S, D), F32), jax.ShapeDtypeStruct((1, D), F32)]
    if two:
        in_specs += [pl.BlockSpec((tm, D), row), pl.BlockSpec((1, D), fix)]
        args += list(second)
        out_specs += [pl.BlockSpec((tm, D), row), pl.BlockSpec((1, D), fix)]
        out_shape += [jax.ShapeDtypeStruct((S, D), BF16), jax.ShapeDtypeStruct((1, D), F32)]
    return pl.pallas_call(
        body, name=name, grid=(S // tm, nk), in_specs=in_specs, out_specs=out_specs, out_shape=out_shape,
        scratch_shapes=[pltpu.VMEM((tm, D), F32)],
        compiler_params=_cp("arbitrary", "arbitrary"),
    )(*args)


def _band_mask(first_block):
    qi = lax.broadcasted_iota(jnp.int32, (BLOCK, 2 * BLOCK), 0)
    ki = lax.broadcasted_iota(jnp.int32, (BLOCK, 2 * BLOCK), 1)
    first_key = jnp.where(first_block, BLOCK, 0)
    return (ki >= qi) & (ki <= qi + BLOCK) & (ki >= first_key)


def _lane_masks():
    lane = lax.broadcasted_iota(jnp.int32, (1, 2 * HEAD_DIM), 1)
    return (lane < HEAD_DIM, lane >= HEAD_DIM)


def _attn_fwd(qkv, d, *, name):
    S = qkv.shape[0]
    L = S // d
    nb = L // BLOCK
    W = ATTN_WIDTH
    qkv_v = qkv.reshape(L, d * 3 * W)

    def body(q_ref, kp_ref, kc_ref, vp_ref, vc_ref, o_ref, lse_ref):
        i = pl.program_id(1)
        mask = _band_mask(i == 0)
        lms = _lane_masks()
        for hp in range(N_HEADS // 2):
            sl = slice(2 * HEAD_DIM * hp, 2 * HEAD_DIM * (hp + 1))
            q = q_ref[:, sl]
            kcat = jnp.concatenate([kp_ref[:, sl], kc_ref[:, sl]], axis=0)
            vcat = jnp.concatenate([vp_ref[:, sl], vc_ref[:, sl]], axis=0)
            o_slab = jnp.zeros((BLOCK, 2 * HEAD_DIM), F32)
            lse_slab = jnp.zeros((BLOCK, 2 * HEAD_DIM), F32)
            for lm in lms:
                qs = jnp.where(lm, q, jnp.zeros_like(q)) * SCALE
                vh = jnp.where(lm, vcat, jnp.zeros_like(vcat))
                s = jnp.where(mask, _dot(qs, kcat, NT), NEG_INF)
                m = jnp.max(s, axis=-1, keepdims=True)
                p = jnp.exp(s - m)
                l = jnp.sum(p, axis=-1, keepdims=True)
                o_slab = o_slab + _dot(p.astype(BF16), vh, NN) / l
                lse_slab = jnp.where(lm, m + jnp.log(l), lse_slab)
            o_ref[:, sl] = o_slab
            lse_ref[:, sl] = lse_slab

    prev = lambda r, i: jnp.maximum(i - 1, 0)
    in_specs = [
        pl.BlockSpec((BLOCK, W), lambda r, i: (i, 3 * r)),
        pl.BlockSpec((BLOCK, W), lambda r, i: (prev(r, i), 3 * r + 1)),
        pl.BlockSpec((BLOCK, W), lambda r, i: (i, 3 * r + 1)),
        pl.BlockSpec((BLOCK, W), lambda r, i: (prev(r, i), 3 * r + 2)),
        pl.BlockSpec((BLOCK, W), lambda r, i: (i, 3 * r + 2)),
    ]
    o, lse = pl.pallas_call(
        body, name=name, grid=(d, nb), in_specs=in_specs,
        out_specs=[pl.BlockSpec((BLOCK, W), lambda r, i: (i, r))] * 2,
        out_shape=[jax.ShapeDtypeStruct((L, d * W), F32)] * 2,
        compiler_params=_cp("parallel", "parallel"),
    )(qkv_v, qkv_v, qkv_v, qkv_v, qkv_v)
    return o.reshape(S, W), lse.reshape(S, W)


def _attn_combine(outs, lses, *, name, tm=512):
    S, W = outs[0].shape
    n = len(outs)

    def body(*refs):
        o_refs, l_refs = refs[:n], refs[n:2 * n]
        attn_ref, lse_ref = refs[2 * n:]
        ls = [r[...] for r in l_refs]
        top = functools.reduce(jnp.maximum, ls)
        es = [jnp.exp(l - top) for l in ls]
        den = functools.reduce(jnp.add, es)
        num = functools.reduce(jnp.add, [e * r[...] for e, r in zip(es, o_refs)])
        attn_ref[...] = num / den
        lse_ref[...] = top + jnp.log(den)

    spec = pl.BlockSpec((tm, W), lambda i: (i, 0))
    return pl.pallas_call(
        body, name=name, grid=(S // tm,), in_specs=[spec] * (2 * n), out_specs=[spec] * 2,
        out_shape=[jax.ShapeDtypeStruct((S, W), F32)] * 2,
        compiler_params=_cp("parallel"),
    )(*outs, *lses)


def _attn_bwd(qkv, d_attn, attn, lse, d, *, name):
    S = qkv.shape[0]
    L = S // d
    nb = L // BLOCK
    W = ATTN_WIDTH
    qkv_v = qkv.reshape(L, d * 3 * W)
    view = lambda a: a.reshape(L, d * W)

    def body(q_ref, kp_ref, kc_ref, vp_ref, vc_ref, do_ref, o_ref, lse_ref,
             dq_ref, dk_ref, dv_ref, carry_k, carry_v):
        i = pl.program_id(1)

        @pl.when(i == 0)
        def _():
            carry_k[...] = jnp.zeros_like(carry_k)
            carry_v[...] = jnp.zeros_like(carry_v)

        @pl.when(i < nb)
        def _():
            mask = _band_mask(i == 0)
            lms = _lane_masks()
            for hp in range(N_HEADS // 2):
                sl = slice(2 * HEAD_DIM * hp, 2 * HEAD_DIM * (hp + 1))
                q = q_ref[:, sl]
                kcat = jnp.concatenate([kp_ref[:, sl], kc_ref[:, sl]], axis=0)
                vcat = jnp.concatenate([vp_ref[:, sl], vc_ref[:, sl]], axis=0)
                do = do_ref[:, sl]
                prod = do * o_ref[:, sl]
                lse_sl = lse_ref[:, sl]
                dq = jnp.zeros((BLOCK, 2 * HEAD_DIM), F32)
                dkc = jnp.zeros((2 * BLOCK, 2 * HEAD_DIM), F32)
                dvc = jnp.zeros((2 * BLOCK, 2 * HEAD_DIM), F32)
                for lm in lms:
                    qs = jnp.where(lm, q, jnp.zeros_like(q)) * SCALE
                    kh = jnp.where(lm, kcat, jnp.zeros_like(kcat))
                    doh = jnp.where(lm, do, 0.0).astype(BF16)
                    delta = jnp.sum(jnp.where(lm, prod, 0.0), axis=-1, keepdims=True)
                    lse_h = jnp.max(jnp.where(lm, lse_sl, -jnp.inf), axis=-1, keepdims=True)
                    s = _dot(qs, kcat, NT)
                    p = jnp.where(mask, jnp.exp(s - lse_h), 0.0)
                    dp = _dot(doh, vcat, NT)
                    ds = (p * (dp - delta)).astype(BF16)
                    dq = dq + _dot(ds, kh, NN) * SCALE
                    dkc = dkc + _dot(ds, qs, TN)
                    dvc = dvc + _dot(p.astype(BF16), doh, TN)
                dq_ref[:, sl] = dq
                dk_ref[:, sl] = carry_k[:, sl] + dkc[:BLOCK]
                dv_ref[:, sl] = carry_v[:, sl] + dvc[:BLOCK]
                carry_k[:, sl] = dkc[BLOCK:]
                carry_v[:, sl] = dvc[BLOCK:]

        @pl.when(i == nb)
        def _():
            dk_ref[...] = carry_k[...]
            dv_ref[...] = carry_v[...]

    cur = lambda i: jnp.minimum(i, nb - 1)
    prev = lambda i: jnp.maximum(jnp.minimum(i, nb - 1) - 1, 0)
    out_k = lambda r, i: (jnp.maximum(i - 1, 0), r)
    blk = lambda f: pl.BlockSpec((BLOCK, W), f)
    in_specs = [
        blk(lambda r, i: (cur(i), 3 * r)),
        blk(lambda r, i: (prev(i), 3 * r + 1)), blk(lambda r, i: (cur(i), 3 * r + 1)),
        blk(lambda r, i: (prev(i), 3 * r + 2)), blk(lambda r, i: (cur(i), 3 * r + 2)),
        blk(lambda r, i: (cur(i), r)), blk(lambda r, i: (cur(i), r)), blk(lambda r, i: (cur(i), r)),
    ]
    dq, dk, dv = pl.pallas_call(
        body, name=name, grid=(d, nb + 1), in_specs=in_specs,
        out_specs=[blk(lambda r, i: (cur(i), r)), blk(out_k), blk(out_k)],
        out_shape=[jax.ShapeDtypeStruct((L, d * W), F32)] * 3,
        scratch_shapes=[pltpu.VMEM((BLOCK, W), F32), pltpu.VMEM((BLOCK, W), F32)],
        compiler_params=_cp("arbitrary", "arbitrary"),
    )(qkv_v, qkv_v, qkv_v, qkv_v, qkv_v, view(d_attn), view(attn), view(lse))
    return dq.reshape(S, W), dk.reshape(S, W), dv.reshape(S, W)


def _dproj_combine(parts, d_pool_in, *, name, tm=512):
    S, W = d_pool_in.shape
    n = len(parts)

    def body(*refs):
        out_ref = refs[-1]
        for c in range(3):
            tot = refs[c][...]
            for p in range(1, n):
                tot = tot + refs[3 * p + c][...]
            out_ref[:, c * W:(c + 1) * W] = tot.astype(BF16)
        out_ref[:, 3 * W:] = refs[3 * n][...].astype(BF16)

    spec = pl.BlockSpec((tm, W), lambda i: (i, 0))
    flat = [a for part in parts for a in part]
    return pl.pallas_call(
        body, name=name, grid=(S // tm,), in_specs=[spec] * (3 * n + 1),
        out_specs=pl.BlockSpec((tm, 4 * W), lambda i: (i, 0)),
        out_shape=jax.ShapeDtypeStruct((S, 4 * W), BF16),
        compiler_params=_cp("parallel"),
    )(*flat, d_pool_in)


def _split_bf16(a):
    hi = a.astype(BF16)
    lo = (a - hi.astype(F32)).astype(BF16)
    return hi, lo


def _pooled(ug, halo_g, w, row0, tm):
    ext = jnp.concatenate([halo_g, ug], axis=0)
    hi, lo = _split_bf16(ext)
    rr = lax.broadcasted_iota(jnp.int32, (tm, tm + HALO), 0)
    cc = lax.broadcasted_iota(jnp.int32, (tm, tm + HALO), 1)
    back = rr + HALO - cc
    win = ((back >= 0) & (back < w)).astype(BF16)
    wsum = _dot(win, hi, NN) + _dot(win, lo, NN)
    rows = row0 + lax.broadcasted_iota(jnp.int32, (tm, 1), 0)
    inv = 1.0 / jnp.minimum(rows + 1, w).astype(F32)
    return wsum * inv - ug


def _pool_fwd(u, pool_w, pool_scale, *, name, tm=256):
    S, W = u.shape
    G = POOL_GROUP_DIM

    def body(u_ref, h_ref, w_ref, s_ref, o_ref):
        i = pl.program_id(0)
        uv = u_ref[...]
        halo = jnp.where(i > 0, h_ref[...], 0.0)
        for g, w in enumerate(POOL_WINDOWS):
            sl = slice(g * G, (g + 1) * G)
            pooled = _pooled(uv[:, sl], halo[:, sl], w, i * tm, tm)
            z = _dot(pooled.astype(BF16), w_ref[g].astype(BF16), NN)
            o_ref[:, sl] = z * s_ref[:, sl]

    per = tm // HALO
    return pl.pallas_call(
        body, name=name, grid=(S // tm,),
        in_specs=[pl.BlockSpec((tm, W), lambda i: (i, 0)),
                  pl.BlockSpec((HALO, W), lambda i: (jnp.maximum(i * per - 1, 0), 0)),
                  pl.BlockSpec((len(POOL_WINDOWS), G, G), lambda i: (0, 0, 0)),
                  pl.BlockSpec((1, W), lambda i: (0, 0))],
        out_specs=pl.BlockSpec((tm, W), lambda i: (i, 0)),
        out_shape=jax.ShapeDtypeStruct((S, W), F32),
        compiler_params=_cp("parallel"),
    )(u, u, pool_w, pool_scale)


def _pool_bwd(u, dy, pool_w, pool_scale, *, name, tm=256):
    S, W = u.shape
    G = POOL_GROUP_DIM
    nt = S // tm

    def body(u_ref, h_ref, dy_ref, dyn_ref, w_ref, s_ref, du_ref, gw_ref, gs_ref):
        i = pl.program_id(0)

        @pl.when(i == 0)
        def _():
            gw_ref[...] = jnp.zeros_like(gw_ref)
            gs_ref[...] = jnp.zeros_like(gs_ref)

        uv = u_ref[...]
        halo = jnp.where(i > 0, h_ref[...], 0.0)
        dyv = dy_ref[...]
        dyn = jnp.where(i < nt - 1, dyn_ref[...], 0.0)
        rr = lax.broadcasted_iota(jnp.int32, (tm, tm + HALO), 0)
        cc = lax.broadcasted_iota(jnp.int32, (tm, tm + HALO), 1)
        rows_ext = i * tm + lax.broadcasted_iota(jnp.int32, (tm + HALO, 1), 0)
        for g, w in enumerate(POOL_WINDOWS):
            sl = slice(g * G, (g + 1) * G)
            wg = w_ref[g].astype(BF16)
            sc = s_ref[:, sl]
            pooled = _pooled(uv[:, sl], halo[:, sl], w, i * tm, tm)
            z = _dot(pooled.astype(BF16), wg, NN)
            gs_ref[:, sl] += jnp.sum(dyv[:, sl] * z, axis=0, keepdims=True)
            dz = dyv[:, sl] * sc
            gw_ref[g] += _dot(pooled.astype(BF16), dz.astype(BF16), TN)
            dz_ext = jnp.concatenate([dz, dyn[:, sl] * sc], axis=0)
            dp_ext = _dot(dz_ext.astype(BF16), wg, NT)
            inv_ext = 1.0 / jnp.minimum(rows_ext + 1, w).astype(F32)
            hi, lo = _split_bf16(dp_ext * inv_ext)
            ahead = cc - rr
            win = ((ahead >= 0) & (ahead < w)).astype(BF16)
            du_ref[:, sl] = _dot(win, hi, NN) + _dot(win, lo, NN) - dp_ext[:tm]

    per = tm // HALO
    nh = S // HALO
    return pl.pallas_call(
        body, name=name, grid=(nt,),
        in_specs=[pl.BlockSpec((tm, W), lambda i: (i, 0)),
                  pl.BlockSpec((HALO, W), lambda i: (jnp.maximum(i * per - 1, 0), 0)),
                  pl.BlockSpec((tm, W), lambda i: (i, 0)),
                  pl.BlockSpec((HALO, W), lambda i: (jnp.minimum((i + 1) * per, nh - 1), 0)),
                  pl.BlockSpec((len(POOL_WINDOWS), G, G), lambda i: (0, 0, 0)),
                  pl.BlockSpec((1, W), lambda i: (0, 0))],
        out_specs=[pl.BlockSpec((tm, W), lambda i: (i, 0)),
                   pl.BlockSpec((len(POOL_WINDOWS), G, G), lambda i: (0, 0, 0)),
                   pl.BlockSpec((1, W), lambda i: (0, 0))],
        out_shape=[jax.ShapeDtypeStruct((S, W), F32),
                   jax.ShapeDtypeStruct((len(POOL_WINDOWS), G, G), F32),
                   jax.ShapeDtypeStruct((1, W), F32)],
        compiler_params=_cp("arbitrary"),
    )(u, u, dy, dy, pool_w, pool_scale)


GELU_K0 = math.sqrt(2.0 / math.pi)
GELU_K1 = 0.044715


def _gelu_parts(x):
    t = jnp.tanh(GELU_K0 * (x + GELU_K1 * x * x * x))
    gelu = 0.5 * x * (1.0 + t)
    dgelu = 0.5 * (1.0 + t) + 0.5 * x * (1.0 - t * t) * (GELU_K0 * (1.0 + 3.0 * GELU_K1 * x * x))
    return gelu, dgelu


def _shifted(ext):
    return (pltpu.roll(ext, 2, 0)[HALO:], pltpu.roll(ext, 1, 0)[HALO:], ext[HALO:])


def _conv(sh, w, b):
    return b + (sh[0] * w[0:1] + sh[1] * w[1:2] + sh[2] * w[2:3])


def _conv_glu_fwd(u, conv_w, conv_b, *, name, tm=512, tn=256):
    S = u.shape[0]
    F = D_FF
    nj = F // tn

    def body(ug_ref, hg_ref, uv_ref, hv_ref, wg_ref, wv_ref, bg_ref, bv_ref, y_ref):
        i = pl.program_id(0)

        def conv(u_ref, h_ref, w_ref, b_ref):
            halo = jnp.where(i > 0, h_ref[...].astype(F32), 0.0)
            ext = jnp.concatenate([halo, u_ref[...].astype(F32)], axis=0)
            return _conv(_shifted(ext), w_ref[...], b_ref[...])

        gelu, _ = _gelu_parts(conv(ug_ref, hg_ref, wg_ref, bg_ref))
        y_ref[...] = (gelu * conv(uv_ref, hv_ref, wv_ref, bv_ref)).astype(BF16)

    per = tm // HALO
    hrow = lambda i: jnp.maximum(i * per - 1, 0)
    return pl.pallas_call(
        body, name=name, grid=(S // tm, nj),
        in_specs=[pl.BlockSpec((tm, tn), lambda i, j: (i, j)),
                  pl.BlockSpec((HALO, tn), lambda i, j: (hrow(i), j)),
                  pl.BlockSpec((tm, tn), lambda i, j: (i, j + nj)),
                  pl.BlockSpec((HALO, tn), lambda i, j: (hrow(i), j + nj)),
                  pl.BlockSpec((3, tn), lambda i, j: (0, j)), pl.BlockSpec((3, tn), lambda i, j: (0, j + nj)),
                  pl.BlockSpec((1, tn), lambda i, j: (0, j)), pl.BlockSpec((1, tn), lambda i, j: (0, j + nj))],
        out_specs=pl.BlockSpec((tm, tn), lambda i, j: (i, j)),
        out_shape=jax.ShapeDtypeStruct((S, F), BF16),
        compiler_params=_cp("parallel", "parallel"),
    )(u, u, u, u, conv_w, conv_w, conv_b, conv_b)


def _conv_glu_bwd(u, dy, conv_w, conv_b, *, name, tm=512, tn=256):
    S = u.shape[0]
    F = D_FF
    nj = F // tn
    nt = S // tm
    n_ext = tm + HALO

    def body(ug_ref, hgp_ref, hgn_ref, uv_ref, hvp_ref, hvn_ref, dy_ref, dyn_ref,
             wg_ref, wv_ref, bg_ref, bv_ref, dug_ref, duv_ref, gwg_ref, gwv_ref, gbg_ref, gbv_ref):
        i = pl.program_id(1)

        @pl.when(i == 0)
        def _():
            for r in (gwg_ref, gwv_ref, gbg_ref, gbv_ref):
                r[...] = jnp.zeros_like(r)

        def shifted(u_ref, hp_ref, hn_ref):
            halo = jnp.where(i > 0, hp_ref[...].astype(F32), 0.0)
            ext = jnp.concatenate([halo, u_ref[...].astype(F32), hn_ref[...].astype(F32)], axis=0)
            return _shifted(ext)

        sh_g = shifted(ug_ref, hgp_ref, hgn_ref)
        sh_v = shifted(uv_ref, hvp_ref, hvn_ref)
        wg, wv = wg_ref[...], wv_ref[...]
        cg = _conv(sh_g, wg, bg_ref[...])
        cv = _conv(sh_v, wv, bv_ref[...])
        dyn = jnp.where(i < nt - 1, dyn_ref[...].astype(F32), 0.0)
        dy_ext = jnp.concatenate([dy_ref[...].astype(F32), dyn], axis=0)
        gelu, dgelu = _gelu_parts(cg)
        dcg = dy_ext * cv * dgelu
        dcv = dy_ext * gelu

        def back(dc, w):
            return (dc[:tm] * w[2:3] + pltpu.roll(dc, n_ext - 1, 0)[:tm] * w[1:2]
                    + pltpu.roll(dc, n_ext - 2, 0)[:tm] * w[0:1])

        dug_ref[...] = back(dcg, wg).astype(BF16)
        duv_ref[...] = back(dcv, wv).astype(BF16)
        for dc, sh, gw_ref, gb_ref in ((dcg, sh_g, gwg_ref, gbg_ref), (dcv, sh_v, gwv_ref, gbv_ref)):
            dct = dc[:tm]
            gb_ref[...] += jnp.sum(dct, axis=0, keepdims=True)
            for t in range(3):
                gw_ref[t:t + 1, :] += jnp.sum(dct * sh[t][:tm], axis=0, keepdims=True)

    per = tm // HALO
    nh = S // HALO
    hprev = lambda i: jnp.maximum(i * per - 1, 0)
    hnext = lambda i: jnp.minimum((i + 1) * per, nh - 1)
    tile = lambda off: pl.BlockSpec((tm, tn), lambda j, i: (i, j + off))
    hp = lambda off: pl.BlockSpec((HALO, tn), lambda j, i: (hprev(i), j + off))
    hn = lambda off: pl.BlockSpec((HALO, tn), lambda j, i: (hnext(i), j + off))
    vec = lambda rows, off: pl.BlockSpec((rows, tn), lambda j, i: (0, j + off))
    return pl.pallas_call(
        body, name=name, grid=(nj, nt),
        in_specs=[tile(0), hp(0), hn(0), tile(nj), hp(nj), hn(nj), tile(0), hn(0),
                  vec(3, 0), vec(3, nj), vec(1, 0), vec(1, nj)],
        out_specs=[tile(0), tile(0), vec(3, 0), vec(3, 0), vec(1, 0), vec(1, 0)],
        out_shape=[jax.ShapeDtypeStruct((S, F), BF16), jax.ShapeDtypeStruct((S, F), BF16),
                   jax.ShapeDtypeStruct((3, F), F32), jax.ShapeDtypeStruct((3, F), F32),
                   jax.ShapeDtypeStruct((1, F), F32), jax.ShapeDtypeStruct((1, F), F32)],
        compiler_params=_cp("parallel", "arbitrary"),
    )(u, u, u, u, u, u, dy, dy, conv_w, conv_w, conv_b, conv_b)


def _sum_partials(parts, *, name, tr):
    _, R, C = parts.shape

    def body(p_ref, o_ref):
        tot = p_ref[0].astype(F32)
        for j in range(1, N_DEV):
            tot = tot + p_ref[j].astype(F32)
        o_ref[...] = tot

    return pl.pallas_call(
        body, name=name, grid=(R // tr,),
        in_specs=[pl.BlockSpec((N_DEV, tr, C), lambda i: (0, i, 0))],
        out_specs=pl.BlockSpec((tr, C), lambda i: (i, 0)),
        out_shape=jax.ShapeDtypeStruct((R, C), F32),
        compiler_params=_cp("parallel"),
    )(parts)


def _adamw(w, g, m, v, *, name, tr):
    R, C = w.shape
    c1 = 1.0 - ADAM_B1 ** ADAM_STEP
    c2 = 1.0 - ADAM_B2 ** ADAM_STEP

    def body(w_ref, g_ref, m_ref, v_ref, d_ref, nm_ref, nv_ref):
        g = g_ref[...]
        nm = ADAM_B1 * m_ref[...] + (1.0 - ADAM_B1) * g
        nv = ADAM_B2 * v_ref[...] + (1.0 - ADAM_B2) * (g * g)
        d_ref[...] = -ADAM_LR * ((nm / c1) / (jnp.sqrt(nv / c2) + ADAM_EPS) + ADAM_WD * w_ref[...])
        nm_ref[...] = nm
        nv_ref[...] = nv

    spec = pl.BlockSpec((tr, C), lambda i: (i, 0))
    return pl.pallas_call(
        body, name=name, grid=(R // tr,), in_specs=[spec] * 4, out_specs=[spec] * 3,
        out_shape=[jax.ShapeDtypeStruct((R, C), F32)] * 3,
        compiler_params=_cp("parallel"),
    )(w, g, m, v)


def _mesh_pos():
    return lax.axis_index("x"), lax.axis_index("y"), lax.axis_index("c")


def _two_level_gather(x_ref, out_ref, send_sems, recv_sems, local_sem):
    x, y, c = _mesh_pos()
    me, sibling = (x, y, c), (x, y, 1 - c)
    chips = [(1 - x, y), (x, 1 - y), (1 - x, 1 - y)]

    def slot(px, py, pc):
        return out_ref.at[4 * px + 2 * py + pc]

    def copy(k, block, to, src=None):
        return pltpu.make_async_remote_copy(
            src_ref=slot(*block) if src is None else src, dst_ref=slot(*block),
            send_sem=send_sems.at[k], recv_sem=recv_sems.at[k], device_id=to, device_id_type=MESH)

    mine = pltpu.make_async_copy(x_ref, slot(*me), local_sem)
    mine.start()
    first = [copy(0, me, sibling, src=x_ref)]
    first += [copy(1 + j, me, (*chip, c), src=x_ref) for j, chip in enumerate(chips)]
    for cp in first:
        cp.start()
    passed = [copy(4 + j, (*chip, c), sibling) for j, chip in enumerate(chips)]
    for j, chip in enumerate(chips):
        copy(1 + j, (*chip, c), me).wait_recv()
        passed[j].start()
    copy(0, sibling, me).wait_recv()
    for j, chip in enumerate(chips):
        copy(4 + j, (*chip, 1 - c), me).wait_recv()
    for cp in first + passed:
        cp.wait_send()
    mine.wait()


_GATHER_SEMS = [pltpu.SemaphoreType.DMA((7,)), pltpu.SemaphoreType.DMA((7,)), pltpu.SemaphoreType.DMA]


def _all_gather_hbm(block, *, name):
    def body(x_ref, out_ref, send_sems, recv_sems, local_sem):
        _two_level_gather(x_ref, out_ref, send_sems, recv_sems, local_sem)

    return pl.pallas_call(
        body, name=name, in_specs=[ANY], out_specs=ANY,
        out_shape=jax.ShapeDtypeStruct((N_DEV,) + block.shape, block.dtype),
        scratch_shapes=_GATHER_SEMS,
    )(block)


def _all_reduce_small(block, *, name):
    def body(x_ref, all_ref, sum_ref, send_sems, recv_sems, local_sem):
        _two_level_gather(x_ref, all_ref, send_sems, recv_sems, local_sem)
        tot = all_ref[0]
        for j in range(1, N_DEV):
            tot = tot + all_ref[j]
        sum_ref[...] = tot

    return pl.pallas_call(
        body, name=name, in_specs=[VMEM], out_specs=[VMEM, VMEM],
        out_shape=[jax.ShapeDtypeStruct((N_DEV,) + block.shape, block.dtype),
                   jax.ShapeDtypeStruct(block.shape, block.dtype)],
        scratch_shapes=_GATHER_SEMS,
        compiler_params=pltpu.CompilerParams(vmem_limit_bytes=V7X_VMEM_LIMIT),
    )(block)[1]


def _exchange_partials(grads, *, name):
    n = len(grads)

    def body(*refs):
        g_refs, r_refs = refs[:n], refs[n:2 * n]
        send_sems, recv_sems, local_sems = refs[2 * n:]
        x, y, c = _mesh_pos()
        me = 4 * x + 2 * y + c
        sends = []
        for k in range(n):
            rows = g_refs[k].shape[0] // N_DEV
            own = pltpu.make_async_copy(g_refs[k].at[pl.ds(me * rows, rows)], r_refs[k].at[me], local_sems.at[k])
            own.start()
            sends.append(own)
        remote = []
        for p in range(1, N_DEV):
            px, py, pc = x ^ (p >> 2), y ^ ((p >> 1) & 1), c ^ (p & 1)
            peer = 4 * px + 2 * py + pc
            for k in range(n):
                rows = g_refs[k].shape[0] // N_DEV
                cp = pltpu.make_async_remote_copy(
                    src_ref=g_refs[k].at[pl.ds(peer * rows, rows)], dst_ref=r_refs[k].at[me],
                    send_sem=send_sems.at[k, p], recv_sem=recv_sems.at[k, p],
                    device_id=(px, py, pc), device_id_type=MESH)
                cp.start()
                arrival = pltpu.make_async_remote_copy(
                    src_ref=g_refs[k].at[pl.ds(peer * rows, rows)], dst_ref=r_refs[k].at[peer],
                    send_sem=send_sems.at[k, p], recv_sem=recv_sems.at[k, p],
                    device_id=(px, py, pc), device_id_type=MESH)
                remote.append((cp, arrival))
        for cp, arrival in remote:
            arrival.wait_recv()
        for cp, arrival in remote:
            cp.wait_send()
        for own in sends:
            own.wait()

    return pl.pallas_call(
        body, name=name, in_specs=[ANY] * n, out_specs=[ANY] * n,
        out_shape=[jax.ShapeDtypeStruct((N_DEV, g.shape[0] // N_DEV, g.shape[1]), g.dtype) for g in grads],
        scratch_shapes=[pltpu.SemaphoreType.DMA((n, N_DEV)), pltpu.SemaphoreType.DMA((n, N_DEV)),
                        pltpu.SemaphoreType.DMA((n,))],
    )(*grads)


def _local_step(x, target, g_mix_pre, w_in_t, pool_w, pool_scale, w_out, g_mix_post, g_ffn_pre,
                w_up_t, conv_w, conv_b, w_down, g_ffn_post):
    S = x.shape[0]
    aw = ATTN_WIDTH
    h1 = _rms_norm(x, g_mix_pre, name="rms_mix_pre")
    qkv = _matmul(h1, w_in_t[:3 * aw], trans_b=True, out_dtype=BF16, tm=1024, tn=512, name="proj_qkv")
    pool_in = _matmul(h1, w_in_t[3 * aw:], trans_b=True, out_dtype=F32, tm=1024, tn=512, name="proj_pool")
    outs, lses = [], []
    for d in DILATIONS:
        o, l = _attn_fwd(qkv, d, name=f"attn_fwd_d{d}")
        outs.append(o)
        lses.append(l)
    attn, lse = _attn_combine(outs, lses, name="attn_combine")
    pool = _pool_fwd(pool_in, pool_w, pool_scale, name="pool_fwd")
    cat = jnp.concatenate([attn.astype(BF16), pool.astype(BF16)], axis=1)
    mixed, x2, h2 = _mix_out(cat, w_out, x, g_mix_post, g_ffn_pre, name="mix_out")
    u = _matmul(h2, w_up_t, trans_b=True, out_dtype=BF16, tm=1024, tn=512, name="ffn_up")
    y = _conv_glu_fwd(u, conv_w, conv_b, name="conv_glu_fwd")
    df, d_out, loss_blk, gg_ffn_post = _ffn_out(y, w_down, x2, target, g_ffn_post, name="ffn_out")
    gw_down = _matmul_tn(y, df, ta=1408, ts=512, name="grad_w_down")
    dyy = _matmul(df, w_down, trans_b=True, out_dtype=BF16, tm=512, tn=D_FF, name="ffn_down_dgrad")
    du_g, du_v, gcw_g, gcw_v, gcb_g, gcb_v = _conv_glu_bwd(u, dyy, conv_w, conv_b, name="conv_glu_bwd")
    gw_up_t = jnp.concatenate([_matmul_tn(du_g, h2, ta=1408, ts=512, name="grad_w_up_gate"),
                               _matmul_tn(du_v, h2, ta=1408, ts=512, name="grad_w_up_val")], axis=0)
    dx2, gg_ffn_pre, dmixed, gg_mix_post = _dgrad_norm(
        [du_g, du_v], w_up_t, d_out, x2, g_ffn_pre, (mixed, g_mix_post), tk=1408, name="ffn_up_dgrad")
    gw_out = _matmul_tn(cat, dmixed, ta=512, ts=1024, name="grad_w_out")
    dcat = _matmul(dmixed, w_out, trans_b=True, out_dtype=F32, tm=512, tn=1024, name="mix_out_dgrad")
    d_attn, d_pool = dcat[:, :aw], dcat[:, aw:]
    d_pool_in, g_pool_w, g_pool_scale = _pool_bwd(pool_in, d_pool, pool_w, pool_scale, name="pool_bwd")
    parts = [_attn_bwd(qkv, d_attn, attn, lse, d, name=f"attn_bwd_d{d}") for d in DILATIONS]
    dproj = _dproj_combine(parts, d_pool_in, name="dproj_combine")
    gw_in_t = _matmul_tn(dproj, h1, ta=512, ts=1024, name="grad_w_in")
    grad_x, gg_mix_pre = _dgrad_norm([dproj], w_in_t, dx2, x, g_mix_pre, None, tk=512, name="proj_dgrad")
    g_conv_w = jnp.concatenate([gcw_g, gcw_v], axis=1)
    g_conv_b = jnp.concatenate([gcb_g, gcb_v], axis=1)
    big = (gw_in_t, gw_out, gw_up_t, gw_down)
    small = dict(g_mix_pre=gg_mix_pre, g_mix_post=gg_mix_post, g_ffn_pre=gg_ffn_pre, g_ffn_post=gg_ffn_post,
                 pool_scale=g_pool_scale, conv_b=g_conv_b, pool_w=g_pool_w, conv_w=g_conv_w)
    return loss_blk, grad_x, big, small


_SMALL = ("g_mix_pre", "g_mix_post", "g_ffn_pre", "g_ffn_post", "pool_scale", "conv_b", "pool_w")
LANES = 128


def _pack_rows(arrays):
    parts = []
    for a in arrays:
        a2 = a.reshape(-1, LANES)
        parts.append(jnp.pad(a2, ((0, (-a2.shape[0]) % 8), (0, 0))))
    return jnp.concatenate(parts, axis=0)


def _unpack_rows(packed, shapes):
    out, row = [], 0
    for shape in shapes:
        rows = math.prod(shape) // LANES
        out.append(packed[row:row + rows].reshape(shape))
        row += -(-rows // 8) * 8
    return out


def kernel(x, g_mix_pre, w_in, pool_w, pool_scale, w_out, g_mix_post, g_ffn_pre, w_up, conv_w, conv_b, w_down, g_ffn_post, loss_target, m_g_mix_pre, m_w_in, m_pool_w, m_pool_scale, m_w_out, m_g_mix_post, m_g_ffn_pre, m_w_up, m_conv_w, m_conv_b, m_w_down, m_g_ffn_post, v_g_mix_pre, v_w_in, v_pool_w, v_pool_scale, v_w_out, v_g_mix_post, v_g_ffn_pre, v_w_up, v_conv_w, v_conv_b, v_w_down, v_g_ffn_post):
    me = 4 * lax.axis_index("x") + 2 * lax.axis_index("y") + lax.axis_index("c")
    n_in, n_out, n_up, n_down = 4 * ATTN_WIDTH, D_MODEL, 2 * D_FF, D_FF
    r_in, r_out, r_up, r_down = (n // N_DEV for n in (n_in, n_out, n_up, n_down))

    cw_shard = conv_w[0]
    cw_bits = lax.bitcast_convert_type(cw_shard.reshape(-1), BF16).reshape(-1)
    cw_rows = 16
    cw_bits = jnp.pad(cw_bits, (0, cw_rows * D_MODEL - cw_bits.shape[0])).reshape(cw_rows, D_MODEL)
    payload = jnp.concatenate([w_in[0].T.astype(BF16), w_out[0].astype(BF16), w_up[0].T.astype(BF16),
                               w_down[0].astype(BF16), cw_bits], axis=0)
    gathered = _all_gather_hbm(payload, name="gather_weights")
    offs = [0, r_in, r_in + r_out, r_in + r_out + r_up, r_in + r_out + r_up + r_down]
    w_in_t = gathered[:, offs[0]:offs[1]].reshape(n_in, D_MODEL)
    w_out_f = gathered[:, offs[1]:offs[2]].reshape(n_out, D_MODEL)
    w_up_t = gathered[:, offs[2]:offs[3]].reshape(n_up, D_MODEL)
    w_down_f = gathered[:, offs[3]:offs[4]].reshape(n_down, D_MODEL)
    n_cw = 3 * (2 * D_FF // N_DEV)
    cw_all = gathered[:, offs[4]:].reshape(N_DEV, -1)[:, :2 * n_cw].reshape(N_DEV, n_cw, 2)
    cw_all = lax.bitcast_convert_type(cw_all, F32).reshape(N_DEV, 3, -1)
    conv_w_f = jnp.transpose(cw_all, (1, 0, 2)).reshape(3, 2 * D_FF)

    loss_blk, grad_x, big, small = _local_step(
        x[0], loss_target[0], g_mix_pre, w_in_t, pool_w[0], pool_scale, w_out_f, g_mix_post, g_ffn_pre,
        w_up_t, conv_w_f, conv_b, w_down_f, g_ffn_post)
    loss = lax.psum(loss_blk[0, 0], ("x", "y", "c"))

    recv = _exchange_partials(list(big), name="exchange_grads")
    g_in_t, g_out, g_up_t, g_down = (
        _sum_partials(r, name=f"sum_partials_{k}", tr=r.shape[1] // 2) for k, r in enumerate(recv))
    grads = {"w_in": g_in_t.T, "w_out": g_out, "w_up": g_up_t.T, "w_down": g_down}

    given = dict(g_mix_pre=g_mix_pre, g_mix_post=g_mix_post, g_ffn_pre=g_ffn_pre, g_ffn_post=g_ffn_post,
                 pool_scale=pool_scale, conv_b=conv_b, pool_w=pool_w)
    small_shapes = [given[k].shape for k in _SMALL]
    total = _all_reduce_small(_pack_rows([small[k] for k in _SMALL] + [small["conv_w"]]), name="all_reduce_small")
    *small_grads, g_conv_w_all = _unpack_rows(total, small_shapes + [(3, 2 * D_FF)])
    grads.update(zip(_SMALL, small_grads))
    width = 2 * D_FF // N_DEV
    grads["conv_w"] = lax.dynamic_slice_in_dim(g_conv_w_all, me * width, width, axis=1)[None]

    weights = dict(g_mix_pre=g_mix_pre, w_in=w_in, pool_w=pool_w, pool_scale=pool_scale, w_out=w_out,
                   g_mix_post=g_mix_post, g_ffn_pre=g_ffn_pre, w_up=w_up, conv_w=conv_w, conv_b=conv_b,
                   w_down=w_down, g_ffn_post=g_ffn_post)
    m_in = dict(g_mix_pre=m_g_mix_pre, w_in=m_w_in, pool_w=m_pool_w, pool_scale=m_pool_scale, w_out=m_w_out,
                g_mix_post=m_g_mix_post, g_ffn_pre=m_g_ffn_pre, w_up=m_w_up, conv_w=m_conv_w, conv_b=m_conv_b,
                w_down=m_w_down, g_ffn_post=m_g_ffn_post)
    v_in = dict(g_mix_pre=v_g_mix_pre, w_in=v_w_in, pool_w=v_pool_w, pool_scale=v_pool_scale, w_out=v_w_out,
                g_mix_post=v_g_mix_post, g_ffn_pre=v_g_ffn_pre, w_up=v_w_up, conv_w=v_conv_w, conv_b=v_conv_b,
                w_down=v_w_down, g_ffn_post=v_g_ffn_post)
    delta, new_m, new_v = {}, {}, {}
    for k in ("w_in", "w_out", "w_up", "w_down"):
        g = grads[k]
        d, nm, nv = _adamw(weights[k][0], g, m_in[k][0], v_in[k][0], name=f"adamw_{k}", tr=g.shape[0] // 2)
        grads[k], delta[k], new_m[k], new_v[k] = g[None], d[None], nm[None], nv[None]
    d, nm, nv = _adamw(weights["conv_w"][0], grads["conv_w"][0], m_in["conv_w"][0], v_in["conv_w"][0],
                       name="adamw_conv_w", tr=3)
    delta["conv_w"], new_m["conv_w"], new_v["conv_w"] = d[None], nm[None], nv[None]
    packed_w = _pack_rows([weights[k] for k in _SMALL])
    small_rows = packed_w.shape[0]
    d, nm, nv = _adamw(packed_w, total[:small_rows], _pack_rows([m_in[k] for k in _SMALL]),
                       _pack_rows([v_in[k] for k in _SMALL]), name="adamw_small", tr=small_rows)
    for k, dk, mk, vk in zip(_SMALL, _unpack_rows(d, small_shapes), _unpack_rows(nm, small_shapes),
                             _unpack_rows(nv, small_shapes)):
        delta[k], new_m[k], new_v[k] = dk, mk, vk

    order = ("g_mix_pre", "w_in", "pool_w", "pool_scale", "w_out", "g_mix_post", "g_ffn_pre", "w_up",
             "conv_w", "conv_b", "w_down", "g_ffn_post")
    return (loss, grad_x[None], *[grads[k] for k in order], *[delta[k] for k in order],
            *[new_m[k] for k in order], *[new_v[k] for k in order])
```

```python
import functools
import math

import jax
import jax.numpy as jnp
from jax import lax
from jax.experimental import pallas as pl
from jax.experimental.pallas import tpu as pltpu

F32 = jnp.float32
BF16 = jnp.bfloat16

D_MODEL = 1024
ATTN_WIDTH = 512
N_HEADS = 8
HEAD_DIM = 64
DILATIONS = (1, 4, 16)
BLOCK = 128
POOL_WIDTH = 512
POOL_WINDOWS = (2, 4, 8, 16)
POOL_GROUP_DIM = 128
D_FF = 2816
EPS = 1e-6
NEG_INF = -1e30
SCALE = HEAD_DIM ** -0.5

ADAM_LR = 0.001
ADAM_B1 = 0.9
ADAM_B2 = 0.999
ADAM_EPS = 1e-08
ADAM_WD = 0.01
ADAM_STEP = 10

N_DEV = 8
HALO = 16
V7X_VMEM_LIMIT = 56 * 1024 * 1024

MESH = pl.DeviceIdType.MESH
ANY = pl.BlockSpec(memory_space=pl.ANY)
VMEM = pl.BlockSpec(memory_space=pltpu.VMEM)

NT = (((1,), (1,)), ((), ()))
NN = (((1,), (0,)), ((), ()))
TN = (((0,), (0,)), ((), ()))


def _cp(*sem):
    return pltpu.CompilerParams(dimension_semantics=sem, vmem_limit_bytes=V7X_VMEM_LIMIT)


def _dot(a, b, dn):
    return lax.dot_general(a, b, dn, preferred_element_type=F32)


def _rms_bwd(xin, g, dy):
    r = lax.rsqrt(jnp.mean(xin * xin, axis=-1, keepdims=True) + EPS)
    xh = xin * r
    gdy = g * dy
    dx = r * (gdy - xh * jnp.mean(gdy * xh, axis=-1, keepdims=True))
    dg = jnp.sum(dy * xh, axis=0, keepdims=True)
    return dx, dg


def _rms_norm(x, g, *, name, tm=512):
    S, D = x.shape

    def body(x_ref, g_ref, o_ref):
        xv = x_ref[...]
        r = lax.rsqrt(jnp.mean(xv * xv, axis=-1, keepdims=True) + EPS)
        o_ref[...] = (xv * r * g_ref[...]).astype(BF16)

    return pl.pallas_call(
        body, name=name, grid=(S // tm,),
        in_specs=[pl.BlockSpec((tm, D), lambda i: (i, 0)), pl.BlockSpec((1, D), lambda i: (0, 0))],
        out_specs=pl.BlockSpec((tm, D), lambda i: (i, 0)),
        out_shape=jax.ShapeDtypeStruct((S, D), BF16),
        compiler_params=_cp("parallel"),
    )(x, g)


def _matmul(a, b, *, trans_b, out_dtype, tm, tn, name):
    M, K = a.shape
    N = b.shape[0] if trans_b else b.shape[1]
    dn = NT if trans_b else NN

    def body(a_ref, b_ref, o_ref):
        o_ref[...] = _dot(a_ref[...], b_ref[...], dn).astype(out_dtype)

    b_spec = (pl.BlockSpec((tn, K), lambda i, j: (j, 0)) if trans_b
              else pl.BlockSpec((K, tn), lambda i, j: (0, j)))
    return pl.pallas_call(
        body, name=name, grid=(M // tm, N // tn),
        in_specs=[pl.BlockSpec((tm, K), lambda i, j: (i, 0)), b_spec],
        out_specs=pl.BlockSpec((tm, tn), lambda i, j: (i, j)),
        out_shape=jax.ShapeDtypeStruct((M, N), out_dtype),
        compiler_params=_cp("parallel", "parallel"),
    )(a, b)


def _matmul_tn(a, b, *, ta, ts, name):
    S, Ka = a.shape
    Nb = b.shape[1]
    ns = S // ts

    def body(a_ref, b_ref, o_ref, acc):
        s = pl.program_id(1)

        @pl.when(s == 0)
        def _():
            acc[...] = jnp.zeros_like(acc)

        acc[...] += _dot(a_ref[...], b_ref[...], TN)

        @pl.when(s == ns - 1)
        def _():
            o_ref[...] = acc[...].astype(BF16)

    return pl.pallas_call(
        body, name=name, grid=(Ka // ta, ns),
        in_specs=[pl.BlockSpec((ts, ta), lambda i, s: (s, i)), pl.BlockSpec((ts, Nb), lambda i, s: (s, 0))],
        out_specs=pl.BlockSpec((ta, Nb), lambda i, s: (i, 0)),
        out_shape=jax.ShapeDtypeStruct((Ka, Nb), BF16),
        scratch_shapes=[pltpu.VMEM((ta, Nb), F32)],
        compiler_params=_cp("parallel", "arbitrary"),
    )(a, b)


def _mix_out(attn, pool, w_out, x, g_post, g_next, *, name, tm=256):
    S, K = attn.shape
    D = w_out.shape[1]

    def body(a_ref, p_ref, w_ref, x_ref, gp_ref, gn_ref, mixed_ref, x2_ref, h2_ref):
        mixed = _dot(a_ref[...], w_ref[:K, :], NN) + _dot(p_ref[...], w_ref[K:, :], NN)
        r = lax.rsqrt(jnp.mean(mixed * mixed, axis=-1, keepdims=True) + EPS)
        x2 = x_ref[...] + mixed * r * gp_ref[...]
        r2 = lax.rsqrt(jnp.mean(x2 * x2, axis=-1, keepdims=True) + EPS)
        mixed_ref[...] = mixed
        x2_ref[...] = x2
        h2_ref[...] = (x2 * r2 * gn_ref[...]).astype(BF16)

    row = lambda i: (i, 0)
    fix = lambda i: (0, 0)
    return pl.pallas_call(
        body, name=name, grid=(S // tm,),
        in_specs=[pl.BlockSpec((tm, K), row), pl.BlockSpec((tm, K), row), pl.BlockSpec((2 * K, D), fix),
                  pl.BlockSpec((tm, D), row), pl.BlockSpec((1, D), fix), pl.BlockSpec((1, D), fix)],
        out_specs=[pl.BlockSpec((tm, D), row)] * 3,
        out_shape=[jax.ShapeDtypeStruct((S, D), F32), jax.ShapeDtypeStruct((S, D), F32),
                   jax.ShapeDtypeStruct((S, D), BF16)],
        compiler_params=_cp("parallel"),
    )(attn, pool, w_out, x, g_post, g_next)


def _ffn_out(y, w_down, x2, target, g_post, *, name, tm=256):
    S, K = y.shape
    D = w_down.shape[1]

    def body(y_ref, w_ref, x2_ref, t_ref, g_ref, df_ref, dout_ref, loss_ref, gg_ref):
        i = pl.program_id(0)

        @pl.when(i == 0)
        def _():
            loss_ref[...] = jnp.zeros_like(loss_ref)
            gg_ref[...] = jnp.zeros_like(gg_ref)

        f = _dot(y_ref[...], w_ref[...], NN)
        g = g_ref[...]
        r = lax.rsqrt(jnp.mean(f * f, axis=-1, keepdims=True) + EPS)
        out = x2_ref[...] + f * r * g
        err = out - t_ref[...]
        dy = err * (1.0 / D)
        df, dg = _rms_bwd(f, g, dy)
        df_ref[...] = df.astype(BF16)
        dout_ref[...] = dy
        gg_ref[...] += dg
        loss_ref[...] += 0.5 * jnp.sum(jnp.mean(err * err, axis=-1, keepdims=True))

    row = lambda i: (i, 0)
    fix = lambda i: (0, 0)
    return pl.pallas_call(
        body, name=name, grid=(S // tm,),
        in_specs=[pl.BlockSpec((tm, K), row), pl.BlockSpec((K, D), fix), pl.BlockSpec((tm, D), row),
                  pl.BlockSpec((tm, D), row), pl.BlockSpec((1, D), fix)],
        out_specs=[pl.BlockSpec((tm, D), row), pl.BlockSpec((tm, D), row),
                   pl.BlockSpec((8, 128), fix), pl.BlockSpec((1, D), fix)],
        out_shape=[jax.ShapeDtypeStruct((S, D), BF16), jax.ShapeDtypeStruct((S, D), F32),
                   jax.ShapeDtypeStruct((8, 128), F32), jax.ShapeDtypeStruct((1, D), F32)],
        compiler_params=_cp("arbitrary"),
    )(y, w_down, x2, target, g_post)


def _dgrad_norm(a_list, w, resid, xin, g, second, *, tk, name, tm=512):
    S, Kp = a_list[0].shape
    na = len(a_list)
    D = w.shape[1]
    kper = Kp // tk
    nk = na * kper
    two = second is not None

    def body(*refs):
        a_refs = refs[:na]
        w_ref, r_ref, x_ref, g_ref = refs[na:na + 4]
        pos = na + 4
        if two:
            x2_ref, g2_ref = refs[pos:pos + 2]
            pos += 2
        dx_ref, gg_ref = refs[pos:pos + 2]
        pos += 2
        if two:
            d2_ref, gg2_ref = refs[pos:pos + 2]
            pos += 2
        acc = refs[pos]
        i = pl.program_id(0)
        k = pl.program_id(1)

        @pl.when(k == 0)
        def _():
            acc[...] = jnp.zeros_like(acc)

        @pl.when((i == 0) & (k == 0))
        def _():
            gg_ref[...] = jnp.zeros_like(gg_ref)
            if two:
                gg2_ref[...] = jnp.zeros_like(gg2_ref)

        for q in range(na):
            @pl.when(k // kper == q)
            def _(q=q):
                acc[...] += _dot(a_refs[q][...], w_ref[...], NN)

        @pl.when(k == nk - 1)
        def _():
            d1, dg1 = _rms_bwd(x_ref[...], g_ref[...], acc[...])
            dx = r_ref[...] + d1
            dx_ref[...] = dx
            gg_ref[...] += dg1
            if two:
                d2, dg2 = _rms_bwd(x2_ref[...], g2_ref[...], dx)
                d2_ref[...] = d2.astype(BF16)
                gg2_ref[...] += dg2

    row = lambda i, k: (i, 0)
    fix = lambda i, k: (0, 0)
    a_specs = [pl.BlockSpec((tm, tk), functools.partial(
        lambda i, k, q: (i, jnp.clip(k - q * kper, 0, kper - 1)), q=q)) for q in range(na)]
    in_specs = a_specs + [pl.BlockSpec((tk, D), lambda i, k: (k, 0)), pl.BlockSpec((tm, D), row),
                          pl.BlockSpec((tm, D), row), pl.BlockSpec((1, D), fix)]
    args = list(a_list) + [w, resid, xin, g]
    out_specs = [pl.BlockSpec((tm, D), row), pl.BlockSpec((1, D), fix)]
    out_shape = [jax.ShapeDtypeStruct((S, D), F32), jax.ShapeDtypeStruct((1, D), F32)]
    if two:
        in_specs += [pl.BlockSpec((tm, D), row), pl.BlockSpec((1, D), fix)]
        args += list(second)
        out_specs += [pl.BlockSpec((tm, D), row), pl.BlockSpec((1, D), fix)]
        out_shape += [jax.ShapeDtypeStruct((S, D), BF16), jax.ShapeDtypeStruct((1, D), F32)]
    return pl.pallas_call(
        body, name=name, grid=(S // tm, nk), in_specs=in_specs, out_specs=out_specs, out_shape=out_shape,
        scratch_shapes=[pltpu.VMEM((tm, D), F32)],
        compiler_params=_cp("arbitrary", "arbitrary"),
    )(*args)


def _band_mask(first_block):
    qi = lax.broadcasted_iota(jnp.int32, (BLOCK, 2 * BLOCK), 0)
    ki = lax.broadcasted_iota(jnp.int32, (BLOCK, 2 * BLOCK), 1)
    first_key = jnp.where(first_block, BLOCK, 0)
    return (ki >= qi) & (ki <= qi + BLOCK) & (ki >= first_key)


def _lane_masks():
    lane = lax.broadcasted_iota(jnp.int32, (1, 2 * HEAD_DIM), 1)
    return (lane < HEAD_DIM, lane >= HEAD_DIM)


CHUNK = BLOCK * max(DILATIONS)
SLAB = 2 * HEAD_DIM
N_SLABS = ATTN_WIDTH // SLAB


def _unit_rows(d, b):
    def rows(r):
        start = r + BLOCK * d * b
        return pl.ds(start, BLOCK, stride=d) if d > 1 else pl.ds(start, BLOCK)
    return rows


def _attn_units():
    for p, d in enumerate(DILATIONS):
        nbc = CHUNK // (BLOCK * d)
        for b in range(nbc):
            for r in range(d):
                yield p, d, b, r, nbc


def _attn_in_specs(nc, n_cur):
    prev = lambda c: jnp.maximum(jnp.minimum(c, nc - 1) - 1, 0)
    cur = lambda c: jnp.minimum(c, nc - 1)
    blk = lambda f: pl.BlockSpec((CHUNK, SLAB), f)
    specs = [blk(lambda h, c: (cur(c), h)),
             blk(lambda h, c: (prev(c), N_SLABS + h)), blk(lambda h, c: (cur(c), N_SLABS + h)),
             blk(lambda h, c: (prev(c), 2 * N_SLABS + h)), blk(lambda h, c: (cur(c), 2 * N_SLABS + h))]
    return specs + [blk(lambda h, c: (cur(c), h))] * n_cur


def _attn_fwd(proj, *, name):
    S = proj.shape[0]
    nc = S // CHUNK

    n = len(DILATIONS)

    def body(q_ref, kp_ref, kc_ref, vp_ref, vc_ref, attn_ref, lse_ref, attn16_ref, *scr):
        o_scr, l_scr = scr[:n], scr[n:]
        c = pl.program_id(1)
        lms = _lane_masks()
        plain, first = _band_mask(False), _band_mask(c == 0)
        for p, d, b, r, nbc in _attn_units():
            rows = _unit_rows(d, b)(r)
            prow = _unit_rows(d, b - 1 if b > 0 else nbc - 1)(r)
            kpr, vpr = (kc_ref, vc_ref) if b > 0 else (kp_ref, vp_ref)
            mask = plain if b > 0 else first
            q = q_ref[rows, :].astype(BF16)
            kcat = jnp.concatenate([kpr[prow, :], kc_ref[rows, :]], axis=0).astype(BF16)
            vcat = jnp.concatenate([vpr[prow, :], vc_ref[rows, :]], axis=0).astype(BF16)
            o_slab = jnp.zeros((BLOCK, SLAB), F32)
            lse_slab = jnp.zeros((BLOCK, SLAB), F32)
            for lm in lms:
                qs = jnp.where(lm, q, jnp.zeros_like(q)) * SCALE
                vh = jnp.where(lm, vcat, jnp.zeros_like(vcat))
                s = jnp.where(mask, _dot(qs, kcat, NT), NEG_INF)
                m = jnp.max(s, axis=-1, keepdims=True)
                e = jnp.exp(s - m)
                l = jnp.sum(e, axis=-1, keepdims=True)
                o_slab = o_slab + _dot(e.astype(BF16), vh, NN) / l
                lse_slab = jnp.where(lm, m + jnp.log(l), lse_slab)
            o_scr[p][rows, :] = o_slab
            l_scr[p][rows, :] = lse_slab
        ls = [l_scr[p][...] for p in range(n)]
        top = functools.reduce(jnp.maximum, ls)
        es = [jnp.exp(l - top) for l in ls]
        den = functools.reduce(jnp.add, es)
        num = functools.reduce(jnp.add, [e * o_scr[p][...] for p, e in enumerate(es)])
        attn = num / den
        attn_ref[...] = attn
        attn16_ref[...] = attn.astype(BF16)
        lse_ref[...] = top + jnp.log(den)

    return pl.pallas_call(
        body, name=name, grid=(N_SLABS, nc), in_specs=_attn_in_specs(nc, 0),
        out_specs=[pl.BlockSpec((CHUNK, SLAB), lambda h, c: (c, h))] * 3,
        out_shape=[jax.ShapeDtypeStruct((S, ATTN_WIDTH), F32)] * 2 + [jax.ShapeDtypeStruct((S, ATTN_WIDTH), BF16)],
        scratch_shapes=[pltpu.VMEM((CHUNK, SLAB), F32)] * (2 * n),
        compiler_params=_cp("parallel", "parallel"),
    )(proj, proj, proj, proj, proj)


def _attn_bwd(proj, dcat, attn, lse, *, name):
    S = proj.shape[0]
    nc = S // CHUNK

    def body(q_ref, kp_ref, kc_ref, vp_ref, vc_ref, do_ref, o_ref, lse_ref, dq_ref, dk_ref, dv_ref,
             dq_acc, dk_prev, dk_cur, dv_prev, dv_cur, delta_scr):
        c = pl.program_id(1)

        @pl.when(c == 0)
        def _():
            dk_prev[...] = jnp.zeros_like(dk_prev)
            dv_prev[...] = jnp.zeros_like(dv_prev)

        @pl.when(c < nc)
        def _():
            lms = _lane_masks()
            plain, first = _band_mask(False), _band_mask(c == 0)
            for acc in (dq_acc, dk_cur, dv_cur):
                acc[...] = jnp.zeros_like(acc)
            prod = do_ref[...] * o_ref[...]
            delta_a, delta_b = (jnp.sum(jnp.where(lm, prod, 0.0), axis=-1, keepdims=True) for lm in lms)
            delta_scr[...] = jnp.where(lms[0], delta_a, delta_b)
            for p, d, b, r, nbc in _attn_units():
                rows = _unit_rows(d, b)(r)
                prow = _unit_rows(d, b - 1 if b > 0 else nbc - 1)(r)
                kpr, vpr = (kc_ref, vc_ref) if b > 0 else (kp_ref, vp_ref)
                dkp, dvp = (dk_cur, dv_cur) if b > 0 else (dk_prev, dv_prev)
                mask = plain if b > 0 else first
                q = q_ref[rows, :].astype(BF16)
                kcat = jnp.concatenate([kpr[prow, :], kc_ref[rows, :]], axis=0).astype(BF16)
                vcat = jnp.concatenate([vpr[prow, :], vc_ref[rows, :]], axis=0).astype(BF16)
                do = do_ref[rows, :]
                lse_sl = lse_ref[rows, :]
                delta_sl = delta_scr[rows, :]
                dq = jnp.zeros((BLOCK, SLAB), F32)
                dkc = jnp.zeros((2 * BLOCK, SLAB), F32)
                dvc = jnp.zeros((2 * BLOCK, SLAB), F32)
                for lm in lms:
                    qs = jnp.where(lm, q, jnp.zeros_like(q)) * SCALE
                    kh = jnp.where(lm, kcat, jnp.zeros_like(kcat))
                    doh = jnp.where(lm, do, 0.0).astype(BF16)
                    delta = jnp.max(jnp.where(lm, delta_sl, -jnp.inf), axis=-1, keepdims=True)
                    lse_h = jnp.max(jnp.where(lm, lse_sl, -jnp.inf), axis=-1, keepdims=True)
                    s = _dot(qs, kcat, NT)
                    e = jnp.where(mask, jnp.exp(s - lse_h), 0.0)
                    dp = _dot(doh, vcat, NT)
                    ds = (e * (dp - delta)).astype(BF16)
                    dq = dq + _dot(ds, kh, NN) * SCALE
                    dkc = dkc + _dot(ds, qs, TN)
                    dvc = dvc + _dot(e.astype(BF16), doh, TN)
                dq_acc[rows, :] += dq
                dkp[prow, :] += dkc[:BLOCK]
                dvp[prow, :] += dvc[:BLOCK]
                dk_cur[rows, :] += dkc[BLOCK:]
                dv_cur[rows, :] += dvc[BLOCK:]
            dq_ref[...] = dq_acc[...].astype(BF16)
            dk_ref[...] = dk_prev[...].astype(BF16)
            dv_ref[...] = dv_prev[...].astype(BF16)
            dk_prev[...] = dk_cur[...]
            dv_prev[...] = dv_cur[...]

        @pl.when(c == nc)
        def _():
            dk_ref[...] = dk_prev[...].astype(BF16)
            dv_ref[...] = dv_prev[...].astype(BF16)

    blk = lambda f: pl.BlockSpec((CHUNK, SLAB), f)
    late = lambda h, c: (jnp.maximum(c - 1, 0), h)
    return pl.pallas_call(
        body, name=name, grid=(N_SLABS, nc + 1), in_specs=_attn_in_specs(nc, 3),
        out_specs=[blk(lambda h, c: (jnp.minimum(c, nc - 1), h)), blk(late), blk(late)],
        out_shape=[jax.ShapeDtypeStruct((S, ATTN_WIDTH), BF16)] * 3,
        scratch_shapes=[pltpu.VMEM((CHUNK, SLAB), F32)] * 6,
        compiler_params=_cp("parallel", "arbitrary"),
    )(proj, proj, proj, proj, proj, dcat, attn, lse)


def _split_bf16(a):
    hi = a.astype(BF16)
    lo = (a - hi.astype(F32)).astype(BF16)
    return hi, lo


def _pooled(ug, halo_g, w, row0, tm):
    ext = jnp.concatenate([halo_g, ug], axis=0)
    hi, lo = _split_bf16(ext)
    rr = lax.broadcasted_iota(jnp.int32, (tm, tm + HALO), 0)
    cc = lax.broadcasted_iota(jnp.int32, (tm, tm + HALO), 1)
    back = rr + HALO - cc
    win = ((back >= 0) & (back < w)).astype(BF16)
    wsum = _dot(win, hi, NN) + _dot(win, lo, NN)
    rows = row0 + lax.broadcasted_iota(jnp.int32, (tm, 1), 0)
    inv = 1.0 / jnp.minimum(rows + 1, w).astype(F32)
    return wsum * inv - ug


def _pool_fwd(u, u_col, pool_w, pool_scale, *, name, tm=256):
    S, W = u.shape[0], POOL_WIDTH
    G = POOL_GROUP_DIM

    def body(u_ref, h_ref, w_ref, s_ref, o_ref):
        i = pl.program_id(0)
        uv = u_ref[...]
        halo = jnp.where(i > 0, h_ref[...], 0.0)
        for g, w in enumerate(POOL_WINDOWS):
            sl = slice(g * G, (g + 1) * G)
            pooled = _pooled(uv[:, sl], halo[:, sl], w, i * tm, tm)
            z = _dot(pooled.astype(BF16), w_ref[g].astype(BF16), NN)
            o_ref[:, sl] = (z * s_ref[:, sl]).astype(BF16)

    per = tm // HALO
    return pl.pallas_call(
        body, name=name, grid=(S // tm,),
        in_specs=[pl.BlockSpec((tm, W), lambda i: (i, u_col)),
                  pl.BlockSpec((HALO, W), lambda i: (jnp.maximum(i * per - 1, 0), u_col)),
                  pl.BlockSpec((len(POOL_WINDOWS), G, G), lambda i: (0, 0, 0)),
                  pl.BlockSpec((1, W), lambda i: (0, 0))],
        out_specs=pl.BlockSpec((tm, W), lambda i: (i, 0)),
        out_shape=jax.ShapeDtypeStruct((S, W), BF16),
        compiler_params=_cp("parallel"),
    )(u, u, pool_w, pool_scale)


def _pool_bwd(u, u_col, dy, dy_col, pool_w, pool_scale, *, name, tm=256):
    S, W = u.shape[0], POOL_WIDTH
    G = POOL_GROUP_DIM
    nt = S // tm

    def body(u_ref, h_ref, dy_ref, dyn_ref, w_ref, s_ref, du_ref, gw_ref, gs_ref):
        i = pl.program_id(0)

        @pl.when(i == 0)
        def _():
            gw_ref[...] = jnp.zeros_like(gw_ref)
            gs_ref[...] = jnp.zeros_like(gs_ref)

        uv = u_ref[...]
        halo = jnp.where(i > 0, h_ref[...], 0.0)
        dyv = dy_ref[...]
        dyn = jnp.where(i < nt - 1, dyn_ref[...], 0.0)
        rr = lax.broadcasted_iota(jnp.int32, (tm, tm + HALO), 0)
        cc = lax.broadcasted_iota(jnp.int32, (tm, tm + HALO), 1)
        rows_ext = i * tm + lax.broadcasted_iota(jnp.int32, (tm + HALO, 1), 0)
        for g, w in enumerate(POOL_WINDOWS):
            sl = slice(g * G, (g + 1) * G)
            wg = w_ref[g].astype(BF16)
            sc = s_ref[:, sl]
            pooled = _pooled(uv[:, sl], halo[:, sl], w, i * tm, tm)
            z = _dot(pooled.astype(BF16), wg, NN)
            gs_ref[:, sl] += jnp.sum(dyv[:, sl] * z, axis=0, keepdims=True)
            dz = dyv[:, sl] * sc
            gw_ref[g] += _dot(pooled.astype(BF16), dz.astype(BF16), TN)
            dz_ext = jnp.concatenate([dz, dyn[:, sl] * sc], axis=0)
            dp_ext = _dot(dz_ext.astype(BF16), wg, NT)
            inv_ext = 1.0 / jnp.minimum(rows_ext + 1, w).astype(F32)
            hi, lo = _split_bf16(dp_ext * inv_ext)
            ahead = cc - rr
            win = ((ahead >= 0) & (ahead < w)).astype(BF16)
            du_ref[:, sl] = (_dot(win, hi, NN) + _dot(win, lo, NN) - dp_ext[:tm]).astype(BF16)

    per = tm // HALO
    nh = S // HALO
    return pl.pallas_call(
        body, name=name, grid=(nt,),
        in_specs=[pl.BlockSpec((tm, W), lambda i: (i, u_col)),
                  pl.BlockSpec((HALO, W), lambda i: (jnp.maximum(i * per - 1, 0), u_col)),
                  pl.BlockSpec((tm, W), lambda i: (i, dy_col)),
                  pl.BlockSpec((HALO, W), lambda i: (jnp.minimum((i + 1) * per, nh - 1), dy_col)),
                  pl.BlockSpec((len(POOL_WINDOWS), G, G), lambda i: (0, 0, 0)),
                  pl.BlockSpec((1, W), lambda i: (0, 0))],
        out_specs=[pl.BlockSpec((tm, W), lambda i: (i, 0)),
                   pl.BlockSpec((len(POOL_WINDOWS), G, G), lambda i: (0, 0, 0)),
                   pl.BlockSpec((1, W), lambda i: (0, 0))],
        out_shape=[jax.ShapeDtypeStruct((S, W), BF16),
                   jax.ShapeDtypeStruct((len(POOL_WINDOWS), G, G), F32),
                   jax.ShapeDtypeStruct((1, W), F32)],
        compiler_params=_cp("arbitrary"),
    )(u, u, dy, dy, pool_w, pool_scale)


GELU_K0 = math.sqrt(2.0 / math.pi)
GELU_K1 = 0.044715


def _gelu_parts(x):
    t = jnp.tanh(GELU_K0 * (x + GELU_K1 * x * x * x))
    gelu = 0.5 * x * (1.0 + t)
    dgelu = 0.5 * (1.0 + t) + 0.5 * x * (1.0 - t * t) * (GELU_K0 * (1.0 + 3.0 * GELU_K1 * x * x))
    return gelu, dgelu


def _shifted(ext):
    return (pltpu.roll(ext, 2, 0)[HALO:], pltpu.roll(ext, 1, 0)[HALO:], ext[HALO:])


def _conv(sh, w, b):
    return b + (sh[0] * w[0:1] + sh[1] * w[1:2] + sh[2] * w[2:3])


def _conv_glu_fwd(u, conv_w, conv_b, *, name, tm=512, tn=256):
    S = u.shape[0]
    F = D_FF
    nj = F // tn

    def body(ug_ref, hg_ref, uv_ref, hv_ref, wg_ref, wv_ref, bg_ref, bv_ref, y_ref):
        i = pl.program_id(0)

        def conv(u_ref, h_ref, w_ref, b_ref):
            halo = jnp.where(i > 0, h_ref[...].astype(F32), 0.0)
            ext = jnp.concatenate([halo, u_ref[...].astype(F32)], axis=0)
            return _conv(_shifted(ext), w_ref[...], b_ref[...])

        gelu, _ = _gelu_parts(conv(ug_ref, hg_ref, wg_ref, bg_ref))
        y_ref[...] = (gelu * conv(uv_ref, hv_ref, wv_ref, bv_ref)).astype(BF16)

    per = tm // HALO
    hrow = lambda i: jnp.maximum(i * per - 1, 0)
    return pl.pallas_call(
        body, name=name, grid=(S // tm, nj),
        in_specs=[pl.BlockSpec((tm, tn), lambda i, j: (i, j)),
                  pl.BlockSpec((HALO, tn), lambda i, j: (hrow(i), j)),
                  pl.BlockSpec((tm, tn), lambda i, j: (i, j + nj)),
                  pl.BlockSpec((HALO, tn), lambda i, j: (hrow(i), j + nj)),
                  pl.BlockSpec((3, tn), lambda i, j: (0, j)), pl.BlockSpec((3, tn), lambda i, j: (0, j + nj)),
                  pl.BlockSpec((1, tn), lambda i, j: (0, j)), pl.BlockSpec((1, tn), lambda i, j: (0, j + nj))],
        out_specs=pl.BlockSpec((tm, tn), lambda i, j: (i, j)),
        out_shape=jax.ShapeDtypeStruct((S, F), BF16),
        compiler_params=_cp("parallel", "parallel"),
    )(u, u, u, u, conv_w, conv_w, conv_b, conv_b)


def _conv_glu_bwd(u, dy, conv_w, conv_b, *, name, tm=512, tn=256):
    S = u.shape[0]
    F = D_FF
    nj = F // tn
    nt = S // tm
    n_ext = tm + HALO

    def body(ug_ref, hgp_ref, hgn_ref, uv_ref, hvp_ref, hvn_ref, dy_ref, dyn_ref,
             wg_ref, wv_ref, bg_ref, bv_ref, dug_ref, duv_ref, gwg_ref, gwv_ref, gbg_ref, gbv_ref):
        i = pl.program_id(1)

        @pl.when(i == 0)
        def _():
            for r in (gwg_ref, gwv_ref, gbg_ref, gbv_ref):
                r[...] = jnp.zeros_like(r)

        def shifted(u_ref, hp_ref, hn_ref):
            halo = jnp.where(i > 0, hp_ref[...].astype(F32), 0.0)
            ext = jnp.concatenate([halo, u_ref[...].astype(F32), hn_ref[...].astype(F32)], axis=0)
            return _shifted(ext)

        sh_g = shifted(ug_ref, hgp_ref, hgn_ref)
        sh_v = shifted(uv_ref, hvp_ref, hvn_ref)
        wg, wv = wg_ref[...], wv_ref[...]
        cg = _conv(sh_g, wg, bg_ref[...])
        cv = _conv(sh_v, wv, bv_ref[...])
        dyn = jnp.where(i < nt - 1, dyn_ref[...].astype(F32), 0.0)
        dy_ext = jnp.concatenate([dy_ref[...].astype(F32), dyn], axis=0)
        gelu, dgelu = _gelu_parts(cg)
        dcg = dy_ext * cv * dgelu
        dcv = dy_ext * gelu

        def back(dc, w):
            return (dc[:tm] * w[2:3] + pltpu.roll(dc, n_ext - 1, 0)[:tm] * w[1:2]
                    + pltpu.roll(dc, n_ext - 2, 0)[:tm] * w[0:1])

        dug_ref[...] = back(dcg, wg).astype(BF16)
        duv_ref[...] = back(dcv, wv).astype(BF16)
        for dc, sh, gw_ref, gb_ref in ((dcg, sh_g, gwg_ref, gbg_ref), (dcv, sh_v, gwv_ref, gbv_ref)):
            dct = dc[:tm]
            gb_ref[...] += jnp.sum(dct, axis=0, keepdims=True)
            for t in range(3):
                gw_ref[t:t + 1, :] += jnp.sum(dct * sh[t][:tm], axis=0, keepdims=True)

    per = tm // HALO
    nh = S // HALO
    hprev = lambda i: jnp.maximum(i * per - 1, 0)
    hnext = lambda i: jnp.minimum((i + 1) * per, nh - 1)
    tile = lambda off: pl.BlockSpec((tm, tn), lambda j, i: (i, j + off))
    hp = lambda off: pl.BlockSpec((HALO, tn), lambda j, i: (hprev(i), j + off))
    hn = lambda off: pl.BlockSpec((HALO, tn), lambda j, i: (hnext(i), j + off))
    vec = lambda rows, off: pl.BlockSpec((rows, tn), lambda j, i: (0, j + off))
    return pl.pallas_call(
        body, name=name, grid=(nj, nt),
        in_specs=[tile(0), hp(0), hn(0), tile(nj), hp(nj), hn(nj), tile(0), hn(0),
                  vec(3, 0), vec(3, nj), vec(1, 0), vec(1, nj)],
        out_specs=[tile(0), tile(0), vec(3, 0), vec(3, 0), vec(1, 0), vec(1, 0)],
        out_shape=[jax.ShapeDtypeStruct((S, F), BF16), jax.ShapeDtypeStruct((S, F), BF16),
                   jax.ShapeDtypeStruct((3, F), F32), jax.ShapeDtypeStruct((3, F), F32),
                   jax.ShapeDtypeStruct((1, F), F32), jax.ShapeDtypeStruct((1, F), F32)],
        compiler_params=_cp("parallel", "arbitrary"),
    )(u, u, u, u, u, u, dy, dy, conv_w, conv_w, conv_b, conv_b)


def _sum_partials(parts, *, name, tr):
    _, R, C = parts.shape

    def body(p_ref, o_ref):
        tot = p_ref[0].astype(F32)
        for j in range(1, N_DEV):
            tot = tot + p_ref[j].astype(F32)
        o_ref[...] = tot

    return pl.pallas_call(
        body, name=name, grid=(R // tr,),
        in_specs=[pl.BlockSpec((N_DEV, tr, C), lambda i: (0, i, 0))],
        out_specs=pl.BlockSpec((tr, C), lambda i: (i, 0)),
        out_shape=jax.ShapeDtypeStruct((R, C), F32),
        compiler_params=_cp("parallel"),
    )(parts)


def _adamw(w, g, m, v, *, name, tr):
    R, C = w.shape
    c1 = 1.0 - ADAM_B1 ** ADAM_STEP
    c2 = 1.0 - ADAM_B2 ** ADAM_STEP

    def body(w_ref, g_ref, m_ref, v_ref, d_ref, nm_ref, nv_ref):
        g = g_ref[...]
        nm = ADAM_B1 * m_ref[...] + (1.0 - ADAM_B1) * g
        nv = ADAM_B2 * v_ref[...] + (1.0 - ADAM_B2) * (g * g)
        d_ref[...] = -ADAM_LR * ((nm / c1) / (jnp.sqrt(nv / c2) + ADAM_EPS) + ADAM_WD * w_ref[...])
        nm_ref[...] = nm
        nv_ref[...] = nv

    spec = pl.BlockSpec((tr, C), lambda i: (i, 0))
    return pl.pallas_call(
        body, name=name, grid=(R // tr,), in_specs=[spec] * 4, out_specs=[spec] * 3,
        out_shape=[jax.ShapeDtypeStruct((R, C), F32)] * 3,
        compiler_params=_cp("parallel"),
    )(w, g, m, v)


def _mesh_pos():
    return lax.axis_index("x"), lax.axis_index("y"), lax.axis_index("c")


def _two_level_gather(x_ref, out_ref, send_sems, recv_sems, local_sem):
    x, y, c = _mesh_pos()
    me, sibling = (x, y, c), (x, y, 1 - c)
    chips = [(1 - x, y), (x, 1 - y), (1 - x, 1 - y)]

    def slot(px, py, pc):
        return out_ref.at[4 * px + 2 * py + pc]

    def copy(k, block, to, src=None):
        return pltpu.make_async_remote_copy(
            src_ref=slot(*block) if src is None else src, dst_ref=slot(*block),
            send_sem=send_sems.at[k], recv_sem=recv_sems.at[k], device_id=to, device_id_type=MESH)

    mine = pltpu.make_async_copy(x_ref, slot(*me), local_sem)
    mine.start()
    first = [copy(0, me, sibling, src=x_ref)]
    first += [copy(1 + j, me, (*chip, c), src=x_ref) for j, chip in enumerate(chips)]
    for cp in first:
        cp.start()
    passed = [copy(4 + j, (*chip, c), sibling) for j, chip in enumerate(chips)]
    for j, chip in enumerate(chips):
        copy(1 + j, (*chip, c), me).wait_recv()
        passed[j].start()
    copy(0, sibling, me).wait_recv()
    for j, chip in enumerate(chips):
        copy(4 + j, (*chip, 1 - c), me).wait_recv()
    for cp in first + passed:
        cp.wait_send()
    mine.wait()


_GATHER_SEMS = [pltpu.SemaphoreType.DMA((7,)), pltpu.SemaphoreType.DMA((7,)), pltpu.SemaphoreType.DMA]


def _all_gather_hbm(block, *, name):
    def body(x_ref, out_ref, send_sems, recv_sems, local_sem):
        _two_level_gather(x_ref, out_ref, send_sems, recv_sems, local_sem)

    return pl.pallas_call(
        body, name=name, in_specs=[ANY], out_specs=ANY,
        out_shape=jax.ShapeDtypeStruct((N_DEV,) + block.shape, block.dtype),
        scratch_shapes=_GATHER_SEMS,
    )(block)


def _all_reduce_small(block, *, name):
    def body(x_ref, all_ref, sum_ref, send_sems, recv_sems, local_sem):
        _two_level_gather(x_ref, all_ref, send_sems, recv_sems, local_sem)
        tot = all_ref[0]
        for j in range(1, N_DEV):
            tot = tot + all_ref[j]
        sum_ref[...] = tot

    return pl.pallas_call(
        body, name=name, in_specs=[VMEM], out_specs=[VMEM, VMEM],
        out_shape=[jax.ShapeDtypeStruct((N_DEV,) + block.shape, block.dtype),
                   jax.ShapeDtypeStruct(block.shape, block.dtype)],
        scratch_shapes=_GATHER_SEMS,
        compiler_params=pltpu.CompilerParams(vmem_limit_bytes=V7X_VMEM_LIMIT),
    )(block)[1]


def _exchange_partials(grads, *, name):
    n = len(grads)

    def body(*refs):
        g_refs, r_refs = refs[:n], refs[n:2 * n]
        send_sems, recv_sems, local_sems = refs[2 * n:]
        x, y, c = _mesh_pos()
        me = 4 * x + 2 * y + c
        sends = []
        for k in range(n):
            rows = g_refs[k].shape[0] // N_DEV
            own = pltpu.make_async_copy(g_refs[k].at[pl.ds(me * rows, rows)], r_refs[k].at[me], local_sems.at[k])
            own.start()
            sends.append(own)
        remote = []
        for p in range(1, N_DEV):
            px, py, pc = x ^ (p >> 2), y ^ ((p >> 1) & 1), c ^ (p & 1)
            peer = 4 * px + 2 * py + pc
            for k in range(n):
                rows = g_refs[k].shape[0] // N_DEV
                cp = pltpu.make_async_remote_copy(
                    src_ref=g_refs[k].at[pl.ds(peer * rows, rows)], dst_ref=r_refs[k].at[me],
                    send_sem=send_sems.at[k, p], recv_sem=recv_sems.at[k, p],
                    device_id=(px, py, pc), device_id_type=MESH)
                cp.start()
                arrival = pltpu.make_async_remote_copy(
                    src_ref=g_refs[k].at[pl.ds(peer * rows, rows)], dst_ref=r_refs[k].at[peer],
                    send_sem=send_sems.at[k, p], recv_sem=recv_sems.at[k, p],
                    device_id=(px, py, pc), device_id_type=MESH)
                remote.append((cp, arrival))
        for cp, arrival in remote:
            arrival.wait_recv()
        for cp, arrival in remote:
            cp.wait_send()
        for own in sends:
            own.wait()

    return pl.pallas_call(
        body, name=name, in_specs=[ANY] * n, out_specs=[ANY] * n,
        out_shape=[jax.ShapeDtypeStruct((N_DEV, g.shape[0] // N_DEV, g.shape[1]), g.dtype) for g in grads],
        scratch_shapes=[pltpu.SemaphoreType.DMA((n, N_DEV)), pltpu.SemaphoreType.DMA((n, N_DEV)),
                        pltpu.SemaphoreType.DMA((n,))],
    )(*grads)


def _local_step(x, target, g_mix_pre, w_in_t, pool_w, pool_scale, w_out, g_mix_post, g_ffn_pre,
                w_up_t, conv_w, conv_b, w_down, g_ffn_post):
    S = x.shape[0]
    aw = ATTN_WIDTH
    h1 = _rms_norm(x, g_mix_pre, name="rms_mix_pre")
    proj = _matmul(h1, w_in_t, trans_b=True, out_dtype=F32, tm=1024, tn=512, name="proj")
    attn, lse, attn16 = _attn_fwd(proj, name="attn_fwd")
    pool = _pool_fwd(proj, 3, pool_w, pool_scale, name="pool_fwd")
    mixed, x2, h2 = _mix_out(attn16, pool, w_out, x, g_mix_post, g_ffn_pre, name="mix_out")
    u = _matmul(h2, w_up_t, trans_b=True, out_dtype=BF16, tm=1024, tn=512, name="ffn_up")
    y = _conv_glu_fwd(u, conv_w, conv_b, name="conv_glu_fwd")
    df, d_out, loss_blk, gg_ffn_post = _ffn_out(y, w_down, x2, target, g_ffn_post, name="ffn_out")
    gw_down = _matmul_tn(y, df, ta=1408, ts=512, name="grad_w_down")
    dyy = _matmul(df, w_down, trans_b=True, out_dtype=BF16, tm=512, tn=D_FF, name="ffn_down_dgrad")
    du_g, du_v, gcw_g, gcw_v, gcb_g, gcb_v = _conv_glu_bwd(u, dyy, conv_w, conv_b, name="conv_glu_bwd")
    gw_up_t = jnp.concatenate([_matmul_tn(du_g, h2, ta=1408, ts=512, name="grad_w_up_gate"),
                               _matmul_tn(du_v, h2, ta=1408, ts=512, name="grad_w_up_val")], axis=0)
    dx2, gg_ffn_pre, dmixed, gg_mix_post = _dgrad_norm(
        [du_g, du_v], w_up_t, d_out, x2, g_ffn_pre, (mixed, g_mix_post), tk=1408, name="ffn_up_dgrad")
    gw_out = jnp.concatenate([_matmul_tn(attn16, dmixed, ta=512, ts=1024, name="grad_w_out_attn"),
                              _matmul_tn(pool, dmixed, ta=512, ts=1024, name="grad_w_out_pool")], axis=0)
    dcat = _matmul(dmixed, w_out, trans_b=True, out_dtype=F32, tm=512, tn=1024, name="mix_out_dgrad")
    d_pool_in, g_pool_w, g_pool_scale = _pool_bwd(proj, 3, dcat, 1, pool_w, pool_scale, name="pool_bwd")
    dproj = list(_attn_bwd(proj, dcat, attn, lse, name="attn_bwd")) + [d_pool_in]
    gw_in_t = jnp.concatenate([_matmul_tn(a, h1, ta=512, ts=1024, name=f"grad_w_in_{k}")
                               for k, a in enumerate(dproj)], axis=0)
    grad_x, gg_mix_pre = _dgrad_norm(dproj, w_in_t, dx2, x, g_mix_pre, None, tk=512, name="proj_dgrad")
    g_conv_w = jnp.concatenate([gcw_g, gcw_v], axis=1)
    g_conv_b = jnp.concatenate([gcb_g, gcb_v], axis=1)
    big = (gw_in_t, gw_out, gw_up_t, gw_down)
    small = dict(g_mix_pre=gg_mix_pre, g_mix_post=gg_mix_post, g_ffn_pre=gg_ffn_pre, g_ffn_post=gg_ffn_post,
                 pool_scale=g_pool_scale, conv_b=g_conv_b, pool_w=g_pool_w, conv_w=g_conv_w)
    return loss_blk, grad_x, big, small


_SMALL = ("g_mix_pre", "g_mix_post", "g_ffn_pre", "g_ffn_post", "pool_scale", "conv_b", "pool_w")
LANES = 128


def _pack_rows(arrays):
    parts = []
    for a in arrays:
        a2 = a.reshape(-1, LANES)
        parts.append(jnp.pad(a2, ((0, (-a2.shape[0]) % 8), (0, 0))))
    return jnp.concatenate(parts, axis=0)


def _unpack_rows(packed, shapes):
    out, row = [], 0
    for shape in shapes:
        rows = math.prod(shape) // LANES
        out.append(packed[row:row + rows].reshape(shape))
        row += -(-rows // 8) * 8
    return out


def kernel(x, g_mix_pre, w_in, pool_w, pool_scale, w_out, g_mix_post, g_ffn_pre, w_up, conv_w, conv_b, w_down, g_ffn_post, loss_target, m_g_mix_pre, m_w_in, m_pool_w, m_pool_scale, m_w_out, m_g_mix_post, m_g_ffn_pre, m_w_up, m_conv_w, m_conv_b, m_w_down, m_g_ffn_post, v_g_mix_pre, v_w_in, v_pool_w, v_pool_scale, v_w_out, v_g_mix_post, v_g_ffn_pre, v_w_up, v_conv_w, v_conv_b, v_w_down, v_g_ffn_post):
    me = 4 * lax.axis_index("x") + 2 * lax.axis_index("y") + lax.axis_index("c")
    n_in, n_out, n_up, n_down = 4 * ATTN_WIDTH, D_MODEL, 2 * D_FF, D_FF
    r_in, r_out, r_up, r_down = (n // N_DEV for n in (n_in, n_out, n_up, n_down))

    cw_shard = conv_w[0]
    cw_bits = lax.bitcast_convert_type(cw_shard.reshape(-1), BF16).reshape(-1)
    cw_rows = 16
    cw_bits = jnp.pad(cw_bits, (0, cw_rows * D_MODEL - cw_bits.shape[0])).reshape(cw_rows, D_MODEL)
    payload = jnp.concatenate([w_in[0].T.astype(BF16), w_out[0].astype(BF16), w_up[0].T.astype(BF16),
                               w_down[0].astype(BF16), cw_bits], axis=0)
    gathered = _all_gather_hbm(payload, name="gather_weights")
    offs = [0, r_in, r_in + r_out, r_in + r_out + r_up, r_in + r_out + r_up + r_down]
    w_in_t = gathered[:, offs[0]:offs[1]].reshape(n_in, D_MODEL)
    w_out_f = gathered[:, offs[1]:offs[2]].reshape(n_out, D_MODEL)
    w_up_t = gathered[:, offs[2]:offs[3]].reshape(n_up, D_MODEL)
    w_down_f = gathered[:, offs[3]:offs[4]].reshape(n_down, D_MODEL)
    n_cw = 3 * (2 * D_FF // N_DEV)
    cw_all = gathered[:, offs[4]:].reshape(N_DEV, -1)[:, :2 * n_cw].reshape(N_DEV, n_cw, 2)
    cw_all = lax.bitcast_convert_type(cw_all, F32).reshape(N_DEV, 3, -1)
    conv_w_f = jnp.transpose(cw_all, (1, 0, 2)).reshape(3, 2 * D_FF)

    loss_blk, grad_x, big, small = _local_step(
        x[0], loss_target[0], g_mix_pre, w_in_t, pool_w[0], pool_scale, w_out_f, g_mix_post, g_ffn_pre,
        w_up_t, conv_w_f, conv_b, w_down_f, g_ffn_post)
    loss = lax.psum(loss_blk[0, 0], ("x", "y", "c"))

    recv = _exchange_partials(list(big), name="exchange_grads")
    g_in_t, g_out, g_up_t, g_down = (
        _sum_partials(r, name=f"sum_partials_{k}", tr=r.shape[1] // 2) for k, r in enumerate(recv))
    grads = {"w_in": g_in_t.T, "w_out": g_out, "w_up": g_up_t.T, "w_down": g_down}

    given = dict(g_mix_pre=g_mix_pre, g_mix_post=g_mix_post, g_ffn_pre=g_ffn_pre, g_ffn_post=g_ffn_post,
                 pool_scale=pool_scale, conv_b=conv_b, pool_w=pool_w)
    small_shapes = [given[k].shape for k in _SMALL]
    total = _all_reduce_small(_pack_rows([small[k] for k in _SMALL] + [small["conv_w"]]), name="all_reduce_small")
    *small_grads, g_conv_w_all = _unpack_rows(total, small_shapes + [(3, 2 * D_FF)])
    grads.update(zip(_SMALL, small_grads))
    width = 2 * D_FF // N_DEV
    grads["conv_w"] = lax.dynamic_slice_in_dim(g_conv_w_all, me * width, width, axis=1)[None]

    weights = dict(g_mix_pre=g_mix_pre, w_in=w_in, pool_w=pool_w, pool_scale=pool_scale, w_out=w_out,
                   g_mix_post=g_mix_post, g_ffn_pre=g_ffn_pre, w_up=w_up, conv_w=conv_w, conv_b=conv_b,
                   w_down=w_down, g_ffn_post=g_ffn_post)
    m_in = dict(g_mix_pre=m_g_mix_pre, w_in=m_w_in, pool_w=m_pool_w, pool_scale=m_pool_scale, w_out=m_w_out,
                g_mix_post=m_g_mix_post, g_ffn_pre=m_g_ffn_pre, w_up=m_w_up, conv_w=m_conv_w, conv_b=m_conv_b,
                w_down=m_w_down, g_ffn_post=m_g_ffn_post)
    v_in = dict(g_mix_pre=v_g_mix_pre, w_in=v_w_in, pool_w=v_pool_w, pool_scale=v_pool_scale, w_out=v_w_out,
                g_mix_post=v_g_mix_post, g_ffn_pre=v_g_ffn_pre, w_up=v_w_up, conv_w=v_conv_w, conv_b=v_conv_b,
                w_down=v_w_down, g_ffn_post=v_g_ffn_post)
    delta, new_m, new_v = {}, {}, {}
    for k in ("w_in", "w_out", "w_up", "w_down"):
        g = grads[k]
        d, nm, nv = _adamw(weights[k][0], g, m_in[k][0], v_in[k][0], name=f"adamw_{k}", tr=g.shape[0] // 2)
        grads[k], delta[k], new_m[k], new_v[k] = g[None], d[None], nm[None], nv[None]
    d, nm, nv = _adamw(weights["conv_w"][0], grads["conv_w"][0], m_in["conv_w"][0], v_in["conv_w"][0],
                       name="adamw_conv_w", tr=3)
    delta["conv_w"], new_m["conv_w"], new_v["conv_w"] = d[None], nm[None], nv[None]
    packed_w = _pack_rows([weights[k] for k in _SMALL])
    small_rows = packed_w.shape[0]
    d, nm, nv = _adamw(packed_w, total[:small_rows], _pack_rows([m_in[k] for k in _SMALL]),
                       _pack_rows([v_in[k] for k in _SMALL]), name="adamw_small", tr=small_rows)
    for k, dk, mk, vk in zip(_SMALL, _unpack_rows(d, small_shapes), _unpack_rows(nm, small_shapes),
                             _unpack_rows(nv, small_shapes)):
        delta[k], new_m[k], new_v[k] = dk, mk, vk

    order = ("g_mix_pre", "w_in", "pool_w", "pool_scale", "w_out", "g_mix_post", "g_ffn_pre", "w_up",
             "conv_w", "conv_b", "w_down", "g_ffn_post")
    return (loss, grad_x[None], *[grads[k] for k in order], *[delta[k] for k in order],
            *[new_m[k] for k in order], *[new_v[k] for k in order])
```

```python
import functools
import math

import jax
import jax.numpy as jnp
from jax import lax
from jax.experimental import pallas as pl
from jax.experimental.pallas import tpu as pltpu

F32 = jnp.float32
BF16 = jnp.bfloat16

D_MODEL = 1024
ATTN_WIDTH = 512
N_HEADS = 8
HEAD_DIM = 64
DILATIONS = (1, 4, 16)
BLOCK = 128
POOL_WIDTH = 512
POOL_WINDOWS = (2, 4, 8, 16)
POOL_GROUP_DIM = 128
D_FF = 2816
EPS = 1e-6
NEG_INF = -1e30
SCALE = HEAD_DIM ** -0.5

ADAM_LR = 0.001
ADAM_B1 = 0.9
ADAM_B2 = 0.999
ADAM_EPS = 1e-08
ADAM_WD = 0.01
ADAM_STEP = 10

N_DEV = 8
HALO = 16
V7X_VMEM_LIMIT = 56 * 1024 * 1024

MESH = pl.DeviceIdType.MESH
ANY = pl.BlockSpec(memory_space=pl.ANY)
VMEM = pl.BlockSpec(memory_space=pltpu.VMEM)

NT = (((1,), (1,)), ((), ()))
NN = (((1,), (0,)), ((), ()))
TN = (((0,), (0,)), ((), ()))


def _cp(*sem):
    return pltpu.CompilerParams(dimension_semantics=sem, vmem_limit_bytes=V7X_VMEM_LIMIT)


def _dot(a, b, dn):
    return lax.dot_general(a, b, dn, preferred_element_type=F32)


def _rms_bwd(xin, g, dy):
    r = lax.rsqrt(jnp.mean(xin * xin, axis=-1, keepdims=True) + EPS)
    xh = xin * r
    gdy = g * dy
    dx = r * (gdy - xh * jnp.mean(gdy * xh, axis=-1, keepdims=True))
    dg = jnp.sum(dy * xh, axis=0, keepdims=True)
    return dx, dg


def _rms_norm(x, g, *, name, tm=512):
    S, D = x.shape

    def body(x_ref, g_ref, o_ref):
        xv = x_ref[...]
        r = lax.rsqrt(jnp.mean(xv * xv, axis=-1, keepdims=True) + EPS)
        o_ref[...] = (xv * r * g_ref[...]).astype(BF16)

    return pl.pallas_call(
        body, name=name, grid=(S // tm,),
        in_specs=[pl.BlockSpec((tm, D), lambda i: (i, 0)), pl.BlockSpec((1, D), lambda i: (0, 0))],
        out_specs=pl.BlockSpec((tm, D), lambda i: (i, 0)),
        out_shape=jax.ShapeDtypeStruct((S, D), BF16),
        compiler_params=_cp("parallel"),
    )(x, g)


def _matmul(a, b, *, trans_b, out_dtype, tm, tn, name):
    M, K = a.shape
    N = b.shape[0] if trans_b else b.shape[1]
    dn = NT if trans_b else NN

    def body(a_ref, b_ref, o_ref):
        o_ref[...] = _dot(a_ref[...], b_ref[...], dn).astype(out_dtype)

    b_spec = (pl.BlockSpec((tn, K), lambda i, j: (j, 0)) if trans_b
              else pl.BlockSpec((K, tn), lambda i, j: (0, j)))
    return pl.pallas_call(
        body, name=name, grid=(M // tm, N // tn),
        in_specs=[pl.BlockSpec((tm, K), lambda i, j: (i, 0)), b_spec],
        out_specs=pl.BlockSpec((tm, tn), lambda i, j: (i, j)),
        out_shape=jax.ShapeDtypeStruct((M, N), out_dtype),
        compiler_params=_cp("parallel", "parallel"),
    )(a, b)


def _matmul_tn(a, b, *, ta, ts, name):
    S, Ka = a.shape
    Nb = b.shape[1]
    ns = S // ts

    def body(a_ref, b_ref, o_ref, acc):
        s = pl.program_id(1)

        @pl.when(s == 0)
        def _():
            acc[...] = jnp.zeros_like(acc)

        acc[...] += _dot(a_ref[...], b_ref[...], TN)

        @pl.when(s == ns - 1)
        def _():
            o_ref[...] = acc[...].astype(BF16)

    return pl.pallas_call(
        body, name=name, grid=(Ka // ta, ns),
        in_specs=[pl.BlockSpec((ts, ta), lambda i, s: (s, i)), pl.BlockSpec((ts, Nb), lambda i, s: (s, 0))],
        out_specs=pl.BlockSpec((ta, Nb), lambda i, s: (i, 0)),
        out_shape=jax.ShapeDtypeStruct((Ka, Nb), BF16),
        scratch_shapes=[pltpu.VMEM((ta, Nb), F32)],
        compiler_params=_cp("parallel", "arbitrary"),
    )(a, b)


def _mix_out(attn, pool, w_out, x, g_post, g_next, *, name, tm=256):
    S, K = attn.shape
    D = w_out.shape[1]

    def body(a_ref, p_ref, w_ref, x_ref, gp_ref, gn_ref, mixed_ref, x2_ref, h2_ref):
        mixed = _dot(a_ref[...], w_ref[:K, :], NN) + _dot(p_ref[...], w_ref[K:, :], NN)
        r = lax.rsqrt(jnp.mean(mixed * mixed, axis=-1, keepdims=True) + EPS)
        x2 = x_ref[...] + mixed * r * gp_ref[...]
        r2 = lax.rsqrt(jnp.mean(x2 * x2, axis=-1, keepdims=True) + EPS)
        mixed_ref[...] = mixed
        x2_ref[...] = x2
        h2_ref[...] = (x2 * r2 * gn_ref[...]).astype(BF16)

    row = lambda i: (i, 0)
    fix = lambda i: (0, 0)
    return pl.pallas_call(
        body, name=name, grid=(S // tm,),
        in_specs=[pl.BlockSpec((tm, K), row), pl.BlockSpec((tm, K), row), pl.BlockSpec((2 * K, D), fix),
                  pl.BlockSpec((tm, D), row), pl.BlockSpec((1, D), fix), pl.BlockSpec((1, D), fix)],
        out_specs=[pl.BlockSpec((tm, D), row)] * 3,
        out_shape=[jax.ShapeDtypeStruct((S, D), F32), jax.ShapeDtypeStruct((S, D), F32),
                   jax.ShapeDtypeStruct((S, D), BF16)],
        compiler_params=_cp("parallel"),
    )(attn, pool, w_out, x, g_post, g_next)


def _ffn_out(y, w_down, x2, target, g_post, *, name, tm=256):
    S, K = y.shape
    D = w_down.shape[1]

    def body(y_ref, w_ref, x2_ref, t_ref, g_ref, df_ref, dout_ref, loss_ref, gg_ref):
        i = pl.program_id(0)

        @pl.when(i == 0)
        def _():
            loss_ref[...] = jnp.zeros_like(loss_ref)
            gg_ref[...] = jnp.zeros_like(gg_ref)

        f = _dot(y_ref[...], w_ref[...], NN)
        g = g_ref[...]
        r = lax.rsqrt(jnp.mean(f * f, axis=-1, keepdims=True) + EPS)
        out = x2_ref[...] + f * r * g
        err = out - t_ref[...]
        dy = err * (1.0 / D)
        df, dg = _rms_bwd(f, g, dy)
        df_ref[...] = df.astype(BF16)
        dout_ref[...] = dy
        gg_ref[...] += dg
        loss_ref[...] += 0.5 * jnp.sum(jnp.mean(err * err, axis=-1, keepdims=True))

    row = lambda i: (i, 0)
    fix = lambda i: (0, 0)
    return pl.pallas_call(
        body, name=name, grid=(S // tm,),
        in_specs=[pl.BlockSpec((tm, K), row), pl.BlockSpec((K, D), fix), pl.BlockSpec((tm, D), row),
                  pl.BlockSpec((tm, D), row), pl.BlockSpec((1, D), fix)],
        out_specs=[pl.BlockSpec((tm, D), row), pl.BlockSpec((tm, D), row),
                   pl.BlockSpec((8, 128), fix), pl.BlockSpec((1, D), fix)],
        out_shape=[jax.ShapeDtypeStruct((S, D), BF16), jax.ShapeDtypeStruct((S, D), F32),
                   jax.ShapeDtypeStruct((8, 128), F32), jax.ShapeDtypeStruct((1, D), F32)],
        compiler_params=_cp("arbitrary"),
    )(y, w_down, x2, target, g_post)


def _dgrad_norm(a_list, w, resid, xin, g, second, *, tk, name, tm=512):
    S, Kp = a_list[0].shape
    na = len(a_list)
    D = w.shape[1]
    kper = Kp // tk
    nk = na * kper
    two = second is not None

    def body(*refs):
        a_refs = refs[:na]
        w_ref, r_ref, x_ref, g_ref = refs[na:na + 4]
        pos = na + 4
        if two:
            x2_ref, g2_ref = refs[pos:pos + 2]
            pos += 2
        dx_ref, gg_ref = refs[pos:pos + 2]
        pos += 2
        if two:
            d2_ref, gg2_ref = refs[pos:pos + 2]
            pos += 2
        acc = refs[pos]
        i = pl.program_id(0)
        k = pl.program_id(1)

        @pl.when(k == 0)
        def _():
            acc[...] = jnp.zeros_like(acc)

        @pl.when((i == 0) & (k == 0))
        def _():
            gg_ref[...] = jnp.zeros_like(gg_ref)
            if two:
                gg2_ref[...] = jnp.zeros_like(gg2_ref)

        for q in range(na):
            @pl.when(k // kper == q)
            def _(q=q):
                acc[...] += _dot(a_refs[q][...], w_ref[...], NN)

        @pl.when(k == nk - 1)
        def _():
            d1, dg1 = _rms_bwd(x_ref[...], g_ref[...], acc[...])
            dx = r_ref[...] + d1
            dx_ref[...] = dx
            gg_ref[...] += dg1
            if two:
                d2, dg2 = _rms_bwd(x2_ref[...], g2_ref[...], dx)
                d2_ref[...] = d2.astype(BF16)
                gg2_ref[...] += dg2

    row = lambda i, k: (i, 0)
    fix = lambda i, k: (0, 0)
    a_specs = [pl.BlockSpec((tm, tk), functools.partial(
        lambda i, k, q: (i, jnp.clip(k - q * kper, 0, kper - 1)), q=q)) for q in range(na)]
    in_specs = a_specs + [pl.BlockSpec((tk, D), lambda i, k: (k, 0)), pl.BlockSpec((tm, D), row),
                          pl.BlockSpec((tm, D), row), pl.BlockSpec((1, D), fix)]
    args = list(a_list) + [w, resid, xin, g]
    out_specs = [pl.BlockSpec((tm, D), row), pl.BlockSpec((1, D), fix)]
    out_shape = [jax.ShapeDtypeStruct((S, D), F32), jax.ShapeDtypeStruct((1, D), F32)]
    if two:
        in_specs += [pl.BlockSpec((tm, D), row), pl.BlockSpec((1, D), fix)]
        args += list(second)
        out_specs += [pl.BlockSpec((tm, D), row), pl.BlockSpec((1, D), fix)]
        out_shape += [jax.ShapeDtypeStruct((S, D), BF16), jax.ShapeDtypeStruct((1, D), F32)]
    return pl.pallas_call(
        body, name=name, grid=(S // tm, nk), in_specs=in_specs, out_specs=out_specs, out_shape=out_shape,
        scratch_shapes=[pltpu.VMEM((tm, D), F32)],
        compiler_params=_cp("arbitrary", "arbitrary"),
    )(*args)


def _band_mask(first_block):
    qi = lax.broadcasted_iota(jnp.int32, (BLOCK, 2 * BLOCK), 0)
    ki = lax.broadcasted_iota(jnp.int32, (BLOCK, 2 * BLOCK), 1)
    first_key = jnp.where(first_block, BLOCK, 0)
    return (ki >= qi) & (ki <= qi + BLOCK) & (ki >= first_key)


def _lane_masks():
    lane = lax.broadcasted_iota(jnp.int32, (1, 2 * HEAD_DIM), 1)
    return (lane < HEAD_DIM, lane >= HEAD_DIM)


CHUNK = BLOCK * max(DILATIONS)
SLAB = 2 * HEAD_DIM
N_SLABS = ATTN_WIDTH // SLAB


def _unit_rows(d, b):
    def rows(r):
        start = r + BLOCK * d * b
        return pl.ds(start, BLOCK, stride=d) if d > 1 else pl.ds(start, BLOCK)
    return rows


def _attn_units():
    for p, d in enumerate(DILATIONS):
        nbc = CHUNK // (BLOCK * d)
        for b in range(nbc):
            for r in range(d):
                yield p, d, b, r, nbc


def _attn_in_specs(nc, n_cur):
    prev = lambda c: jnp.maximum(jnp.minimum(c, nc - 1) - 1, 0)
    cur = lambda c: jnp.minimum(c, nc - 1)
    blk = lambda f: pl.BlockSpec((CHUNK, SLAB), f)
    specs = [blk(lambda h, c: (cur(c), h)),
             blk(lambda h, c: (prev(c), N_SLABS + h)), blk(lambda h, c: (cur(c), N_SLABS + h)),
             blk(lambda h, c: (prev(c), 2 * N_SLABS + h)), blk(lambda h, c: (cur(c), 2 * N_SLABS + h))]
    return specs + [blk(lambda h, c: (cur(c), h))] * n_cur


def _attn_fwd(proj, *, name):
    S = proj.shape[0]
    nc = S // CHUNK

    n = len(DILATIONS)

    def body(q_ref, kp_ref, kc_ref, vp_ref, vc_ref, attn_ref, lse_ref, attn16_ref, *scr):
        o_scr, l_scr = scr[:n], scr[n:]
        c = pl.program_id(1)
        lms = _lane_masks()
        plain, first = _band_mask(False), _band_mask(c == 0)
        for p, d, b, r, nbc in _attn_units():
            rows = _unit_rows(d, b)(r)
            prow = _unit_rows(d, b - 1 if b > 0 else nbc - 1)(r)
            kpr, vpr = (kc_ref, vc_ref) if b > 0 else (kp_ref, vp_ref)
            mask = plain if b > 0 else first
            q = q_ref[rows, :].astype(BF16)
            kcat = jnp.concatenate([kpr[prow, :], kc_ref[rows, :]], axis=0).astype(BF16)
            vcat = jnp.concatenate([vpr[prow, :], vc_ref[rows, :]], axis=0).astype(BF16)
            o_slab = jnp.zeros((BLOCK, SLAB), F32)
            lse_slab = jnp.zeros((BLOCK, SLAB), F32)
            for lm in lms:
                qs = jnp.where(lm, q, jnp.zeros_like(q)) * SCALE
                vh = jnp.where(lm, vcat, jnp.zeros_like(vcat))
                s = jnp.where(mask, _dot(qs, kcat, NT), NEG_INF)
                m = jnp.max(s, axis=-1, keepdims=True)
                e = jnp.exp(s - m)
                l = jnp.sum(e, axis=-1, keepdims=True)
                o_slab = o_slab + _dot(e.astype(BF16), vh, NN) / l
                lse_slab = jnp.where(lm, m + jnp.log(l), lse_slab)
            o_scr[p][rows, :] = o_slab
            l_scr[p][rows, :] = lse_slab
        ls = [l_scr[p][...] for p in range(n)]
        top = functools.reduce(jnp.maximum, ls)
        es = [jnp.exp(l - top) for l in ls]
        den = functools.reduce(jnp.add, es)
        num = functools.reduce(jnp.add, [e * o_scr[p][...] for p, e in enumerate(es)])
        attn = num / den
        attn_ref[...] = attn
        attn16_ref[...] = attn.astype(BF16)
        lse_ref[...] = top + jnp.log(den)

    return pl.pallas_call(
        body, name=name, grid=(N_SLABS, nc), in_specs=_attn_in_specs(nc, 0),
        out_specs=[pl.BlockSpec((CHUNK, SLAB), lambda h, c: (c, h))] * 3,
        out_shape=[jax.ShapeDtypeStruct((S, ATTN_WIDTH), F32)] * 2 + [jax.ShapeDtypeStruct((S, ATTN_WIDTH), BF16)],
        scratch_shapes=[pltpu.VMEM((CHUNK, SLAB), F32)] * (2 * n),
        compiler_params=_cp("parallel", "parallel"),
    )(proj, proj, proj, proj, proj)


def _attn_bwd(proj, dcat, attn, lse, *, name):
    S = proj.shape[0]
    nc = S // CHUNK

    def body(q_ref, kp_ref, kc_ref, vp_ref, vc_ref, do_ref, o_ref, lse_ref, dq_ref, dk_ref, dv_ref,
             dq_acc, dk_prev, dk_cur, dv_prev, dv_cur, delta_scr):
        c = pl.program_id(1)

        @pl.when(c == 0)
        def _():
            dk_prev[...] = jnp.zeros_like(dk_prev)
            dv_prev[...] = jnp.zeros_like(dv_prev)

        @pl.when(c < nc)
        def _():
            lms = _lane_masks()
            plain, first = _band_mask(False), _band_mask(c == 0)
            for acc in (dq_acc, dk_cur, dv_cur):
                acc[...] = jnp.zeros_like(acc)
            prod = do_ref[...] * o_ref[...]
            delta_a, delta_b = (jnp.sum(jnp.where(lm, prod, 0.0), axis=-1, keepdims=True) for lm in lms)
            delta_scr[...] = jnp.where(lms[0], delta_a, delta_b)
            for p, d, b, r, nbc in _attn_units():
                rows = _unit_rows(d, b)(r)
                prow = _unit_rows(d, b - 1 if b > 0 else nbc - 1)(r)
                kpr, vpr = (kc_ref, vc_ref) if b > 0 else (kp_ref, vp_ref)
                dkp, dvp = (dk_cur, dv_cur) if b > 0 else (dk_prev, dv_prev)
                mask = plain if b > 0 else first
                q = q_ref[rows, :].astype(BF16)
                kcat = jnp.concatenate([kpr[prow, :], kc_ref[rows, :]], axis=0).astype(BF16)
                vcat = jnp.concatenate([vpr[prow, :], vc_ref[rows, :]], axis=0).astype(BF16)
                do = do_ref[rows, :]
                lse_sl = lse_ref[rows, :]
                delta_sl = delta_scr[rows, :]
                dq = jnp.zeros((BLOCK, SLAB), F32)
                dkc = jnp.zeros((2 * BLOCK, SLAB), F32)
                dvc = jnp.zeros((2 * BLOCK, SLAB), F32)
                for lm in lms:
                    qs = jnp.where(lm, q, jnp.zeros_like(q)) * SCALE
                    kh = jnp.where(lm, kcat, jnp.zeros_like(kcat))
                    doh = jnp.where(lm, do, 0.0).astype(BF16)
                    delta = jnp.max(jnp.where(lm, delta_sl, -jnp.inf), axis=-1, keepdims=True)
                    lse_h = jnp.max(jnp.where(lm, lse_sl, -jnp.inf), axis=-1, keepdims=True)
                    s = _dot(qs, kcat, NT)
                    e = jnp.where(mask, jnp.exp(s - lse_h), 0.0)
                    dp = _dot(doh, vcat, NT)
                    ds = (e * (dp - delta)).astype(BF16)
                    dq = dq + _dot(ds, kh, NN) * SCALE
                    dkc = dkc + _dot(ds, qs, TN)
                    dvc = dvc + _dot(e.astype(BF16), doh, TN)
                dq_acc[rows, :] += dq
                dkp[prow, :] += dkc[:BLOCK]
                dvp[prow, :] += dvc[:BLOCK]
                dk_cur[rows, :] += dkc[BLOCK:]
                dv_cur[rows, :] += dvc[BLOCK:]
            dq_ref[...] = dq_acc[...].astype(BF16)
            dk_ref[...] = dk_prev[...].astype(BF16)
            dv_ref[...] = dv_prev[...].astype(BF16)
            dk_prev[...] = dk_cur[...]
            dv_prev[...] = dv_cur[...]

        @pl.when(c == nc)
        def _():
            dk_ref[...] = dk_prev[...].astype(BF16)
            dv_ref[...] = dv_prev[...].astype(BF16)

    blk = lambda f: pl.BlockSpec((CHUNK, SLAB), f)
    late = lambda h, c: (jnp.maximum(c - 1, 0), h)
    return pl.pallas_call(
        body, name=name, grid=(N_SLABS, nc + 1), in_specs=_attn_in_specs(nc, 3),
        out_specs=[blk(lambda h, c: (jnp.minimum(c, nc - 1), h)), blk(late), blk(late)],
        out_shape=[jax.ShapeDtypeStruct((S, ATTN_WIDTH), BF16)] * 3,
        scratch_shapes=[pltpu.VMEM((CHUNK, SLAB), F32)] * 6,
        compiler_params=_cp("parallel", "arbitrary"),
    )(proj, proj, proj, proj, proj, dcat, attn, lse)


def _split_bf16(a):
    hi = a.astype(BF16)
    lo = (a - hi.astype(F32)).astype(BF16)
    return hi, lo


def _pooled(ug, halo_g, w, row0, tm):
    ext = jnp.concatenate([halo_g, ug], axis=0)
    hi, lo = _split_bf16(ext)
    rr = lax.broadcasted_iota(jnp.int32, (tm, tm + HALO), 0)
    cc = lax.broadcasted_iota(jnp.int32, (tm, tm + HALO), 1)
    back = rr + HALO - cc
    win = ((back >= 0) & (back < w)).astype(BF16)
    wsum = _dot(win, hi, NN) + _dot(win, lo, NN)
    rows = row0 + lax.broadcasted_iota(jnp.int32, (tm, 1), 0)
    inv = 1.0 / jnp.minimum(rows + 1, w).astype(F32)
    return wsum * inv - ug


def _pool_fwd(u, u_col, pool_w, pool_scale, *, name, tm=256):
    S, W = u.shape[0], POOL_WIDTH
    G = POOL_GROUP_DIM

    def body(u_ref, h_ref, w_ref, s_ref, o_ref):
        i = pl.program_id(0)
        uv = u_ref[...]
        halo = jnp.where(i > 0, h_ref[...], 0.0)
        for g, w in enumerate(POOL_WINDOWS):
            sl = slice(g * G, (g + 1) * G)
            pooled = _pooled(uv[:, sl], halo[:, sl], w, i * tm, tm)
            z = _dot(pooled.astype(BF16), w_ref[g].astype(BF16), NN)
            o_ref[:, sl] = (z * s_ref[:, sl]).astype(BF16)

    per = tm // HALO
    return pl.pallas_call(
        body, name=name, grid=(S // tm,),
        in_specs=[pl.BlockSpec((tm, W), lambda i: (i, u_col)),
                  pl.BlockSpec((HALO, W), lambda i: (jnp.maximum(i * per - 1, 0), u_col)),
                  pl.BlockSpec((len(POOL_WINDOWS), G, G), lambda i: (0, 0, 0)),
                  pl.BlockSpec((1, W), lambda i: (0, 0))],
        out_specs=pl.BlockSpec((tm, W), lambda i: (i, 0)),
        out_shape=jax.ShapeDtypeStruct((S, W), BF16),
        compiler_params=_cp("parallel"),
    )(u, u, pool_w, pool_scale)


def _pool_bwd(u, u_col, dy, dy_col, pool_w, pool_scale, *, name, tm=256):
    S, W = u.shape[0], POOL_WIDTH
    G = POOL_GROUP_DIM
    nt = S // tm

    def body(u_ref, h_ref, dy_ref, dyn_ref, w_ref, s_ref, du_ref, gw_ref, gs_ref):
        i = pl.program_id(0)

        @pl.when(i == 0)
        def _():
            gw_ref[...] = jnp.zeros_like(gw_ref)
            gs_ref[...] = jnp.zeros_like(gs_ref)

        uv = u_ref[...]
        halo = jnp.where(i > 0, h_ref[...], 0.0)
        dyv = dy_ref[...]
        dyn = jnp.where(i < nt - 1, dyn_ref[...], 0.0)
        rr = lax.broadcasted_iota(jnp.int32, (tm, tm + HALO), 0)
        cc = lax.broadcasted_iota(jnp.int32, (tm, tm + HALO), 1)
        rows_ext = i * tm + lax.broadcasted_iota(jnp.int32, (tm + HALO, 1), 0)
        for g, w in enumerate(POOL_WINDOWS):
            sl = slice(g * G, (g + 1) * G)
            wg = w_ref[g].astype(BF16)
            sc = s_ref[:, sl]
            pooled = _pooled(uv[:, sl], halo[:, sl], w, i * tm, tm)
            z = _dot(pooled.astype(BF16), wg, NN)
            gs_ref[:, sl] += jnp.sum(dyv[:, sl] * z, axis=0, keepdims=True)
            dz = dyv[:, sl] * sc
            gw_ref[g] += _dot(pooled.astype(BF16), dz.astype(BF16), TN)
            dz_ext = jnp.concatenate([dz, dyn[:, sl] * sc], axis=0)
            dp_ext = _dot(dz_ext.astype(BF16), wg, NT)
            inv_ext = 1.0 / jnp.minimum(rows_ext + 1, w).astype(F32)
            hi, lo = _split_bf16(dp_ext * inv_ext)
            ahead = cc - rr
            win = ((ahead >= 0) & (ahead < w)).astype(BF16)
            du_ref[:, sl] = (_dot(win, hi, NN) + _dot(win, lo, NN) - dp_ext[:tm]).astype(BF16)

    per = tm // HALO
    nh = S // HALO
    return pl.pallas_call(
        body, name=name, grid=(nt,),
        in_specs=[pl.BlockSpec((tm, W), lambda i: (i, u_col)),
                  pl.BlockSpec((HALO, W), lambda i: (jnp.maximum(i * per - 1, 0), u_col)),
                  pl.BlockSpec((tm, W), lambda i: (i, dy_col)),
                  pl.BlockSpec((HALO, W), lambda i: (jnp.minimum((i + 1) * per, nh - 1), dy_col)),
                  pl.BlockSpec((len(POOL_WINDOWS), G, G), lambda i: (0, 0, 0)),
                  pl.BlockSpec((1, W), lambda i: (0, 0))],
        out_specs=[pl.BlockSpec((tm, W), lambda i: (i, 0)),
                   pl.BlockSpec((len(POOL_WINDOWS), G, G), lambda i: (0, 0, 0)),
                   pl.BlockSpec((1, W), lambda i: (0, 0))],
        out_shape=[jax.ShapeDtypeStruct((S, W), BF16),
                   jax.ShapeDtypeStruct((len(POOL_WINDOWS), G, G), F32),
                   jax.ShapeDtypeStruct((1, W), F32)],
        compiler_params=_cp("arbitrary"),
    )(u, u, dy, dy, pool_w, pool_scale)


GELU_K0 = math.sqrt(2.0 / math.pi)
GELU_K1 = 0.044715


def _gelu_parts(x):
    t = jnp.tanh(GELU_K0 * (x + GELU_K1 * x * x * x))
    gelu = 0.5 * x * (1.0 + t)
    dgelu = 0.5 * (1.0 + t) + 0.5 * x * (1.0 - t * t) * (GELU_K0 * (1.0 + 3.0 * GELU_K1 * x * x))
    return gelu, dgelu


def _shifted(ext, halo):
    return (pltpu.roll(ext, 2, 0)[halo:], pltpu.roll(ext, 1, 0)[halo:], ext[halo:])


def _conv(sh, w, b):
    return b + (sh[0] * w[0:1] + sh[1] * w[1:2] + sh[2] * w[2:3])


F32_ROWS = 8


def _ffn_up_glu(h, w_up_t, conv_w, conv_b, *, name, tm=1024, tn=256):
    S, K = h.shape
    F = D_FF
    nj = F // tn

    def body(h_ref, wg_ref, wv_ref, cwg_ref, cwv_ref, cbg_ref, cbv_ref, ug_ref, uv_ref, y_ref, carry):
        i = pl.program_id(0)
        j = pl.program_id(1)

        def half(w_ref, u_ref, cw_ref, cb_ref, slot):
            u16 = _dot(h_ref[...], w_ref[...], NT).astype(BF16)
            u_ref[...] = u16
            u = u16.astype(F32)
            halo = jnp.where(i > 0, carry[j, slot], 0.0)
            carry[j, slot] = u[tm - F32_ROWS:]
            ext = jnp.concatenate([halo, u], axis=0)
            return _conv(_shifted(ext, F32_ROWS), cw_ref[...], cb_ref[...])

        gelu, _ = _gelu_parts(half(wg_ref, ug_ref, cwg_ref, cbg_ref, 0))
        y_ref[...] = (gelu * half(wv_ref, uv_ref, cwv_ref, cbv_ref, 1)).astype(BF16)

    tile = pl.BlockSpec((tm, tn), lambda i, j: (i, j))
    vec = lambda rows, off: pl.BlockSpec((rows, tn), lambda i, j: (0, j + off))
    return pl.pallas_call(
        body, name=name, grid=(S // tm, nj),
        in_specs=[pl.BlockSpec((tm, K), lambda i, j: (i, 0)),
                  pl.BlockSpec((tn, K), lambda i, j: (j, 0)), pl.BlockSpec((tn, K), lambda i, j: (j + nj, 0)),
                  vec(3, 0), vec(3, nj), vec(1, 0), vec(1, nj)],
        out_specs=[tile, tile, tile],
        out_shape=[jax.ShapeDtypeStruct((S, F), BF16)] * 3,
        scratch_shapes=[pltpu.VMEM((nj, 2, F32_ROWS, tn), F32)],
        compiler_params=_cp("arbitrary", "arbitrary"),
    )(h, w_up_t, w_up_t, conv_w, conv_w, conv_b, conv_b)


def _ffn_glu_bwd(u_g, u_v, df, w_down, h, conv_w, conv_b, *, name, tm=512, tn=256):
    S = u_g.shape[0]
    F = D_FF
    D = df.shape[1]
    nj = F // tn
    nt = S // tm
    n_ext = tm + HALO

    def body(ug_ref, hgp_ref, hgn_ref, uv_ref, hvp_ref, hvn_ref, df_ref, dfn_ref, wd_ref, h_ref,
             wg_ref, wv_ref, bg_ref, bv_ref,
             dug_ref, duv_ref, gug_ref, guv_ref, gd_ref, gwg_ref, gwv_ref, gbg_ref, gbv_ref,
             acc_ug, acc_uv, acc_d):
        i = pl.program_id(1)

        @pl.when(i == 0)
        def _():
            for r in (gwg_ref, gwv_ref, gbg_ref, gbv_ref, acc_ug, acc_uv, acc_d):
                r[...] = jnp.zeros_like(r)

        def shifted(u_ref, hp_ref, hn_ref):
            halo = jnp.where(i > 0, hp_ref[...].astype(F32), 0.0)
            ext = jnp.concatenate([halo, u_ref[...].astype(F32), hn_ref[...].astype(F32)], axis=0)
            return _shifted(ext, HALO)

        sh_g = shifted(ug_ref, hgp_ref, hgn_ref)
        sh_v = shifted(uv_ref, hvp_ref, hvn_ref)
        wg, wv = wg_ref[...], wv_ref[...]
        cg = _conv(sh_g, wg, bg_ref[...])
        cv = _conv(sh_v, wv, bv_ref[...])
        dfn = jnp.where(i < nt - 1, dfn_ref[...], jnp.zeros_like(dfn_ref))
        df_tile = df_ref[...]
        dy_ext = _dot(jnp.concatenate([df_tile, dfn], axis=0), wd_ref[...], NT)
        gelu, dgelu = _gelu_parts(cg)
        dcg = dy_ext * cv * dgelu
        dcv = dy_ext * gelu

        def back(dc, w):
            return (dc[:tm] * w[2:3] + pltpu.roll(dc, n_ext - 1, 0)[:tm] * w[1:2]
                    + pltpu.roll(dc, n_ext - 2, 0)[:tm] * w[0:1])

        du_g = back(dcg, wg).astype(BF16)
        du_v = back(dcv, wv).astype(BF16)
        dug_ref[...] = du_g
        duv_ref[...] = du_v
        h_tile = h_ref[...]
        acc_ug[...] += _dot(du_g, h_tile, TN)
        acc_uv[...] += _dot(du_v, h_tile, TN)
        acc_d[...] += _dot((gelu[:tm] * cv[:tm]).astype(BF16), df_tile, TN)
        for dc, sh, gw_ref, gb_ref in ((dcg, sh_g, gwg_ref, gbg_ref), (dcv, sh_v, gwv_ref, gbv_ref)):
            dct = dc[:tm]
            gb_ref[...] += jnp.sum(dct, axis=0, keepdims=True)
            for t in range(3):
                gw_ref[t:t + 1, :] += jnp.sum(dct * sh[t][:tm], axis=0, keepdims=True)

        @pl.when(i == nt - 1)
        def _():
            gug_ref[...] = acc_ug[...].astype(BF16)
            guv_ref[...] = acc_uv[...].astype(BF16)
            gd_ref[...] = acc_d[...].astype(BF16)

    per = tm // HALO
    nh = S // HALO
    hprev = lambda i: jnp.maximum(i * per - 1, 0)
    hnext = lambda i: jnp.minimum((i + 1) * per, nh - 1)
    tile = pl.BlockSpec((tm, tn), lambda j, i: (i, j))
    hp = pl.BlockSpec((HALO, tn), lambda j, i: (hprev(i), j))
    hn = pl.BlockSpec((HALO, tn), lambda j, i: (hnext(i), j))
    vec = lambda rows, off: pl.BlockSpec((rows, tn), lambda j, i: (0, j + off))
    wide = pl.BlockSpec((tm, D), lambda j, i: (i, 0))
    wrow = pl.BlockSpec((tn, D), lambda j, i: (j, 0))
    return pl.pallas_call(
        body, name=name, grid=(nj, nt),
        in_specs=[tile, hp, hn, tile, hp, hn, wide, pl.BlockSpec((HALO, D), lambda j, i: (hnext(i), 0)),
                  wrow, wide, vec(3, 0), vec(3, nj), vec(1, 0), vec(1, nj)],
        out_specs=[tile, tile, wrow, wrow, wrow, vec(3, 0), vec(3, 0), vec(1, 0), vec(1, 0)],
        out_shape=[jax.ShapeDtypeStruct((S, F), BF16), jax.ShapeDtypeStruct((S, F), BF16),
                   jax.ShapeDtypeStruct((F, D), BF16), jax.ShapeDtypeStruct((F, D), BF16),
                   jax.ShapeDtypeStruct((F, D), BF16),
                   jax.ShapeDtypeStruct((3, F), F32), jax.ShapeDtypeStruct((3, F), F32),
                   jax.ShapeDtypeStruct((1, F), F32), jax.ShapeDtypeStruct((1, F), F32)],
        scratch_shapes=[pltpu.VMEM((tn, D), F32)] * 3,
        compiler_params=_cp("parallel", "arbitrary"),
    )(u_g, u_g, u_g, u_v, u_v, u_v, df, df, w_down, h, conv_w, conv_w, conv_b, conv_b)


def _sum_partials(parts, *, name, tr):
    _, R, C = parts.shape

    def body(p_ref, o_ref):
        tot = p_ref[0].astype(F32)
        for j in range(1, N_DEV):
            tot = tot + p_ref[j].astype(F32)
        o_ref[...] = tot

    return pl.pallas_call(
        body, name=name, grid=(R // tr,),
        in_specs=[pl.BlockSpec((N_DEV, tr, C), lambda i: (0, i, 0))],
        out_specs=pl.BlockSpec((tr, C), lambda i: (i, 0)),
        out_shape=jax.ShapeDtypeStruct((R, C), F32),
        compiler_params=_cp("parallel"),
    )(parts)


def _adamw(w, g, m, v, *, name, tr):
    R, C = w.shape
    c1 = 1.0 - ADAM_B1 ** ADAM_STEP
    c2 = 1.0 - ADAM_B2 ** ADAM_STEP

    def body(w_ref, g_ref, m_ref, v_ref, d_ref, nm_ref, nv_ref):
        g = g_ref[...]
        nm = ADAM_B1 * m_ref[...] + (1.0 - ADAM_B1) * g
        nv = ADAM_B2 * v_ref[...] + (1.0 - ADAM_B2) * (g * g)
        d_ref[...] = -ADAM_LR * ((nm / c1) / (jnp.sqrt(nv / c2) + ADAM_EPS) + ADAM_WD * w_ref[...])
        nm_ref[...] = nm
        nv_ref[...] = nv

    spec = pl.BlockSpec((tr, C), lambda i: (i, 0))
    return pl.pallas_call(
        body, name=name, grid=(R // tr,), in_specs=[spec] * 4, out_specs=[spec] * 3,
        out_shape=[jax.ShapeDtypeStruct((R, C), F32)] * 3,
        compiler_params=_cp("parallel"),
    )(w, g, m, v)


def _mesh_pos():
    return lax.axis_index("x"), lax.axis_index("y"), lax.axis_index("c")


def _two_level_gather(x_ref, out_ref, send_sems, recv_sems, local_sem):
    x, y, c = _mesh_pos()
    me, sibling = (x, y, c), (x, y, 1 - c)
    chips = [(1 - x, y), (x, 1 - y), (1 - x, 1 - y)]

    def slot(px, py, pc):
        return out_ref.at[4 * px + 2 * py + pc]

    def copy(k, block, to, src=None):
        return pltpu.make_async_remote_copy(
            src_ref=slot(*block) if src is None else src, dst_ref=slot(*block),
            send_sem=send_sems.at[k], recv_sem=recv_sems.at[k], device_id=to, device_id_type=MESH)

    mine = pltpu.make_async_copy(x_ref, slot(*me), local_sem)
    mine.start()
    first = [copy(0, me, sibling, src=x_ref)]
    first += [copy(1 + j, me, (*chip, c), src=x_ref) for j, chip in enumerate(chips)]
    for cp in first:
        cp.start()
    passed = [copy(4 + j, (*chip, c), sibling) for j, chip in enumerate(chips)]
    for j, chip in enumerate(chips):
        copy(1 + j, (*chip, c), me).wait_recv()
        passed[j].start()
    copy(0, sibling, me).wait_recv()
    for j, chip in enumerate(chips):
        copy(4 + j, (*chip, 1 - c), me).wait_recv()
    for cp in first + passed:
        cp.wait_send()
    mine.wait()


_GATHER_SEMS = [pltpu.SemaphoreType.DMA((7,)), pltpu.SemaphoreType.DMA((7,)), pltpu.SemaphoreType.DMA]


def _all_gather_hbm(block, *, name):
    def body(x_ref, out_ref, send_sems, recv_sems, local_sem):
        _two_level_gather(x_ref, out_ref, send_sems, recv_sems, local_sem)

    return pl.pallas_call(
        body, name=name, in_specs=[ANY], out_specs=ANY,
        out_shape=jax.ShapeDtypeStruct((N_DEV,) + block.shape, block.dtype),
        scratch_shapes=_GATHER_SEMS,
    )(block)


def _all_reduce_small(block, *, name):
    def body(x_ref, all_ref, sum_ref, send_sems, recv_sems, local_sem):
        _two_level_gather(x_ref, all_ref, send_sems, recv_sems, local_sem)
        tot = all_ref[0]
        for j in range(1, N_DEV):
            tot = tot + all_ref[j]
        sum_ref[...] = tot

    return pl.pallas_call(
        body, name=name, in_specs=[VMEM], out_specs=[VMEM, VMEM],
        out_shape=[jax.ShapeDtypeStruct((N_DEV,) + block.shape, block.dtype),
                   jax.ShapeDtypeStruct(block.shape, block.dtype)],
        scratch_shapes=_GATHER_SEMS,
        compiler_params=pltpu.CompilerParams(vmem_limit_bytes=V7X_VMEM_LIMIT),
    )(block)[1]


def _exchange_partials(grads, *, name):
    n = len(grads)

    def body(*refs):
        g_refs, r_refs = refs[:n], refs[n:2 * n]
        send_sems, recv_sems, local_sems = refs[2 * n:]
        x, y, c = _mesh_pos()
        me = 4 * x + 2 * y + c
        sends = []
        for k in range(n):
            rows = g_refs[k].shape[0] // N_DEV
            own = pltpu.make_async_copy(g_refs[k].at[pl.ds(me * rows, rows)], r_refs[k].at[me], local_sems.at[k])
            own.start()
            sends.append(own)
        remote = []
        for p in range(1, N_DEV):
            px, py, pc = x ^ (p >> 2), y ^ ((p >> 1) & 1), c ^ (p & 1)
            peer = 4 * px + 2 * py + pc
            for k in range(n):
                rows = g_refs[k].shape[0] // N_DEV
                cp = pltpu.make_async_remote_copy(
                    src_ref=g_refs[k].at[pl.ds(peer * rows, rows)], dst_ref=r_refs[k].at[me],
                    send_sem=send_sems.at[k, p], recv_sem=recv_sems.at[k, p],
                    device_id=(px, py, pc), device_id_type=MESH)
                cp.start()
                arrival = pltpu.make_async_remote_copy(
                    src_ref=g_refs[k].at[pl.ds(peer * rows, rows)], dst_ref=r_refs[k].at[peer],
                    send_sem=send_sems.at[k, p], recv_sem=recv_sems.at[k, p],
                    device_id=(px, py, pc), device_id_type=MESH)
                remote.append((cp, arrival))
        for cp, arrival in remote:
            arrival.wait_recv()
        for cp, arrival in remote:
            cp.wait_send()
        for own in sends:
            own.wait()

    return pl.pallas_call(
        body, name=name, in_specs=[ANY] * n, out_specs=[ANY] * n,
        out_shape=[jax.ShapeDtypeStruct((N_DEV, g.shape[0] // N_DEV, g.shape[1]), g.dtype) for g in grads],
        scratch_shapes=[pltpu.SemaphoreType.DMA((n, N_DEV)), pltpu.SemaphoreType.DMA((n, N_DEV)),
                        pltpu.SemaphoreType.DMA((n,))],
    )(*grads)


def _local_step(x, target, g_mix_pre, w_in_t, pool_w, pool_scale, w_out, g_mix_post, g_ffn_pre,
                w_up_t, conv_w, conv_b, w_down, g_ffn_post):
    S = x.shape[0]
    aw = ATTN_WIDTH
    h1 = _rms_norm(x, g_mix_pre, name="rms_mix_pre")
    proj = _matmul(h1, w_in_t, trans_b=True, out_dtype=F32, tm=1024, tn=512, name="proj")
    attn, lse, attn16 = _attn_fwd(proj, name="attn_fwd")
    pool = _pool_fwd(proj, 3, pool_w, pool_scale, name="pool_fwd")
    mixed, x2, h2 = _mix_out(attn16, pool, w_out, x, g_mix_post, g_ffn_pre, name="mix_out")
    u_g, u_v, y = _ffn_up_glu(h2, w_up_t, conv_w, conv_b, name="ffn_up_glu")
    df, d_out, loss_blk, gg_ffn_post = _ffn_out(y, w_down, x2, target, g_ffn_post, name="ffn_out")
    du_g, du_v, gw_up_g, gw_up_v, gw_down, gcw_g, gcw_v, gcb_g, gcb_v = _ffn_glu_bwd(
        u_g, u_v, df, w_down, h2, conv_w, conv_b, name="ffn_glu_bwd")
    gw_up_t = jnp.concatenate([gw_up_g, gw_up_v], axis=0)
    dx2, gg_ffn_pre, dmixed, gg_mix_post = _dgrad_norm(
        [du_g, du_v], w_up_t, d_out, x2, g_ffn_pre, (mixed, g_mix_post), tk=1408, name="ffn_up_dgrad")
    gw_out = jnp.concatenate([_matmul_tn(attn16, dmixed, ta=512, ts=1024, name="grad_w_out_attn"),
                              _matmul_tn(pool, dmixed, ta=512, ts=1024, name="grad_w_out_pool")], axis=0)
    dcat = _matmul(dmixed, w_out, trans_b=True, out_dtype=F32, tm=512, tn=1024, name="mix_out_dgrad")
    d_pool_in, g_pool_w, g_pool_scale = _pool_bwd(proj, 3, dcat, 1, pool_w, pool_scale, name="pool_bwd")
    dproj = list(_attn_bwd(proj, dcat, attn, lse, name="attn_bwd")) + [d_pool_in]
    gw_in_t = jnp.concatenate([_matmul_tn(a, h1, ta=512, ts=1024, name=f"grad_w_in_{k}")
                               for k, a in enumerate(dproj)], axis=0)
    grad_x, gg_mix_pre = _dgrad_norm(dproj, w_in_t, dx2, x, g_mix_pre, None, tk=512, name="proj_dgrad")
    g_conv_w = jnp.concatenate([gcw_g, gcw_v], axis=1)
    g_conv_b = jnp.concatenate([gcb_g, gcb_v], axis=1)
    big = (gw_in_t, gw_out, gw_up_t, gw_down)
    small = dict(g_mix_pre=gg_mix_pre, g_mix_post=gg_mix_post, g_ffn_pre=gg_ffn_pre, g_ffn_post=gg_ffn_post,
                 pool_scale=g_pool_scale, conv_b=g_conv_b, pool_w=g_pool_w, conv_w=g_conv_w)
    return loss_blk, grad_x, big, small


_SMALL = ("g_mix_pre", "g_mix_post", "g_ffn_pre", "g_ffn_post", "pool_scale", "conv_b", "pool_w")
LANES = 128


def _pack_rows(arrays):
    parts = []
    for a in arrays:
        a2 = a.reshape(-1, LANES)
        parts.append(jnp.pad(a2, ((0, (-a2.shape[0]) % 8), (0, 0))))
    return jnp.concatenate(parts, axis=0)


def _unpack_rows(packed, shapes):
    out, row = [], 0
    for shape in shapes:
        rows = math.prod(shape) // LANES
        out.append(packed[row:row + rows].reshape(shape))
        row += -(-rows // 8) * 8
    return out


def kernel(x, g_mix_pre, w_in, pool_w, pool_scale, w_out, g_mix_post, g_ffn_pre, w_up, conv_w, conv_b, w_down, g_ffn_post, loss_target, m_g_mix_pre, m_w_in, m_pool_w, m_pool_scale, m_w_out, m_g_mix_post, m_g_ffn_pre, m_w_up, m_conv_w, m_conv_b, m_w_down, m_g_ffn_post, v_g_mix_pre, v_w_in, v_pool_w, v_pool_scale, v_w_out, v_g_mix_post, v_g_ffn_pre, v_w_up, v_conv_w, v_conv_b, v_w_down, v_g_ffn_post):
    me = 4 * lax.axis_index("x") + 2 * lax.axis_index("y") + lax.axis_index("c")
    n_in, n_out, n_up, n_down = 4 * ATTN_WIDTH, D_MODEL, 2 * D_FF, D_FF
    r_in, r_out, r_up, r_down = (n // N_DEV for n in (n_in, n_out, n_up, n_down))

    cw_shard = conv_w[0]
    cw_bits = lax.bitcast_convert_type(cw_shard.reshape(-1), BF16).reshape(-1)
    cw_rows = 16
    cw_bits = jnp.pad(cw_bits, (0, cw_rows * D_MODEL - cw_bits.shape[0])).reshape(cw_rows, D_MODEL)
    payload = jnp.concatenate([w_in[0].T.astype(BF16), w_out[0].astype(BF16), w_up[0].T.astype(BF16),
                               w_down[0].astype(BF16), cw_bits], axis=0)
    gathered = _all_gather_hbm(payload, name="gather_weights")
    offs = [0, r_in, r_in + r_out, r_in + r_out + r_up, r_in + r_out + r_up + r_down]
    w_in_t = gathered[:, offs[0]:offs[1]].reshape(n_in, D_MODEL)
    w_out_f = gathered[:, offs[1]:offs[2]].reshape(n_out, D_MODEL)
    w_up_t = gathered[:, offs[2]:offs[3]].reshape(n_up, D_MODEL)
    w_down_f = gathered[:, offs[3]:offs[4]].reshape(n_down, D_MODEL)
    n_cw = 3 * (2 * D_FF // N_DEV)
    cw_all = gathered[:, offs[4]:].reshape(N_DEV, -1)[:, :2 * n_cw].reshape(N_DEV, n_cw, 2)
    cw_all = lax.bitcast_convert_type(cw_all, F32).reshape(N_DEV, 3, -1)
    conv_w_f = jnp.transpose(cw_all, (1, 0, 2)).reshape(3, 2 * D_FF)

    loss_blk, grad_x, big, small = _local_step(
        x[0], loss_target[0], g_mix_pre, w_in_t, pool_w[0], pool_scale, w_out_f, g_mix_post, g_ffn_pre,
        w_up_t, conv_w_f, conv_b, w_down_f, g_ffn_post)
    loss = lax.psum(loss_blk[0, 0], ("x", "y", "c"))

    recv = _exchange_partials(list(big), name="exchange_grads")
    g_in_t, g_out, g_up_t, g_down = (
        _sum_partials(r, name=f"sum_partials_{k}", tr=r.shape[1] // 2) for k, r in enumerate(recv))
    grads = {"w_in": g_in_t.T, "w_out": g_out, "w_up": g_up_t.T, "w_down": g_down}

    given = dict(g_mix_pre=g_mix_pre, g_mix_post=g_mix_post, g_ffn_pre=g_ffn_pre, g_ffn_post=g_ffn_post,
                 pool_scale=pool_scale, conv_b=conv_b, pool_w=pool_w)
    small_shapes = [given[k].shape for k in _SMALL]
    total = _all_reduce_small(_pack_rows([small[k] for k in _SMALL] + [small["conv_w"]]), name="all_reduce_small")
    *small_grads, g_conv_w_all = _unpack_rows(total, small_shapes + [(3, 2 * D_FF)])
    grads.update(zip(_SMALL, small_grads))
    width = 2 * D_FF // N_DEV
    grads["conv_w"] = lax.dynamic_slice_in_dim(g_conv_w_all, me * width, width, axis=1)[None]

    weights = dict(g_mix_pre=g_mix_pre, w_in=w_in, pool_w=pool_w, pool_scale=pool_scale, w_out=w_out,
                   g_mix_post=g_mix_post, g_ffn_pre=g_ffn_pre, w_up=w_up, conv_w=conv_w, conv_b=conv_b,
                   w_down=w_down, g_ffn_post=g_ffn_post)
    m_in = dict(g_mix_pre=m_g_mix_pre, w_in=m_w_in, pool_w=m_pool_w, pool_scale=m_pool_scale, w_out=m_w_out,
                g_mix_post=m_g_mix_post, g_ffn_pre=m_g_ffn_pre, w_up=m_w_up, conv_w=m_conv_w, conv_b=m_conv_b,
                w_down=m_w_down, g_ffn_post=m_g_ffn_post)
    v_in = dict(g_mix_pre=v_g_mix_pre, w_in=v_w_in, pool_w=v_pool_w, pool_scale=v_pool_scale, w_out=v_w_out,
                g_mix_post=v_g_mix_post, g_ffn_pre=v_g_ffn_pre, w_up=v_w_up, conv_w=v_conv_w, conv_b=v_conv_b,
                w_down=v_w_down, g_ffn_post=v_g_ffn_post)
    delta, new_m, new_v = {}, {}, {}
    for k in ("w_in", "w_out", "w_up", "w_down"):
        g = grads[k]
        d, nm, nv = _adamw(weights[k][0], g, m_in[k][0], v_in[k][0], name=f"adamw_{k}", tr=g.shape[0] // 2)
        grads[k], delta[k], new_m[k], new_v[k] = g[None], d[None], nm[None], nv[None]
    d, nm, nv = _adamw(weights["conv_w"][0], grads["conv_w"][0], m_in["conv_w"][0], v_in["conv_w"][0],
                       name="adamw_conv_w", tr=3)
    delta["conv_w"], new_m["conv_w"], new_v["conv_w"] = d[None], nm[None], nv[None]
    packed_w = _pack_rows([weights[k] for k in _SMALL])
    small_rows = packed_w.shape[0]
    d, nm, nv = _adamw(packed_w, total[:small_rows], _pack_rows([m_in[k] for k in _SMALL]),
                       _pack_rows([v_in[k] for k in _SMALL]), name="adamw_small", tr=small_rows)
    for k, dk, mk, vk in zip(_SMALL, _unpack_rows(d, small_shapes), _unpack_rows(nm, small_shapes),
                             _unpack_rows(nv, small_shapes)):
        delta[k], new_m[k], new_v[k] = dk, mk, vk

    order = ("g_mix_pre", "w_in", "pool_w", "pool_scale", "w_out", "g_mix_post", "g_ffn_pre", "w_up",
             "conv_w", "conv_b", "w_down", "g_ffn_post")
    return (loss, grad_x[None], *[grads[k] for k in order], *[delta[k] for k in order],
            *[new_m[k] for k in order], *[new_v[k] for k in order])
```

```python
import functools
import math

import jax
import jax.numpy as jnp
from jax import lax
from jax.experimental import pallas as pl
from jax.experimental.pallas import tpu as pltpu

F32 = jnp.float32
BF16 = jnp.bfloat16

D_MODEL = 1024
ATTN_WIDTH = 512
N_HEADS = 8
HEAD_DIM = 64
DILATIONS = (1, 4, 16)
BLOCK = 128
POOL_WIDTH = 512
POOL_WINDOWS = (2, 4, 8, 16)
POOL_GROUP_DIM = 128
D_FF = 2816
EPS = 1e-6
NEG_INF = -1e30
SCALE = HEAD_DIM ** -0.5

ADAM_LR = 0.001
ADAM_B1 = 0.9
ADAM_B2 = 0.999
ADAM_EPS = 1e-08
ADAM_WD = 0.01
ADAM_STEP = 10

N_DEV = 8
HALO = 16
V7X_VMEM_LIMIT = 56 * 1024 * 1024

MESH = pl.DeviceIdType.MESH
ANY = pl.BlockSpec(memory_space=pl.ANY)
VMEM = pl.BlockSpec(memory_space=pltpu.VMEM)

NT = (((1,), (1,)), ((), ()))
NN = (((1,), (0,)), ((), ()))
TN = (((0,), (0,)), ((), ()))


def _cp(*sem):
    return pltpu.CompilerParams(dimension_semantics=sem, vmem_limit_bytes=V7X_VMEM_LIMIT)


def _dot(a, b, dn):
    return lax.dot_general(a, b, dn, preferred_element_type=F32)


def _rms_bwd(xin, g, dy):
    r = lax.rsqrt(jnp.mean(xin * xin, axis=-1, keepdims=True) + EPS)
    xh = xin * r
    gdy = g * dy
    dx = r * (gdy - xh * jnp.mean(gdy * xh, axis=-1, keepdims=True))
    dg = jnp.sum(dy * xh, axis=0, keepdims=True)
    return dx, dg


def _rms_norm(x, g, *, name, tm=512):
    S, D = x.shape

    def body(x_ref, g_ref, o_ref):
        xv = x_ref[...]
        r = lax.rsqrt(jnp.mean(xv * xv, axis=-1, keepdims=True) + EPS)
        o_ref[...] = (xv * r * g_ref[...]).astype(BF16)

    return pl.pallas_call(
        body, name=name, grid=(S // tm,),
        in_specs=[pl.BlockSpec((tm, D), lambda i: (i, 0)), pl.BlockSpec((1, D), lambda i: (0, 0))],
        out_specs=pl.BlockSpec((tm, D), lambda i: (i, 0)),
        out_shape=jax.ShapeDtypeStruct((S, D), BF16),
        compiler_params=_cp("parallel"),
    )(x, g)


def _matmul(a, b, *, trans_b, out_dtype, tm, tn, name):
    M, K = a.shape
    N = b.shape[0] if trans_b else b.shape[1]
    dn = NT if trans_b else NN

    def body(a_ref, b_ref, o_ref):
        o_ref[...] = _dot(a_ref[...], b_ref[...], dn).astype(out_dtype)

    b_spec = (pl.BlockSpec((tn, K), lambda i, j: (j, 0)) if trans_b
              else pl.BlockSpec((K, tn), lambda i, j: (0, j)))
    return pl.pallas_call(
        body, name=name, grid=(M // tm, N // tn),
        in_specs=[pl.BlockSpec((tm, K), lambda i, j: (i, 0)), b_spec],
        out_specs=pl.BlockSpec((tm, tn), lambda i, j: (i, j)),
        out_shape=jax.ShapeDtypeStruct((M, N), out_dtype),
        compiler_params=_cp("parallel", "parallel"),
    )(a, b)


def _matmul_tn(a, b, *, ta, ts, name):
    S, Ka = a.shape
    Nb = b.shape[1]
    ns = S // ts

    def body(a_ref, b_ref, o_ref, acc):
        s = pl.program_id(1)

        @pl.when(s == 0)
        def _():
            acc[...] = jnp.zeros_like(acc)

        acc[...] += _dot(a_ref[...], b_ref[...], TN)

        @pl.when(s == ns - 1)
        def _():
            o_ref[...] = acc[...].astype(BF16)

    return pl.pallas_call(
        body, name=name, grid=(Ka // ta, ns),
        in_specs=[pl.BlockSpec((ts, ta), lambda i, s: (s, i)), pl.BlockSpec((ts, Nb), lambda i, s: (s, 0))],
        out_specs=pl.BlockSpec((ta, Nb), lambda i, s: (i, 0)),
        out_shape=jax.ShapeDtypeStruct((Ka, Nb), BF16),
        scratch_shapes=[pltpu.VMEM((ta, Nb), F32)],
        compiler_params=_cp("parallel", "arbitrary"),
    )(a, b)


def _mix_out(attn, pool, w_out, x, g_post, g_next, *, name, tm=256):
    S, K = attn.shape
    D = w_out.shape[1]

    def body(a_ref, p_ref, w_ref, x_ref, gp_ref, gn_ref, mixed_ref, x2_ref, h2_ref):
        mixed = _dot(a_ref[...], w_ref[:K, :], NN) + _dot(p_ref[...], w_ref[K:, :], NN)
        r = lax.rsqrt(jnp.mean(mixed * mixed, axis=-1, keepdims=True) + EPS)
        x2 = x_ref[...] + mixed * r * gp_ref[...]
        r2 = lax.rsqrt(jnp.mean(x2 * x2, axis=-1, keepdims=True) + EPS)
        mixed_ref[...] = mixed
        x2_ref[...] = x2
        h2_ref[...] = (x2 * r2 * gn_ref[...]).astype(BF16)

    row = lambda i: (i, 0)
    fix = lambda i: (0, 0)
    return pl.pallas_call(
        body, name=name, grid=(S // tm,),
        in_specs=[pl.BlockSpec((tm, K), row), pl.BlockSpec((tm, K), row), pl.BlockSpec((2 * K, D), fix),
                  pl.BlockSpec((tm, D), row), pl.BlockSpec((1, D), fix), pl.BlockSpec((1, D), fix)],
        out_specs=[pl.BlockSpec((tm, D), row)] * 3,
        out_shape=[jax.ShapeDtypeStruct((S, D), F32), jax.ShapeDtypeStruct((S, D), F32),
                   jax.ShapeDtypeStruct((S, D), BF16)],
        compiler_params=_cp("parallel"),
    )(attn, pool, w_out, x, g_post, g_next)


def _ffn_out(y, w_down, x2, target, g_post, *, name, tm=256):
    S, K = y.shape
    D = w_down.shape[1]

    def body(y_ref, w_ref, x2_ref, t_ref, g_ref, df_ref, dout_ref, loss_ref, gg_ref):
        i = pl.program_id(0)

        @pl.when(i == 0)
        def _():
            loss_ref[...] = jnp.zeros_like(loss_ref)
            gg_ref[...] = jnp.zeros_like(gg_ref)

        f = _dot(y_ref[...], w_ref[...], NN)
        g = g_ref[...]
        r = lax.rsqrt(jnp.mean(f * f, axis=-1, keepdims=True) + EPS)
        out = x2_ref[...] + f * r * g
        err = out - t_ref[...]
        dy = err * (1.0 / D)
        df, dg = _rms_bwd(f, g, dy)
        df_ref[...] = df.astype(BF16)
        dout_ref[...] = dy
        gg_ref[...] += dg
        loss_ref[...] += 0.5 * jnp.sum(jnp.mean(err * err, axis=-1, keepdims=True))

    row = lambda i: (i, 0)
    fix = lambda i: (0, 0)
    return pl.pallas_call(
        body, name=name, grid=(S // tm,),
        in_specs=[pl.BlockSpec((tm, K), row), pl.BlockSpec((K, D), fix), pl.BlockSpec((tm, D), row),
                  pl.BlockSpec((tm, D), row), pl.BlockSpec((1, D), fix)],
        out_specs=[pl.BlockSpec((tm, D), row), pl.BlockSpec((tm, D), row),
                   pl.BlockSpec((8, 128), fix), pl.BlockSpec((1, D), fix)],
        out_shape=[jax.ShapeDtypeStruct((S, D), BF16), jax.ShapeDtypeStruct((S, D), F32),
                   jax.ShapeDtypeStruct((8, 128), F32), jax.ShapeDtypeStruct((1, D), F32)],
        compiler_params=_cp("arbitrary"),
    )(y, w_down, x2, target, g_post)


def _dgrad_norm(a_list, w, resid, xin, g, second, exchange, *, tk, name, tm=512):
    S, Kp = a_list[0].shape
    na = len(a_list)
    D = w.shape[1]
    kper = Kp // tk
    nk = na * kper
    nt = S // tm
    two = second is not None
    ng = len(exchange)
    recv_shapes, exchange_sems = _exchange_buffers(exchange)

    def body(*refs):
        a_refs = refs[:na]
        w_ref, r_ref, x_ref, g_ref = refs[na:na + 4]
        pos = na + 4
        if two:
            x2_ref, g2_ref = refs[pos:pos + 2]
            pos += 2
        g_refs = refs[pos:pos + ng]
        pos += ng
        dx_ref, gg_ref = refs[pos:pos + 2]
        pos += 2
        if two:
            d2_ref, gg2_ref = refs[pos:pos + 2]
            pos += 2
        r_refs = refs[pos:pos + ng]
        pos += ng
        acc = refs[pos]
        i = pl.program_id(0)
        k = pl.program_id(1)
        if ng:
            start, finish = _exchange_phases(g_refs, r_refs, *refs[pos + 1:])
            pl.when((i == 0) & (k == 0))(start)

        @pl.when(k == 0)
        def _():
            acc[...] = jnp.zeros_like(acc)

        @pl.when((i == 0) & (k == 0))
        def _():
            gg_ref[...] = jnp.zeros_like(gg_ref)
            if two:
                gg2_ref[...] = jnp.zeros_like(gg2_ref)

        for q in range(na):
            @pl.when(k // kper == q)
            def _(q=q):
                acc[...] += _dot(a_refs[q][...], w_ref[...], NN)

        @pl.when(k == nk - 1)
        def _():
            d1, dg1 = _rms_bwd(x_ref[...], g_ref[...], acc[...])
            dx = r_ref[...] + d1
            dx_ref[...] = dx
            gg_ref[...] += dg1
            if two:
                d2, dg2 = _rms_bwd(x2_ref[...], g2_ref[...], dx)
                d2_ref[...] = d2.astype(BF16)
                gg2_ref[...] += dg2

        if ng:
            pl.when((i == nt - 1) & (k == nk - 1))(finish)

    row = lambda i, k: (i, 0)
    fix = lambda i, k: (0, 0)
    a_specs = [pl.BlockSpec((tm, tk), functools.partial(
        lambda i, k, q: (i, jnp.clip(k - q * kper, 0, kper - 1)), q=q)) for q in range(na)]
    in_specs = a_specs + [pl.BlockSpec((tk, D), lambda i, k: (k, 0)), pl.BlockSpec((tm, D), row),
                          pl.BlockSpec((tm, D), row), pl.BlockSpec((1, D), fix)]
    args = list(a_list) + [w, resid, xin, g]
    out_specs = [pl.BlockSpec((tm, D), row), pl.BlockSpec((1, D), fix)]
    out_shape = [jax.ShapeDtypeStruct((S, D), F32), jax.ShapeDtypeStruct((1, D), F32)]
    if two:
        in_specs += [pl.BlockSpec((tm, D), row), pl.BlockSpec((1, D), fix)]
        args += list(second)
        out_specs += [pl.BlockSpec((tm, D), row), pl.BlockSpec((1, D), fix)]
        out_shape += [jax.ShapeDtypeStruct((S, D), BF16), jax.ShapeDtypeStruct((1, D), F32)]
    n_plain = len(out_shape)
    out = pl.pallas_call(
        body, name=name, grid=(nt, nk), in_specs=in_specs + [ANY] * ng, out_specs=out_specs + [ANY] * ng,
        out_shape=out_shape + recv_shapes,
        scratch_shapes=[pltpu.VMEM((tm, D), F32)] + (exchange_sems if ng else []),
        compiler_params=_cp("arbitrary", "arbitrary"),
    )(*args, *exchange)
    return (*out[:n_plain], out[n_plain:]) if ng else out


def _band_mask(first_block):
    qi = lax.broadcasted_iota(jnp.int32, (BLOCK, 2 * BLOCK), 0)
    ki = lax.broadcasted_iota(jnp.int32, (BLOCK, 2 * BLOCK), 1)
    first_key = jnp.where(first_block, BLOCK, 0)
    return (ki >= qi) & (ki <= qi + BLOCK) & (ki >= first_key)


def _lane_masks():
    lane = lax.broadcasted_iota(jnp.int32, (1, 2 * HEAD_DIM), 1)
    return (lane < HEAD_DIM, lane >= HEAD_DIM)


CHUNK = BLOCK * max(DILATIONS)
SLAB = 2 * HEAD_DIM
N_SLABS = ATTN_WIDTH // SLAB


def _unit_rows(d, b):
    def rows(r):
        start = r + BLOCK * d * b
        return pl.ds(start, BLOCK, stride=d) if d > 1 else pl.ds(start, BLOCK)
    return rows


def _attn_units():
    for p, d in enumerate(DILATIONS):
        nbc = CHUNK // (BLOCK * d)
        for b in range(nbc):
            for r in range(d):
                yield p, d, b, r, nbc


def _attn_in_specs(nc, n_cur):
    prev = lambda c: jnp.maximum(jnp.minimum(c, nc - 1) - 1, 0)
    cur = lambda c: jnp.minimum(c, nc - 1)
    blk = lambda f: pl.BlockSpec((CHUNK, SLAB), f)
    specs = [blk(lambda h, c: (cur(c), h)),
             blk(lambda h, c: (prev(c), N_SLABS + h)), blk(lambda h, c: (cur(c), N_SLABS + h)),
             blk(lambda h, c: (prev(c), 2 * N_SLABS + h)), blk(lambda h, c: (cur(c), 2 * N_SLABS + h))]
    return specs + [blk(lambda h, c: (cur(c), h))] * n_cur


def _attn_fwd(proj, payload, *, name):
    S = proj.shape[0]
    nc = S // CHUNK
    n = len(DILATIONS)
    n_steps = N_SLABS * nc

    def body(q_ref, kp_ref, kc_ref, vp_ref, vc_ref, pay_ref, attn_ref, lse_ref, attn16_ref, all_ref, *scr):
        o_scr, l_scr = scr[:n], scr[n:2 * n]
        start, forward, finish = _gather_phases(pay_ref, all_ref, *scr[2 * n:])
        step = pl.program_id(0) * nc + pl.program_id(1)
        pl.when(step == 0)(start)
        c = pl.program_id(1)
        lms = _lane_masks()
        plain, first = _band_mask(False), _band_mask(c == 0)
        for p, d, b, r, nbc in _attn_units():
            rows = _unit_rows(d, b)(r)
            prow = _unit_rows(d, b - 1 if b > 0 else nbc - 1)(r)
            kpr, vpr = (kc_ref, vc_ref) if b > 0 else (kp_ref, vp_ref)
            mask = plain if b > 0 else first
            q = q_ref[rows, :].astype(BF16)
            kcat = jnp.concatenate([kpr[prow, :], kc_ref[rows, :]], axis=0).astype(BF16)
            vcat = jnp.concatenate([vpr[prow, :], vc_ref[rows, :]], axis=0).astype(BF16)
            o_slab = jnp.zeros((BLOCK, SLAB), F32)
            lse_slab = jnp.zeros((BLOCK, SLAB), F32)
            for lm in lms:
                qs = jnp.where(lm, q, jnp.zeros_like(q)) * SCALE
                vh = jnp.where(lm, vcat, jnp.zeros_like(vcat))
                s = jnp.where(mask, _dot(qs, kcat, NT), NEG_INF)
                m = jnp.max(s, axis=-1, keepdims=True)
                e = jnp.exp(s - m)
                l = jnp.sum(e, axis=-1, keepdims=True)
                o_slab = o_slab + _dot(e.astype(BF16), vh, NN) / l
                lse_slab = jnp.where(lm, m + jnp.log(l), lse_slab)
            o_scr[p][rows, :] = o_slab
            l_scr[p][rows, :] = lse_slab
        ls = [l_scr[p][...] for p in range(n)]
        top = functools.reduce(jnp.maximum, ls)
        es = [jnp.exp(l - top) for l in ls]
        den = functools.reduce(jnp.add, es)
        num = functools.reduce(jnp.add, [e * o_scr[p][...] for p, e in enumerate(es)])
        attn = num / den
        attn_ref[...] = attn
        attn16_ref[...] = attn.astype(BF16)
        lse_ref[...] = top + jnp.log(den)
        pl.when(step == (2 * n_steps) // 3)(forward)
        pl.when(step == n_steps - 1)(finish)

    return pl.pallas_call(
        body, name=name, grid=(N_SLABS, nc), in_specs=_attn_in_specs(nc, 0) + [ANY],
        out_specs=[pl.BlockSpec((CHUNK, SLAB), lambda h, c: (c, h))] * 3 + [ANY],
        out_shape=[jax.ShapeDtypeStruct((S, ATTN_WIDTH), F32)] * 2 + [jax.ShapeDtypeStruct((S, ATTN_WIDTH), BF16),
                   jax.ShapeDtypeStruct((N_DEV,) + payload.shape, payload.dtype)],
        scratch_shapes=[pltpu.VMEM((CHUNK, SLAB), F32)] * (2 * n) + _GATHER_SEMS,
        compiler_params=_cp("arbitrary", "arbitrary"),
    )(proj, proj, proj, proj, proj, payload)


def _attn_bwd(proj, dcat, attn, lse, grads, *, name):
    S = proj.shape[0]
    nc = S // CHUNK
    ng = len(grads)
    recv_shapes, exchange_sems = _exchange_buffers(grads)

    def body(*refs):
        q_ref, kp_ref, kc_ref, vp_ref, vc_ref, do_ref, o_ref, lse_ref = refs[:8]
        g_refs = refs[8:8 + ng]
        dq_ref, dk_ref, dv_ref = refs[8 + ng:11 + ng]
        r_refs = refs[11 + ng:11 + 2 * ng]
        dq_acc, dk_prev, dk_cur, dv_prev, dv_cur, delta_scr = refs[11 + 2 * ng:17 + 2 * ng]
        start, finish = _exchange_phases(g_refs, r_refs, *refs[17 + 2 * ng:])
        c = pl.program_id(1)
        pl.when((pl.program_id(0) == 0) & (c == 0))(start)

        @pl.when(c == 0)
        def _():
            dk_prev[...] = jnp.zeros_like(dk_prev)
            dv_prev[...] = jnp.zeros_like(dv_prev)

        @pl.when(c < nc)
        def _():
            lms = _lane_masks()
            plain, first = _band_mask(False), _band_mask(c == 0)
            for acc in (dq_acc, dk_cur, dv_cur):
                acc[...] = jnp.zeros_like(acc)
            prod = do_ref[...] * o_ref[...]
            delta_a, delta_b = (jnp.sum(jnp.where(lm, prod, 0.0), axis=-1, keepdims=True) for lm in lms)
            delta_scr[...] = jnp.where(lms[0], delta_a, delta_b)
            for p, d, b, r, nbc in _attn_units():
                rows = _unit_rows(d, b)(r)
                prow = _unit_rows(d, b - 1 if b > 0 else nbc - 1)(r)
                kpr, vpr = (kc_ref, vc_ref) if b > 0 else (kp_ref, vp_ref)
                dkp, dvp = (dk_cur, dv_cur) if b > 0 else (dk_prev, dv_prev)
                mask = plain if b > 0 else first
                q = q_ref[rows, :].astype(BF16)
                kcat = jnp.concatenate([kpr[prow, :], kc_ref[rows, :]], axis=0).astype(BF16)
                vcat = jnp.concatenate([vpr[prow, :], vc_ref[rows, :]], axis=0).astype(BF16)
                do = do_ref[rows, :]
                lse_sl = lse_ref[rows, :]
                delta_sl = delta_scr[rows, :]
                dq = jnp.zeros((BLOCK, SLAB), F32)
                dkc = jnp.zeros((2 * BLOCK, SLAB), F32)
                dvc = jnp.zeros((2 * BLOCK, SLAB), F32)
                for lm in lms:
                    qs = jnp.where(lm, q, jnp.zeros_like(q)) * SCALE
                    kh = jnp.where(lm, kcat, jnp.zeros_like(kcat))
                    doh = jnp.where(lm, do, 0.0).astype(BF16)
                    delta = jnp.max(jnp.where(lm, delta_sl, -jnp.inf), axis=-1, keepdims=True)
                    lse_h = jnp.max(jnp.where(lm, lse_sl, -jnp.inf), axis=-1, keepdims=True)
                    s = _dot(qs, kcat, NT)
                    e = jnp.where(mask, jnp.exp(s - lse_h), 0.0)
                    dp = _dot(doh, vcat, NT)
                    ds = (e * (dp - delta)).astype(BF16)
                    dq = dq + _dot(ds, kh, NN) * SCALE
                    dkc = dkc + _dot(ds, qs, TN)
                    dvc = dvc + _dot(e.astype(BF16), doh, TN)
                dq_acc[rows, :] += dq
                dkp[prow, :] += dkc[:BLOCK]
                dvp[prow, :] += dvc[:BLOCK]
                dk_cur[rows, :] += dkc[BLOCK:]
                dv_cur[rows, :] += dvc[BLOCK:]
            dq_ref[...] = dq_acc[...].astype(BF16)
            dk_ref[...] = dk_prev[...].astype(BF16)
            dv_ref[...] = dv_prev[...].astype(BF16)
            dk_prev[...] = dk_cur[...]
            dv_prev[...] = dv_cur[...]

        @pl.when(c == nc)
        def _():
            dk_ref[...] = dk_prev[...].astype(BF16)
            dv_ref[...] = dv_prev[...].astype(BF16)

        pl.when((pl.program_id(0) == N_SLABS - 1) & (c == nc))(finish)

    blk = lambda f: pl.BlockSpec((CHUNK, SLAB), f)
    late = lambda h, c: (jnp.maximum(c - 1, 0), h)
    out = pl.pallas_call(
        body, name=name, grid=(N_SLABS, nc + 1), in_specs=_attn_in_specs(nc, 3) + [ANY] * ng,
        out_specs=[blk(lambda h, c: (jnp.minimum(c, nc - 1), h)), blk(late), blk(late)] + [ANY] * ng,
        out_shape=[jax.ShapeDtypeStruct((S, ATTN_WIDTH), BF16)] * 3 + recv_shapes,
        scratch_shapes=[pltpu.VMEM((CHUNK, SLAB), F32)] * 6 + exchange_sems,
        compiler_params=_cp("arbitrary", "arbitrary"),
    )(proj, proj, proj, proj, proj, dcat, attn, lse, *grads)
    return out[:3], out[3:]


def _split_bf16(a):
    hi = a.astype(BF16)
    lo = (a - hi.astype(F32)).astype(BF16)
    return hi, lo


def _pooled(ug, halo_g, w, row0, tm):
    ext = jnp.concatenate([halo_g, ug], axis=0)
    hi, lo = _split_bf16(ext)
    rr = lax.broadcasted_iota(jnp.int32, (tm, tm + HALO), 0)
    cc = lax.broadcasted_iota(jnp.int32, (tm, tm + HALO), 1)
    back = rr + HALO - cc
    win = ((back >= 0) & (back < w)).astype(BF16)
    wsum = _dot(win, hi, NN) + _dot(win, lo, NN)
    rows = row0 + lax.broadcasted_iota(jnp.int32, (tm, 1), 0)
    inv = 1.0 / jnp.minimum(rows + 1, w).astype(F32)
    return wsum * inv - ug


def _pool_fwd(u, u_col, pool_w, pool_scale, *, name, tm=256):
    S, W = u.shape[0], POOL_WIDTH
    G = POOL_GROUP_DIM

    def body(u_ref, h_ref, w_ref, s_ref, o_ref):
        i = pl.program_id(0)
        uv = u_ref[...]
        halo = jnp.where(i > 0, h_ref[...], 0.0)
        for g, w in enumerate(POOL_WINDOWS):
            sl = slice(g * G, (g + 1) * G)
            pooled = _pooled(uv[:, sl], halo[:, sl], w, i * tm, tm)
            z = _dot(pooled.astype(BF16), w_ref[g].astype(BF16), NN)
            o_ref[:, sl] = (z * s_ref[:, sl]).astype(BF16)

    per = tm // HALO
    return pl.pallas_call(
        body, name=name, grid=(S // tm,),
        in_specs=[pl.BlockSpec((tm, W), lambda i: (i, u_col)),
                  pl.BlockSpec((HALO, W), lambda i: (jnp.maximum(i * per - 1, 0), u_col)),
                  pl.BlockSpec((len(POOL_WINDOWS), G, G), lambda i: (0, 0, 0)),
                  pl.BlockSpec((1, W), lambda i: (0, 0))],
        out_specs=pl.BlockSpec((tm, W), lambda i: (i, 0)),
        out_shape=jax.ShapeDtypeStruct((S, W), BF16),
        compiler_params=_cp("parallel"),
    )(u, u, pool_w, pool_scale)


def _pool_bwd(u, u_col, dy, dy_col, pool_w, pool_scale, *, name, tm=256):
    S, W = u.shape[0], POOL_WIDTH
    G = POOL_GROUP_DIM
    nt = S // tm

    def body(u_ref, h_ref, dy_ref, dyn_ref, w_ref, s_ref, du_ref, gw_ref, gs_ref):
        i = pl.program_id(0)

        @pl.when(i == 0)
        def _():
            gw_ref[...] = jnp.zeros_like(gw_ref)
            gs_ref[...] = jnp.zeros_like(gs_ref)

        uv = u_ref[...]
        halo = jnp.where(i > 0, h_ref[...], 0.0)
        dyv = dy_ref[...]
        dyn = jnp.where(i < nt - 1, dyn_ref[...], 0.0)
        rr = lax.broadcasted_iota(jnp.int32, (tm, tm + HALO), 0)
        cc = lax.broadcasted_iota(jnp.int32, (tm, tm + HALO), 1)
        rows_ext = i * tm + lax.broadcasted_iota(jnp.int32, (tm + HALO, 1), 0)
        for g, w in enumerate(POOL_WINDOWS):
            sl = slice(g * G, (g + 1) * G)
            wg = w_ref[g].astype(BF16)
            sc = s_ref[:, sl]
            pooled = _pooled(uv[:, sl], halo[:, sl], w, i * tm, tm)
            z = _dot(pooled.astype(BF16), wg, NN)
            gs_ref[:, sl] += jnp.sum(dyv[:, sl] * z, axis=0, keepdims=True)
            dz = dyv[:, sl] * sc
            gw_ref[g] += _dot(pooled.astype(BF16), dz.astype(BF16), TN)
            dz_ext = jnp.concatenate([dz, dyn[:, sl] * sc], axis=0)
            dp_ext = _dot(dz_ext.astype(BF16), wg, NT)
            inv_ext = 1.0 / jnp.minimum(rows_ext + 1, w).astype(F32)
            hi, lo = _split_bf16(dp_ext * inv_ext)
            ahead = cc - rr
            win = ((ahead >= 0) & (ahead < w)).astype(BF16)
            du_ref[:, sl] = (_dot(win, hi, NN) + _dot(win, lo, NN) - dp_ext[:tm]).astype(BF16)

    per = tm // HALO
    nh = S // HALO
    return pl.pallas_call(
        body, name=name, grid=(nt,),
        in_specs=[pl.BlockSpec((tm, W), lambda i: (i, u_col)),
                  pl.BlockSpec((HALO, W), lambda i: (jnp.maximum(i * per - 1, 0), u_col)),
                  pl.BlockSpec((tm, W), lambda i: (i, dy_col)),
                  pl.BlockSpec((HALO, W), lambda i: (jnp.minimum((i + 1) * per, nh - 1), dy_col)),
                  pl.BlockSpec((len(POOL_WINDOWS), G, G), lambda i: (0, 0, 0)),
                  pl.BlockSpec((1, W), lambda i: (0, 0))],
        out_specs=[pl.BlockSpec((tm, W), lambda i: (i, 0)),
                   pl.BlockSpec((len(POOL_WINDOWS), G, G), lambda i: (0, 0, 0)),
                   pl.BlockSpec((1, W), lambda i: (0, 0))],
        out_shape=[jax.ShapeDtypeStruct((S, W), BF16),
                   jax.ShapeDtypeStruct((len(POOL_WINDOWS), G, G), F32),
                   jax.ShapeDtypeStruct((1, W), F32)],
        compiler_params=_cp("arbitrary"),
    )(u, u, dy, dy, pool_w, pool_scale)


GELU_K0 = math.sqrt(2.0 / math.pi)
GELU_K1 = 0.044715


def _gelu_parts(x):
    t = jnp.tanh(GELU_K0 * (x + GELU_K1 * x * x * x))
    gelu = 0.5 * x * (1.0 + t)
    dgelu = 0.5 * (1.0 + t) + 0.5 * x * (1.0 - t * t) * (GELU_K0 * (1.0 + 3.0 * GELU_K1 * x * x))
    return gelu, dgelu


def _shifted(ext, halo):
    return (pltpu.roll(ext, 2, 0)[halo:], pltpu.roll(ext, 1, 0)[halo:], ext[halo:])


def _conv(sh, w, b):
    return b + (sh[0] * w[0:1] + sh[1] * w[1:2] + sh[2] * w[2:3])


F32_ROWS = 8


def _ffn_up_glu(h, w_up_t, conv_w, conv_b, *, name, tm=1024, tn=256):
    S, K = h.shape
    F = D_FF
    nj = F // tn

    def body(h_ref, wg_ref, wv_ref, cwg_ref, cwv_ref, cbg_ref, cbv_ref, ug_ref, uv_ref, y_ref, carry):
        i = pl.program_id(0)
        j = pl.program_id(1)

        def half(w_ref, u_ref, cw_ref, cb_ref, slot):
            u16 = _dot(h_ref[...], w_ref[...], NT).astype(BF16)
            u_ref[...] = u16
            u = u16.astype(F32)
            halo = jnp.where(i > 0, carry[j, slot], 0.0)
            carry[j, slot] = u[tm - F32_ROWS:]
            ext = jnp.concatenate([halo, u], axis=0)
            return _conv(_shifted(ext, F32_ROWS), cw_ref[...], cb_ref[...])

        gelu, _ = _gelu_parts(half(wg_ref, ug_ref, cwg_ref, cbg_ref, 0))
        y_ref[...] = (gelu * half(wv_ref, uv_ref, cwv_ref, cbv_ref, 1)).astype(BF16)

    tile = pl.BlockSpec((tm, tn), lambda i, j: (i, j))
    vec = lambda rows, off: pl.BlockSpec((rows, tn), lambda i, j: (0, j + off))
    return pl.pallas_call(
        body, name=name, grid=(S // tm, nj),
        in_specs=[pl.BlockSpec((tm, K), lambda i, j: (i, 0)),
                  pl.BlockSpec((tn, K), lambda i, j: (j, 0)), pl.BlockSpec((tn, K), lambda i, j: (j + nj, 0)),
                  vec(3, 0), vec(3, nj), vec(1, 0), vec(1, nj)],
        out_specs=[tile, tile, tile],
        out_shape=[jax.ShapeDtypeStruct((S, F), BF16)] * 3,
        scratch_shapes=[pltpu.VMEM((nj, 2, F32_ROWS, tn), F32)],
        compiler_params=_cp("arbitrary", "arbitrary"),
    )(h, w_up_t, w_up_t, conv_w, conv_w, conv_b, conv_b)


def _ffn_glu_bwd(u_g, u_v, df, w_down, h, conv_w, conv_b, *, name, tm=512, tn=256):
    S = u_g.shape[0]
    F = D_FF
    D = df.shape[1]
    nj = F // tn
    nt = S // tm
    n_ext = tm + HALO

    def body(ug_ref, hgp_ref, hgn_ref, uv_ref, hvp_ref, hvn_ref, df_ref, dfn_ref, wd_ref, h_ref,
             wg_ref, wv_ref, bg_ref, bv_ref,
             dug_ref, duv_ref, gug_ref, guv_ref, gd_ref, gwg_ref, gwv_ref, gbg_ref, gbv_ref,
             acc_ug, acc_uv, acc_d):
        i = pl.program_id(1)

        @pl.when(i == 0)
        def _():
            for r in (gwg_ref, gwv_ref, gbg_ref, gbv_ref, acc_ug, acc_uv, acc_d):
                r[...] = jnp.zeros_like(r)

        def shifted(u_ref, hp_ref, hn_ref):
            halo = jnp.where(i > 0, hp_ref[...].astype(F32), 0.0)
            ext = jnp.concatenate([halo, u_ref[...].astype(F32), hn_ref[...].astype(F32)], axis=0)
            return _shifted(ext, HALO)

        sh_g = shifted(ug_ref, hgp_ref, hgn_ref)
        sh_v = shifted(uv_ref, hvp_ref, hvn_ref)
        wg, wv = wg_ref[...], wv_ref[...]
        cg = _conv(sh_g, wg, bg_ref[...])
        cv = _conv(sh_v, wv, bv_ref[...])
        dfn = jnp.where(i < nt - 1, dfn_ref[...], jnp.zeros_like(dfn_ref))
        df_tile = df_ref[...]
        dy_ext = _dot(jnp.concatenate([df_tile, dfn], axis=0), wd_ref[...], NT)
        gelu, dgelu = _gelu_parts(cg)
        dcg = dy_ext * cv * dgelu
        dcv = dy_ext * gelu

        def back(dc, w):
            return (dc[:tm] * w[2:3] + pltpu.roll(dc, n_ext - 1, 0)[:tm] * w[1:2]
                    + pltpu.roll(dc, n_ext - 2, 0)[:tm] * w[0:1])

        du_g = back(dcg, wg).astype(BF16)
        du_v = back(dcv, wv).astype(BF16)
        dug_ref[...] = du_g
        duv_ref[...] = du_v
        h_tile = h_ref[...]
        acc_ug[...] += _dot(du_g, h_tile, TN)
        acc_uv[...] += _dot(du_v, h_tile, TN)
        acc_d[...] += _dot((gelu[:tm] * cv[:tm]).astype(BF16), df_tile, TN)
        for dc, sh, gw_ref, gb_ref in ((dcg, sh_g, gwg_ref, gbg_ref), (dcv, sh_v, gwv_ref, gbv_ref)):
            dct = dc[:tm]
            gb_ref[...] += jnp.sum(dct, axis=0, keepdims=True)
            for t in range(3):
                gw_ref[t:t + 1, :] += jnp.sum(dct * sh[t][:tm], axis=0, keepdims=True)

        @pl.when(i == nt - 1)
        def _():
            gug_ref[...] = acc_ug[...].astype(BF16)
            guv_ref[...] = acc_uv[...].astype(BF16)
            gd_ref[...] = acc_d[...].astype(BF16)

    per = tm // HALO
    nh = S // HALO
    hprev = lambda i: jnp.maximum(i * per - 1, 0)
    hnext = lambda i: jnp.minimum((i + 1) * per, nh - 1)
    tile = pl.BlockSpec((tm, tn), lambda j, i: (i, j))
    hp = pl.BlockSpec((HALO, tn), lambda j, i: (hprev(i), j))
    hn = pl.BlockSpec((HALO, tn), lambda j, i: (hnext(i), j))
    vec = lambda rows, off: pl.BlockSpec((rows, tn), lambda j, i: (0, j + off))
    wide = pl.BlockSpec((tm, D), lambda j, i: (i, 0))
    wrow = pl.BlockSpec((tn, D), lambda j, i: (j, 0))
    return pl.pallas_call(
        body, name=name, grid=(nj, nt),
        in_specs=[tile, hp, hn, tile, hp, hn, wide, pl.BlockSpec((HALO, D), lambda j, i: (hnext(i), 0)),
                  wrow, wide, vec(3, 0), vec(3, nj), vec(1, 0), vec(1, nj)],
        out_specs=[tile, tile, wrow, wrow, wrow, vec(3, 0), vec(3, 0), vec(1, 0), vec(1, 0)],
        out_shape=[jax.ShapeDtypeStruct((S, F), BF16), jax.ShapeDtypeStruct((S, F), BF16),
                   jax.ShapeDtypeStruct((F, D), BF16), jax.ShapeDtypeStruct((F, D), BF16),
                   jax.ShapeDtypeStruct((F, D), BF16),
                   jax.ShapeDtypeStruct((3, F), F32), jax.ShapeDtypeStruct((3, F), F32),
                   jax.ShapeDtypeStruct((1, F), F32), jax.ShapeDtypeStruct((1, F), F32)],
        scratch_shapes=[pltpu.VMEM((tn, D), F32)] * 3,
        compiler_params=_cp("parallel", "arbitrary"),
    )(u_g, u_g, u_g, u_v, u_v, u_v, df, df, w_down, h, conv_w, conv_w, conv_b, conv_b)


def _sum_partials(parts, *, name, tr):
    _, R, C = parts.shape

    def body(p_ref, o_ref):
        tot = p_ref[0].astype(F32)
        for j in range(1, N_DEV):
            tot = tot + p_ref[j].astype(F32)
        o_ref[...] = tot

    return pl.pallas_call(
        body, name=name, grid=(R // tr,),
        in_specs=[pl.BlockSpec((N_DEV, tr, C), lambda i: (0, i, 0))],
        out_specs=pl.BlockSpec((tr, C), lambda i: (i, 0)),
        out_shape=jax.ShapeDtypeStruct((R, C), F32),
        compiler_params=_cp("parallel"),
    )(parts)


def _adamw(w, g, m, v, *, name, tr):
    R, C = w.shape
    c1 = 1.0 - ADAM_B1 ** ADAM_STEP
    c2 = 1.0 - ADAM_B2 ** ADAM_STEP

    def body(w_ref, g_ref, m_ref, v_ref, d_ref, nm_ref, nv_ref):
        g = g_ref[...]
        nm = ADAM_B1 * m_ref[...] + (1.0 - ADAM_B1) * g
        nv = ADAM_B2 * v_ref[...] + (1.0 - ADAM_B2) * (g * g)
        d_ref[...] = -ADAM_LR * ((nm / c1) / (jnp.sqrt(nv / c2) + ADAM_EPS) + ADAM_WD * w_ref[...])
        nm_ref[...] = nm
        nv_ref[...] = nv

    spec = pl.BlockSpec((tr, C), lambda i: (i, 0))
    return pl.pallas_call(
        body, name=name, grid=(R // tr,), in_specs=[spec] * 4, out_specs=[spec] * 3,
        out_shape=[jax.ShapeDtypeStruct((R, C), F32)] * 3,
        compiler_params=_cp("parallel"),
    )(w, g, m, v)


def _mesh_pos():
    return lax.axis_index("x"), lax.axis_index("y"), lax.axis_index("c")


def _gather_phases(x_ref, out_ref, send_sems, recv_sems, local_sem):
    x, y, c = _mesh_pos()
    me, sibling = (x, y, c), (x, y, 1 - c)
    chips = [(1 - x, y), (x, 1 - y), (1 - x, 1 - y)]

    def slot(px, py, pc):
        return out_ref.at[4 * px + 2 * py + pc]

    def copy(k, block, to, src=None):
        return pltpu.make_async_remote_copy(
            src_ref=slot(*block) if src is None else src, dst_ref=slot(*block),
            send_sem=send_sems.at[k], recv_sem=recv_sems.at[k], device_id=to, device_id_type=MESH)

    mine = pltpu.make_async_copy(x_ref, slot(*me), local_sem)
    first = [copy(0, me, sibling, src=x_ref)]
    first += [copy(1 + j, me, (*chip, c), src=x_ref) for j, chip in enumerate(chips)]
    passed = [copy(4 + j, (*chip, c), sibling) for j, chip in enumerate(chips)]

    def start():
        mine.start()
        for cp in first:
            cp.start()

    def forward():
        for j, chip in enumerate(chips):
            copy(1 + j, (*chip, c), me).wait_recv()
            passed[j].start()

    def finish():
        copy(0, sibling, me).wait_recv()
        for j, chip in enumerate(chips):
            copy(4 + j, (*chip, 1 - c), me).wait_recv()
        for cp in first + passed:
            cp.wait_send()
        mine.wait()

    return start, forward, finish


def _two_level_gather(x_ref, out_ref, send_sems, recv_sems, local_sem):
    for phase in _gather_phases(x_ref, out_ref, send_sems, recv_sems, local_sem):
        phase()


_GATHER_SEMS = [pltpu.SemaphoreType.DMA((7,)), pltpu.SemaphoreType.DMA((7,)), pltpu.SemaphoreType.DMA]


def _all_gather_hbm(block, *, name):
    def body(x_ref, out_ref, send_sems, recv_sems, local_sem):
        _two_level_gather(x_ref, out_ref, send_sems, recv_sems, local_sem)

    return pl.pallas_call(
        body, name=name, in_specs=[ANY], out_specs=ANY,
        out_shape=jax.ShapeDtypeStruct((N_DEV,) + block.shape, block.dtype),
        scratch_shapes=_GATHER_SEMS,
    )(block)


def _all_reduce_small(block, *, name):
    def body(x_ref, all_ref, sum_ref, send_sems, recv_sems, local_sem):
        _two_level_gather(x_ref, all_ref, send_sems, recv_sems, local_sem)
        tot = all_ref[0]
        for j in range(1, N_DEV):
            tot = tot + all_ref[j]
        sum_ref[...] = tot

    return pl.pallas_call(
        body, name=name, in_specs=[VMEM], out_specs=[VMEM, VMEM],
        out_shape=[jax.ShapeDtypeStruct((N_DEV,) + block.shape, block.dtype),
                   jax.ShapeDtypeStruct(block.shape, block.dtype)],
        scratch_shapes=_GATHER_SEMS,
        compiler_params=pltpu.CompilerParams(vmem_limit_bytes=V7X_VMEM_LIMIT),
    )(block)[1]


def _exchange_phases(g_refs, r_refs, send_sems, recv_sems, local_sems):
    x, y, c = _mesh_pos()
    me = 4 * x + 2 * y + c
    owns, remote = [], []
    for k, (g_ref, r_ref) in enumerate(zip(g_refs, r_refs)):
        rows = g_ref.shape[0] // N_DEV
        owns.append(pltpu.make_async_copy(g_ref.at[pl.ds(me * rows, rows)], r_ref.at[me], local_sems.at[k]))
        for p in range(1, N_DEV):
            px, py, pc = x ^ (p >> 2), y ^ ((p >> 1) & 1), c ^ (p & 1)
            peer = 4 * px + 2 * py + pc
            link = dict(send_sem=send_sems.at[k, p], recv_sem=recv_sems.at[k, p],
                        device_id=(px, py, pc), device_id_type=MESH)
            src = g_ref.at[pl.ds(peer * rows, rows)]
            send = pltpu.make_async_remote_copy(src_ref=src, dst_ref=r_ref.at[me], **link)
            arrival = pltpu.make_async_remote_copy(src_ref=src, dst_ref=r_ref.at[peer], **link)
            remote.append((send, arrival))

    def start():
        for own in owns:
            own.start()
        for send, _ in remote:
            send.start()

    def finish():
        for _, arrival in remote:
            arrival.wait_recv()
        for send, _ in remote:
            send.wait_send()
        for own in owns:
            own.wait()

    return start, finish


def _exchange_buffers(grads):
    n = len(grads)
    shapes = [jax.ShapeDtypeStruct((N_DEV, g.shape[0] // N_DEV, g.shape[1]), g.dtype) for g in grads]
    sems = [pltpu.SemaphoreType.DMA((n, N_DEV)), pltpu.SemaphoreType.DMA((n, N_DEV)),
            pltpu.SemaphoreType.DMA((n,))]
    return shapes, sems


def _unpack_gathered(gathered):
    n_out, n_up, n_down = D_MODEL, 2 * D_FF, D_FF
    offs, row = [], 0
    for n in (n_out, n_up, n_down):
        offs.append((row, row + n // N_DEV))
        row += n // N_DEV
    w_out, w_up_t, w_down = (gathered[:, a:b].reshape(-1, D_MODEL) for a, b in offs)
    n_cw = 3 * (2 * D_FF // N_DEV)
    cw_all = gathered[:, row:].reshape(N_DEV, -1)[:, :2 * n_cw].reshape(N_DEV, n_cw, 2)
    cw_all = lax.bitcast_convert_type(cw_all, F32).reshape(N_DEV, 3, -1)
    conv_w = jnp.transpose(cw_all, (1, 0, 2)).reshape(3, 2 * D_FF)
    return w_out, w_up_t, w_down, conv_w


def _rest_payload(w_out, w_up, w_down, conv_w):
    cw_bits = lax.bitcast_convert_type(conv_w.reshape(-1), BF16).reshape(-1)
    cw_bits = jnp.pad(cw_bits, (0, HALO * D_MODEL - cw_bits.shape[0])).reshape(HALO, D_MODEL)
    return jnp.concatenate([w_out.astype(BF16), w_up.T.astype(BF16), w_down.astype(BF16), cw_bits], axis=0)


def _device_step(x, target, g_mix_pre, w_in_t, rest_payload, pool_w, pool_scale, g_mix_post, g_ffn_pre,
                 conv_b, g_ffn_post):
    h1 = _rms_norm(x, g_mix_pre, name="rms_mix_pre")
    proj = _matmul(h1, w_in_t, trans_b=True, out_dtype=F32, tm=1024, tn=512, name="proj")
    attn, lse, attn16, gathered = _attn_fwd(proj, rest_payload, name="attn_fwd")
    w_out, w_up_t, w_down, conv_w = _unpack_gathered(gathered)
    pool = _pool_fwd(proj, 3, pool_w, pool_scale, name="pool_fwd")
    mixed, x2, h2 = _mix_out(attn16, pool, w_out, x, g_mix_post, g_ffn_pre, name="mix_out")
    u_g, u_v, y = _ffn_up_glu(h2, w_up_t, conv_w, conv_b, name="ffn_up_glu")
    df, d_out, loss_blk, gg_ffn_post = _ffn_out(y, w_down, x2, target, g_ffn_post, name="ffn_out")
    du_g, du_v, gw_up_g, gw_up_v, gw_down, gcw_g, gcw_v, gcb_g, gcb_v = _ffn_glu_bwd(
        u_g, u_v, df, w_down, h2, conv_w, conv_b, name="ffn_glu_bwd")
    gw_up_t = jnp.concatenate([gw_up_g, gw_up_v], axis=0)
    dx2, gg_ffn_pre, dmixed, gg_mix_post = _dgrad_norm(
        [du_g, du_v], w_up_t, d_out, x2, g_ffn_pre, (mixed, g_mix_post), [], tk=1408, name="ffn_up_dgrad")
    gw_out = jnp.concatenate([_matmul_tn(attn16, dmixed, ta=512, ts=1024, name="grad_w_out_attn"),
                              _matmul_tn(pool, dmixed, ta=512, ts=1024, name="grad_w_out_pool")], axis=0)
    dcat = _matmul(dmixed, w_out, trans_b=True, out_dtype=F32, tm=512, tn=1024, name="mix_out_dgrad")
    d_pool_in, g_pool_w, g_pool_scale = _pool_bwd(proj, 3, dcat, 1, pool_w, pool_scale, name="pool_bwd")
    dqkv, (r_out, r_up_t, r_down) = _attn_bwd(proj, dcat, attn, lse, [gw_out, gw_up_t, gw_down], name="attn_bwd")
    dproj = list(dqkv) + [d_pool_in]
    gw_in_t = jnp.concatenate([_matmul_tn(a, h1, ta=512, ts=1024, name=f"grad_w_in_{k}")
                               for k, a in enumerate(dproj)], axis=0)
    grad_x, gg_mix_pre, (r_in_t,) = _dgrad_norm(dproj, w_in_t, dx2, x, g_mix_pre, None, [gw_in_t], tk=512,
                                                name="proj_dgrad")
    g_conv_w = jnp.concatenate([gcw_g, gcw_v], axis=1)
    g_conv_b = jnp.concatenate([gcb_g, gcb_v], axis=1)
    received = (r_in_t, r_out, r_up_t, r_down)
    small = dict(g_mix_pre=gg_mix_pre, g_mix_post=gg_mix_post, g_ffn_pre=gg_ffn_pre, g_ffn_post=gg_ffn_post,
                 pool_scale=g_pool_scale, conv_b=g_conv_b, pool_w=g_pool_w, conv_w=g_conv_w)
    return loss_blk, grad_x, received, small


_SMALL = ("g_mix_pre", "g_mix_post", "g_ffn_pre", "g_ffn_post", "pool_scale", "conv_b", "pool_w")
LANES = 128


def _pack_rows(arrays):
    parts = []
    for a in arrays:
        a2 = a.reshape(-1, LANES)
        parts.append(jnp.pad(a2, ((0, (-a2.shape[0]) % 8), (0, 0))))
    return jnp.concatenate(parts, axis=0)


def _unpack_rows(packed, shapes):
    out, row = [], 0
    for shape in shapes:
        rows = math.prod(shape) // LANES
        out.append(packed[row:row + rows].reshape(shape))
        row += -(-rows // 8) * 8
    return out


def kernel(x, g_mix_pre, w_in, pool_w, pool_scale, w_out, g_mix_post, g_ffn_pre, w_up, conv_w, conv_b, w_down, g_ffn_post, loss_target, m_g_mix_pre, m_w_in, m_pool_w, m_pool_scale, m_w_out, m_g_mix_post, m_g_ffn_pre, m_w_up, m_conv_w, m_conv_b, m_w_down, m_g_ffn_post, v_g_mix_pre, v_w_in, v_pool_w, v_pool_scale, v_w_out, v_g_mix_post, v_g_ffn_pre, v_w_up, v_conv_w, v_conv_b, v_w_down, v_g_ffn_post):
    me = 4 * lax.axis_index("x") + 2 * lax.axis_index("y") + lax.axis_index("c")
    w_in_t = _all_gather_hbm(w_in[0].T.astype(BF16), name="gather_w_in").reshape(4 * ATTN_WIDTH, D_MODEL)
    loss_blk, grad_x, recv, small = _device_step(
        x[0], loss_target[0], g_mix_pre, w_in_t, _rest_payload(w_out[0], w_up[0], w_down[0], conv_w[0]),
        pool_w[0], pool_scale, g_mix_post, g_ffn_pre, conv_b, g_ffn_post)
    loss = lax.psum(loss_blk[0, 0], ("x", "y", "c"))

    g_in_t, g_out, g_up_t, g_down = (
        _sum_partials(r, name=f"sum_partials_{k}", tr=r.shape[1] // 2) for k, r in enumerate(recv))
    grads = {"w_in": g_in_t.T, "w_out": g_out, "w_up": g_up_t.T, "w_down": g_down}

    given = dict(g_mix_pre=g_mix_pre, g_mix_post=g_mix_post, g_ffn_pre=g_ffn_pre, g_ffn_post=g_ffn_post,
                 pool_scale=pool_scale, conv_b=conv_b, pool_w=pool_w)
    small_shapes = [given[k].shape for k in _SMALL]
    total = _all_reduce_small(_pack_rows([small[k] for k in _SMALL] + [small["conv_w"]]), name="all_reduce_small")
    *small_grads, g_conv_w_all = _unpack_rows(total, small_shapes + [(3, 2 * D_FF)])
    grads.update(zip(_SMALL, small_grads))
    width = 2 * D_FF // N_DEV
    grads["conv_w"] = lax.dynamic_slice_in_dim(g_conv_w_all, me * width, width, axis=1)[None]

    weights = dict(g_mix_pre=g_mix_pre, w_in=w_in, pool_w=pool_w, pool_scale=pool_scale, w_out=w_out,
                   g_mix_post=g_mix_post, g_ffn_pre=g_ffn_pre, w_up=w_up, conv_w=conv_w, conv_b=conv_b,
                   w_down=w_down, g_ffn_post=g_ffn_post)
    m_in = dict(g_mix_pre=m_g_mix_pre, w_in=m_w_in, pool_w=m_pool_w, pool_scale=m_pool_scale, w_out=m_w_out,
                g_mix_post=m_g_mix_post, g_ffn_pre=m_g_ffn_pre, w_up=m_w_up, conv_w=m_conv_w, conv_b=m_conv_b,
                w_down=m_w_down, g_ffn_post=m_g_ffn_post)
    v_in = dict(g_mix_pre=v_g_mix_pre, w_in=v_w_in, pool_w=v_pool_w, pool_scale=v_pool_scale, w_out=v_w_out,
                g_mix_post=v_g_mix_post, g_ffn_pre=v_g_ffn_pre, w_up=v_w_up, conv_w=v_conv_w, conv_b=v_conv_b,
                w_down=v_w_down, g_ffn_post=v_g_ffn_post)
    delta, new_m, new_v = {}, {}, {}
    for k in ("w_in", "w_out", "w_up", "w_down"):
        g = grads[k]
        d, nm, nv = _adamw(weights[k][0], g, m_in[k][0], v_in[k][0], name=f"adamw_{k}", tr=g.shape[0] // 2)
        grads[k], delta[k], new_m[k], new_v[k] = g[None], d[None], nm[None], nv[None]
    d, nm, nv = _adamw(weights["conv_w"][0], grads["conv_w"][0], m_in["conv_w"][0], v_in["conv_w"][0],
                       name="adamw_conv_w", tr=3)
    delta["conv_w"], new_m["conv_w"], new_v["conv_w"] = d[None], nm[None], nv[None]
    packed_w = _pack_rows([weights[k] for k in _SMALL])
    small_rows = packed_w.shape[0]
    d, nm, nv = _adamw(packed_w, total[:small_rows], _pack_rows([m_in[k] for k in _SMALL]),
                       _pack_rows([v_in[k] for k in _SMALL]), name="adamw_small", tr=small_rows)
    for k, dk, mk, vk in zip(_SMALL, _unpack_rows(d, small_shapes), _unpack_rows(nm, small_shapes),
                             _unpack_rows(nv, small_shapes)):
        delta[k], new_m[k], new_v[k] = dk, mk, vk

    order = ("g_mix_pre", "w_in", "pool_w", "pool_scale", "w_out", "g_mix_post", "g_ffn_pre", "w_up",
             "conv_w", "conv_b", "w_down", "g_ffn_post")
    return (loss, grad_x[None], *[grads[k] for k in order], *[delta[k] for k in order],
            *[new_m[k] for k in order], *[new_v[k] for k in order])
```

```python
import functools
import math

import jax
import jax.numpy as jnp
from jax import lax
from jax.experimental import pallas as pl
from jax.experimental.pallas import tpu as pltpu

F32 = jnp.float32
BF16 = jnp.bfloat16

D_MODEL = 1024
ATTN_WIDTH = 512
N_HEADS = 8
HEAD_DIM = 64
DILATIONS = (1, 4, 16)
BLOCK = 128
POOL_WIDTH = 512
POOL_WINDOWS = (2, 4, 8, 16)
POOL_GROUP_DIM = 128
D_FF = 2816
EPS = 1e-6
NEG_INF = -1e30
SCALE = HEAD_DIM ** -0.5

ADAM_LR = 0.001
ADAM_B1 = 0.9
ADAM_B2 = 0.999
ADAM_EPS = 1e-08
ADAM_WD = 0.01
ADAM_STEP = 10

N_DEV = 8
HALO = 16
V7X_VMEM_LIMIT = 56 * 1024 * 1024

MESH = pl.DeviceIdType.MESH
ANY = pl.BlockSpec(memory_space=pl.ANY)
VMEM = pl.BlockSpec(memory_space=pltpu.VMEM)

NT = (((1,), (1,)), ((), ()))
NN = (((1,), (0,)), ((), ()))
TN = (((0,), (0,)), ((), ()))


def _cp(*sem):
    return pltpu.CompilerParams(dimension_semantics=sem, vmem_limit_bytes=V7X_VMEM_LIMIT)


def _dot(a, b, dn):
    return lax.dot_general(a, b, dn, preferred_element_type=F32)


def _rms_bwd(xin, g, dy):
    r = lax.rsqrt(jnp.mean(xin * xin, axis=-1, keepdims=True) + EPS)
    xh = xin * r
    gdy = g * dy
    dx = r * (gdy - xh * jnp.mean(gdy * xh, axis=-1, keepdims=True))
    dg = jnp.sum(dy * xh, axis=0, keepdims=True)
    return dx, dg


def _rms_norm(x, g, *, name, tm=512):
    S, D = x.shape

    def body(x_ref, g_ref, o_ref):
        xv = x_ref[...]
        r = lax.rsqrt(jnp.mean(xv * xv, axis=-1, keepdims=True) + EPS)
        o_ref[...] = (xv * r * g_ref[...]).astype(BF16)

    return pl.pallas_call(
        body, name=name, grid=(S // tm,),
        in_specs=[pl.BlockSpec((tm, D), lambda i: (i, 0)), pl.BlockSpec((1, D), lambda i: (0, 0))],
        out_specs=pl.BlockSpec((tm, D), lambda i: (i, 0)),
        out_shape=jax.ShapeDtypeStruct((S, D), BF16),
        compiler_params=_cp("parallel"),
    )(x, g)


def _matmul(a, b, *, trans_b, out_dtype, tm, tn, name):
    M, K = a.shape
    N = b.shape[0] if trans_b else b.shape[1]
    dn = NT if trans_b else NN

    def body(a_ref, b_ref, o_ref):
        o_ref[...] = _dot(a_ref[...], b_ref[...], dn).astype(out_dtype)

    b_spec = (pl.BlockSpec((tn, K), lambda i, j: (j, 0)) if trans_b
              else pl.BlockSpec((K, tn), lambda i, j: (0, j)))
    return pl.pallas_call(
        body, name=name, grid=(M // tm, N // tn),
        in_specs=[pl.BlockSpec((tm, K), lambda i, j: (i, 0)), b_spec],
        out_specs=pl.BlockSpec((tm, tn), lambda i, j: (i, j)),
        out_shape=jax.ShapeDtypeStruct((M, N), out_dtype),
        compiler_params=_cp("parallel", "parallel"),
    )(a, b)


def _matmul_tn(a, b, *, ta, ts, name):
    S, Ka = a.shape
    Nb = b.shape[1]
    ns = S // ts

    def body(a_ref, b_ref, o_ref, acc):
        s = pl.program_id(1)

        @pl.when(s == 0)
        def _():
            acc[...] = jnp.zeros_like(acc)

        acc[...] += _dot(a_ref[...], b_ref[...], TN)

        @pl.when(s == ns - 1)
        def _():
            o_ref[...] = acc[...].astype(BF16)

    return pl.pallas_call(
        body, name=name, grid=(Ka // ta, ns),
        in_specs=[pl.BlockSpec((ts, ta), lambda i, s: (s, i)), pl.BlockSpec((ts, Nb), lambda i, s: (s, 0))],
        out_specs=pl.BlockSpec((ta, Nb), lambda i, s: (i, 0)),
        out_shape=jax.ShapeDtypeStruct((Ka, Nb), BF16),
        scratch_shapes=[pltpu.VMEM((ta, Nb), F32)],
        compiler_params=_cp("parallel", "arbitrary"),
    )(a, b)


def _mix_out(attn, pool, w_out, x, g_post, g_next, *, name, tm=256):
    S, K = attn.shape
    D = w_out.shape[1]

    def body(a_ref, p_ref, w_ref, x_ref, gp_ref, gn_ref, mixed_ref, x2_ref, h2_ref):
        mixed = _dot(a_ref[...], w_ref[:K, :], NN) + _dot(p_ref[...], w_ref[K:, :], NN)
        r = lax.rsqrt(jnp.mean(mixed * mixed, axis=-1, keepdims=True) + EPS)
        x2 = x_ref[...] + mixed * r * gp_ref[...]
        r2 = lax.rsqrt(jnp.mean(x2 * x2, axis=-1, keepdims=True) + EPS)
        mixed_ref[...] = mixed
        x2_ref[...] = x2
        h2_ref[...] = (x2 * r2 * gn_ref[...]).astype(BF16)

    row = lambda i: (i, 0)
    fix = lambda i: (0, 0)
    return pl.pallas_call(
        body, name=name, grid=(S // tm,),
        in_specs=[pl.BlockSpec((tm, K), row), pl.BlockSpec((tm, K), row), pl.BlockSpec((2 * K, D), fix),
                  pl.BlockSpec((tm, D), row), pl.BlockSpec((1, D), fix), pl.BlockSpec((1, D), fix)],
        out_specs=[pl.BlockSpec((tm, D), row)] * 3,
        out_shape=[jax.ShapeDtypeStruct((S, D), F32), jax.ShapeDtypeStruct((S, D), F32),
                   jax.ShapeDtypeStruct((S, D), BF16)],
        compiler_params=_cp("parallel"),
    )(attn, pool, w_out, x, g_post, g_next)


def _ffn_out(y, w_down, x2, target, g_post, *, name, tm=256):
    S, K = y.shape
    D = w_down.shape[1]

    def body(y_ref, w_ref, x2_ref, t_ref, g_ref, df_ref, dout_ref, loss_ref, gg_ref):
        i = pl.program_id(0)

        @pl.when(i == 0)
        def _():
            loss_ref[...] = jnp.zeros_like(loss_ref)
            gg_ref[...] = jnp.zeros_like(gg_ref)

        f = _dot(y_ref[...], w_ref[...], NN)
        g = g_ref[...]
        r = lax.rsqrt(jnp.mean(f * f, axis=-1, keepdims=True) + EPS)
        out = x2_ref[...] + f * r * g
        err = out - t_ref[...]
        dy = err * (1.0 / D)
        df, dg = _rms_bwd(f, g, dy)
        df_ref[...] = df.astype(BF16)
        dout_ref[...] = dy
        gg_ref[...] += dg
        loss_ref[...] += 0.5 * jnp.sum(jnp.mean(err * err, axis=-1, keepdims=True))

    row = lambda i: (i, 0)
    fix = lambda i: (0, 0)
    return pl.pallas_call(
        body, name=name, grid=(S // tm,),
        in_specs=[pl.BlockSpec((tm, K), row), pl.BlockSpec((K, D), fix), pl.BlockSpec((tm, D), row),
                  pl.BlockSpec((tm, D), row), pl.BlockSpec((1, D), fix)],
        out_specs=[pl.BlockSpec((tm, D), row), pl.BlockSpec((tm, D), row),
                   pl.BlockSpec((8, 128), fix), pl.BlockSpec((1, D), fix)],
        out_shape=[jax.ShapeDtypeStruct((S, D), BF16), jax.ShapeDtypeStruct((S, D), F32),
                   jax.ShapeDtypeStruct((8, 128), F32), jax.ShapeDtypeStruct((1, D), F32)],
        compiler_params=_cp("arbitrary"),
    )(y, w_down, x2, target, g_post)


def _dgrad_norm(a_list, w, resid, xin, g, second, exchange, *, tk, name, tm=512):
    S, Kp = a_list[0].shape
    na = len(a_list)
    D = w.shape[1]
    kper = Kp // tk
    nk = na * kper
    nt = S // tm
    two = second is not None
    ng = len(exchange)
    recv_shapes, exchange_sems = _exchange_buffers(exchange)

    def body(*refs):
        a_refs = refs[:na]
        w_ref, r_ref, x_ref, g_ref = refs[na:na + 4]
        pos = na + 4
        if two:
            x2_ref, g2_ref = refs[pos:pos + 2]
            pos += 2
        g_refs = refs[pos:pos + ng]
        pos += ng
        dx_ref, gg_ref = refs[pos:pos + 2]
        pos += 2
        if two:
            d2_ref, gg2_ref = refs[pos:pos + 2]
            pos += 2
        r_refs = refs[pos:pos + ng]
        pos += ng
        acc = refs[pos]
        i = pl.program_id(0)
        k = pl.program_id(1)
        if ng:
            start, finish = _exchange_phases(g_refs, r_refs, *refs[pos + 1:])
            pl.when((i == 0) & (k == 0))(start)

        @pl.when(k == 0)
        def _():
            acc[...] = jnp.zeros_like(acc)

        @pl.when((i == 0) & (k == 0))
        def _():
            gg_ref[...] = jnp.zeros_like(gg_ref)
            if two:
                gg2_ref[...] = jnp.zeros_like(gg2_ref)

        for q in range(na):
            @pl.when(k // kper == q)
            def _(q=q):
                acc[...] += _dot(a_refs[q][...], w_ref[...], NN)

        @pl.when(k == nk - 1)
        def _():
            d1, dg1 = _rms_bwd(x_ref[...], g_ref[...], acc[...])
            dx = r_ref[...] + d1
            dx_ref[...] = dx
            gg_ref[...] += dg1
            if two:
                d2, dg2 = _rms_bwd(x2_ref[...], g2_ref[...], dx)
                d2_ref[...] = d2.astype(BF16)
                gg2_ref[...] += dg2

        if ng:
            pl.when((i == nt - 1) & (k == nk - 1))(finish)

    row = lambda i, k: (i, 0)
    fix = lambda i, k: (0, 0)
    a_specs = [pl.BlockSpec((tm, tk), functools.partial(
        lambda i, k, q: (i, jnp.clip(k - q * kper, 0, kper - 1)), q=q)) for q in range(na)]
    in_specs = a_specs + [pl.BlockSpec((tk, D), lambda i, k: (k, 0)), pl.BlockSpec((tm, D), row),
                          pl.BlockSpec((tm, D), row), pl.BlockSpec((1, D), fix)]
    args = list(a_list) + [w, resid, xin, g]
    out_specs = [pl.BlockSpec((tm, D), row), pl.BlockSpec((1, D), fix)]
    out_shape = [jax.ShapeDtypeStruct((S, D), F32), jax.ShapeDtypeStruct((1, D), F32)]
    if two:
        in_specs += [pl.BlockSpec((tm, D), row), pl.BlockSpec((1, D), fix)]
        args += list(second)
        out_specs += [pl.BlockSpec((tm, D), row), pl.BlockSpec((1, D), fix)]
        out_shape += [jax.ShapeDtypeStruct((S, D), BF16), jax.ShapeDtypeStruct((1, D), F32)]
    n_plain = len(out_shape)
    out = pl.pallas_call(
        body, name=name, grid=(nt, nk), in_specs=in_specs + [ANY] * ng, out_specs=out_specs + [ANY] * ng,
        out_shape=out_shape + recv_shapes,
        scratch_shapes=[pltpu.VMEM((tm, D), F32)] + (exchange_sems if ng else []),
        compiler_params=_cp("arbitrary", "arbitrary"),
    )(*args, *exchange)
    return (*out[:n_plain], out[n_plain:]) if ng else out


def _band_mask(first_block):
    qi = lax.broadcasted_iota(jnp.int32, (BLOCK, 2 * BLOCK), 0)
    ki = lax.broadcasted_iota(jnp.int32, (BLOCK, 2 * BLOCK), 1)
    first_key = jnp.where(first_block, BLOCK, 0)
    return (ki >= qi) & (ki <= qi + BLOCK) & (ki >= first_key)


def _lane_masks():
    lane = lax.broadcasted_iota(jnp.int32, (1, 2 * HEAD_DIM), 1)
    return (lane < HEAD_DIM, lane >= HEAD_DIM)


CHUNK = BLOCK * max(DILATIONS)
SLAB = 2 * HEAD_DIM
N_SLABS = ATTN_WIDTH // SLAB


def _unit_rows(d, b):
    def rows(r):
        start = r + BLOCK * d * b
        return pl.ds(start, BLOCK, stride=d) if d > 1 else pl.ds(start, BLOCK)
    return rows


def _attn_units():
    for p, d in enumerate(DILATIONS):
        nbc = CHUNK // (BLOCK * d)
        for b in range(nbc):
            for r in range(d):
                yield p, d, b, r, nbc


def _attn_in_specs(nc, n_cur):
    prev = lambda c: jnp.maximum(jnp.minimum(c, nc - 1) - 1, 0)
    cur = lambda c: jnp.minimum(c, nc - 1)
    blk = lambda f: pl.BlockSpec((CHUNK, SLAB), f)
    specs = [blk(lambda h, c: (cur(c), h)),
             blk(lambda h, c: (prev(c), N_SLABS + h)), blk(lambda h, c: (cur(c), N_SLABS + h)),
             blk(lambda h, c: (prev(c), 2 * N_SLABS + h)), blk(lambda h, c: (cur(c), 2 * N_SLABS + h))]
    return specs + [blk(lambda h, c: (cur(c), h))] * n_cur


def _attn_fwd(proj, payload, *, name):
    S = proj.shape[0]
    nc = S // CHUNK
    n = len(DILATIONS)
    n_steps = N_SLABS * nc

    def body(q_ref, kp_ref, kc_ref, vp_ref, vc_ref, pay_ref, attn_ref, lse_ref, attn16_ref, all_ref, *scr):
        o_scr, l_scr = scr[:n], scr[n:2 * n]
        start, forward, finish = _gather_phases(pay_ref, all_ref, *scr[2 * n:])
        step = pl.program_id(0) * nc + pl.program_id(1)
        pl.when(step == 0)(start)
        c = pl.program_id(1)
        lms = _lane_masks()
        plain, first = _band_mask(False), _band_mask(c == 0)
        for p, d, b, r, nbc in _attn_units():
            rows = _unit_rows(d, b)(r)
            prow = _unit_rows(d, b - 1 if b > 0 else nbc - 1)(r)
            kpr, vpr = (kc_ref, vc_ref) if b > 0 else (kp_ref, vp_ref)
            mask = plain if b > 0 else first
            q = q_ref[rows, :].astype(BF16)
            kcat = jnp.concatenate([kpr[prow, :], kc_ref[rows, :]], axis=0).astype(BF16)
            vcat = jnp.concatenate([vpr[prow, :], vc_ref[rows, :]], axis=0).astype(BF16)
            o_slab = jnp.zeros((BLOCK, SLAB), F32)
            lse_slab = jnp.zeros((BLOCK, SLAB), F32)
            for lm in lms:
                qs = jnp.where(lm, q, jnp.zeros_like(q)) * SCALE
                vh = jnp.where(lm, vcat, jnp.zeros_like(vcat))
                s = jnp.where(mask, _dot(qs, kcat, NT), NEG_INF)
                m = jnp.max(s, axis=-1, keepdims=True)
                e = jnp.exp(s - m)
                l = jnp.sum(e, axis=-1, keepdims=True)
                o_slab = o_slab + _dot(e.astype(BF16), vh, NN) / l
                lse_slab = jnp.where(lm, m + jnp.log(l), lse_slab)
            o_scr[p][rows, :] = o_slab
            l_scr[p][rows, :] = lse_slab
        ls = [l_scr[p][...] for p in range(n)]
        top = functools.reduce(jnp.maximum, ls)
        es = [jnp.exp(l - top) for l in ls]
        den = functools.reduce(jnp.add, es)
        num = functools.reduce(jnp.add, [e * o_scr[p][...] for p, e in enumerate(es)])
        attn = num / den
        attn_ref[...] = attn
        attn16_ref[...] = attn.astype(BF16)
        lse_ref[...] = top + jnp.log(den)
        pl.when(step == (2 * n_steps) // 3)(forward)
        pl.when(step == n_steps - 1)(finish)

    return pl.pallas_call(
        body, name=name, grid=(N_SLABS, nc), in_specs=_attn_in_specs(nc, 0) + [ANY],
        out_specs=[pl.BlockSpec((CHUNK, SLAB), lambda h, c: (c, h))] * 3 + [ANY],
        out_shape=[jax.ShapeDtypeStruct((S, ATTN_WIDTH), F32)] * 2 + [jax.ShapeDtypeStruct((S, ATTN_WIDTH), BF16),
                   jax.ShapeDtypeStruct((N_DEV,) + payload.shape, payload.dtype)],
        scratch_shapes=[pltpu.VMEM((CHUNK, SLAB), F32)] * (2 * n) + _GATHER_SEMS,
        compiler_params=_cp("arbitrary", "arbitrary"),
    )(proj, proj, proj, proj, proj, payload)


def _attn_bwd(proj, dcat, attn, lse, grads, *, name):
    S = proj.shape[0]
    nc = S // CHUNK
    ng = len(grads)
    recv_shapes, exchange_sems = _exchange_buffers(grads)

    def body(*refs):
        q_ref, kp_ref, kc_ref, vp_ref, vc_ref, do_ref, o_ref, lse_ref = refs[:8]
        g_refs = refs[8:8 + ng]
        dq_ref, dk_ref, dv_ref = refs[8 + ng:11 + ng]
        r_refs = refs[11 + ng:11 + 2 * ng]
        dq_acc, dk_prev, dk_cur, dv_prev, dv_cur, delta_scr = refs[11 + 2 * ng:17 + 2 * ng]
        start, finish = _exchange_phases(g_refs, r_refs, *refs[17 + 2 * ng:])
        c = pl.program_id(1)
        pl.when((pl.program_id(0) == 0) & (c == 0))(start)

        @pl.when(c == 0)
        def _():
            dk_prev[...] = jnp.zeros_like(dk_prev)
            dv_prev[...] = jnp.zeros_like(dv_prev)

        @pl.when(c < nc)
        def _():
            lms = _lane_masks()
            plain, first = _band_mask(False), _band_mask(c == 0)
            for acc in (dq_acc, dk_cur, dv_cur):
                acc[...] = jnp.zeros_like(acc)
            prod = do_ref[...] * o_ref[...]
            delta_a, delta_b = (jnp.sum(jnp.where(lm, prod, 0.0), axis=-1, keepdims=True) for lm in lms)
            delta_scr[...] = jnp.where(lms[0], delta_a, delta_b)
            for p, d, b, r, nbc in _attn_units():
                rows = _unit_rows(d, b)(r)
                prow = _unit_rows(d, b - 1 if b > 0 else nbc - 1)(r)
                kpr, vpr = (kc_ref, vc_ref) if b > 0 else (kp_ref, vp_ref)
                dkp, dvp = (dk_cur, dv_cur) if b > 0 else (dk_prev, dv_prev)
                mask = plain if b > 0 else first
                q = q_ref[rows, :].astype(BF16)
                kcat = jnp.concatenate([kpr[prow, :], kc_ref[rows, :]], axis=0).astype(BF16)
                vcat = jnp.concatenate([vpr[prow, :], vc_ref[rows, :]], axis=0).astype(BF16)
                do = do_ref[rows, :]
                lse_sl = lse_ref[rows, :]
                delta_sl = delta_scr[rows, :]
                dq = jnp.zeros((BLOCK, SLAB), F32)
                dkc = jnp.zeros((2 * BLOCK, SLAB), F32)
                dvc = jnp.zeros((2 * BLOCK, SLAB), F32)
                for lm in lms:
                    qs = jnp.where(lm, q, jnp.zeros_like(q)) * SCALE
                    kh = jnp.where(lm, kcat, jnp.zeros_like(kcat))
                    doh = jnp.where(lm, do, 0.0).astype(BF16)
                    delta = jnp.max(jnp.where(lm, delta_sl, -jnp.inf), axis=-1, keepdims=True)
                    lse_h = jnp.max(jnp.where(lm, lse_sl, -jnp.inf), axis=-1, keepdims=True)
                    s = _dot(qs, kcat, NT)
                    e = jnp.where(mask, jnp.exp(s - lse_h), 0.0)
                    dp = _dot(doh, vcat, NT)
                    ds = (e * (dp - delta)).astype(BF16)
                    dq = dq + _dot(ds, kh, NN) * SCALE
                    dkc = dkc + _dot(ds, qs, TN)
                    dvc = dvc + _dot(e.astype(BF16), doh, TN)
                dq_acc[rows, :] += dq
                dkp[prow, :] += dkc[:BLOCK]
                dvp[prow, :] += dvc[:BLOCK]
                dk_cur[rows, :] += dkc[BLOCK:]
                dv_cur[rows, :] += dvc[BLOCK:]
            dq_ref[...] = dq_acc[...].astype(BF16)
            dk_ref[...] = dk_prev[...].astype(BF16)
            dv_ref[...] = dv_prev[...].astype(BF16)
            dk_prev[...] = dk_cur[...]
            dv_prev[...] = dv_cur[...]

        @pl.when(c == nc)
        def _():
            dk_ref[...] = dk_prev[...].astype(BF16)
            dv_ref[...] = dv_prev[...].astype(BF16)

        pl.when((pl.program_id(0) == N_SLABS - 1) & (c == nc))(finish)

    blk = lambda f: pl.BlockSpec((CHUNK, SLAB), f)
    late = lambda h, c: (jnp.maximum(c - 1, 0), h)
    out = pl.pallas_call(
        body, name=name, grid=(N_SLABS, nc + 1), in_specs=_attn_in_specs(nc, 3) + [ANY] * ng,
        out_specs=[blk(lambda h, c: (jnp.minimum(c, nc - 1), h)), blk(late), blk(late)] + [ANY] * ng,
        out_shape=[jax.ShapeDtypeStruct((S, ATTN_WIDTH), BF16)] * 3 + recv_shapes,
        scratch_shapes=[pltpu.VMEM((CHUNK, SLAB), F32)] * 6 + exchange_sems,
        compiler_params=_cp("arbitrary", "arbitrary"),
    )(proj, proj, proj, proj, proj, dcat, attn, lse, *grads)
    return out[:3], out[3:]


def _split_bf16(a):
    hi = a.astype(BF16)
    lo = (a - hi.astype(F32)).astype(BF16)
    return hi, lo


def _pooled(ug, halo_g, w, row0, tm):
    ext = jnp.concatenate([halo_g, ug], axis=0)
    hi, lo = _split_bf16(ext)
    rr = lax.broadcasted_iota(jnp.int32, (tm, tm + HALO), 0)
    cc = lax.broadcasted_iota(jnp.int32, (tm, tm + HALO), 1)
    back = rr + HALO - cc
    win = ((back >= 0) & (back < w)).astype(BF16)
    wsum = _dot(win, hi, NN) + _dot(win, lo, NN)
    rows = row0 + lax.broadcasted_iota(jnp.int32, (tm, 1), 0)
    inv = 1.0 / jnp.minimum(rows + 1, w).astype(F32)
    return wsum * inv - ug


def _pool_fwd(u, u_col, pool_w, pool_scale, *, name, tm=256):
    S, W = u.shape[0], POOL_WIDTH
    G = POOL_GROUP_DIM

    def body(u_ref, h_ref, w_ref, s_ref, o_ref):
        i = pl.program_id(0)
        uv = u_ref[...]
        halo = jnp.where(i > 0, h_ref[...], 0.0)
        for g, w in enumerate(POOL_WINDOWS):
            sl = slice(g * G, (g + 1) * G)
            pooled = _pooled(uv[:, sl], halo[:, sl], w, i * tm, tm)
            z = _dot(pooled.astype(BF16), w_ref[g].astype(BF16), NN)
            o_ref[:, sl] = (z * s_ref[:, sl]).astype(BF16)

    per = tm // HALO
    return pl.pallas_call(
        body, name=name, grid=(S // tm,),
        in_specs=[pl.BlockSpec((tm, W), lambda i: (i, u_col)),
                  pl.BlockSpec((HALO, W), lambda i: (jnp.maximum(i * per - 1, 0), u_col)),
                  pl.BlockSpec((len(POOL_WINDOWS), G, G), lambda i: (0, 0, 0)),
                  pl.BlockSpec((1, W), lambda i: (0, 0))],
        out_specs=pl.BlockSpec((tm, W), lambda i: (i, 0)),
        out_shape=jax.ShapeDtypeStruct((S, W), BF16),
        compiler_params=_cp("parallel"),
    )(u, u, pool_w, pool_scale)


def _pool_bwd(u, u_col, dy, dy_col, pool_w, pool_scale, *, name, tm=256):
    S, W = u.shape[0], POOL_WIDTH
    G = POOL_GROUP_DIM
    nt = S // tm

    def body(u_ref, h_ref, dy_ref, dyn_ref, w_ref, s_ref, du_ref, gw_ref, gs_ref):
        i = pl.program_id(0)

        @pl.when(i == 0)
        def _():
            gw_ref[...] = jnp.zeros_like(gw_ref)
            gs_ref[...] = jnp.zeros_like(gs_ref)

        uv = u_ref[...]
        halo = jnp.where(i > 0, h_ref[...], 0.0)
        dyv = dy_ref[...]
        dyn = jnp.where(i < nt - 1, dyn_ref[...], 0.0)
        rr = lax.broadcasted_iota(jnp.int32, (tm, tm + HALO), 0)
        cc = lax.broadcasted_iota(jnp.int32, (tm, tm + HALO), 1)
        rows_ext = i * tm + lax.broadcasted_iota(jnp.int32, (tm + HALO, 1), 0)
        for g, w in enumerate(POOL_WINDOWS):
            sl = slice(g * G, (g + 1) * G)
            wg = w_ref[g].astype(BF16)
            sc = s_ref[:, sl]
            pooled = _pooled(uv[:, sl], halo[:, sl], w, i * tm, tm)
            z = _dot(pooled.astype(BF16), wg, NN)
            gs_ref[:, sl] += jnp.sum(dyv[:, sl] * z, axis=0, keepdims=True)
            dz = dyv[:, sl] * sc
            gw_ref[g] += _dot(pooled.astype(BF16), dz.astype(BF16), TN)
            dz_ext = jnp.concatenate([dz, dyn[:, sl] * sc], axis=0)
            dp_ext = _dot(dz_ext.astype(BF16), wg, NT)
            inv_ext = 1.0 / jnp.minimum(rows_ext + 1, w).astype(F32)
            hi, lo = _split_bf16(dp_ext * inv_ext)
            ahead = cc - rr
            win = ((ahead >= 0) & (ahead < w)).astype(BF16)
            du_ref[:, sl] = (_dot(win, hi, NN) + _dot(win, lo, NN) - dp_ext[:tm]).astype(BF16)

    per = tm // HALO
    nh = S // HALO
    return pl.pallas_call(
        body, name=name, grid=(nt,),
        in_specs=[pl.BlockSpec((tm, W), lambda i: (i, u_col)),
                  pl.BlockSpec((HALO, W), lambda i: (jnp.maximum(i * per - 1, 0), u_col)),
                  pl.BlockSpec((tm, W), lambda i: (i, dy_col)),
                  pl.BlockSpec((HALO, W), lambda i: (jnp.minimum((i + 1) * per, nh - 1), dy_col)),
                  pl.BlockSpec((len(POOL_WINDOWS), G, G), lambda i: (0, 0, 0)),
                  pl.BlockSpec((1, W), lambda i: (0, 0))],
        out_specs=[pl.BlockSpec((tm, W), lambda i: (i, 0)),
                   pl.BlockSpec((len(POOL_WINDOWS), G, G), lambda i: (0, 0, 0)),
                   pl.BlockSpec((1, W), lambda i: (0, 0))],
        out_shape=[jax.ShapeDtypeStruct((S, W), BF16),
                   jax.ShapeDtypeStruct((len(POOL_WINDOWS), G, G), F32),
                   jax.ShapeDtypeStruct((1, W), F32)],
        compiler_params=_cp("arbitrary"),
    )(u, u, dy, dy, pool_w, pool_scale)


GELU_K0 = math.sqrt(2.0 / math.pi)
GELU_K1 = 0.044715


def _gelu_parts(x):
    x2 = x * x
    t = jnp.tanh(x * (GELU_K0 + (GELU_K0 * GELU_K1) * x2))
    hp = 0.5 + 0.5 * t
    gelu = x * hp
    dgelu = hp + (x * (hp * (1.0 - t))) * (GELU_K0 + (3.0 * GELU_K0 * GELU_K1) * x2)
    return gelu, dgelu


def _shifted(ext, halo):
    return (pltpu.roll(ext, 2, 0)[halo:], pltpu.roll(ext, 1, 0)[halo:], ext[halo:])


def _conv(sh, w, b):
    return b + (sh[0] * w[0:1] + sh[1] * w[1:2] + sh[2] * w[2:3])


F32_ROWS = 8


def _ffn_up_glu(h, w_up_t, conv_w, conv_b, *, name, tm=1024, tn=256, sub=512):
    S, K = h.shape
    F = D_FF
    nj = F // tn

    def body(h_ref, wg_ref, wv_ref, cwg_ref, cwv_ref, cbg_ref, cbv_ref,
             ug_ref, uv_ref, cg_ref, cv_ref, y_ref, carry):
        i = pl.program_id(0)
        j = pl.program_id(1)

        w_cat = jnp.concatenate([wg_ref[...], wv_ref[...]], axis=0)
        conv_w_b = ((cwg_ref[...], cbg_ref[...]), (cwv_ref[...], cbv_ref[...]))
        halo = [jnp.where(i > 0, carry[j, s], 0.0) for s in range(2)]
        for a in range(0, tm, sub):
            u16 = _dot(h_ref[a:a + sub, :], w_cat, NT).astype(BF16)
            ug_ref[a:a + sub, :] = u16[:, :tn]
            uv_ref[a:a + sub, :] = u16[:, tn:]
            c = []
            for s, (cw, cb) in enumerate(conv_w_b):
                u = u16[:, s * tn:(s + 1) * tn].astype(F32)
                ext = jnp.concatenate([halo[s], u], axis=0)
                c.append(_conv(_shifted(ext, F32_ROWS), cw, cb))
                halo[s] = u[sub - F32_ROWS:]
            cg_ref[a:a + sub, :] = c[0].astype(BF16)
            cv_ref[a:a + sub, :] = c[1].astype(BF16)
            gelu, _ = _gelu_parts(c[0])
            y_ref[a:a + sub, :] = (gelu * c[1]).astype(BF16)
        for s in range(2):
            carry[j, s] = halo[s]

    tile = pl.BlockSpec((tm, tn), lambda i, j: (i, j))
    vec = lambda rows, off: pl.BlockSpec((rows, tn), lambda i, j: (0, j + off))
    return pl.pallas_call(
        body, name=name, grid=(S // tm, nj),
        in_specs=[pl.BlockSpec((tm, K), lambda i, j: (i, 0)),
                  pl.BlockSpec((tn, K), lambda i, j: (j, 0)), pl.BlockSpec((tn, K), lambda i, j: (j + nj, 0)),
                  vec(3, 0), vec(3, nj), vec(1, 0), vec(1, nj)],
        out_specs=[tile] * 5,
        out_shape=[jax.ShapeDtypeStruct((S, F), BF16)] * 5,
        scratch_shapes=[pltpu.VMEM((nj, 2, F32_ROWS, tn), F32)],
        compiler_params=_cp("arbitrary", "arbitrary"),
    )(h, w_up_t, w_up_t, conv_w, conv_w, conv_b, conv_b)


def _ffn_glu_bwd(u_g, u_v, c_g, c_v, df, w_down, h, conv_w, *, name, tm=512, tn=256, sub=256):
    S = u_g.shape[0]
    F = D_FF
    D = df.shape[1]
    nj = F // tn
    nt = S // tm

    def body(ug_ref, uv_ref, cg_ref, cgn_ref, cv_ref, cvn_ref, df_ref, dfn_ref, wd_ref, h_ref, wg_ref, wv_ref,
             dug_ref, duv_ref, gug_ref, guv_ref, gd_ref, gwg_ref, gwv_ref, gbg_ref, gbv_ref,
             acc_u, acc_d):
        i = pl.program_id(1)

        @pl.when(i == 0)
        def _():
            for r in (gwg_ref, gwv_ref, gbg_ref, gbv_ref, acc_u, acc_d):
                r[...] = jnp.zeros_like(r)

        wg, wv = wg_ref[...], wv_ref[...]
        wd = wd_ref[...]
        dfn = jnp.where(i < nt - 1, dfn_ref[...], jnp.zeros_like(dfn_ref))
        n_ext = sub + HALO

        def ahead(dc):
            return dc[:sub], pltpu.roll(dc, n_ext - 1, 0)[:sub], pltpu.roll(dc, n_ext - 2, 0)[:sub]

        for a in range(0, tm, sub):
            b = a + sub
            ext = lambda ref, nxt: jnp.concatenate(
                [ref[a:b, :], ref[b:b + HALO, :] if b < tm else nxt], axis=0)
            cg = ext(cg_ref, cgn_ref[...]).astype(F32)
            cv = ext(cv_ref, cvn_ref[...]).astype(F32)
            df_sub = df_ref[a:b, :]
            dy_ext = _dot(ext(df_ref, dfn), wd, NT)
            gelu, dgelu = _gelu_parts(cg)
            dcs_g = ahead(dy_ext * cv * dgelu)
            dcs_v = ahead(dy_ext * gelu)
            du_g = (dcs_g[0] * wg[2:3] + dcs_g[1] * wg[1:2] + dcs_g[2] * wg[0:1]).astype(BF16)
            du_v = (dcs_v[0] * wv[2:3] + dcs_v[1] * wv[1:2] + dcs_v[2] * wv[0:1]).astype(BF16)
            dug_ref[a:b, :] = du_g
            duv_ref[a:b, :] = du_v
            acc_u[...] += _dot(jnp.concatenate([du_g, du_v], axis=1), h_ref[a:b, :], TN)
            acc_d[...] += _dot((gelu[:sub] * cv[:sub]).astype(BF16), df_sub, TN)
            for dcs, u_ref, gw_ref, gb_ref in ((dcs_g, ug_ref, gwg_ref, gbg_ref), (dcs_v, uv_ref, gwv_ref, gbv_ref)):
                u = u_ref[a:b, :].astype(F32)
                gb_ref[...] += jnp.sum(dcs[0], axis=0, keepdims=True)
                for k in range(3):
                    gw_ref[k:k + 1, :] += jnp.sum(dcs[2 - k] * u, axis=0, keepdims=True)

        @pl.when(i == nt - 1)
        def _():
            gug_ref[...] = acc_u[:tn, :].astype(BF16)
            guv_ref[...] = acc_u[tn:, :].astype(BF16)
            gd_ref[...] = acc_d[...].astype(BF16)

    per = tm // HALO
    nh = S // HALO
    hnext = lambda i: jnp.minimum((i + 1) * per, nh - 1)
    tile = pl.BlockSpec((tm, tn), lambda j, i: (i, j))
    hn = pl.BlockSpec((HALO, tn), lambda j, i: (hnext(i), j))
    vec = lambda rows, off: pl.BlockSpec((rows, tn), lambda j, i: (0, j + off))
    wide = pl.BlockSpec((tm, D), lambda j, i: (i, 0))
    wrow = pl.BlockSpec((tn, D), lambda j, i: (j, 0))
    return pl.pallas_call(
        body, name=name, grid=(nj, nt),
        in_specs=[tile, tile, tile, hn, tile, hn, wide, pl.BlockSpec((HALO, D), lambda j, i: (hnext(i), 0)),
                  wrow, wide, vec(3, 0), vec(3, nj)],
        out_specs=[tile, tile, wrow, wrow, wrow, vec(3, 0), vec(3, 0), vec(1, 0), vec(1, 0)],
        out_shape=[jax.ShapeDtypeStruct((S, F), BF16), jax.ShapeDtypeStruct((S, F), BF16),
                   jax.ShapeDtypeStruct((F, D), BF16), jax.ShapeDtypeStruct((F, D), BF16),
                   jax.ShapeDtypeStruct((F, D), BF16),
                   jax.ShapeDtypeStruct((3, F), F32), jax.ShapeDtypeStruct((3, F), F32),
                   jax.ShapeDtypeStruct((1, F), F32), jax.ShapeDtypeStruct((1, F), F32)],
        scratch_shapes=[pltpu.VMEM((2 * tn, D), F32), pltpu.VMEM((tn, D), F32)],
        compiler_params=_cp("parallel", "arbitrary"),
    )(u_g, u_v, c_g, c_g, c_v, c_v, df, df, w_down, h, conv_w, conv_w)


def _sum_partials(parts, *, name, tr):
    _, R, C = parts.shape

    def body(p_ref, o_ref):
        tot = p_ref[0].astype(F32)
        for j in range(1, N_DEV):
            tot = tot + p_ref[j].astype(F32)
        o_ref[...] = tot

    return pl.pallas_call(
        body, name=name, grid=(R // tr,),
        in_specs=[pl.BlockSpec((N_DEV, tr, C), lambda i: (0, i, 0))],
        out_specs=pl.BlockSpec((tr, C), lambda i: (i, 0)),
        out_shape=jax.ShapeDtypeStruct((R, C), F32),
        compiler_params=_cp("parallel"),
    )(parts)


def _adamw(w, g, m, v, *, name, tr):
    R, C = w.shape
    c1 = 1.0 - ADAM_B1 ** ADAM_STEP
    c2 = 1.0 - ADAM_B2 ** ADAM_STEP

    def body(w_ref, g_ref, m_ref, v_ref, d_ref, nm_ref, nv_ref):
        g = g_ref[...]
        nm = ADAM_B1 * m_ref[...] + (1.0 - ADAM_B1) * g
        nv = ADAM_B2 * v_ref[...] + (1.0 - ADAM_B2) * (g * g)
        d_ref[...] = -ADAM_LR * ((nm / c1) / (jnp.sqrt(nv / c2) + ADAM_EPS) + ADAM_WD * w_ref[...])
        nm_ref[...] = nm
        nv_ref[...] = nv

    spec = pl.BlockSpec((tr, C), lambda i: (i, 0))
    return pl.pallas_call(
        body, name=name, grid=(R // tr,), in_specs=[spec] * 4, out_specs=[spec] * 3,
        out_shape=[jax.ShapeDtypeStruct((R, C), F32)] * 3,
        compiler_params=_cp("parallel"),
    )(w, g, m, v)


def _mesh_pos():
    return lax.axis_index("x"), lax.axis_index("y"), lax.axis_index("c")


def _gather_phases(x_ref, out_ref, send_sems, recv_sems, local_sem):
    x, y, c = _mesh_pos()
    me, sibling = (x, y, c), (x, y, 1 - c)
    chips = [(1 - x, y), (x, 1 - y), (1 - x, 1 - y)]

    def slot(px, py, pc):
        return out_ref.at[4 * px + 2 * py + pc]

    def copy(k, block, to, src=None):
        return pltpu.make_async_remote_copy(
            src_ref=slot(*block) if src is None else src, dst_ref=slot(*block),
            send_sem=send_sems.at[k], recv_sem=recv_sems.at[k], device_id=to, device_id_type=MESH)

    mine = pltpu.make_async_copy(x_ref, slot(*me), local_sem)
    first = [copy(0, me, sibling, src=x_ref)]
    first += [copy(1 + j, me, (*chip, c), src=x_ref) for j, chip in enumerate(chips)]
    passed = [copy(4 + j, (*chip, c), sibling) for j, chip in enumerate(chips)]

    def start():
        mine.start()
        for cp in first:
            cp.start()

    def forward():
        for j, chip in enumerate(chips):
            copy(1 + j, (*chip, c), me).wait_recv()
            passed[j].start()

    def finish():
        copy(0, sibling, me).wait_recv()
        for j, chip in enumerate(chips):
            copy(4 + j, (*chip, 1 - c), me).wait_recv()
        for cp in first + passed:
            cp.wait_send()
        mine.wait()

    return start, forward, finish


def _two_level_gather(x_ref, out_ref, send_sems, recv_sems, local_sem):
    for phase in _gather_phases(x_ref, out_ref, send_sems, recv_sems, local_sem):
        phase()


_GATHER_SEMS = [pltpu.SemaphoreType.DMA((7,)), pltpu.SemaphoreType.DMA((7,)), pltpu.SemaphoreType.DMA]


def _all_gather_hbm(block, *, name):
    def body(x_ref, out_ref, send_sems, recv_sems, local_sem):
        _two_level_gather(x_ref, out_ref, send_sems, recv_sems, local_sem)

    return pl.pallas_call(
        body, name=name, in_specs=[ANY], out_specs=ANY,
        out_shape=jax.ShapeDtypeStruct((N_DEV,) + block.shape, block.dtype),
        scratch_shapes=_GATHER_SEMS,
    )(block)


def _all_reduce_small(block, *, name):
    def body(x_ref, all_ref, sum_ref, send_sems, recv_sems, local_sem):
        _two_level_gather(x_ref, all_ref, send_sems, recv_sems, local_sem)
        tot = all_ref[0]
        for j in range(1, N_DEV):
            tot = tot + all_ref[j]
        sum_ref[...] = tot

    return pl.pallas_call(
        body, name=name, in_specs=[VMEM], out_specs=[VMEM, VMEM],
        out_shape=[jax.ShapeDtypeStruct((N_DEV,) + block.shape, block.dtype),
                   jax.ShapeDtypeStruct(block.shape, block.dtype)],
        scratch_shapes=_GATHER_SEMS,
        compiler_params=pltpu.CompilerParams(vmem_limit_bytes=V7X_VMEM_LIMIT),
    )(block)[1]


def _exchange_phases(g_refs, r_refs, send_sems, recv_sems, local_sems):
    x, y, c = _mesh_pos()
    me = 4 * x + 2 * y + c
    owns, remote = [], []
    for k, (g_ref, r_ref) in enumerate(zip(g_refs, r_refs)):
        rows = g_ref.shape[0] // N_DEV
        owns.append(pltpu.make_async_copy(g_ref.at[pl.ds(me * rows, rows)], r_ref.at[me], local_sems.at[k]))
        for p in range(1, N_DEV):
            px, py, pc = x ^ (p >> 2), y ^ ((p >> 1) & 1), c ^ (p & 1)
            peer = 4 * px + 2 * py + pc
            link = dict(send_sem=send_sems.at[k, p], recv_sem=recv_sems.at[k, p],
                        device_id=(px, py, pc), device_id_type=MESH)
            src = g_ref.at[pl.ds(peer * rows, rows)]
            send = pltpu.make_async_remote_copy(src_ref=src, dst_ref=r_ref.at[me], **link)
            arrival = pltpu.make_async_remote_copy(src_ref=src, dst_ref=r_ref.at[peer], **link)
            remote.append((send, arrival))

    def start():
        for own in owns:
            own.start()
        for send, _ in remote:
            send.start()

    def finish():
        for _, arrival in remote:
            arrival.wait_recv()
        for send, _ in remote:
            send.wait_send()
        for own in owns:
            own.wait()

    return start, finish


def _exchange_buffers(grads):
    n = len(grads)
    shapes = [jax.ShapeDtypeStruct((N_DEV, g.shape[0] // N_DEV, g.shape[1]), g.dtype) for g in grads]
    sems = [pltpu.SemaphoreType.DMA((n, N_DEV)), pltpu.SemaphoreType.DMA((n, N_DEV)),
            pltpu.SemaphoreType.DMA((n,))]
    return shapes, sems


def _unpack_gathered(gathered):
    n_out, n_up, n_down = D_MODEL, 2 * D_FF, D_FF
    offs, row = [], 0
    for n in (n_out, n_up, n_down):
        offs.append((row, row + n // N_DEV))
        row += n // N_DEV
    w_out, w_up_t, w_down = (gathered[:, a:b].reshape(-1, D_MODEL) for a, b in offs)
    n_cw = 3 * (2 * D_FF // N_DEV)
    cw_all = gathered[:, row:].reshape(N_DEV, -1)[:, :2 * n_cw].reshape(N_DEV, n_cw, 2)
    cw_all = lax.bitcast_convert_type(cw_all, F32).reshape(N_DEV, 3, -1)
    conv_w = jnp.transpose(cw_all, (1, 0, 2)).reshape(3, 2 * D_FF)
    return w_out, w_up_t, w_down, conv_w


def _rest_payload(w_out, w_up, w_down, conv_w):
    cw_bits = lax.bitcast_convert_type(conv_w.reshape(-1), BF16).reshape(-1)
    cw_bits = jnp.pad(cw_bits, (0, HALO * D_MODEL - cw_bits.shape[0])).reshape(HALO, D_MODEL)
    return jnp.concatenate([w_out.astype(BF16), w_up.T.astype(BF16), w_down.astype(BF16), cw_bits], axis=0)


def _device_step(x, target, g_mix_pre, w_in_t, rest_payload, pool_w, pool_scale, g_mix_post, g_ffn_pre,
                 conv_b, g_ffn_post):
    h1 = _rms_norm(x, g_mix_pre, name="rms_mix_pre")
    proj = _matmul(h1, w_in_t, trans_b=True, out_dtype=F32, tm=1024, tn=512, name="proj")
    attn, lse, attn16, gathered = _attn_fwd(proj, rest_payload, name="attn_fwd")
    w_out, w_up_t, w_down, conv_w = _unpack_gathered(gathered)
    pool = _pool_fwd(proj, 3, pool_w, pool_scale, name="pool_fwd")
    mixed, x2, h2 = _mix_out(attn16, pool, w_out, x, g_mix_post, g_ffn_pre, name="mix_out")
    u_g, u_v, c_g, c_v, y = _ffn_up_glu(h2, w_up_t, conv_w, conv_b, name="ffn_up_glu")
    df, d_out, loss_blk, gg_ffn_post = _ffn_out(y, w_down, x2, target, g_ffn_post, name="ffn_out")
    du_g, du_v, gw_up_g, gw_up_v, gw_down, gcw_g, gcw_v, gcb_g, gcb_v = _ffn_glu_bwd(
        u_g, u_v, c_g, c_v, df, w_down, h2, conv_w, name="ffn_glu_bwd")
    gw_up_t = jnp.concatenate([gw_up_g, gw_up_v], axis=0)
    dx2, gg_ffn_pre, dmixed, gg_mix_post = _dgrad_norm(
        [du_g, du_v], w_up_t, d_out, x2, g_ffn_pre, (mixed, g_mix_post), [], tk=1408, name="ffn_up_dgrad")
    gw_out = jnp.concatenate([_matmul_tn(attn16, dmixed, ta=512, ts=1024, name="grad_w_out_attn"),
                              _matmul_tn(pool, dmixed, ta=512, ts=1024, name="grad_w_out_pool")], axis=0)
    dcat = _matmul(dmixed, w_out, trans_b=True, out_dtype=F32, tm=512, tn=1024, name="mix_out_dgrad")
    d_pool_in, g_pool_w, g_pool_scale = _pool_bwd(proj, 3, dcat, 1, pool_w, pool_scale, name="pool_bwd")
    dqkv, (r_out, r_up_t, r_down) = _attn_bwd(proj, dcat, attn, lse, [gw_out, gw_up_t, gw_down], name="attn_bwd")
    dproj = list(dqkv) + [d_pool_in]
    gw_in_t = jnp.concatenate([_matmul_tn(a, h1, ta=512, ts=1024, name=f"grad_w_in_{k}")
                               for k, a in enumerate(dproj)], axis=0)
    grad_x, gg_mix_pre, (r_in_t,) = _dgrad_norm(dproj, w_in_t, dx2, x, g_mix_pre, None, [gw_in_t], tk=512,
                                                name="proj_dgrad")
    g_conv_w = jnp.concatenate([gcw_g, gcw_v], axis=1)
    g_conv_b = jnp.concatenate([gcb_g, gcb_v], axis=1)
    received = (r_in_t, r_out, r_up_t, r_down)
    small = dict(g_mix_pre=gg_mix_pre, g_mix_post=gg_mix_post, g_ffn_pre=gg_ffn_pre, g_ffn_post=gg_ffn_post,
                 pool_scale=g_pool_scale, conv_b=g_conv_b, pool_w=g_pool_w, conv_w=g_conv_w)
    return loss_blk, grad_x, received, small


_SMALL = ("g_mix_pre", "g_mix_post", "g_ffn_pre", "g_ffn_post", "pool_scale", "conv_b", "pool_w")
LANES = 128


def _pack_rows(arrays):
    parts = []
    for a in arrays:
        a2 = a.reshape(-1, LANES)
        parts.append(jnp.pad(a2, ((0, (-a2.shape[0]) % 8), (0, 0))))
    return jnp.concatenate(parts, axis=0)


def _unpack_rows(packed, shapes):
    out, row = [], 0
    for shape in shapes:
        rows = math.prod(shape) // LANES
        out.append(packed[row:row + rows].reshape(shape))
        row += -(-rows // 8) * 8
    return out


def kernel(x, g_mix_pre, w_in, pool_w, pool_scale, w_out, g_mix_post, g_ffn_pre, w_up, conv_w, conv_b, w_down, g_ffn_post, loss_target, m_g_mix_pre, m_w_in, m_pool_w, m_pool_scale, m_w_out, m_g_mix_post, m_g_ffn_pre, m_w_up, m_conv_w, m_conv_b, m_w_down, m_g_ffn_post, v_g_mix_pre, v_w_in, v_pool_w, v_pool_scale, v_w_out, v_g_mix_post, v_g_ffn_pre, v_w_up, v_conv_w, v_conv_b, v_w_down, v_g_ffn_post):
    me = 4 * lax.axis_index("x") + 2 * lax.axis_index("y") + lax.axis_index("c")
    w_in_t = _all_gather_hbm(w_in[0].T.astype(BF16), name="gather_w_in").reshape(4 * ATTN_WIDTH, D_MODEL)
    loss_blk, grad_x, recv, small = _device_step(
        x[0], loss_target[0], g_mix_pre, w_in_t, _rest_payload(w_out[0], w_up[0], w_down[0], conv_w[0]),
        pool_w[0], pool_scale, g_mix_post, g_ffn_pre, conv_b, g_ffn_post)
    loss = lax.psum(loss_blk[0, 0], ("x", "y", "c"))

    g_in_t, g_out, g_up_t, g_down = (
        _sum_partials(r, name=f"sum_partials_{k}", tr=r.shape[1] // 2) for k, r in enumerate(recv))
    grads = {"w_in": g_in_t.T, "w_out": g_out, "w_up": g_up_t.T, "w_down": g_down}

    given = dict(g_mix_pre=g_mix_pre, g_mix_post=g_mix_post, g_ffn_pre=g_ffn_pre, g_ffn_post=g_ffn_post,
                 pool_scale=pool_scale, conv_b=conv_b, pool_w=pool_w)
    small_shapes = [given[k].shape for k in _SMALL]
    total = _all_reduce_small(_pack_rows([small[k] for k in _SMALL] + [small["conv_w"]]), name="all_reduce_small")
    *small_grads, g_conv_w_all = _unpack_rows(total, small_shapes + [(3, 2 * D_FF)])
    grads.update(zip(_SMALL, small_grads))
    width = 2 * D_FF // N_DEV
    grads["conv_w"] = lax.dynamic_slice_in_dim(g_conv_w_all, me * width, width, axis=1)[None]

    weights = dict(g_mix_pre=g_mix_pre, w_in=w_in, pool_w=pool_w, pool_scale=pool_scale, w_out=w_out,
                   g_mix_post=g_mix_post, g_ffn_pre=g_ffn_pre, w_up=w_up, conv_w=conv_w, conv_b=conv_b,
                   w_down=w_down, g_ffn_post=g_ffn_post)
    m_in = dict(g_mix_pre=m_g_mix_pre, w_in=m_w_in, pool_w=m_pool_w, pool_scale=m_pool_scale, w_out=m_w_out,
                g_mix_post=m_g_mix_post, g_ffn_pre=m_g_ffn_pre, w_up=m_w_up, conv_w=m_conv_w, conv_b=m_conv_b,
                w_down=m_w_down, g_ffn_post=m_g_ffn_post)
    v_in = dict(g_mix_pre=v_g_mix_pre, w_in=v_w_in, pool_w=v_pool_w, pool_scale=v_pool_scale, w_out=v_w_out,
                g_mix_post=v_g_mix_post, g_ffn_pre=v_g_ffn_pre, w_up=v_w_up, conv_w=v_conv_w, conv_b=v_conv_b,
                w_down=v_w_down, g_ffn_post=v_g_ffn_post)
    delta, new_m, new_v = {}, {}, {}
    for k in ("w_in", "w_out", "w_up", "w_down"):
        g = grads[k]
        d, nm, nv = _adamw(weights[k][0], g, m_in[k][0], v_in[k][0], name=f"adamw_{k}", tr=g.shape[0] // 2)
        grads[k], delta[k], new_m[k], new_v[k] = g[None], d[None], nm[None], nv[None]
    d, nm, nv = _adamw(weights["conv_w"][0], grads["conv_w"][0], m_in["conv_w"][0], v_in["conv_w"][0],
                       name="adamw_conv_w", tr=3)
    delta["conv_w"], new_m["conv_w"], new_v["conv_w"] = d[None], nm[None], nv[None]
    packed_w = _pack_rows([weights[k] for k in _SMALL])
    small_rows = packed_w.shape[0]
    d, nm, nv = _adamw(packed_w, total[:small_rows], _pack_rows([m_in[k] for k in _SMALL]),
                       _pack_rows([v_in[k] for k in _SMALL]), name="adamw_small", tr=small_rows)
    for k, dk, mk, vk in zip(_SMALL, _unpack_rows(d, small_shapes), _unpack_rows(nm, small_shapes),
                             _unpack_rows(nv, small_shapes)):
        delta[k], new_m[k], new_v[k] = dk, mk, vk

    order = ("g_mix_pre", "w_in", "pool_w", "pool_scale", "w_out", "g_mix_post", "g_ffn_pre", "w_up",
             "conv_w", "conv_b", "w_down", "g_ffn_post")
    return (loss, grad_x[None], *[grads[k] for k in order], *[delta[k] for k in order],
            *[new_m[k] for k in order], *[new_v[k] for k in order])
```

```python
import functools
import math

import jax
import jax.numpy as jnp
from jax import lax
from jax.experimental import pallas as pl
from jax.experimental.pallas import tpu as pltpu

F32 = jnp.float32
BF16 = jnp.bfloat16

D_MODEL = 1024
ATTN_WIDTH = 512
N_HEADS = 8
HEAD_DIM = 64
DILATIONS = (1, 4, 16)
BLOCK = 128
POOL_WIDTH = 512
POOL_WINDOWS = (2, 4, 8, 16)
POOL_GROUP_DIM = 128
D_FF = 2816
EPS = 1e-6
NEG_INF = -1e30
SCALE = HEAD_DIM ** -0.5

ADAM_LR = 0.001
ADAM_B1 = 0.9
ADAM_B2 = 0.999
ADAM_EPS = 1e-08
ADAM_WD = 0.01
ADAM_STEP = 10

N_DEV = 8
HALO = 16
V7X_VMEM_LIMIT = 56 * 1024 * 1024

MESH = pl.DeviceIdType.MESH
ANY = pl.BlockSpec(memory_space=pl.ANY)
VMEM = pl.BlockSpec(memory_space=pltpu.VMEM)

NT = (((1,), (1,)), ((), ()))
NN = (((1,), (0,)), ((), ()))
TN = (((0,), (0,)), ((), ()))


def _cp(*sem):
    return pltpu.CompilerParams(dimension_semantics=sem, vmem_limit_bytes=V7X_VMEM_LIMIT)


def _dot(a, b, dn):
    return lax.dot_general(a, b, dn, preferred_element_type=F32)


def _rms_bwd(xin, g, dy):
    r = lax.rsqrt(jnp.mean(xin * xin, axis=-1, keepdims=True) + EPS)
    xh = xin * r
    gdy = g * dy
    dx = r * (gdy - xh * jnp.mean(gdy * xh, axis=-1, keepdims=True))
    dg = jnp.sum(dy * xh, axis=0, keepdims=True)
    return dx, dg


def _rms_norm(x, g, *, name, tm=512):
    S, D = x.shape

    def body(x_ref, g_ref, o_ref):
        xv = x_ref[...]
        r = lax.rsqrt(jnp.mean(xv * xv, axis=-1, keepdims=True) + EPS)
        o_ref[...] = (xv * r * g_ref[...]).astype(BF16)

    return pl.pallas_call(
        body, name=name, grid=(S // tm,),
        in_specs=[pl.BlockSpec((tm, D), lambda i: (i, 0)), pl.BlockSpec((1, D), lambda i: (0, 0))],
        out_specs=pl.BlockSpec((tm, D), lambda i: (i, 0)),
        out_shape=jax.ShapeDtypeStruct((S, D), BF16),
        compiler_params=_cp("parallel"),
    )(x, g)


def _matmul(a, b, *, trans_b, out_dtype, tm, tn, name):
    M, K = a.shape
    N = b.shape[0] if trans_b else b.shape[1]
    dn = NT if trans_b else NN

    def body(a_ref, b_ref, o_ref):
        o_ref[...] = _dot(a_ref[...], b_ref[...], dn).astype(out_dtype)

    b_spec = (pl.BlockSpec((tn, K), lambda i, j: (j, 0)) if trans_b
              else pl.BlockSpec((K, tn), lambda i, j: (0, j)))
    return pl.pallas_call(
        body, name=name, grid=(M // tm, N // tn),
        in_specs=[pl.BlockSpec((tm, K), lambda i, j: (i, 0)), b_spec],
        out_specs=pl.BlockSpec((tm, tn), lambda i, j: (i, j)),
        out_shape=jax.ShapeDtypeStruct((M, N), out_dtype),
        compiler_params=_cp("parallel", "parallel"),
    )(a, b)


def _matmul_tn(a, b, *, ta, ts, name):
    S, Ka = a.shape
    Nb = b.shape[1]
    ns = S // ts

    def body(a_ref, b_ref, o_ref, acc):
        s = pl.program_id(1)

        @pl.when(s == 0)
        def _():
            acc[...] = jnp.zeros_like(acc)

        acc[...] += _dot(a_ref[...], b_ref[...], TN)

        @pl.when(s == ns - 1)
        def _():
            o_ref[...] = acc[...].astype(BF16)

    return pl.pallas_call(
        body, name=name, grid=(Ka // ta, ns),
        in_specs=[pl.BlockSpec((ts, ta), lambda i, s: (s, i)), pl.BlockSpec((ts, Nb), lambda i, s: (s, 0))],
        out_specs=pl.BlockSpec((ta, Nb), lambda i, s: (i, 0)),
        out_shape=jax.ShapeDtypeStruct((Ka, Nb), BF16),
        scratch_shapes=[pltpu.VMEM((ta, Nb), F32)],
        compiler_params=_cp("parallel", "arbitrary"),
    )(a, b)


def _mix_out(attn, pool, w_out, x, g_post, g_next, *, name, tm=256):
    S, K = attn.shape
    D = w_out.shape[1]

    def body(a_ref, p_ref, w_ref, x_ref, gp_ref, gn_ref, mixed_ref, x2_ref, h2_ref):
        mixed = _dot(a_ref[...], w_ref[:K, :], NN) + _dot(p_ref[...], w_ref[K:, :], NN)
        r = lax.rsqrt(jnp.mean(mixed * mixed, axis=-1, keepdims=True) + EPS)
        x2 = x_ref[...] + mixed * r * gp_ref[...]
        r2 = lax.rsqrt(jnp.mean(x2 * x2, axis=-1, keepdims=True) + EPS)
        mixed_ref[...] = mixed
        x2_ref[...] = x2
        h2_ref[...] = (x2 * r2 * gn_ref[...]).astype(BF16)

    row = lambda i: (i, 0)
    fix = lambda i: (0, 0)
    return pl.pallas_call(
        body, name=name, grid=(S // tm,),
        in_specs=[pl.BlockSpec((tm, K), row), pl.BlockSpec((tm, K), row), pl.BlockSpec((2 * K, D), fix),
                  pl.BlockSpec((tm, D), row), pl.BlockSpec((1, D), fix), pl.BlockSpec((1, D), fix)],
        out_specs=[pl.BlockSpec((tm, D), row)] * 3,
        out_shape=[jax.ShapeDtypeStruct((S, D), F32), jax.ShapeDtypeStruct((S, D), F32),
                   jax.ShapeDtypeStruct((S, D), BF16)],
        compiler_params=_cp("parallel"),
    )(attn, pool, w_out, x, g_post, g_next)


def _ffn_out(y, w_down, x2, target, g_post, *, name, tm=256):
    S, K = y.shape
    D = w_down.shape[1]

    def body(y_ref, w_ref, x2_ref, t_ref, g_ref, df_ref, dout_ref, loss_ref, gg_ref):
        i = pl.program_id(0)

        @pl.when(i == 0)
        def _():
            loss_ref[...] = jnp.zeros_like(loss_ref)
            gg_ref[...] = jnp.zeros_like(gg_ref)

        f = _dot(y_ref[...], w_ref[...], NN)
        g = g_ref[...]
        r = lax.rsqrt(jnp.mean(f * f, axis=-1, keepdims=True) + EPS)
        out = x2_ref[...] + f * r * g
        err = out - t_ref[...]
        dy = err * (1.0 / D)
        df, dg = _rms_bwd(f, g, dy)
        df_ref[...] = df.astype(BF16)
        dout_ref[...] = dy
        gg_ref[...] += dg
        loss_ref[...] += 0.5 * jnp.sum(jnp.mean(err * err, axis=-1, keepdims=True))

    row = lambda i: (i, 0)
    fix = lambda i: (0, 0)
    return pl.pallas_call(
        body, name=name, grid=(S // tm,),
        in_specs=[pl.BlockSpec((tm, K), row), pl.BlockSpec((K, D), fix), pl.BlockSpec((tm, D), row),
                  pl.BlockSpec((tm, D), row), pl.BlockSpec((1, D), fix)],
        out_specs=[pl.BlockSpec((tm, D), row), pl.BlockSpec((tm, D), row),
                   pl.BlockSpec((8, 128), fix), pl.BlockSpec((1, D), fix)],
        out_shape=[jax.ShapeDtypeStruct((S, D), BF16), jax.ShapeDtypeStruct((S, D), F32),
                   jax.ShapeDtypeStruct((8, 128), F32), jax.ShapeDtypeStruct((1, D), F32)],
        compiler_params=_cp("arbitrary"),
    )(y, w_down, x2, target, g_post)


def _dgrad_norm(a_list, w, resid, xin, g, second, exchange, *, tk, name, tm=512):
    S, Kp = a_list[0].shape
    na = len(a_list)
    D = w.shape[1]
    kper = Kp // tk
    nk = na * kper
    nt = S // tm
    two = second is not None
    ng = len(exchange)
    recv_shapes, exchange_sems = _exchange_buffers(exchange)

    def body(*refs):
        a_refs = refs[:na]
        w_ref, r_ref, x_ref, g_ref = refs[na:na + 4]
        pos = na + 4
        if two:
            x2_ref, g2_ref = refs[pos:pos + 2]
            pos += 2
        g_refs = refs[pos:pos + ng]
        pos += ng
        dx_ref, gg_ref = refs[pos:pos + 2]
        pos += 2
        if two:
            d2_ref, gg2_ref = refs[pos:pos + 2]
            pos += 2
        r_refs = refs[pos:pos + ng]
        pos += ng
        acc = refs[pos]
        i = pl.program_id(0)
        k = pl.program_id(1)
        if ng:
            start, finish = _exchange_phases(g_refs, r_refs, *refs[pos + 1:])
            pl.when((i == 0) & (k == 0))(start)

        @pl.when(k == 0)
        def _():
            acc[...] = jnp.zeros_like(acc)

        @pl.when((i == 0) & (k == 0))
        def _():
            gg_ref[...] = jnp.zeros_like(gg_ref)
            if two:
                gg2_ref[...] = jnp.zeros_like(gg2_ref)

        for q in range(na):
            @pl.when(k // kper == q)
            def _(q=q):
                acc[...] += _dot(a_refs[q][...], w_ref[...], NN)

        @pl.when(k == nk - 1)
        def _():
            d1, dg1 = _rms_bwd(x_ref[...], g_ref[...], acc[...])
            dx = r_ref[...] + d1
            dx_ref[...] = dx
            gg_ref[...] += dg1
            if two:
                d2, dg2 = _rms_bwd(x2_ref[...], g2_ref[...], dx)
                d2_ref[...] = d2.astype(BF16)
                gg2_ref[...] += dg2

        if ng:
            pl.when((i == nt - 1) & (k == nk - 1))(finish)

    row = lambda i, k: (i, 0)
    fix = lambda i, k: (0, 0)
    a_specs = [pl.BlockSpec((tm, tk), functools.partial(
        lambda i, k, q: (i, jnp.clip(k - q * kper, 0, kper - 1)), q=q)) for q in range(na)]
    in_specs = a_specs + [pl.BlockSpec((tk, D), lambda i, k: (k, 0)), pl.BlockSpec((tm, D), row),
                          pl.BlockSpec((tm, D), row), pl.BlockSpec((1, D), fix)]
    args = list(a_list) + [w, resid, xin, g]
    out_specs = [pl.BlockSpec((tm, D), row), pl.BlockSpec((1, D), fix)]
    out_shape = [jax.ShapeDtypeStruct((S, D), F32), jax.ShapeDtypeStruct((1, D), F32)]
    if two:
        in_specs += [pl.BlockSpec((tm, D), row), pl.BlockSpec((1, D), fix)]
        args += list(second)
        out_specs += [pl.BlockSpec((tm, D), row), pl.BlockSpec((1, D), fix)]
        out_shape += [jax.ShapeDtypeStruct((S, D), BF16), jax.ShapeDtypeStruct((1, D), F32)]
    n_plain = len(out_shape)
    out = pl.pallas_call(
        body, name=name, grid=(nt, nk), in_specs=in_specs + [ANY] * ng, out_specs=out_specs + [ANY] * ng,
        out_shape=out_shape + recv_shapes,
        scratch_shapes=[pltpu.VMEM((tm, D), F32)] + (exchange_sems if ng else []),
        compiler_params=_cp("arbitrary", "arbitrary"),
    )(*args, *exchange)
    return (*out[:n_plain], out[n_plain:]) if ng else out


def _band_mask(first_block):
    qi = lax.broadcasted_iota(jnp.int32, (BLOCK, 2 * BLOCK), 0)
    ki = lax.broadcasted_iota(jnp.int32, (BLOCK, 2 * BLOCK), 1)
    first_key = jnp.where(first_block, BLOCK, 0)
    return (ki >= qi) & (ki <= qi + BLOCK) & (ki >= first_key)


def _lane_masks():
    lane = lax.broadcasted_iota(jnp.int32, (1, 2 * HEAD_DIM), 1)
    return (lane < HEAD_DIM, lane >= HEAD_DIM)


CHUNK = BLOCK * max(DILATIONS)
SLAB = 2 * HEAD_DIM
N_SLABS = ATTN_WIDTH // SLAB


def _unit_rows(d, b):
    def rows(r):
        start = r + BLOCK * d * b
        return pl.ds(start, BLOCK, stride=d) if d > 1 else pl.ds(start, BLOCK)
    return rows


def _attn_units():
    for p, d in enumerate(DILATIONS):
        nbc = CHUNK // (BLOCK * d)
        for b in range(nbc):
            for r in range(d):
                yield p, d, b, r, nbc


def _attn_in_specs(nc, n_cur):
    prev = lambda c: jnp.maximum(jnp.minimum(c, nc - 1) - 1, 0)
    cur = lambda c: jnp.minimum(c, nc - 1)
    blk = lambda f: pl.BlockSpec((CHUNK, SLAB), f)
    specs = [blk(lambda h, c: (cur(c), h)),
             blk(lambda h, c: (prev(c), N_SLABS + h)), blk(lambda h, c: (cur(c), N_SLABS + h)),
             blk(lambda h, c: (prev(c), 2 * N_SLABS + h)), blk(lambda h, c: (cur(c), 2 * N_SLABS + h))]
    return specs + [blk(lambda h, c: (cur(c), h))] * n_cur


def _attn_fwd(proj, payload, *, name):
    S = proj.shape[0]
    nc = S // CHUNK
    n = len(DILATIONS)
    n_steps = N_SLABS * nc

    def body(q_ref, kp_ref, kc_ref, vp_ref, vc_ref, pay_ref, attn_ref, lse_ref, attn16_ref, all_ref, *scr):
        o_scr, l_scr = scr[:n], scr[n:2 * n]
        start, forward, finish = _gather_phases(pay_ref, all_ref, *scr[2 * n:])
        step = pl.program_id(0) * nc + pl.program_id(1)
        pl.when(step == 0)(start)
        c = pl.program_id(1)
        lms = _lane_masks()
        plain, first = (jnp.tile(_band_mask(f), (2, 1)) for f in (False, c == 0))
        for p, d, b, r, nbc in _attn_units():
            rows = _unit_rows(d, b)(r)
            prow = _unit_rows(d, (b - 1) % nbc)(r)
            kpr, vpr = (kc_ref, vc_ref) if b > 0 else (kp_ref, vp_ref)
            mask2 = plain if b > 0 else first
            q = q_ref[rows, :].astype(BF16)
            kcat = jnp.concatenate([kpr[prow, :], kc_ref[rows, :]], axis=0).astype(BF16)
            vcat = jnp.concatenate([vpr[prow, :], vc_ref[rows, :]], axis=0).astype(BF16)
            q2 = jnp.concatenate([jnp.where(lm, q, jnp.zeros_like(q)) for lm in lms], axis=0) * SCALE
            s = jnp.where(mask2, _dot(q2, kcat, NT), NEG_INF)
            m = jnp.max(s, axis=-1, keepdims=True)
            e = jnp.exp(s - m)
            l = jnp.sum(e, axis=-1, keepdims=True)
            o2 = _dot(e.astype(BF16), vcat, NN) / l
            lse2 = m + jnp.log(l)
            o_scr[p][rows, :] = jnp.where(lms[0], o2[:BLOCK], o2[BLOCK:])
            l_scr[p][rows, :] = jnp.where(lms[0], lse2[:BLOCK], lse2[BLOCK:])
        ls = [l_scr[p][...] for p in range(n)]
        top = functools.reduce(jnp.maximum, ls)
        es = [jnp.exp(l - top) for l in ls]
        den = functools.reduce(jnp.add, es)
        num = functools.reduce(jnp.add, [e * o_scr[p][...] for p, e in enumerate(es)])
        attn = num / den
        attn_ref[...] = attn
        attn16_ref[...] = attn.astype(BF16)
        lse_ref[...] = top + jnp.log(den)
        pl.when(step == (2 * n_steps) // 3)(forward)
        pl.when(step == n_steps - 1)(finish)

    return pl.pallas_call(
        body, name=name, grid=(N_SLABS, nc), in_specs=_attn_in_specs(nc, 0) + [ANY],
        out_specs=[pl.BlockSpec((CHUNK, SLAB), lambda h, c: (c, h))] * 3 + [ANY],
        out_shape=[jax.ShapeDtypeStruct((S, ATTN_WIDTH), F32)] * 2 + [jax.ShapeDtypeStruct((S, ATTN_WIDTH), BF16),
                   jax.ShapeDtypeStruct((N_DEV,) + payload.shape, payload.dtype)],
        scratch_shapes=[pltpu.VMEM((CHUNK, SLAB), F32)] * (2 * n) + _GATHER_SEMS,
        compiler_params=_cp("arbitrary", "arbitrary"),
    )(proj, proj, proj, proj, proj, payload)


def _attn_bwd(proj, dcat, attn, lse, grads, *, name):
    S = proj.shape[0]
    nc = S // CHUNK
    ng = len(grads)
    n = len(DILATIONS)
    recv_shapes, exchange_sems = _exchange_buffers(grads)

    def body(*refs):
        q_ref, kp_ref, kc_ref, vp_ref, vc_ref, do_ref, o_ref, lse_ref = refs[:8]
        g_refs = refs[8:8 + ng]
        dq_ref, dk_ref, dv_ref = refs[8 + ng:11 + ng]
        r_refs = refs[11 + ng:11 + 2 * ng]
        scr = refs[11 + 2 * ng:]
        dk_prev, dv_prev = scr[:2]
        delta_h, lse_h = scr[2:4], scr[4:6]
        dq_p, dk_own, dk_back, dv_own, dv_back = (scr[6 + n * k:6 + n * (k + 1)] for k in range(5))
        start, finish = _exchange_phases(g_refs, r_refs, *scr[6 + 5 * n:])
        c = pl.program_id(1)
        pl.when((pl.program_id(0) == 0) & (c == 0))(start)

        @pl.when(c == 0)
        def _():
            dk_prev[...] = jnp.zeros_like(dk_prev)
            dv_prev[...] = jnp.zeros_like(dv_prev)

        @pl.when(c < nc)
        def _():
            lms = _lane_masks()
            plain, first = (jnp.tile(_band_mask(f), (2, 1)) for f in (False, c == 0))
            prod = do_ref[...] * o_ref[...]
            lse = lse_ref[...]
            lse_other = pltpu.roll(lse, HEAD_DIM, 1)
            for h, lm in enumerate(lms):
                delta = jnp.sum(jnp.where(lm, prod, 0.0), axis=-1, keepdims=True)
                delta_h[h][...] = jnp.broadcast_to(delta, (CHUNK, SLAB))
                lse_h[h][...] = jnp.where(lm, lse, lse_other)
            wide = lambda refs, rows: jnp.tile(jnp.concatenate([r[rows, :] for r in refs], axis=0), (1, 2))
            stack = lambda f: jnp.concatenate([f(lm) for lm in lms], axis=0)
            for p, d, b, r, nbc in _attn_units():
                rows = _unit_rows(d, b)(r)
                prow = _unit_rows(d, (b - 1) % nbc)(r)
                kpr, vpr = (kc_ref, vc_ref) if b > 0 else (kp_ref, vp_ref)
                mask2 = plain if b > 0 else first
                q = q_ref[rows, :].astype(BF16)
                kcat = jnp.concatenate([kpr[prow, :], kc_ref[rows, :]], axis=0).astype(BF16)
                vcat = jnp.concatenate([vpr[prow, :], vc_ref[rows, :]], axis=0).astype(BF16)
                do = do_ref[rows, :]
                q2 = stack(lambda lm: jnp.where(lm, q, jnp.zeros_like(q))) * SCALE
                do2 = stack(lambda lm: jnp.where(lm, do, 0.0)).astype(BF16)
                e = jnp.where(mask2, jnp.exp(_dot(q2, kcat, NT) - wide(lse_h, rows)), 0.0)
                ds = (e * (_dot(do2, vcat, NT) - wide(delta_h, rows))).astype(BF16)
                dq = jnp.where(lms[0], _dot(ds[:BLOCK], kcat, NN), _dot(ds[BLOCK:], kcat, NN)) * SCALE
                dkc = _dot(ds, q2, TN)
                dvc = _dot(e.astype(BF16), do2, TN)
                dq_p[p][rows, :] = dq
                dk_own[p][rows, :] = dkc[BLOCK:]
                dv_own[p][rows, :] = dvc[BLOCK:]
                dk_back[p][prow, :] = dkc[:BLOCK]
                dv_back[p][prow, :] = dvc[:BLOCK]
            dq_ref[...] = functools.reduce(jnp.add, [r[...] for r in dq_p]).astype(BF16)
            for prev, own, back, out_ref in ((dk_prev, dk_own, dk_back, dk_ref), (dv_prev, dv_own, dv_back, dv_ref)):
                for p, d in enumerate(DILATIONS):
                    tail = CHUNK - BLOCK * d
                    prev[tail:, :] += back[p][tail:, :]
                out_ref[...] = prev[...].astype(BF16)
                prev[...] = functools.reduce(jnp.add, [r[...] for r in own])
                for p, d in enumerate(DILATIONS):
                    tail = CHUNK - BLOCK * d
                    if tail:
                        prev[:tail, :] += back[p][:tail, :]

        @pl.when(c == nc)
        def _():
            dk_ref[...] = dk_prev[...].astype(BF16)
            dv_ref[...] = dv_prev[...].astype(BF16)

        pl.when((pl.program_id(0) == N_SLABS - 1) & (c == nc))(finish)

    blk = lambda f: pl.BlockSpec((CHUNK, SLAB), f)
    late = lambda h, c: (jnp.maximum(c - 1, 0), h)
    out = pl.pallas_call(
        body, name=name, grid=(N_SLABS, nc + 1), in_specs=_attn_in_specs(nc, 3) + [ANY] * ng,
        out_specs=[blk(lambda h, c: (jnp.minimum(c, nc - 1), h)), blk(late), blk(late)] + [ANY] * ng,
        out_shape=[jax.ShapeDtypeStruct((S, ATTN_WIDTH), BF16)] * 3 + recv_shapes,
        scratch_shapes=[pltpu.VMEM((CHUNK, SLAB), F32)] * (6 + 5 * n) + exchange_sems,
        compiler_params=_cp("arbitrary", "arbitrary"),
    )(proj, proj, proj, proj, proj, dcat, attn, lse, *grads)
    return out[:3], out[3:]


def _split_bf16(a):
    hi = a.astype(BF16)
    lo = (a - hi.astype(F32)).astype(BF16)
    return hi, lo


def _pooled(ug, halo_g, w, row0, tm):
    ext = jnp.concatenate([halo_g, ug], axis=0)
    hi, lo = _split_bf16(ext)
    rr = lax.broadcasted_iota(jnp.int32, (tm, tm + HALO), 0)
    cc = lax.broadcasted_iota(jnp.int32, (tm, tm + HALO), 1)
    back = rr + HALO - cc
    win = ((back >= 0) & (back < w)).astype(BF16)
    wsum = _dot(win, hi, NN) + _dot(win, lo, NN)
    rows = row0 + lax.broadcasted_iota(jnp.int32, (tm, 1), 0)
    inv = 1.0 / jnp.minimum(rows + 1, w).astype(F32)
    return wsum * inv - ug


def _pool_fwd(u, u_col, pool_w, pool_scale, *, name, tm=256):
    S, W = u.shape[0], POOL_WIDTH
    G = POOL_GROUP_DIM

    def body(u_ref, h_ref, w_ref, s_ref, o_ref):
        i = pl.program_id(0)
        uv = u_ref[...]
        halo = jnp.where(i > 0, h_ref[...], 0.0)
        for g, w in enumerate(POOL_WINDOWS):
            sl = slice(g * G, (g + 1) * G)
            pooled = _pooled(uv[:, sl], halo[:, sl], w, i * tm, tm)
            z = _dot(pooled.astype(BF16), w_ref[g].astype(BF16), NN)
            o_ref[:, sl] = (z * s_ref[:, sl]).astype(BF16)

    per = tm // HALO
    return pl.pallas_call(
        body, name=name, grid=(S // tm,),
        in_specs=[pl.BlockSpec((tm, W), lambda i: (i, u_col)),
                  pl.BlockSpec((HALO, W), lambda i: (jnp.maximum(i * per - 1, 0), u_col)),
                  pl.BlockSpec((len(POOL_WINDOWS), G, G), lambda i: (0, 0, 0)),
                  pl.BlockSpec((1, W), lambda i: (0, 0))],
        out_specs=pl.BlockSpec((tm, W), lambda i: (i, 0)),
        out_shape=jax.ShapeDtypeStruct((S, W), BF16),
        compiler_params=_cp("parallel"),
    )(u, u, pool_w, pool_scale)


def _pool_bwd(u, u_col, dy, dy_col, pool_w, pool_scale, *, name, tm=256):
    S, W = u.shape[0], POOL_WIDTH
    G = POOL_GROUP_DIM
    nt = S // tm

    def body(u_ref, h_ref, dy_ref, dyn_ref, w_ref, s_ref, du_ref, gw_ref, gs_ref):
        i = pl.program_id(0)

        @pl.when(i == 0)
        def _():
            gw_ref[...] = jnp.zeros_like(gw_ref)
            gs_ref[...] = jnp.zeros_like(gs_ref)

        uv = u_ref[...]
        halo = jnp.where(i > 0, h_ref[...], 0.0)
        dyv = dy_ref[...]
        dyn = jnp.where(i < nt - 1, dyn_ref[...], 0.0)
        rr = lax.broadcasted_iota(jnp.int32, (tm, tm + HALO), 0)
        cc = lax.broadcasted_iota(jnp.int32, (tm, tm + HALO), 1)
        rows_ext = i * tm + lax.broadcasted_iota(jnp.int32, (tm + HALO, 1), 0)
        for g, w in enumerate(POOL_WINDOWS):
            sl = slice(g * G, (g + 1) * G)
            wg = w_ref[g].astype(BF16)
            sc = s_ref[:, sl]
            pooled = _pooled(uv[:, sl], halo[:, sl], w, i * tm, tm)
            z = _dot(pooled.astype(BF16), wg, NN)
            gs_ref[:, sl] += jnp.sum(dyv[:, sl] * z, axis=0, keepdims=True)
            dz = dyv[:, sl] * sc
            gw_ref[g] += _dot(pooled.astype(BF16), dz.astype(BF16), TN)
            dz_ext = jnp.concatenate([dz, dyn[:, sl] * sc], axis=0)
            dp_ext = _dot(dz_ext.astype(BF16), wg, NT)
            inv_ext = 1.0 / jnp.minimum(rows_ext + 1, w).astype(F32)
            hi, lo = _split_bf16(dp_ext * inv_ext)
            ahead = cc - rr
            win = ((ahead >= 0) & (ahead < w)).astype(BF16)
            du_ref[:, sl] = (_dot(win, hi, NN) + _dot(win, lo, NN) - dp_ext[:tm]).astype(BF16)

    per = tm // HALO
    nh = S // HALO
    return pl.pallas_call(
        body, name=name, grid=(nt,),
        in_specs=[pl.BlockSpec((tm, W), lambda i: (i, u_col)),
                  pl.BlockSpec((HALO, W), lambda i: (jnp.maximum(i * per - 1, 0), u_col)),
                  pl.BlockSpec((tm, W), lambda i: (i, dy_col)),
                  pl.BlockSpec((HALO, W), lambda i: (jnp.minimum((i + 1) * per, nh - 1), dy_col)),
                  pl.BlockSpec((len(POOL_WINDOWS), G, G), lambda i: (0, 0, 0)),
                  pl.BlockSpec((1, W), lambda i: (0, 0))],
        out_specs=[pl.BlockSpec((tm, W), lambda i: (i, 0)),
                   pl.BlockSpec((len(POOL_WINDOWS), G, G), lambda i: (0, 0, 0)),
                   pl.BlockSpec((1, W), lambda i: (0, 0))],
        out_shape=[jax.ShapeDtypeStruct((S, W), BF16),
                   jax.ShapeDtypeStruct((len(POOL_WINDOWS), G, G), F32),
                   jax.ShapeDtypeStruct((1, W), F32)],
        compiler_params=_cp("arbitrary"),
    )(u, u, dy, dy, pool_w, pool_scale)


GELU_K0 = math.sqrt(2.0 / math.pi)
GELU_K1 = 0.044715


def _gelu_parts(x):
    x2 = x * x
    t = jnp.tanh(x * (GELU_K0 + (GELU_K0 * GELU_K1) * x2))
    hp = 0.5 + 0.5 * t
    gelu = x * hp
    dgelu = hp + (x * (hp * (1.0 - t))) * (GELU_K0 + (3.0 * GELU_K0 * GELU_K1) * x2)
    return gelu, dgelu


def _shifted(ext, halo):
    return (pltpu.roll(ext, 2, 0)[halo:], pltpu.roll(ext, 1, 0)[halo:], ext[halo:])


def _conv(sh, w, b):
    return b + (sh[0] * w[0:1] + sh[1] * w[1:2] + sh[2] * w[2:3])


F32_ROWS = 8


def _ffn_up_glu(h, w_up_t, conv_w, conv_b, *, name, tm=1024, tn=256, sub=512):
    S, K = h.shape
    F = D_FF
    nj = F // tn

    def body(h_ref, wg_ref, wv_ref, cwg_ref, cwv_ref, cbg_ref, cbv_ref,
             ug_ref, uv_ref, cg_ref, cv_ref, y_ref, carry):
        i = pl.program_id(0)
        j = pl.program_id(1)

        w_cat = jnp.concatenate([wg_ref[...], wv_ref[...]], axis=0)
        conv_w_b = ((cwg_ref[...], cbg_ref[...]), (cwv_ref[...], cbv_ref[...]))
        halo = [jnp.where(i > 0, carry[j, s], 0.0) for s in range(2)]
        for a in range(0, tm, sub):
            u16 = _dot(h_ref[a:a + sub, :], w_cat, NT).astype(BF16)
            ug_ref[a:a + sub, :] = u16[:, :tn]
            uv_ref[a:a + sub, :] = u16[:, tn:]
            c = []
            for s, (cw, cb) in enumerate(conv_w_b):
                u = u16[:, s * tn:(s + 1) * tn].astype(F32)
                ext = jnp.concatenate([halo[s], u], axis=0)
                c.append(_conv(_shifted(ext, F32_ROWS), cw, cb))
                halo[s] = u[sub - F32_ROWS:]
            cg_ref[a:a + sub, :] = c[0].astype(BF16)
            cv_ref[a:a + sub, :] = c[1].astype(BF16)
            gelu, _ = _gelu_parts(c[0])
            y_ref[a:a + sub, :] = (gelu * c[1]).astype(BF16)
        for s in range(2):
            carry[j, s] = halo[s]

    tile = pl.BlockSpec((tm, tn), lambda i, j: (i, j))
    vec = lambda rows, off: pl.BlockSpec((rows, tn), lambda i, j: (0, j + off))
    return pl.pallas_call(
        body, name=name, grid=(S // tm, nj),
        in_specs=[pl.BlockSpec((tm, K), lambda i, j: (i, 0)),
                  pl.BlockSpec((tn, K), lambda i, j: (j, 0)), pl.BlockSpec((tn, K), lambda i, j: (j + nj, 0)),
                  vec(3, 0), vec(3, nj), vec(1, 0), vec(1, nj)],
        out_specs=[tile] * 5,
        out_shape=[jax.ShapeDtypeStruct((S, F), BF16)] * 5,
        scratch_shapes=[pltpu.VMEM((nj, 2, F32_ROWS, tn), F32)],
        compiler_params=_cp("arbitrary", "arbitrary"),
    )(h, w_up_t, w_up_t, conv_w, conv_w, conv_b, conv_b)


def _ffn_glu_bwd(u_g, u_v, c_g, c_v, df, w_down, h, conv_w, *, name, tm=512, tn=256, sub=256):
    S = u_g.shape[0]
    F = D_FF
    D = df.shape[1]
    nj = F // tn
    nt = S // tm

    def body(ug_ref, uv_ref, cg_ref, cgn_ref, cv_ref, cvn_ref, df_ref, dfn_ref, wd_ref, h_ref, wg_ref, wv_ref,
             dug_ref, duv_ref, gug_ref, guv_ref, gd_ref, gwg_ref, gwv_ref, gbg_ref, gbv_ref,
             acc_u, acc_d):
        i = pl.program_id(1)

        @pl.when(i == 0)
        def _():
            for r in (gwg_ref, gwv_ref, gbg_ref, gbv_ref, acc_u, acc_d):
                r[...] = jnp.zeros_like(r)

        wg, wv = wg_ref[...], wv_ref[...]
        wd = wd_ref[...]
        dfn = jnp.where(i < nt - 1, dfn_ref[...], jnp.zeros_like(dfn_ref))
        n_ext = sub + HALO

        def ahead(dc):
            return dc[:sub], pltpu.roll(dc, n_ext - 1, 0)[:sub], pltpu.roll(dc, n_ext - 2, 0)[:sub]

        for a in range(0, tm, sub):
            b = a + sub
            ext = lambda ref, nxt: jnp.concatenate(
                [ref[a:b, :], ref[b:b + HALO, :] if b < tm else nxt], axis=0)
            cg = ext(cg_ref, cgn_ref[...]).astype(F32)
            cv = ext(cv_ref, cvn_ref[...]).astype(F32)
            df_sub = df_ref[a:b, :]
            dy_ext = _dot(ext(df_ref, dfn), wd, NT)
            gelu, dgelu = _gelu_parts(cg)
            dcs_g = ahead(dy_ext * cv * dgelu)
            dcs_v = ahead(dy_ext * gelu)
            du_g = (dcs_g[0] * wg[2:3] + dcs_g[1] * wg[1:2] + dcs_g[2] * wg[0:1]).astype(BF16)
            du_v = (dcs_v[0] * wv[2:3] + dcs_v[1] * wv[1:2] + dcs_v[2] * wv[0:1]).astype(BF16)
            dug_ref[a:b, :] = du_g
            duv_ref[a:b, :] = du_v
            acc_u[...] += _dot(jnp.concatenate([du_g, du_v], axis=1), h_ref[a:b, :], TN)
            acc_d[...] += _dot((gelu[:sub] * cv[:sub]).astype(BF16), df_sub, TN)
            for dcs, u_ref, gw_ref, gb_ref in ((dcs_g, ug_ref, gwg_ref, gbg_ref), (dcs_v, uv_ref, gwv_ref, gbv_ref)):
                u = u_ref[a:b, :].astype(F32)
                gb_ref[...] += jnp.sum(dcs[0], axis=0, keepdims=True)
                for k in range(3):
                    gw_ref[k:k + 1, :] += jnp.sum(dcs[2 - k] * u, axis=0, keepdims=True)

        @pl.when(i == nt - 1)
        def _():
            gug_ref[...] = acc_u[:tn, :].astype(BF16)
            guv_ref[...] = acc_u[tn:, :].astype(BF16)
            gd_ref[...] = acc_d[...].astype(BF16)

    per = tm // HALO
    nh = S // HALO
    hnext = lambda i: jnp.minimum((i + 1) * per, nh - 1)
    tile = pl.BlockSpec((tm, tn), lambda j, i: (i, j))
    hn = pl.BlockSpec((HALO, tn), lambda j, i: (hnext(i), j))
    vec = lambda rows, off: pl.BlockSpec((rows, tn), lambda j, i: (0, j + off))
    wide = pl.BlockSpec((tm, D), lambda j, i: (i, 0))
    wrow = pl.BlockSpec((tn, D), lambda j, i: (j, 0))
    return pl.pallas_call(
        body, name=name, grid=(nj, nt),
        in_specs=[tile, tile, tile, hn, tile, hn, wide, pl.BlockSpec((HALO, D), lambda j, i: (hnext(i), 0)),
                  wrow, wide, vec(3, 0), vec(3, nj)],
        out_specs=[tile, tile, wrow, wrow, wrow, vec(3, 0), vec(3, 0), vec(1, 0), vec(1, 0)],
        out_shape=[jax.ShapeDtypeStruct((S, F), BF16), jax.ShapeDtypeStruct((S, F), BF16),
                   jax.ShapeDtypeStruct((F, D), BF16), jax.ShapeDtypeStruct((F, D), BF16),
                   jax.ShapeDtypeStruct((F, D), BF16),
                   jax.ShapeDtypeStruct((3, F), F32), jax.ShapeDtypeStruct((3, F), F32),
                   jax.ShapeDtypeStruct((1, F), F32), jax.ShapeDtypeStruct((1, F), F32)],
        scratch_shapes=[pltpu.VMEM((2 * tn, D), F32), pltpu.VMEM((tn, D), F32)],
        compiler_params=_cp("parallel", "arbitrary"),
    )(u_g, u_v, c_g, c_g, c_v, c_v, df, df, w_down, h, conv_w, conv_w)


def _sum_partials(parts, *, name, tr):
    _, R, C = parts.shape

    def body(p_ref, o_ref):
        tot = p_ref[0].astype(F32)
        for j in range(1, N_DEV):
            tot = tot + p_ref[j].astype(F32)
        o_ref[...] = tot

    return pl.pallas_call(
        body, name=name, grid=(R // tr,),
        in_specs=[pl.BlockSpec((N_DEV, tr, C), lambda i: (0, i, 0))],
        out_specs=pl.BlockSpec((tr, C), lambda i: (i, 0)),
        out_shape=jax.ShapeDtypeStruct((R, C), F32),
        compiler_params=_cp("parallel"),
    )(parts)


def _adamw(w, g, m, v, *, name, tr):
    R, C = w.shape
    c1 = 1.0 - ADAM_B1 ** ADAM_STEP
    c2 = 1.0 - ADAM_B2 ** ADAM_STEP

    def body(w_ref, g_ref, m_ref, v_ref, d_ref, nm_ref, nv_ref):
        g = g_ref[...]
        nm = ADAM_B1 * m_ref[...] + (1.0 - ADAM_B1) * g
        nv = ADAM_B2 * v_ref[...] + (1.0 - ADAM_B2) * (g * g)
        d_ref[...] = -ADAM_LR * ((nm / c1) / (jnp.sqrt(nv / c2) + ADAM_EPS) + ADAM_WD * w_ref[...])
        nm_ref[...] = nm
        nv_ref[...] = nv

    spec = pl.BlockSpec((tr, C), lambda i: (i, 0))
    return pl.pallas_call(
        body, name=name, grid=(R // tr,), in_specs=[spec] * 4, out_specs=[spec] * 3,
        out_shape=[jax.ShapeDtypeStruct((R, C), F32)] * 3,
        compiler_params=_cp("parallel"),
    )(w, g, m, v)


def _mesh_pos():
    return lax.axis_index("x"), lax.axis_index("y"), lax.axis_index("c")


def _gather_phases(x_ref, out_ref, send_sems, recv_sems, local_sem):
    x, y, c = _mesh_pos()
    me, sibling = (x, y, c), (x, y, 1 - c)
    chips = [(1 - x, y), (x, 1 - y), (1 - x, 1 - y)]

    def slot(px, py, pc):
        return out_ref.at[4 * px + 2 * py + pc]

    def copy(k, block, to, src=None):
        return pltpu.make_async_remote_copy(
            src_ref=slot(*block) if src is None else src, dst_ref=slot(*block),
            send_sem=send_sems.at[k], recv_sem=recv_sems.at[k], device_id=to, device_id_type=MESH)

    mine = pltpu.make_async_copy(x_ref, slot(*me), local_sem)
    first = [copy(0, me, sibling, src=x_ref)]
    first += [copy(1 + j, me, (*chip, c), src=x_ref) for j, chip in enumerate(chips)]
    passed = [copy(4 + j, (*chip, c), sibling) for j, chip in enumerate(chips)]

    def start():
        mine.start()
        for cp in first:
            cp.start()

    def forward():
        for j, chip in enumerate(chips):
            copy(1 + j, (*chip, c), me).wait_recv()
            passed[j].start()

    def finish():
        copy(0, sibling, me).wait_recv()
        for j, chip in enumerate(chips):
            copy(4 + j, (*chip, 1 - c), me).wait_recv()
        for cp in first + passed:
            cp.wait_send()
        mine.wait()

    return start, forward, finish


def _two_level_gather(x_ref, out_ref, send_sems, recv_sems, local_sem):
    for phase in _gather_phases(x_ref, out_ref, send_sems, recv_sems, local_sem):
        phase()


_GATHER_SEMS = [pltpu.SemaphoreType.DMA((7,)), pltpu.SemaphoreType.DMA((7,)), pltpu.SemaphoreType.DMA]


def _all_gather_hbm(block, *, name):
    def body(x_ref, out_ref, send_sems, recv_sems, local_sem):
        _two_level_gather(x_ref, out_ref, send_sems, recv_sems, local_sem)

    return pl.pallas_call(
        body, name=name, in_specs=[ANY], out_specs=ANY,
        out_shape=jax.ShapeDtypeStruct((N_DEV,) + block.shape, block.dtype),
        scratch_shapes=_GATHER_SEMS,
    )(block)


def _all_reduce_small(block, *, name):
    def body(x_ref, all_ref, sum_ref, send_sems, recv_sems, local_sem):
        _two_level_gather(x_ref, all_ref, send_sems, recv_sems, local_sem)
        tot = all_ref[0]
        for j in range(1, N_DEV):
            tot = tot + all_ref[j]
        sum_ref[...] = tot

    return pl.pallas_call(
        body, name=name, in_specs=[VMEM], out_specs=[VMEM, VMEM],
        out_shape=[jax.ShapeDtypeStruct((N_DEV,) + block.shape, block.dtype),
                   jax.ShapeDtypeStruct(block.shape, block.dtype)],
        scratch_shapes=_GATHER_SEMS,
        compiler_params=pltpu.CompilerParams(vmem_limit_bytes=V7X_VMEM_LIMIT),
    )(block)[1]


def _exchange_phases(g_refs, r_refs, send_sems, recv_sems, local_sems):
    x, y, c = _mesh_pos()
    me = 4 * x + 2 * y + c
    owns, remote = [], []
    for k, (g_ref, r_ref) in enumerate(zip(g_refs, r_refs)):
        rows = g_ref.shape[0] // N_DEV
        owns.append(pltpu.make_async_copy(g_ref.at[pl.ds(me * rows, rows)], r_ref.at[me], local_sems.at[k]))
        for p in range(1, N_DEV):
            px, py, pc = x ^ (p >> 2), y ^ ((p >> 1) & 1), c ^ (p & 1)
            peer = 4 * px + 2 * py + pc
            link = dict(send_sem=send_sems.at[k, p], recv_sem=recv_sems.at[k, p],
                        device_id=(px, py, pc), device_id_type=MESH)
            src = g_ref.at[pl.ds(peer * rows, rows)]
            send = pltpu.make_async_remote_copy(src_ref=src, dst_ref=r_ref.at[me], **link)
            arrival = pltpu.make_async_remote_copy(src_ref=src, dst_ref=r_ref.at[peer], **link)
            remote.append((send, arrival))

    def start():
        for own in owns:
            own.start()
        for send, _ in remote:
            send.start()

    def finish():
        for _, arrival in remote:
            arrival.wait_recv()
        for send, _ in remote:
            send.wait_send()
        for own in owns:
            own.wait()

    return start, finish


def _exchange_buffers(grads):
    n = len(grads)
    shapes = [jax.ShapeDtypeStruct((N_DEV, g.shape[0] // N_DEV, g.shape[1]), g.dtype) for g in grads]
    sems = [pltpu.SemaphoreType.DMA((n, N_DEV)), pltpu.SemaphoreType.DMA((n, N_DEV)),
            pltpu.SemaphoreType.DMA((n,))]
    return shapes, sems


def _unpack_gathered(gathered):
    n_out, n_up, n_down = D_MODEL, 2 * D_FF, D_FF
    offs, row = [], 0
    for n in (n_out, n_up, n_down):
        offs.append((row, row + n // N_DEV))
        row += n // N_DEV
    w_out, w_up_t, w_down = (gathered[:, a:b].reshape(-1, D_MODEL) for a, b in offs)
    n_cw = 3 * (2 * D_FF // N_DEV)
    cw_all = gathered[:, row:].reshape(N_DEV, -1)[:, :2 * n_cw].reshape(N_DEV, n_cw, 2)
    cw_all = lax.bitcast_convert_type(cw_all, F32).reshape(N_DEV, 3, -1)
    conv_w = jnp.transpose(cw_all, (1, 0, 2)).reshape(3, 2 * D_FF)
    return w_out, w_up_t, w_down, conv_w


def _rest_payload(w_out, w_up, w_down, conv_w):
    cw_bits = lax.bitcast_convert_type(conv_w.reshape(-1), BF16).reshape(-1)
    cw_bits = jnp.pad(cw_bits, (0, HALO * D_MODEL - cw_bits.shape[0])).reshape(HALO, D_MODEL)
    return jnp.concatenate([w_out.astype(BF16), w_up.T.astype(BF16), w_down.astype(BF16), cw_bits], axis=0)


def _device_step(x, target, g_mix_pre, w_in_t, rest_payload, pool_w, pool_scale, g_mix_post, g_ffn_pre,
                 conv_b, g_ffn_post):
    h1 = _rms_norm(x, g_mix_pre, name="rms_mix_pre")
    proj = _matmul(h1, w_in_t, trans_b=True, out_dtype=F32, tm=1024, tn=512, name="proj")
    attn, lse, attn16, gathered = _attn_fwd(proj, rest_payload, name="attn_fwd")
    w_out, w_up_t, w_down, conv_w = _unpack_gathered(gathered)
    pool = _pool_fwd(proj, 3, pool_w, pool_scale, name="pool_fwd")
    mixed, x2, h2 = _mix_out(attn16, pool, w_out, x, g_mix_post, g_ffn_pre, name="mix_out")
    u_g, u_v, c_g, c_v, y = _ffn_up_glu(h2, w_up_t, conv_w, conv_b, name="ffn_up_glu")
    df, d_out, loss_blk, gg_ffn_post = _ffn_out(y, w_down, x2, target, g_ffn_post, name="ffn_out")
    du_g, du_v, gw_up_g, gw_up_v, gw_down, gcw_g, gcw_v, gcb_g, gcb_v = _ffn_glu_bwd(
        u_g, u_v, c_g, c_v, df, w_down, h2, conv_w, name="ffn_glu_bwd")
    gw_up_t = jnp.concatenate([gw_up_g, gw_up_v], axis=0)
    dx2, gg_ffn_pre, dmixed, gg_mix_post = _dgrad_norm(
        [du_g, du_v], w_up_t, d_out, x2, g_ffn_pre, (mixed, g_mix_post), [], tk=1408, name="ffn_up_dgrad")
    gw_out = jnp.concatenate([_matmul_tn(attn16, dmixed, ta=512, ts=1024, name="grad_w_out_attn"),
                              _matmul_tn(pool, dmixed, ta=512, ts=1024, name="grad_w_out_pool")], axis=0)
    dcat = _matmul(dmixed, w_out, trans_b=True, out_dtype=F32, tm=512, tn=1024, name="mix_out_dgrad")
    d_pool_in, g_pool_w, g_pool_scale = _pool_bwd(proj, 3, dcat, 1, pool_w, pool_scale, name="pool_bwd")
    dqkv, (r_out, r_up_t, r_down) = _attn_bwd(proj, dcat, attn, lse, [gw_out, gw_up_t, gw_down], name="attn_bwd")
    dproj = list(dqkv) + [d_pool_in]
    gw_in_t = jnp.concatenate([_matmul_tn(a, h1, ta=512, ts=1024, name=f"grad_w_in_{k}")
                               for k, a in enumerate(dproj)], axis=0)
    grad_x, gg_mix_pre, (r_in_t,) = _dgrad_norm(dproj, w_in_t, dx2, x, g_mix_pre, None, [gw_in_t], tk=512,
                                                name="proj_dgrad")
    g_conv_w = jnp.concatenate([gcw_g, gcw_v], axis=1)
    g_conv_b = jnp.concatenate([gcb_g, gcb_v], axis=1)
    received = (r_in_t, r_out, r_up_t, r_down)
    small = dict(g_mix_pre=gg_mix_pre, g_mix_post=gg_mix_post, g_ffn_pre=gg_ffn_pre, g_ffn_post=gg_ffn_post,
                 pool_scale=g_pool_scale, conv_b=g_conv_b, pool_w=g_pool_w, conv_w=g_conv_w)
    return loss_blk, grad_x, received, small


_SMALL = ("g_mix_pre", "g_mix_post", "g_ffn_pre", "g_ffn_post", "pool_scale", "conv_b", "pool_w")
LANES = 128


def _pack_rows(arrays):
    parts = []
    for a in arrays:
        a2 = a.reshape(-1, LANES)
        parts.append(jnp.pad(a2, ((0, (-a2.shape[0]) % 8), (0, 0))))
    return jnp.concatenate(parts, axis=0)


def _unpack_rows(packed, shapes):
    out, row = [], 0
    for shape in shapes:
        rows = math.prod(shape) // LANES
        out.append(packed[row:row + rows].reshape(shape))
        row += -(-rows // 8) * 8
    return out


def kernel(x, g_mix_pre, w_in, pool_w, pool_scale, w_out, g_mix_post, g_ffn_pre, w_up, conv_w, conv_b, w_down, g_ffn_post, loss_target, m_g_mix_pre, m_w_in, m_pool_w, m_pool_scale, m_w_out, m_g_mix_post, m_g_ffn_pre, m_w_up, m_conv_w, m_conv_b, m_w_down, m_g_ffn_post, v_g_mix_pre, v_w_in, v_pool_w, v_pool_scale, v_w_out, v_g_mix_post, v_g_ffn_pre, v_w_up, v_conv_w, v_conv_b, v_w_down, v_g_ffn_post):
    me = 4 * lax.axis_index("x") + 2 * lax.axis_index("y") + lax.axis_index("c")
    w_in_t = _all_gather_hbm(w_in[0].T.astype(BF16), name="gather_w_in").reshape(4 * ATTN_WIDTH, D_MODEL)
    loss_blk, grad_x, recv, small = _device_step(
        x[0], loss_target[0], g_mix_pre, w_in_t, _rest_payload(w_out[0], w_up[0], w_down[0], conv_w[0]),
        pool_w[0], pool_scale, g_mix_post, g_ffn_pre, conv_b, g_ffn_post)
    loss = lax.psum(loss_blk[0, 0], ("x", "y", "c"))

    g_in_t, g_out, g_up_t, g_down = (
        _sum_partials(r, name=f"sum_partials_{k}", tr=r.shape[1] // 2) for k, r in enumerate(recv))
    grads = {"w_in": g_in_t.T, "w_out": g_out, "w_up": g_up_t.T, "w_down": g_down}

    given = dict(g_mix_pre=g_mix_pre, g_mix_post=g_mix_post, g_ffn_pre=g_ffn_pre, g_ffn_post=g_ffn_post,
                 pool_scale=pool_scale, conv_b=conv_b, pool_w=pool_w)
    small_shapes = [given[k].shape for k in _SMALL]
    total = _all_reduce_small(_pack_rows([small[k] for k in _SMALL] + [small["conv_w"]]), name="all_reduce_small")
    *small_grads, g_conv_w_all = _unpack_rows(total, small_shapes + [(3, 2 * D_FF)])
    grads.update(zip(_SMALL, small_grads))
    width = 2 * D_FF // N_DEV
    grads["conv_w"] = lax.dynamic_slice_in_dim(g_conv_w_all, me * width, width, axis=1)[None]

    weights = dict(g_mix_pre=g_mix_pre, w_in=w_in, pool_w=pool_w, pool_scale=pool_scale, w_out=w_out,
                   g_mix_post=g_mix_post, g_ffn_pre=g_ffn_pre, w_up=w_up, conv_w=conv_w, conv_b=conv_b,
                   w_down=w_down, g_ffn_post=g_ffn_post)
    m_in = dict(g_mix_pre=m_g_mix_pre, w_in=m_w_in, pool_w=m_pool_w, pool_scale=m_pool_scale, w_out=m_w_out,
                g_mix_post=m_g_mix_post, g_ffn_pre=m_g_ffn_pre, w_up=m_w_up, conv_w=m_conv_w, conv_b=m_conv_b,
                w_down=m_w_down, g_ffn_post=m_g_ffn_post)
    v_in = dict(g_mix_pre=v_g_mix_pre, w_in=v_w_in, pool_w=v_pool_w, pool_scale=v_pool_scale, w_out=v_w_out,
                g_mix_post=v_g_mix_post, g_ffn_pre=v_g_ffn_pre, w_up=v_w_up, conv_w=v_conv_w, conv_b=v_conv_b,
                w_down=v_w_down, g_ffn_post=v_g_ffn_post)
    delta, new_m, new_v = {}, {}, {}
    for k in ("w_in", "w_out", "w_up", "w_down"):
        g = grads[k]
        d, nm, nv = _adamw(weights[k][0], g, m_in[k][0], v_in[k][0], name=f"adamw_{k}", tr=g.shape[0] // 2)
        grads[k], delta[k], new_m[k], new_v[k] = g[None], d[None], nm[None], nv[None]
    d, nm, nv = _adamw(weights["conv_w"][0], grads["conv_w"][0], m_in["conv_w"][0], v_in["conv_w"][0],
                       name="adamw_conv_w", tr=3)
    delta["conv_w"], new_m["conv_w"], new_v["conv_w"] = d[None], nm[None], nv[None]
    packed_w = _pack_rows([weights[k] for k in _SMALL])
    small_rows = packed_w.shape[0]
    d, nm, nv = _adamw(packed_w, total[:small_rows], _pack_rows([m_in[k] for k in _SMALL]),
                       _pack_rows([v_in[k] for k in _SMALL]), name="adamw_small", tr=small_rows)
    for k, dk, mk, vk in zip(_SMALL, _unpack_rows(d, small_shapes), _unpack_rows(nm, small_shapes),
                             _unpack_rows(nv, small_shapes)):
        delta[k], new_m[k], new_v[k] = dk, mk, vk

    order = ("g_mix_pre", "w_in", "pool_w", "pool_scale", "w_out", "g_mix_post", "g_ffn_pre", "w_up",
             "conv_w", "conv_b", "w_down", "g_ffn_post")
    return (loss, grad_x[None], *[grads[k] for k in order], *[delta[k] for k in order],
            *[new_m[k] for k in order], *[new_v[k] for k in order])
```

```python
import functools
import math

import jax
import jax.numpy as jnp
from jax import lax
from jax.experimental import pallas as pl
from jax.experimental.pallas import tpu as pltpu

F32 = jnp.float32
BF16 = jnp.bfloat16

D_MODEL = 1024
ATTN_WIDTH = 512
N_HEADS = 8
HEAD_DIM = 64
DILATIONS = (1, 4, 16)
BLOCK = 128
POOL_WIDTH = 512
POOL_WINDOWS = (2, 4, 8, 16)
POOL_GROUP_DIM = 128
D_FF = 2816
EPS = 1e-6
NEG_INF = -1e30
SCALE = HEAD_DIM ** -0.5

ADAM_LR = 0.001
ADAM_B1 = 0.9
ADAM_B2 = 0.999
ADAM_EPS = 1e-08
ADAM_WD = 0.01
ADAM_STEP = 10

N_DEV = 8
HALO = 16
V7X_VMEM_LIMIT = 56 * 1024 * 1024

MESH = pl.DeviceIdType.MESH
ANY = pl.BlockSpec(memory_space=pl.ANY)
VMEM = pl.BlockSpec(memory_space=pltpu.VMEM)

NT = (((1,), (1,)), ((), ()))
NN = (((1,), (0,)), ((), ()))
TN = (((0,), (0,)), ((), ()))


def _cp(*sem):
    return pltpu.CompilerParams(dimension_semantics=sem, vmem_limit_bytes=V7X_VMEM_LIMIT)


def _dot(a, b, dn):
    return lax.dot_general(a, b, dn, preferred_element_type=F32)


def _rms_bwd(xin, g, dy):
    r = lax.rsqrt(jnp.mean(xin * xin, axis=-1, keepdims=True) + EPS)
    xh = xin * r
    gdy = g * dy
    dx = r * (gdy - xh * jnp.mean(gdy * xh, axis=-1, keepdims=True))
    dg = jnp.sum(dy * xh, axis=0, keepdims=True)
    return dx, dg


def _rms_norm(x, g, *, name, tm=512):
    S, D = x.shape

    def body(x_ref, g_ref, o_ref):
        xv = x_ref[...]
        r = lax.rsqrt(jnp.mean(xv * xv, axis=-1, keepdims=True) + EPS)
        o_ref[...] = (xv * r * g_ref[...]).astype(BF16)

    return pl.pallas_call(
        body, name=name, grid=(S // tm,),
        in_specs=[pl.BlockSpec((tm, D), lambda i: (i, 0)), pl.BlockSpec((1, D), lambda i: (0, 0))],
        out_specs=pl.BlockSpec((tm, D), lambda i: (i, 0)),
        out_shape=jax.ShapeDtypeStruct((S, D), BF16),
        compiler_params=_cp("parallel"),
    )(x, g)


def _matmul(a, b, *, trans_b, out_dtype, tm, tn, name):
    M, K = a.shape
    N = b.shape[0] if trans_b else b.shape[1]
    dn = NT if trans_b else NN

    def body(a_ref, b_ref, o_ref):
        o_ref[...] = _dot(a_ref[...], b_ref[...], dn).astype(out_dtype)

    b_spec = (pl.BlockSpec((tn, K), lambda i, j: (j, 0)) if trans_b
              else pl.BlockSpec((K, tn), lambda i, j: (0, j)))
    return pl.pallas_call(
        body, name=name, grid=(M // tm, N // tn),
        in_specs=[pl.BlockSpec((tm, K), lambda i, j: (i, 0)), b_spec],
        out_specs=pl.BlockSpec((tm, tn), lambda i, j: (i, j)),
        out_shape=jax.ShapeDtypeStruct((M, N), out_dtype),
        compiler_params=_cp("parallel", "parallel"),
    )(a, b)


def _matmul_tn(a, b, *, ta, ts, name):
    S, Ka = a.shape
    Nb = b.shape[1]
    ns = S // ts

    def body(a_ref, b_ref, o_ref, acc):
        s = pl.program_id(1)

        @pl.when(s == 0)
        def _():
            acc[...] = jnp.zeros_like(acc)

        acc[...] += _dot(a_ref[...], b_ref[...], TN)

        @pl.when(s == ns - 1)
        def _():
            o_ref[...] = acc[...].astype(BF16)

    return pl.pallas_call(
        body, name=name, grid=(Ka // ta, ns),
        in_specs=[pl.BlockSpec((ts, ta), lambda i, s: (s, i)), pl.BlockSpec((ts, Nb), lambda i, s: (s, 0))],
        out_specs=pl.BlockSpec((ta, Nb), lambda i, s: (i, 0)),
        out_shape=jax.ShapeDtypeStruct((Ka, Nb), BF16),
        scratch_shapes=[pltpu.VMEM((ta, Nb), F32)],
        compiler_params=_cp("parallel", "arbitrary"),
    )(a, b)


def _mix_out(attn, pool, w_out, x, g_post, g_next, *, name, tm=256):
    S, K = attn.shape
    D = w_out.shape[1]

    def body(a_ref, p_ref, w_ref, x_ref, gp_ref, gn_ref, mixed_ref, x2_ref, h2_ref):
        mixed = _dot(a_ref[...], w_ref[:K, :], NN) + _dot(p_ref[...], w_ref[K:, :], NN)
        r = lax.rsqrt(jnp.mean(mixed * mixed, axis=-1, keepdims=True) + EPS)
        x2 = x_ref[...] + mixed * r * gp_ref[...]
        r2 = lax.rsqrt(jnp.mean(x2 * x2, axis=-1, keepdims=True) + EPS)
        mixed_ref[...] = mixed
        x2_ref[...] = x2
        h2_ref[...] = (x2 * r2 * gn_ref[...]).astype(BF16)

    row = lambda i: (i, 0)
    fix = lambda i: (0, 0)
    return pl.pallas_call(
        body, name=name, grid=(S // tm,),
        in_specs=[pl.BlockSpec((tm, K), row), pl.BlockSpec((tm, K), row), pl.BlockSpec((2 * K, D), fix),
                  pl.BlockSpec((tm, D), row), pl.BlockSpec((1, D), fix), pl.BlockSpec((1, D), fix)],
        out_specs=[pl.BlockSpec((tm, D), row)] * 3,
        out_shape=[jax.ShapeDtypeStruct((S, D), F32), jax.ShapeDtypeStruct((S, D), F32),
                   jax.ShapeDtypeStruct((S, D), BF16)],
        compiler_params=_cp("parallel"),
    )(attn, pool, w_out, x, g_post, g_next)


def _ffn_out(y, w_down, x2, target, g_post, *, name, tm=256):
    S, K = y.shape
    D = w_down.shape[1]

    def body(y_ref, w_ref, x2_ref, t_ref, g_ref, df_ref, dout_ref, loss_ref, gg_ref):
        i = pl.program_id(0)

        @pl.when(i == 0)
        def _():
            loss_ref[...] = jnp.zeros_like(loss_ref)
            gg_ref[...] = jnp.zeros_like(gg_ref)

        f = _dot(y_ref[...], w_ref[...], NN)
        g = g_ref[...]
        r = lax.rsqrt(jnp.mean(f * f, axis=-1, keepdims=True) + EPS)
        out = x2_ref[...] + f * r * g
        err = out - t_ref[...]
        dy = err * (1.0 / D)
        df, dg = _rms_bwd(f, g, dy)
        df_ref[...] = df.astype(BF16)
        dout_ref[...] = dy
        gg_ref[...] += dg
        loss_ref[...] += 0.5 * jnp.sum(jnp.mean(err * err, axis=-1, keepdims=True))

    row = lambda i: (i, 0)
    fix = lambda i: (0, 0)
    return pl.pallas_call(
        body, name=name, grid=(S // tm,),
        in_specs=[pl.BlockSpec((tm, K), row), pl.BlockSpec((K, D), fix), pl.BlockSpec((tm, D), row),
                  pl.BlockSpec((tm, D), row), pl.BlockSpec((1, D), fix)],
        out_specs=[pl.BlockSpec((tm, D), row), pl.BlockSpec((tm, D), row),
                   pl.BlockSpec((8, 128), fix), pl.BlockSpec((1, D), fix)],
        out_shape=[jax.ShapeDtypeStruct((S, D), BF16), jax.ShapeDtypeStruct((S, D), F32),
                   jax.ShapeDtypeStruct((8, 128), F32), jax.ShapeDtypeStruct((1, D), F32)],
        compiler_params=_cp("arbitrary"),
    )(y, w_down, x2, target, g_post)


def _dgrad_norm(a_list, w, resid, xin, g, second, exchange, *, name, tm=512):
    S, Kp = a_list[0].shape
    na = len(a_list)
    D = w.shape[1]
    nt = S // tm
    two = second is not None
    ng = len(exchange)
    recv_shapes, exchange_sems = _exchange_buffers(exchange)

    def body(*refs):
        a_refs = refs[:na]
        w_ref, r_ref, x_ref, g_ref = refs[na:na + 4]
        pos = na + 4
        if two:
            x2_ref, g2_ref = refs[pos:pos + 2]
            pos += 2
        g_refs = refs[pos:pos + ng]
        pos += ng
        dx_ref, gg_ref = refs[pos:pos + 2]
        pos += 2
        if two:
            d2_ref, gg2_ref = refs[pos:pos + 2]
            pos += 2
        r_refs = refs[pos:pos + ng]
        pos += ng
        i = pl.program_id(0)
        if ng:
            start, finish = _exchange_phases(g_refs, r_refs, *refs[pos:])
            pl.when(i == 0)(start)

        @pl.when(i == 0)
        def _():
            gg_ref[...] = jnp.zeros_like(gg_ref)
            if two:
                gg2_ref[...] = jnp.zeros_like(gg2_ref)

        dh = functools.reduce(jnp.add, [_dot(a_refs[q][...], w_ref[q * Kp:(q + 1) * Kp, :], NN) for q in range(na)])
        d1, dg1 = _rms_bwd(x_ref[...], g_ref[...], dh)
        dx = r_ref[...] + d1
        dx_ref[...] = dx
        gg_ref[...] += dg1
        if two:
            d2, dg2 = _rms_bwd(x2_ref[...], g2_ref[...], dx)
            d2_ref[...] = d2.astype(BF16)
            gg2_ref[...] += dg2
        if ng:
            pl.when(i == nt - 1)(finish)

    row = lambda i: (i, 0)
    fix = lambda i: (0, 0)
    in_specs = [pl.BlockSpec((tm, Kp), row)] * na + [
        pl.BlockSpec((na * Kp, D), fix, pipeline_mode=pl.Buffered(1)), pl.BlockSpec((tm, D), row),
        pl.BlockSpec((tm, D), row), pl.BlockSpec((1, D), fix)]
    args = list(a_list) + [w, resid, xin, g]
    out_specs = [pl.BlockSpec((tm, D), row), pl.BlockSpec((1, D), fix)]
    out_shape = [jax.ShapeDtypeStruct((S, D), F32), jax.ShapeDtypeStruct((1, D), F32)]
    if two:
        in_specs += [pl.BlockSpec((tm, D), row), pl.BlockSpec((1, D), fix)]
        args += list(second)
        out_specs += [pl.BlockSpec((tm, D), row), pl.BlockSpec((1, D), fix)]
        out_shape += [jax.ShapeDtypeStruct((S, D), BF16), jax.ShapeDtypeStruct((1, D), F32)]
    n_plain = len(out_shape)
    out = pl.pallas_call(
        body, name=name, grid=(nt,), in_specs=in_specs + [ANY] * ng, out_specs=out_specs + [ANY] * ng,
        out_shape=out_shape + recv_shapes, scratch_shapes=exchange_sems if ng else [],
        compiler_params=_cp("arbitrary"),
    )(*args, *exchange)
    return (*out[:n_plain], out[n_plain:]) if ng else out


def _band_mask(first_block):
    qi = lax.broadcasted_iota(jnp.int32, (BLOCK, 2 * BLOCK), 0)
    ki = lax.broadcasted_iota(jnp.int32, (BLOCK, 2 * BLOCK), 1)
    first_key = jnp.where(first_block, BLOCK, 0)
    return (ki >= qi) & (ki <= qi + BLOCK) & (ki >= first_key)


def _lane_masks():
    lane = lax.broadcasted_iota(jnp.int32, (1, 2 * HEAD_DIM), 1)
    return (lane < HEAD_DIM, lane >= HEAD_DIM)


CHUNK = BLOCK * max(DILATIONS)
SLAB = 2 * HEAD_DIM
N_SLABS = ATTN_WIDTH // SLAB


def _unit_rows(d, b):
    def rows(r):
        start = r + BLOCK * d * b
        return pl.ds(start, BLOCK, stride=d) if d > 1 else pl.ds(start, BLOCK)
    return rows


def _attn_units():
    for p, d in enumerate(DILATIONS):
        nbc = CHUNK // (BLOCK * d)
        for b in range(nbc):
            for r in range(d):
                yield p, d, b, r, nbc


def _attn_in_specs(nc, n_cur):
    prev = lambda c: jnp.maximum(jnp.minimum(c, nc - 1) - 1, 0)
    cur = lambda c: jnp.minimum(c, nc - 1)
    blk = lambda f: pl.BlockSpec((CHUNK, SLAB), f)
    specs = [blk(lambda h, c: (cur(c), h)),
             blk(lambda h, c: (prev(c), N_SLABS + h)), blk(lambda h, c: (cur(c), N_SLABS + h)),
             blk(lambda h, c: (prev(c), 2 * N_SLABS + h)), blk(lambda h, c: (cur(c), 2 * N_SLABS + h))]
    return specs + [blk(lambda h, c: (cur(c), h))] * n_cur


def _attn_fwd(proj, payload, *, name):
    S = proj.shape[0]
    nc = S // CHUNK
    n = len(DILATIONS)
    n_steps = N_SLABS * nc

    def body(q_ref, kp_ref, kc_ref, vp_ref, vc_ref, pay_ref, attn_ref, lse_ref, attn16_ref, all_ref, *scr):
        o_scr, l_scr = scr[:n], scr[n:2 * n]
        start, forward, finish = _gather_phases(pay_ref, all_ref, *scr[2 * n:])
        step = pl.program_id(0) * nc + pl.program_id(1)
        pl.when(step == 0)(start)
        c = pl.program_id(1)
        lms = _lane_masks()
        plain, first = (jnp.tile(_band_mask(f), (2, 1)) for f in (False, c == 0))
        for p, d, b, r, nbc in _attn_units():
            rows = _unit_rows(d, b)(r)
            prow = _unit_rows(d, (b - 1) % nbc)(r)
            kpr, vpr = (kc_ref, vc_ref) if b > 0 else (kp_ref, vp_ref)
            mask2 = plain if b > 0 else first
            q = q_ref[rows, :].astype(BF16)
            kcat = jnp.concatenate([kpr[prow, :], kc_ref[rows, :]], axis=0).astype(BF16)
            vcat = jnp.concatenate([vpr[prow, :], vc_ref[rows, :]], axis=0).astype(BF16)
            q2 = jnp.concatenate([jnp.where(lm, q, jnp.zeros_like(q)) for lm in lms], axis=0) * SCALE
            s = jnp.where(mask2, _dot(q2, kcat, NT), NEG_INF)
            m = jnp.max(s, axis=-1, keepdims=True)
            e = jnp.exp(s - m)
            l = jnp.sum(e, axis=-1, keepdims=True)
            o2 = _dot(e.astype(BF16), vcat, NN) / l
            lse2 = m + jnp.log(l)
            o_scr[p][rows, :] = jnp.where(lms[0], o2[:BLOCK], o2[BLOCK:])
            l_scr[p][rows, :] = jnp.where(lms[0], lse2[:BLOCK], lse2[BLOCK:])
        ls = [l_scr[p][...] for p in range(n)]
        top = functools.reduce(jnp.maximum, ls)
        es = [jnp.exp(l - top) for l in ls]
        den = functools.reduce(jnp.add, es)
        num = functools.reduce(jnp.add, [e * o_scr[p][...] for p, e in enumerate(es)])
        attn = num / den
        attn_ref[...] = attn
        attn16_ref[...] = attn.astype(BF16)
        lse_ref[...] = top + jnp.log(den)
        pl.when(step == (2 * n_steps) // 3)(forward)
        pl.when(step == n_steps - 1)(finish)

    return pl.pallas_call(
        body, name=name, grid=(N_SLABS, nc), in_specs=_attn_in_specs(nc, 0) + [ANY],
        out_specs=[pl.BlockSpec((CHUNK, SLAB), lambda h, c: (c, h))] * 3 + [ANY],
        out_shape=[jax.ShapeDtypeStruct((S, ATTN_WIDTH), F32)] * 2 + [jax.ShapeDtypeStruct((S, ATTN_WIDTH), BF16),
                   jax.ShapeDtypeStruct((N_DEV,) + payload.shape, payload.dtype)],
        scratch_shapes=[pltpu.VMEM((CHUNK, SLAB), F32)] * (2 * n) + _GATHER_SEMS,
        compiler_params=_cp("arbitrary", "arbitrary"),
    )(proj, proj, proj, proj, proj, payload)


def _attn_bwd(proj, dcat, attn, lse, grads, *, name):
    S = proj.shape[0]
    nc = S // CHUNK
    ng = len(grads)
    n = len(DILATIONS)
    recv_shapes, exchange_sems = _exchange_buffers(grads)

    def body(*refs):
        q_ref, kp_ref, kc_ref, vp_ref, vc_ref, do_ref, o_ref, lse_ref = refs[:8]
        g_refs = refs[8:8 + ng]
        dq_ref, dk_ref, dv_ref = refs[8 + ng:11 + ng]
        r_refs = refs[11 + ng:11 + 2 * ng]
        scr = refs[11 + 2 * ng:]
        dk_prev, dv_prev = scr[:2]
        delta_h, lse_h = scr[2:4], scr[4:6]
        dq_p, dk_own, dk_back, dv_own, dv_back = (scr[6 + n * k:6 + n * (k + 1)] for k in range(5))
        start, finish = _exchange_phases(g_refs, r_refs, *scr[6 + 5 * n:])
        c = pl.program_id(1)
        pl.when((pl.program_id(0) == 0) & (c == 0))(start)

        @pl.when(c == 0)
        def _():
            dk_prev[...] = jnp.zeros_like(dk_prev)
            dv_prev[...] = jnp.zeros_like(dv_prev)

        @pl.when(c < nc)
        def _():
            lms = _lane_masks()
            plain, first = (jnp.tile(_band_mask(f), (2, 1)) for f in (False, c == 0))
            prod = do_ref[...] * o_ref[...]
            lse = lse_ref[...]
            lse_other = pltpu.roll(lse, HEAD_DIM, 1)
            for h, lm in enumerate(lms):
                delta = jnp.sum(jnp.where(lm, prod, 0.0), axis=-1, keepdims=True)
                delta_h[h][...] = jnp.broadcast_to(delta, (CHUNK, SLAB))
                lse_h[h][...] = jnp.where(lm, lse, lse_other)
            wide = lambda refs, rows: jnp.tile(jnp.concatenate([r[rows, :] for r in refs], axis=0), (1, 2))
            stack = lambda f: jnp.concatenate([f(lm) for lm in lms], axis=0)
            for p, d, b, r, nbc in _attn_units():
                rows = _unit_rows(d, b)(r)
                prow = _unit_rows(d, (b - 1) % nbc)(r)
                kpr, vpr = (kc_ref, vc_ref) if b > 0 else (kp_ref, vp_ref)
                mask2 = plain if b > 0 else first
                q = q_ref[rows, :].astype(BF16)
                kcat = jnp.concatenate([kpr[prow, :], kc_ref[rows, :]], axis=0).astype(BF16)
                vcat = jnp.concatenate([vpr[prow, :], vc_ref[rows, :]], axis=0).astype(BF16)
                do = do_ref[rows, :]
                q2 = stack(lambda lm: jnp.where(lm, q, jnp.zeros_like(q))) * SCALE
                do2 = stack(lambda lm: jnp.where(lm, do, 0.0)).astype(BF16)
                e = jnp.where(mask2, jnp.exp(_dot(q2, kcat, NT) - wide(lse_h, rows)), 0.0)
                ds = (e * (_dot(do2, vcat, NT) - wide(delta_h, rows))).astype(BF16)
                dq = jnp.where(lms[0], _dot(ds[:BLOCK], kcat, NN), _dot(ds[BLOCK:], kcat, NN)) * SCALE
                dkc = _dot(ds, q2, TN)
                dvc = _dot(e.astype(BF16), do2, TN)
                dq_p[p][rows, :] = dq
                dk_own[p][rows, :] = dkc[BLOCK:]
                dv_own[p][rows, :] = dvc[BLOCK:]
                dk_back[p][prow, :] = dkc[:BLOCK]
                dv_back[p][prow, :] = dvc[:BLOCK]
            dq_ref[...] = functools.reduce(jnp.add, [r[...] for r in dq_p]).astype(BF16)
            for prev, own, back, out_ref in ((dk_prev, dk_own, dk_back, dk_ref), (dv_prev, dv_own, dv_back, dv_ref)):
                for p, d in enumerate(DILATIONS):
                    tail = CHUNK - BLOCK * d
                    prev[tail:, :] += back[p][tail:, :]
                out_ref[...] = prev[...].astype(BF16)
                prev[...] = functools.reduce(jnp.add, [r[...] for r in own])
                for p, d in enumerate(DILATIONS):
                    tail = CHUNK - BLOCK * d
                    if tail:
                        prev[:tail, :] += back[p][:tail, :]

        @pl.when(c == nc)
        def _():
            dk_ref[...] = dk_prev[...].astype(BF16)
            dv_ref[...] = dv_prev[...].astype(BF16)

        pl.when((pl.program_id(0) == N_SLABS - 1) & (c == nc))(finish)

    blk = lambda f: pl.BlockSpec((CHUNK, SLAB), f)
    late = lambda h, c: (jnp.maximum(c - 1, 0), h)
    out = pl.pallas_call(
        body, name=name, grid=(N_SLABS, nc + 1), in_specs=_attn_in_specs(nc, 3) + [ANY] * ng,
        out_specs=[blk(lambda h, c: (jnp.minimum(c, nc - 1), h)), blk(late), blk(late)] + [ANY] * ng,
        out_shape=[jax.ShapeDtypeStruct((S, ATTN_WIDTH), BF16)] * 3 + recv_shapes,
        scratch_shapes=[pltpu.VMEM((CHUNK, SLAB), F32)] * (6 + 5 * n) + exchange_sems,
        compiler_params=_cp("arbitrary", "arbitrary"),
    )(proj, proj, proj, proj, proj, dcat, attn, lse, *grads)
    return out[:3], out[3:]


def _split_bf16(a):
    hi = a.astype(BF16)
    lo = (a - hi.astype(F32)).astype(BF16)
    return hi, lo


def _pooled(ug, halo_g, w, row0, tm):
    ext = jnp.concatenate([halo_g, ug], axis=0)
    hi, lo = _split_bf16(ext)
    rr = lax.broadcasted_iota(jnp.int32, (tm, tm + HALO), 0)
    cc = lax.broadcasted_iota(jnp.int32, (tm, tm + HALO), 1)
    back = rr + HALO - cc
    win = ((back >= 0) & (back < w)).astype(BF16)
    wsum = _dot(win, hi, NN) + _dot(win, lo, NN)
    rows = row0 + lax.broadcasted_iota(jnp.int32, (tm, 1), 0)
    inv = 1.0 / jnp.minimum(rows + 1, w).astype(F32)
    return wsum * inv - ug


def _pool_fwd(u, u_col, pool_w, pool_scale, *, name, tm=256):
    S, W = u.shape[0], POOL_WIDTH
    G = POOL_GROUP_DIM

    def body(u_ref, h_ref, w_ref, s_ref, o_ref):
        i = pl.program_id(0)
        uv = u_ref[...]
        halo = jnp.where(i > 0, h_ref[...], 0.0)
        for g, w in enumerate(POOL_WINDOWS):
            sl = slice(g * G, (g + 1) * G)
            pooled = _pooled(uv[:, sl], halo[:, sl], w, i * tm, tm)
            z = _dot(pooled.astype(BF16), w_ref[g].astype(BF16), NN)
            o_ref[:, sl] = (z * s_ref[:, sl]).astype(BF16)

    per = tm // HALO
    return pl.pallas_call(
        body, name=name, grid=(S // tm,),
        in_specs=[pl.BlockSpec((tm, W), lambda i: (i, u_col)),
                  pl.BlockSpec((HALO, W), lambda i: (jnp.maximum(i * per - 1, 0), u_col)),
                  pl.BlockSpec((len(POOL_WINDOWS), G, G), lambda i: (0, 0, 0)),
                  pl.BlockSpec((1, W), lambda i: (0, 0))],
        out_specs=pl.BlockSpec((tm, W), lambda i: (i, 0)),
        out_shape=jax.ShapeDtypeStruct((S, W), BF16),
        compiler_params=_cp("parallel"),
    )(u, u, pool_w, pool_scale)


def _pool_bwd(u, u_col, dy, dy_col, pool_w, pool_scale, *, name, tm=256):
    S, W = u.shape[0], POOL_WIDTH
    G = POOL_GROUP_DIM
    nt = S // tm

    def body(u_ref, h_ref, dy_ref, dyn_ref, w_ref, s_ref, du_ref, gw_ref, gs_ref):
        i = pl.program_id(0)

        @pl.when(i == 0)
        def _():
            gw_ref[...] = jnp.zeros_like(gw_ref)
            gs_ref[...] = jnp.zeros_like(gs_ref)

        uv = u_ref[...]
        halo = jnp.where(i > 0, h_ref[...], 0.0)
        dyv = dy_ref[...]
        dyn = jnp.where(i < nt - 1, dyn_ref[...], 0.0)
        rr = lax.broadcasted_iota(jnp.int32, (tm, tm + HALO), 0)
        cc = lax.broadcasted_iota(jnp.int32, (tm, tm + HALO), 1)
        rows_ext = i * tm + lax.broadcasted_iota(jnp.int32, (tm + HALO, 1), 0)
        for g, w in enumerate(POOL_WINDOWS):
            sl = slice(g * G, (g + 1) * G)
            wg = w_ref[g].astype(BF16)
            sc = s_ref[:, sl]
            pooled = _pooled(uv[:, sl], halo[:, sl], w, i * tm, tm)
            z = _dot(pooled.astype(BF16), wg, NN)
            gs_ref[:, sl] += jnp.sum(dyv[:, sl] * z, axis=0, keepdims=True)
            dz = dyv[:, sl] * sc
            gw_ref[g] += _dot(pooled.astype(BF16), dz.astype(BF16), TN)
            dz_ext = jnp.concatenate([dz, dyn[:, sl] * sc], axis=0)
            dp_ext = _dot(dz_ext.astype(BF16), wg, NT)
            inv_ext = 1.0 / jnp.minimum(rows_ext + 1, w).astype(F32)
            hi, lo = _split_bf16(dp_ext * inv_ext)
            ahead = cc - rr
            win = ((ahead >= 0) & (ahead < w)).astype(BF16)
            du_ref[:, sl] = (_dot(win, hi, NN) + _dot(win, lo, NN) - dp_ext[:tm]).astype(BF16)

    per = tm // HALO
    nh = S // HALO
    return pl.pallas_call(
        body, name=name, grid=(nt,),
        in_specs=[pl.BlockSpec((tm, W), lambda i: (i, u_col)),
                  pl.BlockSpec((HALO, W), lambda i: (jnp.maximum(i * per - 1, 0), u_col)),
                  pl.BlockSpec((tm, W), lambda i: (i, dy_col)),
                  pl.BlockSpec((HALO, W), lambda i: (jnp.minimum((i + 1) * per, nh - 1), dy_col)),
                  pl.BlockSpec((len(POOL_WINDOWS), G, G), lambda i: (0, 0, 0)),
                  pl.BlockSpec((1, W), lambda i: (0, 0))],
        out_specs=[pl.BlockSpec((tm, W), lambda i: (i, 0)),
                   pl.BlockSpec((len(POOL_WINDOWS), G, G), lambda i: (0, 0, 0)),
                   pl.BlockSpec((1, W), lambda i: (0, 0))],
        out_shape=[jax.ShapeDtypeStruct((S, W), BF16),
                   jax.ShapeDtypeStruct((len(POOL_WINDOWS), G, G), F32),
                   jax.ShapeDtypeStruct((1, W), F32)],
        compiler_params=_cp("arbitrary"),
    )(u, u, dy, dy, pool_w, pool_scale)


GELU_K0 = math.sqrt(2.0 / math.pi)
GELU_K1 = 0.044715


def _gelu_parts(x):
    x2 = x * x
    t = jnp.tanh(x * (GELU_K0 + (GELU_K0 * GELU_K1) * x2))
    hp = 0.5 + 0.5 * t
    gelu = x * hp
    dgelu = hp + (x * (hp * (1.0 - t))) * (GELU_K0 + (3.0 * GELU_K0 * GELU_K1) * x2)
    return gelu, dgelu


def _shifted(ext, halo):
    return (pltpu.roll(ext, 2, 0)[halo:], pltpu.roll(ext, 1, 0)[halo:], ext[halo:])


def _conv(sh, w, b):
    return b + (sh[0] * w[0:1] + sh[1] * w[1:2] + sh[2] * w[2:3])


F32_ROWS = 8


def _ffn_up_glu(h, w_up_t, conv_w, conv_b, *, name, tm=1024, tn=256, sub=512):
    S, K = h.shape
    F = D_FF
    nj = F // tn

    def body(h_ref, wg_ref, wv_ref, cwg_ref, cwv_ref, cbg_ref, cbv_ref,
             ug_ref, uv_ref, cg_ref, cv_ref, y_ref, carry):
        i = pl.program_id(0)
        j = pl.program_id(1)

        w_cat = jnp.concatenate([wg_ref[...], wv_ref[...]], axis=0)
        conv_w_b = ((cwg_ref[...], cbg_ref[...]), (cwv_ref[...], cbv_ref[...]))
        halo = [jnp.where(i > 0, carry[j, s], 0.0) for s in range(2)]
        for a in range(0, tm, sub):
            u16 = _dot(h_ref[a:a + sub, :], w_cat, NT).astype(BF16)
            ug_ref[a:a + sub, :] = u16[:, :tn]
            uv_ref[a:a + sub, :] = u16[:, tn:]
            c = []
            for s, (cw, cb) in enumerate(conv_w_b):
                u = u16[:, s * tn:(s + 1) * tn].astype(F32)
                ext = jnp.concatenate([halo[s], u], axis=0)
                c.append(_conv(_shifted(ext, F32_ROWS), cw, cb))
                halo[s] = u[sub - F32_ROWS:]
            cg_ref[a:a + sub, :] = c[0].astype(BF16)
            cv_ref[a:a + sub, :] = c[1].astype(BF16)
            gelu, _ = _gelu_parts(c[0])
            y_ref[a:a + sub, :] = (gelu * c[1]).astype(BF16)
        for s in range(2):
            carry[j, s] = halo[s]

    tile = pl.BlockSpec((tm, tn), lambda i, j: (i, j))
    vec = lambda rows, off: pl.BlockSpec((rows, tn), lambda i, j: (0, j + off))
    return pl.pallas_call(
        body, name=name, grid=(S // tm, nj),
        in_specs=[pl.BlockSpec((tm, K), lambda i, j: (i, 0)),
                  pl.BlockSpec((tn, K), lambda i, j: (j, 0)), pl.BlockSpec((tn, K), lambda i, j: (j + nj, 0)),
                  vec(3, 0), vec(3, nj), vec(1, 0), vec(1, nj)],
        out_specs=[tile] * 5,
        out_shape=[jax.ShapeDtypeStruct((S, F), BF16)] * 5,
        scratch_shapes=[pltpu.VMEM((nj, 2, F32_ROWS, tn), F32)],
        compiler_params=_cp("arbitrary", "arbitrary"),
    )(h, w_up_t, w_up_t, conv_w, conv_w, conv_b, conv_b)


def _ffn_glu_bwd(u_g, u_v, c_g, c_v, df, w_down, h, conv_w, *, name, tm=512, tn=256, sub=256):
    S = u_g.shape[0]
    F = D_FF
    D = df.shape[1]
    nj = F // tn
    nt = S // tm

    def body(ug_ref, uv_ref, cg_ref, cgn_ref, cv_ref, cvn_ref, df_ref, dfn_ref, wd_ref, h_ref, wg_ref, wv_ref,
             dug_ref, duv_ref, gug_ref, guv_ref, gd_ref, gwg_ref, gwv_ref, gbg_ref, gbv_ref,
             acc_u, acc_d):
        i = pl.program_id(1)

        @pl.when(i == 0)
        def _():
            for r in (gwg_ref, gwv_ref, gbg_ref, gbv_ref, acc_u, acc_d):
                r[...] = jnp.zeros_like(r)

        wg, wv = wg_ref[...], wv_ref[...]
        wd = wd_ref[...]
        dfn = jnp.where(i < nt - 1, dfn_ref[...], jnp.zeros_like(dfn_ref))
        n_ext = sub + HALO

        def ahead(dc):
            return dc[:sub], pltpu.roll(dc, n_ext - 1, 0)[:sub], pltpu.roll(dc, n_ext - 2, 0)[:sub]

        for a in range(0, tm, sub):
            b = a + sub
            ext = lambda ref, nxt: jnp.concatenate(
                [ref[a:b, :], ref[b:b + HALO, :] if b < tm else nxt], axis=0)
            cg = ext(cg_ref, cgn_ref[...]).astype(F32)
            cv = ext(cv_ref, cvn_ref[...]).astype(F32)
            df_sub = df_ref[a:b, :]
            dy_ext = _dot(ext(df_ref, dfn), wd, NT)
            gelu, dgelu = _gelu_parts(cg)
            dcs_g = ahead(dy_ext * cv * dgelu)
            dcs_v = ahead(dy_ext * gelu)
            du_g = (dcs_g[0] * wg[2:3] + dcs_g[1] * wg[1:2] + dcs_g[2] * wg[0:1]).astype(BF16)
            du_v = (dcs_v[0] * wv[2:3] + dcs_v[1] * wv[1:2] + dcs_v[2] * wv[0:1]).astype(BF16)
            dug_ref[a:b, :] = du_g
            duv_ref[a:b, :] = du_v
            acc_u[...] += _dot(jnp.concatenate([du_g, du_v], axis=1), h_ref[a:b, :], TN)
            acc_d[...] += _dot((gelu[:sub] * cv[:sub]).astype(BF16), df_sub, TN)
            for dcs, u_ref, gw_ref, gb_ref in ((dcs_g, ug_ref, gwg_ref, gbg_ref), (dcs_v, uv_ref, gwv_ref, gbv_ref)):
                u = u_ref[a:b, :].astype(F32)
                gb_ref[...] += jnp.sum(dcs[0], axis=0, keepdims=True)
                for k in range(3):
                    gw_ref[k:k + 1, :] += jnp.sum(dcs[2 - k] * u, axis=0, keepdims=True)

        @pl.when(i == nt - 1)
        def _():
            gug_ref[...] = acc_u[:tn, :].astype(BF16)
            guv_ref[...] = acc_u[tn:, :].astype(BF16)
            gd_ref[...] = acc_d[...].astype(BF16)

    per = tm // HALO
    nh = S // HALO
    hnext = lambda i: jnp.minimum((i + 1) * per, nh - 1)
    tile = pl.BlockSpec((tm, tn), lambda j, i: (i, j))
    hn = pl.BlockSpec((HALO, tn), lambda j, i: (hnext(i), j))
    vec = lambda rows, off: pl.BlockSpec((rows, tn), lambda j, i: (0, j + off))
    wide = pl.BlockSpec((tm, D), lambda j, i: (i, 0))
    wrow = pl.BlockSpec((tn, D), lambda j, i: (j, 0))
    return pl.pallas_call(
        body, name=name, grid=(nj, nt),
        in_specs=[tile, tile, tile, hn, tile, hn, wide, pl.BlockSpec((HALO, D), lambda j, i: (hnext(i), 0)),
                  wrow, wide, vec(3, 0), vec(3, nj)],
        out_specs=[tile, tile, wrow, wrow, wrow, vec(3, 0), vec(3, 0), vec(1, 0), vec(1, 0)],
        out_shape=[jax.ShapeDtypeStruct((S, F), BF16), jax.ShapeDtypeStruct((S, F), BF16),
                   jax.ShapeDtypeStruct((F, D), BF16), jax.ShapeDtypeStruct((F, D), BF16),
                   jax.ShapeDtypeStruct((F, D), BF16),
                   jax.ShapeDtypeStruct((3, F), F32), jax.ShapeDtypeStruct((3, F), F32),
                   jax.ShapeDtypeStruct((1, F), F32), jax.ShapeDtypeStruct((1, F), F32)],
        scratch_shapes=[pltpu.VMEM((2 * tn, D), F32), pltpu.VMEM((tn, D), F32)],
        compiler_params=_cp("parallel", "arbitrary"),
    )(u_g, u_v, c_g, c_g, c_v, c_v, df, df, w_down, h, conv_w, conv_w)


def _sum_partials(parts, *, name, tr):
    _, R, C = parts.shape

    def body(p_ref, o_ref):
        tot = p_ref[0].astype(F32)
        for j in range(1, N_DEV):
            tot = tot + p_ref[j].astype(F32)
        o_ref[...] = tot

    return pl.pallas_call(
        body, name=name, grid=(R // tr,),
        in_specs=[pl.BlockSpec((N_DEV, tr, C), lambda i: (0, i, 0))],
        out_specs=pl.BlockSpec((tr, C), lambda i: (i, 0)),
        out_shape=jax.ShapeDtypeStruct((R, C), F32),
        compiler_params=_cp("parallel"),
    )(parts)


def _adamw(w, g, m, v, *, name, tr):
    R, C = w.shape
    c1 = 1.0 - ADAM_B1 ** ADAM_STEP
    c2 = 1.0 - ADAM_B2 ** ADAM_STEP

    def body(w_ref, g_ref, m_ref, v_ref, d_ref, nm_ref, nv_ref):
        g = g_ref[...]
        nm = ADAM_B1 * m_ref[...] + (1.0 - ADAM_B1) * g
        nv = ADAM_B2 * v_ref[...] + (1.0 - ADAM_B2) * (g * g)
        d_ref[...] = -ADAM_LR * ((nm / c1) / (jnp.sqrt(nv / c2) + ADAM_EPS) + ADAM_WD * w_ref[...])
        nm_ref[...] = nm
        nv_ref[...] = nv

    spec = pl.BlockSpec((tr, C), lambda i: (i, 0))
    return pl.pallas_call(
        body, name=name, grid=(R // tr,), in_specs=[spec] * 4, out_specs=[spec] * 3,
        out_shape=[jax.ShapeDtypeStruct((R, C), F32)] * 3,
        compiler_params=_cp("parallel"),
    )(w, g, m, v)


def _mesh_pos():
    return lax.axis_index("x"), lax.axis_index("y"), lax.axis_index("c")


def _gather_phases(x_ref, out_ref, send_sems, recv_sems, local_sem):
    x, y, c = _mesh_pos()
    me, sibling = (x, y, c), (x, y, 1 - c)
    chips = [(1 - x, y), (x, 1 - y), (1 - x, 1 - y)]

    def slot(px, py, pc):
        return out_ref.at[4 * px + 2 * py + pc]

    def copy(k, block, to, src=None):
        return pltpu.make_async_remote_copy(
            src_ref=slot(*block) if src is None else src, dst_ref=slot(*block),
            send_sem=send_sems.at[k], recv_sem=recv_sems.at[k], device_id=to, device_id_type=MESH)

    mine = pltpu.make_async_copy(x_ref, slot(*me), local_sem)
    first = [copy(0, me, sibling, src=x_ref)]
    first += [copy(1 + j, me, (*chip, c), src=x_ref) for j, chip in enumerate(chips)]
    passed = [copy(4 + j, (*chip, c), sibling) for j, chip in enumerate(chips)]

    def start():
        mine.start()
        for cp in first:
            cp.start()

    def forward():
        for j, chip in enumerate(chips):
            copy(1 + j, (*chip, c), me).wait_recv()
            passed[j].start()

    def finish():
        copy(0, sibling, me).wait_recv()
        for j, chip in enumerate(chips):
            copy(4 + j, (*chip, 1 - c), me).wait_recv()
        for cp in first + passed:
            cp.wait_send()
        mine.wait()

    return start, forward, finish


def _two_level_gather(x_ref, out_ref, send_sems, recv_sems, local_sem):
    for phase in _gather_phases(x_ref, out_ref, send_sems, recv_sems, local_sem):
        phase()


_GATHER_SEMS = [pltpu.SemaphoreType.DMA((7,)), pltpu.SemaphoreType.DMA((7,)), pltpu.SemaphoreType.DMA]


def _all_gather_hbm(block, *, name):
    def body(x_ref, out_ref, send_sems, recv_sems, local_sem):
        _two_level_gather(x_ref, out_ref, send_sems, recv_sems, local_sem)

    return pl.pallas_call(
        body, name=name, in_specs=[ANY], out_specs=ANY,
        out_shape=jax.ShapeDtypeStruct((N_DEV,) + block.shape, block.dtype),
        scratch_shapes=_GATHER_SEMS,
    )(block)


def _all_reduce_small(block, *, name):
    def body(x_ref, all_ref, sum_ref, send_sems, recv_sems, local_sem):
        _two_level_gather(x_ref, all_ref, send_sems, recv_sems, local_sem)
        tot = all_ref[0]
        for j in range(1, N_DEV):
            tot = tot + all_ref[j]
        sum_ref[...] = tot

    return pl.pallas_call(
        body, name=name, in_specs=[VMEM], out_specs=[VMEM, VMEM],
        out_shape=[jax.ShapeDtypeStruct((N_DEV,) + block.shape, block.dtype),
                   jax.ShapeDtypeStruct(block.shape, block.dtype)],
        scratch_shapes=_GATHER_SEMS,
        compiler_params=pltpu.CompilerParams(vmem_limit_bytes=V7X_VMEM_LIMIT),
    )(block)[1]


def _exchange_phases(g_refs, r_refs, send_sems, recv_sems, local_sems):
    x, y, c = _mesh_pos()
    me = 4 * x + 2 * y + c
    owns, remote = [], []
    for k, (g_ref, r_ref) in enumerate(zip(g_refs, r_refs)):
        rows = g_ref.shape[0] // N_DEV
        owns.append(pltpu.make_async_copy(g_ref.at[pl.ds(me * rows, rows)], r_ref.at[me], local_sems.at[k]))
        for p in range(1, N_DEV):
            px, py, pc = x ^ (p >> 2), y ^ ((p >> 1) & 1), c ^ (p & 1)
            peer = 4 * px + 2 * py + pc
            link = dict(send_sem=send_sems.at[k, p], recv_sem=recv_sems.at[k, p],
                        device_id=(px, py, pc), device_id_type=MESH)
            src = g_ref.at[pl.ds(peer * rows, rows)]
            send = pltpu.make_async_remote_copy(src_ref=src, dst_ref=r_ref.at[me], **link)
            arrival = pltpu.make_async_remote_copy(src_ref=src, dst_ref=r_ref.at[peer], **link)
            remote.append((send, arrival))

    def start():
        for own in owns:
            own.start()
        for send, _ in remote:
            send.start()

    def finish():
        for _, arrival in remote:
            arrival.wait_recv()
        for send, _ in remote:
            send.wait_send()
        for own in owns:
            own.wait()

    return start, finish


def _exchange_buffers(grads):
    n = len(grads)
    shapes = [jax.ShapeDtypeStruct((N_DEV, g.shape[0] // N_DEV, g.shape[1]), g.dtype) for g in grads]
    sems = [pltpu.SemaphoreType.DMA((n, N_DEV)), pltpu.SemaphoreType.DMA((n, N_DEV)),
            pltpu.SemaphoreType.DMA((n,))]
    return shapes, sems


def _unpack_gathered(gathered):
    n_out, n_up, n_down = D_MODEL, 2 * D_FF, D_FF
    offs, row = [], 0
    for n in (n_out, n_up, n_down):
        offs.append((row, row + n // N_DEV))
        row += n // N_DEV
    w_out, w_up_t, w_down = (gathered[:, a:b].reshape(-1, D_MODEL) for a, b in offs)
    n_cw = 3 * (2 * D_FF // N_DEV)
    cw_all = gathered[:, row:].reshape(N_DEV, -1)[:, :2 * n_cw].reshape(N_DEV, n_cw, 2)
    cw_all = lax.bitcast_convert_type(cw_all, F32).reshape(N_DEV, 3, -1)
    conv_w = jnp.transpose(cw_all, (1, 0, 2)).reshape(3, 2 * D_FF)
    return w_out, w_up_t, w_down, conv_w


def _rest_payload(w_out, w_up, w_down, conv_w):
    cw_bits = lax.bitcast_convert_type(conv_w.reshape(-1), BF16).reshape(-1)
    cw_bits = jnp.pad(cw_bits, (0, HALO * D_MODEL - cw_bits.shape[0])).reshape(HALO, D_MODEL)
    return jnp.concatenate([w_out.astype(BF16), w_up.T.astype(BF16), w_down.astype(BF16), cw_bits], axis=0)


def _device_step(x, target, g_mix_pre, w_in_t, rest_payload, pool_w, pool_scale, g_mix_post, g_ffn_pre,
                 conv_b, g_ffn_post):
    h1 = _rms_norm(x, g_mix_pre, name="rms_mix_pre")
    proj = _matmul(h1, w_in_t, trans_b=True, out_dtype=F32, tm=1024, tn=512, name="proj")
    attn, lse, attn16, gathered = _attn_fwd(proj, rest_payload, name="attn_fwd")
    w_out, w_up_t, w_down, conv_w = _unpack_gathered(gathered)
    pool = _pool_fwd(proj, 3, pool_w, pool_scale, name="pool_fwd")
    mixed, x2, h2 = _mix_out(attn16, pool, w_out, x, g_mix_post, g_ffn_pre, name="mix_out")
    u_g, u_v, c_g, c_v, y = _ffn_up_glu(h2, w_up_t, conv_w, conv_b, name="ffn_up_glu")
    df, d_out, loss_blk, gg_ffn_post = _ffn_out(y, w_down, x2, target, g_ffn_post, name="ffn_out")
    du_g, du_v, gw_up_g, gw_up_v, gw_down, gcw_g, gcw_v, gcb_g, gcb_v = _ffn_glu_bwd(
        u_g, u_v, c_g, c_v, df, w_down, h2, conv_w, name="ffn_glu_bwd")
    gw_up_t = jnp.concatenate([gw_up_g, gw_up_v], axis=0)
    dx2, gg_ffn_pre, dmixed, gg_mix_post = _dgrad_norm(
        [du_g, du_v], w_up_t, d_out, x2, g_ffn_pre, (mixed, g_mix_post), [], name="ffn_up_dgrad")
    gw_out = jnp.concatenate([_matmul_tn(attn16, dmixed, ta=512, ts=1024, name="grad_w_out_attn"),
                              _matmul_tn(pool, dmixed, ta=512, ts=1024, name="grad_w_out_pool")], axis=0)
    dcat = _matmul(dmixed, w_out, trans_b=True, out_dtype=F32, tm=512, tn=1024, name="mix_out_dgrad")
    d_pool_in, g_pool_w, g_pool_scale = _pool_bwd(proj, 3, dcat, 1, pool_w, pool_scale, name="pool_bwd")
    dqkv, (r_out, r_up_t, r_down) = _attn_bwd(proj, dcat, attn, lse, [gw_out, gw_up_t, gw_down], name="attn_bwd")
    dproj = list(dqkv) + [d_pool_in]
    gw_in_t = jnp.concatenate([_matmul_tn(a, h1, ta=512, ts=1024, name=f"grad_w_in_{k}")
                               for k, a in enumerate(dproj)], axis=0)
    grad_x, gg_mix_pre, (r_in_t,) = _dgrad_norm(dproj, w_in_t, dx2, x, g_mix_pre, None, [gw_in_t], name="proj_dgrad")
    g_conv_w = jnp.concatenate([gcw_g, gcw_v], axis=1)
    g_conv_b = jnp.concatenate([gcb_g, gcb_v], axis=1)
    received = (r_in_t, r_out, r_up_t, r_down)
    small = dict(g_mix_pre=gg_mix_pre, g_mix_post=gg_mix_post, g_ffn_pre=gg_ffn_pre, g_ffn_post=gg_ffn_post,
                 pool_scale=g_pool_scale, conv_b=g_conv_b, pool_w=g_pool_w, conv_w=g_conv_w)
    return loss_blk, grad_x, received, small


_SMALL = ("g_mix_pre", "g_mix_post", "g_ffn_pre", "g_ffn_post", "pool_scale", "conv_b", "pool_w")
LANES = 128


def _pack_rows(arrays):
    parts = []
    for a in arrays:
        a2 = a.reshape(-1, LANES)
        parts.append(jnp.pad(a2, ((0, (-a2.shape[0]) % 8), (0, 0))))
    return jnp.concatenate(parts, axis=0)


def _unpack_rows(packed, shapes):
    out, row = [], 0
    for shape in shapes:
        rows = math.prod(shape) // LANES
        out.append(packed[row:row + rows].reshape(shape))
        row += -(-rows // 8) * 8
    return out


def kernel(x, g_mix_pre, w_in, pool_w, pool_scale, w_out, g_mix_post, g_ffn_pre, w_up, conv_w, conv_b, w_down, g_ffn_post, loss_target, m_g_mix_pre, m_w_in, m_pool_w, m_pool_scale, m_w_out, m_g_mix_post, m_g_ffn_pre, m_w_up, m_conv_w, m_conv_b, m_w_down, m_g_ffn_post, v_g_mix_pre, v_w_in, v_pool_w, v_pool_scale, v_w_out, v_g_mix_post, v_g_ffn_pre, v_w_up, v_conv_w, v_conv_b, v_w_down, v_g_ffn_post):
    me = 4 * lax.axis_index("x") + 2 * lax.axis_index("y") + lax.axis_index("c")
    w_in_t = _all_gather_hbm(w_in[0].T.astype(BF16), name="gather_w_in").reshape(4 * ATTN_WIDTH, D_MODEL)
    loss_blk, grad_x, recv, small = _device_step(
        x[0], loss_target[0], g_mix_pre, w_in_t, _rest_payload(w_out[0], w_up[0], w_down[0], conv_w[0]),
        pool_w[0], pool_scale, g_mix_post, g_ffn_pre, conv_b, g_ffn_post)
    loss = lax.psum(loss_blk[0, 0], ("x", "y", "c"))

    g_in_t, g_out, g_up_t, g_down = (
        _sum_partials(r, name=f"sum_partials_{k}", tr=r.shape[1] // 2) for k, r in enumerate(recv))
    grads = {"w_in": g_in_t.T, "w_out": g_out, "w_up": g_up_t.T, "w_down": g_down}

    given = dict(g_mix_pre=g_mix_pre, g_mix_post=g_mix_post, g_ffn_pre=g_ffn_pre, g_ffn_post=g_ffn_post,
                 pool_scale=pool_scale, conv_b=conv_b, pool_w=pool_w)
    small_shapes = [given[k].shape for k in _SMALL]
    total = _all_reduce_small(_pack_rows([small[k] for k in _SMALL] + [small["conv_w"]]), name="all_reduce_small")
    *small_grads, g_conv_w_all = _unpack_rows(total, small_shapes + [(3, 2 * D_FF)])
    grads.update(zip(_SMALL, small_grads))
    width = 2 * D_FF // N_DEV
    grads["conv_w"] = lax.dynamic_slice_in_dim(g_conv_w_all, me * width, width, axis=1)[None]

    weights = dict(g_mix_pre=g_mix_pre, w_in=w_in, pool_w=pool_w, pool_scale=pool_scale, w_out=w_out,
                   g_mix_post=g_mix_post, g_ffn_pre=g_ffn_pre, w_up=w_up, conv_w=conv_w, conv_b=conv_b,
                   w_down=w_down, g_ffn_post=g_ffn_post)
    m_in = dict(g_mix_pre=m_g_mix_pre, w_in=m_w_in, pool_w=m_pool_w, pool_scale=m_pool_scale, w_out=m_w_out,
                g_mix_post=m_g_mix_post, g_ffn_pre=m_g_ffn_pre, w_up=m_w_up, conv_w=m_conv_w, conv_b=m_conv_b,
                w_down=m_w_down, g_ffn_post=m_g_ffn_post)
    v_in = dict(g_mix_pre=v_g_mix_pre, w_in=v_w_in, pool_w=v_pool_w, pool_scale=v_pool_scale, w_out=v_w_out,
                g_mix_post=v_g_mix_post, g_ffn_pre=v_g_ffn_pre, w_up=v_w_up, conv_w=v_conv_w, conv_b=v_conv_b,
                w_down=v_w_down, g_ffn_post=v_g_ffn_post)
    delta, new_m, new_v = {}, {}, {}
    for k in ("w_in", "w_out", "w_up", "w_down"):
        g = grads[k]
        d, nm, nv = _adamw(weights[k][0], g, m_in[k][0], v_in[k][0], name=f"adamw_{k}", tr=g.shape[0] // 2)
        grads[k], delta[k], new_m[k], new_v[k] = g[None], d[None], nm[None], nv[None]
    d, nm, nv = _adamw(weights["conv_w"][0], grads["conv_w"][0], m_in["conv_w"][0], v_in["conv_w"][0],
                       name="adamw_conv_w", tr=3)
    delta["conv_w"], new_m["conv_w"], new_v["conv_w"] = d[None], nm[None], nv[None]
    packed_w = _pack_rows([weights[k] for k in _SMALL])
    small_rows = packed_w.shape[0]
    d, nm, nv = _adamw(packed_w, total[:small_rows], _pack_rows([m_in[k] for k in _SMALL]),
                       _pack_rows([v_in[k] for k in _SMALL]), name="adamw_small", tr=small_rows)
    for k, dk, mk, vk in zip(_SMALL, _unpack_rows(d, small_shapes), _unpack_rows(nm, small_shapes),
                             _unpack_rows(nv, small_shapes)):
        delta[k], new_m[k], new_v[k] = dk, mk, vk

    order = ("g_mix_pre", "w_in", "pool_w", "pool_scale", "w_out", "g_mix_post", "g_ffn_pre", "w_up",
             "conv_w", "conv_b", "w_down", "g_ffn_post")
    return (loss, grad_x[None], *[grads[k] for k in order], *[delta[k] for k in order],
            *[new_m[k] for k in order], *[new_v[k] for k in order])
```

```python
import functools
import math

import jax
import jax.numpy as jnp
from jax import lax
from jax.experimental import pallas as pl
from jax.experimental.pallas import tpu as pltpu

F32 = jnp.float32
BF16 = jnp.bfloat16

D_MODEL = 1024
ATTN_WIDTH = 512
N_HEADS = 8
HEAD_DIM = 64
DILATIONS = (1, 4, 16)
BLOCK = 128
POOL_WIDTH = 512
POOL_WINDOWS = (2, 4, 8, 16)
POOL_GROUP_DIM = 128
D_FF = 2816
EPS = 1e-6
NEG_INF = -1e30
SCALE = HEAD_DIM ** -0.5

ADAM_LR = 0.001
ADAM_B1 = 0.9
ADAM_B2 = 0.999
ADAM_EPS = 1e-08
ADAM_WD = 0.01
ADAM_STEP = 10

N_DEV = 8
HALO = 16
V7X_VMEM_LIMIT = 56 * 1024 * 1024

MESH = pl.DeviceIdType.MESH
ANY = pl.BlockSpec(memory_space=pl.ANY)
VMEM = pl.BlockSpec(memory_space=pltpu.VMEM)

NT = (((1,), (1,)), ((), ()))
NN = (((1,), (0,)), ((), ()))
TN = (((0,), (0,)), ((), ()))


def _cp(*sem):
    return pltpu.CompilerParams(dimension_semantics=sem, vmem_limit_bytes=V7X_VMEM_LIMIT)


def _dot(a, b, dn):
    return lax.dot_general(a, b, dn, preferred_element_type=F32)


def _rms_bwd(xin, g, dy):
    r = lax.rsqrt(jnp.mean(xin * xin, axis=-1, keepdims=True) + EPS)
    xh = xin * r
    gdy = g * dy
    dx = r * (gdy - xh * jnp.mean(gdy * xh, axis=-1, keepdims=True))
    dg = jnp.sum(dy * xh, axis=0, keepdims=True)
    return dx, dg


def _rms_norm_gather(x, g, block, *, name, tm=512):
    S, D = x.shape
    nt = S // tm

    def body(x_ref, g_ref, blk_ref, o_ref, all_ref, *sems):
        i = pl.program_id(0)
        start, forward, finish = _gather_phases([blk_ref], [all_ref], *sems)
        pl.when(i == 0)(start)
        xv = x_ref[...]
        r = lax.rsqrt(jnp.mean(xv * xv, axis=-1, keepdims=True) + EPS)
        o_ref[...] = (xv * r * g_ref[...]).astype(BF16)
        pl.when(i == (2 * nt) // 3)(forward)
        pl.when(i == nt - 1)(finish)

    return pl.pallas_call(
        body, name=name, grid=(nt,),
        in_specs=[pl.BlockSpec((tm, D), lambda i: (i, 0)), pl.BlockSpec((1, D), lambda i: (0, 0)), ANY],
        out_specs=[pl.BlockSpec((tm, D), lambda i: (i, 0)), ANY],
        out_shape=[jax.ShapeDtypeStruct((S, D), BF16)] + _gathered_shapes([block]),
        scratch_shapes=_gather_sems(1),
        compiler_params=_cp("arbitrary"),
    )(x, g, block)


def _matmul(a, b, *, trans_b, out_dtype, tm, tn, name):
    M, K = a.shape
    N = b.shape[0] if trans_b else b.shape[1]
    dn = NT if trans_b else NN

    def body(a_ref, b_ref, o_ref):
        o_ref[...] = _dot(a_ref[...], b_ref[...], dn).astype(out_dtype)

    b_spec = (pl.BlockSpec((tn, K), lambda i, j: (j, 0)) if trans_b
              else pl.BlockSpec((K, tn), lambda i, j: (0, j)))
    return pl.pallas_call(
        body, name=name, grid=(M // tm, N // tn),
        in_specs=[pl.BlockSpec((tm, K), lambda i, j: (i, 0)), b_spec],
        out_specs=pl.BlockSpec((tm, tn), lambda i, j: (i, j)),
        out_shape=jax.ShapeDtypeStruct((M, N), out_dtype),
        compiler_params=_cp("parallel", "parallel"),
    )(a, b)


def _matmul_tn(a, b, *, ta, ts, name):
    S, Ka = a.shape
    Nb = b.shape[1]
    ns = S // ts

    def body(a_ref, b_ref, o_ref, acc):
        s = pl.program_id(1)

        @pl.when(s == 0)
        def _():
            acc[...] = jnp.zeros_like(acc)

        acc[...] += _dot(a_ref[...], b_ref[...], TN)

        @pl.when(s == ns - 1)
        def _():
            o_ref[...] = acc[...].astype(BF16)

    return pl.pallas_call(
        body, name=name, grid=(Ka // ta, ns),
        in_specs=[pl.BlockSpec((ts, ta), lambda i, s: (s, i)), pl.BlockSpec((ts, Nb), lambda i, s: (s, 0))],
        out_specs=pl.BlockSpec((ta, Nb), lambda i, s: (i, 0)),
        out_shape=jax.ShapeDtypeStruct((Ka, Nb), BF16),
        scratch_shapes=[pltpu.VMEM((ta, Nb), F32)],
        compiler_params=_cp("parallel", "arbitrary"),
    )(a, b)


def _mix_out(attn, pool, w_out, x, g_post, g_next, *, name, tm=256):
    S, K = attn.shape
    D = w_out.shape[1]

    def body(a_ref, p_ref, w_ref, x_ref, gp_ref, gn_ref, mixed_ref, x2_ref, h2_ref):
        mixed = _dot(a_ref[...], w_ref[:K, :], NN) + _dot(p_ref[...], w_ref[K:, :], NN)
        r = lax.rsqrt(jnp.mean(mixed * mixed, axis=-1, keepdims=True) + EPS)
        x2 = x_ref[...] + mixed * r * gp_ref[...]
        r2 = lax.rsqrt(jnp.mean(x2 * x2, axis=-1, keepdims=True) + EPS)
        mixed_ref[...] = mixed
        x2_ref[...] = x2
        h2_ref[...] = (x2 * r2 * gn_ref[...]).astype(BF16)

    row = lambda i: (i, 0)
    fix = lambda i: (0, 0)
    return pl.pallas_call(
        body, name=name, grid=(S // tm,),
        in_specs=[pl.BlockSpec((tm, K), row), pl.BlockSpec((tm, K), row), pl.BlockSpec((2 * K, D), fix),
                  pl.BlockSpec((tm, D), row), pl.BlockSpec((1, D), fix), pl.BlockSpec((1, D), fix)],
        out_specs=[pl.BlockSpec((tm, D), row)] * 3,
        out_shape=[jax.ShapeDtypeStruct((S, D), F32), jax.ShapeDtypeStruct((S, D), F32),
                   jax.ShapeDtypeStruct((S, D), BF16)],
        compiler_params=_cp("parallel"),
    )(attn, pool, w_out, x, g_post, g_next)


def _ffn_out(y, w_down, x2, target, g_post, *, name, tm=256):
    S, K = y.shape
    D = w_down.shape[1]

    def body(y_ref, w_ref, x2_ref, t_ref, g_ref, df_ref, dout_ref, loss_ref, gg_ref):
        i = pl.program_id(0)

        @pl.when(i == 0)
        def _():
            loss_ref[...] = jnp.zeros_like(loss_ref)
            gg_ref[...] = jnp.zeros_like(gg_ref)

        f = _dot(y_ref[...], w_ref[...], NN)
        g = g_ref[...]
        r = lax.rsqrt(jnp.mean(f * f, axis=-1, keepdims=True) + EPS)
        out = x2_ref[...] + f * r * g
        err = out - t_ref[...]
        dy = err * (1.0 / D)
        df, dg = _rms_bwd(f, g, dy)
        df_ref[...] = df.astype(BF16)
        dout_ref[...] = dy
        gg_ref[...] += dg
        loss_ref[...] += 0.5 * jnp.sum(jnp.mean(err * err, axis=-1, keepdims=True))

    row = lambda i: (i, 0)
    fix = lambda i: (0, 0)
    return pl.pallas_call(
        body, name=name, grid=(S // tm,),
        in_specs=[pl.BlockSpec((tm, K), row), pl.BlockSpec((K, D), fix), pl.BlockSpec((tm, D), row),
                  pl.BlockSpec((tm, D), row), pl.BlockSpec((1, D), fix)],
        out_specs=[pl.BlockSpec((tm, D), row), pl.BlockSpec((tm, D), row),
                   pl.BlockSpec((8, 128), fix), pl.BlockSpec((1, D), fix)],
        out_shape=[jax.ShapeDtypeStruct((S, D), BF16), jax.ShapeDtypeStruct((S, D), F32),
                   jax.ShapeDtypeStruct((8, 128), F32), jax.ShapeDtypeStruct((1, D), F32)],
        compiler_params=_cp("arbitrary"),
    )(y, w_down, x2, target, g_post)


def _dgrad_norm(a_list, w, resid, xin, g, second, exchange, *, name, tm=512):
    S, Kp = a_list[0].shape
    na = len(a_list)
    D = w.shape[1]
    nt = S // tm
    two = second is not None
    ng = len(exchange)
    recv_shapes, exchange_sems = _exchange_buffers(exchange)

    def body(*refs):
        a_refs = refs[:na]
        w_ref, r_ref, x_ref, g_ref = refs[na:na + 4]
        pos = na + 4
        if two:
            x2_ref, g2_ref = refs[pos:pos + 2]
            pos += 2
        g_refs = refs[pos:pos + ng]
        pos += ng
        dx_ref, gg_ref = refs[pos:pos + 2]
        pos += 2
        if two:
            d2_ref, gg2_ref = refs[pos:pos + 2]
            pos += 2
        r_refs = refs[pos:pos + ng]
        pos += ng
        i = pl.program_id(0)
        if ng:
            start, finish = _exchange_phases(g_refs, r_refs, *refs[pos:])
            pl.when(i == 0)(start)

        @pl.when(i == 0)
        def _():
            gg_ref[...] = jnp.zeros_like(gg_ref)
            if two:
                gg2_ref[...] = jnp.zeros_like(gg2_ref)

        dh = functools.reduce(jnp.add, [_dot(a_refs[q][...], w_ref[q * Kp:(q + 1) * Kp, :], NN) for q in range(na)])
        d1, dg1 = _rms_bwd(x_ref[...], g_ref[...], dh)
        dx = r_ref[...] + d1
        dx_ref[...] = dx
        gg_ref[...] += dg1
        if two:
            d2, dg2 = _rms_bwd(x2_ref[...], g2_ref[...], dx)
            d2_ref[...] = d2.astype(BF16)
            gg2_ref[...] += dg2
        if ng:
            pl.when(i == nt - 1)(finish)

    row = lambda i: (i, 0)
    fix = lambda i: (0, 0)
    in_specs = [pl.BlockSpec((tm, Kp), row)] * na + [
        pl.BlockSpec((na * Kp, D), fix, pipeline_mode=pl.Buffered(1)), pl.BlockSpec((tm, D), row),
        pl.BlockSpec((tm, D), row), pl.BlockSpec((1, D), fix)]
    args = list(a_list) + [w, resid, xin, g]
    out_specs = [pl.BlockSpec((tm, D), row), pl.BlockSpec((1, D), fix)]
    out_shape = [jax.ShapeDtypeStruct((S, D), F32), jax.ShapeDtypeStruct((1, D), F32)]
    if two:
        in_specs += [pl.BlockSpec((tm, D), row), pl.BlockSpec((1, D), fix)]
        args += list(second)
        out_specs += [pl.BlockSpec((tm, D), row), pl.BlockSpec((1, D), fix)]
        out_shape += [jax.ShapeDtypeStruct((S, D), BF16), jax.ShapeDtypeStruct((1, D), F32)]
    n_plain = len(out_shape)
    out = pl.pallas_call(
        body, name=name, grid=(nt,), in_specs=in_specs + [ANY] * ng, out_specs=out_specs + [ANY] * ng,
        out_shape=out_shape + recv_shapes, scratch_shapes=exchange_sems if ng else [],
        compiler_params=_cp("arbitrary"),
    )(*args, *exchange)
    return (*out[:n_plain], out[n_plain:]) if ng else out


def _band_mask(first_block):
    qi = lax.broadcasted_iota(jnp.int32, (BLOCK, 2 * BLOCK), 0)
    ki = lax.broadcasted_iota(jnp.int32, (BLOCK, 2 * BLOCK), 1)
    first_key = jnp.where(first_block, BLOCK, 0)
    return (ki >= qi) & (ki <= qi + BLOCK) & (ki >= first_key)


def _lane_masks():
    lane = lax.broadcasted_iota(jnp.int32, (1, 2 * HEAD_DIM), 1)
    return (lane < HEAD_DIM, lane >= HEAD_DIM)


CHUNK = BLOCK * max(DILATIONS)
SLAB = 2 * HEAD_DIM
N_SLABS = ATTN_WIDTH // SLAB


def _unit_rows(d, b):
    def rows(r):
        start = r + BLOCK * d * b
        return pl.ds(start, BLOCK, stride=d) if d > 1 else pl.ds(start, BLOCK)
    return rows


def _attn_units():
    for p, d in enumerate(DILATIONS):
        nbc = CHUNK // (BLOCK * d)
        for b in range(nbc):
            for r in range(d):
                yield p, d, b, r, nbc


def _attn_in_specs(nc, n_cur):
    prev = lambda c: jnp.maximum(jnp.minimum(c, nc - 1) - 1, 0)
    cur = lambda c: jnp.minimum(c, nc - 1)
    blk = lambda f: pl.BlockSpec((CHUNK, SLAB), f)
    specs = [blk(lambda h, c: (cur(c), h)),
             blk(lambda h, c: (prev(c), N_SLABS + h)), blk(lambda h, c: (cur(c), N_SLABS + h)),
             blk(lambda h, c: (prev(c), 2 * N_SLABS + h)), blk(lambda h, c: (cur(c), 2 * N_SLABS + h))]
    return specs + [blk(lambda h, c: (cur(c), h))] * n_cur


def _attn_fwd(proj, payload, *, name):
    S = proj.shape[0]
    nc = S // CHUNK
    n = len(DILATIONS)
    npay = len(payload)
    n_steps = N_SLABS * nc

    def body(*refs):
        q_ref, kp_ref, kc_ref, vp_ref, vc_ref = refs[:5]
        pay_refs = refs[5:5 + npay]
        attn_ref, lse_ref, attn16_ref = refs[5 + npay:8 + npay]
        all_refs = refs[8 + npay:8 + 2 * npay]
        scr = refs[8 + 2 * npay:]
        o_scr, l_scr = scr[:n], scr[n:2 * n]
        start, forward, finish = _gather_phases(pay_refs, all_refs, *scr[2 * n:])
        step = pl.program_id(0) * nc + pl.program_id(1)
        pl.when(step == 0)(start)
        c = pl.program_id(1)
        lms = _lane_masks()
        plain, first = (jnp.tile(_band_mask(f), (2, 1)) for f in (False, c == 0))
        for p, d, b, r, nbc in _attn_units():
            rows = _unit_rows(d, b)(r)
            prow = _unit_rows(d, (b - 1) % nbc)(r)
            kpr, vpr = (kc_ref, vc_ref) if b > 0 else (kp_ref, vp_ref)
            mask2 = plain if b > 0 else first
            q = q_ref[rows, :].astype(BF16)
            kcat = jnp.concatenate([kpr[prow, :], kc_ref[rows, :]], axis=0).astype(BF16)
            vcat = jnp.concatenate([vpr[prow, :], vc_ref[rows, :]], axis=0).astype(BF16)
            q2 = jnp.concatenate([jnp.where(lm, q, jnp.zeros_like(q)) for lm in lms], axis=0) * SCALE
            s = jnp.where(mask2, _dot(q2, kcat, NT), NEG_INF)
            m = jnp.max(s, axis=-1, keepdims=True)
            e = jnp.exp(s - m)
            l = jnp.sum(e, axis=-1, keepdims=True)
            o2 = _dot(e.astype(BF16), vcat, NN) / l
            lse2 = m + jnp.log(l)
            o_scr[p][rows, :] = jnp.where(lms[0], o2[:BLOCK], o2[BLOCK:])
            l_scr[p][rows, :] = jnp.where(lms[0], lse2[:BLOCK], lse2[BLOCK:])
        ls = [l_scr[p][...] for p in range(n)]
        top = functools.reduce(jnp.maximum, ls)
        es = [jnp.exp(l - top) for l in ls]
        den = functools.reduce(jnp.add, es)
        num = functools.reduce(jnp.add, [e * o_scr[p][...] for p, e in enumerate(es)])
        attn = num / den
        attn_ref[...] = attn
        attn16_ref[...] = attn.astype(BF16)
        lse_ref[...] = top + jnp.log(den)
        pl.when(step == (2 * n_steps) // 3)(forward)
        pl.when(step == n_steps - 1)(finish)

    out = pl.pallas_call(
        body, name=name, grid=(N_SLABS, nc), in_specs=_attn_in_specs(nc, 0) + [ANY] * npay,
        out_specs=[pl.BlockSpec((CHUNK, SLAB), lambda h, c: (c, h))] * 3 + [ANY] * npay,
        out_shape=[jax.ShapeDtypeStruct((S, ATTN_WIDTH), F32)] * 2 + [jax.ShapeDtypeStruct((S, ATTN_WIDTH), BF16)]
        + _gathered_shapes(payload),
        scratch_shapes=[pltpu.VMEM((CHUNK, SLAB), F32)] * (2 * n) + _gather_sems(npay),
        compiler_params=_cp("arbitrary", "arbitrary"),
    )(proj, proj, proj, proj, proj, *payload)
    return (*out[:3], out[3:])


def _attn_bwd(proj, dcat, attn, lse, grads, *, name):
    S = proj.shape[0]
    nc = S // CHUNK
    ng = len(grads)
    n = len(DILATIONS)
    recv_shapes, exchange_sems = _exchange_buffers(grads)

    def body(*refs):
        q_ref, kp_ref, kc_ref, vp_ref, vc_ref, do_ref, o_ref, lse_ref = refs[:8]
        g_refs = refs[8:8 + ng]
        dq_ref, dk_ref, dv_ref = refs[8 + ng:11 + ng]
        r_refs = refs[11 + ng:11 + 2 * ng]
        scr = refs[11 + 2 * ng:]
        dk_prev, dv_prev = scr[:2]
        delta_h, lse_h = scr[2:4], scr[4:6]
        dq_p, dk_own, dk_back, dv_own, dv_back = (scr[6 + n * k:6 + n * (k + 1)] for k in range(5))
        start, finish = _exchange_phases(g_refs, r_refs, *scr[6 + 5 * n:])
        c = pl.program_id(1)
        pl.when((pl.program_id(0) == 0) & (c == 0))(start)

        @pl.when(c == 0)
        def _():
            dk_prev[...] = jnp.zeros_like(dk_prev)
            dv_prev[...] = jnp.zeros_like(dv_prev)

        @pl.when(c < nc)
        def _():
            lms = _lane_masks()
            plain, first = (jnp.tile(_band_mask(f), (2, 1)) for f in (False, c == 0))
            prod = do_ref[...] * o_ref[...]
            lse = lse_ref[...]
            lse_other = pltpu.roll(lse, HEAD_DIM, 1)
            for h, lm in enumerate(lms):
                delta = jnp.sum(jnp.where(lm, prod, 0.0), axis=-1, keepdims=True)
                delta_h[h][...] = jnp.broadcast_to(delta, (CHUNK, SLAB))
                lse_h[h][...] = jnp.where(lm, lse, lse_other)
            wide = lambda refs, rows: jnp.tile(jnp.concatenate([r[rows, :] for r in refs], axis=0), (1, 2))
            stack = lambda f: jnp.concatenate([f(lm) for lm in lms], axis=0)
            for p, d, b, r, nbc in _attn_units():
                rows = _unit_rows(d, b)(r)
                prow = _unit_rows(d, (b - 1) % nbc)(r)
                kpr, vpr = (kc_ref, vc_ref) if b > 0 else (kp_ref, vp_ref)
                mask2 = plain if b > 0 else first
                q = q_ref[rows, :].astype(BF16)
                kcat = jnp.concatenate([kpr[prow, :], kc_ref[rows, :]], axis=0).astype(BF16)
                vcat = jnp.concatenate([vpr[prow, :], vc_ref[rows, :]], axis=0).astype(BF16)
                do = do_ref[rows, :]
                q2 = stack(lambda lm: jnp.where(lm, q, jnp.zeros_like(q))) * SCALE
                do2 = stack(lambda lm: jnp.where(lm, do, 0.0)).astype(BF16)
                e = jnp.where(mask2, jnp.exp(_dot(q2, kcat, NT) - wide(lse_h, rows)), 0.0)
                ds = (e * (_dot(do2, vcat, NT) - wide(delta_h, rows))).astype(BF16)
                dq = jnp.where(lms[0], _dot(ds[:BLOCK], kcat, NN), _dot(ds[BLOCK:], kcat, NN)) * SCALE
                dkc = _dot(ds, q2, TN)
                dvc = _dot(e.astype(BF16), do2, TN)
                dq_p[p][rows, :] = dq
                dk_own[p][rows, :] = dkc[BLOCK:]
                dv_own[p][rows, :] = dvc[BLOCK:]
                dk_back[p][prow, :] = dkc[:BLOCK]
                dv_back[p][prow, :] = dvc[:BLOCK]
            dq_ref[...] = functools.reduce(jnp.add, [r[...] for r in dq_p]).astype(BF16)
            for prev, own, back, out_ref in ((dk_prev, dk_own, dk_back, dk_ref), (dv_prev, dv_own, dv_back, dv_ref)):
                for p, d in enumerate(DILATIONS):
                    tail = CHUNK - BLOCK * d
                    prev[tail:, :] += back[p][tail:, :]
                out_ref[...] = prev[...].astype(BF16)
                prev[...] = functools.reduce(jnp.add, [r[...] for r in own])
                for p, d in enumerate(DILATIONS):
                    tail = CHUNK - BLOCK * d
                    if tail:
                        prev[:tail, :] += back[p][:tail, :]

        @pl.when(c == nc)
        def _():
            dk_ref[...] = dk_prev[...].astype(BF16)
            dv_ref[...] = dv_prev[...].astype(BF16)

        pl.when((pl.program_id(0) == N_SLABS - 1) & (c == nc))(finish)

    blk = lambda f: pl.BlockSpec((CHUNK, SLAB), f)
    late = lambda h, c: (jnp.maximum(c - 1, 0), h)
    out = pl.pallas_call(
        body, name=name, grid=(N_SLABS, nc + 1), in_specs=_attn_in_specs(nc, 3) + [ANY] * ng,
        out_specs=[blk(lambda h, c: (jnp.minimum(c, nc - 1), h)), blk(late), blk(late)] + [ANY] * ng,
        out_shape=[jax.ShapeDtypeStruct((S, ATTN_WIDTH), BF16)] * 3 + recv_shapes,
        scratch_shapes=[pltpu.VMEM((CHUNK, SLAB), F32)] * (6 + 5 * n) + exchange_sems,
        compiler_params=_cp("arbitrary", "arbitrary"),
    )(proj, proj, proj, proj, proj, dcat, attn, lse, *grads)
    return out[:3], out[3:]


def _split_bf16(a):
    hi = a.astype(BF16)
    lo = (a - hi.astype(F32)).astype(BF16)
    return hi, lo


def _pooled(ug, halo_g, w, row0, tm):
    ext = jnp.concatenate([halo_g, ug], axis=0)
    hi, lo = _split_bf16(ext)
    rr = lax.broadcasted_iota(jnp.int32, (tm, tm + HALO), 0)
    cc = lax.broadcasted_iota(jnp.int32, (tm, tm + HALO), 1)
    back = rr + HALO - cc
    win = ((back >= 0) & (back < w)).astype(BF16)
    wsum = _dot(win, hi, NN) + _dot(win, lo, NN)
    rows = row0 + lax.broadcasted_iota(jnp.int32, (tm, 1), 0)
    inv = 1.0 / jnp.minimum(rows + 1, w).astype(F32)
    return wsum * inv - ug


def _pool_fwd(u, u_col, pool_w, pool_scale, *, name, tm=256):
    S, W = u.shape[0], POOL_WIDTH
    G = POOL_GROUP_DIM

    def body(u_ref, h_ref, w_ref, s_ref, o_ref):
        i = pl.program_id(0)
        uv = u_ref[...]
        halo = jnp.where(i > 0, h_ref[...], 0.0)
        for g, w in enumerate(POOL_WINDOWS):
            sl = slice(g * G, (g + 1) * G)
            pooled = _pooled(uv[:, sl], halo[:, sl], w, i * tm, tm)
            z = _dot(pooled.astype(BF16), w_ref[g].astype(BF16), NN)
            o_ref[:, sl] = (z * s_ref[:, sl]).astype(BF16)

    per = tm // HALO
    return pl.pallas_call(
        body, name=name, grid=(S // tm,),
        in_specs=[pl.BlockSpec((tm, W), lambda i: (i, u_col)),
                  pl.BlockSpec((HALO, W), lambda i: (jnp.maximum(i * per - 1, 0), u_col)),
                  pl.BlockSpec((len(POOL_WINDOWS), G, G), lambda i: (0, 0, 0)),
                  pl.BlockSpec((1, W), lambda i: (0, 0))],
        out_specs=pl.BlockSpec((tm, W), lambda i: (i, 0)),
        out_shape=jax.ShapeDtypeStruct((S, W), BF16),
        compiler_params=_cp("parallel"),
    )(u, u, pool_w, pool_scale)


def _pool_bwd(u, u_col, dy, dy_col, pool_w, pool_scale, *, name, tm=256):
    S, W = u.shape[0], POOL_WIDTH
    G = POOL_GROUP_DIM
    nt = S // tm

    def body(u_ref, h_ref, dy_ref, dyn_ref, w_ref, s_ref, du_ref, gw_ref, gs_ref):
        i = pl.program_id(0)

        @pl.when(i == 0)
        def _():
            gw_ref[...] = jnp.zeros_like(gw_ref)
            gs_ref[...] = jnp.zeros_like(gs_ref)

        uv = u_ref[...]
        halo = jnp.where(i > 0, h_ref[...], 0.0)
        dyv = dy_ref[...]
        dyn = jnp.where(i < nt - 1, dyn_ref[...], 0.0)
        rr = lax.broadcasted_iota(jnp.int32, (tm, tm + HALO), 0)
        cc = lax.broadcasted_iota(jnp.int32, (tm, tm + HALO), 1)
        rows_ext = i * tm + lax.broadcasted_iota(jnp.int32, (tm + HALO, 1), 0)
        for g, w in enumerate(POOL_WINDOWS):
            sl = slice(g * G, (g + 1) * G)
            wg = w_ref[g].astype(BF16)
            sc = s_ref[:, sl]
            pooled = _pooled(uv[:, sl], halo[:, sl], w, i * tm, tm)
            z = _dot(pooled.astype(BF16), wg, NN)
            gs_ref[:, sl] += jnp.sum(dyv[:, sl] * z, axis=0, keepdims=True)
            dz = dyv[:, sl] * sc
            gw_ref[g] += _dot(pooled.astype(BF16), dz.astype(BF16), TN)
            dz_ext = jnp.concatenate([dz, dyn[:, sl] * sc], axis=0)
            dp_ext = _dot(dz_ext.astype(BF16), wg, NT)
            inv_ext = 1.0 / jnp.minimum(rows_ext + 1, w).astype(F32)
            hi, lo = _split_bf16(dp_ext * inv_ext)
            ahead = cc - rr
            win = ((ahead >= 0) & (ahead < w)).astype(BF16)
            du_ref[:, sl] = (_dot(win, hi, NN) + _dot(win, lo, NN) - dp_ext[:tm]).astype(BF16)

    per = tm // HALO
    nh = S // HALO
    return pl.pallas_call(
        body, name=name, grid=(nt,),
        in_specs=[pl.BlockSpec((tm, W), lambda i: (i, u_col)),
                  pl.BlockSpec((HALO, W), lambda i: (jnp.maximum(i * per - 1, 0), u_col)),
                  pl.BlockSpec((tm, W), lambda i: (i, dy_col)),
                  pl.BlockSpec((HALO, W), lambda i: (jnp.minimum((i + 1) * per, nh - 1), dy_col)),
                  pl.BlockSpec((len(POOL_WINDOWS), G, G), lambda i: (0, 0, 0)),
                  pl.BlockSpec((1, W), lambda i: (0, 0))],
        out_specs=[pl.BlockSpec((tm, W), lambda i: (i, 0)),
                   pl.BlockSpec((len(POOL_WINDOWS), G, G), lambda i: (0, 0, 0)),
                   pl.BlockSpec((1, W), lambda i: (0, 0))],
        out_shape=[jax.ShapeDtypeStruct((S, W), BF16),
                   jax.ShapeDtypeStruct((len(POOL_WINDOWS), G, G), F32),
                   jax.ShapeDtypeStruct((1, W), F32)],
        compiler_params=_cp("arbitrary"),
    )(u, u, dy, dy, pool_w, pool_scale)


GELU_K0 = math.sqrt(2.0 / math.pi)
GELU_K1 = 0.044715


def _gelu_parts(x):
    x2 = x * x
    t = jnp.tanh(x * (GELU_K0 + (GELU_K0 * GELU_K1) * x2))
    hp = 0.5 + 0.5 * t
    gelu = x * hp
    dgelu = hp + (x * (hp * (1.0 - t))) * (GELU_K0 + (3.0 * GELU_K0 * GELU_K1) * x2)
    return gelu, dgelu


def _shifted(ext, halo):
    return (pltpu.roll(ext, 2, 0)[halo:], pltpu.roll(ext, 1, 0)[halo:], ext[halo:])


def _conv(sh, w, b):
    return b + (sh[0] * w[0:1] + sh[1] * w[1:2] + sh[2] * w[2:3])


F32_ROWS = 8


def _ffn_up_glu(h, w_up_t, conv_w, conv_b, *, name, tm=1024, tn=256, sub=512):
    S, K = h.shape
    F = D_FF
    nj = F // tn

    def body(h_ref, wg_ref, wv_ref, cwg_ref, cwv_ref, cbg_ref, cbv_ref,
             ug_ref, uv_ref, cg_ref, cv_ref, y_ref, carry):
        i = pl.program_id(0)
        j = pl.program_id(1)

        w_cat = jnp.concatenate([wg_ref[...], wv_ref[...]], axis=0)
        conv_w_b = ((cwg_ref[...], cbg_ref[...]), (cwv_ref[...], cbv_ref[...]))
        halo = [jnp.where(i > 0, carry[j, s], 0.0) for s in range(2)]
        for a in range(0, tm, sub):
            u16 = _dot(h_ref[a:a + sub, :], w_cat, NT).astype(BF16)
            ug_ref[a:a + sub, :] = u16[:, :tn]
            uv_ref[a:a + sub, :] = u16[:, tn:]
            c = []
            for s, (cw, cb) in enumerate(conv_w_b):
                u = u16[:, s * tn:(s + 1) * tn].astype(F32)
                ext = jnp.concatenate([halo[s], u], axis=0)
                c.append(_conv(_shifted(ext, F32_ROWS), cw, cb))
                halo[s] = u[sub - F32_ROWS:]
            cg_ref[a:a + sub, :] = c[0].astype(BF16)
            cv_ref[a:a + sub, :] = c[1].astype(BF16)
            gelu, _ = _gelu_parts(c[0])
            y_ref[a:a + sub, :] = (gelu * c[1]).astype(BF16)
        for s in range(2):
            carry[j, s] = halo[s]

    tile = pl.BlockSpec((tm, tn), lambda i, j: (i, j))
    vec = lambda rows, off: pl.BlockSpec((rows, tn), lambda i, j: (0, j + off))
    return pl.pallas_call(
        body, name=name, grid=(S // tm, nj),
        in_specs=[pl.BlockSpec((tm, K), lambda i, j: (i, 0)),
                  pl.BlockSpec((tn, K), lambda i, j: (j, 0)), pl.BlockSpec((tn, K), lambda i, j: (j + nj, 0)),
                  vec(3, 0), vec(3, nj), vec(1, 0), vec(1, nj)],
        out_specs=[tile] * 5,
        out_shape=[jax.ShapeDtypeStruct((S, F), BF16)] * 5,
        scratch_shapes=[pltpu.VMEM((nj, 2, F32_ROWS, tn), F32)],
        compiler_params=_cp("arbitrary", "arbitrary"),
    )(h, w_up_t, w_up_t, conv_w, conv_w, conv_b, conv_b)


def _ffn_glu_bwd(u_g, u_v, c_g, c_v, df, w_down, h, conv_w, *, name, tm=512, tn=256, sub=256):
    S = u_g.shape[0]
    F = D_FF
    D = df.shape[1]
    nj = F // tn
    nt = S // tm

    def body(ug_ref, uv_ref, cg_ref, cgn_ref, cv_ref, cvn_ref, df_ref, dfn_ref, wd_ref, h_ref, wg_ref, wv_ref,
             dug_ref, duv_ref, gug_ref, guv_ref, gd_ref, gwg_ref, gwv_ref, gbg_ref, gbv_ref,
             acc_u, acc_d):
        i = pl.program_id(1)

        @pl.when(i == 0)
        def _():
            for r in (gwg_ref, gwv_ref, gbg_ref, gbv_ref, acc_u, acc_d):
                r[...] = jnp.zeros_like(r)

        wg, wv = wg_ref[...], wv_ref[...]
        wd = wd_ref[...]
        dfn = jnp.where(i < nt - 1, dfn_ref[...], jnp.zeros_like(dfn_ref))
        n_ext = sub + HALO

        def ahead(dc):
            return dc[:sub], pltpu.roll(dc, n_ext - 1, 0)[:sub], pltpu.roll(dc, n_ext - 2, 0)[:sub]

        for a in range(0, tm, sub):
            b = a + sub
            ext = lambda ref, nxt: jnp.concatenate(
                [ref[a:b, :], ref[b:b + HALO, :] if b < tm else nxt], axis=0)
            cg = ext(cg_ref, cgn_ref[...]).astype(F32)
            cv = ext(cv_ref, cvn_ref[...]).astype(F32)
            df_sub = df_ref[a:b, :]
            dy_ext = _dot(ext(df_ref, dfn), wd, NT)
            gelu, dgelu = _gelu_parts(cg)
            dcs_g = ahead(dy_ext * cv * dgelu)
            dcs_v = ahead(dy_ext * gelu)
            du_g = (dcs_g[0] * wg[2:3] + dcs_g[1] * wg[1:2] + dcs_g[2] * wg[0:1]).astype(BF16)
            du_v = (dcs_v[0] * wv[2:3] + dcs_v[1] * wv[1:2] + dcs_v[2] * wv[0:1]).astype(BF16)
            dug_ref[a:b, :] = du_g
            duv_ref[a:b, :] = du_v
            acc_u[...] += _dot(jnp.concatenate([du_g, du_v], axis=1), h_ref[a:b, :], TN)
            acc_d[...] += _dot((gelu[:sub] * cv[:sub]).astype(BF16), df_sub, TN)
            for dcs, u_ref, gw_ref, gb_ref in ((dcs_g, ug_ref, gwg_ref, gbg_ref), (dcs_v, uv_ref, gwv_ref, gbv_ref)):
                u = u_ref[a:b, :].astype(F32)
                gb_ref[...] += jnp.sum(dcs[0], axis=0, keepdims=True)
                for k in range(3):
                    gw_ref[k:k + 1, :] += jnp.sum(dcs[2 - k] * u, axis=0, keepdims=True)

        @pl.when(i == nt - 1)
        def _():
            gug_ref[...] = acc_u[:tn, :].astype(BF16)
            guv_ref[...] = acc_u[tn:, :].astype(BF16)
            gd_ref[...] = acc_d[...].astype(BF16)

    per = tm // HALO
    nh = S // HALO
    hnext = lambda i: jnp.minimum((i + 1) * per, nh - 1)
    tile = pl.BlockSpec((tm, tn), lambda j, i: (i, j))
    hn = pl.BlockSpec((HALO, tn), lambda j, i: (hnext(i), j))
    vec = lambda rows, off: pl.BlockSpec((rows, tn), lambda j, i: (0, j + off))
    wide = pl.BlockSpec((tm, D), lambda j, i: (i, 0))
    wrow = pl.BlockSpec((tn, D), lambda j, i: (j, 0))
    return pl.pallas_call(
        body, name=name, grid=(nj, nt),
        in_specs=[tile, tile, tile, hn, tile, hn, wide, pl.BlockSpec((HALO, D), lambda j, i: (hnext(i), 0)),
                  wrow, wide, vec(3, 0), vec(3, nj)],
        out_specs=[tile, tile, wrow, wrow, wrow, vec(3, 0), vec(3, 0), vec(1, 0), vec(1, 0)],
        out_shape=[jax.ShapeDtypeStruct((S, F), BF16), jax.ShapeDtypeStruct((S, F), BF16),
                   jax.ShapeDtypeStruct((F, D), BF16), jax.ShapeDtypeStruct((F, D), BF16),
                   jax.ShapeDtypeStruct((F, D), BF16),
                   jax.ShapeDtypeStruct((3, F), F32), jax.ShapeDtypeStruct((3, F), F32),
                   jax.ShapeDtypeStruct((1, F), F32), jax.ShapeDtypeStruct((1, F), F32)],
        scratch_shapes=[pltpu.VMEM((2 * tn, D), F32), pltpu.VMEM((tn, D), F32)],
        compiler_params=_cp("parallel", "arbitrary"),
    )(u_g, u_v, c_g, c_g, c_v, c_v, df, df, w_down, h, conv_w, conv_w)


def _sum_partials(parts, *, name, tr):
    _, R, C = parts.shape

    def body(p_ref, o_ref):
        tot = p_ref[0].astype(F32)
        for j in range(1, N_DEV):
            tot = tot + p_ref[j].astype(F32)
        o_ref[...] = tot

    return pl.pallas_call(
        body, name=name, grid=(R // tr,),
        in_specs=[pl.BlockSpec((N_DEV, tr, C), lambda i: (0, i, 0))],
        out_specs=pl.BlockSpec((tr, C), lambda i: (i, 0)),
        out_shape=jax.ShapeDtypeStruct((R, C), F32),
        compiler_params=_cp("parallel"),
    )(parts)


def _adamw(w, g, m, v, *, name, tr):
    R, C = w.shape
    c1 = 1.0 - ADAM_B1 ** ADAM_STEP
    c2 = 1.0 - ADAM_B2 ** ADAM_STEP

    def body(w_ref, g_ref, m_ref, v_ref, d_ref, nm_ref, nv_ref):
        g = g_ref[...]
        nm = ADAM_B1 * m_ref[...] + (1.0 - ADAM_B1) * g
        nv = ADAM_B2 * v_ref[...] + (1.0 - ADAM_B2) * (g * g)
        d_ref[...] = -ADAM_LR * ((nm / c1) / (jnp.sqrt(nv / c2) + ADAM_EPS) + ADAM_WD * w_ref[...])
        nm_ref[...] = nm
        nv_ref[...] = nv

    spec = pl.BlockSpec((tr, C), lambda i: (i, 0))
    return pl.pallas_call(
        body, name=name, grid=(R // tr,), in_specs=[spec] * 4, out_specs=[spec] * 3,
        out_shape=[jax.ShapeDtypeStruct((R, C), F32)] * 3,
        compiler_params=_cp("parallel"),
    )(w, g, m, v)


def _mesh_pos():
    return lax.axis_index("x"), lax.axis_index("y"), lax.axis_index("c")


def _gather_phases(x_refs, out_refs, send_sems, recv_sems, local_sems):
    x, y, c = _mesh_pos()
    me, sibling = (x, y, c), (x, y, 1 - c)
    chips = [(1 - x, y), (x, 1 - y), (1 - x, 1 - y)]
    arrays = range(len(x_refs))

    def slot(a, px, py, pc):
        return out_refs[a].at[4 * px + 2 * py + pc]

    def copy(a, k, block, to, own=False):
        return pltpu.make_async_remote_copy(
            src_ref=x_refs[a] if own else slot(a, *block), dst_ref=slot(a, *block),
            send_sem=send_sems.at[a, k], recv_sem=recv_sems.at[a, k], device_id=to, device_id_type=MESH)

    mine = [pltpu.make_async_copy(x_refs[a], slot(a, *me), local_sems.at[a]) for a in arrays]
    first = [copy(a, 0, me, sibling, own=True) for a in arrays]
    first += [copy(a, 1 + j, me, (*chip, c), own=True) for j, chip in enumerate(chips) for a in arrays]
    passed = [[copy(a, 4 + j, (*chip, c), sibling) for a in arrays] for j, chip in enumerate(chips)]

    def start():
        for cp in mine + first:
            cp.start()

    def forward():
        for j, chip in enumerate(chips):
            for a in arrays:
                copy(a, 1 + j, (*chip, c), me).wait_recv()
                passed[j][a].start()

    def finish():
        for a in arrays:
            copy(a, 0, sibling, me).wait_recv()
            for j, chip in enumerate(chips):
                copy(a, 4 + j, (*chip, 1 - c), me).wait_recv()
        for cp in first + [cp for row in passed for cp in row]:
            cp.wait_send()
        for cp in mine:
            cp.wait()

    return start, forward, finish


def _gather_sems(n):
    return [pltpu.SemaphoreType.DMA((n, 7)), pltpu.SemaphoreType.DMA((n, 7)), pltpu.SemaphoreType.DMA((n,))]


def _gathered_shapes(blocks):
    return [jax.ShapeDtypeStruct((N_DEV,) + b.shape, b.dtype) for b in blocks]


def _all_reduce_small(block, *, name):
    def body(x_ref, all_ref, sum_ref, *sems):
        for phase in _gather_phases([x_ref], [all_ref], *sems):
            phase()
        tot = all_ref[0]
        for j in range(1, N_DEV):
            tot = tot + all_ref[j]
        sum_ref[...] = tot

    return pl.pallas_call(
        body, name=name, in_specs=[VMEM], out_specs=[VMEM, VMEM],
        out_shape=[jax.ShapeDtypeStruct((N_DEV,) + block.shape, block.dtype),
                   jax.ShapeDtypeStruct(block.shape, block.dtype)],
        scratch_shapes=_gather_sems(1),
        compiler_params=pltpu.CompilerParams(vmem_limit_bytes=V7X_VMEM_LIMIT),
    )(block)[1]


def _exchange_phases(g_refs, r_refs, send_sems, recv_sems, local_sems):
    x, y, c = _mesh_pos()
    me = 4 * x + 2 * y + c
    owns, remote = [], []
    for k, (g_ref, r_ref) in enumerate(zip(g_refs, r_refs)):
        rows = g_ref.shape[0] // N_DEV
        owns.append(pltpu.make_async_copy(g_ref.at[pl.ds(me * rows, rows)], r_ref.at[me], local_sems.at[k]))
        for p in range(1, N_DEV):
            px, py, pc = x ^ (p >> 2), y ^ ((p >> 1) & 1), c ^ (p & 1)
            peer = 4 * px + 2 * py + pc
            link = dict(send_sem=send_sems.at[k, p], recv_sem=recv_sems.at[k, p],
                        device_id=(px, py, pc), device_id_type=MESH)
            src = g_ref.at[pl.ds(peer * rows, rows)]
            send = pltpu.make_async_remote_copy(src_ref=src, dst_ref=r_ref.at[me], **link)
            arrival = pltpu.make_async_remote_copy(src_ref=src, dst_ref=r_ref.at[peer], **link)
            remote.append((send, arrival))

    def start():
        for own in owns:
            own.start()
        for send, _ in remote:
            send.start()

    def finish():
        for _, arrival in remote:
            arrival.wait_recv()
        for send, _ in remote:
            send.wait_send()
        for own in owns:
            own.wait()

    return start, finish


def _exchange_buffers(grads):
    n = len(grads)
    shapes = [jax.ShapeDtypeStruct((N_DEV, g.shape[0] // N_DEV, g.shape[1]), g.dtype) for g in grads]
    sems = [pltpu.SemaphoreType.DMA((n, N_DEV)), pltpu.SemaphoreType.DMA((n, N_DEV)),
            pltpu.SemaphoreType.DMA((n,))]
    return shapes, sems


def _unpack_gathered(gathered):
    w_out, w_up_t, w_down = (g.reshape(-1, D_MODEL) for g in gathered[:3])
    width = 2 * D_FF // N_DEV
    conv_w = jnp.transpose(gathered[3][:, :3, :width], (1, 0, 2)).reshape(3, 2 * D_FF)
    return w_out, w_up_t, w_down, conv_w


def _rest_payload(w_out, w_up, w_down, conv_w):
    rows, cols = conv_w.shape
    conv_w = jnp.pad(conv_w, ((0, (-rows) % F32_ROWS), (0, (-cols) % LANES)))
    return [w_out.astype(BF16), w_up.T.astype(BF16), w_down.astype(BF16), conv_w]


def _device_step(x, target, g_mix_pre, w_in_t_block, rest_payload, pool_w, pool_scale, g_mix_post, g_ffn_pre,
                 conv_b, g_ffn_post):
    h1, w_in_t = _rms_norm_gather(x, g_mix_pre, w_in_t_block, name="rms_mix_pre")
    w_in_t = w_in_t.reshape(-1, D_MODEL)
    proj = _matmul(h1, w_in_t, trans_b=True, out_dtype=F32, tm=1024, tn=512, name="proj")
    attn, lse, attn16, gathered = _attn_fwd(proj, rest_payload, name="attn_fwd")
    w_out, w_up_t, w_down, conv_w = _unpack_gathered(gathered)
    pool = _pool_fwd(proj, 3, pool_w, pool_scale, name="pool_fwd")
    mixed, x2, h2 = _mix_out(attn16, pool, w_out, x, g_mix_post, g_ffn_pre, name="mix_out")
    u_g, u_v, c_g, c_v, y = _ffn_up_glu(h2, w_up_t, conv_w, conv_b, name="ffn_up_glu")
    df, d_out, loss_blk, gg_ffn_post = _ffn_out(y, w_down, x2, target, g_ffn_post, name="ffn_out")
    du_g, du_v, gw_up_g, gw_up_v, gw_down, gcw_g, gcw_v, gcb_g, gcb_v = _ffn_glu_bwd(
        u_g, u_v, c_g, c_v, df, w_down, h2, conv_w, name="ffn_glu_bwd")
    gw_up_t = jnp.concatenate([gw_up_g, gw_up_v], axis=0)
    dx2, gg_ffn_pre, dmixed, gg_mix_post = _dgrad_norm(
        [du_g, du_v], w_up_t, d_out, x2, g_ffn_pre, (mixed, g_mix_post), [], name="ffn_up_dgrad")
    gw_out = jnp.concatenate([_matmul_tn(attn16, dmixed, ta=512, ts=1024, name="grad_w_out_attn"),
                              _matmul_tn(pool, dmixed, ta=512, ts=1024, name="grad_w_out_pool")], axis=0)
    dcat = _matmul(dmixed, w_out, trans_b=True, out_dtype=F32, tm=512, tn=1024, name="mix_out_dgrad")
    d_pool_in, g_pool_w, g_pool_scale = _pool_bwd(proj, 3, dcat, 1, pool_w, pool_scale, name="pool_bwd")
    dqkv, (r_out, r_up_t, r_down) = _attn_bwd(proj, dcat, attn, lse, [gw_out, gw_up_t, gw_down], name="attn_bwd")
    dproj = list(dqkv) + [d_pool_in]
    gw_in_t = jnp.concatenate([_matmul_tn(a, h1, ta=512, ts=1024, name=f"grad_w_in_{k}")
                               for k, a in enumerate(dproj)], axis=0)
    grad_x, gg_mix_pre, (r_in_t,) = _dgrad_norm(dproj, w_in_t, dx2, x, g_mix_pre, None, [gw_in_t], name="proj_dgrad")
    g_conv_w = jnp.concatenate([gcw_g, gcw_v], axis=1)
    g_conv_b = jnp.concatenate([gcb_g, gcb_v], axis=1)
    received = (r_in_t, r_out, r_up_t, r_down)
    small = dict(g_mix_pre=gg_mix_pre, g_mix_post=gg_mix_post, g_ffn_pre=gg_ffn_pre, g_ffn_post=gg_ffn_post,
                 pool_scale=g_pool_scale, conv_b=g_conv_b, pool_w=g_pool_w, conv_w=g_conv_w)
    return loss_blk, grad_x, received, small


_SMALL = ("g_mix_pre", "g_mix_post", "g_ffn_pre", "g_ffn_post", "pool_scale", "conv_b", "pool_w")
LANES = 128


def _pack_rows(arrays):
    parts = []
    for a in arrays:
        a2 = a.reshape(-1, LANES)
        parts.append(jnp.pad(a2, ((0, (-a2.shape[0]) % 8), (0, 0))))
    return jnp.concatenate(parts, axis=0)


def _unpack_rows(packed, shapes):
    out, row = [], 0
    for shape in shapes:
        rows = math.prod(shape) // LANES
        out.append(packed[row:row + rows].reshape(shape))
        row += -(-rows // 8) * 8
    return out


def kernel(x, g_mix_pre, w_in, pool_w, pool_scale, w_out, g_mix_post, g_ffn_pre, w_up, conv_w, conv_b, w_down, g_ffn_post, loss_target, m_g_mix_pre, m_w_in, m_pool_w, m_pool_scale, m_w_out, m_g_mix_post, m_g_ffn_pre, m_w_up, m_conv_w, m_conv_b, m_w_down, m_g_ffn_post, v_g_mix_pre, v_w_in, v_pool_w, v_pool_scale, v_w_out, v_g_mix_post, v_g_ffn_pre, v_w_up, v_conv_w, v_conv_b, v_w_down, v_g_ffn_post):
    me = 4 * lax.axis_index("x") + 2 * lax.axis_index("y") + lax.axis_index("c")
    loss_blk, grad_x, recv, small = _device_step(
        x[0], loss_target[0], g_mix_pre, w_in[0].T.astype(BF16),
        _rest_payload(w_out[0], w_up[0], w_down[0], conv_w[0]),
        pool_w[0], pool_scale, g_mix_post, g_ffn_pre, conv_b, g_ffn_post)

    g_in_t, g_out, g_up_t, g_down = (
        _sum_partials(r, name=f"sum_partials_{k}", tr=r.shape[1] // 2) for k, r in enumerate(recv))
    grads = {"w_in": g_in_t.T, "w_out": g_out, "w_up": g_up_t.T, "w_down": g_down}

    given = dict(g_mix_pre=g_mix_pre, g_mix_post=g_mix_post, g_ffn_pre=g_ffn_pre, g_ffn_post=g_ffn_post,
                 pool_scale=pool_scale, conv_b=conv_b, pool_w=pool_w)
    small_shapes = [given[k].shape for k in _SMALL]
    total = _all_reduce_small(_pack_rows([small[k] for k in _SMALL] + [small["conv_w"], loss_blk]),
                              name="all_reduce_small")
    *small_grads, g_conv_w_all, loss_all = _unpack_rows(total, small_shapes + [(3, 2 * D_FF), loss_blk.shape])
    loss = loss_all[0, 0]
    grads.update(zip(_SMALL, small_grads))
    width = 2 * D_FF // N_DEV
    grads["conv_w"] = lax.dynamic_slice_in_dim(g_conv_w_all, me * width, width, axis=1)[None]

    weights = dict(g_mix_pre=g_mix_pre, w_in=w_in, pool_w=pool_w, pool_scale=pool_scale, w_out=w_out,
                   g_mix_post=g_mix_post, g_ffn_pre=g_ffn_pre, w_up=w_up, conv_w=conv_w, conv_b=conv_b,
                   w_down=w_down, g_ffn_post=g_ffn_post)
    m_in = dict(g_mix_pre=m_g_mix_pre, w_in=m_w_in, pool_w=m_pool_w, pool_scale=m_pool_scale, w_out=m_w_out,
                g_mix_post=m_g_mix_post, g_ffn_pre=m_g_ffn_pre, w_up=m_w_up, conv_w=m_conv_w, conv_b=m_conv_b,
                w_down=m_w_down, g_ffn_post=m_g_ffn_post)
    v_in = dict(g_mix_pre=v_g_mix_pre, w_in=v_w_in, pool_w=v_pool_w, pool_scale=v_pool_scale, w_out=v_w_out,
                g_mix_post=v_g_mix_post, g_ffn_pre=v_g_ffn_pre, w_up=v_w_up, conv_w=v_conv_w, conv_b=v_conv_b,
                w_down=v_w_down, g_ffn_post=v_g_ffn_post)
    delta, new_m, new_v = {}, {}, {}
    for k in ("w_in", "w_out", "w_up", "w_down"):
        g = grads[k]
        d, nm, nv = _adamw(weights[k][0], g, m_in[k][0], v_in[k][0], name=f"adamw_{k}", tr=g.shape[0] // 2)
        grads[k], delta[k], new_m[k], new_v[k] = g[None], d[None], nm[None], nv[None]
    d, nm, nv = _adamw(weights["conv_w"][0], grads["conv_w"][0], m_in["conv_w"][0], v_in["conv_w"][0],
                       name="adamw_conv_w", tr=3)
    delta["conv_w"], new_m["conv_w"], new_v["conv_w"] = d[None], nm[None], nv[None]
    packed_w = _pack_rows([weights[k] for k in _SMALL])
    small_rows = packed_w.shape[0]
    d, nm, nv = _adamw(packed_w, total[:small_rows], _pack_rows([m_in[k] for k in _SMALL]),
                       _pack_rows([v_in[k] for k in _SMALL]), name="adamw_small", tr=small_rows)
    for k, dk, mk, vk in zip(_SMALL, _unpack_rows(d, small_shapes), _unpack_rows(nm, small_shapes),
                             _unpack_rows(nv, small_shapes)):
        delta[k], new_m[k], new_v[k] = dk, mk, vk

    order = ("g_mix_pre", "w_in", "pool_w", "pool_scale", "w_out", "g_mix_post", "g_ffn_pre", "w_up",
             "conv_w", "conv_b", "w_down", "g_ffn_post")
    return (loss, grad_x[None], *[grads[k] for k in order], *[delta[k] for k in order],
            *[new_m[k] for k in order], *[new_v[k] for k in order])
```

```python
import functools
import math

import jax
import jax.numpy as jnp
from jax import lax
from jax.experimental import pallas as pl
from jax.experimental.pallas import tpu as pltpu

F32 = jnp.float32
BF16 = jnp.bfloat16

D_MODEL = 1024
ATTN_WIDTH = 512
N_HEADS = 8
HEAD_DIM = 64
DILATIONS = (1, 4, 16)
BLOCK = 128
POOL_WIDTH = 512
POOL_WINDOWS = (2, 4, 8, 16)
POOL_GROUP_DIM = 128
D_FF = 2816
EPS = 1e-6
NEG_INF = -1e30
SCALE = HEAD_DIM ** -0.5

ADAM_LR = 0.001
ADAM_B1 = 0.9
ADAM_B2 = 0.999
ADAM_EPS = 1e-08
ADAM_WD = 0.01
ADAM_STEP = 10

N_DEV = 8
HALO = 16
V7X_VMEM_LIMIT = 56 * 1024 * 1024

MESH = pl.DeviceIdType.MESH
ANY = pl.BlockSpec(memory_space=pl.ANY)
VMEM = pl.BlockSpec(memory_space=pltpu.VMEM)

NT = (((1,), (1,)), ((), ()))
NN = (((1,), (0,)), ((), ()))
TN = (((0,), (0,)), ((), ()))


def _cp(*sem):
    return pltpu.CompilerParams(dimension_semantics=sem, vmem_limit_bytes=V7X_VMEM_LIMIT)


def _dot(a, b, dn):
    return lax.dot_general(a, b, dn, preferred_element_type=F32)


def _rms_bwd(xin, g, dy):
    r = lax.rsqrt(jnp.mean(xin * xin, axis=-1, keepdims=True) + EPS)
    xh = xin * r
    gdy = g * dy
    dx = r * (gdy - xh * jnp.mean(gdy * xh, axis=-1, keepdims=True))
    dg = jnp.sum(dy * xh, axis=0, keepdims=True)
    return dx, dg


def _rms_norm_gather(x, g, block, *, name, tm=512):
    S, D = x.shape
    nt = S // tm

    def body(x_ref, g_ref, blk_ref, o_ref, all_ref, *sems):
        i = pl.program_id(0)
        start, forward, finish = _gather_phases([blk_ref], [all_ref], *sems)
        pl.when(i == 0)(start)
        xv = x_ref[...]
        r = lax.rsqrt(jnp.mean(xv * xv, axis=-1, keepdims=True) + EPS)
        o_ref[...] = (xv * r * g_ref[...]).astype(BF16)
        pl.when(i == (2 * nt) // 3)(forward)
        pl.when(i == nt - 1)(finish)

    return pl.pallas_call(
        body, name=name, grid=(nt,),
        in_specs=[pl.BlockSpec((tm, D), lambda i: (i, 0)), pl.BlockSpec((1, D), lambda i: (0, 0)), ANY],
        out_specs=[pl.BlockSpec((tm, D), lambda i: (i, 0)), ANY],
        out_shape=[jax.ShapeDtypeStruct((S, D), BF16)] + _gathered_shapes([block]),
        scratch_shapes=_gather_sems(1),
        compiler_params=_cp("arbitrary"),
    )(x, g, block)


def _matmul(a, b, *, trans_b, out_dtype, tm, tn, name):
    M, K = a.shape
    N = b.shape[0] if trans_b else b.shape[1]
    dn = NT if trans_b else NN

    def body(a_ref, b_ref, o_ref):
        o_ref[...] = _dot(a_ref[...], b_ref[...], dn).astype(out_dtype)

    b_spec = (pl.BlockSpec((tn, K), lambda i, j: (j, 0)) if trans_b
              else pl.BlockSpec((K, tn), lambda i, j: (0, j)))
    return pl.pallas_call(
        body, name=name, grid=(M // tm, N // tn),
        in_specs=[pl.BlockSpec((tm, K), lambda i, j: (i, 0)), b_spec],
        out_specs=pl.BlockSpec((tm, tn), lambda i, j: (i, j)),
        out_shape=jax.ShapeDtypeStruct((M, N), out_dtype),
        compiler_params=_cp("parallel", "parallel"),
    )(a, b)


def _matmul_tn(a, b, *, ta, ts, name):
    S, Ka = a.shape
    Nb = b.shape[1]
    ns = S // ts

    def body(a_ref, b_ref, o_ref, acc):
        s = pl.program_id(1)

        @pl.when(s == 0)
        def _():
            acc[...] = jnp.zeros_like(acc)

        acc[...] += _dot(a_ref[...], b_ref[...], TN)

        @pl.when(s == ns - 1)
        def _():
            o_ref[...] = acc[...].astype(BF16)

    return pl.pallas_call(
        body, name=name, grid=(Ka // ta, ns),
        in_specs=[pl.BlockSpec((ts, ta), lambda i, s: (s, i)), pl.BlockSpec((ts, Nb), lambda i, s: (s, 0))],
        out_specs=pl.BlockSpec((ta, Nb), lambda i, s: (i, 0)),
        out_shape=jax.ShapeDtypeStruct((Ka, Nb), BF16),
        scratch_shapes=[pltpu.VMEM((ta, Nb), F32)],
        compiler_params=_cp("parallel", "arbitrary"),
    )(a, b)


def _mix_out(attn, pool, w_out, x, g_post, g_next, *, name, tm=256):
    S, K = attn.shape
    D = w_out.shape[1]

    def body(a_ref, p_ref, w_ref, x_ref, gp_ref, gn_ref, mixed_ref, x2_ref, h2_ref):
        mixed = _dot(a_ref[...], w_ref[:K, :], NN) + _dot(p_ref[...], w_ref[K:, :], NN)
        r = lax.rsqrt(jnp.mean(mixed * mixed, axis=-1, keepdims=True) + EPS)
        x2 = x_ref[...] + mixed * r * gp_ref[...]
        r2 = lax.rsqrt(jnp.mean(x2 * x2, axis=-1, keepdims=True) + EPS)
        mixed_ref[...] = mixed
        x2_ref[...] = x2
        h2_ref[...] = (x2 * r2 * gn_ref[...]).astype(BF16)

    row = lambda i: (i, 0)
    fix = lambda i: (0, 0)
    return pl.pallas_call(
        body, name=name, grid=(S // tm,),
        in_specs=[pl.BlockSpec((tm, K), row), pl.BlockSpec((tm, K), row), pl.BlockSpec((2 * K, D), fix),
                  pl.BlockSpec((tm, D), row), pl.BlockSpec((1, D), fix), pl.BlockSpec((1, D), fix)],
        out_specs=[pl.BlockSpec((tm, D), row)] * 3,
        out_shape=[jax.ShapeDtypeStruct((S, D), F32), jax.ShapeDtypeStruct((S, D), F32),
                   jax.ShapeDtypeStruct((S, D), BF16)],
        compiler_params=_cp("parallel"),
    )(attn, pool, w_out, x, g_post, g_next)


def _ffn_out(y, w_down, x2, target, g_post, *, name, tm=512):
    S, K = y.shape
    D = w_down.shape[1]

    def body(y_ref, w_ref, x2_ref, t_ref, g_ref, df_ref, dout_ref, loss_ref, gg_ref):
        i = pl.program_id(0)

        @pl.when(i == 0)
        def _():
            loss_ref[...] = jnp.zeros_like(loss_ref)
            gg_ref[...] = jnp.zeros_like(gg_ref)

        f = _dot(y_ref[...], w_ref[...], NN)
        g = g_ref[...]
        r = lax.rsqrt(jnp.mean(f * f, axis=-1, keepdims=True) + EPS)
        out = x2_ref[...] + f * r * g
        err = out - t_ref[...]
        dy = err * (1.0 / D)
        df, dg = _rms_bwd(f, g, dy)
        df_ref[...] = df.astype(BF16)
        dout_ref[...] = dy
        gg_ref[...] += dg
        loss_ref[...] += 0.5 * jnp.sum(jnp.mean(err * err, axis=-1, keepdims=True))

    row = lambda i: (i, 0)
    fix = lambda i: (0, 0)
    return pl.pallas_call(
        body, name=name, grid=(S // tm,),
        in_specs=[pl.BlockSpec((tm, K), row), pl.BlockSpec((K, D), fix), pl.BlockSpec((tm, D), row),
                  pl.BlockSpec((tm, D), row), pl.BlockSpec((1, D), fix)],
        out_specs=[pl.BlockSpec((tm, D), row), pl.BlockSpec((tm, D), row),
                   pl.BlockSpec((8, 128), fix), pl.BlockSpec((1, D), fix)],
        out_shape=[jax.ShapeDtypeStruct((S, D), BF16), jax.ShapeDtypeStruct((S, D), F32),
                   jax.ShapeDtypeStruct((8, 128), F32), jax.ShapeDtypeStruct((1, D), F32)],
        compiler_params=_cp("arbitrary"),
    )(y, w_down, x2, target, g_post)


def _dgrad_norm(a_list, w, resid, xin, g, second, exchange, *, name, tm=512):
    S, Kp = a_list[0].shape
    na = len(a_list)
    D = w.shape[1]
    nt = S // tm
    two = second is not None
    ng = len(exchange)
    recv_shapes, exchange_sems = _exchange_buffers(exchange)

    def body(*refs):
        a_refs = refs[:na]
        w_ref, r_ref, x_ref, g_ref = refs[na:na + 4]
        pos = na + 4
        if two:
            x2_ref, g2_ref = refs[pos:pos + 2]
            pos += 2
        g_refs = refs[pos:pos + ng]
        pos += ng
        dx_ref, gg_ref = refs[pos:pos + 2]
        pos += 2
        if two:
            d2_ref, gg2_ref = refs[pos:pos + 2]
            pos += 2
        r_refs = refs[pos:pos + ng]
        pos += ng
        i = pl.program_id(0)
        if ng:
            start, finish = _exchange_phases(g_refs, r_refs, *refs[pos:])
            pl.when(i == 0)(start)

        @pl.when(i == 0)
        def _():
            gg_ref[...] = jnp.zeros_like(gg_ref)
            if two:
                gg2_ref[...] = jnp.zeros_like(gg2_ref)

        dh = functools.reduce(jnp.add, [_dot(a_refs[q][...], w_ref[q * Kp:(q + 1) * Kp, :], NN) for q in range(na)])
        d1, dg1 = _rms_bwd(x_ref[...], g_ref[...], dh)
        dx = r_ref[...] + d1
        dx_ref[...] = dx
        gg_ref[...] += dg1
        if two:
            d2, dg2 = _rms_bwd(x2_ref[...], g2_ref[...], dx)
            d2_ref[...] = d2.astype(BF16)
            gg2_ref[...] += dg2
        if ng:
            pl.when(i == nt - 1)(finish)

    row = lambda i: (i, 0)
    fix = lambda i: (0, 0)
    in_specs = [pl.BlockSpec((tm, Kp), row)] * na + [
        pl.BlockSpec((na * Kp, D), fix, pipeline_mode=pl.Buffered(1)), pl.BlockSpec((tm, D), row),
        pl.BlockSpec((tm, D), row), pl.BlockSpec((1, D), fix)]
    args = list(a_list) + [w, resid, xin, g]
    out_specs = [pl.BlockSpec((tm, D), row), pl.BlockSpec((1, D), fix)]
    out_shape = [jax.ShapeDtypeStruct((S, D), F32), jax.ShapeDtypeStruct((1, D), F32)]
    if two:
        in_specs += [pl.BlockSpec((tm, D), row), pl.BlockSpec((1, D), fix)]
        args += list(second)
        out_specs += [pl.BlockSpec((tm, D), row), pl.BlockSpec((1, D), fix)]
        out_shape += [jax.ShapeDtypeStruct((S, D), BF16), jax.ShapeDtypeStruct((1, D), F32)]
    n_plain = len(out_shape)
    out = pl.pallas_call(
        body, name=name, grid=(nt,), in_specs=in_specs + [ANY] * ng, out_specs=out_specs + [ANY] * ng,
        out_shape=out_shape + recv_shapes, scratch_shapes=exchange_sems if ng else [],
        compiler_params=_cp("arbitrary"),
    )(*args, *exchange)
    return (*out[:n_plain], out[n_plain:]) if ng else out


def _band_mask(first_block):
    qi = lax.broadcasted_iota(jnp.int32, (BLOCK, 2 * BLOCK), 0)
    ki = lax.broadcasted_iota(jnp.int32, (BLOCK, 2 * BLOCK), 1)
    first_key = jnp.where(first_block, BLOCK, 0)
    return (ki >= qi) & (ki <= qi + BLOCK) & (ki >= first_key)


def _lane_masks():
    lane = lax.broadcasted_iota(jnp.int32, (1, 2 * HEAD_DIM), 1)
    return (lane < HEAD_DIM, lane >= HEAD_DIM)


CHUNK = BLOCK * max(DILATIONS)
SLAB = 2 * HEAD_DIM
N_SLABS = ATTN_WIDTH // SLAB


def _unit_rows(d, b):
    def rows(r):
        start = r + BLOCK * d * b
        return pl.ds(start, BLOCK, stride=d) if d > 1 else pl.ds(start, BLOCK)
    return rows


def _attn_units():
    for p, d in enumerate(DILATIONS):
        nbc = CHUNK // (BLOCK * d)
        for b in range(nbc):
            for r in range(d):
                yield p, d, b, r, nbc


def _attn_in_specs(nc, n_cur):
    prev = lambda c: jnp.maximum(jnp.minimum(c, nc - 1) - 1, 0)
    cur = lambda c: jnp.minimum(c, nc - 1)
    blk = lambda f: pl.BlockSpec((CHUNK, SLAB), f)
    specs = [blk(lambda h, c: (cur(c), h)),
             blk(lambda h, c: (prev(c), N_SLABS + h)), blk(lambda h, c: (cur(c), N_SLABS + h)),
             blk(lambda h, c: (prev(c), 2 * N_SLABS + h)), blk(lambda h, c: (cur(c), 2 * N_SLABS + h))]
    return specs + [blk(lambda h, c: (cur(c), h))] * n_cur


def _attn_fwd(proj, payload, *, name):
    S = proj.shape[0]
    nc = S // CHUNK
    n = len(DILATIONS)
    npay = len(payload)
    n_steps = N_SLABS * nc

    def body(*refs):
        q_ref, kp_ref, kc_ref, vp_ref, vc_ref = refs[:5]
        pay_refs = refs[5:5 + npay]
        attn_ref, lse_ref, attn16_ref = refs[5 + npay:8 + npay]
        all_refs = refs[8 + npay:8 + 2 * npay]
        scr = refs[8 + 2 * npay:]
        o_scr, l_scr = scr[:n], scr[n:2 * n]
        start, forward, finish = _gather_phases(pay_refs, all_refs, *scr[2 * n:])
        step = pl.program_id(0) * nc + pl.program_id(1)
        pl.when(step == 0)(start)
        c = pl.program_id(1)
        lms = _lane_masks()
        plain, first = (jnp.tile(_band_mask(f), (2, 1)) for f in (False, c == 0))
        for p, d, b, r, nbc in _attn_units():
            rows = _unit_rows(d, b)(r)
            prow = _unit_rows(d, (b - 1) % nbc)(r)
            kpr, vpr = (kc_ref, vc_ref) if b > 0 else (kp_ref, vp_ref)
            mask2 = plain if b > 0 else first
            q = q_ref[rows, :].astype(BF16)
            kcat = jnp.concatenate([kpr[prow, :], kc_ref[rows, :]], axis=0).astype(BF16)
            vcat = jnp.concatenate([vpr[prow, :], vc_ref[rows, :]], axis=0).astype(BF16)
            q2 = jnp.concatenate([jnp.where(lm, q, jnp.zeros_like(q)) for lm in lms], axis=0) * SCALE
            s = jnp.where(mask2, _dot(q2, kcat, NT), NEG_INF)
            m = jnp.max(s, axis=-1, keepdims=True)
            e = jnp.exp(s - m)
            l = jnp.sum(e, axis=-1, keepdims=True)
            o2 = _dot(e.astype(BF16), vcat, NN) / l
            lse2 = m + jnp.log(l)
            o_scr[p][rows, :] = jnp.where(lms[0], o2[:BLOCK], o2[BLOCK:])
            l_scr[p][rows, :] = jnp.where(lms[0], lse2[:BLOCK], lse2[BLOCK:])
        ls = [l_scr[p][...] for p in range(n)]
        top = functools.reduce(jnp.maximum, ls)
        es = [jnp.exp(l - top) for l in ls]
        den = functools.reduce(jnp.add, es)
        num = functools.reduce(jnp.add, [e * o_scr[p][...] for p, e in enumerate(es)])
        attn = num / den
        attn_ref[...] = attn
        attn16_ref[...] = attn.astype(BF16)
        lse_ref[...] = top + jnp.log(den)
        pl.when(step == (2 * n_steps) // 3)(forward)
        pl.when(step == n_steps - 1)(finish)

    out = pl.pallas_call(
        body, name=name, grid=(N_SLABS, nc), in_specs=_attn_in_specs(nc, 0) + [ANY] * npay,
        out_specs=[pl.BlockSpec((CHUNK, SLAB), lambda h, c: (c, h))] * 3 + [ANY] * npay,
        out_shape=[jax.ShapeDtypeStruct((S, ATTN_WIDTH), F32)] * 2 + [jax.ShapeDtypeStruct((S, ATTN_WIDTH), BF16)]
        + _gathered_shapes(payload),
        scratch_shapes=[pltpu.VMEM((CHUNK, SLAB), F32)] * (2 * n) + _gather_sems(npay),
        compiler_params=_cp("arbitrary", "arbitrary"),
    )(proj, proj, proj, proj, proj, *payload)
    return (*out[:3], out[3:])


def _attn_bwd(proj, dcat, attn, lse, grads, *, name):
    S = proj.shape[0]
    nc = S // CHUNK
    ng = len(grads)
    n = len(DILATIONS)
    recv_shapes, exchange_sems = _exchange_buffers(grads)

    def body(*refs):
        q_ref, kp_ref, kc_ref, vp_ref, vc_ref, do_ref, o_ref, lse_ref = refs[:8]
        g_refs = refs[8:8 + ng]
        dq_ref, dk_ref, dv_ref = refs[8 + ng:11 + ng]
        r_refs = refs[11 + ng:11 + 2 * ng]
        scr = refs[11 + 2 * ng:]
        dk_prev, dv_prev = scr[:2]
        delta_h, lse_h = scr[2:4], scr[4:6]
        dq_p, dk_own, dk_back, dv_own, dv_back = (scr[6 + n * k:6 + n * (k + 1)] for k in range(5))
        start, finish = _exchange_phases(g_refs, r_refs, *scr[6 + 5 * n:])
        c = pl.program_id(1)
        pl.when((pl.program_id(0) == 0) & (c == 0))(start)

        @pl.when(c == 0)
        def _():
            dk_prev[...] = jnp.zeros_like(dk_prev)
            dv_prev[...] = jnp.zeros_like(dv_prev)

        @pl.when(c < nc)
        def _():
            lms = _lane_masks()
            plain, first = (jnp.tile(_band_mask(f), (2, 1)) for f in (False, c == 0))
            prod = do_ref[...] * o_ref[...]
            lse = lse_ref[...]
            lse_other = pltpu.roll(lse, HEAD_DIM, 1)
            for h, lm in enumerate(lms):
                delta = jnp.sum(jnp.where(lm, prod, 0.0), axis=-1, keepdims=True)
                delta_h[h][...] = jnp.broadcast_to(delta, (CHUNK, SLAB))
                lse_h[h][...] = jnp.where(lm, lse, lse_other)
            wide = lambda refs, rows: jnp.tile(jnp.concatenate([r[rows, :] for r in refs], axis=0), (1, 2))
            stack = lambda f: jnp.concatenate([f(lm) for lm in lms], axis=0)
            for p, d, b, r, nbc in _attn_units():
                rows = _unit_rows(d, b)(r)
                prow = _unit_rows(d, (b - 1) % nbc)(r)
                kpr, vpr = (kc_ref, vc_ref) if b > 0 else (kp_ref, vp_ref)
                mask2 = plain if b > 0 else first
                q = q_ref[rows, :].astype(BF16)
                kcat = jnp.concatenate([kpr[prow, :], kc_ref[rows, :]], axis=0).astype(BF16)
                vcat = jnp.concatenate([vpr[prow, :], vc_ref[rows, :]], axis=0).astype(BF16)
                do = do_ref[rows, :]
                q2 = stack(lambda lm: jnp.where(lm, q, jnp.zeros_like(q))) * SCALE
                do2 = stack(lambda lm: jnp.where(lm, do, 0.0)).astype(BF16)
                e = jnp.where(mask2, jnp.exp(_dot(q2, kcat, NT) - wide(lse_h, rows)), 0.0)
                ds = (e * (_dot(do2, vcat, NT) - wide(delta_h, rows))).astype(BF16)
                dq = jnp.where(lms[0], _dot(ds[:BLOCK], kcat, NN), _dot(ds[BLOCK:], kcat, NN)) * SCALE
                dkc = _dot(ds, q2, TN)
                dvc = _dot(e.astype(BF16), do2, TN)
                dq_p[p][rows, :] = dq
                dk_own[p][rows, :] = dkc[BLOCK:]
                dv_own[p][rows, :] = dvc[BLOCK:]
                dk_back[p][prow, :] = dkc[:BLOCK]
                dv_back[p][prow, :] = dvc[:BLOCK]
            dq_ref[...] = functools.reduce(jnp.add, [r[...] for r in dq_p]).astype(BF16)
            for prev, own, back, out_ref in ((dk_prev, dk_own, dk_back, dk_ref), (dv_prev, dv_own, dv_back, dv_ref)):
                for p, d in enumerate(DILATIONS):
                    tail = CHUNK - BLOCK * d
                    prev[tail:, :] += back[p][tail:, :]
                out_ref[...] = prev[...].astype(BF16)
                prev[...] = functools.reduce(jnp.add, [r[...] for r in own])
                for p, d in enumerate(DILATIONS):
                    tail = CHUNK - BLOCK * d
                    if tail:
                        prev[:tail, :] += back[p][:tail, :]

        @pl.when(c == nc)
        def _():
            dk_ref[...] = dk_prev[...].astype(BF16)
            dv_ref[...] = dv_prev[...].astype(BF16)

        pl.when((pl.program_id(0) == N_SLABS - 1) & (c == nc))(finish)

    blk = lambda f: pl.BlockSpec((CHUNK, SLAB), f)
    late = lambda h, c: (jnp.maximum(c - 1, 0), h)
    out = pl.pallas_call(
        body, name=name, grid=(N_SLABS, nc + 1), in_specs=_attn_in_specs(nc, 3) + [ANY] * ng,
        out_specs=[blk(lambda h, c: (jnp.minimum(c, nc - 1), h)), blk(late), blk(late)] + [ANY] * ng,
        out_shape=[jax.ShapeDtypeStruct((S, ATTN_WIDTH), BF16)] * 3 + recv_shapes,
        scratch_shapes=[pltpu.VMEM((CHUNK, SLAB), F32)] * (6 + 5 * n) + exchange_sems,
        compiler_params=_cp("arbitrary", "arbitrary"),
    )(proj, proj, proj, proj, proj, dcat, attn, lse, *grads)
    return out[:3], out[3:]


def _split_bf16(a):
    hi = a.astype(BF16)
    lo = (a - hi.astype(F32)).astype(BF16)
    return hi, lo


def _pooled(ug, halo_g, w, row0, tm):
    ext = jnp.concatenate([halo_g, ug], axis=0)
    hi, lo = _split_bf16(ext)
    rr = lax.broadcasted_iota(jnp.int32, (tm, tm + HALO), 0)
    cc = lax.broadcasted_iota(jnp.int32, (tm, tm + HALO), 1)
    back = rr + HALO - cc
    win = ((back >= 0) & (back < w)).astype(BF16)
    wsum = _dot(win, hi, NN) + _dot(win, lo, NN)
    rows = row0 + lax.broadcasted_iota(jnp.int32, (tm, 1), 0)
    inv = 1.0 / jnp.minimum(rows + 1, w).astype(F32)
    return wsum * inv - ug


def _pool_fwd(u, u_col, pool_w, pool_scale, *, name, tm=256):
    S, W = u.shape[0], POOL_WIDTH
    G = POOL_GROUP_DIM

    def body(u_ref, h_ref, w_ref, s_ref, o_ref):
        i = pl.program_id(0)
        uv = u_ref[...]
        halo = jnp.where(i > 0, h_ref[...], 0.0)
        for g, w in enumerate(POOL_WINDOWS):
            sl = slice(g * G, (g + 1) * G)
            pooled = _pooled(uv[:, sl], halo[:, sl], w, i * tm, tm)
            z = _dot(pooled.astype(BF16), w_ref[g].astype(BF16), NN)
            o_ref[:, sl] = (z * s_ref[:, sl]).astype(BF16)

    per = tm // HALO
    return pl.pallas_call(
        body, name=name, grid=(S // tm,),
        in_specs=[pl.BlockSpec((tm, W), lambda i: (i, u_col)),
                  pl.BlockSpec((HALO, W), lambda i: (jnp.maximum(i * per - 1, 0), u_col)),
                  pl.BlockSpec((len(POOL_WINDOWS), G, G), lambda i: (0, 0, 0)),
                  pl.BlockSpec((1, W), lambda i: (0, 0))],
        out_specs=pl.BlockSpec((tm, W), lambda i: (i, 0)),
        out_shape=jax.ShapeDtypeStruct((S, W), BF16),
        compiler_params=_cp("parallel"),
    )(u, u, pool_w, pool_scale)


def _pool_bwd(u, u_col, dy, dy_col, pool_w, pool_scale, *, name, tm=256):
    S, W = u.shape[0], POOL_WIDTH
    G = POOL_GROUP_DIM
    nt = S // tm

    def body(u_ref, h_ref, dy_ref, dyn_ref, w_ref, s_ref, du_ref, gw_ref, gs_ref):
        i = pl.program_id(0)

        @pl.when(i == 0)
        def _():
            gw_ref[...] = jnp.zeros_like(gw_ref)
            gs_ref[...] = jnp.zeros_like(gs_ref)

        uv = u_ref[...]
        halo = jnp.where(i > 0, h_ref[...], 0.0)
        dyv = dy_ref[...]
        dyn = jnp.where(i < nt - 1, dyn_ref[...], 0.0)
        rr = lax.broadcasted_iota(jnp.int32, (tm, tm + HALO), 0)
        cc = lax.broadcasted_iota(jnp.int32, (tm, tm + HALO), 1)
        rows_ext = i * tm + lax.broadcasted_iota(jnp.int32, (tm + HALO, 1), 0)
        for g, w in enumerate(POOL_WINDOWS):
            sl = slice(g * G, (g + 1) * G)
            wg = w_ref[g].astype(BF16)
            sc = s_ref[:, sl]
            pooled = _pooled(uv[:, sl], halo[:, sl], w, i * tm, tm)
            z = _dot(pooled.astype(BF16), wg, NN)
            gs_ref[:, sl] += jnp.sum(dyv[:, sl] * z, axis=0, keepdims=True)
            dz = dyv[:, sl] * sc
            gw_ref[g] += _dot(pooled.astype(BF16), dz.astype(BF16), TN)
            dz_ext = jnp.concatenate([dz, dyn[:, sl] * sc], axis=0)
            dp_ext = _dot(dz_ext.astype(BF16), wg, NT)
            inv_ext = 1.0 / jnp.minimum(rows_ext + 1, w).astype(F32)
            hi, lo = _split_bf16(dp_ext * inv_ext)
            ahead = cc - rr
            win = ((ahead >= 0) & (ahead < w)).astype(BF16)
            du_ref[:, sl] = (_dot(win, hi, NN) + _dot(win, lo, NN) - dp_ext[:tm]).astype(BF16)

    per = tm // HALO
    nh = S // HALO
    return pl.pallas_call(
        body, name=name, grid=(nt,),
        in_specs=[pl.BlockSpec((tm, W), lambda i: (i, u_col)),
                  pl.BlockSpec((HALO, W), lambda i: (jnp.maximum(i * per - 1, 0), u_col)),
                  pl.BlockSpec((tm, W), lambda i: (i, dy_col)),
                  pl.BlockSpec((HALO, W), lambda i: (jnp.minimum((i + 1) * per, nh - 1), dy_col)),
                  pl.BlockSpec((len(POOL_WINDOWS), G, G), lambda i: (0, 0, 0)),
                  pl.BlockSpec((1, W), lambda i: (0, 0))],
        out_specs=[pl.BlockSpec((tm, W), lambda i: (i, 0)),
                   pl.BlockSpec((len(POOL_WINDOWS), G, G), lambda i: (0, 0, 0)),
                   pl.BlockSpec((1, W), lambda i: (0, 0))],
        out_shape=[jax.ShapeDtypeStruct((S, W), BF16),
                   jax.ShapeDtypeStruct((len(POOL_WINDOWS), G, G), F32),
                   jax.ShapeDtypeStruct((1, W), F32)],
        compiler_params=_cp("arbitrary"),
    )(u, u, dy, dy, pool_w, pool_scale)


GELU_K0 = math.sqrt(2.0 / math.pi)
GELU_K1 = 0.044715


def _gelu_parts(x):
    x2 = x * x
    t = jnp.tanh(x * (GELU_K0 + (GELU_K0 * GELU_K1) * x2))
    hp = 0.5 + 0.5 * t
    gelu = x * hp
    dgelu = hp + (x * (hp * (1.0 - t))) * (GELU_K0 + (3.0 * GELU_K0 * GELU_K1) * x2)
    return gelu, dgelu


def _shifted(ext, halo):
    return (pltpu.roll(ext, 2, 0)[halo:], pltpu.roll(ext, 1, 0)[halo:], ext[halo:])


def _conv(sh, w, b):
    return b + (sh[0] * w[0:1] + sh[1] * w[1:2] + sh[2] * w[2:3])


F32_ROWS = 8


def _ffn_up_glu(h, w_up_t, conv_w, conv_b, *, name, tm=2048, tn=256, sub=256):
    S, K = h.shape
    F = D_FF
    nj = F // tn

    def body(h_ref, wg_ref, wv_ref, cwg_ref, cwv_ref, cbg_ref, cbv_ref,
             ug_ref, uv_ref, cg_ref, cv_ref, y_ref, carry):
        i = pl.program_id(0)
        j = pl.program_id(1)

        w_cat = jnp.concatenate([wg_ref[...], wv_ref[...]], axis=0)
        conv_w_b = ((cwg_ref[...], cbg_ref[...]), (cwv_ref[...], cbv_ref[...]))
        halo = [jnp.where(i > 0, carry[j, s], 0.0) for s in range(2)]
        for a in range(0, tm, sub):
            u16 = _dot(h_ref[a:a + sub, :], w_cat, NT).astype(BF16)
            ug_ref[a:a + sub, :] = u16[:, :tn]
            uv_ref[a:a + sub, :] = u16[:, tn:]
            c = []
            for s, (cw, cb) in enumerate(conv_w_b):
                u = u16[:, s * tn:(s + 1) * tn].astype(F32)
                ext = jnp.concatenate([halo[s], u], axis=0)
                c.append(_conv(_shifted(ext, F32_ROWS), cw, cb))
                halo[s] = u[sub - F32_ROWS:]
            cg_ref[a:a + sub, :] = c[0].astype(BF16)
            cv_ref[a:a + sub, :] = c[1].astype(BF16)
            gelu, _ = _gelu_parts(c[0])
            y_ref[a:a + sub, :] = (gelu * c[1]).astype(BF16)
        for s in range(2):
            carry[j, s] = halo[s]

    tile = pl.BlockSpec((tm, tn), lambda i, j: (i, j))
    vec = lambda rows, off: pl.BlockSpec((rows, tn), lambda i, j: (0, j + off))
    return pl.pallas_call(
        body, name=name, grid=(S // tm, nj),
        in_specs=[pl.BlockSpec((tm, K), lambda i, j: (i, 0)),
                  pl.BlockSpec((tn, K), lambda i, j: (j, 0)), pl.BlockSpec((tn, K), lambda i, j: (j + nj, 0)),
                  vec(3, 0), vec(3, nj), vec(1, 0), vec(1, nj)],
        out_specs=[tile] * 5,
        out_shape=[jax.ShapeDtypeStruct((S, F), BF16)] * 5,
        scratch_shapes=[pltpu.VMEM((nj, 2, F32_ROWS, tn), F32)],
        compiler_params=_cp("arbitrary", "arbitrary"),
    )(h, w_up_t, w_up_t, conv_w, conv_w, conv_b, conv_b)


def _ffn_glu_bwd(u_g, u_v, c_g, c_v, df, w_down, h, conv_w, *, name, tm=2048, tn=256, sub=256):
    S = u_g.shape[0]
    F = D_FF
    D = df.shape[1]
    nj = F // tn
    nt = S // tm

    def body(ug_ref, uv_ref, cg_ref, cgn_ref, cv_ref, cvn_ref, df_ref, dfn_ref, wd_ref, h_ref, wg_ref, wv_ref,
             dug_ref, duv_ref, gug_ref, guv_ref, gd_ref, gwg_ref, gwv_ref, gbg_ref, gbv_ref,
             acc_u, acc_d):
        i = pl.program_id(1)

        @pl.when(i == 0)
        def _():
            for r in (gwg_ref, gwv_ref, gbg_ref, gbv_ref, acc_u, acc_d):
                r[...] = jnp.zeros_like(r)

        wg, wv = wg_ref[...], wv_ref[...]
        wd = wd_ref[...]
        dfn = jnp.where(i < nt - 1, dfn_ref[...], jnp.zeros_like(dfn_ref))
        n_ext = sub + HALO

        def ahead(dc):
            return dc[:sub], pltpu.roll(dc, n_ext - 1, 0)[:sub], pltpu.roll(dc, n_ext - 2, 0)[:sub]

        for a in range(0, tm, sub):
            b = a + sub
            ext = lambda ref, nxt: jnp.concatenate(
                [ref[a:b, :], ref[b:b + HALO, :] if b < tm else nxt], axis=0)
            cg = ext(cg_ref, cgn_ref[...]).astype(F32)
            cv = ext(cv_ref, cvn_ref[...]).astype(F32)
            df_sub = df_ref[a:b, :]
            dy_ext = _dot(ext(df_ref, dfn), wd, NT)
            gelu, dgelu = _gelu_parts(cg)
            dcs_g = ahead(dy_ext * cv * dgelu)
            dcs_v = ahead(dy_ext * gelu)
            du_g = (dcs_g[0] * wg[2:3] + dcs_g[1] * wg[1:2] + dcs_g[2] * wg[0:1]).astype(BF16)
            du_v = (dcs_v[0] * wv[2:3] + dcs_v[1] * wv[1:2] + dcs_v[2] * wv[0:1]).astype(BF16)
            dug_ref[a:b, :] = du_g
            duv_ref[a:b, :] = du_v
            acc_u[...] += _dot(jnp.concatenate([du_g, du_v], axis=1), h_ref[a:b, :], TN)
            acc_d[...] += _dot((gelu[:sub] * cv[:sub]).astype(BF16), df_sub, TN)
            for dcs, u_ref, gw_ref, gb_ref in ((dcs_g, ug_ref, gwg_ref, gbg_ref), (dcs_v, uv_ref, gwv_ref, gbv_ref)):
                u = u_ref[a:b, :].astype(F32)
                gb_ref[...] += jnp.sum(dcs[0], axis=0, keepdims=True)
                for k in range(3):
                    gw_ref[k:k + 1, :] += jnp.sum(dcs[2 - k] * u, axis=0, keepdims=True)

        @pl.when(i == nt - 1)
        def _():
            gug_ref[...] = acc_u[:tn, :].astype(BF16)
            guv_ref[...] = acc_u[tn:, :].astype(BF16)
            gd_ref[...] = acc_d[...].astype(BF16)

    per = tm // HALO
    nh = S // HALO
    hnext = lambda i: jnp.minimum((i + 1) * per, nh - 1)
    tile = pl.BlockSpec((tm, tn), lambda j, i: (i, j))
    hn = pl.BlockSpec((HALO, tn), lambda j, i: (hnext(i), j))
    vec = lambda rows, off: pl.BlockSpec((rows, tn), lambda j, i: (0, j + off))
    wide = pl.BlockSpec((tm, D), lambda j, i: (i, 0))
    wrow = pl.BlockSpec((tn, D), lambda j, i: (j, 0))
    return pl.pallas_call(
        body, name=name, grid=(nj, nt),
        in_specs=[tile, tile, tile, hn, tile, hn, wide, pl.BlockSpec((HALO, D), lambda j, i: (hnext(i), 0)),
                  wrow, wide, vec(3, 0), vec(3, nj)],
        out_specs=[tile, tile, wrow, wrow, wrow, vec(3, 0), vec(3, 0), vec(1, 0), vec(1, 0)],
        out_shape=[jax.ShapeDtypeStruct((S, F), BF16), jax.ShapeDtypeStruct((S, F), BF16),
                   jax.ShapeDtypeStruct((F, D), BF16), jax.ShapeDtypeStruct((F, D), BF16),
                   jax.ShapeDtypeStruct((F, D), BF16),
                   jax.ShapeDtypeStruct((3, F), F32), jax.ShapeDtypeStruct((3, F), F32),
                   jax.ShapeDtypeStruct((1, F), F32), jax.ShapeDtypeStruct((1, F), F32)],
        scratch_shapes=[pltpu.VMEM((2 * tn, D), F32), pltpu.VMEM((tn, D), F32)],
        compiler_params=_cp("parallel", "arbitrary"),
    )(u_g, u_v, c_g, c_g, c_v, c_v, df, df, w_down, h, conv_w, conv_w)


def _sum_partials(parts, *, name, tr):
    _, R, C = parts.shape

    def body(p_ref, o_ref):
        tot = p_ref[0].astype(F32)
        for j in range(1, N_DEV):
            tot = tot + p_ref[j].astype(F32)
        o_ref[...] = tot

    return pl.pallas_call(
        body, name=name, grid=(R // tr,),
        in_specs=[pl.BlockSpec((N_DEV, tr, C), lambda i: (0, i, 0))],
        out_specs=pl.BlockSpec((tr, C), lambda i: (i, 0)),
        out_shape=jax.ShapeDtypeStruct((R, C), F32),
        compiler_params=_cp("parallel"),
    )(parts)


def _adamw(w, g, m, v, *, name, tr):
    R, C = w.shape
    c1 = 1.0 - ADAM_B1 ** ADAM_STEP
    c2 = 1.0 - ADAM_B2 ** ADAM_STEP

    def body(w_ref, g_ref, m_ref, v_ref, d_ref, nm_ref, nv_ref):
        g = g_ref[...]
        nm = ADAM_B1 * m_ref[...] + (1.0 - ADAM_B1) * g
        nv = ADAM_B2 * v_ref[...] + (1.0 - ADAM_B2) * (g * g)
        d_ref[...] = -ADAM_LR * ((nm / c1) / (jnp.sqrt(nv / c2) + ADAM_EPS) + ADAM_WD * w_ref[...])
        nm_ref[...] = nm
        nv_ref[...] = nv

    spec = pl.BlockSpec((tr, C), lambda i: (i, 0))
    return pl.pallas_call(
        body, name=name, grid=(R // tr,), in_specs=[spec] * 4, out_specs=[spec] * 3,
        out_shape=[jax.ShapeDtypeStruct((R, C), F32)] * 3,
        compiler_params=_cp("parallel"),
    )(w, g, m, v)


def _mesh_pos():
    return lax.axis_index("x"), lax.axis_index("y"), lax.axis_index("c")


def _gather_phases(x_refs, out_refs, send_sems, recv_sems, local_sems):
    x, y, c = _mesh_pos()
    me, sibling = (x, y, c), (x, y, 1 - c)
    chips = [(1 - x, y), (x, 1 - y), (1 - x, 1 - y)]
    arrays = range(len(x_refs))

    def slot(a, px, py, pc):
        return out_refs[a].at[4 * px + 2 * py + pc]

    def copy(a, k, block, to, own=False):
        return pltpu.make_async_remote_copy(
            src_ref=x_refs[a] if own else slot(a, *block), dst_ref=slot(a, *block),
            send_sem=send_sems.at[a, k], recv_sem=recv_sems.at[a, k], device_id=to, device_id_type=MESH)

    mine = [pltpu.make_async_copy(x_refs[a], slot(a, *me), local_sems.at[a]) for a in arrays]
    first = [copy(a, 0, me, sibling, own=True) for a in arrays]
    first += [copy(a, 1 + j, me, (*chip, c), own=True) for j, chip in enumerate(chips) for a in arrays]
    passed = [[copy(a, 4 + j, (*chip, c), sibling) for a in arrays] for j, chip in enumerate(chips)]

    def start():
        for cp in mine + first:
            cp.start()

    def forward():
        for j, chip in enumerate(chips):
            for a in arrays:
                copy(a, 1 + j, (*chip, c), me).wait_recv()
                passed[j][a].start()

    def finish():
        for a in arrays:
            copy(a, 0, sibling, me).wait_recv()
            for j, chip in enumerate(chips):
                copy(a, 4 + j, (*chip, 1 - c), me).wait_recv()
        for cp in first + [cp for row in passed for cp in row]:
            cp.wait_send()
        for cp in mine:
            cp.wait()

    return start, forward, finish


def _gather_sems(n):
    return [pltpu.SemaphoreType.DMA((n, 7)), pltpu.SemaphoreType.DMA((n, 7)), pltpu.SemaphoreType.DMA((n,))]


def _gathered_shapes(blocks):
    return [jax.ShapeDtypeStruct((N_DEV,) + b.shape, b.dtype) for b in blocks]


def _all_reduce_small(block, *, name):
    def body(x_ref, all_ref, sum_ref, *sems):
        for phase in _gather_phases([x_ref], [all_ref], *sems):
            phase()
        tot = all_ref[0]
        for j in range(1, N_DEV):
            tot = tot + all_ref[j]
        sum_ref[...] = tot

    return pl.pallas_call(
        body, name=name, in_specs=[VMEM], out_specs=[VMEM, VMEM],
        out_shape=[jax.ShapeDtypeStruct((N_DEV,) + block.shape, block.dtype),
                   jax.ShapeDtypeStruct(block.shape, block.dtype)],
        scratch_shapes=_gather_sems(1),
        compiler_params=pltpu.CompilerParams(vmem_limit_bytes=V7X_VMEM_LIMIT),
    )(block)[1]


def _exchange_phases(g_refs, r_refs, send_sems, recv_sems, local_sems):
    x, y, c = _mesh_pos()
    me = 4 * x + 2 * y + c
    owns, remote = [], []
    for k, (g_ref, r_ref) in enumerate(zip(g_refs, r_refs)):
        rows = g_ref.shape[0] // N_DEV
        owns.append(pltpu.make_async_copy(g_ref.at[pl.ds(me * rows, rows)], r_ref.at[me], local_sems.at[k]))
        for p in range(1, N_DEV):
            px, py, pc = x ^ (p >> 2), y ^ ((p >> 1) & 1), c ^ (p & 1)
            peer = 4 * px + 2 * py + pc
            link = dict(send_sem=send_sems.at[k, p], recv_sem=recv_sems.at[k, p],
                        device_id=(px, py, pc), device_id_type=MESH)
            src = g_ref.at[pl.ds(peer * rows, rows)]
            send = pltpu.make_async_remote_copy(src_ref=src, dst_ref=r_ref.at[me], **link)
            arrival = pltpu.make_async_remote_copy(src_ref=src, dst_ref=r_ref.at[peer], **link)
            remote.append((send, arrival))

    def start():
        for own in owns:
            own.start()
        for send, _ in remote:
            send.start()

    def finish():
        for _, arrival in remote:
            arrival.wait_recv()
        for send, _ in remote:
            send.wait_send()
        for own in owns:
            own.wait()

    return start, finish


def _exchange_buffers(grads):
    n = len(grads)
    shapes = [jax.ShapeDtypeStruct((N_DEV, g.shape[0] // N_DEV, g.shape[1]), g.dtype) for g in grads]
    sems = [pltpu.SemaphoreType.DMA((n, N_DEV)), pltpu.SemaphoreType.DMA((n, N_DEV)),
            pltpu.SemaphoreType.DMA((n,))]
    return shapes, sems


def _unpack_gathered(gathered):
    w_out, w_up_t, w_down = (g.reshape(-1, D_MODEL) for g in gathered[:3])
    width = 2 * D_FF // N_DEV
    conv_w = jnp.transpose(gathered[3][:, :3, :width], (1, 0, 2)).reshape(3, 2 * D_FF)
    return w_out, w_up_t, w_down, conv_w


def _rest_payload(w_out, w_up, w_down, conv_w):
    rows, cols = conv_w.shape
    conv_w = jnp.pad(conv_w, ((0, (-rows) % F32_ROWS), (0, (-cols) % LANES)))
    return [w_out.astype(BF16), w_up.T.astype(BF16), w_down.astype(BF16), conv_w]


def _device_step(x, target, g_mix_pre, w_in_t_block, rest_payload, pool_w, pool_scale, g_mix_post, g_ffn_pre,
                 conv_b, g_ffn_post):
    h1, w_in_t = _rms_norm_gather(x, g_mix_pre, w_in_t_block, name="rms_mix_pre")
    w_in_t = w_in_t.reshape(-1, D_MODEL)
    proj = _matmul(h1, w_in_t, trans_b=True, out_dtype=F32, tm=1024, tn=512, name="proj")
    attn, lse, attn16, gathered = _attn_fwd(proj, rest_payload, name="attn_fwd")
    w_out, w_up_t, w_down, conv_w = _unpack_gathered(gathered)
    pool = _pool_fwd(proj, 3, pool_w, pool_scale, name="pool_fwd")
    mixed, x2, h2 = _mix_out(attn16, pool, w_out, x, g_mix_post, g_ffn_pre, name="mix_out")
    u_g, u_v, c_g, c_v, y = _ffn_up_glu(h2, w_up_t, conv_w, conv_b, name="ffn_up_glu")
    df, d_out, loss_blk, gg_ffn_post = _ffn_out(y, w_down, x2, target, g_ffn_post, name="ffn_out")
    du_g, du_v, gw_up_g, gw_up_v, gw_down, gcw_g, gcw_v, gcb_g, gcb_v = _ffn_glu_bwd(
        u_g, u_v, c_g, c_v, df, w_down, h2, conv_w, name="ffn_glu_bwd")
    gw_up_t = jnp.concatenate([gw_up_g, gw_up_v], axis=0)
    dx2, gg_ffn_pre, dmixed, gg_mix_post = _dgrad_norm(
        [du_g, du_v], w_up_t, d_out, x2, g_ffn_pre, (mixed, g_mix_post), [], name="ffn_up_dgrad")
    gw_out = jnp.concatenate([_matmul_tn(attn16, dmixed, ta=512, ts=1024, name="grad_w_out_attn"),
                              _matmul_tn(pool, dmixed, ta=512, ts=1024, name="grad_w_out_pool")], axis=0)
    dcat = _matmul(dmixed, w_out, trans_b=True, out_dtype=F32, tm=512, tn=1024, name="mix_out_dgrad")
    d_pool_in, g_pool_w, g_pool_scale = _pool_bwd(proj, 3, dcat, 1, pool_w, pool_scale, name="pool_bwd")
    dqkv, (r_out, r_up_t, r_down) = _attn_bwd(proj, dcat, attn, lse, [gw_out, gw_up_t, gw_down], name="attn_bwd")
    dproj = list(dqkv) + [d_pool_in]
    gw_in_t = jnp.concatenate([_matmul_tn(a, h1, ta=512, ts=1024, name=f"grad_w_in_{k}")
                               for k, a in enumerate(dproj)], axis=0)
    grad_x, gg_mix_pre, (r_in_t,) = _dgrad_norm(dproj, w_in_t, dx2, x, g_mix_pre, None, [gw_in_t], name="proj_dgrad")
    g_conv_w = jnp.concatenate([gcw_g, gcw_v], axis=1)
    g_conv_b = jnp.concatenate([gcb_g, gcb_v], axis=1)
    received = (r_in_t, r_out, r_up_t, r_down)
    small = dict(g_mix_pre=gg_mix_pre, g_mix_post=gg_mix_post, g_ffn_pre=gg_ffn_pre, g_ffn_post=gg_ffn_post,
                 pool_scale=g_pool_scale, conv_b=g_conv_b, pool_w=g_pool_w, conv_w=g_conv_w)
    return loss_blk, grad_x, received, small


_SMALL = ("g_mix_pre", "g_mix_post", "g_ffn_pre", "g_ffn_post", "pool_scale", "conv_b", "pool_w")
LANES = 128


def _pack_rows(arrays):
    parts = []
    for a in arrays:
        a2 = a.reshape(-1, LANES)
        parts.append(jnp.pad(a2, ((0, (-a2.shape[0]) % 8), (0, 0))))
    return jnp.concatenate(parts, axis=0)


def _unpack_rows(packed, shapes):
    out, row = [], 0
    for shape in shapes:
        rows = math.prod(shape) // LANES
        out.append(packed[row:row + rows].reshape(shape))
        row += -(-rows // 8) * 8
    return out


def kernel(x, g_mix_pre, w_in, pool_w, pool_scale, w_out, g_mix_post, g_ffn_pre, w_up, conv_w, conv_b, w_down, g_ffn_post, loss_target, m_g_mix_pre, m_w_in, m_pool_w, m_pool_scale, m_w_out, m_g_mix_post, m_g_ffn_pre, m_w_up, m_conv_w, m_conv_b, m_w_down, m_g_ffn_post, v_g_mix_pre, v_w_in, v_pool_w, v_pool_scale, v_w_out, v_g_mix_post, v_g_ffn_pre, v_w_up, v_conv_w, v_conv_b, v_w_down, v_g_ffn_post):
    me = 4 * lax.axis_index("x") + 2 * lax.axis_index("y") + lax.axis_index("c")
    loss_blk, grad_x, recv, small = _device_step(
        x[0], loss_target[0], g_mix_pre, w_in[0].T.astype(BF16),
        _rest_payload(w_out[0], w_up[0], w_down[0], conv_w[0]),
        pool_w[0], pool_scale, g_mix_post, g_ffn_pre, conv_b, g_ffn_post)

    g_in_t, g_out, g_up_t, g_down = (
        _sum_partials(r, name=f"sum_partials_{k}", tr=r.shape[1] // 2) for k, r in enumerate(recv))
    grads = {"w_in": g_in_t.T, "w_out": g_out, "w_up": g_up_t.T, "w_down": g_down}

    given = dict(g_mix_pre=g_mix_pre, g_mix_post=g_mix_post, g_ffn_pre=g_ffn_pre, g_ffn_post=g_ffn_post,
                 pool_scale=pool_scale, conv_b=conv_b, pool_w=pool_w)
    small_shapes = [given[k].shape for k in _SMALL]
    total = _all_reduce_small(_pack_rows([small[k] for k in _SMALL] + [small["conv_w"], loss_blk]),
                              name="all_reduce_small")
    *small_grads, g_conv_w_all, loss_all = _unpack_rows(total, small_shapes + [(3, 2 * D_FF), loss_blk.shape])
    loss = loss_all[0, 0]
    grads.update(zip(_SMALL, small_grads))
    width = 2 * D_FF // N_DEV
    grads["conv_w"] = lax.dynamic_slice_in_dim(g_conv_w_all, me * width, width, axis=1)[None]

    weights = dict(g_mix_pre=g_mix_pre, w_in=w_in, pool_w=pool_w, pool_scale=pool_scale, w_out=w_out,
                   g_mix_post=g_mix_post, g_ffn_pre=g_ffn_pre, w_up=w_up, conv_w=conv_w, conv_b=conv_b,
                   w_down=w_down, g_ffn_post=g_ffn_post)
    m_in = dict(g_mix_pre=m_g_mix_pre, w_in=m_w_in, pool_w=m_pool_w, pool_scale=m_pool_scale, w_out=m_w_out,
                g_mix_post=m_g_mix_post, g_ffn_pre=m_g_ffn_pre, w_up=m_w_up, conv_w=m_conv_w, conv_b=m_conv_b,
                w_down=m_w_down, g_ffn_post=m_g_ffn_post)
    v_in = dict(g_mix_pre=v_g_mix_pre, w_in=v_w_in, pool_w=v_pool_w, pool_scale=v_pool_scale, w_out=v_w_out,
                g_mix_post=v_g_mix_post, g_ffn_pre=v_g_ffn_pre, w_up=v_w_up, conv_w=v_conv_w, conv_b=v_conv_b,
                w_down=v_w_down, g_ffn_post=v_g_ffn_post)
    delta, new_m, new_v = {}, {}, {}
    for k in ("w_in", "w_out", "w_up", "w_down"):
        g = grads[k]
        d, nm, nv = _adamw(weights[k][0], g, m_in[k][0], v_in[k][0], name=f"adamw_{k}", tr=g.shape[0] // 2)
        grads[k], delta[k], new_m[k], new_v[k] = g[None], d[None], nm[None], nv[None]
    d, nm, nv = _adamw(weights["conv_w"][0], grads["conv_w"][0], m_in["conv_w"][0], v_in["conv_w"][0],
                       name="adamw_conv_w", tr=3)
    delta["conv_w"], new_m["conv_w"], new_v["conv_w"] = d[None], nm[None], nv[None]
    packed_w = _pack_rows([weights[k] for k in _SMALL])
    small_rows = packed_w.shape[0]
    d, nm, nv = _adamw(packed_w, total[:small_rows], _pack_rows([m_in[k] for k in _SMALL]),
                       _pack_rows([v_in[k] for k in _SMALL]), name="adamw_small", tr=small_rows)
    for k, dk, mk, vk in zip(_SMALL, _unpack_rows(d, small_shapes), _unpack_rows(nm, small_shapes),
                             _unpack_rows(nv, small_shapes)):
        delta[k], new_m[k], new_v[k] = dk, mk, vk

    order = ("g_mix_pre", "w_in", "pool_w", "pool_scale", "w_out", "g_mix_post", "g_ffn_pre", "w_up",
             "conv_w", "conv_b", "w_down", "g_ffn_post")
    return (loss, grad_x[None], *[grads[k] for k in order], *[delta[k] for k in order],
            *[new_m[k] for k in order], *[new_v[k] for k in order])
```

```python
import functools
import math

import jax
import jax.numpy as jnp
from jax import lax
from jax.experimental import pallas as pl
from jax.experimental.pallas import tpu as pltpu

F32 = jnp.float32
BF16 = jnp.bfloat16

D_MODEL = 1024
ATTN_WIDTH = 512
N_HEADS = 8
HEAD_DIM = 64
DILATIONS = (1, 4, 16)
BLOCK = 128
POOL_WIDTH = 512
POOL_WINDOWS = (2, 4, 8, 16)
POOL_GROUP_DIM = 128
D_FF = 2816
EPS = 1e-6
NEG_INF = -1e30
SCALE = HEAD_DIM ** -0.5

ADAM_LR = 0.001
ADAM_B1 = 0.9
ADAM_B2 = 0.999
ADAM_EPS = 1e-08
ADAM_WD = 0.01
ADAM_STEP = 10

N_DEV = 8
HALO = 16
V7X_VMEM_LIMIT = 56 * 1024 * 1024

MESH = pl.DeviceIdType.MESH
ANY = pl.BlockSpec(memory_space=pl.ANY)
VMEM = pl.BlockSpec(memory_space=pltpu.VMEM)

NT = (((1,), (1,)), ((), ()))
NN = (((1,), (0,)), ((), ()))
TN = (((0,), (0,)), ((), ()))


def _cp(*sem):
    return pltpu.CompilerParams(dimension_semantics=sem, vmem_limit_bytes=V7X_VMEM_LIMIT)


def _dot(a, b, dn):
    return lax.dot_general(a, b, dn, preferred_element_type=F32)


def _rms_bwd(xin, g, dy):
    r = lax.rsqrt(jnp.mean(xin * xin, axis=-1, keepdims=True) + EPS)
    xh = xin * r
    gdy = g * dy
    dx = r * (gdy - xh * jnp.mean(gdy * xh, axis=-1, keepdims=True))
    dg = jnp.sum(dy * xh, axis=0, keepdims=True)
    return dx, dg


def _rms_norm_gather(x, g, block, *, name, tm=512):
    S, D = x.shape
    nt = S // tm

    def body(x_ref, g_ref, blk_ref, o_ref, all_ref, *sems):
        i = pl.program_id(0)
        start, forward, finish = _gather_phases([blk_ref], [all_ref], *sems)
        pl.when(i == 0)(start)
        xv = x_ref[...]
        r = lax.rsqrt(jnp.mean(xv * xv, axis=-1, keepdims=True) + EPS)
        o_ref[...] = (xv * r * g_ref[...]).astype(BF16)
        pl.when(i == (2 * nt) // 3)(forward)
        pl.when(i == nt - 1)(finish)

    return pl.pallas_call(
        body, name=name, grid=(nt,),
        in_specs=[pl.BlockSpec((tm, D), lambda i: (i, 0)), pl.BlockSpec((1, D), lambda i: (0, 0)), ANY],
        out_specs=[pl.BlockSpec((tm, D), lambda i: (i, 0)), ANY],
        out_shape=[jax.ShapeDtypeStruct((S, D), BF16)] + _gathered_shapes([block]),
        scratch_shapes=_gather_sems(1),
        compiler_params=_cp("arbitrary"),
    )(x, g, block)


def _matmul(a, b, *, trans_b, out_dtype, tm, tn, name):
    M, K = a.shape
    N = b.shape[0] if trans_b else b.shape[1]
    dn = NT if trans_b else NN

    def body(a_ref, b_ref, o_ref):
        o_ref[...] = _dot(a_ref[...], b_ref[...], dn).astype(out_dtype)

    b_spec = (pl.BlockSpec((tn, K), lambda i, j: (j, 0)) if trans_b
              else pl.BlockSpec((K, tn), lambda i, j: (0, j)))
    return pl.pallas_call(
        body, name=name, grid=(M // tm, N // tn),
        in_specs=[pl.BlockSpec((tm, K), lambda i, j: (i, 0)), b_spec],
        out_specs=pl.BlockSpec((tm, tn), lambda i, j: (i, j)),
        out_shape=jax.ShapeDtypeStruct((M, N), out_dtype),
        compiler_params=_cp("parallel", "parallel"),
    )(a, b)


def _matmul_tn(a, b, *, ta, ts, name):
    S, Ka = a.shape
    Nb = b.shape[1]
    ns = S // ts

    def body(a_ref, b_ref, o_ref, acc):
        s = pl.program_id(1)

        @pl.when(s == 0)
        def _():
            acc[...] = jnp.zeros_like(acc)

        acc[...] += _dot(a_ref[...], b_ref[...], TN)

        @pl.when(s == ns - 1)
        def _():
            o_ref[...] = acc[...].astype(BF16)

    return pl.pallas_call(
        body, name=name, grid=(Ka // ta, ns),
        in_specs=[pl.BlockSpec((ts, ta), lambda i, s: (s, i)), pl.BlockSpec((ts, Nb), lambda i, s: (s, 0))],
        out_specs=pl.BlockSpec((ta, Nb), lambda i, s: (i, 0)),
        out_shape=jax.ShapeDtypeStruct((Ka, Nb), BF16),
        scratch_shapes=[pltpu.VMEM((ta, Nb), F32)],
        compiler_params=_cp("parallel", "arbitrary"),
    )(a, b)


def _mix_out(attn, pool, w_out, x, g_post, g_next, *, name, tm=256):
    S, K = attn.shape
    D = w_out.shape[1]

    def body(a_ref, p_ref, w_ref, x_ref, gp_ref, gn_ref, mixed_ref, x2_ref, h2_ref):
        mixed = _dot(a_ref[...], w_ref[:K, :], NN) + _dot(p_ref[...], w_ref[K:, :], NN)
        r = lax.rsqrt(jnp.mean(mixed * mixed, axis=-1, keepdims=True) + EPS)
        x2 = x_ref[...] + mixed * r * gp_ref[...]
        r2 = lax.rsqrt(jnp.mean(x2 * x2, axis=-1, keepdims=True) + EPS)
        mixed_ref[...] = mixed
        x2_ref[...] = x2
        h2_ref[...] = (x2 * r2 * gn_ref[...]).astype(BF16)

    row = lambda i: (i, 0)
    fix = lambda i: (0, 0)
    return pl.pallas_call(
        body, name=name, grid=(S // tm,),
        in_specs=[pl.BlockSpec((tm, K), row), pl.BlockSpec((tm, K), row), pl.BlockSpec((2 * K, D), fix),
                  pl.BlockSpec((tm, D), row), pl.BlockSpec((1, D), fix), pl.BlockSpec((1, D), fix)],
        out_specs=[pl.BlockSpec((tm, D), row)] * 3,
        out_shape=[jax.ShapeDtypeStruct((S, D), F32), jax.ShapeDtypeStruct((S, D), F32),
                   jax.ShapeDtypeStruct((S, D), BF16)],
        compiler_params=_cp("parallel"),
    )(attn, pool, w_out, x, g_post, g_next)


def _ffn_out(y, w_down, x2, target, g_post, *, name, tm=512):
    S, K = y.shape
    D = w_down.shape[1]

    def body(y_ref, w_ref, x2_ref, t_ref, g_ref, df_ref, dout_ref, loss_ref, gg_ref):
        i = pl.program_id(0)

        @pl.when(i == 0)
        def _():
            loss_ref[...] = jnp.zeros_like(loss_ref)
            gg_ref[...] = jnp.zeros_like(gg_ref)

        f = _dot(y_ref[...], w_ref[...], NN)
        g = g_ref[...]
        r = lax.rsqrt(jnp.mean(f * f, axis=-1, keepdims=True) + EPS)
        out = x2_ref[...] + f * r * g
        err = out - t_ref[...]
        dy = err * (1.0 / D)
        df, dg = _rms_bwd(f, g, dy)
        df_ref[...] = df.astype(BF16)
        dout_ref[...] = dy
        gg_ref[...] += dg
        loss_ref[...] += 0.5 * jnp.sum(jnp.mean(err * err, axis=-1, keepdims=True))

    row = lambda i: (i, 0)
    fix = lambda i: (0, 0)
    return pl.pallas_call(
        body, name=name, grid=(S // tm,),
        in_specs=[pl.BlockSpec((tm, K), row), pl.BlockSpec((K, D), fix), pl.BlockSpec((tm, D), row),
                  pl.BlockSpec((tm, D), row), pl.BlockSpec((1, D), fix)],
        out_specs=[pl.BlockSpec((tm, D), row), pl.BlockSpec((tm, D), row),
                   pl.BlockSpec((8, 128), fix), pl.BlockSpec((1, D), fix)],
        out_shape=[jax.ShapeDtypeStruct((S, D), BF16), jax.ShapeDtypeStruct((S, D), F32),
                   jax.ShapeDtypeStruct((8, 128), F32), jax.ShapeDtypeStruct((1, D), F32)],
        compiler_params=_cp("arbitrary"),
    )(y, w_down, x2, target, g_post)


def _dgrad_norm(a_list, w, resid, xin, g, second, exchange, *, name, tm=512):
    S, Kp = a_list[0].shape
    na = len(a_list)
    D = w.shape[1]
    nt = S // tm
    two = second is not None
    ng = len(exchange)
    recv_shapes, exchange_sems = _exchange_buffers(exchange)

    def body(*refs):
        a_refs = refs[:na]
        w_ref, r_ref, x_ref, g_ref = refs[na:na + 4]
        pos = na + 4
        if two:
            x2_ref, g2_ref = refs[pos:pos + 2]
            pos += 2
        g_refs = refs[pos:pos + ng]
        pos += ng
        dx_ref, gg_ref = refs[pos:pos + 2]
        pos += 2
        if two:
            d2_ref, gg2_ref = refs[pos:pos + 2]
            pos += 2
        r_refs = refs[pos:pos + ng]
        pos += ng
        i = pl.program_id(0)
        if ng:
            start, finish = _exchange_phases(g_refs, r_refs, *refs[pos:])
            pl.when(i == 0)(start)

        @pl.when(i == 0)
        def _():
            gg_ref[...] = jnp.zeros_like(gg_ref)
            if two:
                gg2_ref[...] = jnp.zeros_like(gg2_ref)

        dh = functools.reduce(jnp.add, [_dot(a_refs[q][...], w_ref[q * Kp:(q + 1) * Kp, :], NN) for q in range(na)])
        d1, dg1 = _rms_bwd(x_ref[...], g_ref[...], dh)
        dx = r_ref[...] + d1
        dx_ref[...] = dx
        gg_ref[...] += dg1
        if two:
            d2, dg2 = _rms_bwd(x2_ref[...], g2_ref[...], dx)
            d2_ref[...] = d2.astype(BF16)
            gg2_ref[...] += dg2
        if ng:
            pl.when(i == nt - 1)(finish)

    row = lambda i: (i, 0)
    fix = lambda i: (0, 0)
    in_specs = [pl.BlockSpec((tm, Kp), row)] * na + [
        pl.BlockSpec((na * Kp, D), fix, pipeline_mode=pl.Buffered(1)), pl.BlockSpec((tm, D), row),
        pl.BlockSpec((tm, D), row), pl.BlockSpec((1, D), fix)]
    args = list(a_list) + [w, resid, xin, g]
    out_specs = [pl.BlockSpec((tm, D), row), pl.BlockSpec((1, D), fix)]
    out_shape = [jax.ShapeDtypeStruct((S, D), F32), jax.ShapeDtypeStruct((1, D), F32)]
    if two:
        in_specs += [pl.BlockSpec((tm, D), row), pl.BlockSpec((1, D), fix)]
        args += list(second)
        out_specs += [pl.BlockSpec((tm, D), row), pl.BlockSpec((1, D), fix)]
        out_shape += [jax.ShapeDtypeStruct((S, D), BF16), jax.ShapeDtypeStruct((1, D), F32)]
    n_plain = len(out_shape)
    out = pl.pallas_call(
        body, name=name, grid=(nt,), in_specs=in_specs + [ANY] * ng, out_specs=out_specs + [ANY] * ng,
        out_shape=out_shape + recv_shapes, scratch_shapes=exchange_sems if ng else [],
        compiler_params=_cp("arbitrary"),
    )(*args, *exchange)
    return (*out[:n_plain], out[n_plain:]) if ng else out


def _band_mask(first_block):
    qi = lax.broadcasted_iota(jnp.int32, (BLOCK, 2 * BLOCK), 0)
    ki = lax.broadcasted_iota(jnp.int32, (BLOCK, 2 * BLOCK), 1)
    first_key = jnp.where(first_block, BLOCK, 0)
    return (ki >= qi) & (ki <= qi + BLOCK) & (ki >= first_key)


def _lane_masks():
    lane = lax.broadcasted_iota(jnp.int32, (1, 2 * HEAD_DIM), 1)
    return (lane < HEAD_DIM, lane >= HEAD_DIM)


CHUNK = BLOCK * max(DILATIONS)
SLAB = 2 * HEAD_DIM
N_SLABS = ATTN_WIDTH // SLAB


def _unit_rows(d, b):
    def rows(r):
        start = r + BLOCK * d * b
        return pl.ds(start, BLOCK, stride=d) if d > 1 else pl.ds(start, BLOCK)
    return rows


def _attn_units():
    for p, d in enumerate(DILATIONS):
        nbc = CHUNK // (BLOCK * d)
        for b in range(nbc):
            for r in range(d):
                yield p, d, b, r, nbc


def _attn_in_specs(nc, n_cur):
    prev = lambda c: jnp.maximum(jnp.minimum(c, nc - 1) - 1, 0)
    cur = lambda c: jnp.minimum(c, nc - 1)
    blk = lambda f: pl.BlockSpec((CHUNK, SLAB), f)
    specs = [blk(lambda h, c: (cur(c), h)),
             blk(lambda h, c: (prev(c), N_SLABS + h)), blk(lambda h, c: (cur(c), N_SLABS + h)),
             blk(lambda h, c: (prev(c), 2 * N_SLABS + h)), blk(lambda h, c: (cur(c), 2 * N_SLABS + h))]
    return specs + [blk(lambda h, c: (cur(c), h))] * n_cur


def _attn_fwd(proj, payload, *, name):
    S = proj.shape[0]
    nc = S // CHUNK
    n = len(DILATIONS)
    npay = len(payload)
    n_steps = N_SLABS * nc

    def body(*refs):
        q_ref, kp_ref, kc_ref, vp_ref, vc_ref = refs[:5]
        pay_refs = refs[5:5 + npay]
        attn_ref, lse_ref, attn16_ref = refs[5 + npay:8 + npay]
        all_refs = refs[8 + npay:8 + 2 * npay]
        scr = refs[8 + 2 * npay:]
        o_scr, l_scr = scr[:n], scr[n:2 * n]
        start, forward, finish = _gather_phases(pay_refs, all_refs, *scr[2 * n:])
        step = pl.program_id(0) * nc + pl.program_id(1)
        pl.when(step == 0)(start)
        c = pl.program_id(1)
        lms = _lane_masks()
        plain, first = (jnp.tile(_band_mask(f), (2, 1)) for f in (False, c == 0))
        for p, d, b, r, nbc in _attn_units():
            rows = _unit_rows(d, b)(r)
            prow = _unit_rows(d, (b - 1) % nbc)(r)
            kpr, vpr = (kc_ref, vc_ref) if b > 0 else (kp_ref, vp_ref)
            mask2 = plain if b > 0 else first
            q = q_ref[rows, :].astype(BF16)
            kcat = jnp.concatenate([kpr[prow, :], kc_ref[rows, :]], axis=0).astype(BF16)
            vcat = jnp.concatenate([vpr[prow, :], vc_ref[rows, :]], axis=0).astype(BF16)
            q2 = jnp.concatenate([jnp.where(lm, q, jnp.zeros_like(q)) for lm in lms], axis=0) * SCALE
            s = jnp.where(mask2, _dot(q2, kcat, NT), NEG_INF)
            m = jnp.max(s, axis=-1, keepdims=True)
            e = jnp.exp(s - m)
            l = jnp.sum(e, axis=-1, keepdims=True)
            o2 = _dot(e.astype(BF16), vcat, NN) / l
            lse2 = m + jnp.log(l)
            o_scr[p][rows, :] = jnp.where(lms[0], o2[:BLOCK], o2[BLOCK:])
            l_scr[p][rows, :] = jnp.where(lms[0], lse2[:BLOCK], lse2[BLOCK:])
        ls = [l_scr[p][...] for p in range(n)]
        top = functools.reduce(jnp.maximum, ls)
        es = [jnp.exp(l - top) for l in ls]
        den = functools.reduce(jnp.add, es)
        num = functools.reduce(jnp.add, [e * o_scr[p][...] for p, e in enumerate(es)])
        attn = num / den
        attn_ref[...] = attn
        attn16_ref[...] = attn.astype(BF16)
        lse_ref[...] = top + jnp.log(den)
        pl.when(step == (2 * n_steps) // 3)(forward)
        pl.when(step == n_steps - 1)(finish)

    out = pl.pallas_call(
        body, name=name, grid=(N_SLABS, nc), in_specs=_attn_in_specs(nc, 0) + [ANY] * npay,
        out_specs=[pl.BlockSpec((CHUNK, SLAB), lambda h, c: (c, h))] * 3 + [ANY] * npay,
        out_shape=[jax.ShapeDtypeStruct((S, ATTN_WIDTH), F32)] * 2 + [jax.ShapeDtypeStruct((S, ATTN_WIDTH), BF16)]
        + _gathered_shapes(payload),
        scratch_shapes=[pltpu.VMEM((CHUNK, SLAB), F32)] * (2 * n) + _gather_sems(npay),
        compiler_params=_cp("arbitrary", "arbitrary"),
    )(proj, proj, proj, proj, proj, *payload)
    return (*out[:3], out[3:])


def _attn_bwd(proj, dcat, attn, lse, grads, *, name):
    S = proj.shape[0]
    nc = S // CHUNK
    ng = len(grads)
    n = len(DILATIONS)
    recv_shapes, exchange_sems = _exchange_buffers(grads)

    def body(*refs):
        q_ref, kp_ref, kc_ref, vp_ref, vc_ref, do_ref, o_ref, lse_ref = refs[:8]
        g_refs = refs[8:8 + ng]
        dq_ref, dk_ref, dv_ref = refs[8 + ng:11 + ng]
        r_refs = refs[11 + ng:11 + 2 * ng]
        scr = refs[11 + 2 * ng:]
        dk_prev, dv_prev = scr[:2]
        delta_h, lse_h = scr[2:4], scr[4:6]
        dq_p, dk_own, dk_back, dv_own, dv_back = (scr[6 + n * k:6 + n * (k + 1)] for k in range(5))
        start, finish = _exchange_phases(g_refs, r_refs, *scr[6 + 5 * n:])
        c = pl.program_id(1)
        pl.when((pl.program_id(0) == 0) & (c == 0))(start)

        @pl.when(c == 0)
        def _():
            dk_prev[...] = jnp.zeros_like(dk_prev)
            dv_prev[...] = jnp.zeros_like(dv_prev)

        @pl.when(c < nc)
        def _():
            lms = _lane_masks()
            plain, first = (jnp.tile(_band_mask(f), (2, 1)) for f in (False, c == 0))
            prod = do_ref[...] * o_ref[...]
            lse = lse_ref[...]
            lse_other = pltpu.roll(lse, HEAD_DIM, 1)
            for h, lm in enumerate(lms):
                delta = jnp.sum(jnp.where(lm, prod, 0.0), axis=-1, keepdims=True)
                delta_h[h][...] = jnp.broadcast_to(delta, (CHUNK, SLAB))
                lse_h[h][...] = jnp.where(lm, lse, lse_other)
            wide = lambda refs, rows: jnp.tile(jnp.concatenate([r[rows, :] for r in refs], axis=0), (1, 2))
            stack = lambda f: jnp.concatenate([f(lm) for lm in lms], axis=0)
            for p, d, b, r, nbc in _attn_units():
                rows = _unit_rows(d, b)(r)
                prow = _unit_rows(d, (b - 1) % nbc)(r)
                kpr, vpr = (kc_ref, vc_ref) if b > 0 else (kp_ref, vp_ref)
                mask2 = plain if b > 0 else first
                q = q_ref[rows, :].astype(BF16)
                kcat = jnp.concatenate([kpr[prow, :], kc_ref[rows, :]], axis=0).astype(BF16)
                vcat = jnp.concatenate([vpr[prow, :], vc_ref[rows, :]], axis=0).astype(BF16)
                do = do_ref[rows, :]
                q2 = stack(lambda lm: jnp.where(lm, q, jnp.zeros_like(q))) * SCALE
                do2 = stack(lambda lm: jnp.where(lm, do, 0.0)).astype(BF16)
                e = jnp.where(mask2, jnp.exp(_dot(q2, kcat, NT) - wide(lse_h, rows)), 0.0)
                ds = (e * (_dot(do2, vcat, NT) - wide(delta_h, rows))).astype(BF16)
                dq = jnp.where(lms[0], _dot(ds[:BLOCK], kcat, NN), _dot(ds[BLOCK:], kcat, NN)) * SCALE
                dkc = _dot(ds, q2, TN)
                dvc = _dot(e.astype(BF16), do2, TN)
                dq_p[p][rows, :] = dq
                dk_own[p][rows, :] = dkc[BLOCK:]
                dv_own[p][rows, :] = dvc[BLOCK:]
                dk_back[p][prow, :] = dkc[:BLOCK]
                dv_back[p][prow, :] = dvc[:BLOCK]
            dq_ref[...] = functools.reduce(jnp.add, [r[...] for r in dq_p]).astype(BF16)
            for prev, own, back, out_ref in ((dk_prev, dk_own, dk_back, dk_ref), (dv_prev, dv_own, dv_back, dv_ref)):
                for p, d in enumerate(DILATIONS):
                    tail = CHUNK - BLOCK * d
                    prev[tail:, :] += back[p][tail:, :]
                out_ref[...] = prev[...].astype(BF16)
                prev[...] = functools.reduce(jnp.add, [r[...] for r in own])
                for p, d in enumerate(DILATIONS):
                    tail = CHUNK - BLOCK * d
                    if tail:
                        prev[:tail, :] += back[p][:tail, :]

        @pl.when(c == nc)
        def _():
            dk_ref[...] = dk_prev[...].astype(BF16)
            dv_ref[...] = dv_prev[...].astype(BF16)

        pl.when((pl.program_id(0) == N_SLABS - 1) & (c == nc))(finish)

    blk = lambda f: pl.BlockSpec((CHUNK, SLAB), f)
    late = lambda h, c: (jnp.maximum(c - 1, 0), h)
    out = pl.pallas_call(
        body, name=name, grid=(N_SLABS, nc + 1), in_specs=_attn_in_specs(nc, 3) + [ANY] * ng,
        out_specs=[blk(lambda h, c: (jnp.minimum(c, nc - 1), h)), blk(late), blk(late)] + [ANY] * ng,
        out_shape=[jax.ShapeDtypeStruct((S, ATTN_WIDTH), BF16)] * 3 + recv_shapes,
        scratch_shapes=[pltpu.VMEM((CHUNK, SLAB), F32)] * (6 + 5 * n) + exchange_sems,
        compiler_params=_cp("arbitrary", "arbitrary"),
    )(proj, proj, proj, proj, proj, dcat, attn, lse, *grads)
    return out[:3], out[3:]


def _split_bf16(a):
    hi = a.astype(BF16)
    lo = (a - hi.astype(F32)).astype(BF16)
    return hi, lo


def _pooled(ug, halo_g, w, row0, tm):
    ext = jnp.concatenate([halo_g, ug], axis=0)
    hi, lo = _split_bf16(ext)
    rr = lax.broadcasted_iota(jnp.int32, (tm, tm + HALO), 0)
    cc = lax.broadcasted_iota(jnp.int32, (tm, tm + HALO), 1)
    back = rr + HALO - cc
    win = ((back >= 0) & (back < w)).astype(BF16)
    wsum = _dot(win, hi, NN) + _dot(win, lo, NN)
    rows = row0 + lax.broadcasted_iota(jnp.int32, (tm, 1), 0)
    inv = 1.0 / jnp.minimum(rows + 1, w).astype(F32)
    return wsum * inv - ug


def _pool_fwd(u, u_col, pool_w, pool_scale, *, name, tm=256):
    S, W = u.shape[0], POOL_WIDTH
    G = POOL_GROUP_DIM

    def body(u_ref, h_ref, w_ref, s_ref, o_ref):
        i = pl.program_id(0)
        uv = u_ref[...]
        halo = jnp.where(i > 0, h_ref[...], 0.0)
        sls = [slice(g * G, (g + 1) * G) for g in range(len(POOL_WINDOWS))]
        pooled = [_pooled(uv[:, sl], halo[:, sl], w, i * tm, tm) for sl, w in zip(sls, POOL_WINDOWS)]
        zs = [_dot(p.astype(BF16), w_ref[g].astype(BF16), NN) for g, p in enumerate(pooled)]
        for sl, z in zip(sls, zs):
            o_ref[:, sl] = (z * s_ref[:, sl]).astype(BF16)

    per = tm // HALO
    return pl.pallas_call(
        body, name=name, grid=(S // tm,),
        in_specs=[pl.BlockSpec((tm, W), lambda i: (i, u_col)),
                  pl.BlockSpec((HALO, W), lambda i: (jnp.maximum(i * per - 1, 0), u_col)),
                  pl.BlockSpec((len(POOL_WINDOWS), G, G), lambda i: (0, 0, 0)),
                  pl.BlockSpec((1, W), lambda i: (0, 0))],
        out_specs=pl.BlockSpec((tm, W), lambda i: (i, 0)),
        out_shape=jax.ShapeDtypeStruct((S, W), BF16),
        compiler_params=_cp("parallel"),
    )(u, u, pool_w, pool_scale)


def _pool_bwd(u, u_col, dy, dy_col, pool_w, pool_scale, *, name, tm=256):
    S, W = u.shape[0], POOL_WIDTH
    G = POOL_GROUP_DIM
    nt = S // tm

    def body(u_ref, h_ref, dy_ref, dyn_ref, w_ref, s_ref, du_ref, gw_ref, gs_ref):
        i = pl.program_id(0)

        @pl.when(i == 0)
        def _():
            gw_ref[...] = jnp.zeros_like(gw_ref)
            gs_ref[...] = jnp.zeros_like(gs_ref)

        uv = u_ref[...]
        halo = jnp.where(i > 0, h_ref[...], 0.0)
        dyv = dy_ref[...]
        dyn = jnp.where(i < nt - 1, dyn_ref[...], 0.0)
        rr = lax.broadcasted_iota(jnp.int32, (tm, tm + HALO), 0)
        cc = lax.broadcasted_iota(jnp.int32, (tm, tm + HALO), 1)
        rows_ext = i * tm + lax.broadcasted_iota(jnp.int32, (tm + HALO, 1), 0)
        groups = list(enumerate(POOL_WINDOWS))
        sls = [slice(g * G, (g + 1) * G) for g, _ in groups]
        wgs = [w_ref[g].astype(BF16) for g, _ in groups]
        pooled = [_pooled(uv[:, sl], halo[:, sl], w, i * tm, tm).astype(BF16) for sl, (_, w) in zip(sls, groups)]
        dzs = [dyv[:, sl] * s_ref[:, sl] for sl in sls]
        dz_ext = [jnp.concatenate([dz, dyn[:, sl] * s_ref[:, sl]], axis=0).astype(BF16) for dz, sl in zip(dzs, sls)]
        dp_ext = [_dot(d, wg, NT) for d, wg in zip(dz_ext, wgs)]
        zs = [_dot(p, wg, NN) for p, wg in zip(pooled, wgs)]
        for (g, w), sl, p, dz, z, dp in zip(groups, sls, pooled, dzs, zs, dp_ext):
            gw_ref[g] += _dot(p, dz.astype(BF16), TN)
            gs_ref[:, sl] += jnp.sum(dyv[:, sl] * z, axis=0, keepdims=True)
            inv_ext = 1.0 / jnp.minimum(rows_ext + 1, w).astype(F32)
            hi, lo = _split_bf16(dp * inv_ext)
            ahead = cc - rr
            win = ((ahead >= 0) & (ahead < w)).astype(BF16)
            du_ref[:, sl] = (_dot(win, hi, NN) + _dot(win, lo, NN) - dp[:tm]).astype(BF16)

    per = tm // HALO
    nh = S // HALO
    return pl.pallas_call(
        body, name=name, grid=(nt,),
        in_specs=[pl.BlockSpec((tm, W), lambda i: (i, u_col)),
                  pl.BlockSpec((HALO, W), lambda i: (jnp.maximum(i * per - 1, 0), u_col)),
                  pl.BlockSpec((tm, W), lambda i: (i, dy_col)),
                  pl.BlockSpec((HALO, W), lambda i: (jnp.minimum((i + 1) * per, nh - 1), dy_col)),
                  pl.BlockSpec((len(POOL_WINDOWS), G, G), lambda i: (0, 0, 0)),
                  pl.BlockSpec((1, W), lambda i: (0, 0))],
        out_specs=[pl.BlockSpec((tm, W), lambda i: (i, 0)),
                   pl.BlockSpec((len(POOL_WINDOWS), G, G), lambda i: (0, 0, 0)),
                   pl.BlockSpec((1, W), lambda i: (0, 0))],
        out_shape=[jax.ShapeDtypeStruct((S, W), BF16),
                   jax.ShapeDtypeStruct((len(POOL_WINDOWS), G, G), F32),
                   jax.ShapeDtypeStruct((1, W), F32)],
        compiler_params=_cp("arbitrary"),
    )(u, u, dy, dy, pool_w, pool_scale)


GELU_K0 = math.sqrt(2.0 / math.pi)
GELU_K1 = 0.044715


def _gelu_parts(x):
    x2 = x * x
    t = jnp.tanh(x * (GELU_K0 + (GELU_K0 * GELU_K1) * x2))
    hp = 0.5 + 0.5 * t
    gelu = x * hp
    dgelu = hp + (x * (hp * (1.0 - t))) * (GELU_K0 + (3.0 * GELU_K0 * GELU_K1) * x2)
    return gelu, dgelu


def _shifted(ext, halo):
    return (pltpu.roll(ext, 2, 0)[halo:], pltpu.roll(ext, 1, 0)[halo:], ext[halo:])


def _conv(sh, w, b):
    return b + (sh[0] * w[0:1] + sh[1] * w[1:2] + sh[2] * w[2:3])


F32_ROWS = 8


def _ffn_up_glu(h, w_up_t, conv_w, conv_b, *, name, tm=2048, tn=256, sub=256):
    S, K = h.shape
    F = D_FF
    nj = F // tn

    def body(h_ref, wg_ref, wv_ref, cwg_ref, cwv_ref, cbg_ref, cbv_ref,
             ug_ref, uv_ref, cg_ref, cv_ref, y_ref, carry):
        i = pl.program_id(0)
        j = pl.program_id(1)

        w_cat = jnp.concatenate([wg_ref[...], wv_ref[...]], axis=0)
        conv_w_b = ((cwg_ref[...], cbg_ref[...]), (cwv_ref[...], cbv_ref[...]))
        halo = [jnp.where(i > 0, carry[j, s], 0.0) for s in range(2)]
        u_next = _dot(h_ref[0:sub, :], w_cat, NT)
        for a in range(0, tm, sub):
            u16 = u_next.astype(BF16)
            if a + sub < tm:
                u_next = _dot(h_ref[a + sub:a + 2 * sub, :], w_cat, NT)
            ug_ref[a:a + sub, :] = u16[:, :tn]
            uv_ref[a:a + sub, :] = u16[:, tn:]
            c = []
            for s, (cw, cb) in enumerate(conv_w_b):
                u = u16[:, s * tn:(s + 1) * tn].astype(F32)
                ext = jnp.concatenate([halo[s], u], axis=0)
                c.append(_conv(_shifted(ext, F32_ROWS), cw, cb))
                halo[s] = u[sub - F32_ROWS:]
            cg_ref[a:a + sub, :] = c[0].astype(BF16)
            cv_ref[a:a + sub, :] = c[1].astype(BF16)
            gelu, _ = _gelu_parts(c[0])
            y_ref[a:a + sub, :] = (gelu * c[1]).astype(BF16)
        for s in range(2):
            carry[j, s] = halo[s]

    tile = pl.BlockSpec((tm, tn), lambda i, j: (i, j))
    vec = lambda rows, off: pl.BlockSpec((rows, tn), lambda i, j: (0, j + off))
    return pl.pallas_call(
        body, name=name, grid=(S // tm, nj),
        in_specs=[pl.BlockSpec((tm, K), lambda i, j: (i, 0)),
                  pl.BlockSpec((tn, K), lambda i, j: (j, 0)), pl.BlockSpec((tn, K), lambda i, j: (j + nj, 0)),
                  vec(3, 0), vec(3, nj), vec(1, 0), vec(1, nj)],
        out_specs=[tile] * 5,
        out_shape=[jax.ShapeDtypeStruct((S, F), BF16)] * 5,
        scratch_shapes=[pltpu.VMEM((nj, 2, F32_ROWS, tn), F32)],
        compiler_params=_cp("arbitrary", "arbitrary"),
    )(h, w_up_t, w_up_t, conv_w, conv_w, conv_b, conv_b)


def _ffn_glu_bwd(u_g, u_v, c_g, c_v, df, w_down, h, conv_w, *, name, tm=2048, tn=256, sub=256):
    S = u_g.shape[0]
    F = D_FF
    D = df.shape[1]
    nj = F // tn
    nt = S // tm

    def body(ug_ref, uv_ref, cg_ref, cgn_ref, cv_ref, cvn_ref, df_ref, dfn_ref, wd_ref, h_ref, wg_ref, wv_ref,
             dug_ref, duv_ref, gug_ref, guv_ref, gd_ref, gwg_ref, gwv_ref, gbg_ref, gbv_ref,
             acc_u, acc_d):
        i = pl.program_id(1)

        @pl.when(i == 0)
        def _():
            for r in (gwg_ref, gwv_ref, gbg_ref, gbv_ref, acc_u, acc_d):
                r[...] = jnp.zeros_like(r)

        wg, wv = wg_ref[...], wv_ref[...]
        wd = wd_ref[...]
        dfn = jnp.where(i < nt - 1, dfn_ref[...], jnp.zeros_like(dfn_ref))
        n_ext = sub + HALO

        def ahead(dc):
            return dc[:sub], pltpu.roll(dc, n_ext - 1, 0)[:sub], pltpu.roll(dc, n_ext - 2, 0)[:sub]

        def ext(ref, nxt, a):
            b = a + sub
            return jnp.concatenate([ref[a:b, :], ref[b:b + HALO, :] if b < tm else nxt], axis=0)

        dy_next = _dot(ext(df_ref, dfn, 0), wd, NT)
        for a in range(0, tm, sub):
            b = a + sub
            dy_ext = dy_next
            if b < tm:
                dy_next = _dot(ext(df_ref, dfn, b), wd, NT)
            cg = ext(cg_ref, cgn_ref[...], a).astype(F32)
            cv = ext(cv_ref, cvn_ref[...], a).astype(F32)
            df_sub = df_ref[a:b, :]
            gelu, dgelu = _gelu_parts(cg)
            dcs_g = ahead(dy_ext * cv * dgelu)
            dcs_v = ahead(dy_ext * gelu)
            du_g = (dcs_g[0] * wg[2:3] + dcs_g[1] * wg[1:2] + dcs_g[2] * wg[0:1]).astype(BF16)
            du_v = (dcs_v[0] * wv[2:3] + dcs_v[1] * wv[1:2] + dcs_v[2] * wv[0:1]).astype(BF16)
            dug_ref[a:b, :] = du_g
            duv_ref[a:b, :] = du_v
            acc_u[...] += _dot(jnp.concatenate([du_g, du_v], axis=1), h_ref[a:b, :], TN)
            acc_d[...] += _dot((gelu[:sub] * cv[:sub]).astype(BF16), df_sub, TN)
            for dcs, u_ref, gw_ref, gb_ref in ((dcs_g, ug_ref, gwg_ref, gbg_ref), (dcs_v, uv_ref, gwv_ref, gbv_ref)):
                u = u_ref[a:b, :].astype(F32)
                gb_ref[...] += jnp.sum(dcs[0], axis=0, keepdims=True)
                for k in range(3):
                    gw_ref[k:k + 1, :] += jnp.sum(dcs[2 - k] * u, axis=0, keepdims=True)

        @pl.when(i == nt - 1)
        def _():
            gug_ref[...] = acc_u[:tn, :].astype(BF16)
            guv_ref[...] = acc_u[tn:, :].astype(BF16)
            gd_ref[...] = acc_d[...].astype(BF16)

    per = tm // HALO
    nh = S // HALO
    hnext = lambda i: jnp.minimum((i + 1) * per, nh - 1)
    tile = pl.BlockSpec((tm, tn), lambda j, i: (i, j))
    hn = pl.BlockSpec((HALO, tn), lambda j, i: (hnext(i), j))
    vec = lambda rows, off: pl.BlockSpec((rows, tn), lambda j, i: (0, j + off))
    wide = pl.BlockSpec((tm, D), lambda j, i: (i, 0))
    wrow = pl.BlockSpec((tn, D), lambda j, i: (j, 0))
    return pl.pallas_call(
        body, name=name, grid=(nj, nt),
        in_specs=[tile, tile, tile, hn, tile, hn, wide, pl.BlockSpec((HALO, D), lambda j, i: (hnext(i), 0)),
                  wrow, wide, vec(3, 0), vec(3, nj)],
        out_specs=[tile, tile, wrow, wrow, wrow, vec(3, 0), vec(3, 0), vec(1, 0), vec(1, 0)],
        out_shape=[jax.ShapeDtypeStruct((S, F), BF16), jax.ShapeDtypeStruct((S, F), BF16),
                   jax.ShapeDtypeStruct((F, D), BF16), jax.ShapeDtypeStruct((F, D), BF16),
                   jax.ShapeDtypeStruct((F, D), BF16),
                   jax.ShapeDtypeStruct((3, F), F32), jax.ShapeDtypeStruct((3, F), F32),
                   jax.ShapeDtypeStruct((1, F), F32), jax.ShapeDtypeStruct((1, F), F32)],
        scratch_shapes=[pltpu.VMEM((2 * tn, D), F32), pltpu.VMEM((tn, D), F32)],
        compiler_params=_cp("parallel", "arbitrary"),
    )(u_g, u_v, c_g, c_g, c_v, c_v, df, df, w_down, h, conv_w, conv_w)


def _sum_partials(parts, *, name, tr):
    _, R, C = parts.shape

    def body(p_ref, o_ref):
        tot = p_ref[0].astype(F32)
        for j in range(1, N_DEV):
            tot = tot + p_ref[j].astype(F32)
        o_ref[...] = tot

    return pl.pallas_call(
        body, name=name, grid=(R // tr,),
        in_specs=[pl.BlockSpec((N_DEV, tr, C), lambda i: (0, i, 0))],
        out_specs=pl.BlockSpec((tr, C), lambda i: (i, 0)),
        out_shape=jax.ShapeDtypeStruct((R, C), F32),
        compiler_params=_cp("parallel"),
    )(parts)


def _adamw(w, g, m, v, *, name, tr):
    R, C = w.shape
    c1 = 1.0 - ADAM_B1 ** ADAM_STEP
    c2 = 1.0 - ADAM_B2 ** ADAM_STEP

    def body(w_ref, g_ref, m_ref, v_ref, d_ref, nm_ref, nv_ref):
        g = g_ref[...]
        nm = ADAM_B1 * m_ref[...] + (1.0 - ADAM_B1) * g
        nv = ADAM_B2 * v_ref[...] + (1.0 - ADAM_B2) * (g * g)
        d_ref[...] = -ADAM_LR * ((nm / c1) / (jnp.sqrt(nv / c2) + ADAM_EPS) + ADAM_WD * w_ref[...])
        nm_ref[...] = nm
        nv_ref[...] = nv

    spec = pl.BlockSpec((tr, C), lambda i: (i, 0))
    return pl.pallas_call(
        body, name=name, grid=(R // tr,), in_specs=[spec] * 4, out_specs=[spec] * 3,
        out_shape=[jax.ShapeDtypeStruct((R, C), F32)] * 3,
        compiler_params=_cp("parallel"),
    )(w, g, m, v)


def _mesh_pos():
    return lax.axis_index("x"), lax.axis_index("y"), lax.axis_index("c")


def _gather_phases(x_refs, out_refs, send_sems, recv_sems, local_sems):
    x, y, c = _mesh_pos()
    me, sibling = (x, y, c), (x, y, 1 - c)
    chips = [(1 - x, y), (x, 1 - y), (1 - x, 1 - y)]
    arrays = range(len(x_refs))

    def slot(a, px, py, pc):
        return out_refs[a].at[4 * px + 2 * py + pc]

    def copy(a, k, block, to, own=False):
        return pltpu.make_async_remote_copy(
            src_ref=x_refs[a] if own else slot(a, *block), dst_ref=slot(a, *block),
            send_sem=send_sems.at[a, k], recv_sem=recv_sems.at[a, k], device_id=to, device_id_type=MESH)

    mine = [pltpu.make_async_copy(x_refs[a], slot(a, *me), local_sems.at[a]) for a in arrays]
    first = [copy(a, 0, me, sibling, own=True) for a in arrays]
    first += [copy(a, 1 + j, me, (*chip, c), own=True) for j, chip in enumerate(chips) for a in arrays]
    passed = [[copy(a, 4 + j, (*chip, c), sibling) for a in arrays] for j, chip in enumerate(chips)]

    def start():
        for cp in mine + first:
            cp.start()

    def forward():
        for j, chip in enumerate(chips):
            for a in arrays:
                copy(a, 1 + j, (*chip, c), me).wait_recv()
                passed[j][a].start()

    def finish():
        for a in arrays:
            copy(a, 0, sibling, me).wait_recv()
            for j, chip in enumerate(chips):
                copy(a, 4 + j, (*chip, 1 - c), me).wait_recv()
        for cp in first + [cp for row in passed for cp in row]:
            cp.wait_send()
        for cp in mine:
            cp.wait()

    return start, forward, finish


def _gather_sems(n):
    return [pltpu.SemaphoreType.DMA((n, 7)), pltpu.SemaphoreType.DMA((n, 7)), pltpu.SemaphoreType.DMA((n,))]


def _gathered_shapes(blocks):
    return [jax.ShapeDtypeStruct((N_DEV,) + b.shape, b.dtype) for b in blocks]


def _all_reduce_small(block, *, name):
    def body(x_ref, all_ref, sum_ref, *sems):
        for phase in _gather_phases([x_ref], [all_ref], *sems):
            phase()
        tot = all_ref[0]
        for j in range(1, N_DEV):
            tot = tot + all_ref[j]
        sum_ref[...] = tot

    return pl.pallas_call(
        body, name=name, in_specs=[VMEM], out_specs=[VMEM, VMEM],
        out_shape=[jax.ShapeDtypeStruct((N_DEV,) + block.shape, block.dtype),
                   jax.ShapeDtypeStruct(block.shape, block.dtype)],
        scratch_shapes=_gather_sems(1),
        compiler_params=pltpu.CompilerParams(vmem_limit_bytes=V7X_VMEM_LIMIT),
    )(block)[1]


def _exchange_phases(g_refs, r_refs, send_sems, recv_sems, local_sems):
    x, y, c = _mesh_pos()
    me = 4 * x + 2 * y + c
    owns, remote = [], []
    for k, (g_ref, r_ref) in enumerate(zip(g_refs, r_refs)):
        rows = g_ref.shape[0] // N_DEV
        owns.append(pltpu.make_async_copy(g_ref.at[pl.ds(me * rows, rows)], r_ref.at[me], local_sems.at[k]))
        for p in range(1, N_DEV):
            px, py, pc = x ^ (p >> 2), y ^ ((p >> 1) & 1), c ^ (p & 1)
            peer = 4 * px + 2 * py + pc
            link = dict(send_sem=send_sems.at[k, p], recv_sem=recv_sems.at[k, p],
                        device_id=(px, py, pc), device_id_type=MESH)
            src = g_ref.at[pl.ds(peer * rows, rows)]
            send = pltpu.make_async_remote_copy(src_ref=src, dst_ref=r_ref.at[me], **link)
            arrival = pltpu.make_async_remote_copy(src_ref=src, dst_ref=r_ref.at[peer], **link)
            remote.append((send, arrival))

    def start():
        for own in owns:
            own.start()
        for send, _ in remote:
            send.start()

    def finish():
        for _, arrival in remote:
            arrival.wait_recv()
        for send, _ in remote:
            send.wait_send()
        for own in owns:
            own.wait()

    return start, finish


def _exchange_buffers(grads):
    n = len(grads)
    shapes = [jax.ShapeDtypeStruct((N_DEV, g.shape[0] // N_DEV, g.shape[1]), g.dtype) for g in grads]
    sems = [pltpu.SemaphoreType.DMA((n, N_DEV)), pltpu.SemaphoreType.DMA((n, N_DEV)),
            pltpu.SemaphoreType.DMA((n,))]
    return shapes, sems


def _unpack_gathered(gathered):
    w_out, w_up_t, w_down = (g.reshape(-1, D_MODEL) for g in gathered[:3])
    width = 2 * D_FF // N_DEV
    conv_w = jnp.transpose(gathered[3][:, :3, :width], (1, 0, 2)).reshape(3, 2 * D_FF)
    return w_out, w_up_t, w_down, conv_w


def _rest_payload(w_out, w_up, w_down, conv_w):
    rows, cols = conv_w.shape
    conv_w = jnp.pad(conv_w, ((0, (-rows) % F32_ROWS), (0, (-cols) % LANES)))
    return [w_out.astype(BF16), w_up.T.astype(BF16), w_down.astype(BF16), conv_w]


def _device_step(x, target, g_mix_pre, w_in_t_block, rest_payload, pool_w, pool_scale, g_mix_post, g_ffn_pre,
                 conv_b, g_ffn_post):
    h1, w_in_t = _rms_norm_gather(x, g_mix_pre, w_in_t_block, name="rms_mix_pre")
    w_in_t = w_in_t.reshape(-1, D_MODEL)
    proj = _matmul(h1, w_in_t, trans_b=True, out_dtype=F32, tm=1024, tn=512, name="proj")
    attn, lse, attn16, gathered = _attn_fwd(proj, rest_payload, name="attn_fwd")
    w_out, w_up_t, w_down, conv_w = _unpack_gathered(gathered)
    pool = _pool_fwd(proj, 3, pool_w, pool_scale, name="pool_fwd")
    mixed, x2, h2 = _mix_out(attn16, pool, w_out, x, g_mix_post, g_ffn_pre, name="mix_out")
    u_g, u_v, c_g, c_v, y = _ffn_up_glu(h2, w_up_t, conv_w, conv_b, name="ffn_up_glu")
    df, d_out, loss_blk, gg_ffn_post = _ffn_out(y, w_down, x2, target, g_ffn_post, name="ffn_out")
    du_g, du_v, gw_up_g, gw_up_v, gw_down, gcw_g, gcw_v, gcb_g, gcb_v = _ffn_glu_bwd(
        u_g, u_v, c_g, c_v, df, w_down, h2, conv_w, name="ffn_glu_bwd")
    gw_up_t = jnp.concatenate([gw_up_g, gw_up_v], axis=0)
    dx2, gg_ffn_pre, dmixed, gg_mix_post = _dgrad_norm(
        [du_g, du_v], w_up_t, d_out, x2, g_ffn_pre, (mixed, g_mix_post), [], name="ffn_up_dgrad")
    gw_out = jnp.concatenate([_matmul_tn(attn16, dmixed, ta=512, ts=1024, name="grad_w_out_attn"),
                              _matmul_tn(pool, dmixed, ta=512, ts=1024, name="grad_w_out_pool")], axis=0)
    dcat = _matmul(dmixed, w_out, trans_b=True, out_dtype=F32, tm=512, tn=1024, name="mix_out_dgrad")
    d_pool_in, g_pool_w, g_pool_scale = _pool_bwd(proj, 3, dcat, 1, pool_w, pool_scale, name="pool_bwd")
    dqkv, (r_out, r_up_t, r_down) = _attn_bwd(proj, dcat, attn, lse, [gw_out, gw_up_t, gw_down], name="attn_bwd")
    dproj = list(dqkv) + [d_pool_in]
    gw_in_t = jnp.concatenate([_matmul_tn(a, h1, ta=512, ts=1024, name=f"grad_w_in_{k}")
                               for k, a in enumerate(dproj)], axis=0)
    grad_x, gg_mix_pre, (r_in_t,) = _dgrad_norm(dproj, w_in_t, dx2, x, g_mix_pre, None, [gw_in_t], name="proj_dgrad")
    g_conv_w = jnp.concatenate([gcw_g, gcw_v], axis=1)
    g_conv_b = jnp.concatenate([gcb_g, gcb_v], axis=1)
    received = (r_in_t, r_out, r_up_t, r_down)
    small = dict(g_mix_pre=gg_mix_pre, g_mix_post=gg_mix_post, g_ffn_pre=gg_ffn_pre, g_ffn_post=gg_ffn_post,
                 pool_scale=g_pool_scale, conv_b=g_conv_b, pool_w=g_pool_w, conv_w=g_conv_w)
    return loss_blk, grad_x, received, small


_SMALL = ("g_mix_pre", "g_mix_post", "g_ffn_pre", "g_ffn_post", "pool_scale", "conv_b", "pool_w")
LANES = 128


def _pack_rows(arrays):
    parts = []
    for a in arrays:
        a2 = a.reshape(-1, LANES)
        parts.append(jnp.pad(a2, ((0, (-a2.shape[0]) % 8), (0, 0))))
    return jnp.concatenate(parts, axis=0)


def _unpack_rows(packed, shapes):
    out, row = [], 0
    for shape in shapes:
        rows = math.prod(shape) // LANES
        out.append(packed[row:row + rows].reshape(shape))
        row += -(-rows // 8) * 8
    return out


def kernel(x, g_mix_pre, w_in, pool_w, pool_scale, w_out, g_mix_post, g_ffn_pre, w_up, conv_w, conv_b, w_down, g_ffn_post, loss_target, m_g_mix_pre, m_w_in, m_pool_w, m_pool_scale, m_w_out, m_g_mix_post, m_g_ffn_pre, m_w_up, m_conv_w, m_conv_b, m_w_down, m_g_ffn_post, v_g_mix_pre, v_w_in, v_pool_w, v_pool_scale, v_w_out, v_g_mix_post, v_g_ffn_pre, v_w_up, v_conv_w, v_conv_b, v_w_down, v_g_ffn_post):
    me = 4 * lax.axis_index("x") + 2 * lax.axis_index("y") + lax.axis_index("c")
    loss_blk, grad_x, recv, small = _device_step(
        x[0], loss_target[0], g_mix_pre, w_in[0].T.astype(BF16),
        _rest_payload(w_out[0], w_up[0], w_down[0], conv_w[0]),
        pool_w[0], pool_scale, g_mix_post, g_ffn_pre, conv_b, g_ffn_post)

    g_in_t, g_out, g_up_t, g_down = (
        _sum_partials(r, name=f"sum_partials_{k}", tr=r.shape[1] // 2) for k, r in enumerate(recv))
    grads = {"w_in": g_in_t.T, "w_out": g_out, "w_up": g_up_t.T, "w_down": g_down}

    given = dict(g_mix_pre=g_mix_pre, g_mix_post=g_mix_post, g_ffn_pre=g_ffn_pre, g_ffn_post=g_ffn_post,
                 pool_scale=pool_scale, conv_b=conv_b, pool_w=pool_w)
    small_shapes = [given[k].shape for k in _SMALL]
    total = _all_reduce_small(_pack_rows([small[k] for k in _SMALL] + [small["conv_w"], loss_blk]),
                              name="all_reduce_small")
    *small_grads, g_conv_w_all, loss_all = _unpack_rows(total, small_shapes + [(3, 2 * D_FF), loss_blk.shape])
    loss = loss_all[0, 0]
    grads.update(zip(_SMALL, small_grads))
    width = 2 * D_FF // N_DEV
    grads["conv_w"] = lax.dynamic_slice_in_dim(g_conv_w_all, me * width, width, axis=1)[None]

    weights = dict(g_mix_pre=g_mix_pre, w_in=w_in, pool_w=pool_w, pool_scale=pool_scale, w_out=w_out,
                   g_mix_post=g_mix_post, g_ffn_pre=g_ffn_pre, w_up=w_up, conv_w=conv_w, conv_b=conv_b,
                   w_down=w_down, g_ffn_post=g_ffn_post)
    m_in = dict(g_mix_pre=m_g_mix_pre, w_in=m_w_in, pool_w=m_pool_w, pool_scale=m_pool_scale, w_out=m_w_out,
                g_mix_post=m_g_mix_post, g_ffn_pre=m_g_ffn_pre, w_up=m_w_up, conv_w=m_conv_w, conv_b=m_conv_b,
                w_down=m_w_down, g_ffn_post=m_g_ffn_post)
    v_in = dict(g_mix_pre=v_g_mix_pre, w_in=v_w_in, pool_w=v_pool_w, pool_scale=v_pool_scale, w_out=v_w_out,
                g_mix_post=v_g_mix_post, g_ffn_pre=v_g_ffn_pre, w_up=v_w_up, conv_w=v_conv_w, conv_b=v_conv_b,
                w_down=v_w_down, g_ffn_post=v_g_ffn_post)
    delta, new_m, new_v = {}, {}, {}
    for k in ("w_in", "w_out", "w_up", "w_down"):
        g = grads[k]
        d, nm, nv = _adamw(weights[k][0], g, m_in[k][0], v_in[k][0], name=f"adamw_{k}", tr=g.shape[0] // 2)
        grads[k], delta[k], new_m[k], new_v[k] = g[None], d[None], nm[None], nv[None]
    d, nm, nv = _adamw(weights["conv_w"][0], grads["conv_w"][0], m_in["conv_w"][0], v_in["conv_w"][0],
                       name="adamw_conv_w", tr=3)
    delta["conv_w"], new_m["conv_w"], new_v["conv_w"] = d[None], nm[None], nv[None]
    packed_w = _pack_rows([weights[k] for k in _SMALL])
    small_rows = packed_w.shape[0]
    d, nm, nv = _adamw(packed_w, total[:small_rows], _pack_rows([m_in[k] for k in _SMALL]),
                       _pack_rows([v_in[k] for k in _SMALL]), name="adamw_small", tr=small_rows)
    for k, dk, mk, vk in zip(_SMALL, _unpack_rows(d, small_shapes), _unpack_rows(nm, small_shapes),
                             _unpack_rows(nv, small_shapes)):
        delta[k], new_m[k], new_v[k] = dk, mk, vk

    order = ("g_mix_pre", "w_in", "pool_w", "pool_scale", "w_out", "g_mix_post", "g_ffn_pre", "w_up",
             "conv_w", "conv_b", "w_down", "g_ffn_post")
    return (loss, grad_x[None], *[grads[k] for k in order], *[delta[k] for k in order],
            *[new_m[k] for k in order], *[new_v[k] for k in order])
```

```python
import functools
import math

import jax
import jax.numpy as jnp
from jax import lax
from jax.experimental import pallas as pl
from jax.experimental.pallas import tpu as pltpu

F32 = jnp.float32
BF16 = jnp.bfloat16

D_MODEL = 1024
ATTN_WIDTH = 512
N_HEADS = 8
HEAD_DIM = 64
DILATIONS = (1, 4, 16)
BLOCK = 128
POOL_WIDTH = 512
POOL_WINDOWS = (2, 4, 8, 16)
POOL_GROUP_DIM = 128
D_FF = 2816
EPS = 1e-6
NEG_INF = -1e30
SCALE = HEAD_DIM ** -0.5

ADAM_LR = 0.001
ADAM_B1 = 0.9
ADAM_B2 = 0.999
ADAM_EPS = 1e-08
ADAM_WD = 0.01
ADAM_STEP = 10

N_DEV = 8
HALO = 16
V7X_VMEM_LIMIT = 56 * 1024 * 1024

MESH = pl.DeviceIdType.MESH
ANY = pl.BlockSpec(memory_space=pl.ANY)
VMEM = pl.BlockSpec(memory_space=pltpu.VMEM)

NT = (((1,), (1,)), ((), ()))
NN = (((1,), (0,)), ((), ()))
TN = (((0,), (0,)), ((), ()))


def _cp(*sem):
    return pltpu.CompilerParams(dimension_semantics=sem, vmem_limit_bytes=V7X_VMEM_LIMIT)


def _dot(a, b, dn):
    return lax.dot_general(a, b, dn, preferred_element_type=F32)


def _rms_bwd(xin, g, dy):
    r = lax.rsqrt(jnp.mean(xin * xin, axis=-1, keepdims=True) + EPS)
    xh = xin * r
    gdy = g * dy
    dx = r * (gdy - xh * jnp.mean(gdy * xh, axis=-1, keepdims=True))
    dg = jnp.sum(dy * xh, axis=0, keepdims=True)
    return dx, dg


def _rms_norm_gather(x, g, block, *, name, tm=512):
    S, D = x.shape
    nt = S // tm

    def body(x_ref, g_ref, blk_ref, o_ref, all_ref, *sems):
        i = pl.program_id(0)
        start, forward, finish = _gather_phases([blk_ref], [all_ref], *sems)
        pl.when(i == 0)(start)
        xv = x_ref[...]
        r = lax.rsqrt(jnp.mean(xv * xv, axis=-1, keepdims=True) + EPS)
        o_ref[...] = (xv * r * g_ref[...]).astype(BF16)
        pl.when(i == (2 * nt) // 3)(forward)
        pl.when(i == nt - 1)(finish)

    return pl.pallas_call(
        body, name=name, grid=(nt,),
        in_specs=[pl.BlockSpec((tm, D), lambda i: (i, 0)), pl.BlockSpec((1, D), lambda i: (0, 0)), ANY],
        out_specs=[pl.BlockSpec((tm, D), lambda i: (i, 0)), ANY],
        out_shape=[jax.ShapeDtypeStruct((S, D), BF16)] + _gathered_shapes([block]),
        scratch_shapes=_gather_sems(1),
        compiler_params=_cp("arbitrary"),
    )(x, g, block)


def _matmul(a, b, *, trans_b, out_dtype, tm, tn, name):
    M, K = a.shape
    N = b.shape[0] if trans_b else b.shape[1]
    dn = NT if trans_b else NN

    def body(a_ref, b_ref, o_ref):
        o_ref[...] = _dot(a_ref[...], b_ref[...], dn).astype(out_dtype)

    b_spec = (pl.BlockSpec((tn, K), lambda i, j: (j, 0)) if trans_b
              else pl.BlockSpec((K, tn), lambda i, j: (0, j)))
    return pl.pallas_call(
        body, name=name, grid=(M // tm, N // tn),
        in_specs=[pl.BlockSpec((tm, K), lambda i, j: (i, 0)), b_spec],
        out_specs=pl.BlockSpec((tm, tn), lambda i, j: (i, j)),
        out_shape=jax.ShapeDtypeStruct((M, N), out_dtype),
        compiler_params=_cp("parallel", "parallel"),
    )(a, b)


def _matmul_tn(a, b, *, ta, ts, name):
    S, Ka = a.shape
    Nb = b.shape[1]
    ns = S // ts

    def body(a_ref, b_ref, o_ref, acc):
        s = pl.program_id(1)

        @pl.when(s == 0)
        def _():
            acc[...] = jnp.zeros_like(acc)

        acc[...] += _dot(a_ref[...], b_ref[...], TN)

        @pl.when(s == ns - 1)
        def _():
            o_ref[...] = acc[...].astype(BF16)

    return pl.pallas_call(
        body, name=name, grid=(Ka // ta, ns),
        in_specs=[pl.BlockSpec((ts, ta), lambda i, s: (s, i)), pl.BlockSpec((ts, Nb), lambda i, s: (s, 0))],
        out_specs=pl.BlockSpec((ta, Nb), lambda i, s: (i, 0)),
        out_shape=jax.ShapeDtypeStruct((Ka, Nb), BF16),
        scratch_shapes=[pltpu.VMEM((ta, Nb), F32)],
        compiler_params=_cp("parallel", "arbitrary"),
    )(a, b)


def _mix_out(attn, pool, w_out, x, g_post, g_next, *, name, tm=256):
    S, K = attn.shape
    D = w_out.shape[1]

    def body(a_ref, p_ref, w_ref, x_ref, gp_ref, gn_ref, mixed_ref, x2_ref, h2_ref):
        mixed = _dot(a_ref[...], w_ref[:K, :], NN) + _dot(p_ref[...], w_ref[K:, :], NN)
        r = lax.rsqrt(jnp.mean(mixed * mixed, axis=-1, keepdims=True) + EPS)
        x2 = x_ref[...] + mixed * r * gp_ref[...]
        r2 = lax.rsqrt(jnp.mean(x2 * x2, axis=-1, keepdims=True) + EPS)
        mixed_ref[...] = mixed
        x2_ref[...] = x2
        h2_ref[...] = (x2 * r2 * gn_ref[...]).astype(BF16)

    row = lambda i: (i, 0)
    fix = lambda i: (0, 0)
    return pl.pallas_call(
        body, name=name, grid=(S // tm,),
        in_specs=[pl.BlockSpec((tm, K), row), pl.BlockSpec((tm, K), row), pl.BlockSpec((2 * K, D), fix),
                  pl.BlockSpec((tm, D), row), pl.BlockSpec((1, D), fix), pl.BlockSpec((1, D), fix)],
        out_specs=[pl.BlockSpec((tm, D), row)] * 3,
        out_shape=[jax.ShapeDtypeStruct((S, D), F32), jax.ShapeDtypeStruct((S, D), F32),
                   jax.ShapeDtypeStruct((S, D), BF16)],
        compiler_params=_cp("parallel"),
    )(attn, pool, w_out, x, g_post, g_next)


def _ffn_out(y, w_down, x2, target, g_post, *, name, tm=512):
    S, K = y.shape
    D = w_down.shape[1]

    def body(y_ref, w_ref, x2_ref, t_ref, g_ref, df_ref, dout_ref, loss_ref, gg_ref):
        i = pl.program_id(0)

        @pl.when(i == 0)
        def _():
            loss_ref[...] = jnp.zeros_like(loss_ref)
            gg_ref[...] = jnp.zeros_like(gg_ref)

        f = _dot(y_ref[...], w_ref[...], NN)
        g = g_ref[...]
        r = lax.rsqrt(jnp.mean(f * f, axis=-1, keepdims=True) + EPS)
        out = x2_ref[...] + f * r * g
        err = out - t_ref[...]
        dy = err * (1.0 / D)
        df, dg = _rms_bwd(f, g, dy)
        df_ref[...] = df.astype(BF16)
        dout_ref[...] = dy
        gg_ref[...] += dg
        loss_ref[...] += 0.5 * jnp.sum(jnp.mean(err * err, axis=-1, keepdims=True))

    row = lambda i: (i, 0)
    fix = lambda i: (0, 0)
    return pl.pallas_call(
        body, name=name, grid=(S // tm,),
        in_specs=[pl.BlockSpec((tm, K), row), pl.BlockSpec((K, D), fix), pl.BlockSpec((tm, D), row),
                  pl.BlockSpec((tm, D), row), pl.BlockSpec((1, D), fix)],
        out_specs=[pl.BlockSpec((tm, D), row), pl.BlockSpec((tm, D), row),
                   pl.BlockSpec((8, 128), fix), pl.BlockSpec((1, D), fix)],
        out_shape=[jax.ShapeDtypeStruct((S, D), BF16), jax.ShapeDtypeStruct((S, D), F32),
                   jax.ShapeDtypeStruct((8, 128), F32), jax.ShapeDtypeStruct((1, D), F32)],
        compiler_params=_cp("arbitrary"),
    )(y, w_down, x2, target, g_post)


def _dgrad_norm(a_list, w, resid, xin, g, second, exchange, *, name, tm=512):
    S, Kp = a_list[0].shape
    na = len(a_list)
    D = w.shape[1]
    nt = S // tm
    two = second is not None
    ng = len(exchange)
    recv_shapes, exchange_sems = _exchange_buffers(exchange)

    def body(*refs):
        a_refs = refs[:na]
        w_ref, r_ref, x_ref, g_ref = refs[na:na + 4]
        pos = na + 4
        if two:
            x2_ref, g2_ref = refs[pos:pos + 2]
            pos += 2
        g_refs = refs[pos:pos + ng]
        pos += ng
        dx_ref, gg_ref = refs[pos:pos + 2]
        pos += 2
        if two:
            d2_ref, gg2_ref = refs[pos:pos + 2]
            pos += 2
        r_refs = refs[pos:pos + ng]
        pos += ng
        i = pl.program_id(0)
        if ng:
            start, finish = _exchange_phases(g_refs, r_refs, *refs[pos:])
            pl.when(i == 0)(start)

        @pl.when(i == 0)
        def _():
            gg_ref[...] = jnp.zeros_like(gg_ref)
            if two:
                gg2_ref[...] = jnp.zeros_like(gg2_ref)

        dh = functools.reduce(jnp.add, [_dot(a_refs[q][...], w_ref[q * Kp:(q + 1) * Kp, :], NN) for q in range(na)])
        d1, dg1 = _rms_bwd(x_ref[...], g_ref[...], dh)
        dx = r_ref[...] + d1
        dx_ref[...] = dx
        gg_ref[...] += dg1
        if two:
            d2, dg2 = _rms_bwd(x2_ref[...], g2_ref[...], dx)
            d2_ref[...] = d2.astype(BF16)
            gg2_ref[...] += dg2
        if ng:
            pl.when(i == nt - 1)(finish)

    row = lambda i: (i, 0)
    fix = lambda i: (0, 0)
    in_specs = [pl.BlockSpec((tm, Kp), row)] * na + [
        pl.BlockSpec((na * Kp, D), fix, pipeline_mode=pl.Buffered(1)), pl.BlockSpec((tm, D), row),
        pl.BlockSpec((tm, D), row), pl.BlockSpec((1, D), fix)]
    args = list(a_list) + [w, resid, xin, g]
    out_specs = [pl.BlockSpec((tm, D), row), pl.BlockSpec((1, D), fix)]
    out_shape = [jax.ShapeDtypeStruct((S, D), F32), jax.ShapeDtypeStruct((1, D), F32)]
    if two:
        in_specs += [pl.BlockSpec((tm, D), row), pl.BlockSpec((1, D), fix)]
        args += list(second)
        out_specs += [pl.BlockSpec((tm, D), row), pl.BlockSpec((1, D), fix)]
        out_shape += [jax.ShapeDtypeStruct((S, D), BF16), jax.ShapeDtypeStruct((1, D), F32)]
    n_plain = len(out_shape)
    out = pl.pallas_call(
        body, name=name, grid=(nt,), in_specs=in_specs + [ANY] * ng, out_specs=out_specs + [ANY] * ng,
        out_shape=out_shape + recv_shapes, scratch_shapes=exchange_sems if ng else [],
        compiler_params=_cp("arbitrary"),
    )(*args, *exchange)
    return (*out[:n_plain], out[n_plain:]) if ng else out


def _band_mask(first_block):
    qi = lax.broadcasted_iota(jnp.int32, (BLOCK, 2 * BLOCK), 0)
    ki = lax.broadcasted_iota(jnp.int32, (BLOCK, 2 * BLOCK), 1)
    first_key = jnp.where(first_block, BLOCK, 0)
    return (ki >= qi) & (ki <= qi + BLOCK) & (ki >= first_key)


def _lane_masks():
    lane = lax.broadcasted_iota(jnp.int32, (1, 2 * HEAD_DIM), 1)
    return (lane < HEAD_DIM, lane >= HEAD_DIM)


CHUNK = BLOCK * max(DILATIONS)
SLAB = 2 * HEAD_DIM
N_SLABS = ATTN_WIDTH // SLAB


def _unit_rows(d, b):
    def rows(r):
        start = r + BLOCK * d * b
        return pl.ds(start, BLOCK, stride=d) if d > 1 else pl.ds(start, BLOCK)
    return rows


def _attn_units():
    for p, d in enumerate(DILATIONS):
        nbc = CHUNK // (BLOCK * d)
        for b in range(nbc):
            for r in range(d):
                yield p, d, b, r, nbc


def _attn_in_specs(nc, n_cur):
    prev = lambda c: jnp.maximum(jnp.minimum(c, nc - 1) - 1, 0)
    cur = lambda c: jnp.minimum(c, nc - 1)
    blk = lambda f: pl.BlockSpec((CHUNK, SLAB), f)
    specs = [blk(lambda h, c: (cur(c), h)),
             blk(lambda h, c: (prev(c), N_SLABS + h)), blk(lambda h, c: (cur(c), N_SLABS + h)),
             blk(lambda h, c: (prev(c), 2 * N_SLABS + h)), blk(lambda h, c: (cur(c), 2 * N_SLABS + h))]
    return specs + [blk(lambda h, c: (cur(c), h))] * n_cur


def _attn_fwd(proj, payload, *, name):
    S = proj.shape[0]
    nc = S // CHUNK
    n = len(DILATIONS)
    npay = len(payload)
    n_steps = N_SLABS * nc

    def body(*refs):
        q_ref, kp_ref, kc_ref, vp_ref, vc_ref = refs[:5]
        pay_refs = refs[5:5 + npay]
        attn_ref, lse_ref, attn16_ref = refs[5 + npay:8 + npay]
        all_refs = refs[8 + npay:8 + 2 * npay]
        scr = refs[8 + 2 * npay:]
        o_scr, l_scr = scr[:n], scr[n:2 * n]
        start, forward, finish = _gather_phases(pay_refs, all_refs, *scr[2 * n:])
        step = pl.program_id(0) * nc + pl.program_id(1)
        pl.when(step == 0)(start)
        c = pl.program_id(1)
        lms = _lane_masks()
        plain, first = (jnp.tile(_band_mask(f), (2, 1)) for f in (False, c == 0))
        def scores(unit):
            p, d, b, r, nbc = unit
            rows = _unit_rows(d, b)(r)
            prow = _unit_rows(d, (b - 1) % nbc)(r)
            kpr, vpr = (kc_ref, vc_ref) if b > 0 else (kp_ref, vp_ref)
            q = q_ref[rows, :].astype(BF16)
            kcat = jnp.concatenate([kpr[prow, :], kc_ref[rows, :]], axis=0).astype(BF16)
            vcat = jnp.concatenate([vpr[prow, :], vc_ref[rows, :]], axis=0).astype(BF16)
            q2 = jnp.concatenate([jnp.where(lm, q, jnp.zeros_like(q)) for lm in lms], axis=0) * SCALE
            return p, rows, plain if b > 0 else first, vcat, _dot(q2, kcat, NT)

        units = list(_attn_units())
        nxt = scores(units[0])
        for k in range(len(units)):
            p, rows, mask2, vcat, s = nxt
            if k + 1 < len(units):
                nxt = scores(units[k + 1])
            s = jnp.where(mask2, s, NEG_INF)
            m = jnp.max(s, axis=-1, keepdims=True)
            e = jnp.exp(s - m)
            l = jnp.sum(e, axis=-1, keepdims=True)
            o2 = _dot(e.astype(BF16), vcat, NN) / l
            lse2 = m + jnp.log(l)
            o_scr[p][rows, :] = jnp.where(lms[0], o2[:BLOCK], o2[BLOCK:])
            l_scr[p][rows, :] = jnp.where(lms[0], lse2[:BLOCK], lse2[BLOCK:])
        ls = [l_scr[p][...] for p in range(n)]
        top = functools.reduce(jnp.maximum, ls)
        es = [jnp.exp(l - top) for l in ls]
        den = functools.reduce(jnp.add, es)
        num = functools.reduce(jnp.add, [e * o_scr[p][...] for p, e in enumerate(es)])
        attn = num / den
        attn_ref[...] = attn
        attn16_ref[...] = attn.astype(BF16)
        lse_ref[...] = top + jnp.log(den)
        pl.when(step == (2 * n_steps) // 3)(forward)
        pl.when(step == n_steps - 1)(finish)

    out = pl.pallas_call(
        body, name=name, grid=(N_SLABS, nc), in_specs=_attn_in_specs(nc, 0) + [ANY] * npay,
        out_specs=[pl.BlockSpec((CHUNK, SLAB), lambda h, c: (c, h))] * 3 + [ANY] * npay,
        out_shape=[jax.ShapeDtypeStruct((S, ATTN_WIDTH), F32)] * 2 + [jax.ShapeDtypeStruct((S, ATTN_WIDTH), BF16)]
        + _gathered_shapes(payload),
        scratch_shapes=[pltpu.VMEM((CHUNK, SLAB), F32)] * (2 * n) + _gather_sems(npay),
        compiler_params=_cp("arbitrary", "arbitrary"),
    )(proj, proj, proj, proj, proj, *payload)
    return (*out[:3], out[3:])


def _attn_bwd(proj, dcat, attn, lse, grads, *, name):
    S = proj.shape[0]
    nc = S // CHUNK
    ng = len(grads)
    n = len(DILATIONS)
    recv_shapes, exchange_sems = _exchange_buffers(grads)

    def body(*refs):
        q_ref, kp_ref, kc_ref, vp_ref, vc_ref, do_ref, o_ref, lse_ref = refs[:8]
        g_refs = refs[8:8 + ng]
        dq_ref, dk_ref, dv_ref = refs[8 + ng:11 + ng]
        r_refs = refs[11 + ng:11 + 2 * ng]
        scr = refs[11 + 2 * ng:]
        dk_prev, dv_prev = scr[:2]
        delta_h, lse_h = scr[2:4], scr[4:6]
        dq_p, dk_own, dk_back, dv_own, dv_back = (scr[6 + n * k:6 + n * (k + 1)] for k in range(5))
        start, finish = _exchange_phases(g_refs, r_refs, *scr[6 + 5 * n:])
        c = pl.program_id(1)
        pl.when((pl.program_id(0) == 0) & (c == 0))(start)

        @pl.when(c == 0)
        def _():
            dk_prev[...] = jnp.zeros_like(dk_prev)
            dv_prev[...] = jnp.zeros_like(dv_prev)

        @pl.when(c < nc)
        def _():
            lms = _lane_masks()
            plain, first = (jnp.tile(_band_mask(f), (2, 1)) for f in (False, c == 0))
            prod = do_ref[...] * o_ref[...]
            lse = lse_ref[...]
            lse_other = pltpu.roll(lse, HEAD_DIM, 1)
            for h, lm in enumerate(lms):
                delta = jnp.sum(jnp.where(lm, prod, 0.0), axis=-1, keepdims=True)
                delta_h[h][...] = jnp.broadcast_to(delta, (CHUNK, SLAB))
                lse_h[h][...] = jnp.where(lm, lse, lse_other)
            wide = lambda refs, rows: jnp.tile(jnp.concatenate([r[rows, :] for r in refs], axis=0), (1, 2))
            stack = lambda f: jnp.concatenate([f(lm) for lm in lms], axis=0)

            def scores(unit):
                p, d, b, r, nbc = unit
                rows = _unit_rows(d, b)(r)
                prow = _unit_rows(d, (b - 1) % nbc)(r)
                kpr, vpr = (kc_ref, vc_ref) if b > 0 else (kp_ref, vp_ref)
                q = q_ref[rows, :].astype(BF16)
                kcat = jnp.concatenate([kpr[prow, :], kc_ref[rows, :]], axis=0).astype(BF16)
                vcat = jnp.concatenate([vpr[prow, :], vc_ref[rows, :]], axis=0).astype(BF16)
                do = do_ref[rows, :]
                q2 = stack(lambda lm: jnp.where(lm, q, jnp.zeros_like(q))) * SCALE
                do2 = stack(lambda lm: jnp.where(lm, do, 0.0)).astype(BF16)
                return dict(p=p, rows=rows, prow=prow, mask2=plain if b > 0 else first, kcat=kcat, q2=q2, do2=do2,
                            s=_dot(q2, kcat, NT), dp=_dot(do2, vcat, NT))

            units = list(_attn_units())
            nxt = scores(units[0])
            for k in range(len(units)):
                u = nxt
                if k + 1 < len(units):
                    nxt = scores(units[k + 1])
                p, rows, prow, kcat = u["p"], u["rows"], u["prow"], u["kcat"]
                e = jnp.where(u["mask2"], jnp.exp(u["s"] - wide(lse_h, rows)), 0.0)
                ds = (e * (u["dp"] - wide(delta_h, rows))).astype(BF16)
                dq = jnp.where(lms[0], _dot(ds[:BLOCK], kcat, NN), _dot(ds[BLOCK:], kcat, NN)) * SCALE
                dkc = _dot(ds, u["q2"], TN)
                dvc = _dot(e.astype(BF16), u["do2"], TN)
                dq_p[p][rows, :] = dq
                dk_own[p][rows, :] = dkc[BLOCK:]
                dv_own[p][rows, :] = dvc[BLOCK:]
                dk_back[p][prow, :] = dkc[:BLOCK]
                dv_back[p][prow, :] = dvc[:BLOCK]
            dq_ref[...] = functools.reduce(jnp.add, [r[...] for r in dq_p]).astype(BF16)
            for prev, own, back, out_ref in ((dk_prev, dk_own, dk_back, dk_ref), (dv_prev, dv_own, dv_back, dv_ref)):
                for p, d in enumerate(DILATIONS):
                    tail = CHUNK - BLOCK * d
                    prev[tail:, :] += back[p][tail:, :]
                out_ref[...] = prev[...].astype(BF16)
                prev[...] = functools.reduce(jnp.add, [r[...] for r in own])
                for p, d in enumerate(DILATIONS):
                    tail = CHUNK - BLOCK * d
                    if tail:
                        prev[:tail, :] += back[p][:tail, :]

        @pl.when(c == nc)
        def _():
            dk_ref[...] = dk_prev[...].astype(BF16)
            dv_ref[...] = dv_prev[...].astype(BF16)

        pl.when((pl.program_id(0) == N_SLABS - 1) & (c == nc))(finish)

    blk = lambda f: pl.BlockSpec((CHUNK, SLAB), f)
    late = lambda h, c: (jnp.maximum(c - 1, 0), h)
    out = pl.pallas_call(
        body, name=name, grid=(N_SLABS, nc + 1), in_specs=_attn_in_specs(nc, 3) + [ANY] * ng,
        out_specs=[blk(lambda h, c: (jnp.minimum(c, nc - 1), h)), blk(late), blk(late)] + [ANY] * ng,
        out_shape=[jax.ShapeDtypeStruct((S, ATTN_WIDTH), BF16)] * 3 + recv_shapes,
        scratch_shapes=[pltpu.VMEM((CHUNK, SLAB), F32)] * (6 + 5 * n) + exchange_sems,
        compiler_params=_cp("arbitrary", "arbitrary"),
    )(proj, proj, proj, proj, proj, dcat, attn, lse, *grads)
    return out[:3], out[3:]


def _split_bf16(a):
    hi = a.astype(BF16)
    lo = (a - hi.astype(F32)).astype(BF16)
    return hi, lo


def _pooled(ug, halo_g, w, row0, tm):
    ext = jnp.concatenate([halo_g, ug], axis=0)
    hi, lo = _split_bf16(ext)
    rr = lax.broadcasted_iota(jnp.int32, (tm, tm + HALO), 0)
    cc = lax.broadcasted_iota(jnp.int32, (tm, tm + HALO), 1)
    back = rr + HALO - cc
    win = ((back >= 0) & (back < w)).astype(BF16)
    wsum = _dot(win, hi, NN) + _dot(win, lo, NN)
    rows = row0 + lax.broadcasted_iota(jnp.int32, (tm, 1), 0)
    inv = 1.0 / jnp.minimum(rows + 1, w).astype(F32)
    return wsum * inv - ug


def _pool_fwd(u, u_col, pool_w, pool_scale, *, name, tm=256):
    S, W = u.shape[0], POOL_WIDTH
    G = POOL_GROUP_DIM

    def body(u_ref, h_ref, w_ref, s_ref, o_ref):
        i = pl.program_id(0)
        uv = u_ref[...]
        halo = jnp.where(i > 0, h_ref[...], 0.0)
        sls = [slice(g * G, (g + 1) * G) for g in range(len(POOL_WINDOWS))]
        pooled = [_pooled(uv[:, sl], halo[:, sl], w, i * tm, tm) for sl, w in zip(sls, POOL_WINDOWS)]
        zs = [_dot(p.astype(BF16), w_ref[g].astype(BF16), NN) for g, p in enumerate(pooled)]
        for sl, z in zip(sls, zs):
            o_ref[:, sl] = (z * s_ref[:, sl]).astype(BF16)

    per = tm // HALO
    return pl.pallas_call(
        body, name=name, grid=(S // tm,),
        in_specs=[pl.BlockSpec((tm, W), lambda i: (i, u_col)),
                  pl.BlockSpec((HALO, W), lambda i: (jnp.maximum(i * per - 1, 0), u_col)),
                  pl.BlockSpec((len(POOL_WINDOWS), G, G), lambda i: (0, 0, 0)),
                  pl.BlockSpec((1, W), lambda i: (0, 0))],
        out_specs=pl.BlockSpec((tm, W), lambda i: (i, 0)),
        out_shape=jax.ShapeDtypeStruct((S, W), BF16),
        compiler_params=_cp("parallel"),
    )(u, u, pool_w, pool_scale)


def _pool_bwd(u, u_col, dy, dy_col, pool_w, pool_scale, *, name, tm=256):
    S, W = u.shape[0], POOL_WIDTH
    G = POOL_GROUP_DIM
    nt = S // tm

    def body(u_ref, h_ref, dy_ref, dyn_ref, w_ref, s_ref, du_ref, gw_ref, gs_ref):
        i = pl.program_id(0)

        @pl.when(i == 0)
        def _():
            gw_ref[...] = jnp.zeros_like(gw_ref)
            gs_ref[...] = jnp.zeros_like(gs_ref)

        uv = u_ref[...]
        halo = jnp.where(i > 0, h_ref[...], 0.0)
        dyv = dy_ref[...]
        dyn = jnp.where(i < nt - 1, dyn_ref[...], 0.0)
        rr = lax.broadcasted_iota(jnp.int32, (tm, tm + HALO), 0)
        cc = lax.broadcasted_iota(jnp.int32, (tm, tm + HALO), 1)
        rows_ext = i * tm + lax.broadcasted_iota(jnp.int32, (tm + HALO, 1), 0)
        groups = list(enumerate(POOL_WINDOWS))
        sls = [slice(g * G, (g + 1) * G) for g, _ in groups]
        wgs = [w_ref[g].astype(BF16) for g, _ in groups]
        pooled = [_pooled(uv[:, sl], halo[:, sl], w, i * tm, tm).astype(BF16) for sl, (_, w) in zip(sls, groups)]
        dzs = [dyv[:, sl] * s_ref[:, sl] for sl in sls]
        dz_ext = [jnp.concatenate([dz, dyn[:, sl] * s_ref[:, sl]], axis=0).astype(BF16) for dz, sl in zip(dzs, sls)]
        dp_ext = [_dot(d, wg, NT) for d, wg in zip(dz_ext, wgs)]
        zs = [_dot(p, wg, NN) for p, wg in zip(pooled, wgs)]
        for (g, w), sl, p, dz, z, dp in zip(groups, sls, pooled, dzs, zs, dp_ext):
            gw_ref[g] += _dot(p, dz.astype(BF16), TN)
            gs_ref[:, sl] += jnp.sum(dyv[:, sl] * z, axis=0, keepdims=True)
            inv_ext = 1.0 / jnp.minimum(rows_ext + 1, w).astype(F32)
            hi, lo = _split_bf16(dp * inv_ext)
            ahead = cc - rr
            win = ((ahead >= 0) & (ahead < w)).astype(BF16)
            du_ref[:, sl] = (_dot(win, hi, NN) + _dot(win, lo, NN) - dp[:tm]).astype(BF16)

    per = tm // HALO
    nh = S // HALO
    return pl.pallas_call(
        body, name=name, grid=(nt,),
        in_specs=[pl.BlockSpec((tm, W), lambda i: (i, u_col)),
                  pl.BlockSpec((HALO, W), lambda i: (jnp.maximum(i * per - 1, 0), u_col)),
                  pl.BlockSpec((tm, W), lambda i: (i, dy_col)),
                  pl.BlockSpec((HALO, W), lambda i: (jnp.minimum((i + 1) * per, nh - 1), dy_col)),
                  pl.BlockSpec((len(POOL_WINDOWS), G, G), lambda i: (0, 0, 0)),
                  pl.BlockSpec((1, W), lambda i: (0, 0))],
        out_specs=[pl.BlockSpec((tm, W), lambda i: (i, 0)),
                   pl.BlockSpec((len(POOL_WINDOWS), G, G), lambda i: (0, 0, 0)),
                   pl.BlockSpec((1, W), lambda i: (0, 0))],
        out_shape=[jax.ShapeDtypeStruct((S, W), BF16),
                   jax.ShapeDtypeStruct((len(POOL_WINDOWS), G, G), F32),
                   jax.ShapeDtypeStruct((1, W), F32)],
        compiler_params=_cp("arbitrary"),
    )(u, u, dy, dy, pool_w, pool_scale)


GELU_K0 = math.sqrt(2.0 / math.pi)
GELU_K1 = 0.044715


def _gelu_parts(x):
    x2 = x * x
    t = jnp.tanh(x * (GELU_K0 + (GELU_K0 * GELU_K1) * x2))
    hp = 0.5 + 0.5 * t
    gelu = x * hp
    dgelu = hp + (x * (hp * (1.0 - t))) * (GELU_K0 + (3.0 * GELU_K0 * GELU_K1) * x2)
    return gelu, dgelu


def _shifted(ext, halo):
    return (pltpu.roll(ext, 2, 0)[halo:], pltpu.roll(ext, 1, 0)[halo:], ext[halo:])


def _conv(sh, w, b):
    return b + (sh[0] * w[0:1] + sh[1] * w[1:2] + sh[2] * w[2:3])


F32_ROWS = 8


def _ffn_up_glu(h, w_up_t, conv_w, conv_b, *, name, tm=2048, tn=256, sub=256):
    S, K = h.shape
    F = D_FF
    nj = F // tn

    def body(h_ref, wg_ref, wv_ref, cwg_ref, cwv_ref, cbg_ref, cbv_ref,
             ug_ref, uv_ref, cg_ref, cv_ref, y_ref, carry):
        i = pl.program_id(0)
        j = pl.program_id(1)

        w_cat = jnp.concatenate([wg_ref[...], wv_ref[...]], axis=0)
        conv_w_b = ((cwg_ref[...], cbg_ref[...]), (cwv_ref[...], cbv_ref[...]))
        halo = [jnp.where(i > 0, carry[j, s], 0.0) for s in range(2)]
        u_next = _dot(h_ref[0:sub, :], w_cat, NT)
        for a in range(0, tm, sub):
            u16 = u_next.astype(BF16)
            if a + sub < tm:
                u_next = _dot(h_ref[a + sub:a + 2 * sub, :], w_cat, NT)
            ug_ref[a:a + sub, :] = u16[:, :tn]
            uv_ref[a:a + sub, :] = u16[:, tn:]
            c = []
            for s, (cw, cb) in enumerate(conv_w_b):
                u = u16[:, s * tn:(s + 1) * tn].astype(F32)
                ext = jnp.concatenate([halo[s], u], axis=0)
                c.append(_conv(_shifted(ext, F32_ROWS), cw, cb))
                halo[s] = u[sub - F32_ROWS:]
            cg_ref[a:a + sub, :] = c[0].astype(BF16)
            cv_ref[a:a + sub, :] = c[1].astype(BF16)
            gelu, _ = _gelu_parts(c[0])
            y_ref[a:a + sub, :] = (gelu * c[1]).astype(BF16)
        for s in range(2):
            carry[j, s] = halo[s]

    tile = pl.BlockSpec((tm, tn), lambda i, j: (i, j))
    vec = lambda rows, off: pl.BlockSpec((rows, tn), lambda i, j: (0, j + off))
    return pl.pallas_call(
        body, name=name, grid=(S // tm, nj),
        in_specs=[pl.BlockSpec((tm, K), lambda i, j: (i, 0)),
                  pl.BlockSpec((tn, K), lambda i, j: (j, 0)), pl.BlockSpec((tn, K), lambda i, j: (j + nj, 0)),
                  vec(3, 0), vec(3, nj), vec(1, 0), vec(1, nj)],
        out_specs=[tile] * 5,
        out_shape=[jax.ShapeDtypeStruct((S, F), BF16)] * 5,
        scratch_shapes=[pltpu.VMEM((nj, 2, F32_ROWS, tn), F32)],
        compiler_params=_cp("arbitrary", "arbitrary"),
    )(h, w_up_t, w_up_t, conv_w, conv_w, conv_b, conv_b)


def _ffn_glu_bwd(u_g, u_v, c_g, c_v, df, w_down, h, conv_w, *, name, tm=2048, tn=256, sub=256):
    S = u_g.shape[0]
    F = D_FF
    D = df.shape[1]
    nj = F // tn
    nt = S // tm

    def body(ug_ref, uv_ref, cg_ref, cgn_ref, cv_ref, cvn_ref, df_ref, dfn_ref, wd_ref, h_ref, wg_ref, wv_ref,
             dug_ref, duv_ref, gug_ref, guv_ref, gd_ref, gwg_ref, gwv_ref, gbg_ref, gbv_ref,
             acc_u, acc_d):
        i = pl.program_id(1)

        @pl.when(i == 0)
        def _():
            for r in (gwg_ref, gwv_ref, gbg_ref, gbv_ref, acc_u, acc_d):
                r[...] = jnp.zeros_like(r)

        wg, wv = wg_ref[...], wv_ref[...]
        wd = wd_ref[...]
        dfn = jnp.where(i < nt - 1, dfn_ref[...], jnp.zeros_like(dfn_ref))
        n_ext = sub + HALO

        def ahead(dc):
            return dc[:sub], pltpu.roll(dc, n_ext - 1, 0)[:sub], pltpu.roll(dc, n_ext - 2, 0)[:sub]

        def ext(ref, nxt, a):
            b = a + sub
            return jnp.concatenate([ref[a:b, :], ref[b:b + HALO, :] if b < tm else nxt], axis=0)

        dy_next = _dot(ext(df_ref, dfn, 0), wd, NT)
        for a in range(0, tm, sub):
            b = a + sub
            dy_ext = dy_next
            if b < tm:
                dy_next = _dot(ext(df_ref, dfn, b), wd, NT)
            cg = ext(cg_ref, cgn_ref[...], a).astype(F32)
            cv = ext(cv_ref, cvn_ref[...], a).astype(F32)
            df_sub = df_ref[a:b, :]
            gelu, dgelu = _gelu_parts(cg)
            dcs_g = ahead(dy_ext * cv * dgelu)
            dcs_v = ahead(dy_ext * gelu)
            du_g = (dcs_g[0] * wg[2:3] + dcs_g[1] * wg[1:2] + dcs_g[2] * wg[0:1]).astype(BF16)
            du_v = (dcs_v[0] * wv[2:3] + dcs_v[1] * wv[1:2] + dcs_v[2] * wv[0:1]).astype(BF16)
            dug_ref[a:b, :] = du_g
            duv_ref[a:b, :] = du_v
            acc_u[...] += _dot(jnp.concatenate([du_g, du_v], axis=1), h_ref[a:b, :], TN)
            acc_d[...] += _dot((gelu[:sub] * cv[:sub]).astype(BF16), df_sub, TN)
            for dcs, u_ref, gw_ref, gb_ref in ((dcs_g, ug_ref, gwg_ref, gbg_ref), (dcs_v, uv_ref, gwv_ref, gbv_ref)):
                u = u_ref[a:b, :].astype(F32)
                gb_ref[...] += jnp.sum(dcs[0], axis=0, keepdims=True)
                for k in range(3):
                    gw_ref[k:k + 1, :] += jnp.sum(dcs[2 - k] * u, axis=0, keepdims=True)

        @pl.when(i == nt - 1)
        def _():
            gug_ref[...] = acc_u[:tn, :].astype(BF16)
            guv_ref[...] = acc_u[tn:, :].astype(BF16)
            gd_ref[...] = acc_d[...].astype(BF16)

    per = tm // HALO
    nh = S // HALO
    hnext = lambda i: jnp.minimum((i + 1) * per, nh - 1)
    tile = pl.BlockSpec((tm, tn), lambda j, i: (i, j))
    hn = pl.BlockSpec((HALO, tn), lambda j, i: (hnext(i), j))
    vec = lambda rows, off: pl.BlockSpec((rows, tn), lambda j, i: (0, j + off))
    wide = pl.BlockSpec((tm, D), lambda j, i: (i, 0))
    wrow = pl.BlockSpec((tn, D), lambda j, i: (j, 0))
    return pl.pallas_call(
        body, name=name, grid=(nj, nt),
        in_specs=[tile, tile, tile, hn, tile, hn, wide, pl.BlockSpec((HALO, D), lambda j, i: (hnext(i), 0)),
                  wrow, wide, vec(3, 0), vec(3, nj)],
        out_specs=[tile, tile, wrow, wrow, wrow, vec(3, 0), vec(3, 0), vec(1, 0), vec(1, 0)],
        out_shape=[jax.ShapeDtypeStruct((S, F), BF16), jax.ShapeDtypeStruct((S, F), BF16),
                   jax.ShapeDtypeStruct((F, D), BF16), jax.ShapeDtypeStruct((F, D), BF16),
                   jax.ShapeDtypeStruct((F, D), BF16),
                   jax.ShapeDtypeStruct((3, F), F32), jax.ShapeDtypeStruct((3, F), F32),
                   jax.ShapeDtypeStruct((1, F), F32), jax.ShapeDtypeStruct((1, F), F32)],
        scratch_shapes=[pltpu.VMEM((2 * tn, D), F32), pltpu.VMEM((tn, D), F32)],
        compiler_params=_cp("parallel", "arbitrary"),
    )(u_g, u_v, c_g, c_g, c_v, c_v, df, df, w_down, h, conv_w, conv_w)


def _sum_partials(parts, *, name, tr):
    _, R, C = parts.shape

    def body(p_ref, o_ref):
        tot = p_ref[0].astype(F32)
        for j in range(1, N_DEV):
            tot = tot + p_ref[j].astype(F32)
        o_ref[...] = tot

    return pl.pallas_call(
        body, name=name, grid=(R // tr,),
        in_specs=[pl.BlockSpec((N_DEV, tr, C), lambda i: (0, i, 0))],
        out_specs=pl.BlockSpec((tr, C), lambda i: (i, 0)),
        out_shape=jax.ShapeDtypeStruct((R, C), F32),
        compiler_params=_cp("parallel"),
    )(parts)


def _adamw(w, g, m, v, *, name, tr):
    R, C = w.shape
    c1 = 1.0 - ADAM_B1 ** ADAM_STEP
    c2 = 1.0 - ADAM_B2 ** ADAM_STEP

    def body(w_ref, g_ref, m_ref, v_ref, d_ref, nm_ref, nv_ref):
        g = g_ref[...]
        nm = ADAM_B1 * m_ref[...] + (1.0 - ADAM_B1) * g
        nv = ADAM_B2 * v_ref[...] + (1.0 - ADAM_B2) * (g * g)
        d_ref[...] = -ADAM_LR * ((nm / c1) / (jnp.sqrt(nv / c2) + ADAM_EPS) + ADAM_WD * w_ref[...])
        nm_ref[...] = nm
        nv_ref[...] = nv

    spec = pl.BlockSpec((tr, C), lambda i: (i, 0))
    return pl.pallas_call(
        body, name=name, grid=(R // tr,), in_specs=[spec] * 4, out_specs=[spec] * 3,
        out_shape=[jax.ShapeDtypeStruct((R, C), F32)] * 3,
        compiler_params=_cp("parallel"),
    )(w, g, m, v)


def _mesh_pos():
    return lax.axis_index("x"), lax.axis_index("y"), lax.axis_index("c")


def _gather_phases(x_refs, out_refs, send_sems, recv_sems, local_sems):
    x, y, c = _mesh_pos()
    me, sibling = (x, y, c), (x, y, 1 - c)
    chips = [(1 - x, y), (x, 1 - y), (1 - x, 1 - y)]
    arrays = range(len(x_refs))

    def slot(a, px, py, pc):
        return out_refs[a].at[4 * px + 2 * py + pc]

    def copy(a, k, block, to, own=False):
        return pltpu.make_async_remote_copy(
            src_ref=x_refs[a] if own else slot(a, *block), dst_ref=slot(a, *block),
            send_sem=send_sems.at[a, k], recv_sem=recv_sems.at[a, k], device_id=to, device_id_type=MESH)

    mine = [pltpu.make_async_copy(x_refs[a], slot(a, *me), local_sems.at[a]) for a in arrays]
    first = [copy(a, 0, me, sibling, own=True) for a in arrays]
    first += [copy(a, 1 + j, me, (*chip, c), own=True) for j, chip in enumerate(chips) for a in arrays]
    passed = [[copy(a, 4 + j, (*chip, c), sibling) for a in arrays] for j, chip in enumerate(chips)]

    def start():
        for cp in mine + first:
            cp.start()

    def forward():
        for j, chip in enumerate(chips):
            for a in arrays:
                copy(a, 1 + j, (*chip, c), me).wait_recv()
                passed[j][a].start()

    def finish():
        for a in arrays:
            copy(a, 0, sibling, me).wait_recv()
            for j, chip in enumerate(chips):
                copy(a, 4 + j, (*chip, 1 - c), me).wait_recv()
        for cp in first + [cp for row in passed for cp in row]:
            cp.wait_send()
        for cp in mine:
            cp.wait()

    return start, forward, finish


def _gather_sems(n):
    return [pltpu.SemaphoreType.DMA((n, 7)), pltpu.SemaphoreType.DMA((n, 7)), pltpu.SemaphoreType.DMA((n,))]


def _gathered_shapes(blocks):
    return [jax.ShapeDtypeStruct((N_DEV,) + b.shape, b.dtype) for b in blocks]


def _all_reduce_small(block, *, name):
    def body(x_ref, all_ref, sum_ref, *sems):
        for phase in _gather_phases([x_ref], [all_ref], *sems):
            phase()
        tot = all_ref[0]
        for j in range(1, N_DEV):
            tot = tot + all_ref[j]
        sum_ref[...] = tot

    return pl.pallas_call(
        body, name=name, in_specs=[VMEM], out_specs=[VMEM, VMEM],
        out_shape=[jax.ShapeDtypeStruct((N_DEV,) + block.shape, block.dtype),
                   jax.ShapeDtypeStruct(block.shape, block.dtype)],
        scratch_shapes=_gather_sems(1),
        compiler_params=pltpu.CompilerParams(vmem_limit_bytes=V7X_VMEM_LIMIT),
    )(block)[1]


def _exchange_phases(g_refs, r_refs, send_sems, recv_sems, local_sems):
    x, y, c = _mesh_pos()
    me = 4 * x + 2 * y + c
    owns, remote = [], []
    for k, (g_ref, r_ref) in enumerate(zip(g_refs, r_refs)):
        rows = g_ref.shape[0] // N_DEV
        owns.append(pltpu.make_async_copy(g_ref.at[pl.ds(me * rows, rows)], r_ref.at[me], local_sems.at[k]))
        for p in range(1, N_DEV):
            px, py, pc = x ^ (p >> 2), y ^ ((p >> 1) & 1), c ^ (p & 1)
            peer = 4 * px + 2 * py + pc
            link = dict(send_sem=send_sems.at[k, p], recv_sem=recv_sems.at[k, p],
                        device_id=(px, py, pc), device_id_type=MESH)
            src = g_ref.at[pl.ds(peer * rows, rows)]
            send = pltpu.make_async_remote_copy(src_ref=src, dst_ref=r_ref.at[me], **link)
            arrival = pltpu.make_async_remote_copy(src_ref=src, dst_ref=r_ref.at[peer], **link)
            remote.append((send, arrival))

    def start():
        for own in owns:
            own.start()
        for send, _ in remote:
            send.start()

    def finish():
        for _, arrival in remote:
            arrival.wait_recv()
        for send, _ in remote:
            send.wait_send()
        for own in owns:
            own.wait()

    return start, finish


def _exchange_buffers(grads):
    n = len(grads)
    shapes = [jax.ShapeDtypeStruct((N_DEV, g.shape[0] // N_DEV, g.shape[1]), g.dtype) for g in grads]
    sems = [pltpu.SemaphoreType.DMA((n, N_DEV)), pltpu.SemaphoreType.DMA((n, N_DEV)),
            pltpu.SemaphoreType.DMA((n,))]
    return shapes, sems


def _unpack_gathered(gathered):
    w_out, w_up_t, w_down = (g.reshape(-1, D_MODEL) for g in gathered[:3])
    width = 2 * D_FF // N_DEV
    conv_w = jnp.transpose(gathered[3][:, :3, :width], (1, 0, 2)).reshape(3, 2 * D_FF)
    return w_out, w_up_t, w_down, conv_w


def _rest_payload(w_out, w_up, w_down, conv_w):
    rows, cols = conv_w.shape
    conv_w = jnp.pad(conv_w, ((0, (-rows) % F32_ROWS), (0, (-cols) % LANES)))
    return [w_out.astype(BF16), w_up.T.astype(BF16), w_down.astype(BF16), conv_w]


def _device_step(x, target, g_mix_pre, w_in_t_block, rest_payload, pool_w, pool_scale, g_mix_post, g_ffn_pre,
                 conv_b, g_ffn_post):
    h1, w_in_t = _rms_norm_gather(x, g_mix_pre, w_in_t_block, name="rms_mix_pre")
    w_in_t = w_in_t.reshape(-1, D_MODEL)
    proj = _matmul(h1, w_in_t, trans_b=True, out_dtype=F32, tm=1024, tn=512, name="proj")
    attn, lse, attn16, gathered = _attn_fwd(proj, rest_payload, name="attn_fwd")
    w_out, w_up_t, w_down, conv_w = _unpack_gathered(gathered)
    pool = _pool_fwd(proj, 3, pool_w, pool_scale, name="pool_fwd")
    mixed, x2, h2 = _mix_out(attn16, pool, w_out, x, g_mix_post, g_ffn_pre, name="mix_out")
    u_g, u_v, c_g, c_v, y = _ffn_up_glu(h2, w_up_t, conv_w, conv_b, name="ffn_up_glu")
    df, d_out, loss_blk, gg_ffn_post = _ffn_out(y, w_down, x2, target, g_ffn_post, name="ffn_out")
    du_g, du_v, gw_up_g, gw_up_v, gw_down, gcw_g, gcw_v, gcb_g, gcb_v = _ffn_glu_bwd(
        u_g, u_v, c_g, c_v, df, w_down, h2, conv_w, name="ffn_glu_bwd")
    gw_up_t = jnp.concatenate([gw_up_g, gw_up_v], axis=0)
    dx2, gg_ffn_pre, dmixed, gg_mix_post = _dgrad_norm(
        [du_g, du_v], w_up_t, d_out, x2, g_ffn_pre, (mixed, g_mix_post), [], name="ffn_up_dgrad")
    gw_out = jnp.concatenate([_matmul_tn(attn16, dmixed, ta=512, ts=1024, name="grad_w_out_attn"),
                              _matmul_tn(pool, dmixed, ta=512, ts=1024, name="grad_w_out_pool")], axis=0)
    dcat = _matmul(dmixed, w_out, trans_b=True, out_dtype=F32, tm=512, tn=1024, name="mix_out_dgrad")
    d_pool_in, g_pool_w, g_pool_scale = _pool_bwd(proj, 3, dcat, 1, pool_w, pool_scale, name="pool_bwd")
    dqkv, (r_out, r_up_t, r_down) = _attn_bwd(proj, dcat, attn, lse, [gw_out, gw_up_t, gw_down], name="attn_bwd")
    dproj = list(dqkv) + [d_pool_in]
    gw_in_t = jnp.concatenate([_matmul_tn(a, h1, ta=512, ts=1024, name=f"grad_w_in_{k}")
                               for k, a in enumerate(dproj)], axis=0)
    grad_x, gg_mix_pre, (r_in_t,) = _dgrad_norm(dproj, w_in_t, dx2, x, g_mix_pre, None, [gw_in_t], name="proj_dgrad")
    g_conv_w = jnp.concatenate([gcw_g, gcw_v], axis=1)
    g_conv_b = jnp.concatenate([gcb_g, gcb_v], axis=1)
    received = (r_in_t, r_out, r_up_t, r_down)
    small = dict(g_mix_pre=gg_mix_pre, g_mix_post=gg_mix_post, g_ffn_pre=gg_ffn_pre, g_ffn_post=gg_ffn_post,
                 pool_scale=g_pool_scale, conv_b=g_conv_b, pool_w=g_pool_w, conv_w=g_conv_w)
    return loss_blk, grad_x, received, small


_SMALL = ("g_mix_pre", "g_mix_post", "g_ffn_pre", "g_ffn_post", "pool_scale", "conv_b", "pool_w")
LANES = 128


def _pack_rows(arrays):
    parts = []
    for a in arrays:
        a2 = a.reshape(-1, LANES)
        parts.append(jnp.pad(a2, ((0, (-a2.shape[0]) % 8), (0, 0))))
    return jnp.concatenate(parts, axis=0)


def _unpack_rows(packed, shapes):
    out, row = [], 0
    for shape in shapes:
        rows = math.prod(shape) // LANES
        out.append(packed[row:row + rows].reshape(shape))
        row += -(-rows // 8) * 8
    return out


def kernel(x, g_mix_pre, w_in, pool_w, pool_scale, w_out, g_mix_post, g_ffn_pre, w_up, conv_w, conv_b, w_down, g_ffn_post, loss_target, m_g_mix_pre, m_w_in, m_pool_w, m_pool_scale, m_w_out, m_g_mix_post, m_g_ffn_pre, m_w_up, m_conv_w, m_conv_b, m_w_down, m_g_ffn_post, v_g_mix_pre, v_w_in, v_pool_w, v_pool_scale, v_w_out, v_g_mix_post, v_g_ffn_pre, v_w_up, v_conv_w, v_conv_b, v_w_down, v_g_ffn_post):
    me = 4 * lax.axis_index("x") + 2 * lax.axis_index("y") + lax.axis_index("c")
    loss_blk, grad_x, recv, small = _device_step(
        x[0], loss_target[0], g_mix_pre, w_in[0].T.astype(BF16),
        _rest_payload(w_out[0], w_up[0], w_down[0], conv_w[0]),
        pool_w[0], pool_scale, g_mix_post, g_ffn_pre, conv_b, g_ffn_post)

    g_in_t, g_out, g_up_t, g_down = (
        _sum_partials(r, name=f"sum_partials_{k}", tr=r.shape[1] // 2) for k, r in enumerate(recv))
    grads = {"w_in": g_in_t.T, "w_out": g_out, "w_up": g_up_t.T, "w_down": g_down}

    given = dict(g_mix_pre=g_mix_pre, g_mix_post=g_mix_post, g_ffn_pre=g_ffn_pre, g_ffn_post=g_ffn_post,
                 pool_scale=pool_scale, conv_b=conv_b, pool_w=pool_w)
    small_shapes = [given[k].shape for k in _SMALL]
    total = _all_reduce_small(_pack_rows([small[k] for k in _SMALL] + [small["conv_w"], loss_blk]),
                              name="all_reduce_small")
    *small_grads, g_conv_w_all, loss_all = _unpack_rows(total, small_shapes + [(3, 2 * D_FF), loss_blk.shape])
    loss = loss_all[0, 0]
    grads.update(zip(_SMALL, small_grads))
    width = 2 * D_FF // N_DEV
    grads["conv_w"] = lax.dynamic_slice_in_dim(g_conv_w_all, me * width, width, axis=1)[None]

    weights = dict(g_mix_pre=g_mix_pre, w_in=w_in, pool_w=pool_w, pool_scale=pool_scale, w_out=w_out,
                   g_mix_post=g_mix_post, g_ffn_pre=g_ffn_pre, w_up=w_up, conv_w=conv_w, conv_b=conv_b,
                   w_down=w_down, g_ffn_post=g_ffn_post)
    m_in = dict(g_mix_pre=m_g_mix_pre, w_in=m_w_in, pool_w=m_pool_w, pool_scale=m_pool_scale, w_out=m_w_out,
                g_mix_post=m_g_mix_post, g_ffn_pre=m_g_ffn_pre, w_up=m_w_up, conv_w=m_conv_w, conv_b=m_conv_b,
                w_down=m_w_down, g_ffn_post=m_g_ffn_post)
    v_in = dict(g_mix_pre=v_g_mix_pre, w_in=v_w_in, pool_w=v_pool_w, pool_scale=v_pool_scale, w_out=v_w_out,
                g_mix_post=v_g_mix_post, g_ffn_pre=v_g_ffn_pre, w_up=v_w_up, conv_w=v_conv_w, conv_b=v_conv_b,
                w_down=v_w_down, g_ffn_post=v_g_ffn_post)
    delta, new_m, new_v = {}, {}, {}
    for k in ("w_in", "w_out", "w_up", "w_down"):
        g = grads[k]
        d, nm, nv = _adamw(weights[k][0], g, m_in[k][0], v_in[k][0], name=f"adamw_{k}", tr=g.shape[0] // 2)
        grads[k], delta[k], new_m[k], new_v[k] = g[None], d[None], nm[None], nv[None]
    d, nm, nv = _adamw(weights["conv_w"][0], grads["conv_w"][0], m_in["conv_w"][0], v_in["conv_w"][0],
                       name="adamw_conv_w", tr=3)
    delta["conv_w"], new_m["conv_w"], new_v["conv_w"] = d[None], nm[None], nv[None]
    packed_w = _pack_rows([weights[k] for k in _SMALL])
    small_rows = packed_w.shape[0]
    d, nm, nv = _adamw(packed_w, total[:small_rows], _pack_rows([m_in[k] for k in _SMALL]),
                       _pack_rows([v_in[k] for k in _SMALL]), name="adamw_small", tr=small_rows)
    for k, dk, mk, vk in zip(_SMALL, _unpack_rows(d, small_shapes), _unpack_rows(nm, small_shapes),
                             _unpack_rows(nv, small_shapes)):
        delta[k], new_m[k], new_v[k] = dk, mk, vk

    order = ("g_mix_pre", "w_in", "pool_w", "pool_scale", "w_out", "g_mix_post", "g_ffn_pre", "w_up",
             "conv_w", "conv_b", "w_down", "g_ffn_post")
    return (loss, grad_x[None], *[grads[k] for k in order], *[delta[k] for k in order],
            *[new_m[k] for k in order], *[new_v[k] for k in order])
```

```python
import functools
import math

import jax
import jax.numpy as jnp
from jax import lax
from jax.experimental import pallas as pl
from jax.experimental.pallas import tpu as pltpu

F32 = jnp.float32
BF16 = jnp.bfloat16

D_MODEL = 1024
ATTN_WIDTH = 512
N_HEADS = 8
HEAD_DIM = 64
DILATIONS = (1, 4, 16)
BLOCK = 128
POOL_WIDTH = 512
POOL_WINDOWS = (2, 4, 8, 16)
POOL_GROUP_DIM = 128
D_FF = 2816
EPS = 1e-6
NEG_INF = -1e30
SCALE = HEAD_DIM ** -0.5

ADAM_LR = 0.001
ADAM_B1 = 0.9
ADAM_B2 = 0.999
ADAM_EPS = 1e-08
ADAM_WD = 0.01
ADAM_STEP = 10

N_DEV = 8
HALO = 16
V7X_VMEM_LIMIT = 56 * 1024 * 1024

MESH = pl.DeviceIdType.MESH
ANY = pl.BlockSpec(memory_space=pl.ANY)
VMEM = pl.BlockSpec(memory_space=pltpu.VMEM)

NT = (((1,), (1,)), ((), ()))
NN = (((1,), (0,)), ((), ()))
TN = (((0,), (0,)), ((), ()))


def _cp(*sem):
    return pltpu.CompilerParams(dimension_semantics=sem, vmem_limit_bytes=V7X_VMEM_LIMIT)


def _dot(a, b, dn):
    return lax.dot_general(a, b, dn, preferred_element_type=F32)


def _rms_bwd(xin, g, dy):
    r = lax.rsqrt(jnp.mean(xin * xin, axis=-1, keepdims=True) + EPS)
    xh = xin * r
    gdy = g * dy
    dx = r * (gdy - xh * jnp.mean(gdy * xh, axis=-1, keepdims=True))
    dg = jnp.sum(dy * xh, axis=0, keepdims=True)
    return dx, dg


def _rms_norm_gather(x, g, block, *, name, tm=512):
    S, D = x.shape
    nt = S // tm

    def body(x_ref, g_ref, blk_ref, o_ref, all_ref, *sems):
        i = pl.program_id(0)
        start, forward, finish = _gather_phases([blk_ref], [all_ref], *sems)
        pl.when(i == 0)(start)
        xv = x_ref[...]
        r = lax.rsqrt(jnp.mean(xv * xv, axis=-1, keepdims=True) + EPS)
        o_ref[...] = (xv * r * g_ref[...]).astype(BF16)
        pl.when(i == (2 * nt) // 3)(forward)
        pl.when(i == nt - 1)(finish)

    return pl.pallas_call(
        body, name=name, grid=(nt,),
        in_specs=[pl.BlockSpec((tm, D), lambda i: (i, 0)), pl.BlockSpec((1, D), lambda i: (0, 0)), ANY],
        out_specs=[pl.BlockSpec((tm, D), lambda i: (i, 0)), ANY],
        out_shape=[jax.ShapeDtypeStruct((S, D), BF16)] + _gathered_shapes([block]),
        scratch_shapes=_gather_sems(1),
        compiler_params=_cp("arbitrary"),
    )(x, g, block)


def _matmul(a, b, *, trans_b, out_dtype, tm, tn, name):
    M, K = a.shape
    N = b.shape[0] if trans_b else b.shape[1]
    dn = NT if trans_b else NN

    def body(a_ref, b_ref, o_ref):
        o_ref[...] = _dot(a_ref[...], b_ref[...], dn).astype(out_dtype)

    b_spec = (pl.BlockSpec((tn, K), lambda i, j: (j, 0)) if trans_b
              else pl.BlockSpec((K, tn), lambda i, j: (0, j)))
    return pl.pallas_call(
        body, name=name, grid=(M // tm, N // tn),
        in_specs=[pl.BlockSpec((tm, K), lambda i, j: (i, 0)), b_spec],
        out_specs=pl.BlockSpec((tm, tn), lambda i, j: (i, j)),
        out_shape=jax.ShapeDtypeStruct((M, N), out_dtype),
        compiler_params=_cp("parallel", "parallel"),
    )(a, b)


def _matmul_tn(a, b, *, ta, ts, name):
    S, Ka = a.shape
    Nb = b.shape[1]
    ns = S // ts

    def body(a_ref, b_ref, o_ref, acc):
        s = pl.program_id(1)

        @pl.when(s == 0)
        def _():
            acc[...] = jnp.zeros_like(acc)

        acc[...] += _dot(a_ref[...], b_ref[...], TN)

        @pl.when(s == ns - 1)
        def _():
            o_ref[...] = acc[...].astype(BF16)

    return pl.pallas_call(
        body, name=name, grid=(Ka // ta, ns),
        in_specs=[pl.BlockSpec((ts, ta), lambda i, s: (s, i)), pl.BlockSpec((ts, Nb), lambda i, s: (s, 0))],
        out_specs=pl.BlockSpec((ta, Nb), lambda i, s: (i, 0)),
        out_shape=jax.ShapeDtypeStruct((Ka, Nb), BF16),
        scratch_shapes=[pltpu.VMEM((ta, Nb), F32)],
        compiler_params=_cp("parallel", "arbitrary"),
    )(a, b)


def _mix_out(attn, pool, w_out, x, g_post, g_next, *, name, tm=256):
    S, K = attn.shape
    D = w_out.shape[1]

    def body(a_ref, p_ref, w_ref, x_ref, gp_ref, gn_ref, mixed_ref, x2_ref, h2_ref):
        mixed = _dot(a_ref[...], w_ref[:K, :], NN) + _dot(p_ref[...], w_ref[K:, :], NN)
        r = lax.rsqrt(jnp.mean(mixed * mixed, axis=-1, keepdims=True) + EPS)
        x2 = x_ref[...] + mixed * r * gp_ref[...]
        r2 = lax.rsqrt(jnp.mean(x2 * x2, axis=-1, keepdims=True) + EPS)
        mixed_ref[...] = mixed
        x2_ref[...] = x2
        h2_ref[...] = (x2 * r2 * gn_ref[...]).astype(BF16)

    row = lambda i: (i, 0)
    fix = lambda i: (0, 0)
    return pl.pallas_call(
        body, name=name, grid=(S // tm,),
        in_specs=[pl.BlockSpec((tm, K), row), pl.BlockSpec((tm, K), row), pl.BlockSpec((2 * K, D), fix),
                  pl.BlockSpec((tm, D), row), pl.BlockSpec((1, D), fix), pl.BlockSpec((1, D), fix)],
        out_specs=[pl.BlockSpec((tm, D), row)] * 3,
        out_shape=[jax.ShapeDtypeStruct((S, D), F32), jax.ShapeDtypeStruct((S, D), F32),
                   jax.ShapeDtypeStruct((S, D), BF16)],
        compiler_params=_cp("parallel"),
    )(attn, pool, w_out, x, g_post, g_next)


def _ffn_out(y, w_down, x2, target, g_post, *, name, tm=512, sub=256):
    S, K = y.shape
    D = w_down.shape[1]

    def body(y_ref, w_ref, x2_ref, t_ref, g_ref, df_ref, dout_ref, loss_ref, gg_ref):
        i = pl.program_id(0)

        @pl.when(i == 0)
        def _():
            loss_ref[...] = jnp.zeros_like(loss_ref)
            gg_ref[...] = jnp.zeros_like(gg_ref)

        g = g_ref[...]
        w = w_ref[...]
        f_next = _dot(y_ref[0:sub, :], w, NN)
        for a in range(0, tm, sub):
            rows = slice(a, a + sub)
            f = f_next
            if a + sub < tm:
                f_next = _dot(y_ref[a + sub:a + 2 * sub, :], w, NN)
            r = lax.rsqrt(jnp.mean(f * f, axis=-1, keepdims=True) + EPS)
            out = x2_ref[rows, :] + f * r * g
            err = out - t_ref[rows, :]
            dy = err * (1.0 / D)
            df, dg = _rms_bwd(f, g, dy)
            df_ref[rows, :] = df.astype(BF16)
            dout_ref[rows, :] = dy
            gg_ref[...] += dg
            loss_ref[...] += 0.5 * jnp.sum(jnp.mean(err * err, axis=-1, keepdims=True))

    row = lambda i: (i, 0)
    fix = lambda i: (0, 0)
    return pl.pallas_call(
        body, name=name, grid=(S // tm,),
        in_specs=[pl.BlockSpec((tm, K), row), pl.BlockSpec((K, D), fix), pl.BlockSpec((tm, D), row),
                  pl.BlockSpec((tm, D), row), pl.BlockSpec((1, D), fix)],
        out_specs=[pl.BlockSpec((tm, D), row), pl.BlockSpec((tm, D), row),
                   pl.BlockSpec((8, 128), fix), pl.BlockSpec((1, D), fix)],
        out_shape=[jax.ShapeDtypeStruct((S, D), BF16), jax.ShapeDtypeStruct((S, D), F32),
                   jax.ShapeDtypeStruct((8, 128), F32), jax.ShapeDtypeStruct((1, D), F32)],
        compiler_params=_cp("arbitrary"),
    )(y, w_down, x2, target, g_post)


def _dgrad_norm(a_list, w, resid, xin, g, second, exchange, *, name, tm=512, sub=256):
    S, Kp = a_list[0].shape
    na = len(a_list)
    D = w.shape[1]
    nt = S // tm
    two = second is not None
    ng = len(exchange)
    recv_shapes, exchange_sems = _exchange_buffers(exchange)

    def body(*refs):
        a_refs = refs[:na]
        w_ref, r_ref, x_ref, g_ref = refs[na:na + 4]
        pos = na + 4
        if two:
            x2_ref, g2_ref = refs[pos:pos + 2]
            pos += 2
        g_refs = refs[pos:pos + ng]
        pos += ng
        dx_ref, gg_ref = refs[pos:pos + 2]
        pos += 2
        if two:
            d2_ref, gg2_ref = refs[pos:pos + 2]
            pos += 2
        r_refs = refs[pos:pos + ng]
        pos += ng
        i = pl.program_id(0)
        if ng:
            start, finish = _exchange_phases(g_refs, r_refs, *refs[pos:])
            pl.when(i == 0)(start)

        @pl.when(i == 0)
        def _():
            gg_ref[...] = jnp.zeros_like(gg_ref)
            if two:
                gg2_ref[...] = jnp.zeros_like(gg2_ref)

        def dh_of(a):
            return functools.reduce(jnp.add, [_dot(a_refs[q][a:a + sub, :], w_ref[q * Kp:(q + 1) * Kp, :], NN)
                                              for q in range(na)])

        dh_next = dh_of(0)
        for a in range(0, tm, sub):
            rows = slice(a, a + sub)
            dh = dh_next
            if a + sub < tm:
                dh_next = dh_of(a + sub)
            d1, dg1 = _rms_bwd(x_ref[rows, :], g_ref[...], dh)
            dx = r_ref[rows, :] + d1
            dx_ref[rows, :] = dx
            gg_ref[...] += dg1
            if two:
                d2, dg2 = _rms_bwd(x2_ref[rows, :], g2_ref[...], dx)
                d2_ref[rows, :] = d2.astype(BF16)
                gg2_ref[...] += dg2
        if ng:
            pl.when(i == nt - 1)(finish)

    row = lambda i: (i, 0)
    fix = lambda i: (0, 0)
    in_specs = [pl.BlockSpec((tm, Kp), row)] * na + [
        pl.BlockSpec((na * Kp, D), fix, pipeline_mode=pl.Buffered(1)), pl.BlockSpec((tm, D), row),
        pl.BlockSpec((tm, D), row), pl.BlockSpec((1, D), fix)]
    args = list(a_list) + [w, resid, xin, g]
    out_specs = [pl.BlockSpec((tm, D), row), pl.BlockSpec((1, D), fix)]
    out_shape = [jax.ShapeDtypeStruct((S, D), F32), jax.ShapeDtypeStruct((1, D), F32)]
    if two:
        in_specs += [pl.BlockSpec((tm, D), row), pl.BlockSpec((1, D), fix)]
        args += list(second)
        out_specs += [pl.BlockSpec((tm, D), row), pl.BlockSpec((1, D), fix)]
        out_shape += [jax.ShapeDtypeStruct((S, D), BF16), jax.ShapeDtypeStruct((1, D), F32)]
    n_plain = len(out_shape)
    out = pl.pallas_call(
        body, name=name, grid=(nt,), in_specs=in_specs + [ANY] * ng, out_specs=out_specs + [ANY] * ng,
        out_shape=out_shape + recv_shapes, scratch_shapes=exchange_sems if ng else [],
        compiler_params=_cp("arbitrary"),
    )(*args, *exchange)
    return (*out[:n_plain], out[n_plain:]) if ng else out


def _band_mask(first_block):
    qi = lax.broadcasted_iota(jnp.int32, (BLOCK, 2 * BLOCK), 0)
    ki = lax.broadcasted_iota(jnp.int32, (BLOCK, 2 * BLOCK), 1)
    first_key = jnp.where(first_block, BLOCK, 0)
    return (ki >= qi) & (ki <= qi + BLOCK) & (ki >= first_key)


def _lane_masks():
    lane = lax.broadcasted_iota(jnp.int32, (1, 2 * HEAD_DIM), 1)
    return (lane < HEAD_DIM, lane >= HEAD_DIM)


CHUNK = BLOCK * max(DILATIONS)
SLAB = 2 * HEAD_DIM
N_SLABS = ATTN_WIDTH // SLAB


def _unit_rows(d, b):
    def rows(r):
        start = r + BLOCK * d * b
        return pl.ds(start, BLOCK, stride=d) if d > 1 else pl.ds(start, BLOCK)
    return rows


def _attn_units():
    for p, d in enumerate(DILATIONS):
        nbc = CHUNK // (BLOCK * d)
        for b in range(nbc):
            for r in range(d):
                yield p, d, b, r, nbc


def _attn_in_specs(nc, n_cur):
    prev = lambda c: jnp.maximum(jnp.minimum(c, nc - 1) - 1, 0)
    cur = lambda c: jnp.minimum(c, nc - 1)
    blk = lambda f: pl.BlockSpec((CHUNK, SLAB), f)
    specs = [blk(lambda h, c: (cur(c), h)),
             blk(lambda h, c: (prev(c), N_SLABS + h)), blk(lambda h, c: (cur(c), N_SLABS + h)),
             blk(lambda h, c: (prev(c), 2 * N_SLABS + h)), blk(lambda h, c: (cur(c), 2 * N_SLABS + h))]
    return specs + [blk(lambda h, c: (cur(c), h))] * n_cur


def _attn_fwd(proj, payload, *, name):
    S = proj.shape[0]
    nc = S // CHUNK
    n = len(DILATIONS)
    npay = len(payload)
    n_steps = N_SLABS * nc

    def body(*refs):
        q_ref, kp_ref, kc_ref, vp_ref, vc_ref = refs[:5]
        pay_refs = refs[5:5 + npay]
        attn_ref, lse_ref, attn16_ref = refs[5 + npay:8 + npay]
        all_refs = refs[8 + npay:8 + 2 * npay]
        scr = refs[8 + 2 * npay:]
        o_scr, l_scr = scr[:n], scr[n:2 * n]
        start, forward, finish = _gather_phases(pay_refs, all_refs, *scr[2 * n:])
        step = pl.program_id(0) * nc + pl.program_id(1)
        pl.when(step == 0)(start)
        c = pl.program_id(1)
        lms = _lane_masks()
        plain, first = (jnp.tile(_band_mask(f), (2, 1)) for f in (False, c == 0))
        def scores(unit):
            p, d, b, r, nbc = unit
            rows = _unit_rows(d, b)(r)
            prow = _unit_rows(d, (b - 1) % nbc)(r)
            kpr, vpr = (kc_ref, vc_ref) if b > 0 else (kp_ref, vp_ref)
            q = q_ref[rows, :].astype(BF16)
            kcat = jnp.concatenate([kpr[prow, :], kc_ref[rows, :]], axis=0).astype(BF16)
            vcat = jnp.concatenate([vpr[prow, :], vc_ref[rows, :]], axis=0).astype(BF16)
            q2 = jnp.concatenate([jnp.where(lm, q, jnp.zeros_like(q)) for lm in lms], axis=0) * SCALE
            return p, rows, plain if b > 0 else first, vcat, _dot(q2, kcat, NT)

        units = list(_attn_units())
        nxt = scores(units[0])
        for k in range(len(units)):
            p, rows, mask2, vcat, s = nxt
            if k + 1 < len(units):
                nxt = scores(units[k + 1])
            s = jnp.where(mask2, s, NEG_INF)
            m = jnp.max(s, axis=-1, keepdims=True)
            e = jnp.exp(s - m)
            l = jnp.sum(e, axis=-1, keepdims=True)
            o2 = _dot(e.astype(BF16), vcat, NN) / l
            lse2 = m + jnp.log(l)
            o_scr[p][rows, :] = jnp.where(lms[0], o2[:BLOCK], o2[BLOCK:])
            l_scr[p][rows, :] = jnp.where(lms[0], lse2[:BLOCK], lse2[BLOCK:])
        ls = [l_scr[p][...] for p in range(n)]
        top = functools.reduce(jnp.maximum, ls)
        es = [jnp.exp(l - top) for l in ls]
        den = functools.reduce(jnp.add, es)
        num = functools.reduce(jnp.add, [e * o_scr[p][...] for p, e in enumerate(es)])
        attn = num / den
        attn_ref[...] = attn
        attn16_ref[...] = attn.astype(BF16)
        lse_ref[...] = top + jnp.log(den)
        pl.when(step == (2 * n_steps) // 3)(forward)
        pl.when(step == n_steps - 1)(finish)

    out = pl.pallas_call(
        body, name=name, grid=(N_SLABS, nc), in_specs=_attn_in_specs(nc, 0) + [ANY] * npay,
        out_specs=[pl.BlockSpec((CHUNK, SLAB), lambda h, c: (c, h))] * 3 + [ANY] * npay,
        out_shape=[jax.ShapeDtypeStruct((S, ATTN_WIDTH), F32)] * 2 + [jax.ShapeDtypeStruct((S, ATTN_WIDTH), BF16)]
        + _gathered_shapes(payload),
        scratch_shapes=[pltpu.VMEM((CHUNK, SLAB), F32)] * (2 * n) + _gather_sems(npay),
        compiler_params=_cp("arbitrary", "arbitrary"),
    )(proj, proj, proj, proj, proj, *payload)
    return (*out[:3], out[3:])


def _attn_bwd(proj, dcat, attn, lse, grads, *, name):
    S = proj.shape[0]
    nc = S // CHUNK
    ng = len(grads)
    n = len(DILATIONS)
    recv_shapes, exchange_sems = _exchange_buffers(grads)

    def body(*refs):
        q_ref, kp_ref, kc_ref, vp_ref, vc_ref, do_ref, o_ref, lse_ref = refs[:8]
        g_refs = refs[8:8 + ng]
        dq_ref, dk_ref, dv_ref = refs[8 + ng:11 + ng]
        r_refs = refs[11 + ng:11 + 2 * ng]
        scr = refs[11 + 2 * ng:]
        dk_prev, dv_prev = scr[:2]
        delta_h, lse_h = scr[2:4], scr[4:6]
        dq_p, dk_own, dk_back, dv_own, dv_back = (scr[6 + n * k:6 + n * (k + 1)] for k in range(5))
        start, finish = _exchange_phases(g_refs, r_refs, *scr[6 + 5 * n:])
        c = pl.program_id(1)
        pl.when((pl.program_id(0) == 0) & (c == 0))(start)

        @pl.when(c == 0)
        def _():
            dk_prev[...] = jnp.zeros_like(dk_prev)
            dv_prev[...] = jnp.zeros_like(dv_prev)

        @pl.when(c < nc)
        def _():
            lms = _lane_masks()
            plain, first = (jnp.tile(_band_mask(f), (2, 1)) for f in (False, c == 0))
            prod = do_ref[...] * o_ref[...]
            lse = lse_ref[...]
            lse_other = pltpu.roll(lse, HEAD_DIM, 1)
            for h, lm in enumerate(lms):
                delta = jnp.sum(jnp.where(lm, prod, 0.0), axis=-1, keepdims=True)
                delta_h[h][...] = jnp.broadcast_to(delta, (CHUNK, SLAB))
                lse_h[h][...] = jnp.where(lm, lse, lse_other)
            wide = lambda refs, rows: jnp.tile(jnp.concatenate([r[rows, :] for r in refs], axis=0), (1, 2))
            stack = lambda f: jnp.concatenate([f(lm) for lm in lms], axis=0)

            def scores(unit):
                p, d, b, r, nbc = unit
                rows = _unit_rows(d, b)(r)
                prow = _unit_rows(d, (b - 1) % nbc)(r)
                kpr, vpr = (kc_ref, vc_ref) if b > 0 else (kp_ref, vp_ref)
                q = q_ref[rows, :].astype(BF16)
                kcat = jnp.concatenate([kpr[prow, :], kc_ref[rows, :]], axis=0).astype(BF16)
                vcat = jnp.concatenate([vpr[prow, :], vc_ref[rows, :]], axis=0).astype(BF16)
                do = do_ref[rows, :]
                q2 = stack(lambda lm: jnp.where(lm, q, jnp.zeros_like(q))) * SCALE
                do2 = stack(lambda lm: jnp.where(lm, do, 0.0)).astype(BF16)
                return dict(p=p, rows=rows, prow=prow, mask2=plain if b > 0 else first, kcat=kcat, q2=q2, do2=do2,
                            s=_dot(q2, kcat, NT), dp=_dot(do2, vcat, NT))

            units = list(_attn_units())
            nxt = scores(units[0])
            for k in range(len(units)):
                u = nxt
                if k + 1 < len(units):
                    nxt = scores(units[k + 1])
                p, rows, prow, kcat = u["p"], u["rows"], u["prow"], u["kcat"]
                e = jnp.where(u["mask2"], jnp.exp(u["s"] - wide(lse_h, rows)), 0.0)
                ds = (e * (u["dp"] - wide(delta_h, rows))).astype(BF16)
                dq = jnp.where(lms[0], _dot(ds[:BLOCK], kcat, NN), _dot(ds[BLOCK:], kcat, NN)) * SCALE
                dkc = _dot(ds, u["q2"], TN)
                dvc = _dot(e.astype(BF16), u["do2"], TN)
                dq_p[p][rows, :] = dq
                dk_own[p][rows, :] = dkc[BLOCK:]
                dv_own[p][rows, :] = dvc[BLOCK:]
                dk_back[p][prow, :] = dkc[:BLOCK]
                dv_back[p][prow, :] = dvc[:BLOCK]
            dq_ref[...] = functools.reduce(jnp.add, [r[...] for r in dq_p]).astype(BF16)
            for prev, own, back, out_ref in ((dk_prev, dk_own, dk_back, dk_ref), (dv_prev, dv_own, dv_back, dv_ref)):
                for p, d in enumerate(DILATIONS):
                    tail = CHUNK - BLOCK * d
                    prev[tail:, :] += back[p][tail:, :]
                out_ref[...] = prev[...].astype(BF16)
                prev[...] = functools.reduce(jnp.add, [r[...] for r in own])
                for p, d in enumerate(DILATIONS):
                    tail = CHUNK - BLOCK * d
                    if tail:
                        prev[:tail, :] += back[p][:tail, :]

        @pl.when(c == nc)
        def _():
            dk_ref[...] = dk_prev[...].astype(BF16)
            dv_ref[...] = dv_prev[...].astype(BF16)

        pl.when((pl.program_id(0) == N_SLABS - 1) & (c == nc))(finish)

    blk = lambda f: pl.BlockSpec((CHUNK, SLAB), f)
    late = lambda h, c: (jnp.maximum(c - 1, 0), h)
    out = pl.pallas_call(
        body, name=name, grid=(N_SLABS, nc + 1), in_specs=_attn_in_specs(nc, 3) + [ANY] * ng,
        out_specs=[blk(lambda h, c: (jnp.minimum(c, nc - 1), h)), blk(late), blk(late)] + [ANY] * ng,
        out_shape=[jax.ShapeDtypeStruct((S, ATTN_WIDTH), BF16)] * 3 + recv_shapes,
        scratch_shapes=[pltpu.VMEM((CHUNK, SLAB), F32)] * (6 + 5 * n) + exchange_sems,
        compiler_params=_cp("arbitrary", "arbitrary"),
    )(proj, proj, proj, proj, proj, dcat, attn, lse, *grads)
    return out[:3], out[3:]


def _split_bf16(a):
    hi = a.astype(BF16)
    lo = (a - hi.astype(F32)).astype(BF16)
    return hi, lo


def _pooled(ug, halo_g, w, row0, tm):
    ext = jnp.concatenate([halo_g, ug], axis=0)
    hi, lo = _split_bf16(ext)
    rr = lax.broadcasted_iota(jnp.int32, (tm, tm + HALO), 0)
    cc = lax.broadcasted_iota(jnp.int32, (tm, tm + HALO), 1)
    back = rr + HALO - cc
    win = ((back >= 0) & (back < w)).astype(BF16)
    wsum = _dot(win, hi, NN) + _dot(win, lo, NN)
    rows = row0 + lax.broadcasted_iota(jnp.int32, (tm, 1), 0)
    inv = 1.0 / jnp.minimum(rows + 1, w).astype(F32)
    return wsum * inv - ug


def _pool_fwd(u, u_col, pool_w, pool_scale, *, name, tm=256):
    S, W = u.shape[0], POOL_WIDTH
    G = POOL_GROUP_DIM

    def body(u_ref, h_ref, w_ref, s_ref, o_ref):
        i = pl.program_id(0)
        uv = u_ref[...]
        halo = jnp.where(i > 0, h_ref[...], 0.0)
        sls = [slice(g * G, (g + 1) * G) for g in range(len(POOL_WINDOWS))]
        pooled = [_pooled(uv[:, sl], halo[:, sl], w, i * tm, tm) for sl, w in zip(sls, POOL_WINDOWS)]
        zs = [_dot(p.astype(BF16), w_ref[g].astype(BF16), NN) for g, p in enumerate(pooled)]
        for sl, z in zip(sls, zs):
            o_ref[:, sl] = (z * s_ref[:, sl]).astype(BF16)

    per = tm // HALO
    return pl.pallas_call(
        body, name=name, grid=(S // tm,),
        in_specs=[pl.BlockSpec((tm, W), lambda i: (i, u_col)),
                  pl.BlockSpec((HALO, W), lambda i: (jnp.maximum(i * per - 1, 0), u_col)),
                  pl.BlockSpec((len(POOL_WINDOWS), G, G), lambda i: (0, 0, 0)),
                  pl.BlockSpec((1, W), lambda i: (0, 0))],
        out_specs=pl.BlockSpec((tm, W), lambda i: (i, 0)),
        out_shape=jax.ShapeDtypeStruct((S, W), BF16),
        compiler_params=_cp("parallel"),
    )(u, u, pool_w, pool_scale)


def _pool_bwd(u, u_col, dy, dy_col, pool_w, pool_scale, *, name, tm=256):
    S, W = u.shape[0], POOL_WIDTH
    G = POOL_GROUP_DIM
    nt = S // tm

    def body(u_ref, h_ref, dy_ref, dyn_ref, w_ref, s_ref, du_ref, gw_ref, gs_ref):
        i = pl.program_id(0)

        @pl.when(i == 0)
        def _():
            gw_ref[...] = jnp.zeros_like(gw_ref)
            gs_ref[...] = jnp.zeros_like(gs_ref)

        uv = u_ref[...]
        halo = jnp.where(i > 0, h_ref[...], 0.0)
        dyv = dy_ref[...]
        dyn = jnp.where(i < nt - 1, dyn_ref[...], 0.0)
        rr = lax.broadcasted_iota(jnp.int32, (tm, tm + HALO), 0)
        cc = lax.broadcasted_iota(jnp.int32, (tm, tm + HALO), 1)
        rows_ext = i * tm + lax.broadcasted_iota(jnp.int32, (tm + HALO, 1), 0)
        groups = list(enumerate(POOL_WINDOWS))
        sls = [slice(g * G, (g + 1) * G) for g, _ in groups]
        wgs = [w_ref[g].astype(BF16) for g, _ in groups]
        pooled = [_pooled(uv[:, sl], halo[:, sl], w, i * tm, tm).astype(BF16) for sl, (_, w) in zip(sls, groups)]
        dzs = [dyv[:, sl] * s_ref[:, sl] for sl in sls]
        dz_ext = [jnp.concatenate([dz, dyn[:, sl] * s_ref[:, sl]], axis=0).astype(BF16) for dz, sl in zip(dzs, sls)]
        dp_ext = [_dot(d, wg, NT) for d, wg in zip(dz_ext, wgs)]
        zs = [_dot(p, wg, NN) for p, wg in zip(pooled, wgs)]
        for (g, w), sl, p, dz, z, dp in zip(groups, sls, pooled, dzs, zs, dp_ext):
            gw_ref[g] += _dot(p, dz.astype(BF16), TN)
            gs_ref[:, sl] += jnp.sum(dyv[:, sl] * z, axis=0, keepdims=True)
            inv_ext = 1.0 / jnp.minimum(rows_ext + 1, w).astype(F32)
            hi, lo = _split_bf16(dp * inv_ext)
            ahead = cc - rr
            win = ((ahead >= 0) & (ahead < w)).astype(BF16)
            du_ref[:, sl] = (_dot(win, hi, NN) + _dot(win, lo, NN) - dp[:tm]).astype(BF16)

    per = tm // HALO
    nh = S // HALO
    return pl.pallas_call(
        body, name=name, grid=(nt,),
        in_specs=[pl.BlockSpec((tm, W), lambda i: (i, u_col)),
                  pl.BlockSpec((HALO, W), lambda i: (jnp.maximum(i * per - 1, 0), u_col)),
                  pl.BlockSpec((tm, W), lambda i: (i, dy_col)),
                  pl.BlockSpec((HALO, W), lambda i: (jnp.minimum((i + 1) * per, nh - 1), dy_col)),
                  pl.BlockSpec((len(POOL_WINDOWS), G, G), lambda i: (0, 0, 0)),
                  pl.BlockSpec((1, W), lambda i: (0, 0))],
        out_specs=[pl.BlockSpec((tm, W), lambda i: (i, 0)),
                   pl.BlockSpec((len(POOL_WINDOWS), G, G), lambda i: (0, 0, 0)),
                   pl.BlockSpec((1, W), lambda i: (0, 0))],
        out_shape=[jax.ShapeDtypeStruct((S, W), BF16),
                   jax.ShapeDtypeStruct((len(POOL_WINDOWS), G, G), F32),
                   jax.ShapeDtypeStruct((1, W), F32)],
        compiler_params=_cp("arbitrary"),
    )(u, u, dy, dy, pool_w, pool_scale)


GELU_K0 = math.sqrt(2.0 / math.pi)
GELU_K1 = 0.044715


def _gelu_parts(x):
    x2 = x * x
    t = jnp.tanh(x * (GELU_K0 + (GELU_K0 * GELU_K1) * x2))
    hp = 0.5 + 0.5 * t
    gelu = x * hp
    dgelu = hp + (x * (hp * (1.0 - t))) * (GELU_K0 + (3.0 * GELU_K0 * GELU_K1) * x2)
    return gelu, dgelu


def _shifted(ext, halo):
    return (pltpu.roll(ext, 2, 0)[halo:], pltpu.roll(ext, 1, 0)[halo:], ext[halo:])


def _conv(sh, w, b):
    return b + (sh[0] * w[0:1] + sh[1] * w[1:2] + sh[2] * w[2:3])


F32_ROWS = 8


def _ffn_up_glu(h, w_up_t, conv_w, conv_b, *, name, tm=2048, tn=256, sub=256):
    S, K = h.shape
    F = D_FF
    nj = F // tn

    def body(h_ref, wg_ref, wv_ref, cwg_ref, cwv_ref, cbg_ref, cbv_ref,
             ug_ref, uv_ref, cg_ref, cv_ref, y_ref, carry):
        i = pl.program_id(0)
        j = pl.program_id(1)

        w_cat = jnp.concatenate([wg_ref[...], wv_ref[...]], axis=0)
        conv_w_b = ((cwg_ref[...], cbg_ref[...]), (cwv_ref[...], cbv_ref[...]))
        halo = [jnp.where(i > 0, carry[j, s], 0.0) for s in range(2)]
        u_next = _dot(h_ref[0:sub, :], w_cat, NT)
        for a in range(0, tm, sub):
            u16 = u_next.astype(BF16)
            if a + sub < tm:
                u_next = _dot(h_ref[a + sub:a + 2 * sub, :], w_cat, NT)
            ug_ref[a:a + sub, :] = u16[:, :tn]
            uv_ref[a:a + sub, :] = u16[:, tn:]
            c = []
            for s, (cw, cb) in enumerate(conv_w_b):
                u = u16[:, s * tn:(s + 1) * tn].astype(F32)
                ext = jnp.concatenate([halo[s], u], axis=0)
                c.append(_conv(_shifted(ext, F32_ROWS), cw, cb))
                halo[s] = u[sub - F32_ROWS:]
            cg_ref[a:a + sub, :] = c[0].astype(BF16)
            cv_ref[a:a + sub, :] = c[1].astype(BF16)
            gelu, _ = _gelu_parts(c[0])
            y_ref[a:a + sub, :] = (gelu * c[1]).astype(BF16)
        for s in range(2):
            carry[j, s] = halo[s]

    tile = pl.BlockSpec((tm, tn), lambda i, j: (i, j))
    vec = lambda rows, off: pl.BlockSpec((rows, tn), lambda i, j: (0, j + off))
    return pl.pallas_call(
        body, name=name, grid=(S // tm, nj),
        in_specs=[pl.BlockSpec((tm, K), lambda i, j: (i, 0)),
                  pl.BlockSpec((tn, K), lambda i, j: (j, 0)), pl.BlockSpec((tn, K), lambda i, j: (j + nj, 0)),
                  vec(3, 0), vec(3, nj), vec(1, 0), vec(1, nj)],
        out_specs=[tile] * 5,
        out_shape=[jax.ShapeDtypeStruct((S, F), BF16)] * 5,
        scratch_shapes=[pltpu.VMEM((nj, 2, F32_ROWS, tn), F32)],
        compiler_params=_cp("arbitrary", "arbitrary"),
    )(h, w_up_t, w_up_t, conv_w, conv_w, conv_b, conv_b)


def _ffn_glu_bwd(u_g, u_v, c_g, c_v, df, w_down, h, conv_w, *, name, tm=2048, tn=256, sub=256):
    S = u_g.shape[0]
    F = D_FF
    D = df.shape[1]
    nj = F // tn
    nt = S // tm

    def body(ug_ref, uv_ref, cg_ref, cgn_ref, cv_ref, cvn_ref, df_ref, dfn_ref, wd_ref, h_ref, wg_ref, wv_ref,
             dug_ref, duv_ref, gug_ref, guv_ref, gd_ref, gwg_ref, gwv_ref, gbg_ref, gbv_ref,
             acc_u, acc_d):
        i = pl.program_id(1)

        @pl.when(i == 0)
        def _():
            for r in (gwg_ref, gwv_ref, gbg_ref, gbv_ref, acc_u, acc_d):
                r[...] = jnp.zeros_like(r)

        wg, wv = wg_ref[...], wv_ref[...]
        wd = wd_ref[...]
        dfn = jnp.where(i < nt - 1, dfn_ref[...], jnp.zeros_like(dfn_ref))
        n_ext = sub + HALO

        def ahead(dc):
            return dc[:sub], pltpu.roll(dc, n_ext - 1, 0)[:sub], pltpu.roll(dc, n_ext - 2, 0)[:sub]

        def ext(ref, nxt, a):
            b = a + sub
            return jnp.concatenate([ref[a:b, :], ref[b:b + HALO, :] if b < tm else nxt], axis=0)

        dy_next = _dot(ext(df_ref, dfn, 0), wd, NT)
        for a in range(0, tm, sub):
            b = a + sub
            dy_ext = dy_next
            if b < tm:
                dy_next = _dot(ext(df_ref, dfn, b), wd, NT)
            cg = ext(cg_ref, cgn_ref[...], a).astype(F32)
            cv = ext(cv_ref, cvn_ref[...], a).astype(F32)
            df_sub = df_ref[a:b, :]
            gelu, dgelu = _gelu_parts(cg)
            dcs_g = ahead(dy_ext * cv * dgelu)
            dcs_v = ahead(dy_ext * gelu)
            du_g = (dcs_g[0] * wg[2:3] + dcs_g[1] * wg[1:2] + dcs_g[2] * wg[0:1]).astype(BF16)
            du_v = (dcs_v[0] * wv[2:3] + dcs_v[1] * wv[1:2] + dcs_v[2] * wv[0:1]).astype(BF16)
            dug_ref[a:b, :] = du_g
            duv_ref[a:b, :] = du_v
            acc_u[...] += _dot(jnp.concatenate([du_g, du_v], axis=1), h_ref[a:b, :], TN)
            acc_d[...] += _dot((gelu[:sub] * cv[:sub]).astype(BF16), df_sub, TN)
            for dcs, u_ref, gw_ref, gb_ref in ((dcs_g, ug_ref, gwg_ref, gbg_ref), (dcs_v, uv_ref, gwv_ref, gbv_ref)):
                u = u_ref[a:b, :].astype(F32)
                gb_ref[...] += jnp.sum(dcs[0], axis=0, keepdims=True)
                for k in range(3):
                    gw_ref[k:k + 1, :] += jnp.sum(dcs[2 - k] * u, axis=0, keepdims=True)

        @pl.when(i == nt - 1)
        def _():
            gug_ref[...] = acc_u[:tn, :].astype(BF16)
            guv_ref[...] = acc_u[tn:, :].astype(BF16)
            gd_ref[...] = acc_d[...].astype(BF16)

    per = tm // HALO
    nh = S // HALO
    hnext = lambda i: jnp.minimum((i + 1) * per, nh - 1)
    tile = pl.BlockSpec((tm, tn), lambda j, i: (i, j))
    hn = pl.BlockSpec((HALO, tn), lambda j, i: (hnext(i), j))
    vec = lambda rows, off: pl.BlockSpec((rows, tn), lambda j, i: (0, j + off))
    wide = pl.BlockSpec((tm, D), lambda j, i: (i, 0))
    wrow = pl.BlockSpec((tn, D), lambda j, i: (j, 0))
    return pl.pallas_call(
        body, name=name, grid=(nj, nt),
        in_specs=[tile, tile, tile, hn, tile, hn, wide, pl.BlockSpec((HALO, D), lambda j, i: (hnext(i), 0)),
                  wrow, wide, vec(3, 0), vec(3, nj)],
        out_specs=[tile, tile, wrow, wrow, wrow, vec(3, 0), vec(3, 0), vec(1, 0), vec(1, 0)],
        out_shape=[jax.ShapeDtypeStruct((S, F), BF16), jax.ShapeDtypeStruct((S, F), BF16),
                   jax.ShapeDtypeStruct((F, D), BF16), jax.ShapeDtypeStruct((F, D), BF16),
                   jax.ShapeDtypeStruct((F, D), BF16),
                   jax.ShapeDtypeStruct((3, F), F32), jax.ShapeDtypeStruct((3, F), F32),
                   jax.ShapeDtypeStruct((1, F), F32), jax.ShapeDtypeStruct((1, F), F32)],
        scratch_shapes=[pltpu.VMEM((2 * tn, D), F32), pltpu.VMEM((tn, D), F32)],
        compiler_params=_cp("parallel", "arbitrary"),
    )(u_g, u_v, c_g, c_g, c_v, c_v, df, df, w_down, h, conv_w, conv_w)


def _sum_partials(parts, *, name, tr):
    _, R, C = parts.shape

    def body(p_ref, o_ref):
        tot = p_ref[0].astype(F32)
        for j in range(1, N_DEV):
            tot = tot + p_ref[j].astype(F32)
        o_ref[...] = tot

    return pl.pallas_call(
        body, name=name, grid=(R // tr,),
        in_specs=[pl.BlockSpec((N_DEV, tr, C), lambda i: (0, i, 0))],
        out_specs=pl.BlockSpec((tr, C), lambda i: (i, 0)),
        out_shape=jax.ShapeDtypeStruct((R, C), F32),
        compiler_params=_cp("parallel"),
    )(parts)


def _adamw(w, g, m, v, *, name, tr):
    R, C = w.shape
    c1 = 1.0 - ADAM_B1 ** ADAM_STEP
    c2 = 1.0 - ADAM_B2 ** ADAM_STEP

    def body(w_ref, g_ref, m_ref, v_ref, d_ref, nm_ref, nv_ref):
        g = g_ref[...]
        nm = ADAM_B1 * m_ref[...] + (1.0 - ADAM_B1) * g
        nv = ADAM_B2 * v_ref[...] + (1.0 - ADAM_B2) * (g * g)
        d_ref[...] = -ADAM_LR * ((nm / c1) / (jnp.sqrt(nv / c2) + ADAM_EPS) + ADAM_WD * w_ref[...])
        nm_ref[...] = nm
        nv_ref[...] = nv

    spec = pl.BlockSpec((tr, C), lambda i: (i, 0))
    return pl.pallas_call(
        body, name=name, grid=(R // tr,), in_specs=[spec] * 4, out_specs=[spec] * 3,
        out_shape=[jax.ShapeDtypeStruct((R, C), F32)] * 3,
        compiler_params=_cp("parallel"),
    )(w, g, m, v)


def _mesh_pos():
    return lax.axis_index("x"), lax.axis_index("y"), lax.axis_index("c")


def _gather_phases(x_refs, out_refs, send_sems, recv_sems, local_sems):
    x, y, c = _mesh_pos()
    me, sibling = (x, y, c), (x, y, 1 - c)
    chips = [(1 - x, y), (x, 1 - y), (1 - x, 1 - y)]
    arrays = range(len(x_refs))

    def slot(a, px, py, pc):
        return out_refs[a].at[4 * px + 2 * py + pc]

    def copy(a, k, block, to, own=False):
        return pltpu.make_async_remote_copy(
            src_ref=x_refs[a] if own else slot(a, *block), dst_ref=slot(a, *block),
            send_sem=send_sems.at[a, k], recv_sem=recv_sems.at[a, k], device_id=to, device_id_type=MESH)

    mine = [pltpu.make_async_copy(x_refs[a], slot(a, *me), local_sems.at[a]) for a in arrays]
    first = [copy(a, 0, me, sibling, own=True) for a in arrays]
    first += [copy(a, 1 + j, me, (*chip, c), own=True) for j, chip in enumerate(chips) for a in arrays]
    passed = [[copy(a, 4 + j, (*chip, c), sibling) for a in arrays] for j, chip in enumerate(chips)]

    def start():
        for cp in mine + first:
            cp.start()

    def forward():
        for j, chip in enumerate(chips):
            for a in arrays:
                copy(a, 1 + j, (*chip, c), me).wait_recv()
                passed[j][a].start()

    def finish():
        for a in arrays:
            copy(a, 0, sibling, me).wait_recv()
            for j, chip in enumerate(chips):
                copy(a, 4 + j, (*chip, 1 - c), me).wait_recv()
        for cp in first + [cp for row in passed for cp in row]:
            cp.wait_send()
        for cp in mine:
            cp.wait()

    return start, forward, finish


def _gather_sems(n):
    return [pltpu.SemaphoreType.DMA((n, 7)), pltpu.SemaphoreType.DMA((n, 7)), pltpu.SemaphoreType.DMA((n,))]


def _gathered_shapes(blocks):
    return [jax.ShapeDtypeStruct((N_DEV,) + b.shape, b.dtype) for b in blocks]


def _all_reduce_small(block, *, name):
    def body(x_ref, all_ref, sum_ref, *sems):
        for phase in _gather_phases([x_ref], [all_ref], *sems):
            phase()
        tot = all_ref[0]
        for j in range(1, N_DEV):
            tot = tot + all_ref[j]
        sum_ref[...] = tot

    return pl.pallas_call(
        body, name=name, in_specs=[VMEM], out_specs=[VMEM, VMEM],
        out_shape=[jax.ShapeDtypeStruct((N_DEV,) + block.shape, block.dtype),
                   jax.ShapeDtypeStruct(block.shape, block.dtype)],
        scratch_shapes=_gather_sems(1),
        compiler_params=pltpu.CompilerParams(vmem_limit_bytes=V7X_VMEM_LIMIT),
    )(block)[1]


def _exchange_phases(g_refs, r_refs, send_sems, recv_sems, local_sems):
    x, y, c = _mesh_pos()
    me = 4 * x + 2 * y + c
    owns, remote = [], []
    for k, (g_ref, r_ref) in enumerate(zip(g_refs, r_refs)):
        rows = g_ref.shape[0] // N_DEV
        owns.append(pltpu.make_async_copy(g_ref.at[pl.ds(me * rows, rows)], r_ref.at[me], local_sems.at[k]))
        for p in range(1, N_DEV):
            px, py, pc = x ^ (p >> 2), y ^ ((p >> 1) & 1), c ^ (p & 1)
            peer = 4 * px + 2 * py + pc
            link = dict(send_sem=send_sems.at[k, p], recv_sem=recv_sems.at[k, p],
                        device_id=(px, py, pc), device_id_type=MESH)
            src = g_ref.at[pl.ds(peer * rows, rows)]
            send = pltpu.make_async_remote_copy(src_ref=src, dst_ref=r_ref.at[me], **link)
            arrival = pltpu.make_async_remote_copy(src_ref=src, dst_ref=r_ref.at[peer], **link)
            remote.append((send, arrival))

    def start():
        for own in owns:
            own.start()
        for send, _ in remote:
            send.start()

    def finish():
        for _, arrival in remote:
            arrival.wait_recv()
        for send, _ in remote:
            send.wait_send()
        for own in owns:
            own.wait()

    return start, finish


def _exchange_buffers(grads):
    n = len(grads)
    shapes = [jax.ShapeDtypeStruct((N_DEV, g.shape[0] // N_DEV, g.shape[1]), g.dtype) for g in grads]
    sems = [pltpu.SemaphoreType.DMA((n, N_DEV)), pltpu.SemaphoreType.DMA((n, N_DEV)),
            pltpu.SemaphoreType.DMA((n,))]
    return shapes, sems


def _unpack_gathered(gathered):
    w_out, w_up_t, w_down = (g.reshape(-1, D_MODEL) for g in gathered[:3])
    width = 2 * D_FF // N_DEV
    conv_w = jnp.transpose(gathered[3][:, :3, :width], (1, 0, 2)).reshape(3, 2 * D_FF)
    return w_out, w_up_t, w_down, conv_w


def _rest_payload(w_out, w_up, w_down, conv_w):
    rows, cols = conv_w.shape
    conv_w = jnp.pad(conv_w, ((0, (-rows) % F32_ROWS), (0, (-cols) % LANES)))
    return [w_out.astype(BF16), w_up.T.astype(BF16), w_down.astype(BF16), conv_w]


def _device_step(x, target, g_mix_pre, w_in_t_block, rest_payload, pool_w, pool_scale, g_mix_post, g_ffn_pre,
                 conv_b, g_ffn_post):
    h1, w_in_t = _rms_norm_gather(x, g_mix_pre, w_in_t_block, name="rms_mix_pre")
    w_in_t = w_in_t.reshape(-1, D_MODEL)
    proj = _matmul(h1, w_in_t, trans_b=True, out_dtype=F32, tm=512, tn=4 * ATTN_WIDTH, name="proj")
    attn, lse, attn16, gathered = _attn_fwd(proj, rest_payload, name="attn_fwd")
    w_out, w_up_t, w_down, conv_w = _unpack_gathered(gathered)
    pool = _pool_fwd(proj, 3, pool_w, pool_scale, name="pool_fwd")
    mixed, x2, h2 = _mix_out(attn16, pool, w_out, x, g_mix_post, g_ffn_pre, name="mix_out")
    u_g, u_v, c_g, c_v, y = _ffn_up_glu(h2, w_up_t, conv_w, conv_b, name="ffn_up_glu")
    df, d_out, loss_blk, gg_ffn_post = _ffn_out(y, w_down, x2, target, g_ffn_post, name="ffn_out")
    du_g, du_v, gw_up_g, gw_up_v, gw_down, gcw_g, gcw_v, gcb_g, gcb_v = _ffn_glu_bwd(
        u_g, u_v, c_g, c_v, df, w_down, h2, conv_w, name="ffn_glu_bwd")
    gw_up_t = jnp.concatenate([gw_up_g, gw_up_v], axis=0)
    dx2, gg_ffn_pre, dmixed, gg_mix_post = _dgrad_norm(
        [du_g, du_v], w_up_t, d_out, x2, g_ffn_pre, (mixed, g_mix_post), [], name="ffn_up_dgrad")
    gw_out = jnp.concatenate([_matmul_tn(attn16, dmixed, ta=512, ts=1024, name="grad_w_out_attn"),
                              _matmul_tn(pool, dmixed, ta=512, ts=1024, name="grad_w_out_pool")], axis=0)
    dcat = _matmul(dmixed, w_out, trans_b=True, out_dtype=F32, tm=512, tn=1024, name="mix_out_dgrad")
    d_pool_in, g_pool_w, g_pool_scale = _pool_bwd(proj, 3, dcat, 1, pool_w, pool_scale, name="pool_bwd")
    dqkv, (r_out, r_up_t, r_down) = _attn_bwd(proj, dcat, attn, lse, [gw_out, gw_up_t, gw_down], name="attn_bwd")
    dproj = list(dqkv) + [d_pool_in]
    gw_in_t = jnp.concatenate([_matmul_tn(a, h1, ta=512, ts=1024, name=f"grad_w_in_{k}")
                               for k, a in enumerate(dproj)], axis=0)
    grad_x, gg_mix_pre, (r_in_t,) = _dgrad_norm(dproj, w_in_t, dx2, x, g_mix_pre, None, [gw_in_t], name="proj_dgrad")
    g_conv_w = jnp.concatenate([gcw_g, gcw_v], axis=1)
    g_conv_b = jnp.concatenate([gcb_g, gcb_v], axis=1)
    received = (r_in_t, r_out, r_up_t, r_down)
    small = dict(g_mix_pre=gg_mix_pre, g_mix_post=gg_mix_post, g_ffn_pre=gg_ffn_pre, g_ffn_post=gg_ffn_post,
                 pool_scale=g_pool_scale, conv_b=g_conv_b, pool_w=g_pool_w, conv_w=g_conv_w)
    return loss_blk, grad_x, received, small


_SMALL = ("g_mix_pre", "g_mix_post", "g_ffn_pre", "g_ffn_post", "pool_scale", "conv_b", "pool_w")
LANES = 128


def _pack_rows(arrays):
    parts = []
    for a in arrays:
        a2 = a.reshape(-1, LANES)
        parts.append(jnp.pad(a2, ((0, (-a2.shape[0]) % 8), (0, 0))))
    return jnp.concatenate(parts, axis=0)


def _unpack_rows(packed, shapes):
    out, row = [], 0
    for shape in shapes:
        rows = math.prod(shape) // LANES
        out.append(packed[row:row + rows].reshape(shape))
        row += -(-rows // 8) * 8
    return out


def kernel(x, g_mix_pre, w_in, pool_w, pool_scale, w_out, g_mix_post, g_ffn_pre, w_up, conv_w, conv_b, w_down, g_ffn_post, loss_target, m_g_mix_pre, m_w_in, m_pool_w, m_pool_scale, m_w_out, m_g_mix_post, m_g_ffn_pre, m_w_up, m_conv_w, m_conv_b, m_w_down, m_g_ffn_post, v_g_mix_pre, v_w_in, v_pool_w, v_pool_scale, v_w_out, v_g_mix_post, v_g_ffn_pre, v_w_up, v_conv_w, v_conv_b, v_w_down, v_g_ffn_post):
    me = 4 * lax.axis_index("x") + 2 * lax.axis_index("y") + lax.axis_index("c")
    loss_blk, grad_x, recv, small = _device_step(
        x[0], loss_target[0], g_mix_pre, w_in[0].T.astype(BF16),
        _rest_payload(w_out[0], w_up[0], w_down[0], conv_w[0]),
        pool_w[0], pool_scale, g_mix_post, g_ffn_pre, conv_b, g_ffn_post)

    g_in_t, g_out, g_up_t, g_down = (
        _sum_partials(r, name=f"sum_partials_{k}", tr=r.shape[1] // 2) for k, r in enumerate(recv))
    grads = {"w_in": g_in_t.T, "w_out": g_out, "w_up": g_up_t.T, "w_down": g_down}

    given = dict(g_mix_pre=g_mix_pre, g_mix_post=g_mix_post, g_ffn_pre=g_ffn_pre, g_ffn_post=g_ffn_post,
                 pool_scale=pool_scale, conv_b=conv_b, pool_w=pool_w)
    small_shapes = [given[k].shape for k in _SMALL]
    total = _all_reduce_small(_pack_rows([small[k] for k in _SMALL] + [small["conv_w"], loss_blk]),
                              name="all_reduce_small")
    *small_grads, g_conv_w_all, loss_all = _unpack_rows(total, small_shapes + [(3, 2 * D_FF), loss_blk.shape])
    loss = loss_all[0, 0]
    grads.update(zip(_SMALL, small_grads))
    width = 2 * D_FF // N_DEV
    grads["conv_w"] = lax.dynamic_slice_in_dim(g_conv_w_all, me * width, width, axis=1)[None]

    weights = dict(g_mix_pre=g_mix_pre, w_in=w_in, pool_w=pool_w, pool_scale=pool_scale, w_out=w_out,
                   g_mix_post=g_mix_post, g_ffn_pre=g_ffn_pre, w_up=w_up, conv_w=conv_w, conv_b=conv_b,
                   w_down=w_down, g_ffn_post=g_ffn_post)
    m_in = dict(g_mix_pre=m_g_mix_pre, w_in=m_w_in, pool_w=m_pool_w, pool_scale=m_pool_scale, w_out=m_w_out,
                g_mix_post=m_g_mix_post, g_ffn_pre=m_g_ffn_pre, w_up=m_w_up, conv_w=m_conv_w, conv_b=m_conv_b,
                w_down=m_w_down, g_ffn_post=m_g_ffn_post)
    v_in = dict(g_mix_pre=v_g_mix_pre, w_in=v_w_in, pool_w=v_pool_w, pool_scale=v_pool_scale, w_out=v_w_out,
                g_mix_post=v_g_mix_post, g_ffn_pre=v_g_ffn_pre, w_up=v_w_up, conv_w=v_conv_w, conv_b=v_conv_b,
                w_down=v_w_down, g_ffn_post=v_g_ffn_post)
    delta, new_m, new_v = {}, {}, {}
    for k in ("w_in", "w_out", "w_up", "w_down"):
        g = grads[k]
        d, nm, nv = _adamw(weights[k][0], g, m_in[k][0], v_in[k][0], name=f"adamw_{k}", tr=g.shape[0] // 2)
        grads[k], delta[k], new_m[k], new_v[k] = g[None], d[None], nm[None], nv[None]
    d, nm, nv = _adamw(weights["conv_w"][0], grads["conv_w"][0], m_in["conv_w"][0], v_in["conv_w"][0],
                       name="adamw_conv_w", tr=3)
    delta["conv_w"], new_m["conv_w"], new_v["conv_w"] = d[None], nm[None], nv[None]
    packed_w = _pack_rows([weights[k] for k in _SMALL])
    small_rows = packed_w.shape[0]
    d, nm, nv = _adamw(packed_w, total[:small_rows], _pack_rows([m_in[k] for k in _SMALL]),
                       _pack_rows([v_in[k] for k in _SMALL]), name="adamw_small", tr=small_rows)
    for k, dk, mk, vk in zip(_SMALL, _unpack_rows(d, small_shapes), _unpack_rows(nm, small_shapes),
                             _unpack_rows(nv, small_shapes)):
        delta[k], new_m[k], new_v[k] = dk, mk, vk

    order = ("g_mix_pre", "w_in", "pool_w", "pool_scale", "w_out", "g_mix_post", "g_ffn_pre", "w_up",
             "conv_w", "conv_b", "w_down", "g_ffn_post")
    return (loss, grad_x[None], *[grads[k] for k in order], *[delta[k] for k in order],
            *[new_m[k] for k in order], *[new_v[k] for k in order])
```

```python
import functools
import math

import jax
import jax.numpy as jnp
from jax import lax
from jax.experimental import pallas as pl
from jax.experimental.pallas import tpu as pltpu

F32 = jnp.float32
BF16 = jnp.bfloat16

D_MODEL = 1024
N_HEADS = 8
HEAD_DIM = 64
ATTN_WIDTH = N_HEADS * HEAD_DIM
DILATIONS = (1, 4, 16)
BLOCK = 128
POOL_WIDTH = 512
POOL_WINDOWS = (2, 4, 8, 16)
POOL_GROUP_DIM = 128
D_FF = 2816
EPS = 1e-6
NEG_INF = -1e30
SCALE = HEAD_DIM ** -0.5

ADAM_LR = 0.001
ADAM_B1 = 0.9
ADAM_B2 = 0.999
ADAM_EPS = 1e-08
ADAM_WD = 0.01
ADAM_STEP = 10

N_DEV = 8
HALO = 16
V7X_VMEM_LIMIT = 56 * 1024 * 1024

MESH = pl.DeviceIdType.MESH
ANY = pl.BlockSpec(memory_space=pl.ANY)
VMEM = pl.BlockSpec(memory_space=pltpu.VMEM)

NT = (((1,), (1,)), ((), ()))
NN = (((1,), (0,)), ((), ()))
TN = (((0,), (0,)), ((), ()))


def _cp(*sem):
    return pltpu.CompilerParams(dimension_semantics=sem, vmem_limit_bytes=V7X_VMEM_LIMIT)


def _dot(a, b, dn):
    return lax.dot_general(a, b, dn, preferred_element_type=F32)


def _rms_bwd(xin, g, dy):
    r = lax.rsqrt(jnp.mean(xin * xin, axis=-1, keepdims=True) + EPS)
    xh = xin * r
    gdy = g * dy
    dx = r * (gdy - xh * jnp.mean(gdy * xh, axis=-1, keepdims=True))
    dg = jnp.sum(dy * xh, axis=0, keepdims=True)
    return dx, dg


def _rms_norm_gather(x, g, block, *, name, tm=512):
    S, D = x.shape
    nt = S // tm

    def body(x_ref, g_ref, blk_ref, o_ref, all_ref, *sems):
        i = pl.program_id(0)
        start, forward, finish = _gather_phases([blk_ref], [all_ref], *sems)
        pl.when(i == 0)(start)
        xv = x_ref[...]
        r = lax.rsqrt(jnp.mean(xv * xv, axis=-1, keepdims=True) + EPS)
        o_ref[...] = (xv * r * g_ref[...]).astype(BF16)
        pl.when(i == (2 * nt) // 3)(forward)
        pl.when(i == nt - 1)(finish)

    return pl.pallas_call(
        body, name=name, grid=(nt,),
        in_specs=[pl.BlockSpec((tm, D), lambda i: (i, 0)), pl.BlockSpec((1, D), lambda i: (0, 0)), ANY],
        out_specs=[pl.BlockSpec((tm, D), lambda i: (i, 0)), ANY],
        out_shape=[jax.ShapeDtypeStruct((S, D), BF16)] + _gathered_shapes([block]),
        scratch_shapes=_gather_sems(1),
        compiler_params=_cp("arbitrary"),
    )(x, g, block)


def _matmul(a, b, *, trans_b, out_dtype, tm, tn, name):
    M, K = a.shape
    N = b.shape[0] if trans_b else b.shape[1]
    dn = NT if trans_b else NN

    def body(a_ref, b_ref, o_ref):
        o_ref[...] = _dot(a_ref[...], b_ref[...], dn).astype(out_dtype)

    b_spec = (pl.BlockSpec((tn, K), lambda i, j: (j, 0)) if trans_b
              else pl.BlockSpec((K, tn), lambda i, j: (0, j)))
    return pl.pallas_call(
        body, name=name, grid=(M // tm, N // tn),
        in_specs=[pl.BlockSpec((tm, K), lambda i, j: (i, 0)), b_spec],
        out_specs=pl.BlockSpec((tm, tn), lambda i, j: (i, j)),
        out_shape=jax.ShapeDtypeStruct((M, N), out_dtype),
        compiler_params=_cp("parallel", "parallel"),
    )(a, b)


def _matmul_tn(a_list, b, *, name, ts=1024):
    S, Ka = a_list[0].shape
    na = len(a_list)
    Nb = b.shape[1]
    ns = S // ts

    def body(*refs):
        a_refs, b_ref, o_ref, acc = refs[:na], refs[na], refs[na + 1], refs[na + 2]
        s = pl.program_id(0)

        @pl.when(s == 0)
        def _():
            acc[...] = jnp.zeros_like(acc)

        acc[...] += _dot(jnp.concatenate([r[...] for r in a_refs], axis=1), b_ref[...], TN)

        @pl.when(s == ns - 1)
        def _():
            o_ref[...] = acc[...].astype(BF16)

    return pl.pallas_call(
        body, name=name, grid=(ns,),
        in_specs=[pl.BlockSpec((ts, Ka), lambda s: (s, 0))] * na + [pl.BlockSpec((ts, Nb), lambda s: (s, 0))],
        out_specs=pl.BlockSpec((na * Ka, Nb), lambda s: (0, 0)),
        out_shape=jax.ShapeDtypeStruct((na * Ka, Nb), BF16),
        scratch_shapes=[pltpu.VMEM((na * Ka, Nb), F32)],
        compiler_params=_cp("arbitrary"),
    )(*a_list, b)


def _mix_out(attn, pool, w_out, x, g_post, g_next, *, name, tm=256):
    S, K = attn.shape
    D = w_out.shape[1]

    def body(a_ref, p_ref, w_ref, x_ref, gp_ref, gn_ref, mixed_ref, x2_ref, h2_ref):
        mixed = _dot(a_ref[...], w_ref[:K, :], NN) + _dot(p_ref[...], w_ref[K:, :], NN)
        r = lax.rsqrt(jnp.mean(mixed * mixed, axis=-1, keepdims=True) + EPS)
        x2 = x_ref[...] + mixed * r * gp_ref[...]
        r2 = lax.rsqrt(jnp.mean(x2 * x2, axis=-1, keepdims=True) + EPS)
        mixed_ref[...] = mixed
        x2_ref[...] = x2
        h2_ref[...] = (x2 * r2 * gn_ref[...]).astype(BF16)

    row = lambda i: (i, 0)
    fix = lambda i: (0, 0)
    return pl.pallas_call(
        body, name=name, grid=(S // tm,),
        in_specs=[pl.BlockSpec((tm, K), row), pl.BlockSpec((tm, K), row), pl.BlockSpec((2 * K, D), fix),
                  pl.BlockSpec((tm, D), row), pl.BlockSpec((1, D), fix), pl.BlockSpec((1, D), fix)],
        out_specs=[pl.BlockSpec((tm, D), row)] * 3,
        out_shape=[jax.ShapeDtypeStruct((S, D), F32), jax.ShapeDtypeStruct((S, D), F32),
                   jax.ShapeDtypeStruct((S, D), BF16)],
        compiler_params=_cp("parallel"),
    )(attn, pool, w_out, x, g_post, g_next)


def _ffn_out(y, w_down, x2, target, g_post, *, name, tm=512, sub=256):
    S, K = y.shape
    D = w_down.shape[1]

    def body(y_ref, w_ref, x2_ref, t_ref, g_ref, df_ref, dout_ref, loss_ref, gg_ref):
        i = pl.program_id(0)

        @pl.when(i == 0)
        def _():
            loss_ref[...] = jnp.zeros_like(loss_ref)
            gg_ref[...] = jnp.zeros_like(gg_ref)

        g = g_ref[...]
        w = w_ref[...]
        f_next = _dot(y_ref[0:sub, :], w, NN)
        for a in range(0, tm, sub):
            rows = slice(a, a + sub)
            f = f_next
            if a + sub < tm:
                f_next = _dot(y_ref[a + sub:a + 2 * sub, :], w, NN)
            r = lax.rsqrt(jnp.mean(f * f, axis=-1, keepdims=True) + EPS)
            out = x2_ref[rows, :] + f * r * g
            err = out - t_ref[rows, :]
            dy = err * (1.0 / D)
            df, dg = _rms_bwd(f, g, dy)
            df_ref[rows, :] = df.astype(BF16)
            dout_ref[rows, :] = dy
            gg_ref[...] += dg
            loss_ref[...] += 0.5 * jnp.sum(jnp.mean(err * err, axis=-1, keepdims=True))

    row = lambda i: (i, 0)
    fix = lambda i: (0, 0)
    return pl.pallas_call(
        body, name=name, grid=(S // tm,),
        in_specs=[pl.BlockSpec((tm, K), row), pl.BlockSpec((K, D), fix), pl.BlockSpec((tm, D), row),
                  pl.BlockSpec((tm, D), row), pl.BlockSpec((1, D), fix)],
        out_specs=[pl.BlockSpec((tm, D), row), pl.BlockSpec((tm, D), row),
                   pl.BlockSpec((8, 128), fix), pl.BlockSpec((1, D), fix)],
        out_shape=[jax.ShapeDtypeStruct((S, D), BF16), jax.ShapeDtypeStruct((S, D), F32),
                   jax.ShapeDtypeStruct((8, 128), F32), jax.ShapeDtypeStruct((1, D), F32)],
        compiler_params=_cp("arbitrary"),
    )(y, w_down, x2, target, g_post)


def _dgrad_norm(a_list, w, resid, xin, g, second, exchange, *, name, tm=512, sub=256):
    S, Kp = a_list[0].shape
    na = len(a_list)
    D = w.shape[1]
    nt = S // tm
    two = second is not None
    ng = len(exchange)
    recv_shapes, exchange_sems = _exchange_buffers(exchange)

    def body(*refs):
        a_refs = refs[:na]
        w_ref, r_ref, x_ref, g_ref = refs[na:na + 4]
        pos = na + 4
        if two:
            x2_ref, g2_ref = refs[pos:pos + 2]
            pos += 2
        g_refs = refs[pos:pos + ng]
        pos += ng
        dx_ref, gg_ref = refs[pos:pos + 2]
        pos += 2
        if two:
            d2_ref, gg2_ref = refs[pos:pos + 2]
            pos += 2
        r_refs = refs[pos:pos + ng]
        pos += ng
        i = pl.program_id(0)
        if ng:
            start, finish = _exchange_phases(g_refs, r_refs, *refs[pos:])
            pl.when(i == 0)(start)

        @pl.when(i == 0)
        def _():
            gg_ref[...] = jnp.zeros_like(gg_ref)
            if two:
                gg2_ref[...] = jnp.zeros_like(gg2_ref)

        def dh_of(a):
            return functools.reduce(jnp.add, [_dot(a_refs[q][a:a + sub, :], w_ref[q * Kp:(q + 1) * Kp, :], NN)
                                              for q in range(na)])

        dh_next = dh_of(0)
        for a in range(0, tm, sub):
            rows = slice(a, a + sub)
            dh = dh_next
            if a + sub < tm:
                dh_next = dh_of(a + sub)
            d1, dg1 = _rms_bwd(x_ref[rows, :], g_ref[...], dh)
            dx = r_ref[rows, :] + d1
            dx_ref[rows, :] = dx
            gg_ref[...] += dg1
            if two:
                d2, dg2 = _rms_bwd(x2_ref[rows, :], g2_ref[...], dx)
                d2_ref[rows, :] = d2.astype(BF16)
                gg2_ref[...] += dg2
        if ng:
            pl.when(i == nt - 1)(finish)

    row = lambda i: (i, 0)
    fix = lambda i: (0, 0)
    in_specs = [pl.BlockSpec((tm, Kp), row)] * na + [
        pl.BlockSpec((na * Kp, D), fix, pipeline_mode=pl.Buffered(1)), pl.BlockSpec((tm, D), row),
        pl.BlockSpec((tm, D), row), pl.BlockSpec((1, D), fix)]
    args = list(a_list) + [w, resid, xin, g]
    out_specs = [pl.BlockSpec((tm, D), row), pl.BlockSpec((1, D), fix)]
    out_shape = [jax.ShapeDtypeStruct((S, D), F32), jax.ShapeDtypeStruct((1, D), F32)]
    if two:
        in_specs += [pl.BlockSpec((tm, D), row), pl.BlockSpec((1, D), fix)]
        args += list(second)
        out_specs += [pl.BlockSpec((tm, D), row), pl.BlockSpec((1, D), fix)]
        out_shape += [jax.ShapeDtypeStruct((S, D), BF16), jax.ShapeDtypeStruct((1, D), F32)]
    n_plain = len(out_shape)
    out = pl.pallas_call(
        body, name=name, grid=(nt,), in_specs=in_specs + [ANY] * ng, out_specs=out_specs + [ANY] * ng,
        out_shape=out_shape + recv_shapes, scratch_shapes=exchange_sems if ng else [],
        compiler_params=_cp("arbitrary"),
    )(*args, *exchange)
    return (*out[:n_plain], out[n_plain:]) if ng else out


def _band_mask(first_block):
    qi = lax.broadcasted_iota(jnp.int32, (BLOCK, 2 * BLOCK), 0)
    ki = lax.broadcasted_iota(jnp.int32, (BLOCK, 2 * BLOCK), 1)
    first_key = jnp.where(first_block, BLOCK, 0)
    return (ki >= qi) & (ki <= qi + BLOCK) & (ki >= first_key)


def _lane_masks():
    lane = lax.broadcasted_iota(jnp.int32, (1, 2 * HEAD_DIM), 1)
    return (lane < HEAD_DIM, lane >= HEAD_DIM)


CHUNK = BLOCK * max(DILATIONS)
SLAB = 2 * HEAD_DIM
N_SLABS = ATTN_WIDTH // SLAB


def _unit_rows(d, b):
    def rows(r):
        start = r + BLOCK * d * b
        return pl.ds(start, BLOCK, stride=d) if d > 1 else pl.ds(start, BLOCK)
    return rows


def _attn_units():
    for p, d in enumerate(DILATIONS):
        nbc = CHUNK // (BLOCK * d)
        for b in range(nbc):
            for r in range(d):
                yield p, d, b, r, nbc


def _attn_in_specs(nc, n_cur):
    prev = lambda c: jnp.maximum(jnp.minimum(c, nc - 1) - 1, 0)
    cur = lambda c: jnp.minimum(c, nc - 1)
    blk = lambda f: pl.BlockSpec((CHUNK, SLAB), f)
    specs = [blk(lambda h, c: (cur(c), h)),
             blk(lambda h, c: (prev(c), N_SLABS + h)), blk(lambda h, c: (cur(c), N_SLABS + h)),
             blk(lambda h, c: (prev(c), 2 * N_SLABS + h)), blk(lambda h, c: (cur(c), 2 * N_SLABS + h))]
    return specs + [blk(lambda h, c: (cur(c), h))] * n_cur


def _attn_fwd(proj, payload, *, name):
    S = proj.shape[0]
    nc = S // CHUNK
    n = len(DILATIONS)
    npay = len(payload)
    n_steps = N_SLABS * nc

    def body(*refs):
        q_ref, kp_ref, kc_ref, vp_ref, vc_ref = refs[:5]
        pay_refs = refs[5:5 + npay]
        attn_ref, lse_ref, attn16_ref = refs[5 + npay:8 + npay]
        all_refs = refs[8 + npay:8 + 2 * npay]
        scr = refs[8 + 2 * npay:]
        o_scr, l_scr = scr[:n], scr[n:2 * n]
        start, forward, finish = _gather_phases(pay_refs, all_refs, *scr[2 * n:])
        step = pl.program_id(0) * nc + pl.program_id(1)
        pl.when(step == 0)(start)
        c = pl.program_id(1)
        lms = _lane_masks()
        plain, first = (jnp.tile(_band_mask(f), (2, 1)) for f in (False, c == 0))
        def scores(unit):
            p, d, b, r, nbc = unit
            rows = _unit_rows(d, b)(r)
            prow = _unit_rows(d, (b - 1) % nbc)(r)
            kpr, vpr = (kc_ref, vc_ref) if b > 0 else (kp_ref, vp_ref)
            q = q_ref[rows, :].astype(BF16)
            kcat = jnp.concatenate([kpr[prow, :], kc_ref[rows, :]], axis=0).astype(BF16)
            vcat = jnp.concatenate([vpr[prow, :], vc_ref[rows, :]], axis=0).astype(BF16)
            q2 = jnp.concatenate([jnp.where(lm, q, jnp.zeros_like(q)) for lm in lms], axis=0) * SCALE
            return p, rows, plain if b > 0 else first, vcat, _dot(q2, kcat, NT)

        units = list(_attn_units())
        nxt = scores(units[0])
        for k in range(len(units)):
            p, rows, mask2, vcat, s = nxt
            if k + 1 < len(units):
                nxt = scores(units[k + 1])
            s = jnp.where(mask2, s, NEG_INF)
            m = jnp.max(s, axis=-1, keepdims=True)
            e = jnp.exp(s - m)
            l = jnp.sum(e, axis=-1, keepdims=True)
            o2 = _dot(e.astype(BF16), vcat, NN) / l
            lse2 = m + jnp.log(l)
            o_scr[p][rows, :] = jnp.where(lms[0], o2[:BLOCK], o2[BLOCK:])
            l_scr[p][rows, :] = jnp.where(lms[0], lse2[:BLOCK], lse2[BLOCK:])
        ls = [l_scr[p][...] for p in range(n)]
        top = functools.reduce(jnp.maximum, ls)
        es = [jnp.exp(l - top) for l in ls]
        den = functools.reduce(jnp.add, es)
        num = functools.reduce(jnp.add, [e * o_scr[p][...] for p, e in enumerate(es)])
        attn = num / den
        attn_ref[...] = attn
        attn16_ref[...] = attn.astype(BF16)
        lse_ref[...] = top + jnp.log(den)
        pl.when(step == (2 * n_steps) // 3)(forward)
        pl.when(step == n_steps - 1)(finish)

    out = pl.pallas_call(
        body, name=name, grid=(N_SLABS, nc), in_specs=_attn_in_specs(nc, 0) + [ANY] * npay,
        out_specs=[pl.BlockSpec((CHUNK, SLAB), lambda h, c: (c, h))] * 3 + [ANY] * npay,
        out_shape=[jax.ShapeDtypeStruct((S, ATTN_WIDTH), F32)] * 2 + [jax.ShapeDtypeStruct((S, ATTN_WIDTH), BF16)]
        + _gathered_shapes(payload),
        scratch_shapes=[pltpu.VMEM((CHUNK, SLAB), F32)] * (2 * n) + _gather_sems(npay),
        compiler_params=_cp("arbitrary", "arbitrary"),
    )(proj, proj, proj, proj, proj, *payload)
    return (*out[:3], out[3:])


def _attn_bwd(proj, dcat, attn, lse, grads, *, name):
    S = proj.shape[0]
    nc = S // CHUNK
    ng = len(grads)
    n = len(DILATIONS)
    recv_shapes, exchange_sems = _exchange_buffers(grads)

    def body(*refs):
        q_ref, kp_ref, kc_ref, vp_ref, vc_ref, do_ref, o_ref, lse_ref = refs[:8]
        g_refs = refs[8:8 + ng]
        dq_ref, dk_ref, dv_ref = refs[8 + ng:11 + ng]
        r_refs = refs[11 + ng:11 + 2 * ng]
        scr = refs[11 + 2 * ng:]
        dk_prev, dv_prev = scr[:2]
        delta_h, lse_h = scr[2:4], scr[4:6]
        dq_p, dk_own, dk_back, dv_own, dv_back = (scr[6 + n * k:6 + n * (k + 1)] for k in range(5))
        start, finish = _exchange_phases(g_refs, r_refs, *scr[6 + 5 * n:])
        c = pl.program_id(1)
        pl.when((pl.program_id(0) == 0) & (c == 0))(start)

        @pl.when(c == 0)
        def _():
            dk_prev[...] = jnp.zeros_like(dk_prev)
            dv_prev[...] = jnp.zeros_like(dv_prev)

        @pl.when(c < nc)
        def _():
            lms = _lane_masks()
            plain, first = (jnp.tile(_band_mask(f), (2, 1)) for f in (False, c == 0))
            prod = do_ref[...] * o_ref[...]
            lse = lse_ref[...]
            lse_other = pltpu.roll(lse, HEAD_DIM, 1)
            for h, lm in enumerate(lms):
                delta = jnp.sum(jnp.where(lm, prod, 0.0), axis=-1, keepdims=True)
                delta_h[h][...] = jnp.broadcast_to(delta, (CHUNK, SLAB))
                lse_h[h][...] = jnp.where(lm, lse, lse_other)
            wide = lambda refs, rows: jnp.tile(jnp.concatenate([r[rows, :] for r in refs], axis=0), (1, 2))
            stack = lambda f: jnp.concatenate([f(lm) for lm in lms], axis=0)

            def scores(unit):
                p, d, b, r, nbc = unit
                rows = _unit_rows(d, b)(r)
                prow = _unit_rows(d, (b - 1) % nbc)(r)
                kpr, vpr = (kc_ref, vc_ref) if b > 0 else (kp_ref, vp_ref)
                q = q_ref[rows, :].astype(BF16)
                kcat = jnp.concatenate([kpr[prow, :], kc_ref[rows, :]], axis=0).astype(BF16)
                vcat = jnp.concatenate([vpr[prow, :], vc_ref[rows, :]], axis=0).astype(BF16)
                do = do_ref[rows, :]
                q2 = stack(lambda lm: jnp.where(lm, q, jnp.zeros_like(q))) * SCALE
                do2 = stack(lambda lm: jnp.where(lm, do, 0.0)).astype(BF16)
                return dict(p=p, rows=rows, prow=prow, mask2=plain if b > 0 else first, kcat=kcat, q2=q2, do2=do2,
                            s=_dot(q2, kcat, NT), dp=_dot(do2, vcat, NT))

            units = list(_attn_units())
            nxt = scores(units[0])
            for k in range(len(units)):
                u = nxt
                if k + 1 < len(units):
                    nxt = scores(units[k + 1])
                p, rows, prow, kcat = u["p"], u["rows"], u["prow"], u["kcat"]
                e = jnp.where(u["mask2"], jnp.exp(u["s"] - wide(lse_h, rows)), 0.0)
                ds = (e * (u["dp"] - wide(delta_h, rows))).astype(BF16)
                dq = jnp.where(lms[0], _dot(ds[:BLOCK], kcat, NN), _dot(ds[BLOCK:], kcat, NN)) * SCALE
                dkc = _dot(ds, u["q2"], TN)
                dvc = _dot(e.astype(BF16), u["do2"], TN)
                dq_p[p][rows, :] = dq
                dk_own[p][rows, :] = dkc[BLOCK:]
                dv_own[p][rows, :] = dvc[BLOCK:]
                dk_back[p][prow, :] = dkc[:BLOCK]
                dv_back[p][prow, :] = dvc[:BLOCK]
            dq_ref[...] = functools.reduce(jnp.add, [r[...] for r in dq_p]).astype(BF16)
            for prev, own, back, out_ref in ((dk_prev, dk_own, dk_back, dk_ref), (dv_prev, dv_own, dv_back, dv_ref)):
                for p, d in enumerate(DILATIONS):
                    tail = CHUNK - BLOCK * d
                    prev[tail:, :] += back[p][tail:, :]
                out_ref[...] = prev[...].astype(BF16)
                prev[...] = functools.reduce(jnp.add, [r[...] for r in own])
                for p, d in enumerate(DILATIONS):
                    tail = CHUNK - BLOCK * d
                    if tail:
                        prev[:tail, :] += back[p][:tail, :]

        @pl.when(c == nc)
        def _():
            dk_ref[...] = dk_prev[...].astype(BF16)
            dv_ref[...] = dv_prev[...].astype(BF16)

        pl.when((pl.program_id(0) == N_SLABS - 1) & (c == nc))(finish)

    blk = lambda f: pl.BlockSpec((CHUNK, SLAB), f)
    late = lambda h, c: (jnp.maximum(c - 1, 0), h)
    out = pl.pallas_call(
        body, name=name, grid=(N_SLABS, nc + 1), in_specs=_attn_in_specs(nc, 3) + [ANY] * ng,
        out_specs=[blk(lambda h, c: (jnp.minimum(c, nc - 1), h)), blk(late), blk(late)] + [ANY] * ng,
        out_shape=[jax.ShapeDtypeStruct((S, ATTN_WIDTH), BF16)] * 3 + recv_shapes,
        scratch_shapes=[pltpu.VMEM((CHUNK, SLAB), F32)] * (6 + 5 * n) + exchange_sems,
        compiler_params=_cp("arbitrary", "arbitrary"),
    )(proj, proj, proj, proj, proj, dcat, attn, lse, *grads)
    return out[:3], out[3:]


def _split_bf16(a):
    hi = a.astype(BF16)
    lo = (a - hi.astype(F32)).astype(BF16)
    return hi, lo


def _pooled(ug, halo_g, w, row0, tm):
    ext = jnp.concatenate([halo_g, ug], axis=0)
    hi, lo = _split_bf16(ext)
    rr = lax.broadcasted_iota(jnp.int32, (tm, tm + HALO), 0)
    cc = lax.broadcasted_iota(jnp.int32, (tm, tm + HALO), 1)
    back = rr + HALO - cc
    win = ((back >= 0) & (back < w)).astype(BF16)
    wsum = _dot(win, hi, NN) + _dot(win, lo, NN)
    rows = row0 + lax.broadcasted_iota(jnp.int32, (tm, 1), 0)
    inv = 1.0 / jnp.minimum(rows + 1, w).astype(F32)
    return wsum * inv - ug


def _pool_fwd(u, u_col, pool_w, pool_scale, *, name, tm=256):
    S, W = u.shape[0], POOL_WIDTH
    G = POOL_GROUP_DIM

    def body(u_ref, h_ref, w_ref, s_ref, o_ref):
        i = pl.program_id(0)
        uv = u_ref[...]
        halo = jnp.where(i > 0, h_ref[...], 0.0)
        sls = [slice(g * G, (g + 1) * G) for g in range(len(POOL_WINDOWS))]
        pooled = [_pooled(uv[:, sl], halo[:, sl], w, i * tm, tm) for sl, w in zip(sls, POOL_WINDOWS)]
        zs = [_dot(p.astype(BF16), w_ref[g].astype(BF16), NN) for g, p in enumerate(pooled)]
        for sl, z in zip(sls, zs):
            o_ref[:, sl] = (z * s_ref[:, sl]).astype(BF16)

    per = tm // HALO
    return pl.pallas_call(
        body, name=name, grid=(S // tm,),
        in_specs=[pl.BlockSpec((tm, W), lambda i: (i, u_col)),
                  pl.BlockSpec((HALO, W), lambda i: (jnp.maximum(i * per - 1, 0), u_col)),
                  pl.BlockSpec((len(POOL_WINDOWS), G, G), lambda i: (0, 0, 0)),
                  pl.BlockSpec((1, W), lambda i: (0, 0))],
        out_specs=pl.BlockSpec((tm, W), lambda i: (i, 0)),
        out_shape=jax.ShapeDtypeStruct((S, W), BF16),
        compiler_params=_cp("parallel"),
    )(u, u, pool_w, pool_scale)


def _pool_bwd(u, u_col, dy, dy_col, pool_w, pool_scale, *, name, tm=256):
    S, W = u.shape[0], POOL_WIDTH
    G = POOL_GROUP_DIM
    nt = S // tm

    def body(u_ref, h_ref, dy_ref, dyn_ref, w_ref, s_ref, du_ref, gw_ref, gs_ref):
        i = pl.program_id(0)

        @pl.when(i == 0)
        def _():
            gw_ref[...] = jnp.zeros_like(gw_ref)
            gs_ref[...] = jnp.zeros_like(gs_ref)

        uv = u_ref[...]
        halo = jnp.where(i > 0, h_ref[...], 0.0)
        dyv = dy_ref[...]
        dyn = jnp.where(i < nt - 1, dyn_ref[...], 0.0)
        rr = lax.broadcasted_iota(jnp.int32, (tm, tm + HALO), 0)
        cc = lax.broadcasted_iota(jnp.int32, (tm, tm + HALO), 1)
        rows_ext = i * tm + lax.broadcasted_iota(jnp.int32, (tm + HALO, 1), 0)
        groups = list(enumerate(POOL_WINDOWS))
        sls = [slice(g * G, (g + 1) * G) for g, _ in groups]
        wgs = [w_ref[g].astype(BF16) for g, _ in groups]
        pooled = [_pooled(uv[:, sl], halo[:, sl], w, i * tm, tm).astype(BF16) for sl, (_, w) in zip(sls, groups)]
        dzs = [dyv[:, sl] * s_ref[:, sl] for sl in sls]
        dz_ext = [jnp.concatenate([dz, dyn[:, sl] * s_ref[:, sl]], axis=0).astype(BF16) for dz, sl in zip(dzs, sls)]
        dp_ext = [_dot(d, wg, NT) for d, wg in zip(dz_ext, wgs)]
        zs = [_dot(p, wg, NN) for p, wg in zip(pooled, wgs)]
        for (g, w), sl, p, dz, z, dp in zip(groups, sls, pooled, dzs, zs, dp_ext):
            gw_ref[g] += _dot(p, dz.astype(BF16), TN)
            gs_ref[:, sl] += jnp.sum(dyv[:, sl] * z, axis=0, keepdims=True)
            inv_ext = 1.0 / jnp.minimum(rows_ext + 1, w).astype(F32)
            hi, lo = _split_bf16(dp * inv_ext)
            ahead = cc - rr
            win = ((ahead >= 0) & (ahead < w)).astype(BF16)
            du_ref[:, sl] = (_dot(win, hi, NN) + _dot(win, lo, NN) - dp[:tm]).astype(BF16)

    per = tm // HALO
    nh = S // HALO
    return pl.pallas_call(
        body, name=name, grid=(nt,),
        in_specs=[pl.BlockSpec((tm, W), lambda i: (i, u_col)),
                  pl.BlockSpec((HALO, W), lambda i: (jnp.maximum(i * per - 1, 0), u_col)),
                  pl.BlockSpec((tm, W), lambda i: (i, dy_col)),
                  pl.BlockSpec((HALO, W), lambda i: (jnp.minimum((i + 1) * per, nh - 1), dy_col)),
                  pl.BlockSpec((len(POOL_WINDOWS), G, G), lambda i: (0, 0, 0)),
                  pl.BlockSpec((1, W), lambda i: (0, 0))],
        out_specs=[pl.BlockSpec((tm, W), lambda i: (i, 0)),
                   pl.BlockSpec((len(POOL_WINDOWS), G, G), lambda i: (0, 0, 0)),
                   pl.BlockSpec((1, W), lambda i: (0, 0))],
        out_shape=[jax.ShapeDtypeStruct((S, W), BF16),
                   jax.ShapeDtypeStruct((len(POOL_WINDOWS), G, G), F32),
                   jax.ShapeDtypeStruct((1, W), F32)],
        compiler_params=_cp("arbitrary"),
    )(u, u, dy, dy, pool_w, pool_scale)


GELU_K0 = math.sqrt(2.0 / math.pi)
GELU_K1 = 0.044715


def _gelu_parts(x):
    x2 = x * x
    t = jnp.tanh(x * (GELU_K0 + (GELU_K0 * GELU_K1) * x2))
    hp = 0.5 + 0.5 * t
    gelu = x * hp
    dgelu = hp + (x * (hp * (1.0 - t))) * (GELU_K0 + (3.0 * GELU_K0 * GELU_K1) * x2)
    return gelu, dgelu


def _shifted(ext, halo):
    return (pltpu.roll(ext, 2, 0)[halo:], pltpu.roll(ext, 1, 0)[halo:], ext[halo:])


def _conv(sh, w, b):
    return b + (sh[0] * w[0:1] + sh[1] * w[1:2] + sh[2] * w[2:3])


F32_ROWS = 8


def _ffn_up_glu(h, w_up_t, conv_w, conv_b, *, name, tm=2048, tn=256, sub=256):
    S, K = h.shape
    F = D_FF
    nj = F // tn

    def body(h_ref, wg_ref, wv_ref, cwg_ref, cwv_ref, cbg_ref, cbv_ref,
             ug_ref, uv_ref, cg_ref, cv_ref, y_ref, carry):
        i = pl.program_id(0)
        j = pl.program_id(1)

        w_cat = jnp.concatenate([wg_ref[...], wv_ref[...]], axis=0)
        conv_w_b = ((cwg_ref[...], cbg_ref[...]), (cwv_ref[...], cbv_ref[...]))
        halo = [jnp.where(i > 0, carry[j, s], 0.0) for s in range(2)]
        u_next = _dot(h_ref[0:sub, :], w_cat, NT)
        for a in range(0, tm, sub):
            u16 = u_next.astype(BF16)
            if a + sub < tm:
                u_next = _dot(h_ref[a + sub:a + 2 * sub, :], w_cat, NT)
            ug_ref[a:a + sub, :] = u16[:, :tn]
            uv_ref[a:a + sub, :] = u16[:, tn:]
            c = []
            for s, (cw, cb) in enumerate(conv_w_b):
                u = u16[:, s * tn:(s + 1) * tn].astype(F32)
                ext = jnp.concatenate([halo[s], u], axis=0)
                c.append(_conv(_shifted(ext, F32_ROWS), cw, cb))
                halo[s] = u[sub - F32_ROWS:]
            cg_ref[a:a + sub, :] = c[0].astype(BF16)
            cv_ref[a:a + sub, :] = c[1].astype(BF16)
            gelu, _ = _gelu_parts(c[0])
            y_ref[a:a + sub, :] = (gelu * c[1]).astype(BF16)
        for s in range(2):
            carry[j, s] = halo[s]

    tile = pl.BlockSpec((tm, tn), lambda i, j: (i, j))
    vec = lambda rows, off: pl.BlockSpec((rows, tn), lambda i, j: (0, j + off))
    return pl.pallas_call(
        body, name=name, grid=(S // tm, nj),
        in_specs=[pl.BlockSpec((tm, K), lambda i, j: (i, 0)),
                  pl.BlockSpec((tn, K), lambda i, j: (j, 0)), pl.BlockSpec((tn, K), lambda i, j: (j + nj, 0)),
                  vec(3, 0), vec(3, nj), vec(1, 0), vec(1, nj)],
        out_specs=[tile] * 5,
        out_shape=[jax.ShapeDtypeStruct((S, F), BF16)] * 5,
        scratch_shapes=[pltpu.VMEM((nj, 2, F32_ROWS, tn), F32)],
        compiler_params=_cp("arbitrary", "arbitrary"),
    )(h, w_up_t, w_up_t, conv_w, conv_w, conv_b, conv_b)


def _ffn_glu_bwd(u_g, u_v, c_g, c_v, df, w_down, h, conv_w, *, name, tm=2048, tn=256, sub=256):
    S = u_g.shape[0]
    F = D_FF
    D = df.shape[1]
    nj = F // tn
    nt = S // tm

    def body(ug_ref, uv_ref, cg_ref, cgn_ref, cv_ref, cvn_ref, df_ref, dfn_ref, wd_ref, h_ref, wg_ref, wv_ref,
             dug_ref, duv_ref, gug_ref, guv_ref, gd_ref, gwg_ref, gwv_ref, gbg_ref, gbv_ref,
             acc_u, acc_d):
        i = pl.program_id(1)

        @pl.when(i == 0)
        def _():
            for r in (gwg_ref, gwv_ref, gbg_ref, gbv_ref, acc_u, acc_d):
                r[...] = jnp.zeros_like(r)

        wg, wv = wg_ref[...], wv_ref[...]
        wd = wd_ref[...]
        dfn = jnp.where(i < nt - 1, dfn_ref[...], jnp.zeros_like(dfn_ref))
        n_ext = sub + HALO

        def ahead(dc):
            return dc[:sub], pltpu.roll(dc, n_ext - 1, 0)[:sub], pltpu.roll(dc, n_ext - 2, 0)[:sub]

        def ext(ref, nxt, a):
            b = a + sub
            return jnp.concatenate([ref[a:b, :], ref[b:b + HALO, :] if b < tm else nxt], axis=0)

        dy_next = _dot(ext(df_ref, dfn, 0), wd, NT)
        for a in range(0, tm, sub):
            b = a + sub
            dy_ext = dy_next
            if b < tm:
                dy_next = _dot(ext(df_ref, dfn, b), wd, NT)
            cg = ext(cg_ref, cgn_ref[...], a).astype(F32)
            cv = ext(cv_ref, cvn_ref[...], a).astype(F32)
            df_sub = df_ref[a:b, :]
            gelu, dgelu = _gelu_parts(cg)
            dcs_g = ahead(dy_ext * cv * dgelu)
            dcs_v = ahead(dy_ext * gelu)
            du_g = (dcs_g[0] * wg[2:3] + dcs_g[1] * wg[1:2] + dcs_g[2] * wg[0:1]).astype(BF16)
            du_v = (dcs_v[0] * wv[2:3] + dcs_v[1] * wv[1:2] + dcs_v[2] * wv[0:1]).astype(BF16)
            dug_ref[a:b, :] = du_g
            duv_ref[a:b, :] = du_v
            acc_u[...] += _dot(jnp.concatenate([du_g, du_v], axis=1), h_ref[a:b, :], TN)
            acc_d[...] += _dot((gelu[:sub] * cv[:sub]).astype(BF16), df_sub, TN)
            for dcs, u_ref, gw_ref, gb_ref in ((dcs_g, ug_ref, gwg_ref, gbg_ref), (dcs_v, uv_ref, gwv_ref, gbv_ref)):
                u = u_ref[a:b, :].astype(F32)
                gb_ref[...] += jnp.sum(dcs[0], axis=0, keepdims=True)
                for k in range(3):
                    gw_ref[k:k + 1, :] += jnp.sum(dcs[2 - k] * u, axis=0, keepdims=True)

        @pl.when(i == nt - 1)
        def _():
            gug_ref[...] = acc_u[:tn, :].astype(BF16)
            guv_ref[...] = acc_u[tn:, :].astype(BF16)
            gd_ref[...] = acc_d[...].astype(BF16)

    per = tm // HALO
    nh = S // HALO
    hnext = lambda i: jnp.minimum((i + 1) * per, nh - 1)
    tile = pl.BlockSpec((tm, tn), lambda j, i: (i, j))
    hn = pl.BlockSpec((HALO, tn), lambda j, i: (hnext(i), j))
    vec = lambda rows, off: pl.BlockSpec((rows, tn), lambda j, i: (0, j + off))
    wide = pl.BlockSpec((tm, D), lambda j, i: (i, 0))
    wrow = pl.BlockSpec((tn, D), lambda j, i: (j, 0))
    return pl.pallas_call(
        body, name=name, grid=(nj, nt),
        in_specs=[tile, tile, tile, hn, tile, hn, wide, pl.BlockSpec((HALO, D), lambda j, i: (hnext(i), 0)),
                  wrow, wide, vec(3, 0), vec(3, nj)],
        out_specs=[tile, tile, wrow, wrow, wrow, vec(3, 0), vec(3, 0), vec(1, 0), vec(1, 0)],
        out_shape=[jax.ShapeDtypeStruct((S, F), BF16), jax.ShapeDtypeStruct((S, F), BF16),
                   jax.ShapeDtypeStruct((F, D), BF16), jax.ShapeDtypeStruct((F, D), BF16),
                   jax.ShapeDtypeStruct((F, D), BF16),
                   jax.ShapeDtypeStruct((3, F), F32), jax.ShapeDtypeStruct((3, F), F32),
                   jax.ShapeDtypeStruct((1, F), F32), jax.ShapeDtypeStruct((1, F), F32)],
        scratch_shapes=[pltpu.VMEM((2 * tn, D), F32), pltpu.VMEM((tn, D), F32)],
        compiler_params=_cp("parallel", "arbitrary"),
    )(u_g, u_v, c_g, c_g, c_v, c_v, df, df, w_down, h, conv_w, conv_w)


def _sum_partials(parts, *, name, tr):
    _, R, C = parts.shape

    def body(p_ref, o_ref):
        tot = p_ref[0].astype(F32)
        for j in range(1, N_DEV):
            tot = tot + p_ref[j].astype(F32)
        o_ref[...] = tot

    return pl.pallas_call(
        body, name=name, grid=(R // tr,),
        in_specs=[pl.BlockSpec((N_DEV, tr, C), lambda i: (0, i, 0))],
        out_specs=pl.BlockSpec((tr, C), lambda i: (i, 0)),
        out_shape=jax.ShapeDtypeStruct((R, C), F32),
        compiler_params=_cp("parallel"),
    )(parts)


def _adamw(w, g, m, v, *, name, tr):
    R, C = w.shape
    c1 = 1.0 - ADAM_B1 ** ADAM_STEP
    c2 = 1.0 - ADAM_B2 ** ADAM_STEP

    def body(w_ref, g_ref, m_ref, v_ref, d_ref, nm_ref, nv_ref):
        g = g_ref[...]
        nm = ADAM_B1 * m_ref[...] + (1.0 - ADAM_B1) * g
        nv = ADAM_B2 * v_ref[...] + (1.0 - ADAM_B2) * (g * g)
        d_ref[...] = -ADAM_LR * ((nm / c1) / (jnp.sqrt(nv / c2) + ADAM_EPS) + ADAM_WD * w_ref[...])
        nm_ref[...] = nm
        nv_ref[...] = nv

    spec = pl.BlockSpec((tr, C), lambda i: (i, 0))
    return pl.pallas_call(
        body, name=name, grid=(R // tr,), in_specs=[spec] * 4, out_specs=[spec] * 3,
        out_shape=[jax.ShapeDtypeStruct((R, C), F32)] * 3,
        compiler_params=_cp("parallel"),
    )(w, g, m, v)


def _mesh_pos():
    return lax.axis_index("x"), lax.axis_index("y"), lax.axis_index("c")


def _gather_phases(x_refs, out_refs, send_sems, recv_sems, local_sems):
    x, y, c = _mesh_pos()
    me, sibling = (x, y, c), (x, y, 1 - c)
    chips = [(1 - x, y), (x, 1 - y), (1 - x, 1 - y)]
    arrays = range(len(x_refs))

    def slot(a, px, py, pc):
        return out_refs[a].at[4 * px + 2 * py + pc]

    def copy(a, k, block, to, own=False):
        return pltpu.make_async_remote_copy(
            src_ref=x_refs[a] if own else slot(a, *block), dst_ref=slot(a, *block),
            send_sem=send_sems.at[a, k], recv_sem=recv_sems.at[a, k], device_id=to, device_id_type=MESH)

    mine = [pltpu.make_async_copy(x_refs[a], slot(a, *me), local_sems.at[a]) for a in arrays]
    first = [copy(a, 0, me, sibling, own=True) for a in arrays]
    first += [copy(a, 1 + j, me, (*chip, c), own=True) for j, chip in enumerate(chips) for a in arrays]
    passed = [[copy(a, 4 + j, (*chip, c), sibling) for a in arrays] for j, chip in enumerate(chips)]

    def start():
        for cp in mine + first:
            cp.start()

    def forward():
        for j, chip in enumerate(chips):
            for a in arrays:
                copy(a, 1 + j, (*chip, c), me).wait_recv()
                passed[j][a].start()

    def finish():
        for a in arrays:
            copy(a, 0, sibling, me).wait_recv()
            for j, chip in enumerate(chips):
                copy(a, 4 + j, (*chip, 1 - c), me).wait_recv()
        for cp in first + [cp for row in passed for cp in row]:
            cp.wait_send()
        for cp in mine:
            cp.wait()

    return start, forward, finish


def _gather_sems(n):
    return [pltpu.SemaphoreType.DMA((n, 7)), pltpu.SemaphoreType.DMA((n, 7)), pltpu.SemaphoreType.DMA((n,))]


def _gathered_shapes(blocks):
    return [jax.ShapeDtypeStruct((N_DEV,) + b.shape, b.dtype) for b in blocks]


def _all_reduce_small(block, *, name):
    def body(x_ref, all_ref, sum_ref, *sems):
        for phase in _gather_phases([x_ref], [all_ref], *sems):
            phase()
        tot = all_ref[0]
        for j in range(1, N_DEV):
            tot = tot + all_ref[j]
        sum_ref[...] = tot

    return pl.pallas_call(
        body, name=name, in_specs=[VMEM], out_specs=[VMEM, VMEM],
        out_shape=[jax.ShapeDtypeStruct((N_DEV,) + block.shape, block.dtype),
                   jax.ShapeDtypeStruct(block.shape, block.dtype)],
        scratch_shapes=_gather_sems(1),
        compiler_params=pltpu.CompilerParams(vmem_limit_bytes=V7X_VMEM_LIMIT),
    )(block)[1]


def _exchange_phases(g_refs, r_refs, send_sems, recv_sems, local_sems):
    x, y, c = _mesh_pos()
    me = 4 * x + 2 * y + c
    owns, remote = [], []
    for k, (g_ref, r_ref) in enumerate(zip(g_refs, r_refs)):
        rows = g_ref.shape[0] // N_DEV
        owns.append(pltpu.make_async_copy(g_ref.at[pl.ds(me * rows, rows)], r_ref.at[me], local_sems.at[k]))
        for p in range(1, N_DEV):
            px, py, pc = x ^ (p >> 2), y ^ ((p >> 1) & 1), c ^ (p & 1)
            peer = 4 * px + 2 * py + pc
            link = dict(send_sem=send_sems.at[k, p], recv_sem=recv_sems.at[k, p],
                        device_id=(px, py, pc), device_id_type=MESH)
            src = g_ref.at[pl.ds(peer * rows, rows)]
            send = pltpu.make_async_remote_copy(src_ref=src, dst_ref=r_ref.at[me], **link)
            arrival = pltpu.make_async_remote_copy(src_ref=src, dst_ref=r_ref.at[peer], **link)
            remote.append((send, arrival))

    def start():
        for own in owns:
            own.start()
        for send, _ in remote:
            send.start()

    def finish():
        for _, arrival in remote:
            arrival.wait_recv()
        for send, _ in remote:
            send.wait_send()
        for own in owns:
            own.wait()

    return start, finish


def _exchange_buffers(grads):
    n = len(grads)
    shapes = [jax.ShapeDtypeStruct((N_DEV, g.shape[0] // N_DEV, g.shape[1]), g.dtype) for g in grads]
    sems = [pltpu.SemaphoreType.DMA((n, N_DEV)), pltpu.SemaphoreType.DMA((n, N_DEV)),
            pltpu.SemaphoreType.DMA((n,))]
    return shapes, sems


def _unpack_gathered(gathered):
    w_out, w_up_t, w_down = (g.reshape(-1, D_MODEL) for g in gathered[:3])
    width = 2 * D_FF // N_DEV
    conv_w = jnp.transpose(gathered[3][:, :3, :width], (1, 0, 2)).reshape(3, 2 * D_FF)
    return w_out, w_up_t, w_down, conv_w


def _rest_payload(w_out, w_up, w_down, conv_w):
    rows, cols = conv_w.shape
    conv_w = jnp.pad(conv_w, ((0, (-rows) % F32_ROWS), (0, (-cols) % LANES)))
    return [w_out.astype(BF16), w_up.T.astype(BF16), w_down.astype(BF16), conv_w]


def _device_step(x, target, g_mix_pre, w_in_t_block, rest_payload, pool_w, pool_scale, g_mix_post, g_ffn_pre,
                 conv_b, g_ffn_post):
    h1, w_in_t = _rms_norm_gather(x, g_mix_pre, w_in_t_block, name="rms_mix_pre")
    w_in_t = w_in_t.reshape(-1, D_MODEL)
    proj = _matmul(h1, w_in_t, trans_b=True, out_dtype=F32, tm=512, tn=4 * ATTN_WIDTH, name="proj")
    attn, lse, attn16, gathered = _attn_fwd(proj, rest_payload, name="attn_fwd")
    w_out, w_up_t, w_down, conv_w = _unpack_gathered(gathered)
    pool = _pool_fwd(proj, 3, pool_w, pool_scale, name="pool_fwd")
    mixed, x2, h2 = _mix_out(attn16, pool, w_out, x, g_mix_post, g_ffn_pre, name="mix_out")
    u_g, u_v, c_g, c_v, y = _ffn_up_glu(h2, w_up_t, conv_w, conv_b, name="ffn_up_glu")
    df, d_out, loss_blk, gg_ffn_post = _ffn_out(y, w_down, x2, target, g_ffn_post, name="ffn_out")
    du_g, du_v, gw_up_g, gw_up_v, gw_down, gcw_g, gcw_v, gcb_g, gcb_v = _ffn_glu_bwd(
        u_g, u_v, c_g, c_v, df, w_down, h2, conv_w, name="ffn_glu_bwd")
    gw_up_t = jnp.concatenate([gw_up_g, gw_up_v], axis=0)
    dx2, gg_ffn_pre, dmixed, gg_mix_post = _dgrad_norm(
        [du_g, du_v], w_up_t, d_out, x2, g_ffn_pre, (mixed, g_mix_post), [], name="ffn_up_dgrad")
    gw_out = _matmul_tn([attn16, pool], dmixed, name="grad_w_out")
    dcat = _matmul(dmixed, w_out, trans_b=True, out_dtype=F32, tm=512, tn=1024, name="mix_out_dgrad")
    d_pool_in, g_pool_w, g_pool_scale = _pool_bwd(proj, 3, dcat, 1, pool_w, pool_scale, name="pool_bwd")
    dqkv, (r_out, r_up_t, r_down) = _attn_bwd(proj, dcat, attn, lse, [gw_out, gw_up_t, gw_down], name="attn_bwd")
    dproj = list(dqkv) + [d_pool_in]
    gw_in_t = _matmul_tn(dproj, h1, name="grad_w_in")
    grad_x, gg_mix_pre, (r_in_t,) = _dgrad_norm(dproj, w_in_t, dx2, x, g_mix_pre, None, [gw_in_t], name="proj_dgrad")
    g_conv_w = jnp.concatenate([gcw_g, gcw_v], axis=1)
    g_conv_b = jnp.concatenate([gcb_g, gcb_v], axis=1)
    received = (r_in_t, r_out, r_up_t, r_down)
    small = dict(g_mix_pre=gg_mix_pre, g_mix_post=gg_mix_post, g_ffn_pre=gg_ffn_pre, g_ffn_post=gg_ffn_post,
                 pool_scale=g_pool_scale, conv_b=g_conv_b, pool_w=g_pool_w, conv_w=g_conv_w)
    return loss_blk, grad_x, received, small


_SMALL = ("g_mix_pre", "g_mix_post", "g_ffn_pre", "g_ffn_post", "pool_scale", "conv_b", "pool_w")
LANES = 128


def _pack_rows(arrays):
    parts = []
    for a in arrays:
        a2 = a.reshape(-1, LANES)
        parts.append(jnp.pad(a2, ((0, (-a2.shape[0]) % 8), (0, 0))))
    return jnp.concatenate(parts, axis=0)


def _unpack_rows(packed, shapes):
    out, row = [], 0
    for shape in shapes:
        rows = math.prod(shape) // LANES
        out.append(packed[row:row + rows].reshape(shape))
        row += -(-rows // 8) * 8
    return out


def kernel(x, g_mix_pre, w_in, pool_w, pool_scale, w_out, g_mix_post, g_ffn_pre, w_up, conv_w, conv_b, w_down, g_ffn_post, loss_target, m_g_mix_pre, m_w_in, m_pool_w, m_pool_scale, m_w_out, m_g_mix_post, m_g_ffn_pre, m_w_up, m_conv_w, m_conv_b, m_w_down, m_g_ffn_post, v_g_mix_pre, v_w_in, v_pool_w, v_pool_scale, v_w_out, v_g_mix_post, v_g_ffn_pre, v_w_up, v_conv_w, v_conv_b, v_w_down, v_g_ffn_post):
    me = 4 * lax.axis_index("x") + 2 * lax.axis_index("y") + lax.axis_index("c")
    loss_blk, grad_x, recv, small = _device_step(
        x[0], loss_target[0], g_mix_pre, w_in[0].T.astype(BF16),
        _rest_payload(w_out[0], w_up[0], w_down[0], conv_w[0]),
        pool_w[0], pool_scale, g_mix_post, g_ffn_pre, conv_b, g_ffn_post)

    g_in_t, g_out, g_up_t, g_down = (
        _sum_partials(r, name=f"sum_partials_{k}", tr=r.shape[1] // 2) for k, r in enumerate(recv))
    grads = {"w_in": g_in_t.T, "w_out": g_out, "w_up": g_up_t.T, "w_down": g_down}

    given = dict(g_mix_pre=g_mix_pre, g_mix_post=g_mix_post, g_ffn_pre=g_ffn_pre, g_ffn_post=g_ffn_post,
                 pool_scale=pool_scale, conv_b=conv_b, pool_w=pool_w)
    small_shapes = [given[k].shape for k in _SMALL]
    total = _all_reduce_small(_pack_rows([small[k] for k in _SMALL] + [small["conv_w"], loss_blk]),
                              name="all_reduce_small")
    *small_grads, g_conv_w_all, loss_all = _unpack_rows(total, small_shapes + [(3, 2 * D_FF), loss_blk.shape])
    loss = loss_all[0, 0]
    grads.update(zip(_SMALL, small_grads))
    width = 2 * D_FF // N_DEV
    grads["conv_w"] = lax.dynamic_slice_in_dim(g_conv_w_all, me * width, width, axis=1)[None]

    weights = dict(g_mix_pre=g_mix_pre, w_in=w_in, pool_w=pool_w, pool_scale=pool_scale, w_out=w_out,
                   g_mix_post=g_mix_post, g_ffn_pre=g_ffn_pre, w_up=w_up, conv_w=conv_w, conv_b=conv_b,
                   w_down=w_down, g_ffn_post=g_ffn_post)
    m_in = dict(g_mix_pre=m_g_mix_pre, w_in=m_w_in, pool_w=m_pool_w, pool_scale=m_pool_scale, w_out=m_w_out,
                g_mix_post=m_g_mix_post, g_ffn_pre=m_g_ffn_pre, w_up=m_w_up, conv_w=m_conv_w, conv_b=m_conv_b,
                w_down=m_w_down, g_ffn_post=m_g_ffn_post)
    v_in = dict(g_mix_pre=v_g_mix_pre, w_in=v_w_in, pool_w=v_pool_w, pool_scale=v_pool_scale, w_out=v_w_out,
                g_mix_post=v_g_mix_post, g_ffn_pre=v_g_ffn_pre, w_up=v_w_up, conv_w=v_conv_w, conv_b=v_conv_b,
                w_down=v_w_down, g_ffn_post=v_g_ffn_post)
    delta, new_m, new_v = {}, {}, {}
    for k in ("w_in", "w_out", "w_up", "w_down"):
        g = grads[k]
        d, nm, nv = _adamw(weights[k][0], g, m_in[k][0], v_in[k][0], name=f"adamw_{k}", tr=g.shape[0] // 2)
        grads[k], delta[k], new_m[k], new_v[k] = g[None], d[None], nm[None], nv[None]
    d, nm, nv = _adamw(weights["conv_w"][0], grads["conv_w"][0], m_in["conv_w"][0], v_in["conv_w"][0],
                       name="adamw_conv_w", tr=3)
    delta["conv_w"], new_m["conv_w"], new_v["conv_w"] = d[None], nm[None], nv[None]
    packed_w = _pack_rows([weights[k] for k in _SMALL])
    small_rows = packed_w.shape[0]
    d, nm, nv = _adamw(packed_w, total[:small_rows], _pack_rows([m_in[k] for k in _SMALL]),
                       _pack_rows([v_in[k] for k in _SMALL]), name="adamw_small", tr=small_rows)
    for k, dk, mk, vk in zip(_SMALL, _unpack_rows(d, small_shapes), _unpack_rows(nm, small_shapes),
                             _unpack_rows(nv, small_shapes)):
        delta[k], new_m[k], new_v[k] = dk, mk, vk

    order = ("g_mix_pre", "w_in", "pool_w", "pool_scale", "w_out", "g_mix_post", "g_ffn_pre", "w_up",
             "conv_w", "conv_b", "w_down", "g_ffn_post")
    return (loss, grad_x[None], *[grads[k] for k in order], *[delta[k] for k in order],
            *[new_m[k] for k in order], *[new_v[k] for k in order])
```

```python
import functools
import math

import jax
import jax.numpy as jnp
from jax import lax
from jax.experimental import pallas as pl
from jax.experimental.pallas import tpu as pltpu

F32 = jnp.float32
BF16 = jnp.bfloat16

D_MODEL = 1024
N_HEADS = 8
HEAD_DIM = 64
ATTN_WIDTH = N_HEADS * HEAD_DIM
DILATIONS = (1, 4, 16)
BLOCK = 128
POOL_WIDTH = 512
POOL_WINDOWS = (2, 4, 8, 16)
POOL_GROUP_DIM = 128
D_FF = 2816
EPS = 1e-6
NEG_INF = -1e30
SCALE = HEAD_DIM ** -0.5

ADAM_LR = 0.001
ADAM_B1 = 0.9
ADAM_B2 = 0.999
ADAM_EPS = 1e-08
ADAM_WD = 0.01
ADAM_STEP = 10

N_DEV = 8
HALO = 16
V7X_VMEM_LIMIT = 56 * 1024 * 1024

MESH = pl.DeviceIdType.MESH
ANY = pl.BlockSpec(memory_space=pl.ANY)
VMEM = pl.BlockSpec(memory_space=pltpu.VMEM)

NT = (((1,), (1,)), ((), ()))
NN = (((1,), (0,)), ((), ()))
TN = (((0,), (0,)), ((), ()))


def _cp(*sem):
    return pltpu.CompilerParams(dimension_semantics=sem, vmem_limit_bytes=V7X_VMEM_LIMIT)


def _dot(a, b, dn):
    return lax.dot_general(a, b, dn, preferred_element_type=F32)


def _rms_bwd(xin, g, dy):
    r = lax.rsqrt(jnp.mean(xin * xin, axis=-1, keepdims=True) + EPS)
    xh = xin * r
    gdy = g * dy
    dx = r * (gdy - xh * jnp.mean(gdy * xh, axis=-1, keepdims=True))
    dg = jnp.sum(dy * xh, axis=0, keepdims=True)
    return dx, dg


def _rms_norm_gather(x, g, block, *, name, tm=512):
    S, D = x.shape
    nt = S // tm

    def body(x_ref, g_ref, blk_ref, o_ref, all_ref, *sems):
        i = pl.program_id(0)
        start, forward, finish = _gather_phases([blk_ref], [all_ref], *sems)
        pl.when(i == 0)(start)
        xv = x_ref[...]
        r = lax.rsqrt(jnp.mean(xv * xv, axis=-1, keepdims=True) + EPS)
        o_ref[...] = (xv * r * g_ref[...]).astype(BF16)
        pl.when(i == nt - 1)(forward)
        pl.when(i == nt - 1)(finish)

    return pl.pallas_call(
        body, name=name, grid=(nt,),
        in_specs=[pl.BlockSpec((tm, D), lambda i: (i, 0)), pl.BlockSpec((1, D), lambda i: (0, 0)), ANY],
        out_specs=[pl.BlockSpec((tm, D), lambda i: (i, 0)), ANY],
        out_shape=[jax.ShapeDtypeStruct((S, D), BF16)] + _gathered_shapes([block]),
        scratch_shapes=_gather_sems(1),
        compiler_params=_cp("arbitrary"),
    )(x, g, block)


def _matmul(a, b, *, trans_b, out_dtype, tm, tn, name):
    M, K = a.shape
    N = b.shape[0] if trans_b else b.shape[1]
    dn = NT if trans_b else NN

    def body(a_ref, b_ref, o_ref):
        o_ref[...] = _dot(a_ref[...], b_ref[...], dn).astype(out_dtype)

    b_spec = (pl.BlockSpec((tn, K), lambda i, j: (j, 0)) if trans_b
              else pl.BlockSpec((K, tn), lambda i, j: (0, j)))
    return pl.pallas_call(
        body, name=name, grid=(M // tm, N // tn),
        in_specs=[pl.BlockSpec((tm, K), lambda i, j: (i, 0)), b_spec],
        out_specs=pl.BlockSpec((tm, tn), lambda i, j: (i, j)),
        out_shape=jax.ShapeDtypeStruct((M, N), out_dtype),
        compiler_params=_cp("parallel", "parallel"),
    )(a, b)


def _matmul_tn(a_list, b, *, name, ts=1024):
    S, Ka = a_list[0].shape
    na = len(a_list)
    Nb = b.shape[1]
    ns = S // ts

    def body(*refs):
        a_refs, b_ref, o_ref, acc = refs[:na], refs[na], refs[na + 1], refs[na + 2]
        s = pl.program_id(0)

        @pl.when(s == 0)
        def _():
            acc[...] = jnp.zeros_like(acc)

        acc[...] += _dot(jnp.concatenate([r[...] for r in a_refs], axis=1), b_ref[...], TN)

        @pl.when(s == ns - 1)
        def _():
            o_ref[...] = acc[...].astype(BF16)

    return pl.pallas_call(
        body, name=name, grid=(ns,),
        in_specs=[pl.BlockSpec((ts, Ka), lambda s: (s, 0))] * na + [pl.BlockSpec((ts, Nb), lambda s: (s, 0))],
        out_specs=pl.BlockSpec((na * Ka, Nb), lambda s: (0, 0)),
        out_shape=jax.ShapeDtypeStruct((na * Ka, Nb), BF16),
        scratch_shapes=[pltpu.VMEM((na * Ka, Nb), F32)],
        compiler_params=_cp("arbitrary"),
    )(*a_list, b)


def _mix_out(attn, pool, w_out, x, g_post, g_next, *, name, tm=256):
    S, K = attn.shape
    D = w_out.shape[1]

    def body(a_ref, p_ref, w_ref, x_ref, gp_ref, gn_ref, mixed_ref, x2_ref, h2_ref):
        mixed = _dot(a_ref[...], w_ref[:K, :], NN) + _dot(p_ref[...], w_ref[K:, :], NN)
        r = lax.rsqrt(jnp.mean(mixed * mixed, axis=-1, keepdims=True) + EPS)
        x2 = x_ref[...] + mixed * r * gp_ref[...]
        r2 = lax.rsqrt(jnp.mean(x2 * x2, axis=-1, keepdims=True) + EPS)
        mixed_ref[...] = mixed
        x2_ref[...] = x2
        h2_ref[...] = (x2 * r2 * gn_ref[...]).astype(BF16)

    row = lambda i: (i, 0)
    fix = lambda i: (0, 0)
    return pl.pallas_call(
        body, name=name, grid=(S // tm,),
        in_specs=[pl.BlockSpec((tm, K), row), pl.BlockSpec((tm, K), row), pl.BlockSpec((2 * K, D), fix),
                  pl.BlockSpec((tm, D), row), pl.BlockSpec((1, D), fix), pl.BlockSpec((1, D), fix)],
        out_specs=[pl.BlockSpec((tm, D), row)] * 3,
        out_shape=[jax.ShapeDtypeStruct((S, D), F32), jax.ShapeDtypeStruct((S, D), F32),
                   jax.ShapeDtypeStruct((S, D), BF16)],
        compiler_params=_cp("parallel"),
    )(attn, pool, w_out, x, g_post, g_next)


def _ffn_out(y, w_down, x2, target, g_post, *, name, tm=512, sub=256):
    S, K = y.shape
    D = w_down.shape[1]

    def body(y_ref, w_ref, x2_ref, t_ref, g_ref, df_ref, dout_ref, loss_ref, gg_ref):
        i = pl.program_id(0)

        @pl.when(i == 0)
        def _():
            loss_ref[...] = jnp.zeros_like(loss_ref)
            gg_ref[...] = jnp.zeros_like(gg_ref)

        g = g_ref[...]
        w = w_ref[...]
        f_next = _dot(y_ref[0:sub, :], w, NN)
        for a in range(0, tm, sub):
            rows = slice(a, a + sub)
            f = f_next
            if a + sub < tm:
                f_next = _dot(y_ref[a + sub:a + 2 * sub, :], w, NN)
            r = lax.rsqrt(jnp.mean(f * f, axis=-1, keepdims=True) + EPS)
            out = x2_ref[rows, :] + f * r * g
            err = out - t_ref[rows, :]
            dy = err * (1.0 / D)
            df, dg = _rms_bwd(f, g, dy)
            df_ref[rows, :] = df.astype(BF16)
            dout_ref[rows, :] = dy
            gg_ref[...] += dg
            loss_ref[...] += 0.5 * jnp.sum(jnp.mean(err * err, axis=-1, keepdims=True))

    row = lambda i: (i, 0)
    fix = lambda i: (0, 0)
    return pl.pallas_call(
        body, name=name, grid=(S // tm,),
        in_specs=[pl.BlockSpec((tm, K), row), pl.BlockSpec((K, D), fix), pl.BlockSpec((tm, D), row),
                  pl.BlockSpec((tm, D), row), pl.BlockSpec((1, D), fix)],
        out_specs=[pl.BlockSpec((tm, D), row), pl.BlockSpec((tm, D), row),
                   pl.BlockSpec((8, 128), fix), pl.BlockSpec((1, D), fix)],
        out_shape=[jax.ShapeDtypeStruct((S, D), BF16), jax.ShapeDtypeStruct((S, D), F32),
                   jax.ShapeDtypeStruct((8, 128), F32), jax.ShapeDtypeStruct((1, D), F32)],
        compiler_params=_cp("arbitrary"),
    )(y, w_down, x2, target, g_post)


def _dgrad_norm(a_list, w, resid, xin, g, second, exchange, *, name, tm=512, sub=256):
    S, Kp = a_list[0].shape
    na = len(a_list)
    D = w.shape[1]
    nt = S // tm
    two = second is not None
    ng = len(exchange)
    recv_shapes, exchange_sems = _exchange_buffers(exchange)

    def body(*refs):
        a_refs = refs[:na]
        w_ref, r_ref, x_ref, g_ref = refs[na:na + 4]
        pos = na + 4
        if two:
            x2_ref, g2_ref = refs[pos:pos + 2]
            pos += 2
        g_refs = refs[pos:pos + ng]
        pos += ng
        dx_ref, gg_ref = refs[pos:pos + 2]
        pos += 2
        if two:
            d2_ref, gg2_ref = refs[pos:pos + 2]
            pos += 2
        r_refs = refs[pos:pos + ng]
        pos += ng
        i = pl.program_id(0)
        if ng:
            start, finish = _exchange_phases(g_refs, r_refs, *refs[pos:])
            pl.when(i == 0)(start)

        @pl.when(i == 0)
        def _():
            gg_ref[...] = jnp.zeros_like(gg_ref)
            if two:
                gg2_ref[...] = jnp.zeros_like(gg2_ref)

        def dh_of(a):
            return functools.reduce(jnp.add, [_dot(a_refs[q][a:a + sub, :], w_ref[q * Kp:(q + 1) * Kp, :], NN)
                                              for q in range(na)])

        dh_next = dh_of(0)
        for a in range(0, tm, sub):
            rows = slice(a, a + sub)
            dh = dh_next
            if a + sub < tm:
                dh_next = dh_of(a + sub)
            d1, dg1 = _rms_bwd(x_ref[rows, :], g_ref[...], dh)
            dx = r_ref[rows, :] + d1
            dx_ref[rows, :] = dx
            gg_ref[...] += dg1
            if two:
                d2, dg2 = _rms_bwd(x2_ref[rows, :], g2_ref[...], dx)
                d2_ref[rows, :] = d2.astype(BF16)
                gg2_ref[...] += dg2
        if ng:
            pl.when(i == nt - 1)(finish)

    row = lambda i: (i, 0)
    fix = lambda i: (0, 0)
    in_specs = [pl.BlockSpec((tm, Kp), row)] * na + [
        pl.BlockSpec((na * Kp, D), fix, pipeline_mode=pl.Buffered(1)), pl.BlockSpec((tm, D), row),
        pl.BlockSpec((tm, D), row), pl.BlockSpec((1, D), fix)]
    args = list(a_list) + [w, resid, xin, g]
    out_specs = [pl.BlockSpec((tm, D), row), pl.BlockSpec((1, D), fix)]
    out_shape = [jax.ShapeDtypeStruct((S, D), F32), jax.ShapeDtypeStruct((1, D), F32)]
    if two:
        in_specs += [pl.BlockSpec((tm, D), row), pl.BlockSpec((1, D), fix)]
        args += list(second)
        out_specs += [pl.BlockSpec((tm, D), row), pl.BlockSpec((1, D), fix)]
        out_shape += [jax.ShapeDtypeStruct((S, D), BF16), jax.ShapeDtypeStruct((1, D), F32)]
    n_plain = len(out_shape)
    out = pl.pallas_call(
        body, name=name, grid=(nt,), in_specs=in_specs + [ANY] * ng, out_specs=out_specs + [ANY] * ng,
        out_shape=out_shape + recv_shapes, scratch_shapes=exchange_sems if ng else [],
        compiler_params=_cp("arbitrary"),
    )(*args, *exchange)
    return (*out[:n_plain], out[n_plain:]) if ng else out


def _band_mask(first_block):
    qi = lax.broadcasted_iota(jnp.int32, (BLOCK, 2 * BLOCK), 0)
    ki = lax.broadcasted_iota(jnp.int32, (BLOCK, 2 * BLOCK), 1)
    first_key = jnp.where(first_block, BLOCK, 0)
    return (ki >= qi) & (ki <= qi + BLOCK) & (ki >= first_key)


def _lane_masks():
    lane = lax.broadcasted_iota(jnp.int32, (1, 2 * HEAD_DIM), 1)
    return (lane < HEAD_DIM, lane >= HEAD_DIM)


CHUNK = BLOCK * max(DILATIONS)
SLAB = 2 * HEAD_DIM
N_SLABS = ATTN_WIDTH // SLAB


def _unit_rows(d, b):
    def rows(r):
        start = r + BLOCK * d * b
        return pl.ds(start, BLOCK, stride=d) if d > 1 else pl.ds(start, BLOCK)
    return rows


def _attn_units():
    for p, d in enumerate(DILATIONS):
        nbc = CHUNK // (BLOCK * d)
        for b in range(nbc):
            for r in range(d):
                yield p, d, b, r, nbc


def _attn_in_specs(nc, n_cur):
    prev = lambda c: jnp.maximum(jnp.minimum(c, nc - 1) - 1, 0)
    cur = lambda c: jnp.minimum(c, nc - 1)
    blk = lambda f: pl.BlockSpec((CHUNK, SLAB), f)
    specs = [blk(lambda h, c: (cur(c), h)),
             blk(lambda h, c: (prev(c), N_SLABS + h)), blk(lambda h, c: (cur(c), N_SLABS + h)),
             blk(lambda h, c: (prev(c), 2 * N_SLABS + h)), blk(lambda h, c: (cur(c), 2 * N_SLABS + h))]
    return specs + [blk(lambda h, c: (cur(c), h))] * n_cur


def _attn_fwd(proj, payload, *, name):
    S = proj.shape[0]
    nc = S // CHUNK
    n = len(DILATIONS)
    npay = len(payload)
    n_steps = N_SLABS * nc

    def body(*refs):
        q_ref, kp_ref, kc_ref, vp_ref, vc_ref = refs[:5]
        pay_refs = refs[5:5 + npay]
        attn_ref, lse_ref, attn16_ref = refs[5 + npay:8 + npay]
        all_refs = refs[8 + npay:8 + 2 * npay]
        scr = refs[8 + 2 * npay:]
        o_scr, l_scr = scr[:n], scr[n:2 * n]
        start, forward, finish = _gather_phases(pay_refs, all_refs, *scr[2 * n:])
        step = pl.program_id(0) * nc + pl.program_id(1)
        pl.when(step == 0)(start)
        c = pl.program_id(1)
        lms = _lane_masks()
        plain, first = (jnp.tile(_band_mask(f), (2, 1)) for f in (False, c == 0))
        def scores(unit):
            p, d, b, r, nbc = unit
            rows = _unit_rows(d, b)(r)
            prow = _unit_rows(d, (b - 1) % nbc)(r)
            kpr, vpr = (kc_ref, vc_ref) if b > 0 else (kp_ref, vp_ref)
            q = q_ref[rows, :].astype(BF16)
            kcat = jnp.concatenate([kpr[prow, :], kc_ref[rows, :]], axis=0).astype(BF16)
            vcat = jnp.concatenate([vpr[prow, :], vc_ref[rows, :]], axis=0).astype(BF16)
            q2 = jnp.concatenate([jnp.where(lm, q, jnp.zeros_like(q)) for lm in lms], axis=0) * SCALE
            return p, rows, plain if b > 0 else first, vcat, _dot(q2, kcat, NT)

        units = list(_attn_units())
        nxt = scores(units[0])
        for k in range(len(units)):
            p, rows, mask2, vcat, s = nxt
            if k + 1 < len(units):
                nxt = scores(units[k + 1])
            s = jnp.where(mask2, s, NEG_INF)
            m = jnp.max(s, axis=-1, keepdims=True)
            e = jnp.exp(s - m)
            l = jnp.sum(e, axis=-1, keepdims=True)
            o2 = _dot(e.astype(BF16), vcat, NN) / l
            lse2 = m + jnp.log(l)
            o_scr[p][rows, :] = jnp.where(lms[0], o2[:BLOCK], o2[BLOCK:])
            l_scr[p][rows, :] = jnp.where(lms[0], lse2[:BLOCK], lse2[BLOCK:])
        ls = [l_scr[p][...] for p in range(n)]
        top = functools.reduce(jnp.maximum, ls)
        es = [jnp.exp(l - top) for l in ls]
        den = functools.reduce(jnp.add, es)
        num = functools.reduce(jnp.add, [e * o_scr[p][...] for p, e in enumerate(es)])
        attn = num / den
        attn_ref[...] = attn
        attn16_ref[...] = attn.astype(BF16)
        lse_ref[...] = top + jnp.log(den)
        pl.when(step == (2 * n_steps) // 3)(forward)
        pl.when(step == n_steps - 1)(finish)

    out = pl.pallas_call(
        body, name=name, grid=(N_SLABS, nc), in_specs=_attn_in_specs(nc, 0) + [ANY] * npay,
        out_specs=[pl.BlockSpec((CHUNK, SLAB), lambda h, c: (c, h))] * 3 + [ANY] * npay,
        out_shape=[jax.ShapeDtypeStruct((S, ATTN_WIDTH), F32)] * 2 + [jax.ShapeDtypeStruct((S, ATTN_WIDTH), BF16)]
        + _gathered_shapes(payload),
        scratch_shapes=[pltpu.VMEM((CHUNK, SLAB), F32)] * (2 * n) + _gather_sems(npay),
        compiler_params=_cp("arbitrary", "arbitrary"),
    )(proj, proj, proj, proj, proj, *payload)
    return (*out[:3], out[3:])


def _attn_bwd(proj, dcat, attn, lse, grads, *, name):
    S = proj.shape[0]
    nc = S // CHUNK
    ng = len(grads)
    n = len(DILATIONS)
    recv_shapes, exchange_sems = _exchange_buffers(grads)

    def body(*refs):
        q_ref, kp_ref, kc_ref, vp_ref, vc_ref, do_ref, o_ref, lse_ref = refs[:8]
        g_refs = refs[8:8 + ng]
        dq_ref, dk_ref, dv_ref = refs[8 + ng:11 + ng]
        r_refs = refs[11 + ng:11 + 2 * ng]
        scr = refs[11 + 2 * ng:]
        dk_prev, dv_prev = scr[:2]
        delta_h, lse_h = scr[2:4], scr[4:6]
        dq_p, dk_own, dk_back, dv_own, dv_back = (scr[6 + n * k:6 + n * (k + 1)] for k in range(5))
        start, finish = _exchange_phases(g_refs, r_refs, *scr[6 + 5 * n:])
        c = pl.program_id(1)
        pl.when((pl.program_id(0) == 0) & (c == 0))(start)

        @pl.when(c == 0)
        def _():
            dk_prev[...] = jnp.zeros_like(dk_prev)
            dv_prev[...] = jnp.zeros_like(dv_prev)

        @pl.when(c < nc)
        def _():
            lms = _lane_masks()
            plain, first = (jnp.tile(_band_mask(f), (2, 1)) for f in (False, c == 0))
            prod = do_ref[...] * o_ref[...]
            lse = lse_ref[...]
            lse_other = pltpu.roll(lse, HEAD_DIM, 1)
            for h, lm in enumerate(lms):
                delta = jnp.sum(jnp.where(lm, prod, 0.0), axis=-1, keepdims=True)
                delta_h[h][...] = jnp.broadcast_to(delta, (CHUNK, SLAB))
                lse_h[h][...] = jnp.where(lm, lse, lse_other)
            wide = lambda refs, rows: jnp.tile(jnp.concatenate([r[rows, :] for r in refs], axis=0), (1, 2))
            stack = lambda f: jnp.concatenate([f(lm) for lm in lms], axis=0)

            def scores(unit):
                p, d, b, r, nbc = unit
                rows = _unit_rows(d, b)(r)
                prow = _unit_rows(d, (b - 1) % nbc)(r)
                kpr, vpr = (kc_ref, vc_ref) if b > 0 else (kp_ref, vp_ref)
                q = q_ref[rows, :].astype(BF16)
                kcat = jnp.concatenate([kpr[prow, :], kc_ref[rows, :]], axis=0).astype(BF16)
                vcat = jnp.concatenate([vpr[prow, :], vc_ref[rows, :]], axis=0).astype(BF16)
                do = do_ref[rows, :]
                q2 = stack(lambda lm: jnp.where(lm, q, jnp.zeros_like(q))) * SCALE
                do2 = stack(lambda lm: jnp.where(lm, do, 0.0)).astype(BF16)
                return dict(p=p, rows=rows, prow=prow, mask2=plain if b > 0 else first, kcat=kcat, q2=q2, do2=do2,
                            s=_dot(q2, kcat, NT), dp=_dot(do2, vcat, NT))

            units = list(_attn_units())
            nxt = scores(units[0])
            for k in range(len(units)):
                u = nxt
                if k + 1 < len(units):
                    nxt = scores(units[k + 1])
                p, rows, prow, kcat = u["p"], u["rows"], u["prow"], u["kcat"]
                e = jnp.where(u["mask2"], jnp.exp(u["s"] - wide(lse_h, rows)), 0.0)
                ds = (e * (u["dp"] - wide(delta_h, rows))).astype(BF16)
                dq2 = _dot(ds, kcat, NN) * SCALE
                dq = jnp.where(lms[0], dq2[:BLOCK], dq2[BLOCK:])
                dkc = _dot(ds, u["q2"], TN)
                dvc = _dot(e.astype(BF16), u["do2"], TN)
                dq_p[p][rows, :] = dq
                dk_own[p][rows, :] = dkc[BLOCK:]
                dv_own[p][rows, :] = dvc[BLOCK:]
                dk_back[p][prow, :] = dkc[:BLOCK]
                dv_back[p][prow, :] = dvc[:BLOCK]
            dq_ref[...] = functools.reduce(jnp.add, [r[...] for r in dq_p]).astype(BF16)
            for prev, own, back, out_ref in ((dk_prev, dk_own, dk_back, dk_ref), (dv_prev, dv_own, dv_back, dv_ref)):
                for p, d in enumerate(DILATIONS):
                    tail = CHUNK - BLOCK * d
                    prev[tail:, :] += back[p][tail:, :]
                out_ref[...] = prev[...].astype(BF16)
                prev[...] = functools.reduce(jnp.add, [r[...] for r in own])
                for p, d in enumerate(DILATIONS):
                    tail = CHUNK - BLOCK * d
                    if tail:
                        prev[:tail, :] += back[p][:tail, :]

        @pl.when(c == nc)
        def _():
            dk_ref[...] = dk_prev[...].astype(BF16)
            dv_ref[...] = dv_prev[...].astype(BF16)

        pl.when((pl.program_id(0) == N_SLABS - 1) & (c == nc))(finish)

    blk = lambda f: pl.BlockSpec((CHUNK, SLAB), f)
    late = lambda h, c: (jnp.maximum(c - 1, 0), h)
    out = pl.pallas_call(
        body, name=name, grid=(N_SLABS, nc + 1), in_specs=_attn_in_specs(nc, 3) + [ANY] * ng,
        out_specs=[blk(lambda h, c: (jnp.minimum(c, nc - 1), h)), blk(late), blk(late)] + [ANY] * ng,
        out_shape=[jax.ShapeDtypeStruct((S, ATTN_WIDTH), BF16)] * 3 + recv_shapes,
        scratch_shapes=[pltpu.VMEM((CHUNK, SLAB), F32)] * (6 + 5 * n) + exchange_sems,
        compiler_params=_cp("arbitrary", "arbitrary"),
    )(proj, proj, proj, proj, proj, dcat, attn, lse, *grads)
    return out[:3], out[3:]


def _split_bf16(a):
    hi = a.astype(BF16)
    lo = (a - hi.astype(F32)).astype(BF16)
    return hi, lo


def _pooled(ug, halo_g, w, row0, tm):
    ext = jnp.concatenate([halo_g, ug], axis=0)
    hi, lo = _split_bf16(ext)
    rr = lax.broadcasted_iota(jnp.int32, (tm, tm + HALO), 0)
    cc = lax.broadcasted_iota(jnp.int32, (tm, tm + HALO), 1)
    back = rr + HALO - cc
    win = ((back >= 0) & (back < w)).astype(BF16)
    wsum = _dot(win, hi, NN) + _dot(win, lo, NN)
    rows = row0 + lax.broadcasted_iota(jnp.int32, (tm, 1), 0)
    inv = 1.0 / jnp.minimum(rows + 1, w).astype(F32)
    return wsum * inv - ug


def _pool_fwd(u, u_col, pool_w, pool_scale, *, name, tm=256):
    S, W = u.shape[0], POOL_WIDTH
    G = POOL_GROUP_DIM

    def body(u_ref, h_ref, w_ref, s_ref, o_ref):
        i = pl.program_id(0)
        uv = u_ref[...]
        halo = jnp.where(i > 0, h_ref[...], 0.0)
        sls = [slice(g * G, (g + 1) * G) for g in range(len(POOL_WINDOWS))]
        pooled = [_pooled(uv[:, sl], halo[:, sl], w, i * tm, tm) for sl, w in zip(sls, POOL_WINDOWS)]
        zs = [_dot(p.astype(BF16), w_ref[g].astype(BF16), NN) for g, p in enumerate(pooled)]
        for sl, z in zip(sls, zs):
            o_ref[:, sl] = (z * s_ref[:, sl]).astype(BF16)

    per = tm // HALO
    return pl.pallas_call(
        body, name=name, grid=(S // tm,),
        in_specs=[pl.BlockSpec((tm, W), lambda i: (i, u_col)),
                  pl.BlockSpec((HALO, W), lambda i: (jnp.maximum(i * per - 1, 0), u_col)),
                  pl.BlockSpec((len(POOL_WINDOWS), G, G), lambda i: (0, 0, 0)),
                  pl.BlockSpec((1, W), lambda i: (0, 0))],
        out_specs=pl.BlockSpec((tm, W), lambda i: (i, 0)),
        out_shape=jax.ShapeDtypeStruct((S, W), BF16),
        compiler_params=_cp("parallel"),
    )(u, u, pool_w, pool_scale)


def _pool_bwd(u, u_col, dy, dy_col, pool_w, pool_scale, *, name, tm=256):
    S, W = u.shape[0], POOL_WIDTH
    G = POOL_GROUP_DIM
    nt = S // tm

    def body(u_ref, h_ref, dy_ref, dyn_ref, w_ref, s_ref, du_ref, gw_ref, gs_ref):
        i = pl.program_id(0)

        @pl.when(i == 0)
        def _():
            gw_ref[...] = jnp.zeros_like(gw_ref)
            gs_ref[...] = jnp.zeros_like(gs_ref)

        uv = u_ref[...]
        halo = jnp.where(i > 0, h_ref[...], 0.0)
        dyv = dy_ref[...]
        dyn = jnp.where(i < nt - 1, dyn_ref[...], 0.0)
        rr = lax.broadcasted_iota(jnp.int32, (tm, tm + HALO), 0)
        cc = lax.broadcasted_iota(jnp.int32, (tm, tm + HALO), 1)
        rows_ext = i * tm + lax.broadcasted_iota(jnp.int32, (tm + HALO, 1), 0)
        groups = list(enumerate(POOL_WINDOWS))
        sls = [slice(g * G, (g + 1) * G) for g, _ in groups]
        wgs = [w_ref[g].astype(BF16) for g, _ in groups]
        pooled = [_pooled(uv[:, sl], halo[:, sl], w, i * tm, tm).astype(BF16) for sl, (_, w) in zip(sls, groups)]
        dzs = [dyv[:, sl] * s_ref[:, sl] for sl in sls]
        dz_ext = [jnp.concatenate([dz, dyn[:, sl] * s_ref[:, sl]], axis=0).astype(BF16) for dz, sl in zip(dzs, sls)]
        dp_ext = [_dot(d, wg, NT) for d, wg in zip(dz_ext, wgs)]
        zs = [_dot(p, wg, NN) for p, wg in zip(pooled, wgs)]
        for (g, w), sl, p, dz, z, dp in zip(groups, sls, pooled, dzs, zs, dp_ext):
            gw_ref[g] += _dot(p, dz.astype(BF16), TN)
            gs_ref[:, sl] += jnp.sum(dyv[:, sl] * z, axis=0, keepdims=True)
            inv_ext = 1.0 / jnp.minimum(rows_ext + 1, w).astype(F32)
            hi, lo = _split_bf16(dp * inv_ext)
            ahead = cc - rr
            win = ((ahead >= 0) & (ahead < w)).astype(BF16)
            du_ref[:, sl] = (_dot(win, hi, NN) + _dot(win, lo, NN) - dp[:tm]).astype(BF16)

    per = tm // HALO
    nh = S // HALO
    return pl.pallas_call(
        body, name=name, grid=(nt,),
        in_specs=[pl.BlockSpec((tm, W), lambda i: (i, u_col)),
                  pl.BlockSpec((HALO, W), lambda i: (jnp.maximum(i * per - 1, 0), u_col)),
                  pl.BlockSpec((tm, W), lambda i: (i, dy_col)),
                  pl.BlockSpec((HALO, W), lambda i: (jnp.minimum((i + 1) * per, nh - 1), dy_col)),
                  pl.BlockSpec((len(POOL_WINDOWS), G, G), lambda i: (0, 0, 0)),
                  pl.BlockSpec((1, W), lambda i: (0, 0))],
        out_specs=[pl.BlockSpec((tm, W), lambda i: (i, 0)),
                   pl.BlockSpec((len(POOL_WINDOWS), G, G), lambda i: (0, 0, 0)),
                   pl.BlockSpec((1, W), lambda i: (0, 0))],
        out_shape=[jax.ShapeDtypeStruct((S, W), BF16),
                   jax.ShapeDtypeStruct((len(POOL_WINDOWS), G, G), F32),
                   jax.ShapeDtypeStruct((1, W), F32)],
        compiler_params=_cp("arbitrary"),
    )(u, u, dy, dy, pool_w, pool_scale)


GELU_K0 = math.sqrt(2.0 / math.pi)
GELU_K1 = 0.044715


def _gelu_parts(x):
    x2 = x * x
    t = jnp.tanh(x * (GELU_K0 + (GELU_K0 * GELU_K1) * x2))
    hp = 0.5 + 0.5 * t
    gelu = x * hp
    dgelu = hp + (x * (hp * (1.0 - t))) * (GELU_K0 + (3.0 * GELU_K0 * GELU_K1) * x2)
    return gelu, dgelu


def _shifted(ext, halo):
    return (pltpu.roll(ext, 2, 0)[halo:], pltpu.roll(ext, 1, 0)[halo:], ext[halo:])


def _conv(sh, w, b):
    return b + (sh[0] * w[0:1] + sh[1] * w[1:2] + sh[2] * w[2:3])


F32_ROWS = 8


def _ffn_up_glu(h, w_up_t, conv_w, conv_b, *, name, tm=2048, tn=256, sub=256):
    S, K = h.shape
    F = D_FF
    nj = F // tn

    def body(h_ref, wg_ref, wv_ref, cwg_ref, cwv_ref, cbg_ref, cbv_ref,
             ug_ref, uv_ref, cg_ref, cv_ref, y_ref, carry):
        i = pl.program_id(0)
        j = pl.program_id(1)

        w_cat = jnp.concatenate([wg_ref[...], wv_ref[...]], axis=0)
        conv_w_b = ((cwg_ref[...], cbg_ref[...]), (cwv_ref[...], cbv_ref[...]))
        halo = [jnp.where(i > 0, carry[j, s], 0.0) for s in range(2)]
        u_next = _dot(h_ref[0:sub, :], w_cat, NT)
        for a in range(0, tm, sub):
            u16 = u_next.astype(BF16)
            if a + sub < tm:
                u_next = _dot(h_ref[a + sub:a + 2 * sub, :], w_cat, NT)
            ug_ref[a:a + sub, :] = u16[:, :tn]
            uv_ref[a:a + sub, :] = u16[:, tn:]
            c = []
            for s, (cw, cb) in enumerate(conv_w_b):
                u = u16[:, s * tn:(s + 1) * tn].astype(F32)
                ext = jnp.concatenate([halo[s], u], axis=0)
                c.append(_conv(_shifted(ext, F32_ROWS), cw, cb))
                halo[s] = u[sub - F32_ROWS:]
            cg_ref[a:a + sub, :] = c[0].astype(BF16)
            cv_ref[a:a + sub, :] = c[1].astype(BF16)
            gelu, _ = _gelu_parts(c[0])
            y_ref[a:a + sub, :] = (gelu * c[1]).astype(BF16)
        for s in range(2):
            carry[j, s] = halo[s]

    tile = pl.BlockSpec((tm, tn), lambda i, j: (i, j))
    vec = lambda rows, off: pl.BlockSpec((rows, tn), lambda i, j: (0, j + off))
    return pl.pallas_call(
        body, name=name, grid=(S // tm, nj),
        in_specs=[pl.BlockSpec((tm, K), lambda i, j: (i, 0)),
                  pl.BlockSpec((tn, K), lambda i, j: (j, 0)), pl.BlockSpec((tn, K), lambda i, j: (j + nj, 0)),
                  vec(3, 0), vec(3, nj), vec(1, 0), vec(1, nj)],
        out_specs=[tile] * 5,
        out_shape=[jax.ShapeDtypeStruct((S, F), BF16)] * 5,
        scratch_shapes=[pltpu.VMEM((nj, 2, F32_ROWS, tn), F32)],
        compiler_params=_cp("arbitrary", "arbitrary"),
    )(h, w_up_t, w_up_t, conv_w, conv_w, conv_b, conv_b)


def _ffn_glu_bwd(u_g, u_v, c_g, c_v, df, w_down, h, conv_w, *, name, tm=2048, tn=256, sub=256):
    S = u_g.shape[0]
    F = D_FF
    D = df.shape[1]
    nj = F // tn
    nt = S // tm

    def body(ug_ref, uv_ref, cg_ref, cgn_ref, cv_ref, cvn_ref, df_ref, dfn_ref, wd_ref, h_ref, wg_ref, wv_ref,
             dug_ref, duv_ref, gug_ref, guv_ref, gd_ref, gwg_ref, gwv_ref, gbg_ref, gbv_ref,
             acc_u, acc_d):
        i = pl.program_id(1)

        @pl.when(i == 0)
        def _():
            for r in (gwg_ref, gwv_ref, gbg_ref, gbv_ref, acc_u, acc_d):
                r[...] = jnp.zeros_like(r)

        wg, wv = wg_ref[...], wv_ref[...]
        wd = wd_ref[...]
        dfn = jnp.where(i < nt - 1, dfn_ref[...], jnp.zeros_like(dfn_ref))
        n_ext = sub + HALO

        def ahead(dc):
            return dc[:sub], pltpu.roll(dc, n_ext - 1, 0)[:sub], pltpu.roll(dc, n_ext - 2, 0)[:sub]

        def ext(ref, nxt, a):
            b = a + sub
            return jnp.concatenate([ref[a:b, :], ref[b:b + HALO, :] if b < tm else nxt], axis=0)

        dy_next = _dot(ext(df_ref, dfn, 0), wd, NT)
        for a in range(0, tm, sub):
            b = a + sub
            dy_ext = dy_next
            if b < tm:
                dy_next = _dot(ext(df_ref, dfn, b), wd, NT)
            cg = ext(cg_ref, cgn_ref[...], a).astype(F32)
            cv = ext(cv_ref, cvn_ref[...], a).astype(F32)
            df_sub = df_ref[a:b, :]
            gelu, dgelu = _gelu_parts(cg)
            dcs_g = ahead(dy_ext * cv * dgelu)
            dcs_v = ahead(dy_ext * gelu)
            du_g = (dcs_g[0] * wg[2:3] + dcs_g[1] * wg[1:2] + dcs_g[2] * wg[0:1]).astype(BF16)
            du_v = (dcs_v[0] * wv[2:3] + dcs_v[1] * wv[1:2] + dcs_v[2] * wv[0:1]).astype(BF16)
            dug_ref[a:b, :] = du_g
            duv_ref[a:b, :] = du_v
            acc_u[...] += _dot(jnp.concatenate([du_g, du_v], axis=1), h_ref[a:b, :], TN)
            acc_d[...] += _dot((gelu[:sub] * cv[:sub]).astype(BF16), df_sub, TN)
            for dcs, u_ref, gw_ref, gb_ref in ((dcs_g, ug_ref, gwg_ref, gbg_ref), (dcs_v, uv_ref, gwv_ref, gbv_ref)):
                u = u_ref[a:b, :].astype(F32)
                gb_ref[...] += jnp.sum(dcs[0], axis=0, keepdims=True)
                for k in range(3):
                    gw_ref[k:k + 1, :] += jnp.sum(dcs[2 - k] * u, axis=0, keepdims=True)

        @pl.when(i == nt - 1)
        def _():
            gug_ref[...] = acc_u[:tn, :].astype(BF16)
            guv_ref[...] = acc_u[tn:, :].astype(BF16)
            gd_ref[...] = acc_d[...].astype(BF16)

    per = tm // HALO
    nh = S // HALO
    hnext = lambda i: jnp.minimum((i + 1) * per, nh - 1)
    tile = pl.BlockSpec((tm, tn), lambda j, i: (i, j))
    hn = pl.BlockSpec((HALO, tn), lambda j, i: (hnext(i), j))
    vec = lambda rows, off: pl.BlockSpec((rows, tn), lambda j, i: (0, j + off))
    wide = pl.BlockSpec((tm, D), lambda j, i: (i, 0))
    wrow = pl.BlockSpec((tn, D), lambda j, i: (j, 0))
    return pl.pallas_call(
        body, name=name, grid=(nj, nt),
        in_specs=[tile, tile, tile, hn, tile, hn, wide, pl.BlockSpec((HALO, D), lambda j, i: (hnext(i), 0)),
                  wrow, wide, vec(3, 0), vec(3, nj)],
        out_specs=[tile, tile, wrow, wrow, wrow, vec(3, 0), vec(3, 0), vec(1, 0), vec(1, 0)],
        out_shape=[jax.ShapeDtypeStruct((S, F), BF16), jax.ShapeDtypeStruct((S, F), BF16),
                   jax.ShapeDtypeStruct((F, D), BF16), jax.ShapeDtypeStruct((F, D), BF16),
                   jax.ShapeDtypeStruct((F, D), BF16),
                   jax.ShapeDtypeStruct((3, F), F32), jax.ShapeDtypeStruct((3, F), F32),
                   jax.ShapeDtypeStruct((1, F), F32), jax.ShapeDtypeStruct((1, F), F32)],
        scratch_shapes=[pltpu.VMEM((2 * tn, D), F32), pltpu.VMEM((tn, D), F32)],
        compiler_params=_cp("parallel", "arbitrary"),
    )(u_g, u_v, c_g, c_g, c_v, c_v, df, df, w_down, h, conv_w, conv_w)


def _sum_partials(parts, *, name, tr):
    _, R, C = parts.shape

    def body(p_ref, o_ref):
        tot = p_ref[0].astype(F32)
        for j in range(1, N_DEV):
            tot = tot + p_ref[j].astype(F32)
        o_ref[...] = tot

    return pl.pallas_call(
        body, name=name, grid=(R // tr,),
        in_specs=[pl.BlockSpec((N_DEV, tr, C), lambda i: (0, i, 0))],
        out_specs=pl.BlockSpec((tr, C), lambda i: (i, 0)),
        out_shape=jax.ShapeDtypeStruct((R, C), F32),
        compiler_params=_cp("parallel"),
    )(parts)


def _adamw(w, g, m, v, *, name, tr):
    R, C = w.shape
    c1 = 1.0 - ADAM_B1 ** ADAM_STEP
    c2 = 1.0 - ADAM_B2 ** ADAM_STEP

    def body(w_ref, g_ref, m_ref, v_ref, d_ref, nm_ref, nv_ref):
        g = g_ref[...]
        nm = ADAM_B1 * m_ref[...] + (1.0 - ADAM_B1) * g
        nv = ADAM_B2 * v_ref[...] + (1.0 - ADAM_B2) * (g * g)
        d_ref[...] = -ADAM_LR * ((nm / c1) / (jnp.sqrt(nv / c2) + ADAM_EPS) + ADAM_WD * w_ref[...])
        nm_ref[...] = nm
        nv_ref[...] = nv

    spec = pl.BlockSpec((tr, C), lambda i: (i, 0))
    return pl.pallas_call(
        body, name=name, grid=(R // tr,), in_specs=[spec] * 4, out_specs=[spec] * 3,
        out_shape=[jax.ShapeDtypeStruct((R, C), F32)] * 3,
        compiler_params=_cp("parallel"),
    )(w, g, m, v)


def _mesh_pos():
    return lax.axis_index("x"), lax.axis_index("y"), lax.axis_index("c")


def _gather_phases(x_refs, out_refs, send_sems, recv_sems, local_sems):
    x, y, c = _mesh_pos()
    me, sibling = (x, y, c), (x, y, 1 - c)
    chips = [(1 - x, y), (x, 1 - y), (1 - x, 1 - y)]
    arrays = range(len(x_refs))

    def slot(a, px, py, pc):
        return out_refs[a].at[4 * px + 2 * py + pc]

    def copy(a, k, block, to, own=False):
        return pltpu.make_async_remote_copy(
            src_ref=x_refs[a] if own else slot(a, *block), dst_ref=slot(a, *block),
            send_sem=send_sems.at[a, k], recv_sem=recv_sems.at[a, k], device_id=to, device_id_type=MESH)

    mine = [pltpu.make_async_copy(x_refs[a], slot(a, *me), local_sems.at[a]) for a in arrays]
    first = [copy(a, 0, me, sibling, own=True) for a in arrays]
    first += [copy(a, 1 + j, me, (*chip, c), own=True) for j, chip in enumerate(chips) for a in arrays]
    passed = [[copy(a, 4 + j, (*chip, c), sibling) for a in arrays] for j, chip in enumerate(chips)]

    def start():
        for cp in mine + first:
            cp.start()

    def forward():
        for j, chip in enumerate(chips):
            for a in arrays:
                copy(a, 1 + j, (*chip, c), me).wait_recv()
                passed[j][a].start()

    def finish():
        for a in arrays:
            copy(a, 0, sibling, me).wait_recv()
            for j, chip in enumerate(chips):
                copy(a, 4 + j, (*chip, 1 - c), me).wait_recv()
        for cp in first + [cp for row in passed for cp in row]:
            cp.wait_send()
        for cp in mine:
            cp.wait()

    return start, forward, finish


def _gather_sems(n):
    return [pltpu.SemaphoreType.DMA((n, 7)), pltpu.SemaphoreType.DMA((n, 7)), pltpu.SemaphoreType.DMA((n,))]


def _gathered_shapes(blocks):
    return [jax.ShapeDtypeStruct((N_DEV,) + b.shape, b.dtype) for b in blocks]


def _all_reduce_small(block, *, name):
    def body(x_ref, all_ref, sum_ref, *sems):
        for phase in _gather_phases([x_ref], [all_ref], *sems):
            phase()
        tot = all_ref[0]
        for j in range(1, N_DEV):
            tot = tot + all_ref[j]
        sum_ref[...] = tot

    return pl.pallas_call(
        body, name=name, in_specs=[VMEM], out_specs=[VMEM, VMEM],
        out_shape=[jax.ShapeDtypeStruct((N_DEV,) + block.shape, block.dtype),
                   jax.ShapeDtypeStruct(block.shape, block.dtype)],
        scratch_shapes=_gather_sems(1),
        compiler_params=pltpu.CompilerParams(vmem_limit_bytes=V7X_VMEM_LIMIT),
    )(block)[1]


def _exchange_phases(g_refs, r_refs, send_sems, recv_sems, local_sems):
    x, y, c = _mesh_pos()
    me = 4 * x + 2 * y + c
    owns, remote = [], []
    for k, (g_ref, r_ref) in enumerate(zip(g_refs, r_refs)):
        rows = g_ref.shape[0] // N_DEV
        owns.append(pltpu.make_async_copy(g_ref.at[pl.ds(me * rows, rows)], r_ref.at[me], local_sems.at[k]))
        for p in range(1, N_DEV):
            px, py, pc = x ^ (p >> 2), y ^ ((p >> 1) & 1), c ^ (p & 1)
            peer = 4 * px + 2 * py + pc
            link = dict(send_sem=send_sems.at[k, p], recv_sem=recv_sems.at[k, p],
                        device_id=(px, py, pc), device_id_type=MESH)
            src = g_ref.at[pl.ds(peer * rows, rows)]
            send = pltpu.make_async_remote_copy(src_ref=src, dst_ref=r_ref.at[me], **link)
            arrival = pltpu.make_async_remote_copy(src_ref=src, dst_ref=r_ref.at[peer], **link)
            remote.append((send, arrival))

    def start():
        for own in owns:
            own.start()
        for send, _ in remote:
            send.start()

    def finish():
        for _, arrival in remote:
            arrival.wait_recv()
        for send, _ in remote:
            send.wait_send()
        for own in owns:
            own.wait()

    return start, finish


def _exchange_buffers(grads):
    n = len(grads)
    shapes = [jax.ShapeDtypeStruct((N_DEV, g.shape[0] // N_DEV, g.shape[1]), g.dtype) for g in grads]
    sems = [pltpu.SemaphoreType.DMA((n, N_DEV)), pltpu.SemaphoreType.DMA((n, N_DEV)),
            pltpu.SemaphoreType.DMA((n,))]
    return shapes, sems


def _unpack_gathered(gathered):
    w_out, w_up_t, w_down = (g.reshape(-1, D_MODEL) for g in gathered[:3])
    width = 2 * D_FF // N_DEV
    conv_w = jnp.transpose(gathered[3][:, :3, :width], (1, 0, 2)).reshape(3, 2 * D_FF)
    return w_out, w_up_t, w_down, conv_w


def _rest_payload(w_out, w_up, w_down, conv_w):
    rows, cols = conv_w.shape
    conv_w = jnp.pad(conv_w, ((0, (-rows) % F32_ROWS), (0, (-cols) % LANES)))
    return [w_out.astype(BF16), w_up.T.astype(BF16), w_down.astype(BF16), conv_w]


def _device_step(x, target, g_mix_pre, w_in_t_block, rest_payload, pool_w, pool_scale, g_mix_post, g_ffn_pre,
                 conv_b, g_ffn_post):
    h1, w_in_t = _rms_norm_gather(x, g_mix_pre, w_in_t_block, name="rms_mix_pre")
    w_in_t = w_in_t.reshape(-1, D_MODEL)
    proj = _matmul(h1, w_in_t, trans_b=True, out_dtype=F32, tm=512, tn=4 * ATTN_WIDTH, name="proj")
    attn, lse, attn16, gathered = _attn_fwd(proj, rest_payload, name="attn_fwd")
    w_out, w_up_t, w_down, conv_w = _unpack_gathered(gathered)
    pool = _pool_fwd(proj, 3, pool_w, pool_scale, name="pool_fwd")
    mixed, x2, h2 = _mix_out(attn16, pool, w_out, x, g_mix_post, g_ffn_pre, name="mix_out")
    u_g, u_v, c_g, c_v, y = _ffn_up_glu(h2, w_up_t, conv_w, conv_b, name="ffn_up_glu")
    df, d_out, loss_blk, gg_ffn_post = _ffn_out(y, w_down, x2, target, g_ffn_post, name="ffn_out")
    du_g, du_v, gw_up_g, gw_up_v, gw_down, gcw_g, gcw_v, gcb_g, gcb_v = _ffn_glu_bwd(
        u_g, u_v, c_g, c_v, df, w_down, h2, conv_w, name="ffn_glu_bwd")
    gw_up_t = jnp.concatenate([gw_up_g, gw_up_v], axis=0)
    dx2, gg_ffn_pre, dmixed, gg_mix_post = _dgrad_norm(
        [du_g, du_v], w_up_t, d_out, x2, g_ffn_pre, (mixed, g_mix_post), [], name="ffn_up_dgrad")
    gw_out = _matmul_tn([attn16, pool], dmixed, name="grad_w_out")
    dcat = _matmul(dmixed, w_out, trans_b=True, out_dtype=F32, tm=512, tn=1024, name="mix_out_dgrad")
    d_pool_in, g_pool_w, g_pool_scale = _pool_bwd(proj, 3, dcat, 1, pool_w, pool_scale, name="pool_bwd")
    dqkv, (r_out, r_up_t, r_down) = _attn_bwd(proj, dcat, attn, lse, [gw_out, gw_up_t, gw_down], name="attn_bwd")
    dproj = list(dqkv) + [d_pool_in]
    gw_in_t = _matmul_tn(dproj, h1, name="grad_w_in")
    grad_x, gg_mix_pre, (r_in_t,) = _dgrad_norm(dproj, w_in_t, dx2, x, g_mix_pre, None, [gw_in_t], name="proj_dgrad")
    g_conv_w = jnp.concatenate([gcw_g, gcw_v], axis=1)
    g_conv_b = jnp.concatenate([gcb_g, gcb_v], axis=1)
    received = (r_in_t, r_out, r_up_t, r_down)
    small = dict(g_mix_pre=gg_mix_pre, g_mix_post=gg_mix_post, g_ffn_pre=gg_ffn_pre, g_ffn_post=gg_ffn_post,
                 pool_scale=g_pool_scale, conv_b=g_conv_b, pool_w=g_pool_w, conv_w=g_conv_w)
    return loss_blk, grad_x, received, small


_SMALL = ("g_mix_pre", "g_mix_post", "g_ffn_pre", "g_ffn_post", "pool_scale", "conv_b", "pool_w")
LANES = 128


def _pack_rows(arrays):
    parts = []
    for a in arrays:
        a2 = a.reshape(-1, LANES)
        parts.append(jnp.pad(a2, ((0, (-a2.shape[0]) % 8), (0, 0))))
    return jnp.concatenate(parts, axis=0)


def _unpack_rows(packed, shapes):
    out, row = [], 0
    for shape in shapes:
        rows = math.prod(shape) // LANES
        out.append(packed[row:row + rows].reshape(shape))
        row += -(-rows // 8) * 8
    return out


def kernel(x, g_mix_pre, w_in, pool_w, pool_scale, w_out, g_mix_post, g_ffn_pre, w_up, conv_w, conv_b, w_down, g_ffn_post, loss_target, m_g_mix_pre, m_w_in, m_pool_w, m_pool_scale, m_w_out, m_g_mix_post, m_g_ffn_pre, m_w_up, m_conv_w, m_conv_b, m_w_down, m_g_ffn_post, v_g_mix_pre, v_w_in, v_pool_w, v_pool_scale, v_w_out, v_g_mix_post, v_g_ffn_pre, v_w_up, v_conv_w, v_conv_b, v_w_down, v_g_ffn_post):
    me = 4 * lax.axis_index("x") + 2 * lax.axis_index("y") + lax.axis_index("c")
    loss_blk, grad_x, recv, small = _device_step(
        x[0], loss_target[0], g_mix_pre, w_in[0].T.astype(BF16),
        _rest_payload(w_out[0], w_up[0], w_down[0], conv_w[0]),
        pool_w[0], pool_scale, g_mix_post, g_ffn_pre, conv_b, g_ffn_post)

    g_in_t, g_out, g_up_t, g_down = (
        _sum_partials(r, name=f"sum_partials_{k}", tr=r.shape[1] // 2) for k, r in enumerate(recv))
    grads = {"w_in": g_in_t.T, "w_out": g_out, "w_up": g_up_t.T, "w_down": g_down}

    given = dict(g_mix_pre=g_mix_pre, g_mix_post=g_mix_post, g_ffn_pre=g_ffn_pre, g_ffn_post=g_ffn_post,
                 pool_scale=pool_scale, conv_b=conv_b, pool_w=pool_w)
    small_shapes = [given[k].shape for k in _SMALL]
    total = _all_reduce_small(_pack_rows([small[k] for k in _SMALL] + [small["conv_w"], loss_blk]),
                              name="all_reduce_small")
    *small_grads, g_conv_w_all, loss_all = _unpack_rows(total, small_shapes + [(3, 2 * D_FF), loss_blk.shape])
    loss = loss_all[0, 0]
    grads.update(zip(_SMALL, small_grads))
    width = 2 * D_FF // N_DEV
    grads["conv_w"] = lax.dynamic_slice_in_dim(g_conv_w_all, me * width, width, axis=1)[None]

    weights = dict(g_mix_pre=g_mix_pre, w_in=w_in, pool_w=pool_w, pool_scale=pool_scale, w_out=w_out,
                   g_mix_post=g_mix_post, g_ffn_pre=g_ffn_pre, w_up=w_up, conv_w=conv_w, conv_b=conv_b,
                   w_down=w_down, g_ffn_post=g_ffn_post)
    m_in = dict(g_mix_pre=m_g_mix_pre, w_in=m_w_in, pool_w=m_pool_w, pool_scale=m_pool_scale, w_out=m_w_out,
                g_mix_post=m_g_mix_post, g_ffn_pre=m_g_ffn_pre, w_up=m_w_up, conv_w=m_conv_w, conv_b=m_conv_b,
                w_down=m_w_down, g_ffn_post=m_g_ffn_post)
    v_in = dict(g_mix_pre=v_g_mix_pre, w_in=v_w_in, pool_w=v_pool_w, pool_scale=v_pool_scale, w_out=v_w_out,
                g_mix_post=v_g_mix_post, g_ffn_pre=v_g_ffn_pre, w_up=v_w_up, conv_w=v_conv_w, conv_b=v_conv_b,
                w_down=v_w_down, g_ffn_post=v_g_ffn_post)
    delta, new_m, new_v = {}, {}, {}
    for k in ("w_in", "w_out", "w_up", "w_down"):
        g = grads[k]
        d, nm, nv = _adamw(weights[k][0], g, m_in[k][0], v_in[k][0], name=f"adamw_{k}", tr=g.shape[0] // 2)
        grads[k], delta[k], new_m[k], new_v[k] = g[None], d[None], nm[None], nv[None]
    d, nm, nv = _adamw(weights["conv_w"][0], grads["conv_w"][0], m_in["conv_w"][0], v_in["conv_w"][0],
                       name="adamw_conv_w", tr=3)
    delta["conv_w"], new_m["conv_w"], new_v["conv_w"] = d[None], nm[None], nv[None]
    packed_w = _pack_rows([weights[k] for k in _SMALL])
    small_rows = packed_w.shape[0]
    d, nm, nv = _adamw(packed_w, total[:small_rows], _pack_rows([m_in[k] for k in _SMALL]),
                       _pack_rows([v_in[k] for k in _SMALL]), name="adamw_small", tr=small_rows)
    for k, dk, mk, vk in zip(_SMALL, _unpack_rows(d, small_shapes), _unpack_rows(nm, small_shapes),
                             _unpack_rows(nv, small_shapes)):
        delta[k], new_m[k], new_v[k] = dk, mk, vk

    order = ("g_mix_pre", "w_in", "pool_w", "pool_scale", "w_out", "g_mix_post", "g_ffn_pre", "w_up",
             "conv_w", "conv_b", "w_down", "g_ffn_post")
    return (loss, grad_x[None], *[grads[k] for k in order], *[delta[k] for k in order],
            *[new_m[k] for k in order], *[new_v[k] for k in order])
```

```python
import functools
import math

import jax
import jax.numpy as jnp
from jax import lax
from jax.experimental import pallas as pl
from jax.experimental.pallas import tpu as pltpu

F32 = jnp.float32
BF16 = jnp.bfloat16

D_MODEL = 1024
N_HEADS = 8
HEAD_DIM = 64
ATTN_WIDTH = N_HEADS * HEAD_DIM
DILATIONS = (1, 4, 16)
BLOCK = 128
POOL_WIDTH = 512
POOL_WINDOWS = (2, 4, 8, 16)
POOL_GROUP_DIM = 128
D_FF = 2816
EPS = 1e-6
NEG_INF = -1e30
SCALE = HEAD_DIM ** -0.5

ADAM_LR = 0.001
ADAM_B1 = 0.9
ADAM_B2 = 0.999
ADAM_EPS = 1e-08
ADAM_WD = 0.01
ADAM_STEP = 10

N_DEV = 8
HALO = 16
V7X_VMEM_LIMIT = 56 * 1024 * 1024

MESH = pl.DeviceIdType.MESH
ANY = pl.BlockSpec(memory_space=pl.ANY)
VMEM = pl.BlockSpec(memory_space=pltpu.VMEM)

NT = (((1,), (1,)), ((), ()))
NN = (((1,), (0,)), ((), ()))
TN = (((0,), (0,)), ((), ()))


def _cp(*sem):
    return pltpu.CompilerParams(dimension_semantics=sem, vmem_limit_bytes=V7X_VMEM_LIMIT)


def _dot(a, b, dn):
    return lax.dot_general(a, b, dn, preferred_element_type=F32)


def _rms_bwd(xin, g, dy):
    r = lax.rsqrt(jnp.mean(xin * xin, axis=-1, keepdims=True) + EPS)
    xh = xin * r
    gdy = g * dy
    dx = r * (gdy - xh * jnp.mean(gdy * xh, axis=-1, keepdims=True))
    dg = jnp.sum(dy * xh, axis=0, keepdims=True)
    return dx, dg


def _rms_norm_gather(x, g, block, *, name, tm=512):
    S, D = x.shape
    nt = S // tm

    def body(x_ref, g_ref, blk_ref, o_ref, all_ref, *sems):
        i = pl.program_id(0)
        start, forward, finish = _gather_phases([blk_ref], [all_ref], *sems)
        pl.when(i == 0)(start)
        xv = x_ref[...]
        r = lax.rsqrt(jnp.mean(xv * xv, axis=-1, keepdims=True) + EPS)
        o_ref[...] = (xv * r * g_ref[...]).astype(BF16)
        pl.when(i == nt - 1)(forward)
        pl.when(i == nt - 1)(finish)

    return pl.pallas_call(
        body, name=name, grid=(nt,),
        in_specs=[pl.BlockSpec((tm, D), lambda i: (i, 0)), pl.BlockSpec((1, D), lambda i: (0, 0)), ANY],
        out_specs=[pl.BlockSpec((tm, D), lambda i: (i, 0)), ANY],
        out_shape=[jax.ShapeDtypeStruct((S, D), BF16)] + _gathered_shapes([block]),
        scratch_shapes=_gather_sems(1),
        compiler_params=_cp("arbitrary"),
    )(x, g, block)


def _matmul(a, b, *, trans_b, out_dtype, tm, tn, name):
    M, K = a.shape
    N = b.shape[0] if trans_b else b.shape[1]
    dn = NT if trans_b else NN

    def body(a_ref, b_ref, o_ref):
        o_ref[...] = _dot(a_ref[...], b_ref[...], dn).astype(out_dtype)

    b_spec = (pl.BlockSpec((tn, K), lambda i, j: (j, 0)) if trans_b
              else pl.BlockSpec((K, tn), lambda i, j: (0, j)))
    return pl.pallas_call(
        body, name=name, grid=(M // tm, N // tn),
        in_specs=[pl.BlockSpec((tm, K), lambda i, j: (i, 0)), b_spec],
        out_specs=pl.BlockSpec((tm, tn), lambda i, j: (i, j)),
        out_shape=jax.ShapeDtypeStruct((M, N), out_dtype),
        compiler_params=_cp("parallel", "parallel"),
    )(a, b)


def _matmul_tn(a_list, b, *, name, ts=1024):
    S, Ka = a_list[0].shape
    na = len(a_list)
    Nb = b.shape[1]
    ns = S // ts

    def body(*refs):
        a_refs, b_ref, o_ref, acc = refs[:na], refs[na], refs[na + 1], refs[na + 2]
        s = pl.program_id(0)

        @pl.when(s == 0)
        def _():
            acc[...] = jnp.zeros_like(acc)

        acc[...] += _dot(jnp.concatenate([r[...] for r in a_refs], axis=1), b_ref[...], TN)

        @pl.when(s == ns - 1)
        def _():
            o_ref[...] = acc[...].astype(BF16)

    return pl.pallas_call(
        body, name=name, grid=(ns,),
        in_specs=[pl.BlockSpec((ts, Ka), lambda s: (s, 0))] * na + [pl.BlockSpec((ts, Nb), lambda s: (s, 0))],
        out_specs=pl.BlockSpec((na * Ka, Nb), lambda s: (0, 0)),
        out_shape=jax.ShapeDtypeStruct((na * Ka, Nb), BF16),
        scratch_shapes=[pltpu.VMEM((na * Ka, Nb), F32)],
        compiler_params=_cp("arbitrary"),
    )(*a_list, b)


def _mix_out(attn, pool, w_out, x, g_post, g_next, *, name, tm=256):
    S, K = attn.shape
    D = w_out.shape[1]

    def body(a_ref, p_ref, w_ref, x_ref, gp_ref, gn_ref, mixed_ref, x2_ref, h2_ref):
        mixed = _dot(a_ref[...], w_ref[:K, :], NN) + _dot(p_ref[...], w_ref[K:, :], NN)
        r = lax.rsqrt(jnp.mean(mixed * mixed, axis=-1, keepdims=True) + EPS)
        x2 = x_ref[...] + mixed * r * gp_ref[...]
        r2 = lax.rsqrt(jnp.mean(x2 * x2, axis=-1, keepdims=True) + EPS)
        mixed_ref[...] = mixed
        x2_ref[...] = x2
        h2_ref[...] = (x2 * r2 * gn_ref[...]).astype(BF16)

    row = lambda i: (i, 0)
    fix = lambda i: (0, 0)
    return pl.pallas_call(
        body, name=name, grid=(S // tm,),
        in_specs=[pl.BlockSpec((tm, K), row), pl.BlockSpec((tm, K), row), pl.BlockSpec((2 * K, D), fix),
                  pl.BlockSpec((tm, D), row), pl.BlockSpec((1, D), fix), pl.BlockSpec((1, D), fix)],
        out_specs=[pl.BlockSpec((tm, D), row)] * 3,
        out_shape=[jax.ShapeDtypeStruct((S, D), F32), jax.ShapeDtypeStruct((S, D), F32),
                   jax.ShapeDtypeStruct((S, D), BF16)],
        compiler_params=_cp("parallel"),
    )(attn, pool, w_out, x, g_post, g_next)


def _ffn_out(y, w_down, x2, target, g_post, *, name, tm=512, sub=256):
    S, K = y.shape
    D = w_down.shape[1]

    def body(y_ref, w_ref, x2_ref, t_ref, g_ref, df_ref, dout_ref, loss_ref, gg_ref):
        i = pl.program_id(0)

        @pl.when(i == 0)
        def _():
            loss_ref[...] = jnp.zeros_like(loss_ref)
            gg_ref[...] = jnp.zeros_like(gg_ref)

        g = g_ref[...]
        w = w_ref[...]
        f_next = _dot(y_ref[0:sub, :], w, NN)
        for a in range(0, tm, sub):
            rows = slice(a, a + sub)
            f = f_next
            if a + sub < tm:
                f_next = _dot(y_ref[a + sub:a + 2 * sub, :], w, NN)
            r = lax.rsqrt(jnp.mean(f * f, axis=-1, keepdims=True) + EPS)
            out = x2_ref[rows, :] + f * r * g
            err = out - t_ref[rows, :]
            dy = err * (1.0 / D)
            df, dg = _rms_bwd(f, g, dy)
            df_ref[rows, :] = df.astype(BF16)
            dout_ref[rows, :] = dy
            gg_ref[...] += dg
            loss_ref[...] += 0.5 * jnp.sum(jnp.mean(err * err, axis=-1, keepdims=True))

    row = lambda i: (i, 0)
    fix = lambda i: (0, 0)
    return pl.pallas_call(
        body, name=name, grid=(S // tm,),
        in_specs=[pl.BlockSpec((tm, K), row), pl.BlockSpec((K, D), fix), pl.BlockSpec((tm, D), row),
                  pl.BlockSpec((tm, D), row), pl.BlockSpec((1, D), fix)],
        out_specs=[pl.BlockSpec((tm, D), row), pl.BlockSpec((tm, D), row),
                   pl.BlockSpec((8, 128), fix), pl.BlockSpec((1, D), fix)],
        out_shape=[jax.ShapeDtypeStruct((S, D), BF16), jax.ShapeDtypeStruct((S, D), F32),
                   jax.ShapeDtypeStruct((8, 128), F32), jax.ShapeDtypeStruct((1, D), F32)],
        compiler_params=_cp("arbitrary"),
    )(y, w_down, x2, target, g_post)


def _dgrad_norm(a_list, w, resid, xin, g, second, exchange, *, name, tm=512, sub=256):
    S, Kp = a_list[0].shape
    na = len(a_list)
    D = w.shape[1]
    nt = S // tm
    two = second is not None
    ng = len(exchange)
    recv_shapes, exchange_sems = _exchange_buffers(exchange)

    def body(*refs):
        a_refs = refs[:na]
        w_ref, r_ref, x_ref, g_ref = refs[na:na + 4]
        pos = na + 4
        if two:
            x2_ref, g2_ref = refs[pos:pos + 2]
            pos += 2
        g_refs = refs[pos:pos + ng]
        pos += ng
        dx_ref, gg_ref = refs[pos:pos + 2]
        pos += 2
        if two:
            d2_ref, gg2_ref = refs[pos:pos + 2]
            pos += 2
        r_refs = refs[pos:pos + ng]
        pos += ng
        i = pl.program_id(0)
        if ng:
            start, finish = _exchange_phases(g_refs, r_refs, *refs[pos:])
            pl.when(i == 0)(start)

        @pl.when(i == 0)
        def _():
            gg_ref[...] = jnp.zeros_like(gg_ref)
            if two:
                gg2_ref[...] = jnp.zeros_like(gg2_ref)

        def dh_of(a):
            return functools.reduce(jnp.add, [_dot(a_refs[q][a:a + sub, :], w_ref[q * Kp:(q + 1) * Kp, :], NN)
                                              for q in range(na)])

        dh_next = dh_of(0)
        for a in range(0, tm, sub):
            rows = slice(a, a + sub)
            dh = dh_next
            if a + sub < tm:
                dh_next = dh_of(a + sub)
            d1, dg1 = _rms_bwd(x_ref[rows, :], g_ref[...], dh)
            dx = r_ref[rows, :] + d1
            dx_ref[rows, :] = dx
            gg_ref[...] += dg1
            if two:
                d2, dg2 = _rms_bwd(x2_ref[rows, :], g2_ref[...], dx)
                d2_ref[rows, :] = d2.astype(BF16)
                gg2_ref[...] += dg2
        if ng:
            pl.when(i == nt - 1)(finish)

    row = lambda i: (i, 0)
    fix = lambda i: (0, 0)
    in_specs = [pl.BlockSpec((tm, Kp), row)] * na + [
        pl.BlockSpec((na * Kp, D), fix, pipeline_mode=pl.Buffered(1)), pl.BlockSpec((tm, D), row),
        pl.BlockSpec((tm, D), row), pl.BlockSpec((1, D), fix)]
    args = list(a_list) + [w, resid, xin, g]
    out_specs = [pl.BlockSpec((tm, D), row), pl.BlockSpec((1, D), fix)]
    out_shape = [jax.ShapeDtypeStruct((S, D), F32), jax.ShapeDtypeStruct((1, D), F32)]
    if two:
        in_specs += [pl.BlockSpec((tm, D), row), pl.BlockSpec((1, D), fix)]
        args += list(second)
        out_specs += [pl.BlockSpec((tm, D), row), pl.BlockSpec((1, D), fix)]
        out_shape += [jax.ShapeDtypeStruct((S, D), BF16), jax.ShapeDtypeStruct((1, D), F32)]
    n_plain = len(out_shape)
    out = pl.pallas_call(
        body, name=name, grid=(nt,), in_specs=in_specs + [ANY] * ng, out_specs=out_specs + [ANY] * ng,
        out_shape=out_shape + recv_shapes, scratch_shapes=exchange_sems if ng else [],
        compiler_params=_cp("arbitrary"),
    )(*args, *exchange)
    return (*out[:n_plain], out[n_plain:]) if ng else out


def _band_mask(first_block):
    qi = lax.broadcasted_iota(jnp.int32, (BLOCK, 2 * BLOCK), 0)
    ki = lax.broadcasted_iota(jnp.int32, (BLOCK, 2 * BLOCK), 1)
    first_key = jnp.where(first_block, BLOCK, 0)
    return (ki >= qi) & (ki <= qi + BLOCK) & (ki >= first_key)


def _lane_masks():
    lane = lax.broadcasted_iota(jnp.int32, (1, 2 * HEAD_DIM), 1)
    return (lane < HEAD_DIM, lane >= HEAD_DIM)


CHUNK = BLOCK * max(DILATIONS)
SLAB = 2 * HEAD_DIM
N_SLABS = ATTN_WIDTH // SLAB


def _unit_rows(d, b):
    def rows(r):
        start = r + BLOCK * d * b
        return pl.ds(start, BLOCK, stride=d) if d > 1 else pl.ds(start, BLOCK)
    return rows


def _attn_units():
    for p, d in enumerate(DILATIONS):
        nbc = CHUNK // (BLOCK * d)
        for b in range(nbc):
            for r in range(d):
                yield p, d, b, r, nbc


def _attn_in_specs(nc, n_cur):
    prev = lambda c: jnp.maximum(jnp.minimum(c, nc - 1) - 1, 0)
    cur = lambda c: jnp.minimum(c, nc - 1)
    blk = lambda f: pl.BlockSpec((CHUNK, SLAB), f)
    specs = [blk(lambda h, c: (cur(c), h)),
             blk(lambda h, c: (prev(c), N_SLABS + h)), blk(lambda h, c: (cur(c), N_SLABS + h)),
             blk(lambda h, c: (prev(c), 2 * N_SLABS + h)), blk(lambda h, c: (cur(c), 2 * N_SLABS + h))]
    return specs + [blk(lambda h, c: (cur(c), h))] * n_cur


def _attn_fwd(proj, payload, *, name):
    S = proj.shape[0]
    nc = S // CHUNK
    n = len(DILATIONS)
    npay = len(payload)
    n_steps = N_SLABS * nc

    def body(*refs):
        q_ref, kp_ref, kc_ref, vp_ref, vc_ref = refs[:5]
        pay_refs = refs[5:5 + npay]
        attn_ref, lse_ref, attn16_ref = refs[5 + npay:8 + npay]
        all_refs = refs[8 + npay:8 + 2 * npay]
        scr = refs[8 + 2 * npay:]
        o_scr, l_scr = scr[:n], scr[n:2 * n]
        start, forward, finish = _gather_phases(pay_refs, all_refs, *scr[2 * n:])
        step = pl.program_id(0) * nc + pl.program_id(1)
        pl.when(step == 0)(start)
        c = pl.program_id(1)
        lms = _lane_masks()
        plain, first = (jnp.tile(_band_mask(f), (2, 1)) for f in (False, c == 0))
        def scores(unit):
            p, d, b, r, nbc = unit
            rows = _unit_rows(d, b)(r)
            prow = _unit_rows(d, (b - 1) % nbc)(r)
            kpr, vpr = (kc_ref, vc_ref) if b > 0 else (kp_ref, vp_ref)
            q = q_ref[rows, :].astype(BF16)
            kcat = jnp.concatenate([kpr[prow, :], kc_ref[rows, :]], axis=0).astype(BF16)
            vcat = jnp.concatenate([vpr[prow, :], vc_ref[rows, :]], axis=0).astype(BF16)
            q2 = jnp.concatenate([jnp.where(lm, q, jnp.zeros_like(q)) for lm in lms], axis=0) * SCALE
            return p, rows, plain if b > 0 else first, vcat, _dot(q2, kcat, NT)

        units = list(_attn_units())
        nxt = scores(units[0])
        for k in range(len(units)):
            p, rows, mask2, vcat, s = nxt
            if k + 1 < len(units):
                nxt = scores(units[k + 1])
            s = jnp.where(mask2, s, NEG_INF)
            m = jnp.max(s, axis=-1, keepdims=True)
            e = jnp.exp(s - m)
            l = jnp.sum(e, axis=-1, keepdims=True)
            o2 = _dot(e.astype(BF16), vcat, NN) / l
            lse2 = m + jnp.log(l)
            o_scr[p][rows, :] = jnp.where(lms[0], o2[:BLOCK], o2[BLOCK:])
            l_scr[p][rows, :] = jnp.where(lms[0], lse2[:BLOCK], lse2[BLOCK:])
        ls = [l_scr[p][...] for p in range(n)]
        top = functools.reduce(jnp.maximum, ls)
        es = [jnp.exp(l - top) for l in ls]
        den = functools.reduce(jnp.add, es)
        num = functools.reduce(jnp.add, [e * o_scr[p][...] for p, e in enumerate(es)])
        attn = num / den
        attn_ref[...] = attn
        attn16_ref[...] = attn.astype(BF16)
        lse_ref[...] = top + jnp.log(den)
        pl.when(step == (2 * n_steps) // 3)(forward)
        pl.when(step == n_steps - 1)(finish)

    out = pl.pallas_call(
        body, name=name, grid=(N_SLABS, nc), in_specs=_attn_in_specs(nc, 0) + [ANY] * npay,
        out_specs=[pl.BlockSpec((CHUNK, SLAB), lambda h, c: (c, h))] * 3 + [ANY] * npay,
        out_shape=[jax.ShapeDtypeStruct((S, ATTN_WIDTH), F32)] * 2 + [jax.ShapeDtypeStruct((S, ATTN_WIDTH), BF16)]
        + _gathered_shapes(payload),
        scratch_shapes=[pltpu.VMEM((CHUNK, SLAB), F32)] * (2 * n) + _gather_sems(npay),
        compiler_params=_cp("arbitrary", "arbitrary"),
    )(proj, proj, proj, proj, proj, *payload)
    return (*out[:3], out[3:])


def _attn_bwd(proj, dcat, attn, lse, grads, blocks, *, name):
    S = proj.shape[0]
    nc = S // CHUNK
    ng, nb = len(grads), len(blocks)
    n = len(DILATIONS)
    n_steps = N_SLABS * (nc + 1)
    recv_shapes, exchange_sems = _exchange_buffers(grads)

    def body(*refs):
        q_ref, kp_ref, kc_ref, vp_ref, vc_ref, do_ref, o_ref, lse_ref = refs[:8]
        g_refs, b_refs = refs[8:8 + ng], refs[8 + ng:8 + ng + nb]
        outs = refs[8 + ng + nb:]
        dq_ref, dk_ref, dv_ref = outs[:3]
        r_refs, all_refs = outs[3:3 + ng], outs[3 + ng:3 + ng + nb]
        scr = outs[3 + ng + nb:]
        dk_prev, dv_prev = scr[:2]
        delta_h, lse_h = scr[2:4], scr[4:6]
        dq_p, dk_own, dk_back, dv_own, dv_back = (scr[6 + n * k:6 + n * (k + 1)] for k in range(5))
        start, finish = _exchange_phases(g_refs, r_refs, *scr[6 + 5 * n:9 + 5 * n])
        gather_start, gather_forward, gather_finish = _gather_phases(b_refs, all_refs, *scr[9 + 5 * n:])
        c = pl.program_id(1)
        step = pl.program_id(0) * (nc + 1) + c

        @pl.when(step == 0)
        def _():
            start()
            gather_start()

        @pl.when(c == 0)
        def _():
            dk_prev[...] = jnp.zeros_like(dk_prev)
            dv_prev[...] = jnp.zeros_like(dv_prev)

        @pl.when(c < nc)
        def _():
            lms = _lane_masks()
            plain, first = (jnp.tile(_band_mask(f), (2, 1)) for f in (False, c == 0))
            prod = do_ref[...] * o_ref[...]
            lse = lse_ref[...]
            lse_other = pltpu.roll(lse, HEAD_DIM, 1)
            for h, lm in enumerate(lms):
                delta = jnp.sum(jnp.where(lm, prod, 0.0), axis=-1, keepdims=True)
                delta_h[h][...] = jnp.broadcast_to(delta, (CHUNK, SLAB))
                lse_h[h][...] = jnp.where(lm, lse, lse_other)
            wide = lambda refs, rows: jnp.tile(jnp.concatenate([r[rows, :] for r in refs], axis=0), (1, 2))
            stack = lambda f: jnp.concatenate([f(lm) for lm in lms], axis=0)

            def scores(unit):
                p, d, b, r, nbc = unit
                rows = _unit_rows(d, b)(r)
                prow = _unit_rows(d, (b - 1) % nbc)(r)
                kpr, vpr = (kc_ref, vc_ref) if b > 0 else (kp_ref, vp_ref)
                q = q_ref[rows, :].astype(BF16)
                kcat = jnp.concatenate([kpr[prow, :], kc_ref[rows, :]], axis=0).astype(BF16)
                vcat = jnp.concatenate([vpr[prow, :], vc_ref[rows, :]], axis=0).astype(BF16)
                do = do_ref[rows, :]
                q2 = stack(lambda lm: jnp.where(lm, q, jnp.zeros_like(q))) * SCALE
                do2 = stack(lambda lm: jnp.where(lm, do, 0.0)).astype(BF16)
                return dict(p=p, rows=rows, prow=prow, mask2=plain if b > 0 else first, kcat=kcat, q2=q2, do2=do2,
                            s=_dot(q2, kcat, NT), dp=_dot(do2, vcat, NT))

            units = list(_attn_units())
            nxt = scores(units[0])
            for k in range(len(units)):
                u = nxt
                if k + 1 < len(units):
                    nxt = scores(units[k + 1])
                p, rows, prow, kcat = u["p"], u["rows"], u["prow"], u["kcat"]
                e = jnp.where(u["mask2"], jnp.exp(u["s"] - wide(lse_h, rows)), 0.0)
                ds = (e * (u["dp"] - wide(delta_h, rows))).astype(BF16)
                dq2 = _dot(ds, kcat, NN) * SCALE
                dq = jnp.where(lms[0], dq2[:BLOCK], dq2[BLOCK:])
                dkc = _dot(ds, u["q2"], TN)
                dvc = _dot(e.astype(BF16), u["do2"], TN)
                dq_p[p][rows, :] = dq
                dk_own[p][rows, :] = dkc[BLOCK:]
                dv_own[p][rows, :] = dvc[BLOCK:]
                dk_back[p][prow, :] = dkc[:BLOCK]
                dv_back[p][prow, :] = dvc[:BLOCK]
            dq_ref[...] = functools.reduce(jnp.add, [r[...] for r in dq_p]).astype(BF16)
            for prev, own, back, out_ref in ((dk_prev, dk_own, dk_back, dk_ref), (dv_prev, dv_own, dv_back, dv_ref)):
                for p, d in enumerate(DILATIONS):
                    tail = CHUNK - BLOCK * d
                    prev[tail:, :] += back[p][tail:, :]
                out_ref[...] = prev[...].astype(BF16)
                prev[...] = functools.reduce(jnp.add, [r[...] for r in own])
                for p, d in enumerate(DILATIONS):
                    tail = CHUNK - BLOCK * d
                    if tail:
                        prev[:tail, :] += back[p][:tail, :]

        @pl.when(c == nc)
        def _():
            dk_ref[...] = dk_prev[...].astype(BF16)
            dv_ref[...] = dv_prev[...].astype(BF16)

        pl.when(step == (2 * n_steps) // 3)(gather_forward)

        @pl.when(step == n_steps - 1)
        def _():
            gather_finish()
            finish()

    blk = lambda f: pl.BlockSpec((CHUNK, SLAB), f)
    late = lambda h, c: (jnp.maximum(c - 1, 0), h)
    out = pl.pallas_call(
        body, name=name, grid=(N_SLABS, nc + 1), in_specs=_attn_in_specs(nc, 3) + [ANY] * (ng + nb),
        out_specs=[blk(lambda h, c: (jnp.minimum(c, nc - 1), h)), blk(late), blk(late)] + [ANY] * (ng + nb),
        out_shape=[jax.ShapeDtypeStruct((S, ATTN_WIDTH), BF16)] * 3 + recv_shapes + _gathered_shapes(blocks),
        scratch_shapes=[pltpu.VMEM((CHUNK, SLAB), F32)] * (6 + 5 * n) + exchange_sems + _gather_sems(nb),
        compiler_params=_cp("arbitrary", "arbitrary"),
    )(proj, proj, proj, proj, proj, dcat, attn, lse, *grads, *blocks)
    return out[:3], out[3:3 + ng], out[3 + ng:]


def _split_bf16(a):
    hi = a.astype(BF16)
    lo = (a - hi.astype(F32)).astype(BF16)
    return hi, lo


def _pooled(ug, halo_g, w, row0, tm):
    ext = jnp.concatenate([halo_g, ug], axis=0)
    hi, lo = _split_bf16(ext)
    rr = lax.broadcasted_iota(jnp.int32, (tm, tm + HALO), 0)
    cc = lax.broadcasted_iota(jnp.int32, (tm, tm + HALO), 1)
    back = rr + HALO - cc
    win = ((back >= 0) & (back < w)).astype(BF16)
    wsum = _dot(win, hi, NN) + _dot(win, lo, NN)
    rows = row0 + lax.broadcasted_iota(jnp.int32, (tm, 1), 0)
    inv = 1.0 / jnp.minimum(rows + 1, w).astype(F32)
    return wsum * inv - ug


def _pool_fwd(u, u_col, pool_w, pool_scale, *, name, tm=256):
    S, W = u.shape[0], POOL_WIDTH
    G = POOL_GROUP_DIM

    def body(u_ref, h_ref, w_ref, s_ref, o_ref):
        i = pl.program_id(0)
        uv = u_ref[...]
        halo = jnp.where(i > 0, h_ref[...], 0.0)
        sls = [slice(g * G, (g + 1) * G) for g in range(len(POOL_WINDOWS))]
        pooled = [_pooled(uv[:, sl], halo[:, sl], w, i * tm, tm) for sl, w in zip(sls, POOL_WINDOWS)]
        zs = [_dot(p.astype(BF16), w_ref[g].astype(BF16), NN) for g, p in enumerate(pooled)]
        for sl, z in zip(sls, zs):
            o_ref[:, sl] = (z * s_ref[:, sl]).astype(BF16)

    per = tm // HALO
    return pl.pallas_call(
        body, name=name, grid=(S // tm,),
        in_specs=[pl.BlockSpec((tm, W), lambda i: (i, u_col)),
                  pl.BlockSpec((HALO, W), lambda i: (jnp.maximum(i * per - 1, 0), u_col)),
                  pl.BlockSpec((len(POOL_WINDOWS), G, G), lambda i: (0, 0, 0)),
                  pl.BlockSpec((1, W), lambda i: (0, 0))],
        out_specs=pl.BlockSpec((tm, W), lambda i: (i, 0)),
        out_shape=jax.ShapeDtypeStruct((S, W), BF16),
        compiler_params=_cp("parallel"),
    )(u, u, pool_w, pool_scale)


def _pool_bwd(u, u_col, dy, dy_col, pool_w, pool_scale, *, name, tm=256):
    S, W = u.shape[0], POOL_WIDTH
    G = POOL_GROUP_DIM
    nt = S // tm

    def body(u_ref, h_ref, dy_ref, dyn_ref, w_ref, s_ref, du_ref, gw_ref, gs_ref):
        i = pl.program_id(0)

        @pl.when(i == 0)
        def _():
            gw_ref[...] = jnp.zeros_like(gw_ref)
            gs_ref[...] = jnp.zeros_like(gs_ref)

        uv = u_ref[...]
        halo = jnp.where(i > 0, h_ref[...], 0.0)
        dyv = dy_ref[...]
        dyn = jnp.where(i < nt - 1, dyn_ref[...], 0.0)
        rr = lax.broadcasted_iota(jnp.int32, (tm, tm + HALO), 0)
        cc = lax.broadcasted_iota(jnp.int32, (tm, tm + HALO), 1)
        rows_ext = i * tm + lax.broadcasted_iota(jnp.int32, (tm + HALO, 1), 0)
        groups = list(enumerate(POOL_WINDOWS))
        sls = [slice(g * G, (g + 1) * G) for g, _ in groups]
        wgs = [w_ref[g].astype(BF16) for g, _ in groups]
        pooled = [_pooled(uv[:, sl], halo[:, sl], w, i * tm, tm).astype(BF16) for sl, (_, w) in zip(sls, groups)]
        dzs = [dyv[:, sl] * s_ref[:, sl] for sl in sls]
        dz_ext = [jnp.concatenate([dz, dyn[:, sl] * s_ref[:, sl]], axis=0).astype(BF16) for dz, sl in zip(dzs, sls)]
        dp_ext = [_dot(d, wg, NT) for d, wg in zip(dz_ext, wgs)]
        zs = [_dot(p, wg, NN) for p, wg in zip(pooled, wgs)]
        for (g, w), sl, p, dz, z, dp in zip(groups, sls, pooled, dzs, zs, dp_ext):
            gw_ref[g] += _dot(p, dz.astype(BF16), TN)
            gs_ref[:, sl] += jnp.sum(dyv[:, sl] * z, axis=0, keepdims=True)
            inv_ext = 1.0 / jnp.minimum(rows_ext + 1, w).astype(F32)
            hi, lo = _split_bf16(dp * inv_ext)
            ahead = cc - rr
            win = ((ahead >= 0) & (ahead < w)).astype(BF16)
            du_ref[:, sl] = (_dot(win, hi, NN) + _dot(win, lo, NN) - dp[:tm]).astype(BF16)

    per = tm // HALO
    nh = S // HALO
    return pl.pallas_call(
        body, name=name, grid=(nt,),
        in_specs=[pl.BlockSpec((tm, W), lambda i: (i, u_col)),
                  pl.BlockSpec((HALO, W), lambda i: (jnp.maximum(i * per - 1, 0), u_col)),
                  pl.BlockSpec((tm, W), lambda i: (i, dy_col)),
                  pl.BlockSpec((HALO, W), lambda i: (jnp.minimum((i + 1) * per, nh - 1), dy_col)),
                  pl.BlockSpec((len(POOL_WINDOWS), G, G), lambda i: (0, 0, 0)),
                  pl.BlockSpec((1, W), lambda i: (0, 0))],
        out_specs=[pl.BlockSpec((tm, W), lambda i: (i, 0)),
                   pl.BlockSpec((len(POOL_WINDOWS), G, G), lambda i: (0, 0, 0)),
                   pl.BlockSpec((1, W), lambda i: (0, 0))],
        out_shape=[jax.ShapeDtypeStruct((S, W), BF16),
                   jax.ShapeDtypeStruct((len(POOL_WINDOWS), G, G), F32),
                   jax.ShapeDtypeStruct((1, W), F32)],
        compiler_params=_cp("arbitrary"),
    )(u, u, dy, dy, pool_w, pool_scale)


GELU_K0 = math.sqrt(2.0 / math.pi)
GELU_K1 = 0.044715


def _gelu_parts(x):
    x2 = x * x
    t = jnp.tanh(x * (GELU_K0 + (GELU_K0 * GELU_K1) * x2))
    hp = 0.5 + 0.5 * t
    gelu = x * hp
    dgelu = hp + (x * (hp * (1.0 - t))) * (GELU_K0 + (3.0 * GELU_K0 * GELU_K1) * x2)
    return gelu, dgelu


def _shifted(ext, halo):
    return (pltpu.roll(ext, 2, 0)[halo:], pltpu.roll(ext, 1, 0)[halo:], ext[halo:])


def _conv(sh, w, b):
    return b + (sh[0] * w[0:1] + sh[1] * w[1:2] + sh[2] * w[2:3])


F32_ROWS = 8


def _ffn_up_glu(h, w_up_t, conv_w, conv_b, *, name, tm=2048, tn=256, sub=256):
    S, K = h.shape
    F = D_FF
    nj = F // tn

    def body(h_ref, wg_ref, wv_ref, cwg_ref, cwv_ref, cbg_ref, cbv_ref,
             ug_ref, uv_ref, cg_ref, cv_ref, y_ref, carry):
        i = pl.program_id(0)
        j = pl.program_id(1)

        w_cat = jnp.concatenate([wg_ref[...], wv_ref[...]], axis=0)
        conv_w_b = ((cwg_ref[...], cbg_ref[...]), (cwv_ref[...], cbv_ref[...]))
        halo = [jnp.where(i > 0, carry[j, s], 0.0) for s in range(2)]
        u_next = _dot(h_ref[0:sub, :], w_cat, NT)
        for a in range(0, tm, sub):
            u16 = u_next.astype(BF16)
            if a + sub < tm:
                u_next = _dot(h_ref[a + sub:a + 2 * sub, :], w_cat, NT)
            ug_ref[a:a + sub, :] = u16[:, :tn]
            uv_ref[a:a + sub, :] = u16[:, tn:]
            c = []
            for s, (cw, cb) in enumerate(conv_w_b):
                u = u16[:, s * tn:(s + 1) * tn].astype(F32)
                ext = jnp.concatenate([halo[s], u], axis=0)
                c.append(_conv(_shifted(ext, F32_ROWS), cw, cb))
                halo[s] = u[sub - F32_ROWS:]
            cg_ref[a:a + sub, :] = c[0].astype(BF16)
            cv_ref[a:a + sub, :] = c[1].astype(BF16)
            gelu, _ = _gelu_parts(c[0])
            y_ref[a:a + sub, :] = (gelu * c[1]).astype(BF16)
        for s in range(2):
            carry[j, s] = halo[s]

    tile = pl.BlockSpec((tm, tn), lambda i, j: (i, j))
    vec = lambda rows, off: pl.BlockSpec((rows, tn), lambda i, j: (0, j + off))
    return pl.pallas_call(
        body, name=name, grid=(S // tm, nj),
        in_specs=[pl.BlockSpec((tm, K), lambda i, j: (i, 0)),
                  pl.BlockSpec((tn, K), lambda i, j: (j, 0)), pl.BlockSpec((tn, K), lambda i, j: (j + nj, 0)),
                  vec(3, 0), vec(3, nj), vec(1, 0), vec(1, nj)],
        out_specs=[tile] * 5,
        out_shape=[jax.ShapeDtypeStruct((S, F), BF16)] * 5,
        scratch_shapes=[pltpu.VMEM((nj, 2, F32_ROWS, tn), F32)],
        compiler_params=_cp("arbitrary", "arbitrary"),
    )(h, w_up_t, w_up_t, conv_w, conv_w, conv_b, conv_b)


def _ffn_glu_bwd(u_g, u_v, c_g, c_v, df, w_down, h, conv_w, *, name, tm=2048, tn=256, sub=256):
    S = u_g.shape[0]
    F = D_FF
    D = df.shape[1]
    nj = F // tn
    nt = S // tm

    def body(ug_ref, uv_ref, cg_ref, cgn_ref, cv_ref, cvn_ref, df_ref, dfn_ref, wd_ref, h_ref, wg_ref, wv_ref,
             dug_ref, duv_ref, gug_ref, guv_ref, gd_ref, gwg_ref, gwv_ref, gbg_ref, gbv_ref,
             acc_u, acc_d):
        i = pl.program_id(1)

        @pl.when(i == 0)
        def _():
            for r in (gwg_ref, gwv_ref, gbg_ref, gbv_ref, acc_u, acc_d):
                r[...] = jnp.zeros_like(r)

        wg, wv = wg_ref[...], wv_ref[...]
        wd = wd_ref[...]
        dfn = jnp.where(i < nt - 1, dfn_ref[...], jnp.zeros_like(dfn_ref))
        n_ext = sub + HALO

        def ahead(dc):
            return dc[:sub], pltpu.roll(dc, n_ext - 1, 0)[:sub], pltpu.roll(dc, n_ext - 2, 0)[:sub]

        def ext(ref, nxt, a):
            b = a + sub
            return jnp.concatenate([ref[a:b, :], ref[b:b + HALO, :] if b < tm else nxt], axis=0)

        dy_next = _dot(ext(df_ref, dfn, 0), wd, NT)
        for a in range(0, tm, sub):
            b = a + sub
            dy_ext = dy_next
            if b < tm:
                dy_next = _dot(ext(df_ref, dfn, b), wd, NT)
            cg = ext(cg_ref, cgn_ref[...], a).astype(F32)
            cv = ext(cv_ref, cvn_ref[...], a).astype(F32)
            df_sub = df_ref[a:b, :]
            gelu, dgelu = _gelu_parts(cg)
            dcs_g = ahead(dy_ext * cv * dgelu)
            dcs_v = ahead(dy_ext * gelu)
            du_g = (dcs_g[0] * wg[2:3] + dcs_g[1] * wg[1:2] + dcs_g[2] * wg[0:1]).astype(BF16)
            du_v = (dcs_v[0] * wv[2:3] + dcs_v[1] * wv[1:2] + dcs_v[2] * wv[0:1]).astype(BF16)
            dug_ref[a:b, :] = du_g
            duv_ref[a:b, :] = du_v
            acc_u[...] += _dot(jnp.concatenate([du_g, du_v], axis=1), h_ref[a:b, :], TN)
            acc_d[...] += _dot((gelu[:sub] * cv[:sub]).astype(BF16), df_sub, TN)
            for dcs, u_ref, gw_ref, gb_ref in ((dcs_g, ug_ref, gwg_ref, gbg_ref), (dcs_v, uv_ref, gwv_ref, gbv_ref)):
                u = u_ref[a:b, :].astype(F32)
                gb_ref[...] += jnp.sum(dcs[0], axis=0, keepdims=True)
                for k in range(3):
                    gw_ref[k:k + 1, :] += jnp.sum(dcs[2 - k] * u, axis=0, keepdims=True)

        @pl.when(i == nt - 1)
        def _():
            gug_ref[...] = acc_u[:tn, :].astype(BF16)
            guv_ref[...] = acc_u[tn:, :].astype(BF16)
            gd_ref[...] = acc_d[...].astype(BF16)

    per = tm // HALO
    nh = S // HALO
    hnext = lambda i: jnp.minimum((i + 1) * per, nh - 1)
    tile = pl.BlockSpec((tm, tn), lambda j, i: (i, j))
    hn = pl.BlockSpec((HALO, tn), lambda j, i: (hnext(i), j))
    vec = lambda rows, off: pl.BlockSpec((rows, tn), lambda j, i: (0, j + off))
    wide = pl.BlockSpec((tm, D), lambda j, i: (i, 0))
    wrow = pl.BlockSpec((tn, D), lambda j, i: (j, 0))
    return pl.pallas_call(
        body, name=name, grid=(nj, nt),
        in_specs=[tile, tile, tile, hn, tile, hn, wide, pl.BlockSpec((HALO, D), lambda j, i: (hnext(i), 0)),
                  wrow, wide, vec(3, 0), vec(3, nj)],
        out_specs=[tile, tile, wrow, wrow, wrow, vec(3, 0), vec(3, 0), vec(1, 0), vec(1, 0)],
        out_shape=[jax.ShapeDtypeStruct((S, F), BF16), jax.ShapeDtypeStruct((S, F), BF16),
                   jax.ShapeDtypeStruct((F, D), BF16), jax.ShapeDtypeStruct((F, D), BF16),
                   jax.ShapeDtypeStruct((F, D), BF16),
                   jax.ShapeDtypeStruct((3, F), F32), jax.ShapeDtypeStruct((3, F), F32),
                   jax.ShapeDtypeStruct((1, F), F32), jax.ShapeDtypeStruct((1, F), F32)],
        scratch_shapes=[pltpu.VMEM((2 * tn, D), F32), pltpu.VMEM((tn, D), F32)],
        compiler_params=_cp("parallel", "arbitrary"),
    )(u_g, u_v, c_g, c_g, c_v, c_v, df, df, w_down, h, conv_w, conv_w)


def _sum_partials(parts, *, name, tr):
    _, R, C = parts.shape

    def body(p_ref, o_ref):
        tot = p_ref[0].astype(F32)
        for j in range(1, N_DEV):
            tot = tot + p_ref[j].astype(F32)
        o_ref[...] = tot

    return pl.pallas_call(
        body, name=name, grid=(R // tr,),
        in_specs=[pl.BlockSpec((N_DEV, tr, C), lambda i: (0, i, 0))],
        out_specs=pl.BlockSpec((tr, C), lambda i: (i, 0)),
        out_shape=jax.ShapeDtypeStruct((R, C), F32),
        compiler_params=_cp("parallel"),
    )(parts)


def _adamw(w, g, m, v, *, name, tr):
    R, C = w.shape
    c1 = 1.0 - ADAM_B1 ** ADAM_STEP
    c2 = 1.0 - ADAM_B2 ** ADAM_STEP

    def body(w_ref, g_ref, m_ref, v_ref, d_ref, nm_ref, nv_ref):
        g = g_ref[...]
        nm = ADAM_B1 * m_ref[...] + (1.0 - ADAM_B1) * g
        nv = ADAM_B2 * v_ref[...] + (1.0 - ADAM_B2) * (g * g)
        d_ref[...] = -ADAM_LR * ((nm / c1) / (jnp.sqrt(nv / c2) + ADAM_EPS) + ADAM_WD * w_ref[...])
        nm_ref[...] = nm
        nv_ref[...] = nv

    spec = pl.BlockSpec((tr, C), lambda i: (i, 0))
    return pl.pallas_call(
        body, name=name, grid=(R // tr,), in_specs=[spec] * 4, out_specs=[spec] * 3,
        out_shape=[jax.ShapeDtypeStruct((R, C), F32)] * 3,
        compiler_params=_cp("parallel"),
    )(w, g, m, v)


def _mesh_pos():
    return lax.axis_index("x"), lax.axis_index("y"), lax.axis_index("c")


def _gather_phases(x_refs, out_refs, send_sems, recv_sems, local_sems):
    x, y, c = _mesh_pos()
    me, sibling = (x, y, c), (x, y, 1 - c)
    chips = [(1 - x, y), (x, 1 - y), (1 - x, 1 - y)]
    arrays = range(len(x_refs))

    def slot(a, px, py, pc):
        return out_refs[a].at[4 * px + 2 * py + pc]

    def copy(a, k, block, to, own=False):
        return pltpu.make_async_remote_copy(
            src_ref=x_refs[a] if own else slot(a, *block), dst_ref=slot(a, *block),
            send_sem=send_sems.at[a, k], recv_sem=recv_sems.at[a, k], device_id=to, device_id_type=MESH)

    mine = [pltpu.make_async_copy(x_refs[a], slot(a, *me), local_sems.at[a]) for a in arrays]
    first = [copy(a, 0, me, sibling, own=True) for a in arrays]
    first += [copy(a, 1 + j, me, (*chip, c), own=True) for j, chip in enumerate(chips) for a in arrays]
    passed = [[copy(a, 4 + j, (*chip, c), sibling) for a in arrays] for j, chip in enumerate(chips)]

    def start():
        for cp in mine + first:
            cp.start()

    def forward():
        for j, chip in enumerate(chips):
            for a in arrays:
                copy(a, 1 + j, (*chip, c), me).wait_recv()
                passed[j][a].start()

    def finish():
        for a in arrays:
            copy(a, 0, sibling, me).wait_recv()
            for j, chip in enumerate(chips):
                copy(a, 4 + j, (*chip, 1 - c), me).wait_recv()
        for cp in first + [cp for row in passed for cp in row]:
            cp.wait_send()
        for cp in mine:
            cp.wait()

    return start, forward, finish


def _gather_sems(n):
    return [pltpu.SemaphoreType.DMA((n, 7)), pltpu.SemaphoreType.DMA((n, 7)), pltpu.SemaphoreType.DMA((n,))]


def _gathered_shapes(blocks):
    return [jax.ShapeDtypeStruct((N_DEV,) + b.shape, b.dtype) for b in blocks]


def _all_reduce_small(block, gathered, *, name):
    r0, r1 = block.shape[0], gathered.shape[1]

    def body(x_ref, more_ref, all_ref, sum_ref, *sems):
        for phase in _gather_phases([x_ref], [all_ref], *sems):
            phase()
        for ref, rows in ((all_ref, slice(0, r0)), (more_ref, slice(r0, r0 + r1))):
            tot = ref[0]
            for j in range(1, N_DEV):
                tot = tot + ref[j]
            sum_ref[rows, :] = tot

    return pl.pallas_call(
        body, name=name, in_specs=[VMEM, VMEM], out_specs=[VMEM, VMEM],
        out_shape=[jax.ShapeDtypeStruct((N_DEV,) + block.shape, block.dtype),
                   jax.ShapeDtypeStruct((r0 + r1, block.shape[1]), block.dtype)],
        scratch_shapes=_gather_sems(1),
        compiler_params=pltpu.CompilerParams(vmem_limit_bytes=V7X_VMEM_LIMIT),
    )(block, gathered)[1]


def _exchange_phases(g_refs, r_refs, send_sems, recv_sems, local_sems):
    x, y, c = _mesh_pos()
    me = 4 * x + 2 * y + c
    owns, remote = [], []
    for k, (g_ref, r_ref) in enumerate(zip(g_refs, r_refs)):
        rows = g_ref.shape[0] // N_DEV
        owns.append(pltpu.make_async_copy(g_ref.at[pl.ds(me * rows, rows)], r_ref.at[me], local_sems.at[k]))
        for p in range(1, N_DEV):
            px, py, pc = x ^ (p >> 2), y ^ ((p >> 1) & 1), c ^ (p & 1)
            peer = 4 * px + 2 * py + pc
            link = dict(send_sem=send_sems.at[k, p], recv_sem=recv_sems.at[k, p],
                        device_id=(px, py, pc), device_id_type=MESH)
            src = g_ref.at[pl.ds(peer * rows, rows)]
            send = pltpu.make_async_remote_copy(src_ref=src, dst_ref=r_ref.at[me], **link)
            arrival = pltpu.make_async_remote_copy(src_ref=src, dst_ref=r_ref.at[peer], **link)
            remote.append((send, arrival))

    def start():
        for own in owns:
            own.start()
        for send, _ in remote:
            send.start()

    def finish():
        for _, arrival in remote:
            arrival.wait_recv()
        for send, _ in remote:
            send.wait_send()
        for own in owns:
            own.wait()

    return start, finish


def _exchange_buffers(grads):
    n = len(grads)
    shapes = [jax.ShapeDtypeStruct((N_DEV, g.shape[0] // N_DEV, g.shape[1]), g.dtype) for g in grads]
    sems = [pltpu.SemaphoreType.DMA((n, N_DEV)), pltpu.SemaphoreType.DMA((n, N_DEV)),
            pltpu.SemaphoreType.DMA((n,))]
    return shapes, sems


def _unpack_gathered(gathered):
    w_out, w_up_t, w_down = (g.reshape(-1, D_MODEL) for g in gathered[:3])
    width = 2 * D_FF // N_DEV
    conv_w = jnp.transpose(gathered[3][:, :3, :width], (1, 0, 2)).reshape(3, 2 * D_FF)
    return w_out, w_up_t, w_down, conv_w


def _rest_payload(w_out, w_up, w_down, conv_w):
    rows, cols = conv_w.shape
    conv_w = jnp.pad(conv_w, ((0, (-rows) % F32_ROWS), (0, (-cols) % LANES)))
    return [w_out.astype(BF16), w_up.T.astype(BF16), w_down.astype(BF16), conv_w]


def _device_step(x, target, g_mix_pre, w_in_t_block, rest_payload, pool_w, pool_scale, g_mix_post, g_ffn_pre,
                 conv_b, g_ffn_post):
    h1, w_in_t = _rms_norm_gather(x, g_mix_pre, w_in_t_block, name="rms_mix_pre")
    w_in_t = w_in_t.reshape(-1, D_MODEL)
    proj = _matmul(h1, w_in_t, trans_b=True, out_dtype=F32, tm=512, tn=4 * ATTN_WIDTH, name="proj")
    attn, lse, attn16, gathered = _attn_fwd(proj, rest_payload, name="attn_fwd")
    w_out, w_up_t, w_down, conv_w = _unpack_gathered(gathered)
    pool = _pool_fwd(proj, 3, pool_w, pool_scale, name="pool_fwd")
    mixed, x2, h2 = _mix_out(attn16, pool, w_out, x, g_mix_post, g_ffn_pre, name="mix_out")
    u_g, u_v, c_g, c_v, y = _ffn_up_glu(h2, w_up_t, conv_w, conv_b, name="ffn_up_glu")
    df, d_out, loss_blk, gg_ffn_post = _ffn_out(y, w_down, x2, target, g_ffn_post, name="ffn_out")
    du_g, du_v, gw_up_g, gw_up_v, gw_down, gcw_g, gcw_v, gcb_g, gcb_v = _ffn_glu_bwd(
        u_g, u_v, c_g, c_v, df, w_down, h2, conv_w, name="ffn_glu_bwd")
    gw_up_t = jnp.concatenate([gw_up_g, gw_up_v], axis=0)
    dx2, gg_ffn_pre, dmixed, gg_mix_post = _dgrad_norm(
        [du_g, du_v], w_up_t, d_out, x2, g_ffn_pre, (mixed, g_mix_post), [], name="ffn_up_dgrad")
    gw_out = _matmul_tn([attn16, pool], dmixed, name="grad_w_out")
    dcat = _matmul(dmixed, w_out, trans_b=True, out_dtype=F32, tm=512, tn=1024, name="mix_out_dgrad")
    d_pool_in, g_pool_w, g_pool_scale = _pool_bwd(proj, 3, dcat, 1, pool_w, pool_scale, name="pool_bwd")
    early = dict(g_mix_post=gg_mix_post, g_ffn_pre=gg_ffn_pre, g_ffn_post=gg_ffn_post, pool_scale=g_pool_scale,
                 conv_b=jnp.concatenate([gcb_g, gcb_v], axis=1), pool_w=g_pool_w)
    early_block = _pack_rows([early[k] for k in _SMALL[1:]] + [jnp.concatenate([gcw_g, gcw_v], axis=1), loss_blk])
    dqkv, (r_out, r_up_t, r_down), (small_gathered,) = _attn_bwd(
        proj, dcat, attn, lse, [gw_out, gw_up_t, gw_down], [early_block], name="attn_bwd")
    dproj = list(dqkv) + [d_pool_in]
    gw_in_t = _matmul_tn(dproj, h1, name="grad_w_in")
    grad_x, gg_mix_pre, (r_in_t,) = _dgrad_norm(dproj, w_in_t, dx2, x, g_mix_pre, None, [gw_in_t], name="proj_dgrad")
    received = (r_in_t, r_out, r_up_t, r_down)
    return grad_x, received, gg_mix_pre, small_gathered


_SMALL = ("g_mix_pre", "g_mix_post", "g_ffn_pre", "g_ffn_post", "pool_scale", "conv_b", "pool_w")
LANES = 128


def _pack_rows(arrays):
    parts = []
    for a in arrays:
        a2 = a.reshape(-1, LANES)
        parts.append(jnp.pad(a2, ((0, (-a2.shape[0]) % 8), (0, 0))))
    return jnp.concatenate(parts, axis=0)


def _unpack_rows(packed, shapes):
    out, row = [], 0
    for shape in shapes:
        rows = math.prod(shape) // LANES
        out.append(packed[row:row + rows].reshape(shape))
        row += -(-rows // 8) * 8
    return out


def kernel(x, g_mix_pre, w_in, pool_w, pool_scale, w_out, g_mix_post, g_ffn_pre, w_up, conv_w, conv_b, w_down, g_ffn_post, loss_target, m_g_mix_pre, m_w_in, m_pool_w, m_pool_scale, m_w_out, m_g_mix_post, m_g_ffn_pre, m_w_up, m_conv_w, m_conv_b, m_w_down, m_g_ffn_post, v_g_mix_pre, v_w_in, v_pool_w, v_pool_scale, v_w_out, v_g_mix_post, v_g_ffn_pre, v_w_up, v_conv_w, v_conv_b, v_w_down, v_g_ffn_post):
    me = 4 * lax.axis_index("x") + 2 * lax.axis_index("y") + lax.axis_index("c")
    grad_x, recv, gg_mix_pre, small_gathered = _device_step(
        x[0], loss_target[0], g_mix_pre, w_in[0].T.astype(BF16),
        _rest_payload(w_out[0], w_up[0], w_down[0], conv_w[0]),
        pool_w[0], pool_scale, g_mix_post, g_ffn_pre, conv_b, g_ffn_post)

    g_in_t, g_out, g_up_t, g_down = (
        _sum_partials(r, name=f"sum_partials_{k}", tr=r.shape[1] // 2) for k, r in enumerate(recv))
    grads = {"w_in": g_in_t.T, "w_out": g_out, "w_up": g_up_t.T, "w_down": g_down}

    given = dict(g_mix_pre=g_mix_pre, g_mix_post=g_mix_post, g_ffn_pre=g_ffn_pre, g_ffn_post=g_ffn_post,
                 pool_scale=pool_scale, conv_b=conv_b, pool_w=pool_w)
    small_shapes = [given[k].shape for k in _SMALL]
    total = _all_reduce_small(_pack_rows([gg_mix_pre]), small_gathered, name="all_reduce_small")
    *small_grads, g_conv_w_all, loss_all = _unpack_rows(total, small_shapes + [(3, 2 * D_FF), (8, LANES)])
    loss = loss_all[0, 0]
    grads.update(zip(_SMALL, small_grads))
    width = 2 * D_FF // N_DEV
    grads["conv_w"] = lax.dynamic_slice_in_dim(g_conv_w_all, me * width, width, axis=1)[None]

    weights = dict(g_mix_pre=g_mix_pre, w_in=w_in, pool_w=pool_w, pool_scale=pool_scale, w_out=w_out,
                   g_mix_post=g_mix_post, g_ffn_pre=g_ffn_pre, w_up=w_up, conv_w=conv_w, conv_b=conv_b,
                   w_down=w_down, g_ffn_post=g_ffn_post)
    m_in = dict(g_mix_pre=m_g_mix_pre, w_in=m_w_in, pool_w=m_pool_w, pool_scale=m_pool_scale, w_out=m_w_out,
                g_mix_post=m_g_mix_post, g_ffn_pre=m_g_ffn_pre, w_up=m_w_up, conv_w=m_conv_w, conv_b=m_conv_b,
                w_down=m_w_down, g_ffn_post=m_g_ffn_post)
    v_in = dict(g_mix_pre=v_g_mix_pre, w_in=v_w_in, pool_w=v_pool_w, pool_scale=v_pool_scale, w_out=v_w_out,
                g_mix_post=v_g_mix_post, g_ffn_pre=v_g_ffn_pre, w_up=v_w_up, conv_w=v_conv_w, conv_b=v_conv_b,
                w_down=v_w_down, g_ffn_post=v_g_ffn_post)
    delta, new_m, new_v = {}, {}, {}
    for k in ("w_in", "w_out", "w_up", "w_down"):
        g = grads[k]
        d, nm, nv = _adamw(weights[k][0], g, m_in[k][0], v_in[k][0], name=f"adamw_{k}", tr=g.shape[0] // 2)
        grads[k], delta[k], new_m[k], new_v[k] = g[None], d[None], nm[None], nv[None]
    d, nm, nv = _adamw(weights["conv_w"][0], grads["conv_w"][0], m_in["conv_w"][0], v_in["conv_w"][0],
                       name="adamw_conv_w", tr=3)
    delta["conv_w"], new_m["conv_w"], new_v["conv_w"] = d[None], nm[None], nv[None]
    packed_w = _pack_rows([weights[k] for k in _SMALL])
    small_rows = packed_w.shape[0]
    d, nm, nv = _adamw(packed_w, total[:small_rows], _pack_rows([m_in[k] for k in _SMALL]),
                       _pack_rows([v_in[k] for k in _SMALL]), name="adamw_small", tr=small_rows)
    for k, dk, mk, vk in zip(_SMALL, _unpack_rows(d, small_shapes), _unpack_rows(nm, small_shapes),
                             _unpack_rows(nv, small_shapes)):
        delta[k], new_m[k], new_v[k] = dk, mk, vk

    order = ("g_mix_pre", "w_in", "pool_w", "pool_scale", "w_out", "g_mix_post", "g_ffn_pre", "w_up",
             "conv_w", "conv_b", "w_down", "g_ffn_post")
    return (loss, grad_x[None], *[grads[k] for k in order], *[delta[k] for k in order],
            *[new_m[k] for k in order], *[new_v[k] for k in order])
```

```python
import functools
import math

import jax
import jax.numpy as jnp
from jax import lax
from jax.experimental import pallas as pl
from jax.experimental.pallas import tpu as pltpu

F32 = jnp.float32
BF16 = jnp.bfloat16

D_MODEL = 1024
N_HEADS = 8
HEAD_DIM = 64
ATTN_WIDTH = N_HEADS * HEAD_DIM
DILATIONS = (1, 4, 16)
BLOCK = 128
POOL_WIDTH = 512
POOL_WINDOWS = (2, 4, 8, 16)
POOL_GROUP_DIM = 128
D_FF = 2816
EPS = 1e-6
NEG_INF = -1e30
SCALE = HEAD_DIM ** -0.5

ADAM_LR = 0.001
ADAM_B1 = 0.9
ADAM_B2 = 0.999
ADAM_EPS = 1e-08
ADAM_WD = 0.01
ADAM_STEP = 10

N_DEV = 8
HALO = 16
V7X_VMEM_LIMIT = 56 * 1024 * 1024

MESH = pl.DeviceIdType.MESH
ANY = pl.BlockSpec(memory_space=pl.ANY)
VMEM = pl.BlockSpec(memory_space=pltpu.VMEM)

NT = (((1,), (1,)), ((), ()))
NN = (((1,), (0,)), ((), ()))
TN = (((0,), (0,)), ((), ()))


def _cp(*sem):
    return pltpu.CompilerParams(dimension_semantics=sem, vmem_limit_bytes=V7X_VMEM_LIMIT)


def _dot(a, b, dn):
    return lax.dot_general(a, b, dn, preferred_element_type=F32)


def _rms_bwd(xin, g, dy):
    r = lax.rsqrt(jnp.mean(xin * xin, axis=-1, keepdims=True) + EPS)
    xh = xin * r
    gdy = g * dy
    dx = r * (gdy - xh * jnp.mean(gdy * xh, axis=-1, keepdims=True))
    dg = jnp.sum(dy * xh, axis=0, keepdims=True)
    return dx, dg


def _rms_norm_gather(x, g, block, *, name, tm=512):
    S, D = x.shape
    nt = S // tm

    def body(x_ref, g_ref, blk_ref, o_ref, all_ref, *sems):
        i = pl.program_id(0)
        start, forward, finish = _gather_phases([blk_ref], [all_ref], *sems)
        pl.when(i == 0)(start)
        xv = x_ref[...]
        r = lax.rsqrt(jnp.mean(xv * xv, axis=-1, keepdims=True) + EPS)
        o_ref[...] = (xv * r * g_ref[...]).astype(BF16)
        pl.when(i == nt - 1)(forward)
        pl.when(i == nt - 1)(finish)

    return pl.pallas_call(
        body, name=name, grid=(nt,),
        in_specs=[pl.BlockSpec((tm, D), lambda i: (i, 0)), pl.BlockSpec((1, D), lambda i: (0, 0)), ANY],
        out_specs=[pl.BlockSpec((tm, D), lambda i: (i, 0)), ANY],
        out_shape=[jax.ShapeDtypeStruct((S, D), BF16)] + _gathered_shapes([block]),
        scratch_shapes=_gather_sems(1),
        compiler_params=_cp("arbitrary"),
    )(x, g, block)


def _matmul(a, b, *, trans_b, out_dtype, tm, tn, name):
    M, K = a.shape
    N = b.shape[0] if trans_b else b.shape[1]
    dn = NT if trans_b else NN

    def body(a_ref, b_ref, o_ref):
        o_ref[...] = _dot(a_ref[...], b_ref[...], dn).astype(out_dtype)

    b_spec = (pl.BlockSpec((tn, K), lambda i, j: (j, 0)) if trans_b
              else pl.BlockSpec((K, tn), lambda i, j: (0, j)))
    return pl.pallas_call(
        body, name=name, grid=(M // tm, N // tn),
        in_specs=[pl.BlockSpec((tm, K), lambda i, j: (i, 0)), b_spec],
        out_specs=pl.BlockSpec((tm, tn), lambda i, j: (i, j)),
        out_shape=jax.ShapeDtypeStruct((M, N), out_dtype),
        compiler_params=_cp("parallel", "parallel"),
    )(a, b)


def _matmul_tn(a_list, b, *, name, ts=1024):
    S, Ka = a_list[0].shape
    na = len(a_list)
    Nb = b.shape[1]
    ns = S // ts

    def body(*refs):
        a_refs, b_ref, o_ref, acc = refs[:na], refs[na], refs[na + 1], refs[na + 2]
        s = pl.program_id(0)

        @pl.when(s == 0)
        def _():
            acc[...] = jnp.zeros_like(acc)

        acc[...] += _dot(jnp.concatenate([r[...] for r in a_refs], axis=1), b_ref[...], TN)

        @pl.when(s == ns - 1)
        def _():
            o_ref[...] = acc[...].astype(BF16)

    return pl.pallas_call(
        body, name=name, grid=(ns,),
        in_specs=[pl.BlockSpec((ts, Ka), lambda s: (s, 0))] * na + [pl.BlockSpec((ts, Nb), lambda s: (s, 0))],
        out_specs=pl.BlockSpec((na * Ka, Nb), lambda s: (0, 0)),
        out_shape=jax.ShapeDtypeStruct((na * Ka, Nb), BF16),
        scratch_shapes=[pltpu.VMEM((na * Ka, Nb), F32)],
        compiler_params=_cp("arbitrary"),
    )(*a_list, b)


def _mix_out(attn, pool, w_out, x, g_post, g_next, *, name, tm=256):
    S, K = attn.shape
    D = w_out.shape[1]

    def body(a_ref, p_ref, w_ref, x_ref, gp_ref, gn_ref, mixed_ref, x2_ref, h2_ref):
        mixed = _dot(a_ref[...], w_ref[:K, :], NN) + _dot(p_ref[...], w_ref[K:, :], NN)
        r = lax.rsqrt(jnp.mean(mixed * mixed, axis=-1, keepdims=True) + EPS)
        x2 = x_ref[...] + mixed * r * gp_ref[...]
        r2 = lax.rsqrt(jnp.mean(x2 * x2, axis=-1, keepdims=True) + EPS)
        mixed_ref[...] = mixed
        x2_ref[...] = x2
        h2_ref[...] = (x2 * r2 * gn_ref[...]).astype(BF16)

    row = lambda i: (i, 0)
    fix = lambda i: (0, 0)
    return pl.pallas_call(
        body, name=name, grid=(S // tm,),
        in_specs=[pl.BlockSpec((tm, K), row), pl.BlockSpec((tm, K), row), pl.BlockSpec((2 * K, D), fix),
                  pl.BlockSpec((tm, D), row), pl.BlockSpec((1, D), fix), pl.BlockSpec((1, D), fix)],
        out_specs=[pl.BlockSpec((tm, D), row)] * 3,
        out_shape=[jax.ShapeDtypeStruct((S, D), F32), jax.ShapeDtypeStruct((S, D), F32),
                   jax.ShapeDtypeStruct((S, D), BF16)],
        compiler_params=_cp("parallel"),
    )(attn, pool, w_out, x, g_post, g_next)


def _ffn_out(y, w_down, x2, target, g_post, *, name, tm=512, sub=256):
    S, K = y.shape
    D = w_down.shape[1]

    def body(y_ref, w_ref, x2_ref, t_ref, g_ref, df_ref, dout_ref, loss_ref, gg_ref):
        i = pl.program_id(0)

        @pl.when(i == 0)
        def _():
            loss_ref[...] = jnp.zeros_like(loss_ref)
            gg_ref[...] = jnp.zeros_like(gg_ref)

        g = g_ref[...]
        w = w_ref[...]
        f_next = _dot(y_ref[0:sub, :], w, NN)
        for a in range(0, tm, sub):
            rows = slice(a, a + sub)
            f = f_next
            if a + sub < tm:
                f_next = _dot(y_ref[a + sub:a + 2 * sub, :], w, NN)
            r = lax.rsqrt(jnp.mean(f * f, axis=-1, keepdims=True) + EPS)
            out = x2_ref[rows, :] + f * r * g
            err = out - t_ref[rows, :]
            dy = err * (1.0 / D)
            df, dg = _rms_bwd(f, g, dy)
            df_ref[rows, :] = df.astype(BF16)
            dout_ref[rows, :] = dy
            gg_ref[...] += dg
            loss_ref[...] += 0.5 * jnp.sum(jnp.mean(err * err, axis=-1, keepdims=True))

    row = lambda i: (i, 0)
    fix = lambda i: (0, 0)
    return pl.pallas_call(
        body, name=name, grid=(S // tm,),
        in_specs=[pl.BlockSpec((tm, K), row), pl.BlockSpec((K, D), fix), pl.BlockSpec((tm, D), row),
                  pl.BlockSpec((tm, D), row), pl.BlockSpec((1, D), fix)],
        out_specs=[pl.BlockSpec((tm, D), row), pl.BlockSpec((tm, D), row),
                   pl.BlockSpec((8, 128), fix), pl.BlockSpec((1, D), fix)],
        out_shape=[jax.ShapeDtypeStruct((S, D), BF16), jax.ShapeDtypeStruct((S, D), F32),
                   jax.ShapeDtypeStruct((8, 128), F32), jax.ShapeDtypeStruct((1, D), F32)],
        compiler_params=_cp("arbitrary"),
    )(y, w_down, x2, target, g_post)


def _dgrad_norm(a_list, w, resid, xin, g, second, exchange, *, name, tm=512, sub=256):
    S, Kp = a_list[0].shape
    na = len(a_list)
    D = w.shape[1]
    nt = S // tm
    two = second is not None
    ng = len(exchange)
    recv_shapes, exchange_sems = _exchange_buffers(exchange)

    def body(*refs):
        a_refs = refs[:na]
        w_ref, r_ref, x_ref, g_ref = refs[na:na + 4]
        pos = na + 4
        if two:
            x2_ref, g2_ref = refs[pos:pos + 2]
            pos += 2
        g_refs = refs[pos:pos + ng]
        pos += ng
        dx_ref, gg_ref = refs[pos:pos + 2]
        pos += 2
        if two:
            d2_ref, gg2_ref = refs[pos:pos + 2]
            pos += 2
        r_refs = refs[pos:pos + ng]
        pos += ng
        i = pl.program_id(0)
        if ng:
            start, finish = _exchange_phases(g_refs, r_refs, *refs[pos:])
            pl.when(i == 0)(start)

        @pl.when(i == 0)
        def _():
            gg_ref[...] = jnp.zeros_like(gg_ref)
            if two:
                gg2_ref[...] = jnp.zeros_like(gg2_ref)

        def dh_of(a):
            return functools.reduce(jnp.add, [_dot(a_refs[q][a:a + sub, :], w_ref[q * Kp:(q + 1) * Kp, :], NN)
                                              for q in range(na)])

        dh_next = dh_of(0)
        for a in range(0, tm, sub):
            rows = slice(a, a + sub)
            dh = dh_next
            if a + sub < tm:
                dh_next = dh_of(a + sub)
            d1, dg1 = _rms_bwd(x_ref[rows, :], g_ref[...], dh)
            dx = r_ref[rows, :] + d1
            dx_ref[rows, :] = dx
            gg_ref[...] += dg1
            if two:
                d2, dg2 = _rms_bwd(x2_ref[rows, :], g2_ref[...], dx)
                d2_ref[rows, :] = d2.astype(BF16)
                gg2_ref[...] += dg2
        if ng:
            pl.when(i == nt - 1)(finish)

    row = lambda i: (i, 0)
    fix = lambda i: (0, 0)
    in_specs = [pl.BlockSpec((tm, Kp), row)] * na + [
        pl.BlockSpec((na * Kp, D), fix, pipeline_mode=pl.Buffered(1)), pl.BlockSpec((tm, D), row),
        pl.BlockSpec((tm, D), row), pl.BlockSpec((1, D), fix)]
    args = list(a_list) + [w, resid, xin, g]
    out_specs = [pl.BlockSpec((tm, D), row), pl.BlockSpec((1, D), fix)]
    out_shape = [jax.ShapeDtypeStruct((S, D), F32), jax.ShapeDtypeStruct((1, D), F32)]
    if two:
        in_specs += [pl.BlockSpec((tm, D), row), pl.BlockSpec((1, D), fix)]
        args += list(second)
        out_specs += [pl.BlockSpec((tm, D), row), pl.BlockSpec((1, D), fix)]
        out_shape += [jax.ShapeDtypeStruct((S, D), BF16), jax.ShapeDtypeStruct((1, D), F32)]
    n_plain = len(out_shape)
    out = pl.pallas_call(
        body, name=name, grid=(nt,), in_specs=in_specs + [ANY] * ng, out_specs=out_specs + [ANY] * ng,
        out_shape=out_shape + recv_shapes, scratch_shapes=exchange_sems if ng else [],
        compiler_params=_cp("arbitrary"),
    )(*args, *exchange)
    return (*out[:n_plain], out[n_plain:]) if ng else out


def _band_mask(first_block):
    qi = lax.broadcasted_iota(jnp.int32, (BLOCK, 2 * BLOCK), 0)
    ki = lax.broadcasted_iota(jnp.int32, (BLOCK, 2 * BLOCK), 1)
    first_key = jnp.where(first_block, BLOCK, 0)
    return (ki >= qi) & (ki <= qi + BLOCK) & (ki >= first_key)


def _lane_masks():
    lane = lax.broadcasted_iota(jnp.int32, (1, 2 * HEAD_DIM), 1)
    return (lane < HEAD_DIM, lane >= HEAD_DIM)


CHUNK = BLOCK * max(DILATIONS)
SLAB = 2 * HEAD_DIM
N_SLABS = ATTN_WIDTH // SLAB


def _unit_rows(d, b):
    def rows(r):
        start = r + BLOCK * d * b
        return pl.ds(start, BLOCK, stride=d) if d > 1 else pl.ds(start, BLOCK)
    return rows


def _attn_units():
    for p, d in enumerate(DILATIONS):
        nbc = CHUNK // (BLOCK * d)
        for b in range(nbc):
            for r in range(d):
                yield p, d, b, r, nbc


def _attn_in_specs(nc, n_cur):
    prev = lambda c: jnp.maximum(jnp.minimum(c, nc - 1) - 1, 0)
    cur = lambda c: jnp.minimum(c, nc - 1)
    blk = lambda f: pl.BlockSpec((CHUNK, SLAB), f)
    specs = [blk(lambda h, c: (cur(c), h)),
             blk(lambda h, c: (prev(c), N_SLABS + h)), blk(lambda h, c: (cur(c), N_SLABS + h)),
             blk(lambda h, c: (prev(c), 2 * N_SLABS + h)), blk(lambda h, c: (cur(c), 2 * N_SLABS + h))]
    return specs + [blk(lambda h, c: (cur(c), h))] * n_cur


def _attn_fwd(proj, payload, *, name):
    S = proj.shape[0]
    nc = S // CHUNK
    n = len(DILATIONS)
    npay = len(payload)
    n_steps = N_SLABS * nc

    def body(*refs):
        q_ref, kp_ref, kc_ref, vp_ref, vc_ref = refs[:5]
        pay_refs = refs[5:5 + npay]
        attn_ref, lse_ref, attn16_ref = refs[5 + npay:8 + npay]
        all_refs = refs[8 + npay:8 + 2 * npay]
        scr = refs[8 + 2 * npay:]
        o_scr, l_scr = scr[:n], scr[n:2 * n]
        start, forward, finish = _gather_phases(pay_refs, all_refs, *scr[2 * n:])
        step = pl.program_id(0) * nc + pl.program_id(1)
        pl.when(step == 0)(start)
        c = pl.program_id(1)
        lms = _lane_masks()
        plain, first = (jnp.tile(_band_mask(f), (2, 1)) for f in (False, c == 0))
        def scores(unit):
            p, d, b, r, nbc = unit
            rows = _unit_rows(d, b)(r)
            prow = _unit_rows(d, (b - 1) % nbc)(r)
            kpr, vpr = (kc_ref, vc_ref) if b > 0 else (kp_ref, vp_ref)
            q = q_ref[rows, :].astype(BF16)
            kcat = jnp.concatenate([kpr[prow, :], kc_ref[rows, :]], axis=0).astype(BF16)
            vcat = jnp.concatenate([vpr[prow, :], vc_ref[rows, :]], axis=0).astype(BF16)
            q2 = jnp.concatenate([jnp.where(lm, q, jnp.zeros_like(q)) for lm in lms], axis=0) * SCALE
            return p, rows, plain if b > 0 else first, vcat, _dot(q2, kcat, NT)

        units = list(_attn_units())
        nxt = scores(units[0])
        for k in range(len(units)):
            p, rows, mask2, vcat, s = nxt
            if k + 1 < len(units):
                nxt = scores(units[k + 1])
            s = jnp.where(mask2, s, NEG_INF)
            m = jnp.max(s, axis=-1, keepdims=True)
            e = jnp.exp(s - m)
            l = jnp.sum(e, axis=-1, keepdims=True)
            o2 = _dot(e.astype(BF16), vcat, NN) / l
            lse2 = m + jnp.log(l)
            o_scr[p][rows, :] = jnp.where(lms[0], o2[:BLOCK], o2[BLOCK:])
            l_scr[p][rows, :] = jnp.where(lms[0], lse2[:BLOCK], lse2[BLOCK:])
        ls = [l_scr[p][...] for p in range(n)]
        top = functools.reduce(jnp.maximum, ls)
        es = [jnp.exp(l - top) for l in ls]
        den = functools.reduce(jnp.add, es)
        num = functools.reduce(jnp.add, [e * o_scr[p][...] for p, e in enumerate(es)])
        attn = num / den
        attn_ref[...] = attn
        attn16_ref[...] = attn.astype(BF16)
        lse_ref[...] = top + jnp.log(den)
        pl.when(step == (2 * n_steps) // 3)(forward)
        pl.when(step == n_steps - 1)(finish)

    out = pl.pallas_call(
        body, name=name, grid=(N_SLABS, nc), in_specs=_attn_in_specs(nc, 0) + [ANY] * npay,
        out_specs=[pl.BlockSpec((CHUNK, SLAB), lambda h, c: (c, h))] * 3 + [ANY] * npay,
        out_shape=[jax.ShapeDtypeStruct((S, ATTN_WIDTH), F32)] * 2 + [jax.ShapeDtypeStruct((S, ATTN_WIDTH), BF16)]
        + _gathered_shapes(payload),
        scratch_shapes=[pltpu.VMEM((CHUNK, SLAB), F32)] * (2 * n) + _gather_sems(npay),
        compiler_params=_cp("arbitrary", "arbitrary"),
    )(proj, proj, proj, proj, proj, *payload)
    return (*out[:3], out[3:])


def _attn_bwd(proj, dcat, attn, lse, grads, blocks, *, name):
    S = proj.shape[0]
    nc = S // CHUNK
    ng, nb = len(grads), len(blocks)
    n = len(DILATIONS)
    n_steps = N_SLABS * (nc + 1)
    recv_shapes, exchange_sems = _exchange_buffers(grads)

    def body(*refs):
        q_ref, kp_ref, kc_ref, vp_ref, vc_ref, do_ref, o_ref, lse_ref = refs[:8]
        g_refs, b_refs = refs[8:8 + ng], refs[8 + ng:8 + ng + nb]
        outs = refs[8 + ng + nb:]
        dq_ref, dk_ref, dv_ref = outs[:3]
        r_refs, all_refs = outs[3:3 + ng], outs[3 + ng:3 + ng + nb]
        scr = outs[3 + ng + nb:]
        dk_prev, dv_prev = scr[:2]
        delta_h, lse_h = scr[2:4], scr[4:6]
        dq_p, dk_own, dk_back, dv_own, dv_back = (scr[6 + n * k:6 + n * (k + 1)] for k in range(5))
        start, finish = _exchange_phases(g_refs, r_refs, *scr[6 + 5 * n:9 + 5 * n])
        gather_start, gather_forward, gather_finish = _gather_phases(b_refs, all_refs, *scr[9 + 5 * n:])
        c = pl.program_id(1)
        step = pl.program_id(0) * (nc + 1) + c

        @pl.when(step == 0)
        def _():
            gather_start()
            start()

        @pl.when(c == 0)
        def _():
            dk_prev[...] = jnp.zeros_like(dk_prev)
            dv_prev[...] = jnp.zeros_like(dv_prev)

        @pl.when(c < nc)
        def _():
            lms = _lane_masks()
            plain, first = (jnp.tile(_band_mask(f), (2, 1)) for f in (False, c == 0))
            prod = do_ref[...] * o_ref[...]
            lse = lse_ref[...]
            lse_other = pltpu.roll(lse, HEAD_DIM, 1)
            for h, lm in enumerate(lms):
                delta = jnp.sum(jnp.where(lm, prod, 0.0), axis=-1, keepdims=True)
                delta_h[h][...] = jnp.broadcast_to(delta, (CHUNK, SLAB))
                lse_h[h][...] = jnp.where(lm, lse, lse_other)
            wide = lambda refs, rows: jnp.tile(jnp.concatenate([r[rows, :] for r in refs], axis=0), (1, 2))
            stack = lambda f: jnp.concatenate([f(lm) for lm in lms], axis=0)

            def scores(unit):
                p, d, b, r, nbc = unit
                rows = _unit_rows(d, b)(r)
                prow = _unit_rows(d, (b - 1) % nbc)(r)
                kpr, vpr = (kc_ref, vc_ref) if b > 0 else (kp_ref, vp_ref)
                q = q_ref[rows, :].astype(BF16)
                kcat = jnp.concatenate([kpr[prow, :], kc_ref[rows, :]], axis=0).astype(BF16)
                vcat = jnp.concatenate([vpr[prow, :], vc_ref[rows, :]], axis=0).astype(BF16)
                do = do_ref[rows, :]
                q2 = stack(lambda lm: jnp.where(lm, q, jnp.zeros_like(q))) * SCALE
                do2 = stack(lambda lm: jnp.where(lm, do, 0.0)).astype(BF16)
                return dict(p=p, rows=rows, prow=prow, mask2=plain if b > 0 else first, kcat=kcat, q2=q2, do2=do2,
                            s=_dot(q2, kcat, NT), dp=_dot(do2, vcat, NT))

            units = list(_attn_units())
            nxt = scores(units[0])
            for k in range(len(units)):
                u = nxt
                if k + 1 < len(units):
                    nxt = scores(units[k + 1])
                p, rows, prow, kcat = u["p"], u["rows"], u["prow"], u["kcat"]
                e = jnp.where(u["mask2"], jnp.exp(u["s"] - wide(lse_h, rows)), 0.0)
                ds = (e * (u["dp"] - wide(delta_h, rows))).astype(BF16)
                dq2 = _dot(ds, kcat, NN) * SCALE
                dq = jnp.where(lms[0], dq2[:BLOCK], dq2[BLOCK:])
                dkc = _dot(ds, u["q2"], TN)
                dvc = _dot(e.astype(BF16), u["do2"], TN)
                dq_p[p][rows, :] = dq
                dk_own[p][rows, :] = dkc[BLOCK:]
                dv_own[p][rows, :] = dvc[BLOCK:]
                dk_back[p][prow, :] = dkc[:BLOCK]
                dv_back[p][prow, :] = dvc[:BLOCK]
            dq_ref[...] = functools.reduce(jnp.add, [r[...] for r in dq_p]).astype(BF16)
            for prev, own, back, out_ref in ((dk_prev, dk_own, dk_back, dk_ref), (dv_prev, dv_own, dv_back, dv_ref)):
                for p, d in enumerate(DILATIONS):
                    tail = CHUNK - BLOCK * d
                    prev[tail:, :] += back[p][tail:, :]
                out_ref[...] = prev[...].astype(BF16)
                prev[...] = functools.reduce(jnp.add, [r[...] for r in own])
                for p, d in enumerate(DILATIONS):
                    tail = CHUNK - BLOCK * d
                    if tail:
                        prev[:tail, :] += back[p][:tail, :]

        @pl.when(c == nc)
        def _():
            dk_ref[...] = dk_prev[...].astype(BF16)
            dv_ref[...] = dv_prev[...].astype(BF16)

        pl.when(step == (2 * n_steps) // 3)(gather_forward)

        @pl.when(step == n_steps - 1)
        def _():
            gather_finish()
            finish()

    blk = lambda f: pl.BlockSpec((CHUNK, SLAB), f)
    late = lambda h, c: (jnp.maximum(c - 1, 0), h)
    out = pl.pallas_call(
        body, name=name, grid=(N_SLABS, nc + 1), in_specs=_attn_in_specs(nc, 3) + [ANY] * (ng + nb),
        out_specs=[blk(lambda h, c: (jnp.minimum(c, nc - 1), h)), blk(late), blk(late)] + [ANY] * (ng + nb),
        out_shape=[jax.ShapeDtypeStruct((S, ATTN_WIDTH), BF16)] * 3 + recv_shapes + _gathered_shapes(blocks),
        scratch_shapes=[pltpu.VMEM((CHUNK, SLAB), F32)] * (6 + 5 * n) + exchange_sems + _gather_sems(nb),
        compiler_params=_cp("arbitrary", "arbitrary"),
    )(proj, proj, proj, proj, proj, dcat, attn, lse, *grads, *blocks)
    return out[:3], out[3:3 + ng], out[3 + ng:]


def _split_bf16(a):
    hi = a.astype(BF16)
    lo = (a - hi.astype(F32)).astype(BF16)
    return hi, lo


def _pooled(ug, halo_g, w, row0, tm):
    ext = jnp.concatenate([halo_g, ug], axis=0)
    hi, lo = _split_bf16(ext)
    rr = lax.broadcasted_iota(jnp.int32, (tm, tm + HALO), 0)
    cc = lax.broadcasted_iota(jnp.int32, (tm, tm + HALO), 1)
    back = rr + HALO - cc
    win = ((back >= 0) & (back < w)).astype(BF16)
    wsum = _dot(win, hi, NN) + _dot(win, lo, NN)
    rows = row0 + lax.broadcasted_iota(jnp.int32, (tm, 1), 0)
    inv = 1.0 / jnp.minimum(rows + 1, w).astype(F32)
    return wsum * inv - ug


def _pool_fwd(u, u_col, pool_w, pool_scale, *, name, tm=256):
    S, W = u.shape[0], POOL_WIDTH
    G = POOL_GROUP_DIM

    def body(u_ref, h_ref, w_ref, s_ref, o_ref):
        i = pl.program_id(0)
        uv = u_ref[...]
        halo = jnp.where(i > 0, h_ref[...], 0.0)
        sls = [slice(g * G, (g + 1) * G) for g in range(len(POOL_WINDOWS))]
        pooled = [_pooled(uv[:, sl], halo[:, sl], w, i * tm, tm) for sl, w in zip(sls, POOL_WINDOWS)]
        zs = [_dot(p.astype(BF16), w_ref[g].astype(BF16), NN) for g, p in enumerate(pooled)]
        for sl, z in zip(sls, zs):
            o_ref[:, sl] = (z * s_ref[:, sl]).astype(BF16)

    per = tm // HALO
    return pl.pallas_call(
        body, name=name, grid=(S // tm,),
        in_specs=[pl.BlockSpec((tm, W), lambda i: (i, u_col)),
                  pl.BlockSpec((HALO, W), lambda i: (jnp.maximum(i * per - 1, 0), u_col)),
                  pl.BlockSpec((len(POOL_WINDOWS), G, G), lambda i: (0, 0, 0)),
                  pl.BlockSpec((1, W), lambda i: (0, 0))],
        out_specs=pl.BlockSpec((tm, W), lambda i: (i, 0)),
        out_shape=jax.ShapeDtypeStruct((S, W), BF16),
        compiler_params=_cp("parallel"),
    )(u, u, pool_w, pool_scale)


def _pool_bwd(u, u_col, dy, dy_col, pool_w, pool_scale, *, name, tm=256):
    S, W = u.shape[0], POOL_WIDTH
    G = POOL_GROUP_DIM
    nt = S // tm

    def body(u_ref, h_ref, dy_ref, dyn_ref, w_ref, s_ref, du_ref, gw_ref, gs_ref):
        i = pl.program_id(0)

        @pl.when(i == 0)
        def _():
            gw_ref[...] = jnp.zeros_like(gw_ref)
            gs_ref[...] = jnp.zeros_like(gs_ref)

        uv = u_ref[...]
        halo = jnp.where(i > 0, h_ref[...], 0.0)
        dyv = dy_ref[...]
        dyn = jnp.where(i < nt - 1, dyn_ref[...], 0.0)
        rr = lax.broadcasted_iota(jnp.int32, (tm, tm + HALO), 0)
        cc = lax.broadcasted_iota(jnp.int32, (tm, tm + HALO), 1)
        rows_ext = i * tm + lax.broadcasted_iota(jnp.int32, (tm + HALO, 1), 0)
        groups = list(enumerate(POOL_WINDOWS))
        sls = [slice(g * G, (g + 1) * G) for g, _ in groups]
        wgs = [w_ref[g].astype(BF16) for g, _ in groups]
        pooled = [_pooled(uv[:, sl], halo[:, sl], w, i * tm, tm).astype(BF16) for sl, (_, w) in zip(sls, groups)]
        dzs = [dyv[:, sl] * s_ref[:, sl] for sl in sls]
        dz_ext = [jnp.concatenate([dz, dyn[:, sl] * s_ref[:, sl]], axis=0).astype(BF16) for dz, sl in zip(dzs, sls)]
        dp_ext = [_dot(d, wg, NT) for d, wg in zip(dz_ext, wgs)]
        zs = [_dot(p, wg, NN) for p, wg in zip(pooled, wgs)]
        for (g, w), sl, p, dz, z, dp in zip(groups, sls, pooled, dzs, zs, dp_ext):
            gw_ref[g] += _dot(p, dz.astype(BF16), TN)
            gs_ref[:, sl] += jnp.sum(dyv[:, sl] * z, axis=0, keepdims=True)
            inv_ext = 1.0 / jnp.minimum(rows_ext + 1, w).astype(F32)
            hi, lo = _split_bf16(dp * inv_ext)
            ahead = cc - rr
            win = ((ahead >= 0) & (ahead < w)).astype(BF16)
            du_ref[:, sl] = (_dot(win, hi, NN) + _dot(win, lo, NN) - dp[:tm]).astype(BF16)

    per = tm // HALO
    nh = S // HALO
    return pl.pallas_call(
        body, name=name, grid=(nt,),
        in_specs=[pl.BlockSpec((tm, W), lambda i: (i, u_col)),
                  pl.BlockSpec((HALO, W), lambda i: (jnp.maximum(i * per - 1, 0), u_col)),
                  pl.BlockSpec((tm, W), lambda i: (i, dy_col)),
                  pl.BlockSpec((HALO, W), lambda i: (jnp.minimum((i + 1) * per, nh - 1), dy_col)),
                  pl.BlockSpec((len(POOL_WINDOWS), G, G), lambda i: (0, 0, 0)),
                  pl.BlockSpec((1, W), lambda i: (0, 0))],
        out_specs=[pl.BlockSpec((tm, W), lambda i: (i, 0)),
                   pl.BlockSpec((len(POOL_WINDOWS), G, G), lambda i: (0, 0, 0)),
                   pl.BlockSpec((1, W), lambda i: (0, 0))],
        out_shape=[jax.ShapeDtypeStruct((S, W), BF16),
                   jax.ShapeDtypeStruct((len(POOL_WINDOWS), G, G), F32),
                   jax.ShapeDtypeStruct((1, W), F32)],
        compiler_params=_cp("arbitrary"),
    )(u, u, dy, dy, pool_w, pool_scale)


GELU_K0 = math.sqrt(2.0 / math.pi)
GELU_K1 = 0.044715


def _gelu_parts(x):
    x2 = x * x
    t = jnp.tanh(x * (GELU_K0 + (GELU_K0 * GELU_K1) * x2))
    hp = 0.5 + 0.5 * t
    gelu = x * hp
    dgelu = hp + (x * (hp * (1.0 - t))) * (GELU_K0 + (3.0 * GELU_K0 * GELU_K1) * x2)
    return gelu, dgelu


def _shifted(ext, halo):
    return (pltpu.roll(ext, 2, 0)[halo:], pltpu.roll(ext, 1, 0)[halo:], ext[halo:])


def _conv(sh, w, b):
    return b + (sh[0] * w[0:1] + sh[1] * w[1:2] + sh[2] * w[2:3])


F32_ROWS = 8


def _ffn_up_glu(h, w_up_t, conv_w, conv_b, *, name, tm=2048, tn=256, sub=256):
    S, K = h.shape
    F = D_FF
    nj = F // tn

    def body(h_ref, wg_ref, wv_ref, cwg_ref, cwv_ref, cbg_ref, cbv_ref,
             ug_ref, uv_ref, cg_ref, cv_ref, y_ref, carry):
        i = pl.program_id(0)
        j = pl.program_id(1)

        w_cat = jnp.concatenate([wg_ref[...], wv_ref[...]], axis=0)
        conv_w_b = ((cwg_ref[...], cbg_ref[...]), (cwv_ref[...], cbv_ref[...]))
        halo = [jnp.where(i > 0, carry[j, s], 0.0) for s in range(2)]
        u_next = _dot(h_ref[0:sub, :], w_cat, NT)
        for a in range(0, tm, sub):
            u16 = u_next.astype(BF16)
            if a + sub < tm:
                u_next = _dot(h_ref[a + sub:a + 2 * sub, :], w_cat, NT)
            ug_ref[a:a + sub, :] = u16[:, :tn]
            uv_ref[a:a + sub, :] = u16[:, tn:]
            c = []
            for s, (cw, cb) in enumerate(conv_w_b):
                u = u16[:, s * tn:(s + 1) * tn].astype(F32)
                ext = jnp.concatenate([halo[s], u], axis=0)
                c.append(_conv(_shifted(ext, F32_ROWS), cw, cb))
                halo[s] = u[sub - F32_ROWS:]
            cg_ref[a:a + sub, :] = c[0].astype(BF16)
            cv_ref[a:a + sub, :] = c[1].astype(BF16)
            gelu, _ = _gelu_parts(c[0])
            y_ref[a:a + sub, :] = (gelu * c[1]).astype(BF16)
        for s in range(2):
            carry[j, s] = halo[s]

    tile = pl.BlockSpec((tm, tn), lambda i, j: (i, j))
    vec = lambda rows, off: pl.BlockSpec((rows, tn), lambda i, j: (0, j + off))
    return pl.pallas_call(
        body, name=name, grid=(S // tm, nj),
        in_specs=[pl.BlockSpec((tm, K), lambda i, j: (i, 0)),
                  pl.BlockSpec((tn, K), lambda i, j: (j, 0)), pl.BlockSpec((tn, K), lambda i, j: (j + nj, 0)),
                  vec(3, 0), vec(3, nj), vec(1, 0), vec(1, nj)],
        out_specs=[tile] * 5,
        out_shape=[jax.ShapeDtypeStruct((S, F), BF16)] * 5,
        scratch_shapes=[pltpu.VMEM((nj, 2, F32_ROWS, tn), F32)],
        compiler_params=_cp("arbitrary", "arbitrary"),
    )(h, w_up_t, w_up_t, conv_w, conv_w, conv_b, conv_b)


def _ffn_glu_bwd(u_g, u_v, c_g, c_v, df, w_down, h, conv_w, *, name, tm=2048, tn=256, sub=256):
    S = u_g.shape[0]
    F = D_FF
    D = df.shape[1]
    nj = F // tn
    nt = S // tm

    def body(ug_ref, uv_ref, cg_ref, cgn_ref, cv_ref, cvn_ref, df_ref, dfn_ref, wd_ref, h_ref, wg_ref, wv_ref,
             dug_ref, duv_ref, gug_ref, guv_ref, gd_ref, gwg_ref, gwv_ref, gbg_ref, gbv_ref,
             acc_u, acc_d):
        i = pl.program_id(1)

        @pl.when(i == 0)
        def _():
            for r in (gwg_ref, gwv_ref, gbg_ref, gbv_ref, acc_u, acc_d):
                r[...] = jnp.zeros_like(r)

        wg, wv = wg_ref[...], wv_ref[...]
        wd = wd_ref[...]
        dfn = jnp.where(i < nt - 1, dfn_ref[...], jnp.zeros_like(dfn_ref))
        n_ext = sub + HALO

        def ahead(dc):
            return dc[:sub], pltpu.roll(dc, n_ext - 1, 0)[:sub], pltpu.roll(dc, n_ext - 2, 0)[:sub]

        def ext(ref, nxt, a):
            b = a + sub
            return jnp.concatenate([ref[a:b, :], ref[b:b + HALO, :] if b < tm else nxt], axis=0)

        dy_next = _dot(ext(df_ref, dfn, 0), wd, NT)
        for a in range(0, tm, sub):
            b = a + sub
            dy_ext = dy_next
            if b < tm:
                dy_next = _dot(ext(df_ref, dfn, b), wd, NT)
            cg = ext(cg_ref, cgn_ref[...], a).astype(F32)
            cv = ext(cv_ref, cvn_ref[...], a).astype(F32)
            df_sub = df_ref[a:b, :]
            gelu, dgelu = _gelu_parts(cg)
            dcs_g = ahead(dy_ext * cv * dgelu)
            dcs_v = ahead(dy_ext * gelu)
            du_g = (dcs_g[0] * wg[2:3] + dcs_g[1] * wg[1:2] + dcs_g[2] * wg[0:1]).astype(BF16)
            du_v = (dcs_v[0] * wv[2:3] + dcs_v[1] * wv[1:2] + dcs_v[2] * wv[0:1]).astype(BF16)
            dug_ref[a:b, :] = du_g
            duv_ref[a:b, :] = du_v
            acc_u[...] += _dot(jnp.concatenate([du_g, du_v], axis=1), h_ref[a:b, :], TN)
            acc_d[...] += _dot((gelu[:sub] * cv[:sub]).astype(BF16), df_sub, TN)
            for dcs, u_ref, gw_ref, gb_ref in ((dcs_g, ug_ref, gwg_ref, gbg_ref), (dcs_v, uv_ref, gwv_ref, gbv_ref)):
                u = u_ref[a:b, :].astype(F32)
                gb_ref[...] += jnp.sum(dcs[0], axis=0, keepdims=True)
                for k in range(3):
                    gw_ref[k:k + 1, :] += jnp.sum(dcs[2 - k] * u, axis=0, keepdims=True)

        @pl.when(i == nt - 1)
        def _():
            gug_ref[...] = acc_u[:tn, :].astype(BF16)
            guv_ref[...] = acc_u[tn:, :].astype(BF16)
            gd_ref[...] = acc_d[...].astype(BF16)

    per = tm // HALO
    nh = S // HALO
    hnext = lambda i: jnp.minimum((i + 1) * per, nh - 1)
    tile = pl.BlockSpec((tm, tn), lambda j, i: (i, j))
    hn = pl.BlockSpec((HALO, tn), lambda j, i: (hnext(i), j))
    vec = lambda rows, off: pl.BlockSpec((rows, tn), lambda j, i: (0, j + off))
    wide = pl.BlockSpec((tm, D), lambda j, i: (i, 0))
    wrow = pl.BlockSpec((tn, D), lambda j, i: (j, 0))
    return pl.pallas_call(
        body, name=name, grid=(nj, nt),
        in_specs=[tile, tile, tile, hn, tile, hn, wide, pl.BlockSpec((HALO, D), lambda j, i: (hnext(i), 0)),
                  wrow, wide, vec(3, 0), vec(3, nj)],
        out_specs=[tile, tile, wrow, wrow, wrow, vec(3, 0), vec(3, 0), vec(1, 0), vec(1, 0)],
        out_shape=[jax.ShapeDtypeStruct((S, F), BF16), jax.ShapeDtypeStruct((S, F), BF16),
                   jax.ShapeDtypeStruct((F, D), BF16), jax.ShapeDtypeStruct((F, D), BF16),
                   jax.ShapeDtypeStruct((F, D), BF16),
                   jax.ShapeDtypeStruct((3, F), F32), jax.ShapeDtypeStruct((3, F), F32),
                   jax.ShapeDtypeStruct((1, F), F32), jax.ShapeDtypeStruct((1, F), F32)],
        scratch_shapes=[pltpu.VMEM((2 * tn, D), F32), pltpu.VMEM((tn, D), F32)],
        compiler_params=_cp("parallel", "arbitrary"),
    )(u_g, u_v, c_g, c_g, c_v, c_v, df, df, w_down, h, conv_w, conv_w)


def _sum_partials(parts, *, name, tr):
    _, R, C = parts.shape

    def body(p_ref, o_ref):
        tot = p_ref[0].astype(F32)
        for j in range(1, N_DEV):
            tot = tot + p_ref[j].astype(F32)
        o_ref[...] = tot

    return pl.pallas_call(
        body, name=name, grid=(R // tr,),
        in_specs=[pl.BlockSpec((N_DEV, tr, C), lambda i: (0, i, 0))],
        out_specs=pl.BlockSpec((tr, C), lambda i: (i, 0)),
        out_shape=jax.ShapeDtypeStruct((R, C), F32),
        compiler_params=_cp("parallel"),
    )(parts)


def _adamw(w, g, m, v, *, name, tr):
    R, C = w.shape
    c1 = 1.0 - ADAM_B1 ** ADAM_STEP
    c2 = 1.0 - ADAM_B2 ** ADAM_STEP

    def body(w_ref, g_ref, m_ref, v_ref, d_ref, nm_ref, nv_ref):
        g = g_ref[...]
        nm = ADAM_B1 * m_ref[...] + (1.0 - ADAM_B1) * g
        nv = ADAM_B2 * v_ref[...] + (1.0 - ADAM_B2) * (g * g)
        d_ref[...] = -ADAM_LR * ((nm / c1) / (jnp.sqrt(nv / c2) + ADAM_EPS) + ADAM_WD * w_ref[...])
        nm_ref[...] = nm
        nv_ref[...] = nv

    spec = pl.BlockSpec((tr, C), lambda i: (i, 0))
    return pl.pallas_call(
        body, name=name, grid=(R // tr,), in_specs=[spec] * 4, out_specs=[spec] * 3,
        out_shape=[jax.ShapeDtypeStruct((R, C), F32)] * 3,
        compiler_params=_cp("parallel"),
    )(w, g, m, v)


def _mesh_pos():
    return lax.axis_index("x"), lax.axis_index("y"), lax.axis_index("c")


def _gather_phases(x_refs, out_refs, send_sems, recv_sems, local_sems):
    x, y, c = _mesh_pos()
    me, sibling = (x, y, c), (x, y, 1 - c)
    chips = [(1 - x, y), (x, 1 - y), (1 - x, 1 - y)]
    arrays = range(len(x_refs))

    def slot(a, px, py, pc):
        return out_refs[a].at[4 * px + 2 * py + pc]

    def copy(a, k, block, to, own=False):
        return pltpu.make_async_remote_copy(
            src_ref=x_refs[a] if own else slot(a, *block), dst_ref=slot(a, *block),
            send_sem=send_sems.at[a, k], recv_sem=recv_sems.at[a, k], device_id=to, device_id_type=MESH)

    mine = [pltpu.make_async_copy(x_refs[a], slot(a, *me), local_sems.at[a]) for a in arrays]
    first = [copy(a, 0, me, sibling, own=True) for a in arrays]
    first += [copy(a, 1 + j, me, (*chip, c), own=True) for j, chip in enumerate(chips) for a in arrays]
    passed = [[copy(a, 4 + j, (*chip, c), sibling) for a in arrays] for j, chip in enumerate(chips)]

    def start():
        for cp in mine + first:
            cp.start()

    def forward():
        for j, chip in enumerate(chips):
            for a in arrays:
                copy(a, 1 + j, (*chip, c), me).wait_recv()
                passed[j][a].start()

    def finish():
        for a in arrays:
            copy(a, 0, sibling, me).wait_recv()
            for j, chip in enumerate(chips):
                copy(a, 4 + j, (*chip, 1 - c), me).wait_recv()
        for cp in first + [cp for row in passed for cp in row]:
            cp.wait_send()
        for cp in mine:
            cp.wait()

    return start, forward, finish


def _gather_sems(n):
    return [pltpu.SemaphoreType.DMA((n, 7)), pltpu.SemaphoreType.DMA((n, 7)), pltpu.SemaphoreType.DMA((n,))]


def _gathered_shapes(blocks):
    return [jax.ShapeDtypeStruct((N_DEV,) + b.shape, b.dtype) for b in blocks]


def _all_reduce_small(block, gathered, *, name):
    r0, r1 = block.shape[0], gathered.shape[1]

    def body(x_ref, more_ref, all_ref, sum_ref, *sems):
        for phase in _gather_phases([x_ref], [all_ref], *sems):
            phase()
        for ref, rows in ((all_ref, slice(0, r0)), (more_ref, slice(r0, r0 + r1))):
            tot = ref[0]
            for j in range(1, N_DEV):
                tot = tot + ref[j]
            sum_ref[rows, :] = tot

    return pl.pallas_call(
        body, name=name, in_specs=[VMEM, VMEM], out_specs=[VMEM, VMEM],
        out_shape=[jax.ShapeDtypeStruct((N_DEV,) + block.shape, block.dtype),
                   jax.ShapeDtypeStruct((r0 + r1, block.shape[1]), block.dtype)],
        scratch_shapes=_gather_sems(1),
        compiler_params=pltpu.CompilerParams(vmem_limit_bytes=V7X_VMEM_LIMIT),
    )(block, gathered)[1]


def _exchange_phases(g_refs, r_refs, send_sems, recv_sems, local_sems):
    x, y, c = _mesh_pos()
    me = 4 * x + 2 * y + c
    owns, remote = [], []
    for k, (g_ref, r_ref) in enumerate(zip(g_refs, r_refs)):
        rows = g_ref.shape[0] // N_DEV
        owns.append(pltpu.make_async_copy(g_ref.at[pl.ds(me * rows, rows)], r_ref.at[me], local_sems.at[k]))
        for p in range(1, N_DEV):
            px, py, pc = x ^ (p >> 2), y ^ ((p >> 1) & 1), c ^ (p & 1)
            peer = 4 * px + 2 * py + pc
            link = dict(send_sem=send_sems.at[k, p], recv_sem=recv_sems.at[k, p],
                        device_id=(px, py, pc), device_id_type=MESH)
            src = g_ref.at[pl.ds(peer * rows, rows)]
            send = pltpu.make_async_remote_copy(src_ref=src, dst_ref=r_ref.at[me], **link)
            arrival = pltpu.make_async_remote_copy(src_ref=src, dst_ref=r_ref.at[peer], **link)
            remote.append((send, arrival))

    def start():
        for own in owns:
            own.start()
        for send, _ in remote:
            send.start()

    def finish():
        for _, arrival in remote:
            arrival.wait_recv()
        for send, _ in remote:
            send.wait_send()
        for own in owns:
            own.wait()

    return start, finish


def _exchange_buffers(grads):
    n = len(grads)
    shapes = [jax.ShapeDtypeStruct((N_DEV, g.shape[0] // N_DEV, g.shape[1]), g.dtype) for g in grads]
    sems = [pltpu.SemaphoreType.DMA((n, N_DEV)), pltpu.SemaphoreType.DMA((n, N_DEV)),
            pltpu.SemaphoreType.DMA((n,))]
    return shapes, sems


def _unpack_gathered(gathered):
    w_out, w_up_t, w_down = (g.reshape(-1, D_MODEL) for g in gathered[:3])
    width = 2 * D_FF // N_DEV
    conv_w = jnp.transpose(gathered[3][:, :3, :width], (1, 0, 2)).reshape(3, 2 * D_FF)
    return w_out, w_up_t, w_down, conv_w


def _rest_payload(w_out, w_up, w_down, conv_w):
    rows, cols = conv_w.shape
    conv_w = jnp.pad(conv_w, ((0, (-rows) % F32_ROWS), (0, (-cols) % LANES)))
    return [w_out.astype(BF16), w_up.T.astype(BF16), w_down.astype(BF16), conv_w]


def _device_step(x, target, g_mix_pre, w_in_t_block, rest_payload, pool_w, pool_scale, g_mix_post, g_ffn_pre,
                 conv_b, g_ffn_post):
    h1, w_in_t = _rms_norm_gather(x, g_mix_pre, w_in_t_block, name="rms_mix_pre")
    w_in_t = w_in_t.reshape(-1, D_MODEL)
    proj = _matmul(h1, w_in_t, trans_b=True, out_dtype=F32, tm=512, tn=4 * ATTN_WIDTH, name="proj")
    attn, lse, attn16, gathered = _attn_fwd(proj, rest_payload, name="attn_fwd")
    w_out, w_up_t, w_down, conv_w = _unpack_gathered(gathered)
    pool = _pool_fwd(proj, 3, pool_w, pool_scale, name="pool_fwd")
    mixed, x2, h2 = _mix_out(attn16, pool, w_out, x, g_mix_post, g_ffn_pre, name="mix_out")
    u_g, u_v, c_g, c_v, y = _ffn_up_glu(h2, w_up_t, conv_w, conv_b, name="ffn_up_glu")
    df, d_out, loss_blk, gg_ffn_post = _ffn_out(y, w_down, x2, target, g_ffn_post, name="ffn_out")
    du_g, du_v, gw_up_g, gw_up_v, gw_down, gcw_g, gcw_v, gcb_g, gcb_v = _ffn_glu_bwd(
        u_g, u_v, c_g, c_v, df, w_down, h2, conv_w, name="ffn_glu_bwd")
    gw_up_t = jnp.concatenate([gw_up_g, gw_up_v], axis=0)
    dx2, gg_ffn_pre, dmixed, gg_mix_post = _dgrad_norm(
        [du_g, du_v], w_up_t, d_out, x2, g_ffn_pre, (mixed, g_mix_post), [], name="ffn_up_dgrad")
    gw_out = _matmul_tn([attn16, pool], dmixed, name="grad_w_out")
    dcat = _matmul(dmixed, w_out, trans_b=True, out_dtype=F32, tm=512, tn=1024, name="mix_out_dgrad")
    d_pool_in, g_pool_w, g_pool_scale = _pool_bwd(proj, 3, dcat, 1, pool_w, pool_scale, name="pool_bwd")
    early = dict(g_mix_post=gg_mix_post, g_ffn_pre=gg_ffn_pre, g_ffn_post=gg_ffn_post, pool_scale=g_pool_scale,
                 conv_b=jnp.concatenate([gcb_g, gcb_v], axis=1), pool_w=g_pool_w)
    early_block = _pack_rows([early[k] for k in _SMALL[1:]] + [jnp.concatenate([gcw_g, gcw_v], axis=1), loss_blk])
    dqkv, (r_out, r_up_t, r_down), (small_gathered,) = _attn_bwd(
        proj, dcat, attn, lse, [gw_out, gw_up_t, gw_down], [early_block], name="attn_bwd")
    dproj = list(dqkv) + [d_pool_in]
    gw_in_t = _matmul_tn(dproj, h1, name="grad_w_in")
    grad_x, gg_mix_pre, (r_in_t,) = _dgrad_norm(dproj, w_in_t, dx2, x, g_mix_pre, None, [gw_in_t], name="proj_dgrad")
    received = (r_in_t, r_out, r_up_t, r_down)
    return grad_x, received, gg_mix_pre, small_gathered


_SMALL = ("g_mix_pre", "g_mix_post", "g_ffn_pre", "g_ffn_post", "pool_scale", "conv_b", "pool_w")
LANES = 128


def _pack_rows(arrays):
    parts = []
    for a in arrays:
        a2 = a.reshape(-1, LANES)
        parts.append(jnp.pad(a2, ((0, (-a2.shape[0]) % 8), (0, 0))))
    return jnp.concatenate(parts, axis=0)


def _unpack_rows(packed, shapes):
    out, row = [], 0
    for shape in shapes:
        rows = math.prod(shape) // LANES
        out.append(packed[row:row + rows].reshape(shape))
        row += -(-rows // 8) * 8
    return out


def kernel(x, g_mix_pre, w_in, pool_w, pool_scale, w_out, g_mix_post, g_ffn_pre, w_up, conv_w, conv_b, w_down, g_ffn_post, loss_target, m_g_mix_pre, m_w_in, m_pool_w, m_pool_scale, m_w_out, m_g_mix_post, m_g_ffn_pre, m_w_up, m_conv_w, m_conv_b, m_w_down, m_g_ffn_post, v_g_mix_pre, v_w_in, v_pool_w, v_pool_scale, v_w_out, v_g_mix_post, v_g_ffn_pre, v_w_up, v_conv_w, v_conv_b, v_w_down, v_g_ffn_post):
    me = 4 * lax.axis_index("x") + 2 * lax.axis_index("y") + lax.axis_index("c")
    grad_x, recv, gg_mix_pre, small_gathered = _device_step(
        x[0], loss_target[0], g_mix_pre, w_in[0].T.astype(BF16),
        _rest_payload(w_out[0], w_up[0], w_down[0], conv_w[0]),
        pool_w[0], pool_scale, g_mix_post, g_ffn_pre, conv_b, g_ffn_post)

    g_in_t, g_out, g_up_t, g_down = (
        _sum_partials(r, name=f"sum_partials_{k}", tr=r.shape[1] // 2) for k, r in enumerate(recv))
    grads = {"w_in": g_in_t.T, "w_out": g_out, "w_up": g_up_t.T, "w_down": g_down}

    given = dict(g_mix_pre=g_mix_pre, g_mix_post=g_mix_post, g_ffn_pre=g_ffn_pre, g_ffn_post=g_ffn_post,
                 pool_scale=pool_scale, conv_b=conv_b, pool_w=pool_w)
    small_shapes = [given[k].shape for k in _SMALL]
    total = _all_reduce_small(_pack_rows([gg_mix_pre]), small_gathered, name="all_reduce_small")
    *small_grads, g_conv_w_all, loss_all = _unpack_rows(total, small_shapes + [(3, 2 * D_FF), (8, LANES)])
    loss = loss_all[0, 0]
    grads.update(zip(_SMALL, small_grads))
    width = 2 * D_FF // N_DEV
    grads["conv_w"] = lax.dynamic_slice_in_dim(g_conv_w_all, me * width, width, axis=1)[None]

    weights = dict(g_mix_pre=g_mix_pre, w_in=w_in, pool_w=pool_w, pool_scale=pool_scale, w_out=w_out,
                   g_mix_post=g_mix_post, g_ffn_pre=g_ffn_pre, w_up=w_up, conv_w=conv_w, conv_b=conv_b,
                   w_down=w_down, g_ffn_post=g_ffn_post)
    m_in = dict(g_mix_pre=m_g_mix_pre, w_in=m_w_in, pool_w=m_pool_w, pool_scale=m_pool_scale, w_out=m_w_out,
                g_mix_post=m_g_mix_post, g_ffn_pre=m_g_ffn_pre, w_up=m_w_up, conv_w=m_conv_w, conv_b=m_conv_b,
                w_down=m_w_down, g_ffn_post=m_g_ffn_post)
    v_in = dict(g_mix_pre=v_g_mix_pre, w_in=v_w_in, pool_w=v_pool_w, pool_scale=v_pool_scale, w_out=v_w_out,
                g_mix_post=v_g_mix_post, g_ffn_pre=v_g_ffn_pre, w_up=v_w_up, conv_w=v_conv_w, conv_b=v_conv_b,
                w_down=v_w_down, g_ffn_post=v_g_ffn_post)
    delta, new_m, new_v = {}, {}, {}
    for k in ("w_in", "w_out", "w_up", "w_down"):
        g = grads[k]
        d, nm, nv = _adamw(weights[k][0], g, m_in[k][0], v_in[k][0], name=f"adamw_{k}", tr=g.shape[0] // 2)
        grads[k], delta[k], new_m[k], new_v[k] = g[None], d[None], nm[None], nv[None]
    d, nm, nv = _adamw(weights["conv_w"][0], grads["conv_w"][0], m_in["conv_w"][0], v_in["conv_w"][0],
                       name="adamw_conv_w", tr=3)
    delta["conv_w"], new_m["conv_w"], new_v["conv_w"] = d[None], nm[None], nv[None]
    packed_w = _pack_rows([weights[k] for k in _SMALL])
    small_rows = packed_w.shape[0]
    d, nm, nv = _adamw(packed_w, total[:small_rows], _pack_rows([m_in[k] for k in _SMALL]),
                       _pack_rows([v_in[k] for k in _SMALL]), name="adamw_small", tr=small_rows)
    for k, dk, mk, vk in zip(_SMALL, _unpack_rows(d, small_shapes), _unpack_rows(nm, small_shapes),
                             _unpack_rows(nv, small_shapes)):
        delta[k], new_m[k], new_v[k] = dk, mk, vk

    order = ("g_mix_pre", "w_in", "pool_w", "pool_scale", "w_out", "g_mix_post", "g_ffn_pre", "w_up",
             "conv_w", "conv_b", "w_down", "g_ffn_post")
    return (loss, grad_x[None], *[grads[k] for k in order], *[delta[k] for k in order],
            *[new_m[k] for k in order], *[new_v[k] for k in order])
```

```python
import functools
import math

import jax
import jax.numpy as jnp
from jax import lax
from jax.experimental import pallas as pl
from jax.experimental.pallas import tpu as pltpu

F32 = jnp.float32
BF16 = jnp.bfloat16

D_MODEL = 1024
N_HEADS = 8
HEAD_DIM = 64
ATTN_WIDTH = N_HEADS * HEAD_DIM
DILATIONS = (1, 4, 16)
BLOCK = 128
POOL_WIDTH = 512
POOL_WINDOWS = (2, 4, 8, 16)
POOL_GROUP_DIM = 128
D_FF = 2816
EPS = 1e-6
NEG_INF = -1e30
SCALE = HEAD_DIM ** -0.5

ADAM_LR = 0.001
ADAM_B1 = 0.9
ADAM_B2 = 0.999
ADAM_EPS = 1e-08
ADAM_WD = 0.01
ADAM_STEP = 10

N_DEV = 8
HALO = 16
V7X_VMEM_LIMIT = 56 * 1024 * 1024

MESH = pl.DeviceIdType.MESH
ANY = pl.BlockSpec(memory_space=pl.ANY)
VMEM = pl.BlockSpec(memory_space=pltpu.VMEM)

NT = (((1,), (1,)), ((), ()))
NN = (((1,), (0,)), ((), ()))
TN = (((0,), (0,)), ((), ()))


def _cp(*sem):
    return pltpu.CompilerParams(dimension_semantics=sem, vmem_limit_bytes=V7X_VMEM_LIMIT)


def _dot(a, b, dn):
    return lax.dot_general(a, b, dn, preferred_element_type=F32)


def _rms_bwd(xin, g, dy):
    r = lax.rsqrt(jnp.mean(xin * xin, axis=-1, keepdims=True) + EPS)
    xh = xin * r
    gdy = g * dy
    dx = r * (gdy - xh * jnp.mean(gdy * xh, axis=-1, keepdims=True))
    dg = jnp.sum(dy * xh, axis=0, keepdims=True)
    return dx, dg


def _rms_norm_gather(x, g, block, *, name, tm=512):
    S, D = x.shape
    nt = S // tm

    def body(x_ref, g_ref, blk_ref, o_ref, all_ref, *sems):
        i = pl.program_id(0)
        start, forward, finish = _gather_phases([blk_ref], [all_ref], *sems)
        pl.when(i == 0)(start)
        xv = x_ref[...]
        r = lax.rsqrt(jnp.mean(xv * xv, axis=-1, keepdims=True) + EPS)
        o_ref[...] = (xv * r * g_ref[...]).astype(BF16)
        pl.when(i == nt - 1)(forward)
        pl.when(i == nt - 1)(finish)

    return pl.pallas_call(
        body, name=name, grid=(nt,),
        in_specs=[pl.BlockSpec((tm, D), lambda i: (i, 0)), pl.BlockSpec((1, D), lambda i: (0, 0)), ANY],
        out_specs=[pl.BlockSpec((tm, D), lambda i: (i, 0)), ANY],
        out_shape=[jax.ShapeDtypeStruct((S, D), BF16)] + _gathered_shapes([block]),
        scratch_shapes=_gather_sems(1),
        compiler_params=_cp("arbitrary"),
    )(x, g, block)


def _matmul(a, b, *, trans_b, out_dtype, tm, tn, name):
    M, K = a.shape
    N = b.shape[0] if trans_b else b.shape[1]
    dn = NT if trans_b else NN

    def body(a_ref, b_ref, o_ref):
        o_ref[...] = _dot(a_ref[...], b_ref[...], dn).astype(out_dtype)

    b_spec = (pl.BlockSpec((tn, K), lambda i, j: (j, 0)) if trans_b
              else pl.BlockSpec((K, tn), lambda i, j: (0, j)))
    return pl.pallas_call(
        body, name=name, grid=(M // tm, N // tn),
        in_specs=[pl.BlockSpec((tm, K), lambda i, j: (i, 0)), b_spec],
        out_specs=pl.BlockSpec((tm, tn), lambda i, j: (i, j)),
        out_shape=jax.ShapeDtypeStruct((M, N), out_dtype),
        compiler_params=_cp("parallel", "parallel"),
    )(a, b)


def _matmul_tn(a_list, b, *, name, ts=1024):
    S, Ka = a_list[0].shape
    na = len(a_list)
    Nb = b.shape[1]
    ns = S // ts

    def body(*refs):
        a_refs, b_ref, o_ref, acc = refs[:na], refs[na], refs[na + 1], refs[na + 2]
        s = pl.program_id(0)

        @pl.when(s == 0)
        def _():
            acc[...] = jnp.zeros_like(acc)

        acc[...] += _dot(jnp.concatenate([r[...] for r in a_refs], axis=1), b_ref[...], TN)

        @pl.when(s == ns - 1)
        def _():
            o_ref[...] = acc[...].astype(BF16)

    return pl.pallas_call(
        body, name=name, grid=(ns,),
        in_specs=[pl.BlockSpec((ts, Ka), lambda s: (s, 0))] * na + [pl.BlockSpec((ts, Nb), lambda s: (s, 0))],
        out_specs=pl.BlockSpec((na * Ka, Nb), lambda s: (0, 0)),
        out_shape=jax.ShapeDtypeStruct((na * Ka, Nb), BF16),
        scratch_shapes=[pltpu.VMEM((na * Ka, Nb), F32)],
        compiler_params=_cp("arbitrary"),
    )(*a_list, b)


def _mix_out(attn, pool, w_out, x, g_post, g_next, *, name, tm=512):
    S, K = attn.shape
    D = w_out.shape[1]

    def body(a_ref, p_ref, w_ref, x_ref, gp_ref, gn_ref, mixed_ref, x2_ref, h2_ref):
        mixed = _dot(a_ref[...], w_ref[:K, :], NN) + _dot(p_ref[...], w_ref[K:, :], NN)
        r = lax.rsqrt(jnp.mean(mixed * mixed, axis=-1, keepdims=True) + EPS)
        x2 = x_ref[...] + mixed * r * gp_ref[...]
        r2 = lax.rsqrt(jnp.mean(x2 * x2, axis=-1, keepdims=True) + EPS)
        mixed_ref[...] = mixed
        x2_ref[...] = x2
        h2_ref[...] = (x2 * r2 * gn_ref[...]).astype(BF16)

    row = lambda i: (i, 0)
    fix = lambda i: (0, 0)
    return pl.pallas_call(
        body, name=name, grid=(S // tm,),
        in_specs=[pl.BlockSpec((tm, K), row), pl.BlockSpec((tm, K), row), pl.BlockSpec((2 * K, D), fix),
                  pl.BlockSpec((tm, D), row), pl.BlockSpec((1, D), fix), pl.BlockSpec((1, D), fix)],
        out_specs=[pl.BlockSpec((tm, D), row)] * 3,
        out_shape=[jax.ShapeDtypeStruct((S, D), F32), jax.ShapeDtypeStruct((S, D), F32),
                   jax.ShapeDtypeStruct((S, D), BF16)],
        compiler_params=_cp("parallel"),
    )(attn, pool, w_out, x, g_post, g_next)


def _ffn_out(y, w_down, x2, target, g_post, *, name, tm=512, sub=256):
    S, K = y.shape
    D = w_down.shape[1]

    def body(y_ref, w_ref, x2_ref, t_ref, g_ref, df_ref, dout_ref, loss_ref, gg_ref):
        i = pl.program_id(0)

        @pl.when(i == 0)
        def _():
            loss_ref[...] = jnp.zeros_like(loss_ref)
            gg_ref[...] = jnp.zeros_like(gg_ref)

        g = g_ref[...]
        w = w_ref[...]
        f_next = _dot(y_ref[0:sub, :], w, NN)
        for a in range(0, tm, sub):
            rows = slice(a, a + sub)
            f = f_next
            if a + sub < tm:
                f_next = _dot(y_ref[a + sub:a + 2 * sub, :], w, NN)
            r = lax.rsqrt(jnp.mean(f * f, axis=-1, keepdims=True) + EPS)
            out = x2_ref[rows, :] + f * r * g
            err = out - t_ref[rows, :]
            dy = err * (1.0 / D)
            df, dg = _rms_bwd(f, g, dy)
            df_ref[rows, :] = df.astype(BF16)
            dout_ref[rows, :] = dy
            gg_ref[...] += dg
            loss_ref[...] += 0.5 * jnp.sum(jnp.mean(err * err, axis=-1, keepdims=True))

    row = lambda i: (i, 0)
    fix = lambda i: (0, 0)
    return pl.pallas_call(
        body, name=name, grid=(S // tm,),
        in_specs=[pl.BlockSpec((tm, K), row), pl.BlockSpec((K, D), fix), pl.BlockSpec((tm, D), row),
                  pl.BlockSpec((tm, D), row), pl.BlockSpec((1, D), fix)],
        out_specs=[pl.BlockSpec((tm, D), row), pl.BlockSpec((tm, D), row),
                   pl.BlockSpec((8, 128), fix), pl.BlockSpec((1, D), fix)],
        out_shape=[jax.ShapeDtypeStruct((S, D), BF16), jax.ShapeDtypeStruct((S, D), F32),
                   jax.ShapeDtypeStruct((8, 128), F32), jax.ShapeDtypeStruct((1, D), F32)],
        compiler_params=_cp("arbitrary"),
    )(y, w_down, x2, target, g_post)


def _dgrad_norm(a_list, w, resid, xin, g, second, exchange, *, name, tm=512, sub=256):
    S, Kp = a_list[0].shape
    na = len(a_list)
    D = w.shape[1]
    nt = S // tm
    two = second is not None
    ng = len(exchange)
    recv_shapes, exchange_sems = _exchange_buffers(exchange)

    def body(*refs):
        a_refs = refs[:na]
        w_ref, r_ref, x_ref, g_ref = refs[na:na + 4]
        pos = na + 4
        if two:
            x2_ref, g2_ref = refs[pos:pos + 2]
            pos += 2
        g_refs = refs[pos:pos + ng]
        pos += ng
        dx_ref, gg_ref = refs[pos:pos + 2]
        pos += 2
        if two:
            d2_ref, gg2_ref = refs[pos:pos + 2]
            pos += 2
        r_refs = refs[pos:pos + ng]
        pos += ng
        i = pl.program_id(0)
        if ng:
            start, finish = _exchange_phases(g_refs, r_refs, *refs[pos:])
            pl.when(i == 0)(start)

        @pl.when(i == 0)
        def _():
            gg_ref[...] = jnp.zeros_like(gg_ref)
            if two:
                gg2_ref[...] = jnp.zeros_like(gg2_ref)

        def dh_of(a):
            return functools.reduce(jnp.add, [_dot(a_refs[q][a:a + sub, :], w_ref[q * Kp:(q + 1) * Kp, :], NN)
                                              for q in range(na)])

        dh_next = dh_of(0)
        for a in range(0, tm, sub):
            rows = slice(a, a + sub)
            dh = dh_next
            if a + sub < tm:
                dh_next = dh_of(a + sub)
            d1, dg1 = _rms_bwd(x_ref[rows, :], g_ref[...], dh)
            dx = r_ref[rows, :] + d1
            dx_ref[rows, :] = dx
            gg_ref[...] += dg1
            if two:
                d2, dg2 = _rms_bwd(x2_ref[rows, :], g2_ref[...], dx)
                d2_ref[rows, :] = d2.astype(BF16)
                gg2_ref[...] += dg2
        if ng:
            pl.when(i == nt - 1)(finish)

    row = lambda i: (i, 0)
    fix = lambda i: (0, 0)
    in_specs = [pl.BlockSpec((tm, Kp), row)] * na + [
        pl.BlockSpec((na * Kp, D), fix, pipeline_mode=pl.Buffered(1)), pl.BlockSpec((tm, D), row),
        pl.BlockSpec((tm, D), row), pl.BlockSpec((1, D), fix)]
    args = list(a_list) + [w, resid, xin, g]
    out_specs = [pl.BlockSpec((tm, D), row), pl.BlockSpec((1, D), fix)]
    out_shape = [jax.ShapeDtypeStruct((S, D), F32), jax.ShapeDtypeStruct((1, D), F32)]
    if two:
        in_specs += [pl.BlockSpec((tm, D), row), pl.BlockSpec((1, D), fix)]
        args += list(second)
        out_specs += [pl.BlockSpec((tm, D), row), pl.BlockSpec((1, D), fix)]
        out_shape += [jax.ShapeDtypeStruct((S, D), BF16), jax.ShapeDtypeStruct((1, D), F32)]
    n_plain = len(out_shape)
    out = pl.pallas_call(
        body, name=name, grid=(nt,), in_specs=in_specs + [ANY] * ng, out_specs=out_specs + [ANY] * ng,
        out_shape=out_shape + recv_shapes, scratch_shapes=exchange_sems if ng else [],
        compiler_params=_cp("arbitrary"),
    )(*args, *exchange)
    return (*out[:n_plain], out[n_plain:]) if ng else out


def _band_mask(first_block):
    qi = lax.broadcasted_iota(jnp.int32, (BLOCK, 2 * BLOCK), 0)
    ki = lax.broadcasted_iota(jnp.int32, (BLOCK, 2 * BLOCK), 1)
    first_key = jnp.where(first_block, BLOCK, 0)
    return (ki >= qi) & (ki <= qi + BLOCK) & (ki >= first_key)


def _lane_masks():
    lane = lax.broadcasted_iota(jnp.int32, (1, 2 * HEAD_DIM), 1)
    return (lane < HEAD_DIM, lane >= HEAD_DIM)


CHUNK = BLOCK * max(DILATIONS)
SLAB = 2 * HEAD_DIM
N_SLABS = ATTN_WIDTH // SLAB


def _unit_rows(d, b):
    def rows(r):
        start = r + BLOCK * d * b
        return pl.ds(start, BLOCK, stride=d) if d > 1 else pl.ds(start, BLOCK)
    return rows


def _attn_units():
    for p, d in enumerate(DILATIONS):
        nbc = CHUNK // (BLOCK * d)
        for b in range(nbc):
            for r in range(d):
                yield p, d, b, r, nbc


def _attn_in_specs(nc, n_cur):
    prev = lambda c: jnp.maximum(jnp.minimum(c, nc - 1) - 1, 0)
    cur = lambda c: jnp.minimum(c, nc - 1)
    blk = lambda f: pl.BlockSpec((CHUNK, SLAB), f)
    specs = [blk(lambda h, c: (cur(c), h)),
             blk(lambda h, c: (prev(c), N_SLABS + h)), blk(lambda h, c: (cur(c), N_SLABS + h)),
             blk(lambda h, c: (prev(c), 2 * N_SLABS + h)), blk(lambda h, c: (cur(c), 2 * N_SLABS + h))]
    return specs + [blk(lambda h, c: (cur(c), h))] * n_cur


def _attn_fwd(proj, payload, *, name):
    S = proj.shape[0]
    nc = S // CHUNK
    n = len(DILATIONS)
    npay = len(payload)
    n_steps = N_SLABS * nc

    def body(*refs):
        q_ref, kp_ref, kc_ref, vp_ref, vc_ref = refs[:5]
        pay_refs = refs[5:5 + npay]
        attn_ref, lse_ref, attn16_ref = refs[5 + npay:8 + npay]
        all_refs = refs[8 + npay:8 + 2 * npay]
        scr = refs[8 + 2 * npay:]
        o_scr, l_scr = scr[:n], scr[n:2 * n]
        start, forward, finish = _gather_phases(pay_refs, all_refs, *scr[2 * n:])
        step = pl.program_id(0) * nc + pl.program_id(1)
        pl.when(step == 0)(start)
        c = pl.program_id(1)
        lms = _lane_masks()
        plain, first = (jnp.tile(_band_mask(f), (2, 1)) for f in (False, c == 0))
        def scores(unit):
            p, d, b, r, nbc = unit
            rows = _unit_rows(d, b)(r)
            prow = _unit_rows(d, (b - 1) % nbc)(r)
            kpr, vpr = (kc_ref, vc_ref) if b > 0 else (kp_ref, vp_ref)
            q = q_ref[rows, :].astype(BF16)
            kcat = jnp.concatenate([kpr[prow, :], kc_ref[rows, :]], axis=0).astype(BF16)
            vcat = jnp.concatenate([vpr[prow, :], vc_ref[rows, :]], axis=0).astype(BF16)
            q2 = jnp.concatenate([jnp.where(lm, q, jnp.zeros_like(q)) for lm in lms], axis=0) * SCALE
            return p, rows, plain if b > 0 else first, vcat, _dot(q2, kcat, NT)

        units = list(_attn_units())
        nxt = scores(units[0])
        for k in range(len(units)):
            p, rows, mask2, vcat, s = nxt
            if k + 1 < len(units):
                nxt = scores(units[k + 1])
            s = jnp.where(mask2, s, NEG_INF)
            m = jnp.max(s, axis=-1, keepdims=True)
            e = jnp.exp(s - m)
            l = jnp.sum(e, axis=-1, keepdims=True)
            o2 = _dot(e.astype(BF16), vcat, NN) / l
            lse2 = m + jnp.log(l)
            o_scr[p][rows, :] = jnp.where(lms[0], o2[:BLOCK], o2[BLOCK:])
            l_scr[p][rows, :] = jnp.where(lms[0], lse2[:BLOCK], lse2[BLOCK:])
        ls = [l_scr[p][...] for p in range(n)]
        top = functools.reduce(jnp.maximum, ls)
        es = [jnp.exp(l - top) for l in ls]
        den = functools.reduce(jnp.add, es)
        num = functools.reduce(jnp.add, [e * o_scr[p][...] for p, e in enumerate(es)])
        attn = num / den
        attn_ref[...] = attn
        attn16_ref[...] = attn.astype(BF16)
        lse_ref[...] = top + jnp.log(den)
        pl.when(step == (2 * n_steps) // 3)(forward)
        pl.when(step == n_steps - 1)(finish)

    out = pl.pallas_call(
        body, name=name, grid=(N_SLABS, nc), in_specs=_attn_in_specs(nc, 0) + [ANY] * npay,
        out_specs=[pl.BlockSpec((CHUNK, SLAB), lambda h, c: (c, h))] * 3 + [ANY] * npay,
        out_shape=[jax.ShapeDtypeStruct((S, ATTN_WIDTH), F32)] * 2 + [jax.ShapeDtypeStruct((S, ATTN_WIDTH), BF16)]
        + _gathered_shapes(payload),
        scratch_shapes=[pltpu.VMEM((CHUNK, SLAB), F32)] * (2 * n) + _gather_sems(npay),
        compiler_params=_cp("arbitrary", "arbitrary"),
    )(proj, proj, proj, proj, proj, *payload)
    return (*out[:3], out[3:])


def _attn_bwd(proj, dcat, attn, lse, grads, blocks, *, name):
    S = proj.shape[0]
    nc = S // CHUNK
    ng, nb = len(grads), len(blocks)
    n = len(DILATIONS)
    n_steps = N_SLABS * (nc + 1)
    recv_shapes, exchange_sems = _exchange_buffers(grads)

    def body(*refs):
        q_ref, kp_ref, kc_ref, vp_ref, vc_ref, do_ref, o_ref, lse_ref = refs[:8]
        g_refs, b_refs = refs[8:8 + ng], refs[8 + ng:8 + ng + nb]
        outs = refs[8 + ng + nb:]
        dq_ref, dk_ref, dv_ref = outs[:3]
        r_refs, all_refs = outs[3:3 + ng], outs[3 + ng:3 + ng + nb]
        scr = outs[3 + ng + nb:]
        dk_prev, dv_prev = scr[:2]
        delta_h, lse_h = scr[2:4], scr[4:6]
        dq_p, dk_own, dk_back, dv_own, dv_back = (scr[6 + n * k:6 + n * (k + 1)] for k in range(5))
        start, finish = _exchange_phases(g_refs, r_refs, *scr[6 + 5 * n:9 + 5 * n])
        gather_start, gather_forward, gather_finish = _gather_phases(b_refs, all_refs, *scr[9 + 5 * n:])
        c = pl.program_id(1)
        step = pl.program_id(0) * (nc + 1) + c

        @pl.when(step == 0)
        def _():
            gather_start()
            start()

        @pl.when(c == 0)
        def _():
            dk_prev[...] = jnp.zeros_like(dk_prev)
            dv_prev[...] = jnp.zeros_like(dv_prev)

        @pl.when(c < nc)
        def _():
            lms = _lane_masks()
            plain, first = (jnp.tile(_band_mask(f), (2, 1)) for f in (False, c == 0))
            prod = do_ref[...] * o_ref[...]
            lse = lse_ref[...]
            lse_other = pltpu.roll(lse, HEAD_DIM, 1)
            for h, lm in enumerate(lms):
                delta = jnp.sum(jnp.where(lm, prod, 0.0), axis=-1, keepdims=True)
                delta_h[h][...] = jnp.broadcast_to(delta, (CHUNK, SLAB))
                lse_h[h][...] = jnp.where(lm, lse, lse_other)
            wide = lambda refs, rows: jnp.tile(jnp.concatenate([r[rows, :] for r in refs], axis=0), (1, 2))
            stack = lambda f: jnp.concatenate([f(lm) for lm in lms], axis=0)

            def scores(unit):
                p, d, b, r, nbc = unit
                rows = _unit_rows(d, b)(r)
                prow = _unit_rows(d, (b - 1) % nbc)(r)
                kpr, vpr = (kc_ref, vc_ref) if b > 0 else (kp_ref, vp_ref)
                q = q_ref[rows, :].astype(BF16)
                kcat = jnp.concatenate([kpr[prow, :], kc_ref[rows, :]], axis=0).astype(BF16)
                vcat = jnp.concatenate([vpr[prow, :], vc_ref[rows, :]], axis=0).astype(BF16)
                do = do_ref[rows, :]
                q2 = stack(lambda lm: jnp.where(lm, q, jnp.zeros_like(q))) * SCALE
                do2 = stack(lambda lm: jnp.where(lm, do, 0.0)).astype(BF16)
                return dict(p=p, rows=rows, prow=prow, mask2=plain if b > 0 else first, kcat=kcat, q2=q2, do2=do2,
                            s=_dot(q2, kcat, NT), dp=_dot(do2, vcat, NT))

            units = list(_attn_units())
            nxt = scores(units[0])
            for k in range(len(units)):
                u = nxt
                if k + 1 < len(units):
                    nxt = scores(units[k + 1])
                p, rows, prow, kcat = u["p"], u["rows"], u["prow"], u["kcat"]
                e = jnp.where(u["mask2"], jnp.exp(u["s"] - wide(lse_h, rows)), 0.0)
                ds = (e * (u["dp"] - wide(delta_h, rows))).astype(BF16)
                dq2 = _dot(ds, kcat, NN) * SCALE
                dq = jnp.where(lms[0], dq2[:BLOCK], dq2[BLOCK:])
                dkc = _dot(ds, u["q2"], TN)
                dvc = _dot(e.astype(BF16), u["do2"], TN)
                dq_p[p][rows, :] = dq
                dk_own[p][rows, :] = dkc[BLOCK:]
                dv_own[p][rows, :] = dvc[BLOCK:]
                dk_back[p][prow, :] = dkc[:BLOCK]
                dv_back[p][prow, :] = dvc[:BLOCK]
            dq_ref[...] = functools.reduce(jnp.add, [r[...] for r in dq_p]).astype(BF16)
            for prev, own, back, out_ref in ((dk_prev, dk_own, dk_back, dk_ref), (dv_prev, dv_own, dv_back, dv_ref)):
                for p, d in enumerate(DILATIONS):
                    tail = CHUNK - BLOCK * d
                    prev[tail:, :] += back[p][tail:, :]
                out_ref[...] = prev[...].astype(BF16)
                prev[...] = functools.reduce(jnp.add, [r[...] for r in own])
                for p, d in enumerate(DILATIONS):
                    tail = CHUNK - BLOCK * d
                    if tail:
                        prev[:tail, :] += back[p][:tail, :]

        @pl.when(c == nc)
        def _():
            dk_ref[...] = dk_prev[...].astype(BF16)
            dv_ref[...] = dv_prev[...].astype(BF16)

        pl.when(step == (2 * n_steps) // 3)(gather_forward)

        @pl.when(step == n_steps - 1)
        def _():
            gather_finish()
            finish()

    blk = lambda f: pl.BlockSpec((CHUNK, SLAB), f)
    late = lambda h, c: (jnp.maximum(c - 1, 0), h)
    out = pl.pallas_call(
        body, name=name, grid=(N_SLABS, nc + 1), in_specs=_attn_in_specs(nc, 3) + [ANY] * (ng + nb),
        out_specs=[blk(lambda h, c: (jnp.minimum(c, nc - 1), h)), blk(late), blk(late)] + [ANY] * (ng + nb),
        out_shape=[jax.ShapeDtypeStruct((S, ATTN_WIDTH), BF16)] * 3 + recv_shapes + _gathered_shapes(blocks),
        scratch_shapes=[pltpu.VMEM((CHUNK, SLAB), F32)] * (6 + 5 * n) + exchange_sems + _gather_sems(nb),
        compiler_params=_cp("arbitrary", "arbitrary"),
    )(proj, proj, proj, proj, proj, dcat, attn, lse, *grads, *blocks)
    return out[:3], out[3:3 + ng], out[3 + ng:]


def _split_bf16(a):
    hi = a.astype(BF16)
    lo = (a - hi.astype(F32)).astype(BF16)
    return hi, lo


def _pooled(ug, halo_g, w, row0, tm):
    ext = jnp.concatenate([halo_g, ug], axis=0)
    hi, lo = _split_bf16(ext)
    rr = lax.broadcasted_iota(jnp.int32, (tm, tm + HALO), 0)
    cc = lax.broadcasted_iota(jnp.int32, (tm, tm + HALO), 1)
    back = rr + HALO - cc
    win = ((back >= 0) & (back < w)).astype(BF16)
    wsum = _dot(win, hi, NN) + _dot(win, lo, NN)
    rows = row0 + lax.broadcasted_iota(jnp.int32, (tm, 1), 0)
    inv = 1.0 / jnp.minimum(rows + 1, w).astype(F32)
    return wsum * inv - ug


def _pool_fwd(u, u_col, pool_w, pool_scale, *, name, tm=256):
    S, W = u.shape[0], POOL_WIDTH
    G = POOL_GROUP_DIM

    def body(u_ref, h_ref, w_ref, s_ref, o_ref):
        i = pl.program_id(0)
        uv = u_ref[...]
        halo = jnp.where(i > 0, h_ref[...], 0.0)
        sls = [slice(g * G, (g + 1) * G) for g in range(len(POOL_WINDOWS))]
        pooled = [_pooled(uv[:, sl], halo[:, sl], w, i * tm, tm) for sl, w in zip(sls, POOL_WINDOWS)]
        zs = [_dot(p.astype(BF16), w_ref[g].astype(BF16), NN) for g, p in enumerate(pooled)]
        for sl, z in zip(sls, zs):
            o_ref[:, sl] = (z * s_ref[:, sl]).astype(BF16)

    per = tm // HALO
    return pl.pallas_call(
        body, name=name, grid=(S // tm,),
        in_specs=[pl.BlockSpec((tm, W), lambda i: (i, u_col)),
                  pl.BlockSpec((HALO, W), lambda i: (jnp.maximum(i * per - 1, 0), u_col)),
                  pl.BlockSpec((len(POOL_WINDOWS), G, G), lambda i: (0, 0, 0)),
                  pl.BlockSpec((1, W), lambda i: (0, 0))],
        out_specs=pl.BlockSpec((tm, W), lambda i: (i, 0)),
        out_shape=jax.ShapeDtypeStruct((S, W), BF16),
        compiler_params=_cp("parallel"),
    )(u, u, pool_w, pool_scale)


def _pool_bwd(u, u_col, dy, dy_col, pool_w, pool_scale, *, name, tm=256):
    S, W = u.shape[0], POOL_WIDTH
    G = POOL_GROUP_DIM
    nt = S // tm

    def body(u_ref, h_ref, dy_ref, dyn_ref, w_ref, s_ref, du_ref, gw_ref, gs_ref):
        i = pl.program_id(0)

        @pl.when(i == 0)
        def _():
            gw_ref[...] = jnp.zeros_like(gw_ref)
            gs_ref[...] = jnp.zeros_like(gs_ref)

        uv = u_ref[...]
        halo = jnp.where(i > 0, h_ref[...], 0.0)
        dyv = dy_ref[...]
        dyn = jnp.where(i < nt - 1, dyn_ref[...], 0.0)
        rr = lax.broadcasted_iota(jnp.int32, (tm, tm + HALO), 0)
        cc = lax.broadcasted_iota(jnp.int32, (tm, tm + HALO), 1)
        rows_ext = i * tm + lax.broadcasted_iota(jnp.int32, (tm + HALO, 1), 0)
        groups = list(enumerate(POOL_WINDOWS))
        sls = [slice(g * G, (g + 1) * G) for g, _ in groups]
        wgs = [w_ref[g].astype(BF16) for g, _ in groups]
        pooled = [_pooled(uv[:, sl], halo[:, sl], w, i * tm, tm).astype(BF16) for sl, (_, w) in zip(sls, groups)]
        dzs = [dyv[:, sl] * s_ref[:, sl] for sl in sls]
        dz_ext = [jnp.concatenate([dz, dyn[:, sl] * s_ref[:, sl]], axis=0).astype(BF16) for dz, sl in zip(dzs, sls)]
        dp_ext = [_dot(d, wg, NT) for d, wg in zip(dz_ext, wgs)]
        zs = [_dot(p, wg, NN) for p, wg in zip(pooled, wgs)]
        for (g, w), sl, p, dz, z, dp in zip(groups, sls, pooled, dzs, zs, dp_ext):
            gw_ref[g] += _dot(p, dz.astype(BF16), TN)
            gs_ref[:, sl] += jnp.sum(dyv[:, sl] * z, axis=0, keepdims=True)
            inv_ext = 1.0 / jnp.minimum(rows_ext + 1, w).astype(F32)
            hi, lo = _split_bf16(dp * inv_ext)
            ahead = cc - rr
            win = ((ahead >= 0) & (ahead < w)).astype(BF16)
            du_ref[:, sl] = (_dot(win, hi, NN) + _dot(win, lo, NN) - dp[:tm]).astype(BF16)

    per = tm // HALO
    nh = S // HALO
    return pl.pallas_call(
        body, name=name, grid=(nt,),
        in_specs=[pl.BlockSpec((tm, W), lambda i: (i, u_col)),
                  pl.BlockSpec((HALO, W), lambda i: (jnp.maximum(i * per - 1, 0), u_col)),
                  pl.BlockSpec((tm, W), lambda i: (i, dy_col)),
                  pl.BlockSpec((HALO, W), lambda i: (jnp.minimum((i + 1) * per, nh - 1), dy_col)),
                  pl.BlockSpec((len(POOL_WINDOWS), G, G), lambda i: (0, 0, 0)),
                  pl.BlockSpec((1, W), lambda i: (0, 0))],
        out_specs=[pl.BlockSpec((tm, W), lambda i: (i, 0)),
                   pl.BlockSpec((len(POOL_WINDOWS), G, G), lambda i: (0, 0, 0)),
                   pl.BlockSpec((1, W), lambda i: (0, 0))],
        out_shape=[jax.ShapeDtypeStruct((S, W), BF16),
                   jax.ShapeDtypeStruct((len(POOL_WINDOWS), G, G), F32),
                   jax.ShapeDtypeStruct((1, W), F32)],
        compiler_params=_cp("arbitrary"),
    )(u, u, dy, dy, pool_w, pool_scale)


GELU_K0 = math.sqrt(2.0 / math.pi)
GELU_K1 = 0.044715


def _gelu_parts(x):
    x2 = x * x
    t = jnp.tanh(x * (GELU_K0 + (GELU_K0 * GELU_K1) * x2))
    hp = 0.5 + 0.5 * t
    gelu = x * hp
    dgelu = hp + (x * (hp * (1.0 - t))) * (GELU_K0 + (3.0 * GELU_K0 * GELU_K1) * x2)
    return gelu, dgelu


def _shifted(ext, halo):
    return (pltpu.roll(ext, 2, 0)[halo:], pltpu.roll(ext, 1, 0)[halo:], ext[halo:])


def _conv(sh, w, b):
    return b + (sh[0] * w[0:1] + sh[1] * w[1:2] + sh[2] * w[2:3])


F32_ROWS = 8


def _ffn_up_glu(h, w_up_t, conv_w, conv_b, *, name, tm=2048, tn=256, sub=256):
    S, K = h.shape
    F = D_FF
    nj = F // tn

    def body(h_ref, wg_ref, wv_ref, cwg_ref, cwv_ref, cbg_ref, cbv_ref,
             ug_ref, uv_ref, cg_ref, cv_ref, y_ref, carry):
        i = pl.program_id(0)
        j = pl.program_id(1)

        w_cat = jnp.concatenate([wg_ref[...], wv_ref[...]], axis=0)
        conv_w_b = ((cwg_ref[...], cbg_ref[...]), (cwv_ref[...], cbv_ref[...]))
        halo = [jnp.where(i > 0, carry[j, s], 0.0) for s in range(2)]
        u_next = _dot(h_ref[0:sub, :], w_cat, NT)
        for a in range(0, tm, sub):
            u16 = u_next.astype(BF16)
            if a + sub < tm:
                u_next = _dot(h_ref[a + sub:a + 2 * sub, :], w_cat, NT)
            ug_ref[a:a + sub, :] = u16[:, :tn]
            uv_ref[a:a + sub, :] = u16[:, tn:]
            c = []
            for s, (cw, cb) in enumerate(conv_w_b):
                u = u16[:, s * tn:(s + 1) * tn].astype(F32)
                ext = jnp.concatenate([halo[s], u], axis=0)
                c.append(_conv(_shifted(ext, F32_ROWS), cw, cb))
                halo[s] = u[sub - F32_ROWS:]
            cg_ref[a:a + sub, :] = c[0].astype(BF16)
            cv_ref[a:a + sub, :] = c[1].astype(BF16)
            gelu, _ = _gelu_parts(c[0])
            y_ref[a:a + sub, :] = (gelu * c[1]).astype(BF16)
        for s in range(2):
            carry[j, s] = halo[s]

    tile = pl.BlockSpec((tm, tn), lambda i, j: (i, j))
    vec = lambda rows, off: pl.BlockSpec((rows, tn), lambda i, j: (0, j + off))
    return pl.pallas_call(
        body, name=name, grid=(S // tm, nj),
        in_specs=[pl.BlockSpec((tm, K), lambda i, j: (i, 0)),
                  pl.BlockSpec((tn, K), lambda i, j: (j, 0)), pl.BlockSpec((tn, K), lambda i, j: (j + nj, 0)),
                  vec(3, 0), vec(3, nj), vec(1, 0), vec(1, nj)],
        out_specs=[tile] * 5,
        out_shape=[jax.ShapeDtypeStruct((S, F), BF16)] * 5,
        scratch_shapes=[pltpu.VMEM((nj, 2, F32_ROWS, tn), F32)],
        compiler_params=_cp("arbitrary", "arbitrary"),
    )(h, w_up_t, w_up_t, conv_w, conv_w, conv_b, conv_b)


def _ffn_glu_bwd(u_g, u_v, c_g, c_v, df, w_down, h, conv_w, *, name, tm=2048, tn=256, sub=256):
    S = u_g.shape[0]
    F = D_FF
    D = df.shape[1]
    nj = F // tn
    nt = S // tm

    def body(ug_ref, uv_ref, cg_ref, cgn_ref, cv_ref, cvn_ref, df_ref, dfn_ref, wd_ref, h_ref, wg_ref, wv_ref,
             dug_ref, duv_ref, gug_ref, guv_ref, gd_ref, gwg_ref, gwv_ref, gbg_ref, gbv_ref,
             acc_u, acc_d):
        i = pl.program_id(1)

        @pl.when(i == 0)
        def _():
            for r in (gwg_ref, gwv_ref, gbg_ref, gbv_ref, acc_u, acc_d):
                r[...] = jnp.zeros_like(r)

        wg, wv = wg_ref[...], wv_ref[...]
        wd = wd_ref[...]
        dfn = jnp.where(i < nt - 1, dfn_ref[...], jnp.zeros_like(dfn_ref))
        n_ext = sub + HALO

        def ahead(dc):
            return dc[:sub], pltpu.roll(dc, n_ext - 1, 0)[:sub], pltpu.roll(dc, n_ext - 2, 0)[:sub]

        def ext(ref, nxt, a):
            b = a + sub
            return jnp.concatenate([ref[a:b, :], ref[b:b + HALO, :] if b < tm else nxt], axis=0)

        dy_next = _dot(ext(df_ref, dfn, 0), wd, NT)
        for a in range(0, tm, sub):
            b = a + sub
            dy_ext = dy_next
            if b < tm:
                dy_next = _dot(ext(df_ref, dfn, b), wd, NT)
            cg = ext(cg_ref, cgn_ref[...], a).astype(F32)
            cv = ext(cv_ref, cvn_ref[...], a).astype(F32)
            df_sub = df_ref[a:b, :]
            gelu, dgelu = _gelu_parts(cg)
            dcs_g = ahead(dy_ext * cv * dgelu)
            dcs_v = ahead(dy_ext * gelu)
            du_g = (dcs_g[0] * wg[2:3] + dcs_g[1] * wg[1:2] + dcs_g[2] * wg[0:1]).astype(BF16)
            du_v = (dcs_v[0] * wv[2:3] + dcs_v[1] * wv[1:2] + dcs_v[2] * wv[0:1]).astype(BF16)
            dug_ref[a:b, :] = du_g
            duv_ref[a:b, :] = du_v
            acc_u[...] += _dot(jnp.concatenate([du_g, du_v], axis=1), h_ref[a:b, :], TN)
            acc_d[...] += _dot((gelu[:sub] * cv[:sub]).astype(BF16), df_sub, TN)
            for dcs, u_ref, gw_ref, gb_ref in ((dcs_g, ug_ref, gwg_ref, gbg_ref), (dcs_v, uv_ref, gwv_ref, gbv_ref)):
                u = u_ref[a:b, :].astype(F32)
                gb_ref[...] += jnp.sum(dcs[0], axis=0, keepdims=True)
                for k in range(3):
                    gw_ref[k:k + 1, :] += jnp.sum(dcs[2 - k] * u, axis=0, keepdims=True)

        @pl.when(i == nt - 1)
        def _():
            gug_ref[...] = acc_u[:tn, :].astype(BF16)
            guv_ref[...] = acc_u[tn:, :].astype(BF16)
            gd_ref[...] = acc_d[...].astype(BF16)

    per = tm // HALO
    nh = S // HALO
    hnext = lambda i: jnp.minimum((i + 1) * per, nh - 1)
    tile = pl.BlockSpec((tm, tn), lambda j, i: (i, j))
    hn = pl.BlockSpec((HALO, tn), lambda j, i: (hnext(i), j))
    vec = lambda rows, off: pl.BlockSpec((rows, tn), lambda j, i: (0, j + off))
    wide = pl.BlockSpec((tm, D), lambda j, i: (i, 0))
    wrow = pl.BlockSpec((tn, D), lambda j, i: (j, 0))
    return pl.pallas_call(
        body, name=name, grid=(nj, nt),
        in_specs=[tile, tile, tile, hn, tile, hn, wide, pl.BlockSpec((HALO, D), lambda j, i: (hnext(i), 0)),
                  wrow, wide, vec(3, 0), vec(3, nj)],
        out_specs=[tile, tile, wrow, wrow, wrow, vec(3, 0), vec(3, 0), vec(1, 0), vec(1, 0)],
        out_shape=[jax.ShapeDtypeStruct((S, F), BF16), jax.ShapeDtypeStruct((S, F), BF16),
                   jax.ShapeDtypeStruct((F, D), BF16), jax.ShapeDtypeStruct((F, D), BF16),
                   jax.ShapeDtypeStruct((F, D), BF16),
                   jax.ShapeDtypeStruct((3, F), F32), jax.ShapeDtypeStruct((3, F), F32),
                   jax.ShapeDtypeStruct((1, F), F32), jax.ShapeDtypeStruct((1, F), F32)],
        scratch_shapes=[pltpu.VMEM((2 * tn, D), F32), pltpu.VMEM((tn, D), F32)],
        compiler_params=_cp("parallel", "arbitrary"),
    )(u_g, u_v, c_g, c_g, c_v, c_v, df, df, w_down, h, conv_w, conv_w)


def _sum_partials(parts, *, name, tr):
    _, R, C = parts.shape

    def body(p_ref, o_ref):
        tot = p_ref[0].astype(F32)
        for j in range(1, N_DEV):
            tot = tot + p_ref[j].astype(F32)
        o_ref[...] = tot

    return pl.pallas_call(
        body, name=name, grid=(R // tr,),
        in_specs=[pl.BlockSpec((N_DEV, tr, C), lambda i: (0, i, 0))],
        out_specs=pl.BlockSpec((tr, C), lambda i: (i, 0)),
        out_shape=jax.ShapeDtypeStruct((R, C), F32),
        compiler_params=_cp("parallel"),
    )(parts)


def _adamw(w, g, m, v, *, name, tr):
    R, C = w.shape
    c1 = 1.0 - ADAM_B1 ** ADAM_STEP
    c2 = 1.0 - ADAM_B2 ** ADAM_STEP

    def body(w_ref, g_ref, m_ref, v_ref, d_ref, nm_ref, nv_ref):
        g = g_ref[...]
        nm = ADAM_B1 * m_ref[...] + (1.0 - ADAM_B1) * g
        nv = ADAM_B2 * v_ref[...] + (1.0 - ADAM_B2) * (g * g)
        d_ref[...] = -ADAM_LR * ((nm / c1) / (jnp.sqrt(nv / c2) + ADAM_EPS) + ADAM_WD * w_ref[...])
        nm_ref[...] = nm
        nv_ref[...] = nv

    spec = pl.BlockSpec((tr, C), lambda i: (i, 0))
    return pl.pallas_call(
        body, name=name, grid=(R // tr,), in_specs=[spec] * 4, out_specs=[spec] * 3,
        out_shape=[jax.ShapeDtypeStruct((R, C), F32)] * 3,
        compiler_params=_cp("parallel"),
    )(w, g, m, v)


def _mesh_pos():
    return lax.axis_index("x"), lax.axis_index("y"), lax.axis_index("c")


def _gather_phases(x_refs, out_refs, send_sems, recv_sems, local_sems):
    x, y, c = _mesh_pos()
    me, sibling = (x, y, c), (x, y, 1 - c)
    chips = [(1 - x, y), (x, 1 - y), (1 - x, 1 - y)]
    arrays = range(len(x_refs))

    def slot(a, px, py, pc):
        return out_refs[a].at[4 * px + 2 * py + pc]

    def copy(a, k, block, to, own=False):
        return pltpu.make_async_remote_copy(
            src_ref=x_refs[a] if own else slot(a, *block), dst_ref=slot(a, *block),
            send_sem=send_sems.at[a, k], recv_sem=recv_sems.at[a, k], device_id=to, device_id_type=MESH)

    mine = [pltpu.make_async_copy(x_refs[a], slot(a, *me), local_sems.at[a]) for a in arrays]
    first = [copy(a, 0, me, sibling, own=True) for a in arrays]
    first += [copy(a, 1 + j, me, (*chip, c), own=True) for j, chip in enumerate(chips) for a in arrays]
    passed = [[copy(a, 4 + j, (*chip, c), sibling) for a in arrays] for j, chip in enumerate(chips)]

    def start():
        for cp in mine + first:
            cp.start()

    def forward():
        for j, chip in enumerate(chips):
            for a in arrays:
                copy(a, 1 + j, (*chip, c), me).wait_recv()
                passed[j][a].start()

    def finish():
        for a in arrays:
            copy(a, 0, sibling, me).wait_recv()
            for j, chip in enumerate(chips):
                copy(a, 4 + j, (*chip, 1 - c), me).wait_recv()
        for cp in first + [cp for row in passed for cp in row]:
            cp.wait_send()
        for cp in mine:
            cp.wait()

    return start, forward, finish


def _gather_sems(n):
    return [pltpu.SemaphoreType.DMA((n, 7)), pltpu.SemaphoreType.DMA((n, 7)), pltpu.SemaphoreType.DMA((n,))]


def _gathered_shapes(blocks):
    return [jax.ShapeDtypeStruct((N_DEV,) + b.shape, b.dtype) for b in blocks]


def _all_reduce_small(block, gathered, *, name):
    r0, r1 = block.shape[0], gathered.shape[1]

    def body(x_ref, more_ref, all_ref, sum_ref, *sems):
        for phase in _gather_phases([x_ref], [all_ref], *sems):
            phase()
        for ref, rows in ((all_ref, slice(0, r0)), (more_ref, slice(r0, r0 + r1))):
            tot = ref[0]
            for j in range(1, N_DEV):
                tot = tot + ref[j]
            sum_ref[rows, :] = tot

    return pl.pallas_call(
        body, name=name, in_specs=[VMEM, VMEM], out_specs=[VMEM, VMEM],
        out_shape=[jax.ShapeDtypeStruct((N_DEV,) + block.shape, block.dtype),
                   jax.ShapeDtypeStruct((r0 + r1, block.shape[1]), block.dtype)],
        scratch_shapes=_gather_sems(1),
        compiler_params=pltpu.CompilerParams(vmem_limit_bytes=V7X_VMEM_LIMIT),
    )(block, gathered)[1]


def _exchange_phases(g_refs, r_refs, send_sems, recv_sems, local_sems):
    x, y, c = _mesh_pos()
    me = 4 * x + 2 * y + c
    owns, remote = [], []
    for k, (g_ref, r_ref) in enumerate(zip(g_refs, r_refs)):
        rows = g_ref.shape[0] // N_DEV
        owns.append(pltpu.make_async_copy(g_ref.at[pl.ds(me * rows, rows)], r_ref.at[me], local_sems.at[k]))
        for p in range(1, N_DEV):
            px, py, pc = x ^ (p >> 2), y ^ ((p >> 1) & 1), c ^ (p & 1)
            peer = 4 * px + 2 * py + pc
            link = dict(send_sem=send_sems.at[k, p], recv_sem=recv_sems.at[k, p],
                        device_id=(px, py, pc), device_id_type=MESH)
            src = g_ref.at[pl.ds(peer * rows, rows)]
            send = pltpu.make_async_remote_copy(src_ref=src, dst_ref=r_ref.at[me], **link)
            arrival = pltpu.make_async_remote_copy(src_ref=src, dst_ref=r_ref.at[peer], **link)
            remote.append((send, arrival))

    def start():
        for own in owns:
            own.start()
        for send, _ in remote:
            send.start()

    def finish():
        for _, arrival in remote:
            arrival.wait_recv()
        for send, _ in remote:
            send.wait_send()
        for own in owns:
            own.wait()

    return start, finish


def _exchange_buffers(grads):
    n = len(grads)
    shapes = [jax.ShapeDtypeStruct((N_DEV, g.shape[0] // N_DEV, g.shape[1]), g.dtype) for g in grads]
    sems = [pltpu.SemaphoreType.DMA((n, N_DEV)), pltpu.SemaphoreType.DMA((n, N_DEV)),
            pltpu.SemaphoreType.DMA((n,))]
    return shapes, sems


def _unpack_gathered(gathered):
    w_out, w_up_t, w_down = (g.reshape(-1, D_MODEL) for g in gathered[:3])
    width = 2 * D_FF // N_DEV
    conv_w = jnp.transpose(gathered[3][:, :3, :width], (1, 0, 2)).reshape(3, 2 * D_FF)
    return w_out, w_up_t, w_down, conv_w


def _rest_payload(w_out, w_up, w_down, conv_w):
    rows, cols = conv_w.shape
    conv_w = jnp.pad(conv_w, ((0, (-rows) % F32_ROWS), (0, (-cols) % LANES)))
    return [w_out.astype(BF16), w_up.T.astype(BF16), w_down.astype(BF16), conv_w]


def _device_step(x, target, g_mix_pre, w_in_t_block, rest_payload, pool_w, pool_scale, g_mix_post, g_ffn_pre,
                 conv_b, g_ffn_post):
    h1, w_in_t = _rms_norm_gather(x, g_mix_pre, w_in_t_block, name="rms_mix_pre")
    w_in_t = w_in_t.reshape(-1, D_MODEL)
    proj = _matmul(h1, w_in_t, trans_b=True, out_dtype=F32, tm=512, tn=4 * ATTN_WIDTH, name="proj")
    attn, lse, attn16, gathered = _attn_fwd(proj, rest_payload, name="attn_fwd")
    w_out, w_up_t, w_down, conv_w = _unpack_gathered(gathered)
    pool = _pool_fwd(proj, 3, pool_w, pool_scale, name="pool_fwd")
    mixed, x2, h2 = _mix_out(attn16, pool, w_out, x, g_mix_post, g_ffn_pre, name="mix_out")
    u_g, u_v, c_g, c_v, y = _ffn_up_glu(h2, w_up_t, conv_w, conv_b, name="ffn_up_glu")
    df, d_out, loss_blk, gg_ffn_post = _ffn_out(y, w_down, x2, target, g_ffn_post, name="ffn_out")
    du_g, du_v, gw_up_g, gw_up_v, gw_down, gcw_g, gcw_v, gcb_g, gcb_v = _ffn_glu_bwd(
        u_g, u_v, c_g, c_v, df, w_down, h2, conv_w, name="ffn_glu_bwd")
    gw_up_t = jnp.concatenate([gw_up_g, gw_up_v], axis=0)
    dx2, gg_ffn_pre, dmixed, gg_mix_post = _dgrad_norm(
        [du_g, du_v], w_up_t, d_out, x2, g_ffn_pre, (mixed, g_mix_post), [], name="ffn_up_dgrad")
    gw_out = _matmul_tn([attn16, pool], dmixed, name="grad_w_out")
    dcat = _matmul(dmixed, w_out, trans_b=True, out_dtype=F32, tm=512, tn=1024, name="mix_out_dgrad")
    d_pool_in, g_pool_w, g_pool_scale = _pool_bwd(proj, 3, dcat, 1, pool_w, pool_scale, name="pool_bwd")
    early = dict(g_mix_post=gg_mix_post, g_ffn_pre=gg_ffn_pre, g_ffn_post=gg_ffn_post, pool_scale=g_pool_scale,
                 conv_b=jnp.concatenate([gcb_g, gcb_v], axis=1), pool_w=g_pool_w)
    early_block = _pack_rows([early[k] for k in _SMALL[1:]] + [jnp.concatenate([gcw_g, gcw_v], axis=1), loss_blk])
    dqkv, (r_out, r_up_t, r_down), (small_gathered,) = _attn_bwd(
        proj, dcat, attn, lse, [gw_out, gw_up_t, gw_down], [early_block], name="attn_bwd")
    dproj = list(dqkv) + [d_pool_in]
    gw_in_t = _matmul_tn(dproj, h1, name="grad_w_in")
    grad_x, gg_mix_pre, (r_in_t,) = _dgrad_norm(dproj, w_in_t, dx2, x, g_mix_pre, None, [gw_in_t], name="proj_dgrad")
    received = (r_in_t, r_out, r_up_t, r_down)
    return grad_x, received, gg_mix_pre, small_gathered


_SMALL = ("g_mix_pre", "g_mix_post", "g_ffn_pre", "g_ffn_post", "pool_scale", "conv_b", "pool_w")
LANES = 128


def _pack_rows(arrays):
    parts = []
    for a in arrays:
        a2 = a.reshape(-1, LANES)
        parts.append(jnp.pad(a2, ((0, (-a2.shape[0]) % 8), (0, 0))))
    return jnp.concatenate(parts, axis=0)


def _unpack_rows(packed, shapes):
    out, row = [], 0
    for shape in shapes:
        rows = math.prod(shape) // LANES
        out.append(packed[row:row + rows].reshape(shape))
        row += -(-rows // 8) * 8
    return out


def kernel(x, g_mix_pre, w_in, pool_w, pool_scale, w_out, g_mix_post, g_ffn_pre, w_up, conv_w, conv_b, w_down, g_ffn_post, loss_target, m_g_mix_pre, m_w_in, m_pool_w, m_pool_scale, m_w_out, m_g_mix_post, m_g_ffn_pre, m_w_up, m_conv_w, m_conv_b, m_w_down, m_g_ffn_post, v_g_mix_pre, v_w_in, v_pool_w, v_pool_scale, v_w_out, v_g_mix_post, v_g_ffn_pre, v_w_up, v_conv_w, v_conv_b, v_w_down, v_g_ffn_post):
    me = 4 * lax.axis_index("x") + 2 * lax.axis_index("y") + lax.axis_index("c")
    grad_x, recv, gg_mix_pre, small_gathered = _device_step(
        x[0], loss_target[0], g_mix_pre, w_in[0].T.astype(BF16),
        _rest_payload(w_out[0], w_up[0], w_down[0], conv_w[0]),
        pool_w[0], pool_scale, g_mix_post, g_ffn_pre, conv_b, g_ffn_post)

    g_in_t, g_out, g_up_t, g_down = (
        _sum_partials(r, name=f"sum_partials_{k}", tr=r.shape[1] // 2) for k, r in enumerate(recv))
    grads = {"w_in": g_in_t.T, "w_out": g_out, "w_up": g_up_t.T, "w_down": g_down}

    given = dict(g_mix_pre=g_mix_pre, g_mix_post=g_mix_post, g_ffn_pre=g_ffn_pre, g_ffn_post=g_ffn_post,
                 pool_scale=pool_scale, conv_b=conv_b, pool_w=pool_w)
    small_shapes = [given[k].shape for k in _SMALL]
    total = _all_reduce_small(_pack_rows([gg_mix_pre]), small_gathered, name="all_reduce_small")
    *small_grads, g_conv_w_all, loss_all = _unpack_rows(total, small_shapes + [(3, 2 * D_FF), (8, LANES)])
    loss = loss_all[0, 0]
    grads.update(zip(_SMALL, small_grads))
    width = 2 * D_FF // N_DEV
    grads["conv_w"] = lax.dynamic_slice_in_dim(g_conv_w_all, me * width, width, axis=1)[None]

    weights = dict(g_mix_pre=g_mix_pre, w_in=w_in, pool_w=pool_w, pool_scale=pool_scale, w_out=w_out,
                   g_mix_post=g_mix_post, g_ffn_pre=g_ffn_pre, w_up=w_up, conv_w=conv_w, conv_b=conv_b,
                   w_down=w_down, g_ffn_post=g_ffn_post)
    m_in = dict(g_mix_pre=m_g_mix_pre, w_in=m_w_in, pool_w=m_pool_w, pool_scale=m_pool_scale, w_out=m_w_out,
                g_mix_post=m_g_mix_post, g_ffn_pre=m_g_ffn_pre, w_up=m_w_up, conv_w=m_conv_w, conv_b=m_conv_b,
                w_down=m_w_down, g_ffn_post=m_g_ffn_post)
    v_in = dict(g_mix_pre=v_g_mix_pre, w_in=v_w_in, pool_w=v_pool_w, pool_scale=v_pool_scale, w_out=v_w_out,
                g_mix_post=v_g_mix_post, g_ffn_pre=v_g_ffn_pre, w_up=v_w_up, conv_w=v_conv_w, conv_b=v_conv_b,
                w_down=v_w_down, g_ffn_post=v_g_ffn_post)
    delta, new_m, new_v = {}, {}, {}
    for k in ("w_in", "w_out", "w_up", "w_down"):
        g = grads[k]
        d, nm, nv = _adamw(weights[k][0], g, m_in[k][0], v_in[k][0], name=f"adamw_{k}", tr=g.shape[0] // 2)
        grads[k], delta[k], new_m[k], new_v[k] = g[None], d[None], nm[None], nv[None]
    d, nm, nv = _adamw(weights["conv_w"][0], grads["conv_w"][0], m_in["conv_w"][0], v_in["conv_w"][0],
                       name="adamw_conv_w", tr=3)
    delta["conv_w"], new_m["conv_w"], new_v["conv_w"] = d[None], nm[None], nv[None]
    packed_w = _pack_rows([weights[k] for k in _SMALL])
    small_rows = packed_w.shape[0]
    d, nm, nv = _adamw(packed_w, total[:small_rows], _pack_rows([m_in[k] for k in _SMALL]),
                       _pack_rows([v_in[k] for k in _SMALL]), name="adamw_small", tr=small_rows)
    for k, dk, mk, vk in zip(_SMALL, _unpack_rows(d, small_shapes), _unpack_rows(nm, small_shapes),
                             _unpack_rows(nv, small_shapes)):
        delta[k], new_m[k], new_v[k] = dk, mk, vk

    order = ("g_mix_pre", "w_in", "pool_w", "pool_scale", "w_out", "g_mix_post", "g_ffn_pre", "w_up",
             "conv_w", "conv_b", "w_down", "g_ffn_post")
    return (loss, grad_x[None], *[grads[k] for k in order], *[delta[k] for k in order],
            *[new_m[k] for k in order], *[new_v[k] for k in order])
```

```python
import functools
import math

import jax
import jax.numpy as jnp
from jax import lax
from jax.experimental import pallas as pl
from jax.experimental.pallas import tpu as pltpu

F32 = jnp.float32
BF16 = jnp.bfloat16

D_MODEL = 1024
N_HEADS = 8
HEAD_DIM = 64
ATTN_WIDTH = N_HEADS * HEAD_DIM
DILATIONS = (1, 4, 16)
BLOCK = 128
POOL_WIDTH = 512
POOL_WINDOWS = (2, 4, 8, 16)
POOL_GROUP_DIM = 128
D_FF = 2816
EPS = 1e-6
NEG_INF = -1e30
SCALE = HEAD_DIM ** -0.5

ADAM_LR = 0.001
ADAM_B1 = 0.9
ADAM_B2 = 0.999
ADAM_EPS = 1e-08
ADAM_WD = 0.01
ADAM_STEP = 10

N_DEV = 8
HALO = 16
V7X_VMEM_LIMIT = 56 * 1024 * 1024

MESH = pl.DeviceIdType.MESH
ANY = pl.BlockSpec(memory_space=pl.ANY)
VMEM = pl.BlockSpec(memory_space=pltpu.VMEM)

NT = (((1,), (1,)), ((), ()))
NN = (((1,), (0,)), ((), ()))
TN = (((0,), (0,)), ((), ()))


def _cp(*sem):
    return pltpu.CompilerParams(dimension_semantics=sem, vmem_limit_bytes=V7X_VMEM_LIMIT)


def _dot(a, b, dn):
    return lax.dot_general(a, b, dn, preferred_element_type=F32)


def _rms_bwd(xin, g, dy):
    r = lax.rsqrt(jnp.mean(xin * xin, axis=-1, keepdims=True) + EPS)
    xh = xin * r
    gdy = g * dy
    dx = r * (gdy - xh * jnp.mean(gdy * xh, axis=-1, keepdims=True))
    dg = jnp.sum(dy * xh, axis=0, keepdims=True)
    return dx, dg


def _rms_norm_gather(x, g, block, *, name, tm=512):
    S, D = x.shape
    nt = S // tm

    def body(x_ref, g_ref, blk_ref, o_ref, all_ref, *sems):
        i = pl.program_id(0)
        start, forward, finish = _gather_phases([blk_ref], [all_ref], *sems)
        pl.when(i == 0)(start)
        xv = x_ref[...]
        r = lax.rsqrt(jnp.mean(xv * xv, axis=-1, keepdims=True) + EPS)
        o_ref[...] = (xv * r * g_ref[...]).astype(BF16)
        pl.when(i == nt - 1)(forward)
        pl.when(i == nt - 1)(finish)

    return pl.pallas_call(
        body, name=name, grid=(nt,),
        in_specs=[pl.BlockSpec((tm, D), lambda i: (i, 0)), pl.BlockSpec((1, D), lambda i: (0, 0)), ANY],
        out_specs=[pl.BlockSpec((tm, D), lambda i: (i, 0)), ANY],
        out_shape=[jax.ShapeDtypeStruct((S, D), BF16)] + _gathered_shapes([block]),
        scratch_shapes=_gather_sems(1),
        compiler_params=_cp("arbitrary"),
    )(x, g, block)


def _matmul(a, b, *, trans_b, out_dtype, tm, tn, name):
    M, K = a.shape
    N = b.shape[0] if trans_b else b.shape[1]
    dn = NT if trans_b else NN

    def body(a_ref, b_ref, o_ref):
        o_ref[...] = _dot(a_ref[...], b_ref[...], dn).astype(out_dtype)

    b_spec = (pl.BlockSpec((tn, K), lambda i, j: (j, 0)) if trans_b
              else pl.BlockSpec((K, tn), lambda i, j: (0, j)))
    return pl.pallas_call(
        body, name=name, grid=(M // tm, N // tn),
        in_specs=[pl.BlockSpec((tm, K), lambda i, j: (i, 0)), b_spec],
        out_specs=pl.BlockSpec((tm, tn), lambda i, j: (i, j)),
        out_shape=jax.ShapeDtypeStruct((M, N), out_dtype),
        compiler_params=_cp("parallel", "parallel"),
    )(a, b)


def _matmul_tn(a_list, b, *, name, ts=1024):
    S, Ka = a_list[0].shape
    na = len(a_list)
    Nb = b.shape[1]
    ns = S // ts

    def body(*refs):
        a_refs, b_ref, o_ref, acc = refs[:na], refs[na], refs[na + 1], refs[na + 2]
        s = pl.program_id(0)

        @pl.when(s == 0)
        def _():
            acc[...] = jnp.zeros_like(acc)

        acc[...] += _dot(jnp.concatenate([r[...] for r in a_refs], axis=1), b_ref[...], TN)

        @pl.when(s == ns - 1)
        def _():
            o_ref[...] = acc[...].astype(BF16)

    return pl.pallas_call(
        body, name=name, grid=(ns,),
        in_specs=[pl.BlockSpec((ts, Ka), lambda s: (s, 0))] * na + [pl.BlockSpec((ts, Nb), lambda s: (s, 0))],
        out_specs=pl.BlockSpec((na * Ka, Nb), lambda s: (0, 0)),
        out_shape=jax.ShapeDtypeStruct((na * Ka, Nb), BF16),
        scratch_shapes=[pltpu.VMEM((na * Ka, Nb), F32)],
        compiler_params=_cp("arbitrary"),
    )(*a_list, b)


def _mix_out(attn, pool, w_out, x, g_post, g_next, *, name, tm=512):
    S, K = attn.shape
    D = w_out.shape[1]

    def body(a_ref, p_ref, w_ref, x_ref, gp_ref, gn_ref, mixed_ref, x2_ref, h2_ref):
        mixed = _dot(a_ref[...], w_ref[:K, :], NN) + _dot(p_ref[...], w_ref[K:, :], NN)
        r = lax.rsqrt(jnp.mean(mixed * mixed, axis=-1, keepdims=True) + EPS)
        x2 = x_ref[...] + mixed * r * gp_ref[...]
        r2 = lax.rsqrt(jnp.mean(x2 * x2, axis=-1, keepdims=True) + EPS)
        mixed_ref[...] = mixed
        x2_ref[...] = x2
        h2_ref[...] = (x2 * r2 * gn_ref[...]).astype(BF16)

    row = lambda i: (i, 0)
    fix = lambda i: (0, 0)
    return pl.pallas_call(
        body, name=name, grid=(S // tm,),
        in_specs=[pl.BlockSpec((tm, K), row), pl.BlockSpec((tm, K), row), pl.BlockSpec((2 * K, D), fix),
                  pl.BlockSpec((tm, D), row), pl.BlockSpec((1, D), fix), pl.BlockSpec((1, D), fix)],
        out_specs=[pl.BlockSpec((tm, D), row)] * 3,
        out_shape=[jax.ShapeDtypeStruct((S, D), F32), jax.ShapeDtypeStruct((S, D), F32),
                   jax.ShapeDtypeStruct((S, D), BF16)],
        compiler_params=_cp("parallel"),
    )(attn, pool, w_out, x, g_post, g_next)


def _ffn_out(y, w_down, x2, target, g_post, *, name, tm=512, sub=256):
    S, K = y.shape
    D = w_down.shape[1]

    def body(y_ref, w_ref, x2_ref, t_ref, g_ref, df_ref, dout_ref, loss_ref, gg_ref):
        i = pl.program_id(0)

        @pl.when(i == 0)
        def _():
            loss_ref[...] = jnp.zeros_like(loss_ref)
            gg_ref[...] = jnp.zeros_like(gg_ref)

        g = g_ref[...]
        w = w_ref[...]
        f_next = _dot(y_ref[0:sub, :], w, NN)
        for a in range(0, tm, sub):
            rows = slice(a, a + sub)
            f = f_next
            if a + sub < tm:
                f_next = _dot(y_ref[a + sub:a + 2 * sub, :], w, NN)
            r = lax.rsqrt(jnp.mean(f * f, axis=-1, keepdims=True) + EPS)
            out = x2_ref[rows, :] + f * r * g
            err = out - t_ref[rows, :]
            dy = err * (1.0 / D)
            df, dg = _rms_bwd(f, g, dy)
            df_ref[rows, :] = df.astype(BF16)
            dout_ref[rows, :] = dy
            gg_ref[...] += dg
            loss_ref[...] += 0.5 * jnp.sum(jnp.mean(err * err, axis=-1, keepdims=True))

    row = lambda i: (i, 0)
    fix = lambda i: (0, 0)
    return pl.pallas_call(
        body, name=name, grid=(S // tm,),
        in_specs=[pl.BlockSpec((tm, K), row), pl.BlockSpec((K, D), fix), pl.BlockSpec((tm, D), row),
                  pl.BlockSpec((tm, D), row), pl.BlockSpec((1, D), fix)],
        out_specs=[pl.BlockSpec((tm, D), row), pl.BlockSpec((tm, D), row),
                   pl.BlockSpec((8, 128), fix), pl.BlockSpec((1, D), fix)],
        out_shape=[jax.ShapeDtypeStruct((S, D), BF16), jax.ShapeDtypeStruct((S, D), F32),
                   jax.ShapeDtypeStruct((8, 128), F32), jax.ShapeDtypeStruct((1, D), F32)],
        compiler_params=_cp("arbitrary"),
    )(y, w_down, x2, target, g_post)


def _dgrad_norm(a_list, w, resid, xin, g, second, exchange, *, name, tm=512, sub=256):
    S, Kp = a_list[0].shape
    na = len(a_list)
    D = w.shape[1]
    nt = S // tm
    two = second is not None
    ng = len(exchange)
    recv_shapes, exchange_sems = _exchange_buffers(exchange)

    def body(*refs):
        a_refs = refs[:na]
        w_ref, r_ref, x_ref, g_ref = refs[na:na + 4]
        pos = na + 4
        if two:
            x2_ref, g2_ref = refs[pos:pos + 2]
            pos += 2
        g_refs = refs[pos:pos + ng]
        pos += ng
        dx_ref, gg_ref = refs[pos:pos + 2]
        pos += 2
        if two:
            d2_ref, gg2_ref = refs[pos:pos + 2]
            pos += 2
        r_refs = refs[pos:pos + ng]
        pos += ng
        i = pl.program_id(0)
        if ng:
            start, finish = _exchange_phases(g_refs, r_refs, *refs[pos:])
            pl.when(i == 0)(start)

        @pl.when(i == 0)
        def _():
            gg_ref[...] = jnp.zeros_like(gg_ref)
            if two:
                gg2_ref[...] = jnp.zeros_like(gg2_ref)

        def dh_of(a):
            return functools.reduce(jnp.add, [_dot(a_refs[q][a:a + sub, :], w_ref[q * Kp:(q + 1) * Kp, :], NN)
                                              for q in range(na)])

        dh_next = dh_of(0)
        for a in range(0, tm, sub):
            rows = slice(a, a + sub)
            dh = dh_next
            if a + sub < tm:
                dh_next = dh_of(a + sub)
            d1, dg1 = _rms_bwd(x_ref[rows, :], g_ref[...], dh)
            dx = r_ref[rows, :] + d1
            dx_ref[rows, :] = dx
            gg_ref[...] += dg1
            if two:
                d2, dg2 = _rms_bwd(x2_ref[rows, :], g2_ref[...], dx)
                d2_ref[rows, :] = d2.astype(BF16)
                gg2_ref[...] += dg2
        if ng:
            pl.when(i == nt - 1)(finish)

    row = lambda i: (i, 0)
    fix = lambda i: (0, 0)
    in_specs = [pl.BlockSpec((tm, Kp), row)] * na + [
        pl.BlockSpec((na * Kp, D), fix, pipeline_mode=pl.Buffered(1)), pl.BlockSpec((tm, D), row),
        pl.BlockSpec((tm, D), row), pl.BlockSpec((1, D), fix)]
    args = list(a_list) + [w, resid, xin, g]
    out_specs = [pl.BlockSpec((tm, D), row), pl.BlockSpec((1, D), fix)]
    out_shape = [jax.ShapeDtypeStruct((S, D), F32), jax.ShapeDtypeStruct((1, D), F32)]
    if two:
        in_specs += [pl.BlockSpec((tm, D), row), pl.BlockSpec((1, D), fix)]
        args += list(second)
        out_specs += [pl.BlockSpec((tm, D), row), pl.BlockSpec((1, D), fix)]
        out_shape += [jax.ShapeDtypeStruct((S, D), BF16), jax.ShapeDtypeStruct((1, D), F32)]
    n_plain = len(out_shape)
    out = pl.pallas_call(
        body, name=name, grid=(nt,), in_specs=in_specs + [ANY] * ng, out_specs=out_specs + [ANY] * ng,
        out_shape=out_shape + recv_shapes, scratch_shapes=exchange_sems if ng else [],
        compiler_params=_cp("arbitrary"),
    )(*args, *exchange)
    return (*out[:n_plain], out[n_plain:]) if ng else out


def _band_mask(first_block):
    qi = lax.broadcasted_iota(jnp.int32, (BLOCK, 2 * BLOCK), 0)
    ki = lax.broadcasted_iota(jnp.int32, (BLOCK, 2 * BLOCK), 1)
    first_key = jnp.where(first_block, BLOCK, 0)
    return (ki >= qi) & (ki <= qi + BLOCK) & (ki >= first_key)


def _lane_masks():
    lane = lax.broadcasted_iota(jnp.int32, (1, 2 * HEAD_DIM), 1)
    return (lane < HEAD_DIM, lane >= HEAD_DIM)


CHUNK = BLOCK * max(DILATIONS)
SLAB = 2 * HEAD_DIM
N_SLABS = ATTN_WIDTH // SLAB


def _unit_rows(d, b):
    def rows(r):
        start = r + BLOCK * d * b
        return pl.ds(start, BLOCK, stride=d) if d > 1 else pl.ds(start, BLOCK)
    return rows


def _attn_units():
    for p, d in enumerate(DILATIONS):
        nbc = CHUNK // (BLOCK * d)
        for b in range(nbc):
            for r in range(d):
                yield p, d, b, r, nbc


def _attn_in_specs(nc, n_cur):
    prev = lambda c: jnp.maximum(jnp.minimum(c, nc - 1) - 1, 0)
    cur = lambda c: jnp.minimum(c, nc - 1)
    blk = lambda f: pl.BlockSpec((CHUNK, SLAB), f)
    specs = [blk(lambda h, c: (cur(c), h)),
             blk(lambda h, c: (prev(c), N_SLABS + h)), blk(lambda h, c: (cur(c), N_SLABS + h)),
             blk(lambda h, c: (prev(c), 2 * N_SLABS + h)), blk(lambda h, c: (cur(c), 2 * N_SLABS + h))]
    return specs + [blk(lambda h, c: (cur(c), h))] * n_cur


def _attn_fwd(proj, payload, *, name):
    S = proj.shape[0]
    nc = S // CHUNK
    n = len(DILATIONS)
    npay = len(payload)
    n_steps = N_SLABS * nc

    def body(*refs):
        q_ref, kp_ref, kc_ref, vp_ref, vc_ref = refs[:5]
        pay_refs = refs[5:5 + npay]
        attn_ref, lse_ref, attn16_ref = refs[5 + npay:8 + npay]
        all_refs = refs[8 + npay:8 + 2 * npay]
        scr = refs[8 + 2 * npay:]
        o_scr, l_scr = scr[:n], scr[n:2 * n]
        start, forward, finish = _gather_phases(pay_refs, all_refs, *scr[2 * n:])
        step = pl.program_id(0) * nc + pl.program_id(1)
        pl.when(step == 0)(start)
        c = pl.program_id(1)
        lms = _lane_masks()
        plain, first = (jnp.tile(_band_mask(f), (2, 1)) for f in (False, c == 0))
        def scores(unit):
            p, d, b, r, nbc = unit
            rows = _unit_rows(d, b)(r)
            prow = _unit_rows(d, (b - 1) % nbc)(r)
            kpr, vpr = (kc_ref, vc_ref) if b > 0 else (kp_ref, vp_ref)
            q = q_ref[rows, :].astype(BF16)
            kcat = jnp.concatenate([kpr[prow, :], kc_ref[rows, :]], axis=0).astype(BF16)
            vcat = jnp.concatenate([vpr[prow, :], vc_ref[rows, :]], axis=0).astype(BF16)
            q2 = jnp.concatenate([jnp.where(lm, q, jnp.zeros_like(q)) for lm in lms], axis=0) * SCALE
            return p, rows, plain if b > 0 else first, vcat, _dot(q2, kcat, NT)

        units = list(_attn_units())
        nxt = scores(units[0])
        for k in range(len(units)):
            p, rows, mask2, vcat, s = nxt
            if k + 1 < len(units):
                nxt = scores(units[k + 1])
            s = jnp.where(mask2, s, NEG_INF)
            m = jnp.max(s, axis=-1, keepdims=True)
            e = jnp.exp(s - m)
            l = jnp.sum(e, axis=-1, keepdims=True)
            o2 = _dot(e.astype(BF16), vcat, NN) / l
            lse2 = m + jnp.log(l)
            o_scr[p][rows, :] = jnp.where(lms[0], o2[:BLOCK], o2[BLOCK:])
            l_scr[p][rows, :] = jnp.where(lms[0], lse2[:BLOCK], lse2[BLOCK:])
        ls = [l_scr[p][...] for p in range(n)]
        top = functools.reduce(jnp.maximum, ls)
        es = [jnp.exp(l - top) for l in ls]
        den = functools.reduce(jnp.add, es)
        num = functools.reduce(jnp.add, [e * o_scr[p][...] for p, e in enumerate(es)])
        attn = num / den
        attn_ref[...] = attn
        attn16_ref[...] = attn.astype(BF16)
        lse_ref[...] = top + jnp.log(den)
        pl.when(step == (2 * n_steps) // 3)(forward)
        pl.when(step == n_steps - 1)(finish)

    out = pl.pallas_call(
        body, name=name, grid=(N_SLABS, nc), in_specs=_attn_in_specs(nc, 0) + [ANY] * npay,
        out_specs=[pl.BlockSpec((CHUNK, SLAB), lambda h, c: (c, h))] * 3 + [ANY] * npay,
        out_shape=[jax.ShapeDtypeStruct((S, ATTN_WIDTH), F32)] * 2 + [jax.ShapeDtypeStruct((S, ATTN_WIDTH), BF16)]
        + _gathered_shapes(payload),
        scratch_shapes=[pltpu.VMEM((CHUNK, SLAB), F32)] * (2 * n) + _gather_sems(npay),
        compiler_params=_cp("arbitrary", "arbitrary"),
    )(proj, proj, proj, proj, proj, *payload)
    return (*out[:3], out[3:])


def _attn_bwd(proj, dcat, attn, lse, grads, blocks, *, name):
    S = proj.shape[0]
    nc = S // CHUNK
    ng, nb = len(grads), len(blocks)
    n = len(DILATIONS)
    n_steps = N_SLABS * (nc + 1)
    recv_shapes, exchange_sems = _exchange_buffers(grads)

    def body(*refs):
        q_ref, kp_ref, kc_ref, vp_ref, vc_ref, do_ref, o_ref, lse_ref = refs[:8]
        g_refs, b_refs = refs[8:8 + ng], refs[8 + ng:8 + ng + nb]
        outs = refs[8 + ng + nb:]
        dq_ref, dk_ref, dv_ref = outs[:3]
        r_refs, all_refs = outs[3:3 + ng], outs[3 + ng:3 + ng + nb]
        scr = outs[3 + ng + nb:]
        dk_prev, dv_prev = scr[:2]
        delta_h, lse_h = scr[2:4], scr[4:6]
        dq_p, dk_own, dk_back, dv_own, dv_back = (scr[6 + n * k:6 + n * (k + 1)] for k in range(5))
        start, finish = _exchange_phases(g_refs, r_refs, *scr[6 + 5 * n:9 + 5 * n])
        gather_start, gather_forward, gather_finish = _gather_phases(b_refs, all_refs, *scr[9 + 5 * n:])
        c = pl.program_id(1)
        step = pl.program_id(0) * (nc + 1) + c

        @pl.when(step == 0)
        def _():
            gather_start()
            start()

        @pl.when(c == 0)
        def _():
            dk_prev[...] = jnp.zeros_like(dk_prev)
            dv_prev[...] = jnp.zeros_like(dv_prev)

        @pl.when(c < nc)
        def _():
            lms = _lane_masks()
            plain, first = (jnp.tile(_band_mask(f), (2, 1)) for f in (False, c == 0))
            prod = do_ref[...] * o_ref[...]
            lse = lse_ref[...]
            lse_other = pltpu.roll(lse, HEAD_DIM, 1)
            for h, lm in enumerate(lms):
                delta = jnp.sum(jnp.where(lm, prod, 0.0), axis=-1, keepdims=True)
                delta_h[h][...] = jnp.broadcast_to(delta, (CHUNK, SLAB))
                lse_h[h][...] = jnp.where(lm, lse, lse_other)
            wide = lambda refs, rows: jnp.tile(jnp.concatenate([r[rows, :] for r in refs], axis=0), (1, 2))
            stack = lambda f: jnp.concatenate([f(lm) for lm in lms], axis=0)

            def scores(unit):
                p, d, b, r, nbc = unit
                rows = _unit_rows(d, b)(r)
                prow = _unit_rows(d, (b - 1) % nbc)(r)
                kpr, vpr = (kc_ref, vc_ref) if b > 0 else (kp_ref, vp_ref)
                q = q_ref[rows, :].astype(BF16)
                kcat = jnp.concatenate([kpr[prow, :], kc_ref[rows, :]], axis=0).astype(BF16)
                vcat = jnp.concatenate([vpr[prow, :], vc_ref[rows, :]], axis=0).astype(BF16)
                do = do_ref[rows, :]
                q2 = stack(lambda lm: jnp.where(lm, q, jnp.zeros_like(q))) * SCALE
                do2 = stack(lambda lm: jnp.where(lm, do, 0.0)).astype(BF16)
                return dict(p=p, rows=rows, prow=prow, mask2=plain if b > 0 else first, kcat=kcat, q2=q2, do2=do2,
                            s=_dot(q2, kcat, NT), dp=_dot(do2, vcat, NT))

            units = list(_attn_units())
            nxt = scores(units[0])
            for k in range(len(units)):
                u = nxt
                if k + 1 < len(units):
                    nxt = scores(units[k + 1])
                p, rows, prow, kcat = u["p"], u["rows"], u["prow"], u["kcat"]
                e = jnp.where(u["mask2"], jnp.exp(u["s"] - wide(lse_h, rows)), 0.0)
                ds = (e * (u["dp"] - wide(delta_h, rows))).astype(BF16)
                dq2 = _dot(ds, kcat, NN) * SCALE
                dq = jnp.where(lms[0], dq2[:BLOCK], dq2[BLOCK:])
                dkc = _dot(ds, u["q2"], TN)
                dvc = _dot(e.astype(BF16), u["do2"], TN)
                dq_p[p][rows, :] = dq
                dk_own[p][rows, :] = dkc[BLOCK:]
                dv_own[p][rows, :] = dvc[BLOCK:]
                dk_back[p][prow, :] = dkc[:BLOCK]
                dv_back[p][prow, :] = dvc[:BLOCK]
            dq_ref[...] = functools.reduce(jnp.add, [r[...] for r in dq_p]).astype(BF16)
            for prev, own, back, out_ref in ((dk_prev, dk_own, dk_back, dk_ref), (dv_prev, dv_own, dv_back, dv_ref)):
                for p, d in enumerate(DILATIONS):
                    tail = CHUNK - BLOCK * d
                    prev[tail:, :] += back[p][tail:, :]
                out_ref[...] = prev[...].astype(BF16)
                prev[...] = functools.reduce(jnp.add, [r[...] for r in own])
                for p, d in enumerate(DILATIONS):
                    tail = CHUNK - BLOCK * d
                    if tail:
                        prev[:tail, :] += back[p][:tail, :]

        @pl.when(c == nc)
        def _():
            dk_ref[...] = dk_prev[...].astype(BF16)
            dv_ref[...] = dv_prev[...].astype(BF16)

        pl.when(step == (2 * n_steps) // 3)(gather_forward)

        @pl.when(step == n_steps - 1)
        def _():
            gather_finish()
            finish()

    blk = lambda f: pl.BlockSpec((CHUNK, SLAB), f)
    late = lambda h, c: (jnp.maximum(c - 1, 0), h)
    out = pl.pallas_call(
        body, name=name, grid=(N_SLABS, nc + 1), in_specs=_attn_in_specs(nc, 3) + [ANY] * (ng + nb),
        out_specs=[blk(lambda h, c: (jnp.minimum(c, nc - 1), h)), blk(late), blk(late)] + [ANY] * (ng + nb),
        out_shape=[jax.ShapeDtypeStruct((S, ATTN_WIDTH), BF16)] * 3 + recv_shapes + _gathered_shapes(blocks),
        scratch_shapes=[pltpu.VMEM((CHUNK, SLAB), F32)] * (6 + 5 * n) + exchange_sems + _gather_sems(nb),
        compiler_params=_cp("arbitrary", "arbitrary"),
    )(proj, proj, proj, proj, proj, dcat, attn, lse, *grads, *blocks)
    return out[:3], out[3:3 + ng], out[3 + ng:]


def _split_bf16(a):
    hi = a.astype(BF16)
    lo = (a - hi.astype(F32)).astype(BF16)
    return hi, lo


def _pooled(ug, halo_g, w, row0, tm):
    ext = jnp.concatenate([halo_g, ug], axis=0)
    hi, lo = _split_bf16(ext)
    rr = lax.broadcasted_iota(jnp.int32, (tm, tm + HALO), 0)
    cc = lax.broadcasted_iota(jnp.int32, (tm, tm + HALO), 1)
    back = rr + HALO - cc
    win = ((back >= 0) & (back < w)).astype(BF16)
    wsum = _dot(win, hi, NN) + _dot(win, lo, NN)
    rows = row0 + lax.broadcasted_iota(jnp.int32, (tm, 1), 0)
    inv = 1.0 / jnp.minimum(rows + 1, w).astype(F32)
    return wsum * inv - ug


def _pool_fwd(u, u_col, pool_w, pool_scale, *, name, tm=256):
    S, W = u.shape[0], POOL_WIDTH
    G = POOL_GROUP_DIM

    def body(u_ref, h_ref, w_ref, s_ref, o_ref):
        i = pl.program_id(0)
        uv = u_ref[...]
        halo = jnp.where(i > 0, h_ref[...], 0.0)
        sls = [slice(g * G, (g + 1) * G) for g in range(len(POOL_WINDOWS))]
        pooled = [_pooled(uv[:, sl], halo[:, sl], w, i * tm, tm) for sl, w in zip(sls, POOL_WINDOWS)]
        zs = [_dot(p.astype(BF16), w_ref[g].astype(BF16), NN) for g, p in enumerate(pooled)]
        for sl, z in zip(sls, zs):
            o_ref[:, sl] = (z * s_ref[:, sl]).astype(BF16)

    per = tm // HALO
    return pl.pallas_call(
        body, name=name, grid=(S // tm,),
        in_specs=[pl.BlockSpec((tm, W), lambda i: (i, u_col)),
                  pl.BlockSpec((HALO, W), lambda i: (jnp.maximum(i * per - 1, 0), u_col)),
                  pl.BlockSpec((len(POOL_WINDOWS), G, G), lambda i: (0, 0, 0)),
                  pl.BlockSpec((1, W), lambda i: (0, 0))],
        out_specs=pl.BlockSpec((tm, W), lambda i: (i, 0)),
        out_shape=jax.ShapeDtypeStruct((S, W), BF16),
        compiler_params=_cp("parallel"),
    )(u, u, pool_w, pool_scale)


def _pool_bwd(u, u_col, dy, dy_col, pool_w, pool_scale, *, name, tm=256):
    S, W = u.shape[0], POOL_WIDTH
    G = POOL_GROUP_DIM
    nt = S // tm

    def body(u_ref, h_ref, dy_ref, dyn_ref, w_ref, s_ref, du_ref, gw_ref, gs_ref):
        i = pl.program_id(0)

        @pl.when(i == 0)
        def _():
            gw_ref[...] = jnp.zeros_like(gw_ref)
            gs_ref[...] = jnp.zeros_like(gs_ref)

        uv = u_ref[...]
        halo = jnp.where(i > 0, h_ref[...], 0.0)
        dyv = dy_ref[...]
        dyn = jnp.where(i < nt - 1, dyn_ref[...], 0.0)
        rr = lax.broadcasted_iota(jnp.int32, (tm, tm + HALO), 0)
        cc = lax.broadcasted_iota(jnp.int32, (tm, tm + HALO), 1)
        rows_ext = i * tm + lax.broadcasted_iota(jnp.int32, (tm + HALO, 1), 0)
        groups = list(enumerate(POOL_WINDOWS))
        sls = [slice(g * G, (g + 1) * G) for g, _ in groups]
        wgs = [w_ref[g].astype(BF16) for g, _ in groups]
        pooled = [_pooled(uv[:, sl], halo[:, sl], w, i * tm, tm).astype(BF16) for sl, (_, w) in zip(sls, groups)]
        dzs = [dyv[:, sl] * s_ref[:, sl] for sl in sls]
        dz_ext = [jnp.concatenate([dz, dyn[:, sl] * s_ref[:, sl]], axis=0).astype(BF16) for dz, sl in zip(dzs, sls)]
        dp_ext = [_dot(d, wg, NT) for d, wg in zip(dz_ext, wgs)]
        zs = [_dot(p, wg, NN) for p, wg in zip(pooled, wgs)]
        for (g, w), sl, p, dz, z, dp in zip(groups, sls, pooled, dzs, zs, dp_ext):
            gw_ref[g] += _dot(p, dz.astype(BF16), TN)
            gs_ref[:, sl] += jnp.sum(dyv[:, sl] * z, axis=0, keepdims=True)
            inv_ext = 1.0 / jnp.minimum(rows_ext + 1, w).astype(F32)
            hi, lo = _split_bf16(dp * inv_ext)
            ahead = cc - rr
            win = ((ahead >= 0) & (ahead < w)).astype(BF16)
            du_ref[:, sl] = (_dot(win, hi, NN) + _dot(win, lo, NN) - dp[:tm]).astype(BF16)

    per = tm // HALO
    nh = S // HALO
    return pl.pallas_call(
        body, name=name, grid=(nt,),
        in_specs=[pl.BlockSpec((tm, W), lambda i: (i, u_col)),
                  pl.BlockSpec((HALO, W), lambda i: (jnp.maximum(i * per - 1, 0), u_col)),
                  pl.BlockSpec((tm, W), lambda i: (i, dy_col)),
                  pl.BlockSpec((HALO, W), lambda i: (jnp.minimum((i + 1) * per, nh - 1), dy_col)),
                  pl.BlockSpec((len(POOL_WINDOWS), G, G), lambda i: (0, 0, 0)),
                  pl.BlockSpec((1, W), lambda i: (0, 0))],
        out_specs=[pl.BlockSpec((tm, W), lambda i: (i, 0)),
                   pl.BlockSpec((len(POOL_WINDOWS), G, G), lambda i: (0, 0, 0)),
                   pl.BlockSpec((1, W), lambda i: (0, 0))],
        out_shape=[jax.ShapeDtypeStruct((S, W), BF16),
                   jax.ShapeDtypeStruct((len(POOL_WINDOWS), G, G), F32),
                   jax.ShapeDtypeStruct((1, W), F32)],
        compiler_params=_cp("arbitrary"),
    )(u, u, dy, dy, pool_w, pool_scale)


GELU_K0 = math.sqrt(2.0 / math.pi)
GELU_K1 = 0.044715


def _gelu_parts(x):
    x2 = x * x
    t = jnp.tanh(x * (GELU_K0 + (GELU_K0 * GELU_K1) * x2))
    hp = 0.5 + 0.5 * t
    gelu = x * hp
    dgelu = hp + (x * (hp * (1.0 - t))) * (GELU_K0 + (3.0 * GELU_K0 * GELU_K1) * x2)
    return gelu, dgelu


def _shifted(ext, halo):
    return (pltpu.roll(ext, 2, 0)[halo:], pltpu.roll(ext, 1, 0)[halo:], ext[halo:])


def _conv(sh, w, b):
    return b + (sh[0] * w[0:1] + sh[1] * w[1:2] + sh[2] * w[2:3])


F32_ROWS = 8


def _ffn_up_glu(h, w_up_t, conv_w, conv_b, *, name, tm=2048, tn=256, sub=256):
    S, K = h.shape
    F = D_FF
    nj = F // tn

    def body(h_ref, wg_ref, wv_ref, cwg_ref, cwv_ref, cbg_ref, cbv_ref,
             ug_ref, uv_ref, cg_ref, cv_ref, y_ref, carry):
        i = pl.program_id(0)
        j = pl.program_id(1)

        w_cat = jnp.concatenate([wg_ref[...], wv_ref[...]], axis=0)
        conv_w_b = ((cwg_ref[...], cbg_ref[...]), (cwv_ref[...], cbv_ref[...]))
        halo = [jnp.where(i > 0, carry[j, s], 0.0) for s in range(2)]
        u_next = _dot(h_ref[0:sub, :], w_cat, NT)
        for a in range(0, tm, sub):
            u16 = u_next.astype(BF16)
            if a + sub < tm:
                u_next = _dot(h_ref[a + sub:a + 2 * sub, :], w_cat, NT)
            ug_ref[a:a + sub, :] = u16[:, :tn]
            uv_ref[a:a + sub, :] = u16[:, tn:]
            c = []
            for s, (cw, cb) in enumerate(conv_w_b):
                u = u16[:, s * tn:(s + 1) * tn].astype(F32)
                ext = jnp.concatenate([halo[s], u], axis=0)
                c.append(_conv(_shifted(ext, F32_ROWS), cw, cb))
                halo[s] = u[sub - F32_ROWS:]
            cg_ref[a:a + sub, :] = c[0].astype(BF16)
            cv_ref[a:a + sub, :] = c[1].astype(BF16)
            gelu, _ = _gelu_parts(c[0])
            y_ref[a:a + sub, :] = (gelu * c[1]).astype(BF16)
        for s in range(2):
            carry[j, s] = halo[s]

    tile = pl.BlockSpec((tm, tn), lambda i, j: (i, j))
    vec = lambda rows, off: pl.BlockSpec((rows, tn), lambda i, j: (0, j + off))
    return pl.pallas_call(
        body, name=name, grid=(S // tm, nj),
        in_specs=[pl.BlockSpec((tm, K), lambda i, j: (i, 0)),
                  pl.BlockSpec((tn, K), lambda i, j: (j, 0)), pl.BlockSpec((tn, K), lambda i, j: (j + nj, 0)),
                  vec(3, 0), vec(3, nj), vec(1, 0), vec(1, nj)],
        out_specs=[tile] * 5,
        out_shape=[jax.ShapeDtypeStruct((S, F), BF16)] * 5,
        scratch_shapes=[pltpu.VMEM((nj, 2, F32_ROWS, tn), F32)],
        compiler_params=_cp("arbitrary", "arbitrary"),
    )(h, w_up_t, w_up_t, conv_w, conv_w, conv_b, conv_b)


def _ffn_glu_bwd(u_g, u_v, c_g, c_v, df, w_down, h, conv_w, *, name, tm=2048, tn=256, sub=256):
    S = u_g.shape[0]
    F = D_FF
    D = df.shape[1]
    nj = F // tn
    nt = S // tm

    def body(ug_ref, uv_ref, cg_ref, cgn_ref, cv_ref, cvn_ref, df_ref, dfn_ref, wd_ref, h_ref, wg_ref, wv_ref,
             dug_ref, duv_ref, gug_ref, guv_ref, gd_ref, gwg_ref, gwv_ref, gbg_ref, gbv_ref,
             acc_u, acc_d):
        i = pl.program_id(1)

        @pl.when(i == 0)
        def _():
            for r in (gwg_ref, gwv_ref, gbg_ref, gbv_ref, acc_u, acc_d):
                r[...] = jnp.zeros_like(r)

        wg, wv = wg_ref[...], wv_ref[...]
        wd = wd_ref[...]
        dfn = jnp.where(i < nt - 1, dfn_ref[...], jnp.zeros_like(dfn_ref))
        n_ext = sub + HALO

        def ahead(dc):
            return dc[:sub], pltpu.roll(dc, n_ext - 1, 0)[:sub], pltpu.roll(dc, n_ext - 2, 0)[:sub]

        def ext(ref, nxt, a):
            b = a + sub
            return jnp.concatenate([ref[a:b, :], ref[b:b + HALO, :] if b < tm else nxt], axis=0)

        dy_next = _dot(ext(df_ref, dfn, 0), wd, NT)
        for a in range(0, tm, sub):
            b = a + sub
            dy_ext = dy_next
            if b < tm:
                dy_next = _dot(ext(df_ref, dfn, b), wd, NT)
            cg = ext(cg_ref, cgn_ref[...], a).astype(F32)
            cv = ext(cv_ref, cvn_ref[...], a).astype(F32)
            df_sub = df_ref[a:b, :]
            gelu, dgelu = _gelu_parts(cg)
            dcs_g = ahead(dy_ext * cv * dgelu)
            dcs_v = ahead(dy_ext * gelu)
            du_g = (dcs_g[0] * wg[2:3] + dcs_g[1] * wg[1:2] + dcs_g[2] * wg[0:1]).astype(BF16)
            du_v = (dcs_v[0] * wv[2:3] + dcs_v[1] * wv[1:2] + dcs_v[2] * wv[0:1]).astype(BF16)
            dug_ref[a:b, :] = du_g
            duv_ref[a:b, :] = du_v
            acc_u[...] += _dot(jnp.concatenate([du_g, du_v], axis=1), h_ref[a:b, :], TN)
            acc_d[...] += _dot((gelu[:sub] * cv[:sub]).astype(BF16), df_sub, TN)
            for dcs, u_ref, gw_ref, gb_ref in ((dcs_g, ug_ref, gwg_ref, gbg_ref), (dcs_v, uv_ref, gwv_ref, gbv_ref)):
                u = u_ref[a:b, :].astype(F32)
                gb_ref[...] += jnp.sum(dcs[0], axis=0, keepdims=True)
                for k in range(3):
                    gw_ref[k:k + 1, :] += jnp.sum(dcs[2 - k] * u, axis=0, keepdims=True)

        @pl.when(i == nt - 1)
        def _():
            gug_ref[...] = acc_u[:tn, :].astype(BF16)
            guv_ref[...] = acc_u[tn:, :].astype(BF16)
            gd_ref[...] = acc_d[...].astype(BF16)

    per = tm // HALO
    nh = S // HALO
    hnext = lambda i: jnp.minimum((i + 1) * per, nh - 1)
    tile = pl.BlockSpec((tm, tn), lambda j, i: (i, j))
    hn = pl.BlockSpec((HALO, tn), lambda j, i: (hnext(i), j))
    vec = lambda rows, off: pl.BlockSpec((rows, tn), lambda j, i: (0, j + off))
    wide = pl.BlockSpec((tm, D), lambda j, i: (i, 0))
    wrow = pl.BlockSpec((tn, D), lambda j, i: (j, 0))
    return pl.pallas_call(
        body, name=name, grid=(nj, nt),
        in_specs=[tile, tile, tile, hn, tile, hn, wide, pl.BlockSpec((HALO, D), lambda j, i: (hnext(i), 0)),
                  wrow, wide, vec(3, 0), vec(3, nj)],
        out_specs=[tile, tile, wrow, wrow, wrow, vec(3, 0), vec(3, 0), vec(1, 0), vec(1, 0)],
        out_shape=[jax.ShapeDtypeStruct((S, F), BF16), jax.ShapeDtypeStruct((S, F), BF16),
                   jax.ShapeDtypeStruct((F, D), BF16), jax.ShapeDtypeStruct((F, D), BF16),
                   jax.ShapeDtypeStruct((F, D), BF16),
                   jax.ShapeDtypeStruct((3, F), F32), jax.ShapeDtypeStruct((3, F), F32),
                   jax.ShapeDtypeStruct((1, F), F32), jax.ShapeDtypeStruct((1, F), F32)],
        scratch_shapes=[pltpu.VMEM((2 * tn, D), F32), pltpu.VMEM((tn, D), F32)],
        compiler_params=_cp("parallel", "arbitrary"),
    )(u_g, u_v, c_g, c_g, c_v, c_v, df, df, w_down, h, conv_w, conv_w)


def _sum_partials(parts, *, name, tr):
    _, R, C = parts.shape

    def body(p_ref, o_ref):
        tot = p_ref[0].astype(F32)
        for j in range(1, N_DEV):
            tot = tot + p_ref[j].astype(F32)
        o_ref[...] = tot

    return pl.pallas_call(
        body, name=name, grid=(R // tr,),
        in_specs=[pl.BlockSpec((N_DEV, tr, C), lambda i: (0, i, 0))],
        out_specs=pl.BlockSpec((tr, C), lambda i: (i, 0)),
        out_shape=jax.ShapeDtypeStruct((R, C), F32),
        compiler_params=_cp("parallel"),
    )(parts)


def _adamw(w, g, m, v, *, name, tr):
    R, C = w.shape
    c1 = 1.0 - ADAM_B1 ** ADAM_STEP
    c2 = 1.0 - ADAM_B2 ** ADAM_STEP

    def body(w_ref, g_ref, m_ref, v_ref, d_ref, nm_ref, nv_ref):
        g = g_ref[...]
        nm = ADAM_B1 * m_ref[...] + (1.0 - ADAM_B1) * g
        nv = ADAM_B2 * v_ref[...] + (1.0 - ADAM_B2) * (g * g)
        d_ref[...] = -ADAM_LR * ((nm / c1) / (jnp.sqrt(nv / c2) + ADAM_EPS) + ADAM_WD * w_ref[...])
        nm_ref[...] = nm
        nv_ref[...] = nv

    spec = pl.BlockSpec((tr, C), lambda i: (i, 0))
    return pl.pallas_call(
        body, name=name, grid=(R // tr,), in_specs=[spec] * 4, out_specs=[spec] * 3,
        out_shape=[jax.ShapeDtypeStruct((R, C), F32)] * 3,
        compiler_params=_cp("parallel"),
    )(w, g, m, v)


def _mesh_pos():
    return lax.axis_index("x"), lax.axis_index("y"), lax.axis_index("c")


def _gather_phases(x_refs, out_refs, send_sems, recv_sems, local_sems):
    x, y, c = _mesh_pos()
    me, sibling = (x, y, c), (x, y, 1 - c)
    chips = [(1 - x, y), (x, 1 - y), (1 - x, 1 - y)]
    arrays = range(len(x_refs))

    def slot(a, px, py, pc):
        return out_refs[a].at[4 * px + 2 * py + pc]

    def copy(a, k, block, to, own=False):
        return pltpu.make_async_remote_copy(
            src_ref=x_refs[a] if own else slot(a, *block), dst_ref=slot(a, *block),
            send_sem=send_sems.at[a, k], recv_sem=recv_sems.at[a, k], device_id=to, device_id_type=MESH)

    mine = [pltpu.make_async_copy(x_refs[a], slot(a, *me), local_sems.at[a]) for a in arrays]
    first = [copy(a, 0, me, sibling, own=True) for a in arrays]
    first += [copy(a, 1 + j, me, (*chip, c), own=True) for j, chip in enumerate(chips) for a in arrays]
    passed = [[copy(a, 4 + j, (*chip, c), sibling) for a in arrays] for j, chip in enumerate(chips)]

    def start():
        for cp in mine + first:
            cp.start()

    def forward():
        for j, chip in enumerate(chips):
            for a in arrays:
                copy(a, 1 + j, (*chip, c), me).wait_recv()
                passed[j][a].start()

    def finish():
        for a in arrays:
            copy(a, 0, sibling, me).wait_recv()
            for j, chip in enumerate(chips):
                copy(a, 4 + j, (*chip, 1 - c), me).wait_recv()
        for cp in first + [cp for row in passed for cp in row]:
            cp.wait_send()
        for cp in mine:
            cp.wait()

    return start, forward, finish


def _gather_sems(n):
    return [pltpu.SemaphoreType.DMA((n, 7)), pltpu.SemaphoreType.DMA((n, 7)), pltpu.SemaphoreType.DMA((n,))]


def _gathered_shapes(blocks):
    return [jax.ShapeDtypeStruct((N_DEV,) + b.shape, b.dtype) for b in blocks]


def _all_reduce_small(block, gathered, *, name):
    r0, r1 = block.shape[0], gathered.shape[1]

    def body(x_ref, more_ref, all_ref, sum_ref, *sems):
        for phase in _gather_phases([x_ref], [all_ref], *sems):
            phase()
        for ref, rows in ((all_ref, slice(0, r0)), (more_ref, slice(r0, r0 + r1))):
            tot = ref[0]
            for j in range(1, N_DEV):
                tot = tot + ref[j]
            sum_ref[rows, :] = tot

    return pl.pallas_call(
        body, name=name, in_specs=[VMEM, VMEM], out_specs=[VMEM, VMEM],
        out_shape=[jax.ShapeDtypeStruct((N_DEV,) + block.shape, block.dtype),
                   jax.ShapeDtypeStruct((r0 + r1, block.shape[1]), block.dtype)],
        scratch_shapes=_gather_sems(1),
        compiler_params=pltpu.CompilerParams(vmem_limit_bytes=V7X_VMEM_LIMIT),
    )(block, gathered)[1]


def _exchange_phases(g_refs, r_refs, send_sems, recv_sems, local_sems):
    x, y, c = _mesh_pos()
    me = 4 * x + 2 * y + c
    owns, remote = [], []
    for k, (g_ref, r_ref) in enumerate(zip(g_refs, r_refs)):
        rows = g_ref.shape[0] // N_DEV
        owns.append(pltpu.make_async_copy(g_ref.at[pl.ds(me * rows, rows)], r_ref.at[me], local_sems.at[k]))
        for p in range(1, N_DEV):
            px, py, pc = x ^ (p >> 2), y ^ ((p >> 1) & 1), c ^ (p & 1)
            peer = 4 * px + 2 * py + pc
            link = dict(send_sem=send_sems.at[k, p], recv_sem=recv_sems.at[k, p],
                        device_id=(px, py, pc), device_id_type=MESH)
            src = g_ref.at[pl.ds(peer * rows, rows)]
            send = pltpu.make_async_remote_copy(src_ref=src, dst_ref=r_ref.at[me], **link)
            arrival = pltpu.make_async_remote_copy(src_ref=src, dst_ref=r_ref.at[peer], **link)
            remote.append((send, arrival))

    def start():
        for own in owns:
            own.start()
        for send, _ in remote:
            send.start()

    def finish():
        for _, arrival in remote:
            arrival.wait_recv()
        for send, _ in remote:
            send.wait_send()
        for own in owns:
            own.wait()

    return start, finish


def _exchange_buffers(grads):
    n = len(grads)
    shapes = [jax.ShapeDtypeStruct((N_DEV, g.shape[0] // N_DEV, g.shape[1]), g.dtype) for g in grads]
    sems = [pltpu.SemaphoreType.DMA((n, N_DEV)), pltpu.SemaphoreType.DMA((n, N_DEV)),
            pltpu.SemaphoreType.DMA((n,))]
    return shapes, sems


def _unpack_gathered(gathered):
    w_out, w_up_t, w_down = (g.reshape(-1, D_MODEL) for g in gathered[:3])
    width = 2 * D_FF // N_DEV
    conv_w = jnp.transpose(gathered[3][:, :3, :width], (1, 0, 2)).reshape(3, 2 * D_FF)
    return w_out, w_up_t, w_down, conv_w


def _rest_payload(w_out, w_up, w_down, conv_w):
    rows, cols = conv_w.shape
    conv_w = jnp.pad(conv_w, ((0, (-rows) % F32_ROWS), (0, (-cols) % LANES)))
    return [w_out.astype(BF16), w_up.T.astype(BF16), w_down.astype(BF16), conv_w]


def _device_step(x, target, g_mix_pre, w_in_t_block, rest_payload, pool_w, pool_scale, g_mix_post, g_ffn_pre,
                 conv_b, g_ffn_post):
    h1, w_in_t = _rms_norm_gather(x, g_mix_pre, w_in_t_block, name="rms_mix_pre")
    w_in_t = w_in_t.reshape(-1, D_MODEL)
    proj = _matmul(h1, w_in_t, trans_b=True, out_dtype=F32, tm=1024, tn=4 * ATTN_WIDTH, name="proj")
    attn, lse, attn16, gathered = _attn_fwd(proj, rest_payload, name="attn_fwd")
    w_out, w_up_t, w_down, conv_w = _unpack_gathered(gathered)
    pool = _pool_fwd(proj, 3, pool_w, pool_scale, name="pool_fwd")
    mixed, x2, h2 = _mix_out(attn16, pool, w_out, x, g_mix_post, g_ffn_pre, name="mix_out")
    u_g, u_v, c_g, c_v, y = _ffn_up_glu(h2, w_up_t, conv_w, conv_b, name="ffn_up_glu")
    df, d_out, loss_blk, gg_ffn_post = _ffn_out(y, w_down, x2, target, g_ffn_post, name="ffn_out")
    du_g, du_v, gw_up_g, gw_up_v, gw_down, gcw_g, gcw_v, gcb_g, gcb_v = _ffn_glu_bwd(
        u_g, u_v, c_g, c_v, df, w_down, h2, conv_w, name="ffn_glu_bwd")
    gw_up_t = jnp.concatenate([gw_up_g, gw_up_v], axis=0)
    dx2, gg_ffn_pre, dmixed, gg_mix_post = _dgrad_norm(
        [du_g, du_v], w_up_t, d_out, x2, g_ffn_pre, (mixed, g_mix_post), [], name="ffn_up_dgrad")
    gw_out = _matmul_tn([attn16, pool], dmixed, name="grad_w_out")
    dcat = _matmul(dmixed, w_out, trans_b=True, out_dtype=F32, tm=1024, tn=1024, name="mix_out_dgrad")
    d_pool_in, g_pool_w, g_pool_scale = _pool_bwd(proj, 3, dcat, 1, pool_w, pool_scale, name="pool_bwd")
    early = dict(g_mix_post=gg_mix_post, g_ffn_pre=gg_ffn_pre, g_ffn_post=gg_ffn_post, pool_scale=g_pool_scale,
                 conv_b=jnp.concatenate([gcb_g, gcb_v], axis=1), pool_w=g_pool_w)
    early_block = _pack_rows([early[k] for k in _SMALL[1:]] + [jnp.concatenate([gcw_g, gcw_v], axis=1), loss_blk])
    dqkv, (r_out, r_up_t, r_down), (small_gathered,) = _attn_bwd(
        proj, dcat, attn, lse, [gw_out, gw_up_t, gw_down], [early_block], name="attn_bwd")
    dproj = list(dqkv) + [d_pool_in]
    gw_in_t = _matmul_tn(dproj, h1, name="grad_w_in")
    grad_x, gg_mix_pre, (r_in_t,) = _dgrad_norm(dproj, w_in_t, dx2, x, g_mix_pre, None, [gw_in_t], name="proj_dgrad")
    received = (r_in_t, r_out, r_up_t, r_down)
    return grad_x, received, gg_mix_pre, small_gathered


_SMALL = ("g_mix_pre", "g_mix_post", "g_ffn_pre", "g_ffn_post", "pool_scale", "conv_b", "pool_w")
LANES = 128


def _pack_rows(arrays):
    parts = []
    for a in arrays:
        a2 = a.reshape(-1, LANES)
        parts.append(jnp.pad(a2, ((0, (-a2.shape[0]) % 8), (0, 0))))
    return jnp.concatenate(parts, axis=0)


def _unpack_rows(packed, shapes):
    out, row = [], 0
    for shape in shapes:
        rows = math.prod(shape) // LANES
        out.append(packed[row:row + rows].reshape(shape))
        row += -(-rows // 8) * 8
    return out


def kernel(x, g_mix_pre, w_in, pool_w, pool_scale, w_out, g_mix_post, g_ffn_pre, w_up, conv_w, conv_b, w_down, g_ffn_post, loss_target, m_g_mix_pre, m_w_in, m_pool_w, m_pool_scale, m_w_out, m_g_mix_post, m_g_ffn_pre, m_w_up, m_conv_w, m_conv_b, m_w_down, m_g_ffn_post, v_g_mix_pre, v_w_in, v_pool_w, v_pool_scale, v_w_out, v_g_mix_post, v_g_ffn_pre, v_w_up, v_conv_w, v_conv_b, v_w_down, v_g_ffn_post):
    me = 4 * lax.axis_index("x") + 2 * lax.axis_index("y") + lax.axis_index("c")
    grad_x, recv, gg_mix_pre, small_gathered = _device_step(
        x[0], loss_target[0], g_mix_pre, w_in[0].T.astype(BF16),
        _rest_payload(w_out[0], w_up[0], w_down[0], conv_w[0]),
        pool_w[0], pool_scale, g_mix_post, g_ffn_pre, conv_b, g_ffn_post)

    g_in_t, g_out, g_up_t, g_down = (
        _sum_partials(r, name=f"sum_partials_{k}", tr=r.shape[1] // 2) for k, r in enumerate(recv))
    grads = {"w_in": g_in_t.T, "w_out": g_out, "w_up": g_up_t.T, "w_down": g_down}

    given = dict(g_mix_pre=g_mix_pre, g_mix_post=g_mix_post, g_ffn_pre=g_ffn_pre, g_ffn_post=g_ffn_post,
                 pool_scale=pool_scale, conv_b=conv_b, pool_w=pool_w)
    small_shapes = [given[k].shape for k in _SMALL]
    total = _all_reduce_small(_pack_rows([gg_mix_pre]), small_gathered, name="all_reduce_small")
    *small_grads, g_conv_w_all, loss_all = _unpack_rows(total, small_shapes + [(3, 2 * D_FF), (8, LANES)])
    loss = loss_all[0, 0]
    grads.update(zip(_SMALL, small_grads))
    width = 2 * D_FF // N_DEV
    grads["conv_w"] = lax.dynamic_slice_in_dim(g_conv_w_all, me * width, width, axis=1)[None]

    weights = dict(g_mix_pre=g_mix_pre, w_in=w_in, pool_w=pool_w, pool_scale=pool_scale, w_out=w_out,
                   g_mix_post=g_mix_post, g_ffn_pre=g_ffn_pre, w_up=w_up, conv_w=conv_w, conv_b=conv_b,
                   w_down=w_down, g_ffn_post=g_ffn_post)
    m_in = dict(g_mix_pre=m_g_mix_pre, w_in=m_w_in, pool_w=m_pool_w, pool_scale=m_pool_scale, w_out=m_w_out,
                g_mix_post=m_g_mix_post, g_ffn_pre=m_g_ffn_pre, w_up=m_w_up, conv_w=m_conv_w, conv_b=m_conv_b,
                w_down=m_w_down, g_ffn_post=m_g_ffn_post)
    v_in = dict(g_mix_pre=v_g_mix_pre, w_in=v_w_in, pool_w=v_pool_w, pool_scale=v_pool_scale, w_out=v_w_out,
                g_mix_post=v_g_mix_post, g_ffn_pre=v_g_ffn_pre, w_up=v_w_up, conv_w=v_conv_w, conv_b=v_conv_b,
                w_down=v_w_down, g_ffn_post=v_g_ffn_post)
    delta, new_m, new_v = {}, {}, {}
    for k in ("w_in", "w_out", "w_up", "w_down"):
        g = grads[k]
        d, nm, nv = _adamw(weights[k][0], g, m_in[k][0], v_in[k][0], name=f"adamw_{k}", tr=g.shape[0] // 2)
        grads[k], delta[k], new_m[k], new_v[k] = g[None], d[None], nm[None], nv[None]
    d, nm, nv = _adamw(weights["conv_w"][0], grads["conv_w"][0], m_in["conv_w"][0], v_in["conv_w"][0],
                       name="adamw_conv_w", tr=3)
    delta["conv_w"], new_m["conv_w"], new_v["conv_w"] = d[None], nm[None], nv[None]
    packed_w = _pack_rows([weights[k] for k in _SMALL])
    small_rows = packed_w.shape[0]
    d, nm, nv = _adamw(packed_w, total[:small_rows], _pack_rows([m_in[k] for k in _SMALL]),
                       _pack_rows([v_in[k] for k in _SMALL]), name="adamw_small", tr=small_rows)
    for k, dk, mk, vk in zip(_SMALL, _unpack_rows(d, small_shapes), _unpack_rows(nm, small_shapes),
                             _unpack_rows(nv, small_shapes)):
        delta[k], new_m[k], new_v[k] = dk, mk, vk

    order = ("g_mix_pre", "w_in", "pool_w", "pool_scale", "w_out", "g_mix_post", "g_ffn_pre", "w_up",
             "conv_w", "conv_b", "w_down", "g_ffn_post")
    return (loss, grad_x[None], *[grads[k] for k in order], *[delta[k] for k in order],
            *[new_m[k] for k in order], *[new_v[k] for k in order])
```

```python
import functools
import math

import jax
import jax.numpy as jnp
from jax import lax
from jax.experimental import pallas as pl
from jax.experimental.pallas import tpu as pltpu

F32 = jnp.float32
BF16 = jnp.bfloat16

D_MODEL = 1024
N_HEADS = 8
HEAD_DIM = 64
ATTN_WIDTH = N_HEADS * HEAD_DIM
DILATIONS = (1, 4, 16)
BLOCK = 128
POOL_WIDTH = 512
POOL_WINDOWS = (2, 4, 8, 16)
POOL_GROUP_DIM = 128
D_FF = 2816
EPS = 1e-6
NEG_INF = -1e30
SCALE = HEAD_DIM ** -0.5

ADAM_LR = 0.001
ADAM_B1 = 0.9
ADAM_B2 = 0.999
ADAM_EPS = 1e-08
ADAM_WD = 0.01
ADAM_STEP = 10

N_DEV = 8
HALO = 16
V7X_VMEM_LIMIT = 56 * 1024 * 1024

MESH = pl.DeviceIdType.MESH
ANY = pl.BlockSpec(memory_space=pl.ANY)
VMEM = pl.BlockSpec(memory_space=pltpu.VMEM)

NT = (((1,), (1,)), ((), ()))
NN = (((1,), (0,)), ((), ()))
TN = (((0,), (0,)), ((), ()))


def _cp(*sem):
    return pltpu.CompilerParams(dimension_semantics=sem, vmem_limit_bytes=V7X_VMEM_LIMIT)


def _dot(a, b, dn):
    return lax.dot_general(a, b, dn, preferred_element_type=F32)


def _rms_bwd(xin, g, dy):
    r = lax.rsqrt(jnp.mean(xin * xin, axis=-1, keepdims=True) + EPS)
    xh = xin * r
    gdy = g * dy
    dx = r * (gdy - xh * jnp.mean(gdy * xh, axis=-1, keepdims=True))
    dg = jnp.sum(dy * xh, axis=0, keepdims=True)
    return dx, dg


def _rms_norm_gather(x, g, block, *, name, tm=1024):
    S, D = x.shape
    nt = S // tm

    def body(x_ref, g_ref, blk_ref, o_ref, all_ref, *sems):
        i = pl.program_id(0)
        start, forward, finish = _gather_phases([blk_ref], [all_ref], *sems)
        pl.when(i == 0)(start)
        xv = x_ref[...]
        r = lax.rsqrt(jnp.mean(xv * xv, axis=-1, keepdims=True) + EPS)
        o_ref[...] = (xv * r * g_ref[...]).astype(BF16)
        pl.when(i == nt - 1)(forward)
        pl.when(i == nt - 1)(finish)

    return pl.pallas_call(
        body, name=name, grid=(nt,),
        in_specs=[pl.BlockSpec((tm, D), lambda i: (i, 0)), pl.BlockSpec((1, D), lambda i: (0, 0)), ANY],
        out_specs=[pl.BlockSpec((tm, D), lambda i: (i, 0)), ANY],
        out_shape=[jax.ShapeDtypeStruct((S, D), BF16)] + _gathered_shapes([block]),
        scratch_shapes=_gather_sems(1),
        compiler_params=_cp("arbitrary"),
    )(x, g, block)


def _matmul(a, b, *, trans_b, out_dtype, tm, tn, name):
    M, K = a.shape
    N = b.shape[0] if trans_b else b.shape[1]
    dn = NT if trans_b else NN

    def body(a_ref, b_ref, o_ref):
        o_ref[...] = _dot(a_ref[...], b_ref[...], dn).astype(out_dtype)

    b_spec = (pl.BlockSpec((tn, K), lambda i, j: (j, 0)) if trans_b
              else pl.BlockSpec((K, tn), lambda i, j: (0, j)))
    return pl.pallas_call(
        body, name=name, grid=(M // tm, N // tn),
        in_specs=[pl.BlockSpec((tm, K), lambda i, j: (i, 0)), b_spec],
        out_specs=pl.BlockSpec((tm, tn), lambda i, j: (i, j)),
        out_shape=jax.ShapeDtypeStruct((M, N), out_dtype),
        compiler_params=_cp("parallel", "parallel"),
    )(a, b)


def _matmul_tn(a_list, b, *, name, ts=2048):
    S, Ka = a_list[0].shape
    na = len(a_list)
    Nb = b.shape[1]
    ns = S // ts

    def body(*refs):
        a_refs, b_ref, o_ref, acc = refs[:na], refs[na], refs[na + 1], refs[na + 2]
        s = pl.program_id(0)

        @pl.when(s == 0)
        def _():
            acc[...] = jnp.zeros_like(acc)

        acc[...] += _dot(jnp.concatenate([r[...] for r in a_refs], axis=1), b_ref[...], TN)

        @pl.when(s == ns - 1)
        def _():
            o_ref[...] = acc[...].astype(BF16)

    return pl.pallas_call(
        body, name=name, grid=(ns,),
        in_specs=[pl.BlockSpec((ts, Ka), lambda s: (s, 0))] * na + [pl.BlockSpec((ts, Nb), lambda s: (s, 0))],
        out_specs=pl.BlockSpec((na * Ka, Nb), lambda s: (0, 0)),
        out_shape=jax.ShapeDtypeStruct((na * Ka, Nb), BF16),
        scratch_shapes=[pltpu.VMEM((na * Ka, Nb), F32)],
        compiler_params=_cp("arbitrary"),
    )(*a_list, b)


def _mix_out(attn, pool, w_out, x, g_post, g_next, *, name, tm=512):
    S, K = attn.shape
    D = w_out.shape[1]

    def body(a_ref, p_ref, w_ref, x_ref, gp_ref, gn_ref, mixed_ref, x2_ref, h2_ref):
        mixed = _dot(a_ref[...], w_ref[:K, :], NN) + _dot(p_ref[...], w_ref[K:, :], NN)
        r = lax.rsqrt(jnp.mean(mixed * mixed, axis=-1, keepdims=True) + EPS)
        x2 = x_ref[...] + mixed * r * gp_ref[...]
        r2 = lax.rsqrt(jnp.mean(x2 * x2, axis=-1, keepdims=True) + EPS)
        mixed_ref[...] = mixed
        x2_ref[...] = x2
        h2_ref[...] = (x2 * r2 * gn_ref[...]).astype(BF16)

    row = lambda i: (i, 0)
    fix = lambda i: (0, 0)
    return pl.pallas_call(
        body, name=name, grid=(S // tm,),
        in_specs=[pl.BlockSpec((tm, K), row), pl.BlockSpec((tm, K), row), pl.BlockSpec((2 * K, D), fix),
                  pl.BlockSpec((tm, D), row), pl.BlockSpec((1, D), fix), pl.BlockSpec((1, D), fix)],
        out_specs=[pl.BlockSpec((tm, D), row)] * 3,
        out_shape=[jax.ShapeDtypeStruct((S, D), F32), jax.ShapeDtypeStruct((S, D), F32),
                   jax.ShapeDtypeStruct((S, D), BF16)],
        compiler_params=_cp("parallel"),
    )(attn, pool, w_out, x, g_post, g_next)


def _ffn_out(y, w_down, x2, target, g_post, *, name, tm=512, sub=256):
    S, K = y.shape
    D = w_down.shape[1]

    def body(y_ref, w_ref, x2_ref, t_ref, g_ref, df_ref, dout_ref, loss_ref, gg_ref):
        i = pl.program_id(0)

        @pl.when(i == 0)
        def _():
            loss_ref[...] = jnp.zeros_like(loss_ref)
            gg_ref[...] = jnp.zeros_like(gg_ref)

        g = g_ref[...]
        w = w_ref[...]
        f_next = _dot(y_ref[0:sub, :], w, NN)
        for a in range(0, tm, sub):
            rows = slice(a, a + sub)
            f = f_next
            if a + sub < tm:
                f_next = _dot(y_ref[a + sub:a + 2 * sub, :], w, NN)
            r = lax.rsqrt(jnp.mean(f * f, axis=-1, keepdims=True) + EPS)
            out = x2_ref[rows, :] + f * r * g
            err = out - t_ref[rows, :]
            dy = err * (1.0 / D)
            df, dg = _rms_bwd(f, g, dy)
            df_ref[rows, :] = df.astype(BF16)
            dout_ref[rows, :] = dy
            gg_ref[...] += dg
            loss_ref[...] += 0.5 * jnp.sum(jnp.mean(err * err, axis=-1, keepdims=True))

    row = lambda i: (i, 0)
    fix = lambda i: (0, 0)
    return pl.pallas_call(
        body, name=name, grid=(S // tm,),
        in_specs=[pl.BlockSpec((tm, K), row), pl.BlockSpec((K, D), fix), pl.BlockSpec((tm, D), row),
                  pl.BlockSpec((tm, D), row), pl.BlockSpec((1, D), fix)],
        out_specs=[pl.BlockSpec((tm, D), row), pl.BlockSpec((tm, D), row),
                   pl.BlockSpec((8, 128), fix), pl.BlockSpec((1, D), fix)],
        out_shape=[jax.ShapeDtypeStruct((S, D), BF16), jax.ShapeDtypeStruct((S, D), F32),
                   jax.ShapeDtypeStruct((8, 128), F32), jax.ShapeDtypeStruct((1, D), F32)],
        compiler_params=_cp("arbitrary"),
    )(y, w_down, x2, target, g_post)


def _dgrad_norm(a_list, w, resid, xin, g, second, exchange, *, name, tm=512, sub=256):
    S, Kp = a_list[0].shape
    na = len(a_list)
    D = w.shape[1]
    nt = S // tm
    two = second is not None
    ng = len(exchange)
    recv_shapes, exchange_sems = _exchange_buffers(exchange)

    def body(*refs):
        a_refs = refs[:na]
        w_ref, r_ref, x_ref, g_ref = refs[na:na + 4]
        pos = na + 4
        if two:
            x2_ref, g2_ref = refs[pos:pos + 2]
            pos += 2
        g_refs = refs[pos:pos + ng]
        pos += ng
        dx_ref, gg_ref = refs[pos:pos + 2]
        pos += 2
        if two:
            d2_ref, gg2_ref = refs[pos:pos + 2]
            pos += 2
        r_refs = refs[pos:pos + ng]
        pos += ng
        i = pl.program_id(0)
        if ng:
            start, finish = _exchange_phases(g_refs, r_refs, *refs[pos:])
            pl.when(i == 0)(start)

        @pl.when(i == 0)
        def _():
            gg_ref[...] = jnp.zeros_like(gg_ref)
            if two:
                gg2_ref[...] = jnp.zeros_like(gg2_ref)

        def dh_of(a):
            return functools.reduce(jnp.add, [_dot(a_refs[q][a:a + sub, :], w_ref[q * Kp:(q + 1) * Kp, :], NN)
                                              for q in range(na)])

        dh_next = dh_of(0)
        for a in range(0, tm, sub):
            rows = slice(a, a + sub)
            dh = dh_next
            if a + sub < tm:
                dh_next = dh_of(a + sub)
            d1, dg1 = _rms_bwd(x_ref[rows, :], g_ref[...], dh)
            dx = r_ref[rows, :] + d1
            dx_ref[rows, :] = dx
            gg_ref[...] += dg1
            if two:
                d2, dg2 = _rms_bwd(x2_ref[rows, :], g2_ref[...], dx)
                d2_ref[rows, :] = d2.astype(BF16)
                gg2_ref[...] += dg2
        if ng:
            pl.when(i == nt - 1)(finish)

    row = lambda i: (i, 0)
    fix = lambda i: (0, 0)
    in_specs = [pl.BlockSpec((tm, Kp), row)] * na + [
        pl.BlockSpec((na * Kp, D), fix, pipeline_mode=pl.Buffered(1)), pl.BlockSpec((tm, D), row),
        pl.BlockSpec((tm, D), row), pl.BlockSpec((1, D), fix)]
    args = list(a_list) + [w, resid, xin, g]
    out_specs = [pl.BlockSpec((tm, D), row), pl.BlockSpec((1, D), fix)]
    out_shape = [jax.ShapeDtypeStruct((S, D), F32), jax.ShapeDtypeStruct((1, D), F32)]
    if two:
        in_specs += [pl.BlockSpec((tm, D), row), pl.BlockSpec((1, D), fix)]
        args += list(second)
        out_specs += [pl.BlockSpec((tm, D), row), pl.BlockSpec((1, D), fix)]
        out_shape += [jax.ShapeDtypeStruct((S, D), BF16), jax.ShapeDtypeStruct((1, D), F32)]
    n_plain = len(out_shape)
    out = pl.pallas_call(
        body, name=name, grid=(nt,), in_specs=in_specs + [ANY] * ng, out_specs=out_specs + [ANY] * ng,
        out_shape=out_shape + recv_shapes, scratch_shapes=exchange_sems if ng else [],
        compiler_params=_cp("arbitrary"),
    )(*args, *exchange)
    return (*out[:n_plain], out[n_plain:]) if ng else out


def _band_mask(first_block):
    qi = lax.broadcasted_iota(jnp.int32, (BLOCK, 2 * BLOCK), 0)
    ki = lax.broadcasted_iota(jnp.int32, (BLOCK, 2 * BLOCK), 1)
    first_key = jnp.where(first_block, BLOCK, 0)
    return (ki >= qi) & (ki <= qi + BLOCK) & (ki >= first_key)


def _lane_masks():
    lane = lax.broadcasted_iota(jnp.int32, (1, 2 * HEAD_DIM), 1)
    return (lane < HEAD_DIM, lane >= HEAD_DIM)


CHUNK = BLOCK * max(DILATIONS)
SLAB = 2 * HEAD_DIM
N_SLABS = ATTN_WIDTH // SLAB


def _unit_rows(d, b):
    def rows(r):
        start = r + BLOCK * d * b
        return pl.ds(start, BLOCK, stride=d) if d > 1 else pl.ds(start, BLOCK)
    return rows


def _attn_units():
    for p, d in enumerate(DILATIONS):
        nbc = CHUNK // (BLOCK * d)
        for b in range(nbc):
            for r in range(d):
                yield p, d, b, r, nbc


def _attn_in_specs(nc, n_cur):
    prev = lambda c: jnp.maximum(jnp.minimum(c, nc - 1) - 1, 0)
    cur = lambda c: jnp.minimum(c, nc - 1)
    blk = lambda f: pl.BlockSpec((CHUNK, SLAB), f)
    specs = [blk(lambda h, c: (cur(c), h)),
             blk(lambda h, c: (prev(c), N_SLABS + h)), blk(lambda h, c: (cur(c), N_SLABS + h)),
             blk(lambda h, c: (prev(c), 2 * N_SLABS + h)), blk(lambda h, c: (cur(c), 2 * N_SLABS + h))]
    return specs + [blk(lambda h, c: (cur(c), h))] * n_cur


def _attn_fwd(proj, payload, *, name):
    S = proj.shape[0]
    nc = S // CHUNK
    n = len(DILATIONS)
    npay = len(payload)
    n_steps = N_SLABS * nc

    def body(*refs):
        q_ref, kp_ref, kc_ref, vp_ref, vc_ref = refs[:5]
        pay_refs = refs[5:5 + npay]
        attn_ref, lse_ref, attn16_ref = refs[5 + npay:8 + npay]
        all_refs = refs[8 + npay:8 + 2 * npay]
        scr = refs[8 + 2 * npay:]
        o_scr, l_scr = scr[:n], scr[n:2 * n]
        start, forward, finish = _gather_phases(pay_refs, all_refs, *scr[2 * n:])
        step = pl.program_id(0) * nc + pl.program_id(1)
        pl.when(step == 0)(start)
        c = pl.program_id(1)
        lms = _lane_masks()
        plain, first = (jnp.tile(_band_mask(f), (2, 1)) for f in (False, c == 0))
        def scores(unit):
            p, d, b, r, nbc = unit
            rows = _unit_rows(d, b)(r)
            prow = _unit_rows(d, (b - 1) % nbc)(r)
            kpr, vpr = (kc_ref, vc_ref) if b > 0 else (kp_ref, vp_ref)
            q = q_ref[rows, :].astype(BF16)
            kcat = jnp.concatenate([kpr[prow, :], kc_ref[rows, :]], axis=0).astype(BF16)
            vcat = jnp.concatenate([vpr[prow, :], vc_ref[rows, :]], axis=0).astype(BF16)
            q2 = jnp.concatenate([jnp.where(lm, q, jnp.zeros_like(q)) for lm in lms], axis=0) * SCALE
            return p, rows, plain if b > 0 else first, vcat, _dot(q2, kcat, NT)

        units = list(_attn_units())
        nxt = scores(units[0])
        for k in range(len(units)):
            p, rows, mask2, vcat, s = nxt
            if k + 1 < len(units):
                nxt = scores(units[k + 1])
            s = jnp.where(mask2, s, NEG_INF)
            m = jnp.max(s, axis=-1, keepdims=True)
            e = jnp.exp(s - m)
            l = jnp.sum(e, axis=-1, keepdims=True)
            o2 = _dot(e.astype(BF16), vcat, NN) / l
            lse2 = m + jnp.log(l)
            o_scr[p][rows, :] = jnp.where(lms[0], o2[:BLOCK], o2[BLOCK:])
            l_scr[p][rows, :] = jnp.where(lms[0], lse2[:BLOCK], lse2[BLOCK:])
        ls = [l_scr[p][...] for p in range(n)]
        top = functools.reduce(jnp.maximum, ls)
        es = [jnp.exp(l - top) for l in ls]
        den = functools.reduce(jnp.add, es)
        num = functools.reduce(jnp.add, [e * o_scr[p][...] for p, e in enumerate(es)])
        attn = num / den
        attn_ref[...] = attn
        attn16_ref[...] = attn.astype(BF16)
        lse_ref[...] = top + jnp.log(den)
        pl.when(step == (2 * n_steps) // 3)(forward)
        pl.when(step == n_steps - 1)(finish)

    out = pl.pallas_call(
        body, name=name, grid=(N_SLABS, nc), in_specs=_attn_in_specs(nc, 0) + [ANY] * npay,
        out_specs=[pl.BlockSpec((CHUNK, SLAB), lambda h, c: (c, h))] * 3 + [ANY] * npay,
        out_shape=[jax.ShapeDtypeStruct((S, ATTN_WIDTH), F32)] * 2 + [jax.ShapeDtypeStruct((S, ATTN_WIDTH), BF16)]
        + _gathered_shapes(payload),
        scratch_shapes=[pltpu.VMEM((CHUNK, SLAB), F32)] * (2 * n) + _gather_sems(npay),
        compiler_params=_cp("arbitrary", "arbitrary"),
    )(proj, proj, proj, proj, proj, *payload)
    return (*out[:3], out[3:])


def _attn_bwd(proj, dcat, attn, lse, grads, blocks, *, name):
    S = proj.shape[0]
    nc = S // CHUNK
    ng, nb = len(grads), len(blocks)
    n = len(DILATIONS)
    n_steps = N_SLABS * (nc + 1)
    recv_shapes, exchange_sems = _exchange_buffers(grads)

    def body(*refs):
        q_ref, kp_ref, kc_ref, vp_ref, vc_ref, do_ref, o_ref, lse_ref = refs[:8]
        g_refs, b_refs = refs[8:8 + ng], refs[8 + ng:8 + ng + nb]
        outs = refs[8 + ng + nb:]
        dq_ref, dk_ref, dv_ref = outs[:3]
        r_refs, all_refs = outs[3:3 + ng], outs[3 + ng:3 + ng + nb]
        scr = outs[3 + ng + nb:]
        dk_prev, dv_prev = scr[:2]
        delta_h, lse_h = scr[2:4], scr[4:6]
        dq_p, dk_own, dk_back, dv_own, dv_back = (scr[6 + n * k:6 + n * (k + 1)] for k in range(5))
        start, finish = _exchange_phases(g_refs, r_refs, *scr[6 + 5 * n:9 + 5 * n])
        gather_start, gather_forward, gather_finish = _gather_phases(b_refs, all_refs, *scr[9 + 5 * n:])
        c = pl.program_id(1)
        step = pl.program_id(0) * (nc + 1) + c

        @pl.when(step == 0)
        def _():
            gather_start()
            start()

        @pl.when(c == 0)
        def _():
            dk_prev[...] = jnp.zeros_like(dk_prev)
            dv_prev[...] = jnp.zeros_like(dv_prev)

        @pl.when(c < nc)
        def _():
            lms = _lane_masks()
            plain, first = (jnp.tile(_band_mask(f), (2, 1)) for f in (False, c == 0))
            prod = do_ref[...] * o_ref[...]
            lse = lse_ref[...]
            lse_other = pltpu.roll(lse, HEAD_DIM, 1)
            for h, lm in enumerate(lms):
                delta = jnp.sum(jnp.where(lm, prod, 0.0), axis=-1, keepdims=True)
                delta_h[h][...] = jnp.broadcast_to(delta, (CHUNK, SLAB))
                lse_h[h][...] = jnp.where(lm, lse, lse_other)
            wide = lambda refs, rows: jnp.tile(jnp.concatenate([r[rows, :] for r in refs], axis=0), (1, 2))
            stack = lambda f: jnp.concatenate([f(lm) for lm in lms], axis=0)

            def scores(unit):
                p, d, b, r, nbc = unit
                rows = _unit_rows(d, b)(r)
                prow = _unit_rows(d, (b - 1) % nbc)(r)
                kpr, vpr = (kc_ref, vc_ref) if b > 0 else (kp_ref, vp_ref)
                q = q_ref[rows, :].astype(BF16)
                kcat = jnp.concatenate([kpr[prow, :], kc_ref[rows, :]], axis=0).astype(BF16)
                vcat = jnp.concatenate([vpr[prow, :], vc_ref[rows, :]], axis=0).astype(BF16)
                do = do_ref[rows, :]
                q2 = stack(lambda lm: jnp.where(lm, q, jnp.zeros_like(q))) * SCALE
                do2 = stack(lambda lm: jnp.where(lm, do, 0.0)).astype(BF16)
                return dict(p=p, rows=rows, prow=prow, mask2=plain if b > 0 else first, kcat=kcat, q2=q2, do2=do2,
                            s=_dot(q2, kcat, NT), dp=_dot(do2, vcat, NT))

            units = list(_attn_units())
            nxt = scores(units[0])
            for k in range(len(units)):
                u = nxt
                if k + 1 < len(units):
                    nxt = scores(units[k + 1])
                p, rows, prow, kcat = u["p"], u["rows"], u["prow"], u["kcat"]
                e = jnp.where(u["mask2"], jnp.exp(u["s"] - wide(lse_h, rows)), 0.0)
                ds = (e * (u["dp"] - wide(delta_h, rows))).astype(BF16)
                dq2 = _dot(ds, kcat, NN) * SCALE
                dq = jnp.where(lms[0], dq2[:BLOCK], dq2[BLOCK:])
                dkc = _dot(ds, u["q2"], TN)
                dvc = _dot(e.astype(BF16), u["do2"], TN)
                dq_p[p][rows, :] = dq
                dk_own[p][rows, :] = dkc[BLOCK:]
                dv_own[p][rows, :] = dvc[BLOCK:]
                dk_back[p][prow, :] = dkc[:BLOCK]
                dv_back[p][prow, :] = dvc[:BLOCK]
            dq_ref[...] = functools.reduce(jnp.add, [r[...] for r in dq_p]).astype(BF16)
            for prev, own, back, out_ref in ((dk_prev, dk_own, dk_back, dk_ref), (dv_prev, dv_own, dv_back, dv_ref)):
                for p, d in enumerate(DILATIONS):
                    tail = CHUNK - BLOCK * d
                    prev[tail:, :] += back[p][tail:, :]
                out_ref[...] = prev[...].astype(BF16)
                prev[...] = functools.reduce(jnp.add, [r[...] for r in own])
                for p, d in enumerate(DILATIONS):
                    tail = CHUNK - BLOCK * d
                    if tail:
                        prev[:tail, :] += back[p][:tail, :]

        @pl.when(c == nc)
        def _():
            dk_ref[...] = dk_prev[...].astype(BF16)
            dv_ref[...] = dv_prev[...].astype(BF16)

        pl.when(step == (2 * n_steps) // 3)(gather_forward)

        @pl.when(step == n_steps - 1)
        def _():
            gather_finish()
            finish()

    blk = lambda f: pl.BlockSpec((CHUNK, SLAB), f)
    late = lambda h, c: (jnp.maximum(c - 1, 0), h)
    out = pl.pallas_call(
        body, name=name, grid=(N_SLABS, nc + 1), in_specs=_attn_in_specs(nc, 3) + [ANY] * (ng + nb),
        out_specs=[blk(lambda h, c: (jnp.minimum(c, nc - 1), h)), blk(late), blk(late)] + [ANY] * (ng + nb),
        out_shape=[jax.ShapeDtypeStruct((S, ATTN_WIDTH), BF16)] * 3 + recv_shapes + _gathered_shapes(blocks),
        scratch_shapes=[pltpu.VMEM((CHUNK, SLAB), F32)] * (6 + 5 * n) + exchange_sems + _gather_sems(nb),
        compiler_params=_cp("arbitrary", "arbitrary"),
    )(proj, proj, proj, proj, proj, dcat, attn, lse, *grads, *blocks)
    return out[:3], out[3:3 + ng], out[3 + ng:]


def _split_bf16(a):
    hi = a.astype(BF16)
    lo = (a - hi.astype(F32)).astype(BF16)
    return hi, lo


def _pooled(ug, halo_g, w, row0, tm):
    ext = jnp.concatenate([halo_g, ug], axis=0)
    hi, lo = _split_bf16(ext)
    rr = lax.broadcasted_iota(jnp.int32, (tm, tm + HALO), 0)
    cc = lax.broadcasted_iota(jnp.int32, (tm, tm + HALO), 1)
    back = rr + HALO - cc
    win = ((back >= 0) & (back < w)).astype(BF16)
    wsum = _dot(win, hi, NN) + _dot(win, lo, NN)
    rows = row0 + lax.broadcasted_iota(jnp.int32, (tm, 1), 0)
    inv = 1.0 / jnp.minimum(rows + 1, w).astype(F32)
    return wsum * inv - ug


def _pool_fwd(u, u_col, pool_w, pool_scale, *, name, tm=256):
    S, W = u.shape[0], POOL_WIDTH
    G = POOL_GROUP_DIM

    def body(u_ref, h_ref, w_ref, s_ref, o_ref):
        i = pl.program_id(0)
        uv = u_ref[...]
        halo = jnp.where(i > 0, h_ref[...], 0.0)
        sls = [slice(g * G, (g + 1) * G) for g in range(len(POOL_WINDOWS))]
        pooled = [_pooled(uv[:, sl], halo[:, sl], w, i * tm, tm) for sl, w in zip(sls, POOL_WINDOWS)]
        zs = [_dot(p.astype(BF16), w_ref[g].astype(BF16), NN) for g, p in enumerate(pooled)]
        for sl, z in zip(sls, zs):
            o_ref[:, sl] = (z * s_ref[:, sl]).astype(BF16)

    per = tm // HALO
    return pl.pallas_call(
        body, name=name, grid=(S // tm,),
        in_specs=[pl.BlockSpec((tm, W), lambda i: (i, u_col)),
                  pl.BlockSpec((HALO, W), lambda i: (jnp.maximum(i * per - 1, 0), u_col)),
                  pl.BlockSpec((len(POOL_WINDOWS), G, G), lambda i: (0, 0, 0)),
                  pl.BlockSpec((1, W), lambda i: (0, 0))],
        out_specs=pl.BlockSpec((tm, W), lambda i: (i, 0)),
        out_shape=jax.ShapeDtypeStruct((S, W), BF16),
        compiler_params=_cp("parallel"),
    )(u, u, pool_w, pool_scale)


def _pool_bwd(u, u_col, dy, dy_col, pool_w, pool_scale, *, name, tm=256):
    S, W = u.shape[0], POOL_WIDTH
    G = POOL_GROUP_DIM
    nt = S // tm

    def body(u_ref, h_ref, dy_ref, dyn_ref, w_ref, s_ref, du_ref, gw_ref, gs_ref):
        i = pl.program_id(0)

        @pl.when(i == 0)
        def _():
            gw_ref[...] = jnp.zeros_like(gw_ref)
            gs_ref[...] = jnp.zeros_like(gs_ref)

        uv = u_ref[...]
        halo = jnp.where(i > 0, h_ref[...], 0.0)
        dyv = dy_ref[...]
        dyn = jnp.where(i < nt - 1, dyn_ref[...], 0.0)
        rr = lax.broadcasted_iota(jnp.int32, (tm, tm + HALO), 0)
        cc = lax.broadcasted_iota(jnp.int32, (tm, tm + HALO), 1)
        rows_ext = i * tm + lax.broadcasted_iota(jnp.int32, (tm + HALO, 1), 0)
        groups = list(enumerate(POOL_WINDOWS))
        sls = [slice(g * G, (g + 1) * G) for g, _ in groups]
        wgs = [w_ref[g].astype(BF16) for g, _ in groups]
        pooled = [_pooled(uv[:, sl], halo[:, sl], w, i * tm, tm).astype(BF16) for sl, (_, w) in zip(sls, groups)]
        dzs = [dyv[:, sl] * s_ref[:, sl] for sl in sls]
        dz_ext = [jnp.concatenate([dz, dyn[:, sl] * s_ref[:, sl]], axis=0).astype(BF16) for dz, sl in zip(dzs, sls)]
        dp_ext = [_dot(d, wg, NT) for d, wg in zip(dz_ext, wgs)]
        zs = [_dot(p, wg, NN) for p, wg in zip(pooled, wgs)]
        for (g, w), sl, p, dz, z, dp in zip(groups, sls, pooled, dzs, zs, dp_ext):
            gw_ref[g] += _dot(p, dz.astype(BF16), TN)
            gs_ref[:, sl] += jnp.sum(dyv[:, sl] * z, axis=0, keepdims=True)
            inv_ext = 1.0 / jnp.minimum(rows_ext + 1, w).astype(F32)
            hi, lo = _split_bf16(dp * inv_ext)
            ahead = cc - rr
            win = ((ahead >= 0) & (ahead < w)).astype(BF16)
            du_ref[:, sl] = (_dot(win, hi, NN) + _dot(win, lo, NN) - dp[:tm]).astype(BF16)

    per = tm // HALO
    nh = S // HALO
    return pl.pallas_call(
        body, name=name, grid=(nt,),
        in_specs=[pl.BlockSpec((tm, W), lambda i: (i, u_col)),
                  pl.BlockSpec((HALO, W), lambda i: (jnp.maximum(i * per - 1, 0), u_col)),
                  pl.BlockSpec((tm, W), lambda i: (i, dy_col)),
                  pl.BlockSpec((HALO, W), lambda i: (jnp.minimum((i + 1) * per, nh - 1), dy_col)),
                  pl.BlockSpec((len(POOL_WINDOWS), G, G), lambda i: (0, 0, 0)),
                  pl.BlockSpec((1, W), lambda i: (0, 0))],
        out_specs=[pl.BlockSpec((tm, W), lambda i: (i, 0)),
                   pl.BlockSpec((len(POOL_WINDOWS), G, G), lambda i: (0, 0, 0)),
                   pl.BlockSpec((1, W), lambda i: (0, 0))],
        out_shape=[jax.ShapeDtypeStruct((S, W), BF16),
                   jax.ShapeDtypeStruct((len(POOL_WINDOWS), G, G), F32),
                   jax.ShapeDtypeStruct((1, W), F32)],
        compiler_params=_cp("arbitrary"),
    )(u, u, dy, dy, pool_w, pool_scale)


GELU_K0 = math.sqrt(2.0 / math.pi)
GELU_K1 = 0.044715


def _gelu_parts(x):
    x2 = x * x
    t = jnp.tanh(x * (GELU_K0 + (GELU_K0 * GELU_K1) * x2))
    hp = 0.5 + 0.5 * t
    gelu = x * hp
    dgelu = hp + (x * (hp * (1.0 - t))) * (GELU_K0 + (3.0 * GELU_K0 * GELU_K1) * x2)
    return gelu, dgelu


def _shifted(ext, halo):
    return (pltpu.roll(ext, 2, 0)[halo:], pltpu.roll(ext, 1, 0)[halo:], ext[halo:])


def _conv(sh, w, b):
    return b + (sh[0] * w[0:1] + sh[1] * w[1:2] + sh[2] * w[2:3])


F32_ROWS = 8


def _ffn_up_glu(h, w_up_t, conv_w, conv_b, *, name, tm=2048, tn=256, sub=256):
    S, K = h.shape
    F = D_FF
    nj = F // tn

    def body(h_ref, wg_ref, wv_ref, cwg_ref, cwv_ref, cbg_ref, cbv_ref,
             ug_ref, uv_ref, cg_ref, cv_ref, y_ref, carry):
        i = pl.program_id(0)
        j = pl.program_id(1)

        w_cat = jnp.concatenate([wg_ref[...], wv_ref[...]], axis=0)
        conv_w_b = ((cwg_ref[...], cbg_ref[...]), (cwv_ref[...], cbv_ref[...]))
        halo = [jnp.where(i > 0, carry[j, s], 0.0) for s in range(2)]
        u_next = _dot(h_ref[0:sub, :], w_cat, NT)
        for a in range(0, tm, sub):
            u16 = u_next.astype(BF16)
            if a + sub < tm:
                u_next = _dot(h_ref[a + sub:a + 2 * sub, :], w_cat, NT)
            ug_ref[a:a + sub, :] = u16[:, :tn]
            uv_ref[a:a + sub, :] = u16[:, tn:]
            c = []
            for s, (cw, cb) in enumerate(conv_w_b):
                u = u16[:, s * tn:(s + 1) * tn].astype(F32)
                ext = jnp.concatenate([halo[s], u], axis=0)
                c.append(_conv(_shifted(ext, F32_ROWS), cw, cb))
                halo[s] = u[sub - F32_ROWS:]
            cg_ref[a:a + sub, :] = c[0].astype(BF16)
            cv_ref[a:a + sub, :] = c[1].astype(BF16)
            gelu, _ = _gelu_parts(c[0])
            y_ref[a:a + sub, :] = (gelu * c[1]).astype(BF16)
        for s in range(2):
            carry[j, s] = halo[s]

    tile = pl.BlockSpec((tm, tn), lambda i, j: (i, j))
    vec = lambda rows, off: pl.BlockSpec((rows, tn), lambda i, j: (0, j + off))
    return pl.pallas_call(
        body, name=name, grid=(S // tm, nj),
        in_specs=[pl.BlockSpec((tm, K), lambda i, j: (i, 0)),
                  pl.BlockSpec((tn, K), lambda i, j: (j, 0)), pl.BlockSpec((tn, K), lambda i, j: (j + nj, 0)),
                  vec(3, 0), vec(3, nj), vec(1, 0), vec(1, nj)],
        out_specs=[tile] * 5,
        out_shape=[jax.ShapeDtypeStruct((S, F), BF16)] * 5,
        scratch_shapes=[pltpu.VMEM((nj, 2, F32_ROWS, tn), F32)],
        compiler_params=_cp("arbitrary", "arbitrary"),
    )(h, w_up_t, w_up_t, conv_w, conv_w, conv_b, conv_b)


def _ffn_glu_bwd(u_g, u_v, c_g, c_v, df, w_down, h, conv_w, *, name, tm=2048, tn=256, sub=256):
    S = u_g.shape[0]
    F = D_FF
    D = df.shape[1]
    nj = F // tn
    nt = S // tm

    def body(ug_ref, uv_ref, cg_ref, cgn_ref, cv_ref, cvn_ref, df_ref, dfn_ref, wd_ref, h_ref, wg_ref, wv_ref,
             dug_ref, duv_ref, gug_ref, guv_ref, gd_ref, gwg_ref, gwv_ref, gbg_ref, gbv_ref,
             acc_u, acc_d):
        i = pl.program_id(1)

        @pl.when(i == 0)
        def _():
            for r in (gwg_ref, gwv_ref, gbg_ref, gbv_ref, acc_u, acc_d):
                r[...] = jnp.zeros_like(r)

        wg, wv = wg_ref[...], wv_ref[...]
        wd = wd_ref[...]
        dfn = jnp.where(i < nt - 1, dfn_ref[...], jnp.zeros_like(dfn_ref))
        n_ext = sub + HALO

        def ahead(dc):
            return dc[:sub], pltpu.roll(dc, n_ext - 1, 0)[:sub], pltpu.roll(dc, n_ext - 2, 0)[:sub]

        def ext(ref, nxt, a):
            b = a + sub
            return jnp.concatenate([ref[a:b, :], ref[b:b + HALO, :] if b < tm else nxt], axis=0)

        dy_next = _dot(ext(df_ref, dfn, 0), wd, NT)
        for a in range(0, tm, sub):
            b = a + sub
            dy_ext = dy_next
            if b < tm:
                dy_next = _dot(ext(df_ref, dfn, b), wd, NT)
            cg = ext(cg_ref, cgn_ref[...], a).astype(F32)
            cv = ext(cv_ref, cvn_ref[...], a).astype(F32)
            df_sub = df_ref[a:b, :]
            gelu, dgelu = _gelu_parts(cg)
            dcs_g = ahead(dy_ext * cv * dgelu)
            dcs_v = ahead(dy_ext * gelu)
            du_g = (dcs_g[0] * wg[2:3] + dcs_g[1] * wg[1:2] + dcs_g[2] * wg[0:1]).astype(BF16)
            du_v = (dcs_v[0] * wv[2:3] + dcs_v[1] * wv[1:2] + dcs_v[2] * wv[0:1]).astype(BF16)
            dug_ref[a:b, :] = du_g
            duv_ref[a:b, :] = du_v
            acc_u[...] += _dot(jnp.concatenate([du_g, du_v], axis=1), h_ref[a:b, :], TN)
            acc_d[...] += _dot((gelu[:sub] * cv[:sub]).astype(BF16), df_sub, TN)
            for dcs, u_ref, gw_ref, gb_ref in ((dcs_g, ug_ref, gwg_ref, gbg_ref), (dcs_v, uv_ref, gwv_ref, gbv_ref)):
                u = u_ref[a:b, :].astype(F32)
                gb_ref[...] += jnp.sum(dcs[0], axis=0, keepdims=True)
                for k in range(3):
                    gw_ref[k:k + 1, :] += jnp.sum(dcs[2 - k] * u, axis=0, keepdims=True)

        @pl.when(i == nt - 1)
        def _():
            gug_ref[...] = acc_u[:tn, :].astype(BF16)
            guv_ref[...] = acc_u[tn:, :].astype(BF16)
            gd_ref[...] = acc_d[...].astype(BF16)

    per = tm // HALO
    nh = S // HALO
    hnext = lambda i: jnp.minimum((i + 1) * per, nh - 1)
    tile = pl.BlockSpec((tm, tn), lambda j, i: (i, j))
    hn = pl.BlockSpec((HALO, tn), lambda j, i: (hnext(i), j))
    vec = lambda rows, off: pl.BlockSpec((rows, tn), lambda j, i: (0, j + off))
    wide = pl.BlockSpec((tm, D), lambda j, i: (i, 0))
    wrow = pl.BlockSpec((tn, D), lambda j, i: (j, 0))
    return pl.pallas_call(
        body, name=name, grid=(nj, nt),
        in_specs=[tile, tile, tile, hn, tile, hn, wide, pl.BlockSpec((HALO, D), lambda j, i: (hnext(i), 0)),
                  wrow, wide, vec(3, 0), vec(3, nj)],
        out_specs=[tile, tile, wrow, wrow, wrow, vec(3, 0), vec(3, 0), vec(1, 0), vec(1, 0)],
        out_shape=[jax.ShapeDtypeStruct((S, F), BF16), jax.ShapeDtypeStruct((S, F), BF16),
                   jax.ShapeDtypeStruct((F, D), BF16), jax.ShapeDtypeStruct((F, D), BF16),
                   jax.ShapeDtypeStruct((F, D), BF16),
                   jax.ShapeDtypeStruct((3, F), F32), jax.ShapeDtypeStruct((3, F), F32),
                   jax.ShapeDtypeStruct((1, F), F32), jax.ShapeDtypeStruct((1, F), F32)],
        scratch_shapes=[pltpu.VMEM((2 * tn, D), F32), pltpu.VMEM((tn, D), F32)],
        compiler_params=_cp("parallel", "arbitrary"),
    )(u_g, u_v, c_g, c_g, c_v, c_v, df, df, w_down, h, conv_w, conv_w)


def _sum_partials(parts, *, name, tr):
    _, R, C = parts.shape

    def body(p_ref, o_ref):
        tot = p_ref[0].astype(F32)
        for j in range(1, N_DEV):
            tot = tot + p_ref[j].astype(F32)
        o_ref[...] = tot

    return pl.pallas_call(
        body, name=name, grid=(R // tr,),
        in_specs=[pl.BlockSpec((N_DEV, tr, C), lambda i: (0, i, 0))],
        out_specs=pl.BlockSpec((tr, C), lambda i: (i, 0)),
        out_shape=jax.ShapeDtypeStruct((R, C), F32),
        compiler_params=_cp("parallel"),
    )(parts)


def _adamw(w, g, m, v, *, name, tr):
    R, C = w.shape
    c1 = 1.0 - ADAM_B1 ** ADAM_STEP
    c2 = 1.0 - ADAM_B2 ** ADAM_STEP

    def body(w_ref, g_ref, m_ref, v_ref, d_ref, nm_ref, nv_ref):
        g = g_ref[...]
        nm = ADAM_B1 * m_ref[...] + (1.0 - ADAM_B1) * g
        nv = ADAM_B2 * v_ref[...] + (1.0 - ADAM_B2) * (g * g)
        d_ref[...] = -ADAM_LR * ((nm / c1) / (jnp.sqrt(nv / c2) + ADAM_EPS) + ADAM_WD * w_ref[...])
        nm_ref[...] = nm
        nv_ref[...] = nv

    spec = pl.BlockSpec((tr, C), lambda i: (i, 0))
    return pl.pallas_call(
        body, name=name, grid=(R // tr,), in_specs=[spec] * 4, out_specs=[spec] * 3,
        out_shape=[jax.ShapeDtypeStruct((R, C), F32)] * 3,
        compiler_params=_cp("parallel"),
    )(w, g, m, v)


def _mesh_pos():
    return lax.axis_index("x"), lax.axis_index("y"), lax.axis_index("c")


def _gather_phases(x_refs, out_refs, send_sems, recv_sems, local_sems):
    x, y, c = _mesh_pos()
    me, sibling = (x, y, c), (x, y, 1 - c)
    chips = [(1 - x, y), (x, 1 - y), (1 - x, 1 - y)]
    arrays = range(len(x_refs))

    def slot(a, px, py, pc):
        return out_refs[a].at[4 * px + 2 * py + pc]

    def copy(a, k, block, to, own=False):
        return pltpu.make_async_remote_copy(
            src_ref=x_refs[a] if own else slot(a, *block), dst_ref=slot(a, *block),
            send_sem=send_sems.at[a, k], recv_sem=recv_sems.at[a, k], device_id=to, device_id_type=MESH)

    mine = [pltpu.make_async_copy(x_refs[a], slot(a, *me), local_sems.at[a]) for a in arrays]
    first = [copy(a, 0, me, sibling, own=True) for a in arrays]
    first += [copy(a, 1 + j, me, (*chip, c), own=True) for j, chip in enumerate(chips) for a in arrays]
    passed = [[copy(a, 4 + j, (*chip, c), sibling) for a in arrays] for j, chip in enumerate(chips)]

    def start():
        for cp in mine + first:
            cp.start()

    def forward():
        for j, chip in enumerate(chips):
            for a in arrays:
                copy(a, 1 + j, (*chip, c), me).wait_recv()
                passed[j][a].start()

    def finish():
        for a in arrays:
            copy(a, 0, sibling, me).wait_recv()
            for j, chip in enumerate(chips):
                copy(a, 4 + j, (*chip, 1 - c), me).wait_recv()
        for cp in first + [cp for row in passed for cp in row]:
            cp.wait_send()
        for cp in mine:
            cp.wait()

    return start, forward, finish


def _gather_sems(n):
    return [pltpu.SemaphoreType.DMA((n, 7)), pltpu.SemaphoreType.DMA((n, 7)), pltpu.SemaphoreType.DMA((n,))]


def _gathered_shapes(blocks):
    return [jax.ShapeDtypeStruct((N_DEV,) + b.shape, b.dtype) for b in blocks]


def _all_reduce_small(block, gathered, *, name):
    r0, r1 = block.shape[0], gathered.shape[1]

    def body(x_ref, more_ref, all_ref, sum_ref, *sems):
        for phase in _gather_phases([x_ref], [all_ref], *sems):
            phase()
        for ref, rows in ((all_ref, slice(0, r0)), (more_ref, slice(r0, r0 + r1))):
            tot = ref[0]
            for j in range(1, N_DEV):
                tot = tot + ref[j]
            sum_ref[rows, :] = tot

    return pl.pallas_call(
        body, name=name, in_specs=[VMEM, VMEM], out_specs=[VMEM, VMEM],
        out_shape=[jax.ShapeDtypeStruct((N_DEV,) + block.shape, block.dtype),
                   jax.ShapeDtypeStruct((r0 + r1, block.shape[1]), block.dtype)],
        scratch_shapes=_gather_sems(1),
        compiler_params=pltpu.CompilerParams(vmem_limit_bytes=V7X_VMEM_LIMIT),
    )(block, gathered)[1]


def _exchange_phases(g_refs, r_refs, send_sems, recv_sems, local_sems):
    x, y, c = _mesh_pos()
    me = 4 * x + 2 * y + c
    owns, remote = [], []
    for k, (g_ref, r_ref) in enumerate(zip(g_refs, r_refs)):
        rows = g_ref.shape[0] // N_DEV
        owns.append(pltpu.make_async_copy(g_ref.at[pl.ds(me * rows, rows)], r_ref.at[me], local_sems.at[k]))
        for p in range(1, N_DEV):
            px, py, pc = x ^ (p >> 2), y ^ ((p >> 1) & 1), c ^ (p & 1)
            peer = 4 * px + 2 * py + pc
            link = dict(send_sem=send_sems.at[k, p], recv_sem=recv_sems.at[k, p],
                        device_id=(px, py, pc), device_id_type=MESH)
            src = g_ref.at[pl.ds(peer * rows, rows)]
            send = pltpu.make_async_remote_copy(src_ref=src, dst_ref=r_ref.at[me], **link)
            arrival = pltpu.make_async_remote_copy(src_ref=src, dst_ref=r_ref.at[peer], **link)
            remote.append((send, arrival))

    def start():
        for own in owns:
            own.start()
        for send, _ in remote:
            send.start()

    def finish():
        for _, arrival in remote:
            arrival.wait_recv()
        for send, _ in remote:
            send.wait_send()
        for own in owns:
            own.wait()

    return start, finish


def _exchange_buffers(grads):
    n = len(grads)
    shapes = [jax.ShapeDtypeStruct((N_DEV, g.shape[0] // N_DEV, g.shape[1]), g.dtype) for g in grads]
    sems = [pltpu.SemaphoreType.DMA((n, N_DEV)), pltpu.SemaphoreType.DMA((n, N_DEV)),
            pltpu.SemaphoreType.DMA((n,))]
    return shapes, sems


def _unpack_gathered(gathered):
    w_out, w_up_t, w_down = (g.reshape(-1, D_MODEL) for g in gathered[:3])
    width = 2 * D_FF // N_DEV
    conv_w = jnp.transpose(gathered[3][:, :3, :width], (1, 0, 2)).reshape(3, 2 * D_FF)
    return w_out, w_up_t, w_down, conv_w


def _rest_payload(w_out, w_up, w_down, conv_w):
    rows, cols = conv_w.shape
    conv_w = jnp.pad(conv_w, ((0, (-rows) % F32_ROWS), (0, (-cols) % LANES)))
    return [w_out.astype(BF16), w_up.T.astype(BF16), w_down.astype(BF16), conv_w]


def _device_step(x, target, g_mix_pre, w_in_t_block, rest_payload, pool_w, pool_scale, g_mix_post, g_ffn_pre,
                 conv_b, g_ffn_post):
    h1, w_in_t = _rms_norm_gather(x, g_mix_pre, w_in_t_block, name="rms_mix_pre")
    w_in_t = w_in_t.reshape(-1, D_MODEL)
    proj = _matmul(h1, w_in_t, trans_b=True, out_dtype=F32, tm=1024, tn=4 * ATTN_WIDTH, name="proj")
    attn, lse, attn16, gathered = _attn_fwd(proj, rest_payload, name="attn_fwd")
    w_out, w_up_t, w_down, conv_w = _unpack_gathered(gathered)
    pool = _pool_fwd(proj, 3, pool_w, pool_scale, name="pool_fwd")
    mixed, x2, h2 = _mix_out(attn16, pool, w_out, x, g_mix_post, g_ffn_pre, name="mix_out")
    u_g, u_v, c_g, c_v, y = _ffn_up_glu(h2, w_up_t, conv_w, conv_b, name="ffn_up_glu")
    df, d_out, loss_blk, gg_ffn_post = _ffn_out(y, w_down, x2, target, g_ffn_post, name="ffn_out")
    du_g, du_v, gw_up_g, gw_up_v, gw_down, gcw_g, gcw_v, gcb_g, gcb_v = _ffn_glu_bwd(
        u_g, u_v, c_g, c_v, df, w_down, h2, conv_w, name="ffn_glu_bwd")
    gw_up_t = jnp.concatenate([gw_up_g, gw_up_v], axis=0)
    dx2, gg_ffn_pre, dmixed, gg_mix_post = _dgrad_norm(
        [du_g, du_v], w_up_t, d_out, x2, g_ffn_pre, (mixed, g_mix_post), [], name="ffn_up_dgrad")
    gw_out = _matmul_tn([attn16, pool], dmixed, name="grad_w_out")
    dcat = _matmul(dmixed, w_out, trans_b=True, out_dtype=F32, tm=1024, tn=1024, name="mix_out_dgrad")
    d_pool_in, g_pool_w, g_pool_scale = _pool_bwd(proj, 3, dcat, 1, pool_w, pool_scale, name="pool_bwd")
    early = dict(g_mix_post=gg_mix_post, g_ffn_pre=gg_ffn_pre, g_ffn_post=gg_ffn_post, pool_scale=g_pool_scale,
                 conv_b=jnp.concatenate([gcb_g, gcb_v], axis=1), pool_w=g_pool_w)
    early_block = _pack_rows([early[k] for k in _SMALL[1:]] + [jnp.concatenate([gcw_g, gcw_v], axis=1), loss_blk])
    dqkv, (r_out, r_up_t, r_down), (small_gathered,) = _attn_bwd(
        proj, dcat, attn, lse, [gw_out, gw_up_t, gw_down], [early_block], name="attn_bwd")
    dproj = list(dqkv) + [d_pool_in]
    gw_in_t = _matmul_tn(dproj, h1, name="grad_w_in")
    grad_x, gg_mix_pre, (r_in_t,) = _dgrad_norm(dproj, w_in_t, dx2, x, g_mix_pre, None, [gw_in_t], tm=1024,
                                                name="proj_dgrad")
    received = (r_in_t, r_out, r_up_t, r_down)
    return grad_x, received, gg_mix_pre, small_gathered


_SMALL = ("g_mix_pre", "g_mix_post", "g_ffn_pre", "g_ffn_post", "pool_scale", "conv_b", "pool_w")
LANES = 128


def _pack_rows(arrays):
    parts = []
    for a in arrays:
        a2 = a.reshape(-1, LANES)
        parts.append(jnp.pad(a2, ((0, (-a2.shape[0]) % 8), (0, 0))))
    return jnp.concatenate(parts, axis=0)


def _unpack_rows(packed, shapes):
    out, row = [], 0
    for shape in shapes:
        rows = math.prod(shape) // LANES
        out.append(packed[row:row + rows].reshape(shape))
        row += -(-rows // 8) * 8
    return out


def kernel(x, g_mix_pre, w_in, pool_w, pool_scale, w_out, g_mix_post, g_ffn_pre, w_up, conv_w, conv_b, w_down, g_ffn_post, loss_target, m_g_mix_pre, m_w_in, m_pool_w, m_pool_scale, m_w_out, m_g_mix_post, m_g_ffn_pre, m_w_up, m_conv_w, m_conv_b, m_w_down, m_g_ffn_post, v_g_mix_pre, v_w_in, v_pool_w, v_pool_scale, v_w_out, v_g_mix_post, v_g_ffn_pre, v_w_up, v_conv_w, v_conv_b, v_w_down, v_g_ffn_post):
    me = 4 * lax.axis_index("x") + 2 * lax.axis_index("y") + lax.axis_index("c")
    grad_x, recv, gg_mix_pre, small_gathered = _device_step(
        x[0], loss_target[0], g_mix_pre, w_in[0].T.astype(BF16),
        _rest_payload(w_out[0], w_up[0], w_down[0], conv_w[0]),
        pool_w[0], pool_scale, g_mix_post, g_ffn_pre, conv_b, g_ffn_post)

    g_in_t, g_out, g_up_t, g_down = (
        _sum_partials(r, name=f"sum_partials_{k}", tr=r.shape[1] // 2) for k, r in enumerate(recv))
    grads = {"w_in": g_in_t.T, "w_out": g_out, "w_up": g_up_t.T, "w_down": g_down}

    given = dict(g_mix_pre=g_mix_pre, g_mix_post=g_mix_post, g_ffn_pre=g_ffn_pre, g_ffn_post=g_ffn_post,
                 pool_scale=pool_scale, conv_b=conv_b, pool_w=pool_w)
    small_shapes = [given[k].shape for k in _SMALL]
    total = _all_reduce_small(_pack_rows([gg_mix_pre]), small_gathered, name="all_reduce_small")
    *small_grads, g_conv_w_all, loss_all = _unpack_rows(total, small_shapes + [(3, 2 * D_FF), (8, LANES)])
    loss = loss_all[0, 0]
    grads.update(zip(_SMALL, small_grads))
    width = 2 * D_FF // N_DEV
    grads["conv_w"] = lax.dynamic_slice_in_dim(g_conv_w_all, me * width, width, axis=1)[None]

    weights = dict(g_mix_pre=g_mix_pre, w_in=w_in, pool_w=pool_w, pool_scale=pool_scale, w_out=w_out,
                   g_mix_post=g_mix_post, g_ffn_pre=g_ffn_pre, w_up=w_up, conv_w=conv_w, conv_b=conv_b,
                   w_down=w_down, g_ffn_post=g_ffn_post)
    m_in = dict(g_mix_pre=m_g_mix_pre, w_in=m_w_in, pool_w=m_pool_w, pool_scale=m_pool_scale, w_out=m_w_out,
                g_mix_post=m_g_mix_post, g_ffn_pre=m_g_ffn_pre, w_up=m_w_up, conv_w=m_conv_w, conv_b=m_conv_b,
                w_down=m_w_down, g_ffn_post=m_g_ffn_post)
    v_in = dict(g_mix_pre=v_g_mix_pre, w_in=v_w_in, pool_w=v_pool_w, pool_scale=v_pool_scale, w_out=v_w_out,
                g_mix_post=v_g_mix_post, g_ffn_pre=v_g_ffn_pre, w_up=v_w_up, conv_w=v_conv_w, conv_b=v_conv_b,
                w_down=v_w_down, g_ffn_post=v_g_ffn_post)
    delta, new_m, new_v = {}, {}, {}
    for k in ("w_in", "w_out", "w_up", "w_down"):
        g = grads[k]
        d, nm, nv = _adamw(weights[k][0], g, m_in[k][0], v_in[k][0], name=f"adamw_{k}", tr=g.shape[0] // 2)
        grads[k], delta[k], new_m[k], new_v[k] = g[None], d[None], nm[None], nv[None]
    d, nm, nv = _adamw(weights["conv_w"][0], grads["conv_w"][0], m_in["conv_w"][0], v_in["conv_w"][0],
                       name="adamw_conv_w", tr=3)
    delta["conv_w"], new_m["conv_w"], new_v["conv_w"] = d[None], nm[None], nv[None]
    packed_w = _pack_rows([weights[k] for k in _SMALL])
    small_rows = packed_w.shape[0]
    d, nm, nv = _adamw(packed_w, total[:small_rows], _pack_rows([m_in[k] for k in _SMALL]),
                       _pack_rows([v_in[k] for k in _SMALL]), name="adamw_small", tr=small_rows)
    for k, dk, mk, vk in zip(_SMALL, _unpack_rows(d, small_shapes), _unpack_rows(nm, small_shapes),
                             _unpack_rows(nv, small_shapes)):
        delta[k], new_m[k], new_v[k] = dk, mk, vk

    order = ("g_mix_pre", "w_in", "pool_w", "pool_scale", "w_out", "g_mix_post", "g_ffn_pre", "w_up",
             "conv_w", "conv_b", "w_down", "g_ffn_post")
    return (loss, grad_x[None], *[grads[k] for k in order], *[delta[k] for k in order],
            *[new_m[k] for k in order], *[new_v[k] for k in order])
```

```python
import functools
import math

import jax
import jax.numpy as jnp
from jax import lax
from jax.experimental import pallas as pl
from jax.experimental.pallas import tpu as pltpu

F32 = jnp.float32
BF16 = jnp.bfloat16

D_MODEL = 1024
N_HEADS = 8
HEAD_DIM = 64
ATTN_WIDTH = N_HEADS * HEAD_DIM
DILATIONS = (1, 4, 16)
BLOCK = 128
POOL_WIDTH = 512
POOL_WINDOWS = (2, 4, 8, 16)
POOL_GROUP_DIM = 128
D_FF = 2816
EPS = 1e-6
NEG_INF = -1e30
SCALE = HEAD_DIM ** -0.5

ADAM_LR = 0.001
ADAM_B1 = 0.9
ADAM_B2 = 0.999
ADAM_EPS = 1e-08
ADAM_WD = 0.01
ADAM_STEP = 10

N_DEV = 8
HALO = 16
V7X_VMEM_LIMIT = 56 * 1024 * 1024

MESH = pl.DeviceIdType.MESH
ANY = pl.BlockSpec(memory_space=pl.ANY)
VMEM = pl.BlockSpec(memory_space=pltpu.VMEM)

NT = (((1,), (1,)), ((), ()))
NN = (((1,), (0,)), ((), ()))
TN = (((0,), (0,)), ((), ()))


def _cp(*sem):
    return pltpu.CompilerParams(dimension_semantics=sem, vmem_limit_bytes=V7X_VMEM_LIMIT)


def _dot(a, b, dn):
    return lax.dot_general(a, b, dn, preferred_element_type=F32)


def _rms_bwd(xin, g, dy):
    r = lax.rsqrt(jnp.mean(xin * xin, axis=-1, keepdims=True) + EPS)
    xh = xin * r
    gdy = g * dy
    dx = r * (gdy - xh * jnp.mean(gdy * xh, axis=-1, keepdims=True))
    dg = jnp.sum(dy * xh, axis=0, keepdims=True)
    return dx, dg


def _rms_norm_gather(x, g, block, *, name, tm=512):
    S, D = x.shape
    nt = S // tm

    def body(x_ref, g_ref, blk_ref, o_ref, all_ref, *sems):
        i = pl.program_id(0)
        start, forward, finish = _gather_phases([blk_ref], [all_ref], *sems)
        pl.when(i == 0)(start)
        xv = x_ref[...]
        r = lax.rsqrt(jnp.mean(xv * xv, axis=-1, keepdims=True) + EPS)
        o_ref[...] = (xv * r * g_ref[...]).astype(BF16)
        pl.when(i == nt - 1)(forward)
        pl.when(i == nt - 1)(finish)

    return pl.pallas_call(
        body, name=name, grid=(nt,),
        in_specs=[pl.BlockSpec((tm, D), lambda i: (i, 0)), pl.BlockSpec((1, D), lambda i: (0, 0)), ANY],
        out_specs=[pl.BlockSpec((tm, D), lambda i: (i, 0)), ANY],
        out_shape=[jax.ShapeDtypeStruct((S, D), BF16)] + _gathered_shapes([block]),
        scratch_shapes=_gather_sems(1),
        compiler_params=_cp("arbitrary"),
    )(x, g, block)


def _matmul(a, b, *, trans_b, out_dtype, tm, tn, name):
    M, K = a.shape
    N = b.shape[0] if trans_b else b.shape[1]
    dn = NT if trans_b else NN

    def body(a_ref, b_ref, o_ref):
        o_ref[...] = _dot(a_ref[...], b_ref[...], dn).astype(out_dtype)

    b_spec = (pl.BlockSpec((tn, K), lambda i, j: (j, 0)) if trans_b
              else pl.BlockSpec((K, tn), lambda i, j: (0, j)))
    return pl.pallas_call(
        body, name=name, grid=(M // tm, N // tn),
        in_specs=[pl.BlockSpec((tm, K), lambda i, j: (i, 0)), b_spec],
        out_specs=pl.BlockSpec((tm, tn), lambda i, j: (i, j)),
        out_shape=jax.ShapeDtypeStruct((M, N), out_dtype),
        compiler_params=_cp("parallel", "parallel"),
    )(a, b)


def _matmul_tn(a_list, b, *, name, ts=1024):
    S, Ka = a_list[0].shape
    na = len(a_list)
    Nb = b.shape[1]
    ns = S // ts

    def body(*refs):
        a_refs, b_ref, o_ref, acc = refs[:na], refs[na], refs[na + 1], refs[na + 2]
        s = pl.program_id(0)

        @pl.when(s == 0)
        def _():
            acc[...] = jnp.zeros_like(acc)

        acc[...] += _dot(jnp.concatenate([r[...] for r in a_refs], axis=1), b_ref[...], TN)

        @pl.when(s == ns - 1)
        def _():
            o_ref[...] = acc[...].astype(BF16)

    return pl.pallas_call(
        body, name=name, grid=(ns,),
        in_specs=[pl.BlockSpec((ts, Ka), lambda s: (s, 0))] * na + [pl.BlockSpec((ts, Nb), lambda s: (s, 0))],
        out_specs=pl.BlockSpec((na * Ka, Nb), lambda s: (0, 0)),
        out_shape=jax.ShapeDtypeStruct((na * Ka, Nb), BF16),
        scratch_shapes=[pltpu.VMEM((na * Ka, Nb), F32)],
        compiler_params=_cp("arbitrary"),
    )(*a_list, b)


def _mix_out(attn, pool, w_out, x, g_post, g_next, *, name, tm=512):
    S, K = attn.shape
    D = w_out.shape[1]

    def body(a_ref, p_ref, w_ref, x_ref, gp_ref, gn_ref, mixed_ref, x2_ref, h2_ref):
        mixed = _dot(a_ref[...], w_ref[:K, :], NN) + _dot(p_ref[...], w_ref[K:, :], NN)
        r = lax.rsqrt(jnp.mean(mixed * mixed, axis=-1, keepdims=True) + EPS)
        x2 = x_ref[...] + mixed * r * gp_ref[...]
        r2 = lax.rsqrt(jnp.mean(x2 * x2, axis=-1, keepdims=True) + EPS)
        mixed_ref[...] = mixed
        x2_ref[...] = x2
        h2_ref[...] = (x2 * r2 * gn_ref[...]).astype(BF16)

    row = lambda i: (i, 0)
    fix = lambda i: (0, 0)
    return pl.pallas_call(
        body, name=name, grid=(S // tm,),
        in_specs=[pl.BlockSpec((tm, K), row), pl.BlockSpec((tm, K), row), pl.BlockSpec((2 * K, D), fix),
                  pl.BlockSpec((tm, D), row), pl.BlockSpec((1, D), fix), pl.BlockSpec((1, D), fix)],
        out_specs=[pl.BlockSpec((tm, D), row)] * 3,
        out_shape=[jax.ShapeDtypeStruct((S, D), F32), jax.ShapeDtypeStruct((S, D), F32),
                   jax.ShapeDtypeStruct((S, D), BF16)],
        compiler_params=_cp("parallel"),
    )(attn, pool, w_out, x, g_post, g_next)


def _ffn_out(y, w_down, x2, target, g_post, *, name, tm=512, sub=256):
    S, K = y.shape
    D = w_down.shape[1]

    def body(y_ref, w_ref, x2_ref, t_ref, g_ref, df_ref, dout_ref, loss_ref, gg_ref):
        i = pl.program_id(0)

        @pl.when(i == 0)
        def _():
            loss_ref[...] = jnp.zeros_like(loss_ref)
            gg_ref[...] = jnp.zeros_like(gg_ref)

        g = g_ref[...]
        w = w_ref[...]
        f_next = _dot(y_ref[0:sub, :], w, NN)
        for a in range(0, tm, sub):
            rows = slice(a, a + sub)
            f = f_next
            if a + sub < tm:
                f_next = _dot(y_ref[a + sub:a + 2 * sub, :], w, NN)
            r = lax.rsqrt(jnp.mean(f * f, axis=-1, keepdims=True) + EPS)
            out = x2_ref[rows, :] + f * r * g
            err = out - t_ref[rows, :]
            dy = err * (1.0 / D)
            df, dg = _rms_bwd(f, g, dy)
            df_ref[rows, :] = df.astype(BF16)
            dout_ref[rows, :] = dy
            gg_ref[...] += dg
            loss_ref[...] += 0.5 * jnp.sum(jnp.mean(err * err, axis=-1, keepdims=True))

    row = lambda i: (i, 0)
    fix = lambda i: (0, 0)
    return pl.pallas_call(
        body, name=name, grid=(S // tm,),
        in_specs=[pl.BlockSpec((tm, K), row), pl.BlockSpec((K, D), fix), pl.BlockSpec((tm, D), row),
                  pl.BlockSpec((tm, D), row), pl.BlockSpec((1, D), fix)],
        out_specs=[pl.BlockSpec((tm, D), row), pl.BlockSpec((tm, D), row),
                   pl.BlockSpec((8, 128), fix), pl.BlockSpec((1, D), fix)],
        out_shape=[jax.ShapeDtypeStruct((S, D), BF16), jax.ShapeDtypeStruct((S, D), F32),
                   jax.ShapeDtypeStruct((8, 128), F32), jax.ShapeDtypeStruct((1, D), F32)],
        compiler_params=_cp("arbitrary"),
    )(y, w_down, x2, target, g_post)


def _dgrad_norm(a_list, w, resid, xin, g, second, exchange, *, name, tm=512, sub=256):
    S, Kp = a_list[0].shape
    na = len(a_list)
    D = w.shape[1]
    nt = S // tm
    two = second is not None
    ng = len(exchange)
    recv_shapes, exchange_sems = _exchange_buffers(exchange)

    def body(*refs):
        a_refs = refs[:na]
        w_ref, r_ref, x_ref, g_ref = refs[na:na + 4]
        pos = na + 4
        if two:
            x2_ref, g2_ref = refs[pos:pos + 2]
            pos += 2
        g_refs = refs[pos:pos + ng]
        pos += ng
        dx_ref, gg_ref = refs[pos:pos + 2]
        pos += 2
        if two:
            d2_ref, gg2_ref = refs[pos:pos + 2]
            pos += 2
        r_refs = refs[pos:pos + ng]
        pos += ng
        i = pl.program_id(0)
        if ng:
            start, finish = _exchange_phases(g_refs, r_refs, *refs[pos:])
            pl.when(i == 0)(start)

        @pl.when(i == 0)
        def _():
            gg_ref[...] = jnp.zeros_like(gg_ref)
            if two:
                gg2_ref[...] = jnp.zeros_like(gg2_ref)

        def dh_of(a):
            return functools.reduce(jnp.add, [_dot(a_refs[q][a:a + sub, :], w_ref[q * Kp:(q + 1) * Kp, :], NN)
                                              for q in range(na)])

        dh_next = dh_of(0)
        for a in range(0, tm, sub):
            rows = slice(a, a + sub)
            dh = dh_next
            if a + sub < tm:
                dh_next = dh_of(a + sub)
            d1, dg1 = _rms_bwd(x_ref[rows, :], g_ref[...], dh)
            dx = r_ref[rows, :] + d1
            dx_ref[rows, :] = dx
            gg_ref[...] += dg1
            if two:
                d2, dg2 = _rms_bwd(x2_ref[rows, :], g2_ref[...], dx)
                d2_ref[rows, :] = d2.astype(BF16)
                gg2_ref[...] += dg2
        if ng:
            pl.when(i == nt - 1)(finish)

    row = lambda i: (i, 0)
    fix = lambda i: (0, 0)
    in_specs = [pl.BlockSpec((tm, Kp), row)] * na + [
        pl.BlockSpec((na * Kp, D), fix, pipeline_mode=pl.Buffered(1)), pl.BlockSpec((tm, D), row),
        pl.BlockSpec((tm, D), row), pl.BlockSpec((1, D), fix)]
    args = list(a_list) + [w, resid, xin, g]
    out_specs = [pl.BlockSpec((tm, D), row), pl.BlockSpec((1, D), fix)]
    out_shape = [jax.ShapeDtypeStruct((S, D), F32), jax.ShapeDtypeStruct((1, D), F32)]
    if two:
        in_specs += [pl.BlockSpec((tm, D), row), pl.BlockSpec((1, D), fix)]
        args += list(second)
        out_specs += [pl.BlockSpec((tm, D), row), pl.BlockSpec((1, D), fix)]
        out_shape += [jax.ShapeDtypeStruct((S, D), BF16), jax.ShapeDtypeStruct((1, D), F32)]
    n_plain = len(out_shape)
    out = pl.pallas_call(
        body, name=name, grid=(nt,), in_specs=in_specs + [ANY] * ng, out_specs=out_specs + [ANY] * ng,
        out_shape=out_shape + recv_shapes, scratch_shapes=exchange_sems if ng else [],
        compiler_params=_cp("arbitrary"),
    )(*args, *exchange)
    return (*out[:n_plain], out[n_plain:]) if ng else out


def _band_mask(first_block):
    qi = lax.broadcasted_iota(jnp.int32, (BLOCK, 2 * BLOCK), 0)
    ki = lax.broadcasted_iota(jnp.int32, (BLOCK, 2 * BLOCK), 1)
    first_key = jnp.where(first_block, BLOCK, 0)
    return (ki >= qi) & (ki <= qi + BLOCK) & (ki >= first_key)


def _lane_masks():
    lane = lax.broadcasted_iota(jnp.int32, (1, 2 * HEAD_DIM), 1)
    return (lane < HEAD_DIM, lane >= HEAD_DIM)


CHUNK = BLOCK * max(DILATIONS)
SLAB = 2 * HEAD_DIM
N_SLABS = ATTN_WIDTH // SLAB


def _unit_rows(d, b):
    def rows(r):
        start = r + BLOCK * d * b
        return pl.ds(start, BLOCK, stride=d) if d > 1 else pl.ds(start, BLOCK)
    return rows


def _attn_units():
    for p, d in enumerate(DILATIONS):
        nbc = CHUNK // (BLOCK * d)
        for b in range(nbc):
            for r in range(d):
                yield p, d, b, r, nbc


def _attn_in_specs(nc, n_cur):
    prev = lambda c: jnp.maximum(jnp.minimum(c, nc - 1) - 1, 0)
    cur = lambda c: jnp.minimum(c, nc - 1)
    blk = lambda f: pl.BlockSpec((CHUNK, SLAB), f)
    specs = [blk(lambda h, c: (cur(c), h)),
             blk(lambda h, c: (prev(c), N_SLABS + h)), blk(lambda h, c: (cur(c), N_SLABS + h)),
             blk(lambda h, c: (prev(c), 2 * N_SLABS + h)), blk(lambda h, c: (cur(c), 2 * N_SLABS + h))]
    return specs + [blk(lambda h, c: (cur(c), h))] * n_cur


def _attn_fwd(proj, payload, *, name):
    S = proj.shape[0]
    nc = S // CHUNK
    n = len(DILATIONS)
    npay = len(payload)
    n_steps = N_SLABS * nc

    def body(*refs):
        q_ref, kp_ref, kc_ref, vp_ref, vc_ref = refs[:5]
        pay_refs = refs[5:5 + npay]
        attn_ref, lse_ref, attn16_ref = refs[5 + npay:8 + npay]
        all_refs = refs[8 + npay:8 + 2 * npay]
        scr = refs[8 + 2 * npay:]
        o_scr, l_scr = scr[:n], scr[n:2 * n]
        start, forward, finish = _gather_phases(pay_refs, all_refs, *scr[2 * n:])
        step = pl.program_id(0) * nc + pl.program_id(1)
        pl.when(step == 0)(start)
        c = pl.program_id(1)
        lms = _lane_masks()
        plain, first = (jnp.tile(_band_mask(f), (2, 1)) for f in (False, c == 0))
        def scores(unit):
            p, d, b, r, nbc = unit
            rows = _unit_rows(d, b)(r)
            prow = _unit_rows(d, (b - 1) % nbc)(r)
            kpr, vpr = (kc_ref, vc_ref) if b > 0 else (kp_ref, vp_ref)
            q = q_ref[rows, :].astype(BF16)
            kcat = jnp.concatenate([kpr[prow, :], kc_ref[rows, :]], axis=0).astype(BF16)
            vcat = jnp.concatenate([vpr[prow, :], vc_ref[rows, :]], axis=0).astype(BF16)
            q2 = jnp.concatenate([jnp.where(lm, q, jnp.zeros_like(q)) for lm in lms], axis=0) * SCALE
            return p, rows, plain if b > 0 else first, vcat, _dot(q2, kcat, NT)

        units = list(_attn_units())
        nxt = scores(units[0])
        for k in range(len(units)):
            p, rows, mask2, vcat, s = nxt
            if k + 1 < len(units):
                nxt = scores(units[k + 1])
            s = jnp.where(mask2, s, NEG_INF)
            m = jnp.max(s, axis=-1, keepdims=True)
            e = jnp.exp(s - m)
            l = jnp.sum(e, axis=-1, keepdims=True)
            o2 = _dot(e.astype(BF16), vcat, NN) / l
            lse2 = m + jnp.log(l)
            o_scr[p][rows, :] = jnp.where(lms[0], o2[:BLOCK], o2[BLOCK:])
            l_scr[p][rows, :] = jnp.where(lms[0], lse2[:BLOCK], lse2[BLOCK:])
        ls = [l_scr[p][...] for p in range(n)]
        top = functools.reduce(jnp.maximum, ls)
        es = [jnp.exp(l - top) for l in ls]
        den = functools.reduce(jnp.add, es)
        num = functools.reduce(jnp.add, [e * o_scr[p][...] for p, e in enumerate(es)])
        attn = num / den
        attn_ref[...] = attn
        attn16_ref[...] = attn.astype(BF16)
        lse_ref[...] = top + jnp.log(den)
        pl.when(step == (2 * n_steps) // 3)(forward)
        pl.when(step == n_steps - 1)(finish)

    out = pl.pallas_call(
        body, name=name, grid=(N_SLABS, nc), in_specs=_attn_in_specs(nc, 0) + [ANY] * npay,
        out_specs=[pl.BlockSpec((CHUNK, SLAB), lambda h, c: (c, h))] * 3 + [ANY] * npay,
        out_shape=[jax.ShapeDtypeStruct((S, ATTN_WIDTH), F32)] * 2 + [jax.ShapeDtypeStruct((S, ATTN_WIDTH), BF16)]
        + _gathered_shapes(payload),
        scratch_shapes=[pltpu.VMEM((CHUNK, SLAB), F32)] * (2 * n) + _gather_sems(npay),
        compiler_params=_cp("arbitrary", "arbitrary"),
    )(proj, proj, proj, proj, proj, *payload)
    return (*out[:3], out[3:])


def _attn_bwd(proj, dcat, attn, lse, grads, blocks, *, name):
    S = proj.shape[0]
    nc = S // CHUNK
    ng, nb = len(grads), len(blocks)
    n = len(DILATIONS)
    n_steps = N_SLABS * (nc + 1)
    recv_shapes, exchange_sems = _exchange_buffers(grads)

    def body(*refs):
        q_ref, kp_ref, kc_ref, vp_ref, vc_ref, do_ref, o_ref, lse_ref = refs[:8]
        g_refs, b_refs = refs[8:8 + ng], refs[8 + ng:8 + ng + nb]
        outs = refs[8 + ng + nb:]
        dq_ref, dk_ref, dv_ref = outs[:3]
        r_refs, all_refs = outs[3:3 + ng], outs[3 + ng:3 + ng + nb]
        scr = outs[3 + ng + nb:]
        dk_prev, dv_prev = scr[:2]
        delta_h, lse_h = scr[2:4], scr[4:6]
        dq_p, dk_own, dk_back, dv_own, dv_back = (scr[6 + n * k:6 + n * (k + 1)] for k in range(5))
        start, finish = _exchange_phases(g_refs, r_refs, *scr[6 + 5 * n:9 + 5 * n])
        gather_start, gather_forward, gather_finish = _gather_phases(b_refs, all_refs, *scr[9 + 5 * n:])
        c = pl.program_id(1)
        step = pl.program_id(0) * (nc + 1) + c

        @pl.when(step == 0)
        def _():
            gather_start()
            start()

        @pl.when(c == 0)
        def _():
            dk_prev[...] = jnp.zeros_like(dk_prev)
            dv_prev[...] = jnp.zeros_like(dv_prev)

        @pl.when(c < nc)
        def _():
            lms = _lane_masks()
            plain, first = (jnp.tile(_band_mask(f), (2, 1)) for f in (False, c == 0))
            prod = do_ref[...] * o_ref[...]
            lse = lse_ref[...]
            lse_other = pltpu.roll(lse, HEAD_DIM, 1)
            for h, lm in enumerate(lms):
                delta = jnp.sum(jnp.where(lm, prod, 0.0), axis=-1, keepdims=True)
                delta_h[h][...] = jnp.broadcast_to(delta, (CHUNK, SLAB))
                lse_h[h][...] = jnp.where(lm, lse, lse_other)
            wide = lambda refs, rows: jnp.tile(jnp.concatenate([r[rows, :] for r in refs], axis=0), (1, 2))
            stack = lambda f: jnp.concatenate([f(lm) for lm in lms], axis=0)

            def scores(unit):
                p, d, b, r, nbc = unit
                rows = _unit_rows(d, b)(r)
                prow = _unit_rows(d, (b - 1) % nbc)(r)
                kpr, vpr = (kc_ref, vc_ref) if b > 0 else (kp_ref, vp_ref)
                q = q_ref[rows, :].astype(BF16)
                kcat = jnp.concatenate([kpr[prow, :], kc_ref[rows, :]], axis=0).astype(BF16)
                vcat = jnp.concatenate([vpr[prow, :], vc_ref[rows, :]], axis=0).astype(BF16)
                do = do_ref[rows, :]
                q2 = stack(lambda lm: jnp.where(lm, q, jnp.zeros_like(q))) * SCALE
                do2 = stack(lambda lm: jnp.where(lm, do, 0.0)).astype(BF16)
                return dict(p=p, rows=rows, prow=prow, mask2=plain if b > 0 else first, kcat=kcat, q2=q2, do2=do2,
                            s=_dot(q2, kcat, NT), dp=_dot(do2, vcat, NT))

            units = list(_attn_units())
            nxt = scores(units[0])
            for k in range(len(units)):
                u = nxt
                if k + 1 < len(units):
                    nxt = scores(units[k + 1])
                p, rows, prow, kcat = u["p"], u["rows"], u["prow"], u["kcat"]
                e = jnp.where(u["mask2"], jnp.exp(u["s"] - wide(lse_h, rows)), 0.0)
                ds = (e * (u["dp"] - wide(delta_h, rows))).astype(BF16)
                dq2 = _dot(ds, kcat, NN) * SCALE
                dq = jnp.where(lms[0], dq2[:BLOCK], dq2[BLOCK:])
                dkc = _dot(ds, u["q2"], TN)
                dvc = _dot(e.astype(BF16), u["do2"], TN)
                dq_p[p][rows, :] = dq
                dk_own[p][rows, :] = dkc[BLOCK:]
                dv_own[p][rows, :] = dvc[BLOCK:]
                dk_back[p][prow, :] = dkc[:BLOCK]
                dv_back[p][prow, :] = dvc[:BLOCK]
            dq_ref[...] = functools.reduce(jnp.add, [r[...] for r in dq_p]).astype(BF16)
            for prev, own, back, out_ref in ((dk_prev, dk_own, dk_back, dk_ref), (dv_prev, dv_own, dv_back, dv_ref)):
                for p, d in enumerate(DILATIONS):
                    tail = CHUNK - BLOCK * d
                    prev[tail:, :] += back[p][tail:, :]
                out_ref[...] = prev[...].astype(BF16)
                prev[...] = functools.reduce(jnp.add, [r[...] for r in own])
                for p, d in enumerate(DILATIONS):
                    tail = CHUNK - BLOCK * d
                    if tail:
                        prev[:tail, :] += back[p][:tail, :]

        @pl.when(c == nc)
        def _():
            dk_ref[...] = dk_prev[...].astype(BF16)
            dv_ref[...] = dv_prev[...].astype(BF16)

        pl.when(step == (2 * n_steps) // 3)(gather_forward)

        @pl.when(step == n_steps - 1)
        def _():
            gather_finish()
            finish()

    blk = lambda f: pl.BlockSpec((CHUNK, SLAB), f)
    late = lambda h, c: (jnp.maximum(c - 1, 0), h)
    out = pl.pallas_call(
        body, name=name, grid=(N_SLABS, nc + 1), in_specs=_attn_in_specs(nc, 3) + [ANY] * (ng + nb),
        out_specs=[blk(lambda h, c: (jnp.minimum(c, nc - 1), h)), blk(late), blk(late)] + [ANY] * (ng + nb),
        out_shape=[jax.ShapeDtypeStruct((S, ATTN_WIDTH), BF16)] * 3 + recv_shapes + _gathered_shapes(blocks),
        scratch_shapes=[pltpu.VMEM((CHUNK, SLAB), F32)] * (6 + 5 * n) + exchange_sems + _gather_sems(nb),
        compiler_params=_cp("arbitrary", "arbitrary"),
    )(proj, proj, proj, proj, proj, dcat, attn, lse, *grads, *blocks)
    return out[:3], out[3:3 + ng], out[3 + ng:]


def _split_bf16(a):
    hi = a.astype(BF16)
    lo = (a - hi.astype(F32)).astype(BF16)
    return hi, lo


def _pooled(ug, halo_g, w, row0, tm):
    ext = jnp.concatenate([halo_g, ug], axis=0)
    hi, lo = _split_bf16(ext)
    rr = lax.broadcasted_iota(jnp.int32, (tm, tm + HALO), 0)
    cc = lax.broadcasted_iota(jnp.int32, (tm, tm + HALO), 1)
    back = rr + HALO - cc
    win = ((back >= 0) & (back < w)).astype(BF16)
    wsum = _dot(win, hi, NN) + _dot(win, lo, NN)
    rows = row0 + lax.broadcasted_iota(jnp.int32, (tm, 1), 0)
    inv = 1.0 / jnp.minimum(rows + 1, w).astype(F32)
    return wsum * inv - ug


def _pool_fwd(u, u_col, pool_w, pool_scale, *, name, tm=256):
    S, W = u.shape[0], POOL_WIDTH
    G = POOL_GROUP_DIM

    def body(u_ref, h_ref, w_ref, s_ref, o_ref):
        i = pl.program_id(0)
        uv = u_ref[...]
        halo = jnp.where(i > 0, h_ref[...], 0.0)
        sls = [slice(g * G, (g + 1) * G) for g in range(len(POOL_WINDOWS))]
        pooled = [_pooled(uv[:, sl], halo[:, sl], w, i * tm, tm) for sl, w in zip(sls, POOL_WINDOWS)]
        zs = [_dot(p.astype(BF16), w_ref[g].astype(BF16), NN) for g, p in enumerate(pooled)]
        for sl, z in zip(sls, zs):
            o_ref[:, sl] = (z * s_ref[:, sl]).astype(BF16)

    per = tm // HALO
    return pl.pallas_call(
        body, name=name, grid=(S // tm,),
        in_specs=[pl.BlockSpec((tm, W), lambda i: (i, u_col)),
                  pl.BlockSpec((HALO, W), lambda i: (jnp.maximum(i * per - 1, 0), u_col)),
                  pl.BlockSpec((len(POOL_WINDOWS), G, G), lambda i: (0, 0, 0)),
                  pl.BlockSpec((1, W), lambda i: (0, 0))],
        out_specs=pl.BlockSpec((tm, W), lambda i: (i, 0)),
        out_shape=jax.ShapeDtypeStruct((S, W), BF16),
        compiler_params=_cp("parallel"),
    )(u, u, pool_w, pool_scale)


def _pool_bwd(u, u_col, dy, dy_col, pool_w, pool_scale, *, name, tm=256):
    S, W = u.shape[0], POOL_WIDTH
    G = POOL_GROUP_DIM
    nt = S // tm

    def body(u_ref, h_ref, dy_ref, dyn_ref, w_ref, s_ref, du_ref, gw_ref, gs_ref):
        i = pl.program_id(0)

        @pl.when(i == 0)
        def _():
            gw_ref[...] = jnp.zeros_like(gw_ref)
            gs_ref[...] = jnp.zeros_like(gs_ref)

        uv = u_ref[...]
        halo = jnp.where(i > 0, h_ref[...], 0.0)
        dyv = dy_ref[...]
        dyn = jnp.where(i < nt - 1, dyn_ref[...], 0.0)
        rr = lax.broadcasted_iota(jnp.int32, (tm, tm + HALO), 0)
        cc = lax.broadcasted_iota(jnp.int32, (tm, tm + HALO), 1)
        rows_ext = i * tm + lax.broadcasted_iota(jnp.int32, (tm + HALO, 1), 0)
        groups = list(enumerate(POOL_WINDOWS))
        sls = [slice(g * G, (g + 1) * G) for g, _ in groups]
        wgs = [w_ref[g].astype(BF16) for g, _ in groups]
        pooled = [_pooled(uv[:, sl], halo[:, sl], w, i * tm, tm).astype(BF16) for sl, (_, w) in zip(sls, groups)]
        dzs = [dyv[:, sl] * s_ref[:, sl] for sl in sls]
        dz_ext = [jnp.concatenate([dz, dyn[:, sl] * s_ref[:, sl]], axis=0).astype(BF16) for dz, sl in zip(dzs, sls)]
        dp_ext = [_dot(d, wg, NT) for d, wg in zip(dz_ext, wgs)]
        zs = [_dot(p, wg, NN) for p, wg in zip(pooled, wgs)]
        for (g, w), sl, p, dz, z, dp in zip(groups, sls, pooled, dzs, zs, dp_ext):
            gw_ref[g] += _dot(p, dz.astype(BF16), TN)
            gs_ref[:, sl] += jnp.sum(dyv[:, sl] * z, axis=0, keepdims=True)
            inv_ext = 1.0 / jnp.minimum(rows_ext + 1, w).astype(F32)
            hi, lo = _split_bf16(dp * inv_ext)
            ahead = cc - rr
            win = ((ahead >= 0) & (ahead < w)).astype(BF16)
            du_ref[:, sl] = (_dot(win, hi, NN) + _dot(win, lo, NN) - dp[:tm]).astype(BF16)

    per = tm // HALO
    nh = S // HALO
    return pl.pallas_call(
        body, name=name, grid=(nt,),
        in_specs=[pl.BlockSpec((tm, W), lambda i: (i, u_col)),
                  pl.BlockSpec((HALO, W), lambda i: (jnp.maximum(i * per - 1, 0), u_col)),
                  pl.BlockSpec((tm, W), lambda i: (i, dy_col)),
                  pl.BlockSpec((HALO, W), lambda i: (jnp.minimum((i + 1) * per, nh - 1), dy_col)),
                  pl.BlockSpec((len(POOL_WINDOWS), G, G), lambda i: (0, 0, 0)),
                  pl.BlockSpec((1, W), lambda i: (0, 0))],
        out_specs=[pl.BlockSpec((tm, W), lambda i: (i, 0)),
                   pl.BlockSpec((len(POOL_WINDOWS), G, G), lambda i: (0, 0, 0)),
                   pl.BlockSpec((1, W), lambda i: (0, 0))],
        out_shape=[jax.ShapeDtypeStruct((S, W), BF16),
                   jax.ShapeDtypeStruct((len(POOL_WINDOWS), G, G), F32),
                   jax.ShapeDtypeStruct((1, W), F32)],
        compiler_params=_cp("arbitrary"),
    )(u, u, dy, dy, pool_w, pool_scale)


GELU_K0 = math.sqrt(2.0 / math.pi)
GELU_K1 = 0.044715


def _gelu_parts(x):
    x2 = x * x
    t = jnp.tanh(x * (GELU_K0 + (GELU_K0 * GELU_K1) * x2))
    hp = 0.5 + 0.5 * t
    gelu = x * hp
    dgelu = hp + (x * (hp * (1.0 - t))) * (GELU_K0 + (3.0 * GELU_K0 * GELU_K1) * x2)
    return gelu, dgelu


def _shifted(ext, halo):
    return (pltpu.roll(ext, 2, 0)[halo:], pltpu.roll(ext, 1, 0)[halo:], ext[halo:])


def _conv(sh, w, b):
    return b + (sh[0] * w[0:1] + sh[1] * w[1:2] + sh[2] * w[2:3])


F32_ROWS = 8


def _ffn_up_glu(h, w_up_t, conv_w, conv_b, *, name, tm=4096, tn=256, sub=256):
    S, K = h.shape
    F = D_FF
    nj = F // tn

    def body(h_ref, wg_ref, wv_ref, cwg_ref, cwv_ref, cbg_ref, cbv_ref,
             ug_ref, uv_ref, cg_ref, cv_ref, y_ref, carry):
        i = pl.program_id(0)
        j = pl.program_id(1)

        w_cat = jnp.concatenate([wg_ref[...], wv_ref[...]], axis=0)
        conv_w_b = ((cwg_ref[...], cbg_ref[...]), (cwv_ref[...], cbv_ref[...]))
        halo = [jnp.where(i > 0, carry[j, s], 0.0) for s in range(2)]
        u_next = _dot(h_ref[0:sub, :], w_cat, NT)
        for a in range(0, tm, sub):
            u16 = u_next.astype(BF16)
            if a + sub < tm:
                u_next = _dot(h_ref[a + sub:a + 2 * sub, :], w_cat, NT)
            ug_ref[a:a + sub, :] = u16[:, :tn]
            uv_ref[a:a + sub, :] = u16[:, tn:]
            c = []
            for s, (cw, cb) in enumerate(conv_w_b):
                u = u16[:, s * tn:(s + 1) * tn].astype(F32)
                ext = jnp.concatenate([halo[s], u], axis=0)
                c.append(_conv(_shifted(ext, F32_ROWS), cw, cb))
                halo[s] = u[sub - F32_ROWS:]
            cg_ref[a:a + sub, :] = c[0].astype(BF16)
            cv_ref[a:a + sub, :] = c[1].astype(BF16)
            gelu, _ = _gelu_parts(c[0])
            y_ref[a:a + sub, :] = (gelu * c[1]).astype(BF16)
        for s in range(2):
            carry[j, s] = halo[s]

    tile = pl.BlockSpec((tm, tn), lambda i, j: (i, j))
    vec = lambda rows, off: pl.BlockSpec((rows, tn), lambda i, j: (0, j + off))
    return pl.pallas_call(
        body, name=name, grid=(S // tm, nj),
        in_specs=[pl.BlockSpec((tm, K), lambda i, j: (i, 0)),
                  pl.BlockSpec((tn, K), lambda i, j: (j, 0)), pl.BlockSpec((tn, K), lambda i, j: (j + nj, 0)),
                  vec(3, 0), vec(3, nj), vec(1, 0), vec(1, nj)],
        out_specs=[tile] * 5,
        out_shape=[jax.ShapeDtypeStruct((S, F), BF16)] * 5,
        scratch_shapes=[pltpu.VMEM((nj, 2, F32_ROWS, tn), F32)],
        compiler_params=_cp("arbitrary", "arbitrary"),
    )(h, w_up_t, w_up_t, conv_w, conv_w, conv_b, conv_b)


def _ffn_glu_bwd(u_g, u_v, c_g, c_v, df, w_down, h, conv_w, *, name, tm=2048, tn=256, sub=256):
    S = u_g.shape[0]
    F = D_FF
    D = df.shape[1]
    nj = F // tn
    nt = S // tm

    def body(ug_ref, uv_ref, cg_ref, cgn_ref, cv_ref, cvn_ref, df_ref, dfn_ref, wd_ref, h_ref, wg_ref, wv_ref,
             dug_ref, duv_ref, gug_ref, guv_ref, gd_ref, gwg_ref, gwv_ref, gbg_ref, gbv_ref,
             acc_u, acc_d):
        i = pl.program_id(1)

        @pl.when(i == 0)
        def _():
            for r in (gwg_ref, gwv_ref, gbg_ref, gbv_ref, acc_u, acc_d):
                r[...] = jnp.zeros_like(r)

        wg, wv = wg_ref[...], wv_ref[...]
        wd = wd_ref[...]
        dfn = jnp.where(i < nt - 1, dfn_ref[...], jnp.zeros_like(dfn_ref))
        n_ext = sub + HALO

        def ahead(dc):
            return dc[:sub], pltpu.roll(dc, n_ext - 1, 0)[:sub], pltpu.roll(dc, n_ext - 2, 0)[:sub]

        def ext(ref, nxt, a):
            b = a + sub
            return jnp.concatenate([ref[a:b, :], ref[b:b + HALO, :] if b < tm else nxt], axis=0)

        dy_next = _dot(ext(df_ref, dfn, 0), wd, NT)
        for a in range(0, tm, sub):
            b = a + sub
            dy_ext = dy_next
            if b < tm:
                dy_next = _dot(ext(df_ref, dfn, b), wd, NT)
            cg = ext(cg_ref, cgn_ref[...], a).astype(F32)
            cv = ext(cv_ref, cvn_ref[...], a).astype(F32)
            df_sub = df_ref[a:b, :]
            gelu, dgelu = _gelu_parts(cg)
            dcs_g = ahead(dy_ext * cv * dgelu)
            dcs_v = ahead(dy_ext * gelu)
            du_g = (dcs_g[0] * wg[2:3] + dcs_g[1] * wg[1:2] + dcs_g[2] * wg[0:1]).astype(BF16)
            du_v = (dcs_v[0] * wv[2:3] + dcs_v[1] * wv[1:2] + dcs_v[2] * wv[0:1]).astype(BF16)
            dug_ref[a:b, :] = du_g
            duv_ref[a:b, :] = du_v
            acc_u[...] += _dot(jnp.concatenate([du_g, du_v], axis=1), h_ref[a:b, :], TN)
            acc_d[...] += _dot((gelu[:sub] * cv[:sub]).astype(BF16), df_sub, TN)
            for dcs, u_ref, gw_ref, gb_ref in ((dcs_g, ug_ref, gwg_ref, gbg_ref), (dcs_v, uv_ref, gwv_ref, gbv_ref)):
                u = u_ref[a:b, :].astype(F32)
                gb_ref[...] += jnp.sum(dcs[0], axis=0, keepdims=True)
                for k in range(3):
                    gw_ref[k:k + 1, :] += jnp.sum(dcs[2 - k] * u, axis=0, keepdims=True)

        @pl.when(i == nt - 1)
        def _():
            gug_ref[...] = acc_u[:tn, :].astype(BF16)
            guv_ref[...] = acc_u[tn:, :].astype(BF16)
            gd_ref[...] = acc_d[...].astype(BF16)

    per = tm // HALO
    nh = S // HALO
    hnext = lambda i: jnp.minimum((i + 1) * per, nh - 1)
    tile = pl.BlockSpec((tm, tn), lambda j, i: (i, j))
    hn = pl.BlockSpec((HALO, tn), lambda j, i: (hnext(i), j))
    vec = lambda rows, off: pl.BlockSpec((rows, tn), lambda j, i: (0, j + off))
    wide = pl.BlockSpec((tm, D), lambda j, i: (i, 0))
    wrow = pl.BlockSpec((tn, D), lambda j, i: (j, 0))
    return pl.pallas_call(
        body, name=name, grid=(nj, nt),
        in_specs=[tile, tile, tile, hn, tile, hn, wide, pl.BlockSpec((HALO, D), lambda j, i: (hnext(i), 0)),
                  wrow, wide, vec(3, 0), vec(3, nj)],
        out_specs=[tile, tile, wrow, wrow, wrow, vec(3, 0), vec(3, 0), vec(1, 0), vec(1, 0)],
        out_shape=[jax.ShapeDtypeStruct((S, F), BF16), jax.ShapeDtypeStruct((S, F), BF16),
                   jax.ShapeDtypeStruct((F, D), BF16), jax.ShapeDtypeStruct((F, D), BF16),
                   jax.ShapeDtypeStruct((F, D), BF16),
                   jax.ShapeDtypeStruct((3, F), F32), jax.ShapeDtypeStruct((3, F), F32),
                   jax.ShapeDtypeStruct((1, F), F32), jax.ShapeDtypeStruct((1, F), F32)],
        scratch_shapes=[pltpu.VMEM((2 * tn, D), F32), pltpu.VMEM((tn, D), F32)],
        compiler_params=_cp("parallel", "arbitrary"),
    )(u_g, u_v, c_g, c_g, c_v, c_v, df, df, w_down, h, conv_w, conv_w)


def _sum_partials(parts, *, name, tr):
    _, R, C = parts.shape

    def body(p_ref, o_ref):
        tot = p_ref[0].astype(F32)
        for j in range(1, N_DEV):
            tot = tot + p_ref[j].astype(F32)
        o_ref[...] = tot

    return pl.pallas_call(
        body, name=name, grid=(R // tr,),
        in_specs=[pl.BlockSpec((N_DEV, tr, C), lambda i: (0, i, 0))],
        out_specs=pl.BlockSpec((tr, C), lambda i: (i, 0)),
        out_shape=jax.ShapeDtypeStruct((R, C), F32),
        compiler_params=_cp("parallel"),
    )(parts)


def _adamw(w, g, m, v, *, name, tr):
    R, C = w.shape
    c1 = 1.0 - ADAM_B1 ** ADAM_STEP
    c2 = 1.0 - ADAM_B2 ** ADAM_STEP

    def body(w_ref, g_ref, m_ref, v_ref, d_ref, nm_ref, nv_ref):
        g = g_ref[...]
        nm = ADAM_B1 * m_ref[...] + (1.0 - ADAM_B1) * g
        nv = ADAM_B2 * v_ref[...] + (1.0 - ADAM_B2) * (g * g)
        d_ref[...] = -ADAM_LR * ((nm / c1) / (jnp.sqrt(nv / c2) + ADAM_EPS) + ADAM_WD * w_ref[...])
        nm_ref[...] = nm
        nv_ref[...] = nv

    spec = pl.BlockSpec((tr, C), lambda i: (i, 0))
    return pl.pallas_call(
        body, name=name, grid=(R // tr,), in_specs=[spec] * 4, out_specs=[spec] * 3,
        out_shape=[jax.ShapeDtypeStruct((R, C), F32)] * 3,
        compiler_params=_cp("parallel"),
    )(w, g, m, v)


def _mesh_pos():
    return lax.axis_index("x"), lax.axis_index("y"), lax.axis_index("c")


def _gather_phases(x_refs, out_refs, send_sems, recv_sems, local_sems):
    x, y, c = _mesh_pos()
    me, sibling = (x, y, c), (x, y, 1 - c)
    chips = [(1 - x, y), (x, 1 - y), (1 - x, 1 - y)]
    arrays = range(len(x_refs))

    def slot(a, px, py, pc):
        return out_refs[a].at[4 * px + 2 * py + pc]

    def copy(a, k, block, to, own=False):
        return pltpu.make_async_remote_copy(
            src_ref=x_refs[a] if own else slot(a, *block), dst_ref=slot(a, *block),
            send_sem=send_sems.at[a, k], recv_sem=recv_sems.at[a, k], device_id=to, device_id_type=MESH)

    mine = [pltpu.make_async_copy(x_refs[a], slot(a, *me), local_sems.at[a]) for a in arrays]
    first = [copy(a, 0, me, sibling, own=True) for a in arrays]
    first += [copy(a, 1 + j, me, (*chip, c), own=True) for j, chip in enumerate(chips) for a in arrays]
    passed = [[copy(a, 4 + j, (*chip, c), sibling) for a in arrays] for j, chip in enumerate(chips)]

    def start():
        for cp in mine + first:
            cp.start()

    def forward():
        for j, chip in enumerate(chips):
            for a in arrays:
                copy(a, 1 + j, (*chip, c), me).wait_recv()
                passed[j][a].start()

    def finish():
        for a in arrays:
            copy(a, 0, sibling, me).wait_recv()
            for j, chip in enumerate(chips):
                copy(a, 4 + j, (*chip, 1 - c), me).wait_recv()
        for cp in first + [cp for row in passed for cp in row]:
            cp.wait_send()
        for cp in mine:
            cp.wait()

    return start, forward, finish


def _gather_sems(n):
    return [pltpu.SemaphoreType.DMA((n, 7)), pltpu.SemaphoreType.DMA((n, 7)), pltpu.SemaphoreType.DMA((n,))]


def _gathered_shapes(blocks):
    return [jax.ShapeDtypeStruct((N_DEV,) + b.shape, b.dtype) for b in blocks]


def _all_reduce_small(block, gathered, *, name):
    r0, r1 = block.shape[0], gathered.shape[1]

    def body(x_ref, more_ref, all_ref, sum_ref, *sems):
        for phase in _gather_phases([x_ref], [all_ref], *sems):
            phase()
        for ref, rows in ((all_ref, slice(0, r0)), (more_ref, slice(r0, r0 + r1))):
            tot = ref[0]
            for j in range(1, N_DEV):
                tot = tot + ref[j]
            sum_ref[rows, :] = tot

    return pl.pallas_call(
        body, name=name, in_specs=[VMEM, VMEM], out_specs=[VMEM, VMEM],
        out_shape=[jax.ShapeDtypeStruct((N_DEV,) + block.shape, block.dtype),
                   jax.ShapeDtypeStruct((r0 + r1, block.shape[1]), block.dtype)],
        scratch_shapes=_gather_sems(1),
        compiler_params=pltpu.CompilerParams(vmem_limit_bytes=V7X_VMEM_LIMIT),
    )(block, gathered)[1]


def _exchange_phases(g_refs, r_refs, send_sems, recv_sems, local_sems):
    x, y, c = _mesh_pos()
    me = 4 * x + 2 * y + c
    owns, remote = [], []
    for k, (g_ref, r_ref) in enumerate(zip(g_refs, r_refs)):
        rows = g_ref.shape[0] // N_DEV
        owns.append(pltpu.make_async_copy(g_ref.at[pl.ds(me * rows, rows)], r_ref.at[me], local_sems.at[k]))
        for p in range(1, N_DEV):
            px, py, pc = x ^ (p >> 2), y ^ ((p >> 1) & 1), c ^ (p & 1)
            peer = 4 * px + 2 * py + pc
            link = dict(send_sem=send_sems.at[k, p], recv_sem=recv_sems.at[k, p],
                        device_id=(px, py, pc), device_id_type=MESH)
            src = g_ref.at[pl.ds(peer * rows, rows)]
            send = pltpu.make_async_remote_copy(src_ref=src, dst_ref=r_ref.at[me], **link)
            arrival = pltpu.make_async_remote_copy(src_ref=src, dst_ref=r_ref.at[peer], **link)
            remote.append((send, arrival))

    def start():
        for own in owns:
            own.start()
        for send, _ in remote:
            send.start()

    def finish():
        for _, arrival in remote:
            arrival.wait_recv()
        for send, _ in remote:
            send.wait_send()
        for own in owns:
            own.wait()

    return start, finish


def _exchange_buffers(grads):
    n = len(grads)
    shapes = [jax.ShapeDtypeStruct((N_DEV, g.shape[0] // N_DEV, g.shape[1]), g.dtype) for g in grads]
    sems = [pltpu.SemaphoreType.DMA((n, N_DEV)), pltpu.SemaphoreType.DMA((n, N_DEV)),
            pltpu.SemaphoreType.DMA((n,))]
    return shapes, sems


def _unpack_gathered(gathered):
    w_out, w_up_t, w_down = (g.reshape(-1, D_MODEL) for g in gathered[:3])
    width = 2 * D_FF // N_DEV
    conv_w = jnp.transpose(gathered[3][:, :3, :width], (1, 0, 2)).reshape(3, 2 * D_FF)
    return w_out, w_up_t, w_down, conv_w


def _rest_payload(w_out, w_up, w_down, conv_w):
    rows, cols = conv_w.shape
    conv_w = jnp.pad(conv_w, ((0, (-rows) % F32_ROWS), (0, (-cols) % LANES)))
    return [w_out.astype(BF16), w_up.T.astype(BF16), w_down.astype(BF16), conv_w]


def _device_step(x, target, g_mix_pre, w_in_t_block, rest_payload, pool_w, pool_scale, g_mix_post, g_ffn_pre,
                 conv_b, g_ffn_post):
    h1, w_in_t = _rms_norm_gather(x, g_mix_pre, w_in_t_block, name="rms_mix_pre")
    w_in_t = w_in_t.reshape(-1, D_MODEL)
    proj = _matmul(h1, w_in_t, trans_b=True, out_dtype=F32, tm=1024, tn=4 * ATTN_WIDTH, name="proj")
    attn, lse, attn16, gathered = _attn_fwd(proj, rest_payload, name="attn_fwd")
    w_out, w_up_t, w_down, conv_w = _unpack_gathered(gathered)
    pool = _pool_fwd(proj, 3, pool_w, pool_scale, name="pool_fwd")
    mixed, x2, h2 = _mix_out(attn16, pool, w_out, x, g_mix_post, g_ffn_pre, name="mix_out")
    u_g, u_v, c_g, c_v, y = _ffn_up_glu(h2, w_up_t, conv_w, conv_b, name="ffn_up_glu")
    df, d_out, loss_blk, gg_ffn_post = _ffn_out(y, w_down, x2, target, g_ffn_post, name="ffn_out")
    du_g, du_v, gw_up_g, gw_up_v, gw_down, gcw_g, gcw_v, gcb_g, gcb_v = _ffn_glu_bwd(
        u_g, u_v, c_g, c_v, df, w_down, h2, conv_w, name="ffn_glu_bwd")
    gw_up_t = jnp.concatenate([gw_up_g, gw_up_v], axis=0)
    dx2, gg_ffn_pre, dmixed, gg_mix_post = _dgrad_norm(
        [du_g, du_v], w_up_t, d_out, x2, g_ffn_pre, (mixed, g_mix_post), [], name="ffn_up_dgrad")
    gw_out = _matmul_tn([attn16, pool], dmixed, name="grad_w_out")
    dcat = _matmul(dmixed, w_out, trans_b=True, out_dtype=F32, tm=1024, tn=1024, name="mix_out_dgrad")
    d_pool_in, g_pool_w, g_pool_scale = _pool_bwd(proj, 3, dcat, 1, pool_w, pool_scale, name="pool_bwd")
    early = dict(g_mix_post=gg_mix_post, g_ffn_pre=gg_ffn_pre, g_ffn_post=gg_ffn_post, pool_scale=g_pool_scale,
                 conv_b=jnp.concatenate([gcb_g, gcb_v], axis=1), pool_w=g_pool_w)
    early_block = _pack_rows([early[k] for k in _SMALL[1:]] + [jnp.concatenate([gcw_g, gcw_v], axis=1), loss_blk])
    dqkv, (r_out, r_up_t, r_down), (small_gathered,) = _attn_bwd(
        proj, dcat, attn, lse, [gw_out, gw_up_t, gw_down], [early_block], name="attn_bwd")
    dproj = list(dqkv) + [d_pool_in]
    gw_in_t = _matmul_tn(dproj, h1, name="grad_w_in")
    grad_x, gg_mix_pre, (r_in_t,) = _dgrad_norm(dproj, w_in_t, dx2, x, g_mix_pre, None, [gw_in_t], name="proj_dgrad")
    received = (r_in_t, r_out, r_up_t, r_down)
    return grad_x, received, gg_mix_pre, small_gathered


_SMALL = ("g_mix_pre", "g_mix_post", "g_ffn_pre", "g_ffn_post", "pool_scale", "conv_b", "pool_w")
LANES = 128


def _pack_rows(arrays):
    parts = []
    for a in arrays:
        a2 = a.reshape(-1, LANES)
        parts.append(jnp.pad(a2, ((0, (-a2.shape[0]) % 8), (0, 0))))
    return jnp.concatenate(parts, axis=0)


def _unpack_rows(packed, shapes):
    out, row = [], 0
    for shape in shapes:
        rows = math.prod(shape) // LANES
        out.append(packed[row:row + rows].reshape(shape))
        row += -(-rows // 8) * 8
    return out


def kernel(x, g_mix_pre, w_in, pool_w, pool_scale, w_out, g_mix_post, g_ffn_pre, w_up, conv_w, conv_b, w_down, g_ffn_post, loss_target, m_g_mix_pre, m_w_in, m_pool_w, m_pool_scale, m_w_out, m_g_mix_post, m_g_ffn_pre, m_w_up, m_conv_w, m_conv_b, m_w_down, m_g_ffn_post, v_g_mix_pre, v_w_in, v_pool_w, v_pool_scale, v_w_out, v_g_mix_post, v_g_ffn_pre, v_w_up, v_conv_w, v_conv_b, v_w_down, v_g_ffn_post):
    me = 4 * lax.axis_index("x") + 2 * lax.axis_index("y") + lax.axis_index("c")
    grad_x, recv, gg_mix_pre, small_gathered = _device_step(
        x[0], loss_target[0], g_mix_pre, w_in[0].T.astype(BF16),
        _rest_payload(w_out[0], w_up[0], w_down[0], conv_w[0]),
        pool_w[0], pool_scale, g_mix_post, g_ffn_pre, conv_b, g_ffn_post)

    g_in_t, g_out, g_up_t, g_down = (
        _sum_partials(r, name=f"sum_partials_{k}", tr=r.shape[1] // 2) for k, r in enumerate(recv))
    grads = {"w_in": g_in_t.T, "w_out": g_out, "w_up": g_up_t.T, "w_down": g_down}

    given = dict(g_mix_pre=g_mix_pre, g_mix_post=g_mix_post, g_ffn_pre=g_ffn_pre, g_ffn_post=g_ffn_post,
                 pool_scale=pool_scale, conv_b=conv_b, pool_w=pool_w)
    small_shapes = [given[k].shape for k in _SMALL]
    total = _all_reduce_small(_pack_rows([gg_mix_pre]), small_gathered, name="all_reduce_small")
    *small_grads, g_conv_w_all, loss_all = _unpack_rows(total, small_shapes + [(3, 2 * D_FF), (8, LANES)])
    loss = loss_all[0, 0]
    grads.update(zip(_SMALL, small_grads))
    width = 2 * D_FF // N_DEV
    grads["conv_w"] = lax.dynamic_slice_in_dim(g_conv_w_all, me * width, width, axis=1)[None]

    weights = dict(g_mix_pre=g_mix_pre, w_in=w_in, pool_w=pool_w, pool_scale=pool_scale, w_out=w_out,
                   g_mix_post=g_mix_post, g_ffn_pre=g_ffn_pre, w_up=w_up, conv_w=conv_w, conv_b=conv_b,
                   w_down=w_down, g_ffn_post=g_ffn_post)
    m_in = dict(g_mix_pre=m_g_mix_pre, w_in=m_w_in, pool_w=m_pool_w, pool_scale=m_pool_scale, w_out=m_w_out,
                g_mix_post=m_g_mix_post, g_ffn_pre=m_g_ffn_pre, w_up=m_w_up, conv_w=m_conv_w, conv_b=m_conv_b,
                w_down=m_w_down, g_ffn_post=m_g_ffn_post)
    v_in = dict(g_mix_pre=v_g_mix_pre, w_in=v_w_in, pool_w=v_pool_w, pool_scale=v_pool_scale, w_out=v_w_out,
                g_mix_post=v_g_mix_post, g_ffn_pre=v_g_ffn_pre, w_up=v_w_up, conv_w=v_conv_w, conv_b=v_conv_b,
                w_down=v_w_down, g_ffn_post=v_g_ffn_post)
    delta, new_m, new_v = {}, {}, {}
    for k in ("w_in", "w_out", "w_up", "w_down"):
        g = grads[k]
        d, nm, nv = _adamw(weights[k][0], g, m_in[k][0], v_in[k][0], name=f"adamw_{k}", tr=g.shape[0] // 2)
        grads[k], delta[k], new_m[k], new_v[k] = g[None], d[None], nm[None], nv[None]
    d, nm, nv = _adamw(weights["conv_w"][0], grads["conv_w"][0], m_in["conv_w"][0], v_in["conv_w"][0],
                       name="adamw_conv_w", tr=3)
    delta["conv_w"], new_m["conv_w"], new_v["conv_w"] = d[None], nm[None], nv[None]
    packed_w = _pack_rows([weights[k] for k in _SMALL])
    small_rows = packed_w.shape[0]
    d, nm, nv = _adamw(packed_w, total[:small_rows], _pack_rows([m_in[k] for k in _SMALL]),
                       _pack_rows([v_in[k] for k in _SMALL]), name="adamw_small", tr=small_rows)
    for k, dk, mk, vk in zip(_SMALL, _unpack_rows(d, small_shapes), _unpack_rows(nm, small_shapes),
                             _unpack_rows(nv, small_shapes)):
        delta[k], new_m[k], new_v[k] = dk, mk, vk

    order = ("g_mix_pre", "w_in", "pool_w", "pool_scale", "w_out", "g_mix_post", "g_ffn_pre", "w_up",
             "conv_w", "conv_b", "w_down", "g_ffn_post")
    return (loss, grad_x[None], *[grads[k] for k in order], *[delta[k] for k in order],
            *[new_m[k] for k in order], *[new_v[k] for k in order])
```

```python
import functools
import math

import jax
import jax.numpy as jnp
from jax import lax
from jax.experimental import pallas as pl
from jax.experimental.pallas import tpu as pltpu

F32 = jnp.float32
BF16 = jnp.bfloat16

D_MODEL = 1024
N_HEADS = 8
HEAD_DIM = 64
ATTN_WIDTH = N_HEADS * HEAD_DIM
DILATIONS = (1, 4, 16)
BLOCK = 128
POOL_WIDTH = 512
POOL_WINDOWS = (2, 4, 8, 16)
POOL_GROUP_DIM = 128
D_FF = 2816
EPS = 1e-6
NEG_INF = -1e30
SCALE = HEAD_DIM ** -0.5

ADAM_LR = 0.001
ADAM_B1 = 0.9
ADAM_B2 = 0.999
ADAM_EPS = 1e-08
ADAM_WD = 0.01
ADAM_STEP = 10

N_DEV = 8
HALO = 16
V7X_VMEM_LIMIT = 56 * 1024 * 1024

MESH = pl.DeviceIdType.MESH
ANY = pl.BlockSpec(memory_space=pl.ANY)
VMEM = pl.BlockSpec(memory_space=pltpu.VMEM)

NT = (((1,), (1,)), ((), ()))
NN = (((1,), (0,)), ((), ()))
TN = (((0,), (0,)), ((), ()))


def _cp(*sem):
    return pltpu.CompilerParams(dimension_semantics=sem, vmem_limit_bytes=V7X_VMEM_LIMIT)


def _dot(a, b, dn):
    return lax.dot_general(a, b, dn, preferred_element_type=F32)


def _rms_bwd(xin, g, dy):
    r = lax.rsqrt(jnp.mean(xin * xin, axis=-1, keepdims=True) + EPS)
    xh = xin * r
    gdy = g * dy
    dx = r * (gdy - xh * jnp.mean(gdy * xh, axis=-1, keepdims=True))
    dg = jnp.sum(dy * xh, axis=0, keepdims=True)
    return dx, dg


def _rms_norm_gather(x, g, block, *, name, tm=512):
    S, D = x.shape
    nt = S // tm

    def body(x_ref, g_ref, blk_ref, o_ref, all_ref, *sems):
        i = pl.program_id(0)
        start, forward, finish = _gather_phases([blk_ref], [all_ref], *sems)
        pl.when(i == 0)(start)
        xv = x_ref[...]
        r = lax.rsqrt(jnp.mean(xv * xv, axis=-1, keepdims=True) + EPS)
        o_ref[...] = (xv * r * g_ref[...]).astype(BF16)
        pl.when(i == nt - 1)(forward)
        pl.when(i == nt - 1)(finish)

    return pl.pallas_call(
        body, name=name, grid=(nt,),
        in_specs=[pl.BlockSpec((tm, D), lambda i: (i, 0)), pl.BlockSpec((1, D), lambda i: (0, 0)), ANY],
        out_specs=[pl.BlockSpec((tm, D), lambda i: (i, 0)), ANY],
        out_shape=[jax.ShapeDtypeStruct((S, D), BF16)] + _gathered_shapes([block]),
        scratch_shapes=_gather_sems(1),
        compiler_params=_cp("arbitrary"),
    )(x, g, block)


def _matmul(a, b, *, trans_b, out_dtype, tm, tn, name):
    M, K = a.shape
    N = b.shape[0] if trans_b else b.shape[1]
    dn = NT if trans_b else NN

    def body(a_ref, b_ref, o_ref):
        o_ref[...] = _dot(a_ref[...], b_ref[...], dn).astype(out_dtype)

    b_spec = (pl.BlockSpec((tn, K), lambda i, j: (j, 0)) if trans_b
              else pl.BlockSpec((K, tn), lambda i, j: (0, j)))
    return pl.pallas_call(
        body, name=name, grid=(M // tm, N // tn),
        in_specs=[pl.BlockSpec((tm, K), lambda i, j: (i, 0)), b_spec],
        out_specs=pl.BlockSpec((tm, tn), lambda i, j: (i, j)),
        out_shape=jax.ShapeDtypeStruct((M, N), out_dtype),
        compiler_params=_cp("parallel", "parallel"),
    )(a, b)


def _matmul_tn(a_list, b, *, name, ts=1024):
    S, Ka = a_list[0].shape
    na = len(a_list)
    Nb = b.shape[1]
    ns = S // ts

    def body(*refs):
        a_refs, b_ref, o_ref, acc = refs[:na], refs[na], refs[na + 1], refs[na + 2]
        s = pl.program_id(0)

        @pl.when(s == 0)
        def _():
            acc[...] = jnp.zeros_like(acc)

        acc[...] += _dot(jnp.concatenate([r[...] for r in a_refs], axis=1), b_ref[...], TN)

        @pl.when(s == ns - 1)
        def _():
            o_ref[...] = acc[...].astype(BF16)

    return pl.pallas_call(
        body, name=name, grid=(ns,),
        in_specs=[pl.BlockSpec((ts, Ka), lambda s: (s, 0))] * na + [pl.BlockSpec((ts, Nb), lambda s: (s, 0))],
        out_specs=pl.BlockSpec((na * Ka, Nb), lambda s: (0, 0)),
        out_shape=jax.ShapeDtypeStruct((na * Ka, Nb), BF16),
        scratch_shapes=[pltpu.VMEM((na * Ka, Nb), F32)],
        compiler_params=_cp("arbitrary"),
    )(*a_list, b)


def _mix_out(attn, pool, w_out, x, g_post, g_next, *, name, tm=512):
    S, K = attn.shape
    D = w_out.shape[1]

    def body(a_ref, p_ref, w_ref, x_ref, gp_ref, gn_ref, mixed_ref, x2_ref, h2_ref):
        mixed = _dot(a_ref[...], w_ref[:K, :], NN) + _dot(p_ref[...], w_ref[K:, :], NN)
        r = lax.rsqrt(jnp.mean(mixed * mixed, axis=-1, keepdims=True) + EPS)
        x2 = x_ref[...] + mixed * r * gp_ref[...]
        r2 = lax.rsqrt(jnp.mean(x2 * x2, axis=-1, keepdims=True) + EPS)
        mixed_ref[...] = mixed
        x2_ref[...] = x2
        h2_ref[...] = (x2 * r2 * gn_ref[...]).astype(BF16)

    row = lambda i: (i, 0)
    fix = lambda i: (0, 0)
    return pl.pallas_call(
        body, name=name, grid=(S // tm,),
        in_specs=[pl.BlockSpec((tm, K), row), pl.BlockSpec((tm, K), row), pl.BlockSpec((2 * K, D), fix),
                  pl.BlockSpec((tm, D), row), pl.BlockSpec((1, D), fix), pl.BlockSpec((1, D), fix)],
        out_specs=[pl.BlockSpec((tm, D), row)] * 3,
        out_shape=[jax.ShapeDtypeStruct((S, D), F32), jax.ShapeDtypeStruct((S, D), F32),
                   jax.ShapeDtypeStruct((S, D), BF16)],
        compiler_params=_cp("parallel"),
    )(attn, pool, w_out, x, g_post, g_next)


def _ffn_out(y, w_down, x2, target, g_post, *, name, tm=512, sub=256):
    S, K = y.shape
    D = w_down.shape[1]

    def body(y_ref, w_ref, x2_ref, t_ref, g_ref, df_ref, dout_ref, loss_ref, gg_ref):
        i = pl.program_id(0)

        @pl.when(i == 0)
        def _():
            loss_ref[...] = jnp.zeros_like(loss_ref)
            gg_ref[...] = jnp.zeros_like(gg_ref)

        g = g_ref[...]
        w = w_ref[...]
        f_next = _dot(y_ref[0:sub, :], w, NN)
        for a in range(0, tm, sub):
            rows = slice(a, a + sub)
            f = f_next
            if a + sub < tm:
                f_next = _dot(y_ref[a + sub:a + 2 * sub, :], w, NN)
            r = lax.rsqrt(jnp.mean(f * f, axis=-1, keepdims=True) + EPS)
            out = x2_ref[rows, :] + f * r * g
            err = out - t_ref[rows, :]
            dy = err * (1.0 / D)
            df, dg = _rms_bwd(f, g, dy)
            df_ref[rows, :] = df.astype(BF16)
            dout_ref[rows, :] = dy
            gg_ref[...] += dg
            loss_ref[...] += 0.5 * jnp.sum(jnp.mean(err * err, axis=-1, keepdims=True))

    row = lambda i: (i, 0)
    fix = lambda i: (0, 0)
    return pl.pallas_call(
        body, name=name, grid=(S // tm,),
        in_specs=[pl.BlockSpec((tm, K), row), pl.BlockSpec((K, D), fix), pl.BlockSpec((tm, D), row),
                  pl.BlockSpec((tm, D), row), pl.BlockSpec((1, D), fix)],
        out_specs=[pl.BlockSpec((tm, D), row), pl.BlockSpec((tm, D), row),
                   pl.BlockSpec((8, 128), fix), pl.BlockSpec((1, D), fix)],
        out_shape=[jax.ShapeDtypeStruct((S, D), BF16), jax.ShapeDtypeStruct((S, D), F32),
                   jax.ShapeDtypeStruct((8, 128), F32), jax.ShapeDtypeStruct((1, D), F32)],
        compiler_params=_cp("arbitrary"),
    )(y, w_down, x2, target, g_post)


def _dgrad_norm(a_list, w, resid, xin, g, second, exchange, *, name, tm=512, sub=256):
    S, Kp = a_list[0].shape
    na = len(a_list)
    D = w.shape[1]
    nt = S // tm
    two = second is not None
    ng = len(exchange)
    recv_shapes, exchange_sems = _exchange_buffers(exchange)

    def body(*refs):
        a_refs = refs[:na]
        w_ref, r_ref, x_ref, g_ref = refs[na:na + 4]
        pos = na + 4
        if two:
            x2_ref, g2_ref = refs[pos:pos + 2]
            pos += 2
        g_refs = refs[pos:pos + ng]
        pos += ng
        dx_ref, gg_ref = refs[pos:pos + 2]
        pos += 2
        if two:
            d2_ref, gg2_ref = refs[pos:pos + 2]
            pos += 2
        r_refs = refs[pos:pos + ng]
        pos += ng
        i = pl.program_id(0)
        if ng:
            start, finish = _exchange_phases(g_refs, r_refs, *refs[pos:])
            pl.when(i == 0)(start)

        @pl.when(i == 0)
        def _():
            gg_ref[...] = jnp.zeros_like(gg_ref)
            if two:
                gg2_ref[...] = jnp.zeros_like(gg2_ref)

        def dh_of(a):
            return functools.reduce(jnp.add, [_dot(a_refs[q][a:a + sub, :], w_ref[q * Kp:(q + 1) * Kp, :], NN)
                                              for q in range(na)])

        dh_next = dh_of(0)
        for a in range(0, tm, sub):
            rows = slice(a, a + sub)
            dh = dh_next
            if a + sub < tm:
                dh_next = dh_of(a + sub)
            d1, dg1 = _rms_bwd(x_ref[rows, :], g_ref[...], dh)
            dx = r_ref[rows, :] + d1
            dx_ref[rows, :] = dx
            gg_ref[...] += dg1
            if two:
                d2, dg2 = _rms_bwd(x2_ref[rows, :], g2_ref[...], dx)
                d2_ref[rows, :] = d2.astype(BF16)
                gg2_ref[...] += dg2
        if ng:
            pl.when(i == nt - 1)(finish)

    row = lambda i: (i, 0)
    fix = lambda i: (0, 0)
    in_specs = [pl.BlockSpec((tm, Kp), row)] * na + [
        pl.BlockSpec((na * Kp, D), fix, pipeline_mode=pl.Buffered(1)), pl.BlockSpec((tm, D), row),
        pl.BlockSpec((tm, D), row), pl.BlockSpec((1, D), fix)]
    args = list(a_list) + [w, resid, xin, g]
    out_specs = [pl.BlockSpec((tm, D), row), pl.BlockSpec((1, D), fix)]
    out_shape = [jax.ShapeDtypeStruct((S, D), F32), jax.ShapeDtypeStruct((1, D), F32)]
    if two:
        in_specs += [pl.BlockSpec((tm, D), row), pl.BlockSpec((1, D), fix)]
        args += list(second)
        out_specs += [pl.BlockSpec((tm, D), row), pl.BlockSpec((1, D), fix)]
        out_shape += [jax.ShapeDtypeStruct((S, D), BF16), jax.ShapeDtypeStruct((1, D), F32)]
    n_plain = len(out_shape)
    out = pl.pallas_call(
        body, name=name, grid=(nt,), in_specs=in_specs + [ANY] * ng, out_specs=out_specs + [ANY] * ng,
        out_shape=out_shape + recv_shapes, scratch_shapes=exchange_sems if ng else [],
        compiler_params=_cp("arbitrary"),
    )(*args, *exchange)
    return (*out[:n_plain], out[n_plain:]) if ng else out


def _band_mask(first_block):
    qi = lax.broadcasted_iota(jnp.int32, (BLOCK, 2 * BLOCK), 0)
    ki = lax.broadcasted_iota(jnp.int32, (BLOCK, 2 * BLOCK), 1)
    first_key = jnp.where(first_block, BLOCK, 0)
    return (ki >= qi) & (ki <= qi + BLOCK) & (ki >= first_key)


def _lane_masks():
    lane = lax.broadcasted_iota(jnp.int32, (1, 2 * HEAD_DIM), 1)
    return (lane < HEAD_DIM, lane >= HEAD_DIM)


CHUNK = BLOCK * max(DILATIONS)
SLAB = 2 * HEAD_DIM
N_SLABS = ATTN_WIDTH // SLAB
SOFTMAX_ROWS = 64


def _unit_rows(d, b):
    def rows(r):
        start = r + BLOCK * d * b
        return pl.ds(start, BLOCK, stride=d) if d > 1 else pl.ds(start, BLOCK)
    return rows


def _attn_units():
    for p, d in enumerate(DILATIONS):
        nbc = CHUNK // (BLOCK * d)
        for b in range(nbc):
            for r in range(d):
                yield p, d, b, r, nbc


def _attn_in_specs(nc, n_cur):
    prev = lambda c: jnp.maximum(jnp.minimum(c, nc - 1) - 1, 0)
    cur = lambda c: jnp.minimum(c, nc - 1)
    blk = lambda f: pl.BlockSpec((CHUNK, SLAB), f)
    specs = [blk(lambda h, c: (cur(c), h)),
             blk(lambda h, c: (prev(c), N_SLABS + h)), blk(lambda h, c: (cur(c), N_SLABS + h)),
             blk(lambda h, c: (prev(c), 2 * N_SLABS + h)), blk(lambda h, c: (cur(c), 2 * N_SLABS + h))]
    return specs + [blk(lambda h, c: (cur(c), h))] * n_cur


def _attn_fwd(proj, payload, *, name):
    S = proj.shape[0]
    nc = S // CHUNK
    n = len(DILATIONS)
    npay = len(payload)
    n_steps = N_SLABS * nc

    def body(*refs):
        q_ref, kp_ref, kc_ref, vp_ref, vc_ref = refs[:5]
        pay_refs = refs[5:5 + npay]
        attn_ref, lse_ref, attn16_ref = refs[5 + npay:8 + npay]
        all_refs = refs[8 + npay:8 + 2 * npay]
        scr = refs[8 + 2 * npay:]
        o_scr, l_scr = scr[:n], scr[n:2 * n]
        s_scr, e_scr, den_scr, lsew_scr = scr[2 * n:2 * n + 4]
        start, forward, finish = _gather_phases(pay_refs, all_refs, *scr[2 * n + 4:])
        step = pl.program_id(0) * nc + pl.program_id(1)
        pl.when(step == 0)(start)
        c = pl.program_id(1)
        lms = _lane_masks()
        plain, first = (jnp.tile(_band_mask(f), (2, 1)) for f in (False, c == 0))
        def scores(unit):
            p, d, b, r, nbc = unit
            rows = _unit_rows(d, b)(r)
            prow = _unit_rows(d, (b - 1) % nbc)(r)
            kpr, vpr = (kc_ref, vc_ref) if b > 0 else (kp_ref, vp_ref)
            q = q_ref[rows, :].astype(BF16)
            kcat = jnp.concatenate([kpr[prow, :], kc_ref[rows, :]], axis=0).astype(BF16)
            vcat = jnp.concatenate([vpr[prow, :], vc_ref[rows, :]], axis=0).astype(BF16)
            q2 = jnp.concatenate([jnp.where(lm, q, jnp.zeros_like(q)) for lm in lms], axis=0) * SCALE
            return p, rows, plain if b > 0 else first, vcat, _dot(q2, kcat, NT)

        units = list(_attn_units())
        nxt = scores(units[0])
        for k in range(len(units)):
            p, rows, mask2, vcat, s = nxt
            if k + 1 < len(units):
                nxt = scores(units[k + 1])
            s_scr[...] = s
            for r0 in range(0, 2 * BLOCK, SOFTMAX_ROWS):
                sl = slice(r0, r0 + SOFTMAX_ROWS)
                part = jnp.where(mask2[sl], s_scr[sl, :], NEG_INF)
                m = jnp.max(part, axis=-1, keepdims=True)
                e = jnp.exp(part - m)
                l = jnp.sum(e, axis=-1, keepdims=True)
                e_scr[sl, :] = e.astype(BF16)
                den_scr[sl, :] = jnp.broadcast_to(l, (SOFTMAX_ROWS, SLAB))
                lsew_scr[sl, :] = jnp.broadcast_to(m + jnp.log(l), (SOFTMAX_ROWS, SLAB))
            o2 = _dot(e_scr[...], vcat, NN) / den_scr[...]
            o_scr[p][rows, :] = jnp.where(lms[0], o2[:BLOCK], o2[BLOCK:])
            l_scr[p][rows, :] = jnp.where(lms[0], lsew_scr[:BLOCK, :], lsew_scr[BLOCK:, :])
        ls = [l_scr[p][...] for p in range(n)]
        top = functools.reduce(jnp.maximum, ls)
        es = [jnp.exp(l - top) for l in ls]
        den = functools.reduce(jnp.add, es)
        num = functools.reduce(jnp.add, [e * o_scr[p][...] for p, e in enumerate(es)])
        attn = num / den
        attn_ref[...] = attn
        attn16_ref[...] = attn.astype(BF16)
        lse_ref[...] = top + jnp.log(den)
        pl.when(step == (2 * n_steps) // 3)(forward)
        pl.when(step == n_steps - 1)(finish)

    out = pl.pallas_call(
        body, name=name, grid=(N_SLABS, nc), in_specs=_attn_in_specs(nc, 0) + [ANY] * npay,
        out_specs=[pl.BlockSpec((CHUNK, SLAB), lambda h, c: (c, h))] * 3 + [ANY] * npay,
        out_shape=[jax.ShapeDtypeStruct((S, ATTN_WIDTH), F32)] * 2 + [jax.ShapeDtypeStruct((S, ATTN_WIDTH), BF16)]
        + _gathered_shapes(payload),
        scratch_shapes=[pltpu.VMEM((CHUNK, SLAB), F32)] * (2 * n)
        + [pltpu.VMEM((2 * BLOCK, 2 * BLOCK), F32), pltpu.VMEM((2 * BLOCK, 2 * BLOCK), BF16),
           pltpu.VMEM((2 * BLOCK, SLAB), F32), pltpu.VMEM((2 * BLOCK, SLAB), F32)] + _gather_sems(npay),
        compiler_params=_cp("arbitrary", "arbitrary"),
    )(proj, proj, proj, proj, proj, *payload)
    return (*out[:3], out[3:])


def _attn_bwd(proj, dcat, attn, lse, grads, blocks, *, name):
    S = proj.shape[0]
    nc = S // CHUNK
    ng, nb = len(grads), len(blocks)
    n = len(DILATIONS)
    n_steps = N_SLABS * (nc + 1)
    recv_shapes, exchange_sems = _exchange_buffers(grads)

    def body(*refs):
        q_ref, kp_ref, kc_ref, vp_ref, vc_ref, do_ref, o_ref, lse_ref = refs[:8]
        g_refs, b_refs = refs[8:8 + ng], refs[8 + ng:8 + ng + nb]
        outs = refs[8 + ng + nb:]
        dq_ref, dk_ref, dv_ref = outs[:3]
        r_refs, all_refs = outs[3:3 + ng], outs[3 + ng:3 + ng + nb]
        scr = outs[3 + ng + nb:]
        dk_prev, dv_prev = scr[:2]
        delta_h, lse_h = scr[2:4], scr[4:6]
        dq_p, dk_own, dk_back, dv_own, dv_back = (scr[6 + n * k:6 + n * (k + 1)] for k in range(5))
        start, finish = _exchange_phases(g_refs, r_refs, *scr[6 + 5 * n:9 + 5 * n])
        gather_start, gather_forward, gather_finish = _gather_phases(b_refs, all_refs, *scr[9 + 5 * n:])
        c = pl.program_id(1)
        step = pl.program_id(0) * (nc + 1) + c

        @pl.when(step == 0)
        def _():
            gather_start()
            start()

        @pl.when(c == 0)
        def _():
            dk_prev[...] = jnp.zeros_like(dk_prev)
            dv_prev[...] = jnp.zeros_like(dv_prev)

        @pl.when(c < nc)
        def _():
            lms = _lane_masks()
            plain, first = (jnp.tile(_band_mask(f), (2, 1)) for f in (False, c == 0))
            prod = do_ref[...] * o_ref[...]
            lse = lse_ref[...]
            lse_other = pltpu.roll(lse, HEAD_DIM, 1)
            for h, lm in enumerate(lms):
                delta = jnp.sum(jnp.where(lm, prod, 0.0), axis=-1, keepdims=True)
                delta_h[h][...] = jnp.broadcast_to(delta, (CHUNK, SLAB))
                lse_h[h][...] = jnp.where(lm, lse, lse_other)
            wide = lambda refs, rows: jnp.tile(jnp.concatenate([r[rows, :] for r in refs], axis=0), (1, 2))
            stack = lambda f: jnp.concatenate([f(lm) for lm in lms], axis=0)

            def scores(unit):
                p, d, b, r, nbc = unit
                rows = _unit_rows(d, b)(r)
                prow = _unit_rows(d, (b - 1) % nbc)(r)
                kpr, vpr = (kc_ref, vc_ref) if b > 0 else (kp_ref, vp_ref)
                q = q_ref[rows, :].astype(BF16)
                kcat = jnp.concatenate([kpr[prow, :], kc_ref[rows, :]], axis=0).astype(BF16)
                vcat = jnp.concatenate([vpr[prow, :], vc_ref[rows, :]], axis=0).astype(BF16)
                do = do_ref[rows, :]
                q2 = stack(lambda lm: jnp.where(lm, q, jnp.zeros_like(q))) * SCALE
                do2 = stack(lambda lm: jnp.where(lm, do, 0.0)).astype(BF16)
                return dict(p=p, rows=rows, prow=prow, mask2=plain if b > 0 else first, kcat=kcat, q2=q2, do2=do2,
                            s=_dot(q2, kcat, NT), dp=_dot(do2, vcat, NT))

            units = list(_attn_units())
            nxt = scores(units[0])
            for k in range(len(units)):
                u = nxt
                if k + 1 < len(units):
                    nxt = scores(units[k + 1])
                p, rows, prow, kcat = u["p"], u["rows"], u["prow"], u["kcat"]
                e = jnp.where(u["mask2"], jnp.exp(u["s"] - wide(lse_h, rows)), 0.0)
                ds = (e * (u["dp"] - wide(delta_h, rows))).astype(BF16)
                dq2 = _dot(ds, kcat, NN) * SCALE
                dq = jnp.where(lms[0], dq2[:BLOCK], dq2[BLOCK:])
                dkc = _dot(ds, u["q2"], TN)
                dvc = _dot(e.astype(BF16), u["do2"], TN)
                dq_p[p][rows, :] = dq
                dk_own[p][rows, :] = dkc[BLOCK:]
                dv_own[p][rows, :] = dvc[BLOCK:]
                dk_back[p][prow, :] = dkc[:BLOCK]
                dv_back[p][prow, :] = dvc[:BLOCK]
            dq_ref[...] = functools.reduce(jnp.add, [r[...] for r in dq_p]).astype(BF16)
            for prev, own, back, out_ref in ((dk_prev, dk_own, dk_back, dk_ref), (dv_prev, dv_own, dv_back, dv_ref)):
                for p, d in enumerate(DILATIONS):
                    tail = CHUNK - BLOCK * d
                    prev[tail:, :] += back[p][tail:, :]
                out_ref[...] = prev[...].astype(BF16)
                prev[...] = functools.reduce(jnp.add, [r[...] for r in own])
                for p, d in enumerate(DILATIONS):
                    tail = CHUNK - BLOCK * d
                    if tail:
                        prev[:tail, :] += back[p][:tail, :]

        @pl.when(c == nc)
        def _():
            dk_ref[...] = dk_prev[...].astype(BF16)
            dv_ref[...] = dv_prev[...].astype(BF16)

        pl.when(step == (2 * n_steps) // 3)(gather_forward)

        @pl.when(step == n_steps - 1)
        def _():
            gather_finish()
            finish()

    blk = lambda f: pl.BlockSpec((CHUNK, SLAB), f)
    late = lambda h, c: (jnp.maximum(c - 1, 0), h)
    out = pl.pallas_call(
        body, name=name, grid=(N_SLABS, nc + 1), in_specs=_attn_in_specs(nc, 3) + [ANY] * (ng + nb),
        out_specs=[blk(lambda h, c: (jnp.minimum(c, nc - 1), h)), blk(late), blk(late)] + [ANY] * (ng + nb),
        out_shape=[jax.ShapeDtypeStruct((S, ATTN_WIDTH), BF16)] * 3 + recv_shapes + _gathered_shapes(blocks),
        scratch_shapes=[pltpu.VMEM((CHUNK, SLAB), F32)] * (6 + 5 * n) + exchange_sems + _gather_sems(nb),
        compiler_params=_cp("arbitrary", "arbitrary"),
    )(proj, proj, proj, proj, proj, dcat, attn, lse, *grads, *blocks)
    return out[:3], out[3:3 + ng], out[3 + ng:]


def _split_bf16(a):
    hi = a.astype(BF16)
    lo = (a - hi.astype(F32)).astype(BF16)
    return hi, lo


def _pooled(ug, halo_g, w, row0, tm):
    ext = jnp.concatenate([halo_g, ug], axis=0)
    hi, lo = _split_bf16(ext)
    rr = lax.broadcasted_iota(jnp.int32, (tm, tm + HALO), 0)
    cc = lax.broadcasted_iota(jnp.int32, (tm, tm + HALO), 1)
    back = rr + HALO - cc
    win = ((back >= 0) & (back < w)).astype(BF16)
    wsum = _dot(win, hi, NN) + _dot(win, lo, NN)
    rows = row0 + lax.broadcasted_iota(jnp.int32, (tm, 1), 0)
    inv = 1.0 / jnp.minimum(rows + 1, w).astype(F32)
    return wsum * inv - ug


def _pool_fwd(u, u_col, pool_w, pool_scale, *, name, tm=256):
    S, W = u.shape[0], POOL_WIDTH
    G = POOL_GROUP_DIM

    def body(u_ref, h_ref, w_ref, s_ref, o_ref):
        i = pl.program_id(0)
        uv = u_ref[...]
        halo = jnp.where(i > 0, h_ref[...], 0.0)
        sls = [slice(g * G, (g + 1) * G) for g in range(len(POOL_WINDOWS))]
        pooled = [_pooled(uv[:, sl], halo[:, sl], w, i * tm, tm) for sl, w in zip(sls, POOL_WINDOWS)]
        zs = [_dot(p.astype(BF16), w_ref[g].astype(BF16), NN) for g, p in enumerate(pooled)]
        for sl, z in zip(sls, zs):
            o_ref[:, sl] = (z * s_ref[:, sl]).astype(BF16)

    per = tm // HALO
    return pl.pallas_call(
        body, name=name, grid=(S // tm,),
        in_specs=[pl.BlockSpec((tm, W), lambda i: (i, u_col)),
                  pl.BlockSpec((HALO, W), lambda i: (jnp.maximum(i * per - 1, 0), u_col)),
                  pl.BlockSpec((len(POOL_WINDOWS), G, G), lambda i: (0, 0, 0)),
                  pl.BlockSpec((1, W), lambda i: (0, 0))],
        out_specs=pl.BlockSpec((tm, W), lambda i: (i, 0)),
        out_shape=jax.ShapeDtypeStruct((S, W), BF16),
        compiler_params=_cp("parallel"),
    )(u, u, pool_w, pool_scale)


def _pool_bwd(u, u_col, dy, dy_col, pool_w, pool_scale, *, name, tm=256):
    S, W = u.shape[0], POOL_WIDTH
    G = POOL_GROUP_DIM
    nt = S // tm

    def body(u_ref, h_ref, dy_ref, dyn_ref, w_ref, s_ref, du_ref, gw_ref, gs_ref):
        i = pl.program_id(0)

        @pl.when(i == 0)
        def _():
            gw_ref[...] = jnp.zeros_like(gw_ref)
            gs_ref[...] = jnp.zeros_like(gs_ref)

        uv = u_ref[...]
        halo = jnp.where(i > 0, h_ref[...], 0.0)
        dyv = dy_ref[...]
        dyn = jnp.where(i < nt - 1, dyn_ref[...], 0.0)
        rr = lax.broadcasted_iota(jnp.int32, (tm, tm + HALO), 0)
        cc = lax.broadcasted_iota(jnp.int32, (tm, tm + HALO), 1)
        rows_ext = i * tm + lax.broadcasted_iota(jnp.int32, (tm + HALO, 1), 0)
        groups = list(enumerate(POOL_WINDOWS))
        sls = [slice(g * G, (g + 1) * G) for g, _ in groups]
        wgs = [w_ref[g].astype(BF16) for g, _ in groups]
        pooled = [_pooled(uv[:, sl], halo[:, sl], w, i * tm, tm).astype(BF16) for sl, (_, w) in zip(sls, groups)]
        dzs = [dyv[:, sl] * s_ref[:, sl] for sl in sls]
        dz_ext = [jnp.concatenate([dz, dyn[:, sl] * s_ref[:, sl]], axis=0).astype(BF16) for dz, sl in zip(dzs, sls)]
        dp_ext = [_dot(d, wg, NT) for d, wg in zip(dz_ext, wgs)]
        zs = [_dot(p, wg, NN) for p, wg in zip(pooled, wgs)]
        for (g, w), sl, p, dz, z, dp in zip(groups, sls, pooled, dzs, zs, dp_ext):
            gw_ref[g] += _dot(p, dz.astype(BF16), TN)
            gs_ref[:, sl] += jnp.sum(dyv[:, sl] * z, axis=0, keepdims=True)
            inv_ext = 1.0 / jnp.minimum(rows_ext + 1, w).astype(F32)
            hi, lo = _split_bf16(dp * inv_ext)
            ahead = cc - rr
            win = ((ahead >= 0) & (ahead < w)).astype(BF16)
            du_ref[:, sl] = (_dot(win, hi, NN) + _dot(win, lo, NN) - dp[:tm]).astype(BF16)

    per = tm // HALO
    nh = S // HALO
    return pl.pallas_call(
        body, name=name, grid=(nt,),
        in_specs=[pl.BlockSpec((tm, W), lambda i: (i, u_col)),
                  pl.BlockSpec((HALO, W), lambda i: (jnp.maximum(i * per - 1, 0), u_col)),
                  pl.BlockSpec((tm, W), lambda i: (i, dy_col)),
                  pl.BlockSpec((HALO, W), lambda i: (jnp.minimum((i + 1) * per, nh - 1), dy_col)),
                  pl.BlockSpec((len(POOL_WINDOWS), G, G), lambda i: (0, 0, 0)),
                  pl.BlockSpec((1, W), lambda i: (0, 0))],
        out_specs=[pl.BlockSpec((tm, W), lambda i: (i, 0)),
                   pl.BlockSpec((len(POOL_WINDOWS), G, G), lambda i: (0, 0, 0)),
                   pl.BlockSpec((1, W), lambda i: (0, 0))],
        out_shape=[jax.ShapeDtypeStruct((S, W), BF16),
                   jax.ShapeDtypeStruct((len(POOL_WINDOWS), G, G), F32),
                   jax.ShapeDtypeStruct((1, W), F32)],
        compiler_params=_cp("arbitrary"),
    )(u, u, dy, dy, pool_w, pool_scale)


GELU_K0 = math.sqrt(2.0 / math.pi)
GELU_K1 = 0.044715


def _gelu_parts(x):
    x2 = x * x
    t = jnp.tanh(x * (GELU_K0 + (GELU_K0 * GELU_K1) * x2))
    hp = 0.5 + 0.5 * t
    gelu = x * hp
    dgelu = hp + (x * (hp * (1.0 - t))) * (GELU_K0 + (3.0 * GELU_K0 * GELU_K1) * x2)
    return gelu, dgelu


def _shifted(ext, halo):
    return (pltpu.roll(ext, 2, 0)[halo:], pltpu.roll(ext, 1, 0)[halo:], ext[halo:])


def _conv(sh, w, b):
    return b + (sh[0] * w[0:1] + sh[1] * w[1:2] + sh[2] * w[2:3])


F32_ROWS = 8


def _ffn_up_glu(h, w_up_t, conv_w, conv_b, *, name, tm=2048, tn=256, sub=256):
    S, K = h.shape
    F = D_FF
    nj = F // tn

    def body(h_ref, wg_ref, wv_ref, cwg_ref, cwv_ref, cbg_ref, cbv_ref,
             ug_ref, uv_ref, cg_ref, cv_ref, y_ref, carry):
        i = pl.program_id(0)
        j = pl.program_id(1)

        w_cat = jnp.concatenate([wg_ref[...], wv_ref[...]], axis=0)
        conv_w_b = ((cwg_ref[...], cbg_ref[...]), (cwv_ref[...], cbv_ref[...]))
        halo = [jnp.where(i > 0, carry[j, s], 0.0) for s in range(2)]
        u_next = _dot(h_ref[0:sub, :], w_cat, NT)
        for a in range(0, tm, sub):
            u16 = u_next.astype(BF16)
            if a + sub < tm:
                u_next = _dot(h_ref[a + sub:a + 2 * sub, :], w_cat, NT)
            ug_ref[a:a + sub, :] = u16[:, :tn]
            uv_ref[a:a + sub, :] = u16[:, tn:]
            c = []
            for s, (cw, cb) in enumerate(conv_w_b):
                u = u16[:, s * tn:(s + 1) * tn].astype(F32)
                ext = jnp.concatenate([halo[s], u], axis=0)
                c.append(_conv(_shifted(ext, F32_ROWS), cw, cb))
                halo[s] = u[sub - F32_ROWS:]
            cg_ref[a:a + sub, :] = c[0].astype(BF16)
            cv_ref[a:a + sub, :] = c[1].astype(BF16)
            gelu, _ = _gelu_parts(c[0])
            y_ref[a:a + sub, :] = (gelu * c[1]).astype(BF16)
        for s in range(2):
            carry[j, s] = halo[s]

    tile = pl.BlockSpec((tm, tn), lambda i, j: (i, j))
    vec = lambda rows, off: pl.BlockSpec((rows, tn), lambda i, j: (0, j + off))
    return pl.pallas_call(
        body, name=name, grid=(S // tm, nj),
        in_specs=[pl.BlockSpec((tm, K), lambda i, j: (i, 0)),
                  pl.BlockSpec((tn, K), lambda i, j: (j, 0)), pl.BlockSpec((tn, K), lambda i, j: (j + nj, 0)),
                  vec(3, 0), vec(3, nj), vec(1, 0), vec(1, nj)],
        out_specs=[tile] * 5,
        out_shape=[jax.ShapeDtypeStruct((S, F), BF16)] * 5,
        scratch_shapes=[pltpu.VMEM((nj, 2, F32_ROWS, tn), F32)],
        compiler_params=_cp("arbitrary", "arbitrary"),
    )(h, w_up_t, w_up_t, conv_w, conv_w, conv_b, conv_b)


def _ffn_glu_bwd(u_g, u_v, c_g, c_v, df, w_down, h, conv_w, *, name, tm=2048, tn=256, sub=256):
    S = u_g.shape[0]
    F = D_FF
    D = df.shape[1]
    nj = F // tn
    nt = S // tm

    def body(ug_ref, uv_ref, cg_ref, cgn_ref, cv_ref, cvn_ref, df_ref, dfn_ref, wd_ref, h_ref, wg_ref, wv_ref,
             dug_ref, duv_ref, gug_ref, guv_ref, gd_ref, gwg_ref, gwv_ref, gbg_ref, gbv_ref,
             acc_u, acc_d):
        i = pl.program_id(1)

        @pl.when(i == 0)
        def _():
            for r in (gwg_ref, gwv_ref, gbg_ref, gbv_ref, acc_u, acc_d):
                r[...] = jnp.zeros_like(r)

        wg, wv = wg_ref[...], wv_ref[...]
        wd = wd_ref[...]
        dfn = jnp.where(i < nt - 1, dfn_ref[...], jnp.zeros_like(dfn_ref))
        n_ext = sub + HALO

        def ahead(dc):
            return dc[:sub], pltpu.roll(dc, n_ext - 1, 0)[:sub], pltpu.roll(dc, n_ext - 2, 0)[:sub]

        def ext(ref, nxt, a):
            b = a + sub
            return jnp.concatenate([ref[a:b, :], ref[b:b + HALO, :] if b < tm else nxt], axis=0)

        dy_next = _dot(ext(df_ref, dfn, 0), wd, NT)
        for a in range(0, tm, sub):
            b = a + sub
            dy_ext = dy_next
            if b < tm:
                dy_next = _dot(ext(df_ref, dfn, b), wd, NT)
            cg = ext(cg_ref, cgn_ref[...], a).astype(F32)
            cv = ext(cv_ref, cvn_ref[...], a).astype(F32)
            df_sub = df_ref[a:b, :]
            gelu, dgelu = _gelu_parts(cg)
            dcs_g = ahead(dy_ext * cv * dgelu)
            dcs_v = ahead(dy_ext * gelu)
            du_g = (dcs_g[0] * wg[2:3] + dcs_g[1] * wg[1:2] + dcs_g[2] * wg[0:1]).astype(BF16)
            du_v = (dcs_v[0] * wv[2:3] + dcs_v[1] * wv[1:2] + dcs_v[2] * wv[0:1]).astype(BF16)
            dug_ref[a:b, :] = du_g
            duv_ref[a:b, :] = du_v
            acc_u[...] += _dot(jnp.concatenate([du_g, du_v], axis=1), h_ref[a:b, :], TN)
            acc_d[...] += _dot((gelu[:sub] * cv[:sub]).astype(BF16), df_sub, TN)
            for dcs, u_ref, gw_ref, gb_ref in ((dcs_g, ug_ref, gwg_ref, gbg_ref), (dcs_v, uv_ref, gwv_ref, gbv_ref)):
                u = u_ref[a:b, :].astype(F32)
                gb_ref[...] += jnp.sum(dcs[0], axis=0, keepdims=True)
                for k in range(3):
                    gw_ref[k:k + 1, :] += jnp.sum(dcs[2 - k] * u, axis=0, keepdims=True)

        @pl.when(i == nt - 1)
        def _():
            gug_ref[...] = acc_u[:tn, :].astype(BF16)
            guv_ref[...] = acc_u[tn:, :].astype(BF16)
            gd_ref[...] = acc_d[...].astype(BF16)

    per = tm // HALO
    nh = S // HALO
    hnext = lambda i: jnp.minimum((i + 1) * per, nh - 1)
    tile = pl.BlockSpec((tm, tn), lambda j, i: (i, j))
    hn = pl.BlockSpec((HALO, tn), lambda j, i: (hnext(i), j))
    vec = lambda rows, off: pl.BlockSpec((rows, tn), lambda j, i: (0, j + off))
    wide = pl.BlockSpec((tm, D), lambda j, i: (i, 0))
    wrow = pl.BlockSpec((tn, D), lambda j, i: (j, 0))
    return pl.pallas_call(
        body, name=name, grid=(nj, nt),
        in_specs=[tile, tile, tile, hn, tile, hn, wide, pl.BlockSpec((HALO, D), lambda j, i: (hnext(i), 0)),
                  wrow, wide, vec(3, 0), vec(3, nj)],
        out_specs=[tile, tile, wrow, wrow, wrow, vec(3, 0), vec(3, 0), vec(1, 0), vec(1, 0)],
        out_shape=[jax.ShapeDtypeStruct((S, F), BF16), jax.ShapeDtypeStruct((S, F), BF16),
                   jax.ShapeDtypeStruct((F, D), BF16), jax.ShapeDtypeStruct((F, D), BF16),
                   jax.ShapeDtypeStruct((F, D), BF16),
                   jax.ShapeDtypeStruct((3, F), F32), jax.ShapeDtypeStruct((3, F), F32),
                   jax.ShapeDtypeStruct((1, F), F32), jax.ShapeDtypeStruct((1, F), F32)],
        scratch_shapes=[pltpu.VMEM((2 * tn, D), F32), pltpu.VMEM((tn, D), F32)],
        compiler_params=_cp("parallel", "arbitrary"),
    )(u_g, u_v, c_g, c_g, c_v, c_v, df, df, w_down, h, conv_w, conv_w)


def _sum_partials(parts, *, name, tr):
    _, R, C = parts.shape

    def body(p_ref, o_ref):
        tot = p_ref[0].astype(F32)
        for j in range(1, N_DEV):
            tot = tot + p_ref[j].astype(F32)
        o_ref[...] = tot

    return pl.pallas_call(
        body, name=name, grid=(R // tr,),
        in_specs=[pl.BlockSpec((N_DEV, tr, C), lambda i: (0, i, 0))],
        out_specs=pl.BlockSpec((tr, C), lambda i: (i, 0)),
        out_shape=jax.ShapeDtypeStruct((R, C), F32),
        compiler_params=_cp("parallel"),
    )(parts)


def _adamw(w, g, m, v, *, name, tr):
    R, C = w.shape
    c1 = 1.0 - ADAM_B1 ** ADAM_STEP
    c2 = 1.0 - ADAM_B2 ** ADAM_STEP

    def body(w_ref, g_ref, m_ref, v_ref, d_ref, nm_ref, nv_ref):
        g = g_ref[...]
        nm = ADAM_B1 * m_ref[...] + (1.0 - ADAM_B1) * g
        nv = ADAM_B2 * v_ref[...] + (1.0 - ADAM_B2) * (g * g)
        d_ref[...] = -ADAM_LR * ((nm / c1) / (jnp.sqrt(nv / c2) + ADAM_EPS) + ADAM_WD * w_ref[...])
        nm_ref[...] = nm
        nv_ref[...] = nv

    spec = pl.BlockSpec((tr, C), lambda i: (i, 0))
    return pl.pallas_call(
        body, name=name, grid=(R // tr,), in_specs=[spec] * 4, out_specs=[spec] * 3,
        out_shape=[jax.ShapeDtypeStruct((R, C), F32)] * 3,
        compiler_params=_cp("parallel"),
    )(w, g, m, v)


def _mesh_pos():
    return lax.axis_index("x"), lax.axis_index("y"), lax.axis_index("c")


def _gather_phases(x_refs, out_refs, send_sems, recv_sems, local_sems):
    x, y, c = _mesh_pos()
    me, sibling = (x, y, c), (x, y, 1 - c)
    chips = [(1 - x, y), (x, 1 - y), (1 - x, 1 - y)]
    arrays = range(len(x_refs))

    def slot(a, px, py, pc):
        return out_refs[a].at[4 * px + 2 * py + pc]

    def copy(a, k, block, to, own=False):
        return pltpu.make_async_remote_copy(
            src_ref=x_refs[a] if own else slot(a, *block), dst_ref=slot(a, *block),
            send_sem=send_sems.at[a, k], recv_sem=recv_sems.at[a, k], device_id=to, device_id_type=MESH)

    mine = [pltpu.make_async_copy(x_refs[a], slot(a, *me), local_sems.at[a]) for a in arrays]
    first = [copy(a, 0, me, sibling, own=True) for a in arrays]
    first += [copy(a, 1 + j, me, (*chip, c), own=True) for j, chip in enumerate(chips) for a in arrays]
    passed = [[copy(a, 4 + j, (*chip, c), sibling) for a in arrays] for j, chip in enumerate(chips)]

    def start():
        for cp in mine + first:
            cp.start()

    def forward():
        for j, chip in enumerate(chips):
            for a in arrays:
                copy(a, 1 + j, (*chip, c), me).wait_recv()
                passed[j][a].start()

    def finish():
        for a in arrays:
            copy(a, 0, sibling, me).wait_recv()
            for j, chip in enumerate(chips):
                copy(a, 4 + j, (*chip, 1 - c), me).wait_recv()
        for cp in first + [cp for row in passed for cp in row]:
            cp.wait_send()
        for cp in mine:
            cp.wait()

    return start, forward, finish


def _gather_sems(n):
    return [pltpu.SemaphoreType.DMA((n, 7)), pltpu.SemaphoreType.DMA((n, 7)), pltpu.SemaphoreType.DMA((n,))]


def _gathered_shapes(blocks):
    return [jax.ShapeDtypeStruct((N_DEV,) + b.shape, b.dtype) for b in blocks]


def _all_reduce_small(block, gathered, *, name):
    r0, r1 = block.shape[0], gathered.shape[1]

    def body(x_ref, more_ref, all_ref, sum_ref, *sems):
        for phase in _gather_phases([x_ref], [all_ref], *sems):
            phase()
        for ref, rows in ((all_ref, slice(0, r0)), (more_ref, slice(r0, r0 + r1))):
            tot = ref[0]
            for j in range(1, N_DEV):
                tot = tot + ref[j]
            sum_ref[rows, :] = tot

    return pl.pallas_call(
        body, name=name, in_specs=[VMEM, VMEM], out_specs=[VMEM, VMEM],
        out_shape=[jax.ShapeDtypeStruct((N_DEV,) + block.shape, block.dtype),
                   jax.ShapeDtypeStruct((r0 + r1, block.shape[1]), block.dtype)],
        scratch_shapes=_gather_sems(1),
        compiler_params=pltpu.CompilerParams(vmem_limit_bytes=V7X_VMEM_LIMIT),
    )(block, gathered)[1]


def _exchange_phases(g_refs, r_refs, send_sems, recv_sems, local_sems):
    x, y, c = _mesh_pos()
    me = 4 * x + 2 * y + c
    owns, remote = [], []
    for k, (g_ref, r_ref) in enumerate(zip(g_refs, r_refs)):
        rows = g_ref.shape[0] // N_DEV
        owns.append(pltpu.make_async_copy(g_ref.at[pl.ds(me * rows, rows)], r_ref.at[me], local_sems.at[k]))
        for p in range(1, N_DEV):
            px, py, pc = x ^ (p >> 2), y ^ ((p >> 1) & 1), c ^ (p & 1)
            peer = 4 * px + 2 * py + pc
            link = dict(send_sem=send_sems.at[k, p], recv_sem=recv_sems.at[k, p],
                        device_id=(px, py, pc), device_id_type=MESH)
            src = g_ref.at[pl.ds(peer * rows, rows)]
            send = pltpu.make_async_remote_copy(src_ref=src, dst_ref=r_ref.at[me], **link)
            arrival = pltpu.make_async_remote_copy(src_ref=src, dst_ref=r_ref.at[peer], **link)
            remote.append((send, arrival))

    def start():
        for own in owns:
            own.start()
        for send, _ in remote:
            send.start()

    def finish():
        for _, arrival in remote:
            arrival.wait_recv()
        for send, _ in remote:
            send.wait_send()
        for own in owns:
            own.wait()

    return start, finish


def _exchange_buffers(grads):
    n = len(grads)
    shapes = [jax.ShapeDtypeStruct((N_DEV, g.shape[0] // N_DEV, g.shape[1]), g.dtype) for g in grads]
    sems = [pltpu.SemaphoreType.DMA((n, N_DEV)), pltpu.SemaphoreType.DMA((n, N_DEV)),
            pltpu.SemaphoreType.DMA((n,))]
    return shapes, sems


def _unpack_gathered(gathered):
    w_out, w_up_t, w_down = (g.reshape(-1, D_MODEL) for g in gathered[:3])
    width = 2 * D_FF // N_DEV
    conv_w = jnp.transpose(gathered[3][:, :3, :width], (1, 0, 2)).reshape(3, 2 * D_FF)
    return w_out, w_up_t, w_down, conv_w


def _rest_payload(w_out, w_up, w_down, conv_w):
    rows, cols = conv_w.shape
    conv_w = jnp.pad(conv_w, ((0, (-rows) % F32_ROWS), (0, (-cols) % LANES)))
    return [w_out.astype(BF16), w_up.T.astype(BF16), w_down.astype(BF16), conv_w]


def _device_step(x, target, g_mix_pre, w_in_t_block, rest_payload, pool_w, pool_scale, g_mix_post, g_ffn_pre,
                 conv_b, g_ffn_post):
    h1, w_in_t = _rms_norm_gather(x, g_mix_pre, w_in_t_block, name="rms_mix_pre")
    w_in_t = w_in_t.reshape(-1, D_MODEL)
    proj = _matmul(h1, w_in_t, trans_b=True, out_dtype=F32, tm=1024, tn=4 * ATTN_WIDTH, name="proj")
    attn, lse, attn16, gathered = _attn_fwd(proj, rest_payload, name="attn_fwd")
    w_out, w_up_t, w_down, conv_w = _unpack_gathered(gathered)
    pool = _pool_fwd(proj, 3, pool_w, pool_scale, name="pool_fwd")
    mixed, x2, h2 = _mix_out(attn16, pool, w_out, x, g_mix_post, g_ffn_pre, name="mix_out")
    u_g, u_v, c_g, c_v, y = _ffn_up_glu(h2, w_up_t, conv_w, conv_b, name="ffn_up_glu")
    df, d_out, loss_blk, gg_ffn_post = _ffn_out(y, w_down, x2, target, g_ffn_post, name="ffn_out")
    du_g, du_v, gw_up_g, gw_up_v, gw_down, gcw_g, gcw_v, gcb_g, gcb_v = _ffn_glu_bwd(
        u_g, u_v, c_g, c_v, df, w_down, h2, conv_w, name="ffn_glu_bwd")
    gw_up_t = jnp.concatenate([gw_up_g, gw_up_v], axis=0)
    dx2, gg_ffn_pre, dmixed, gg_mix_post = _dgrad_norm(
        [du_g, du_v], w_up_t, d_out, x2, g_ffn_pre, (mixed, g_mix_post), [], name="ffn_up_dgrad")
    gw_out = _matmul_tn([attn16, pool], dmixed, name="grad_w_out")
    dcat = _matmul(dmixed, w_out, trans_b=True, out_dtype=F32, tm=1024, tn=1024, name="mix_out_dgrad")
    d_pool_in, g_pool_w, g_pool_scale = _pool_bwd(proj, 3, dcat, 1, pool_w, pool_scale, name="pool_bwd")
    early = dict(g_mix_post=gg_mix_post, g_ffn_pre=gg_ffn_pre, g_ffn_post=gg_ffn_post, pool_scale=g_pool_scale,
                 conv_b=jnp.concatenate([gcb_g, gcb_v], axis=1), pool_w=g_pool_w)
    early_block = _pack_rows([early[k] for k in _SMALL[1:]] + [jnp.concatenate([gcw_g, gcw_v], axis=1), loss_blk])
    dqkv, (r_out, r_up_t, r_down), (small_gathered,) = _attn_bwd(
        proj, dcat, attn, lse, [gw_out, gw_up_t, gw_down], [early_block], name="attn_bwd")
    dproj = list(dqkv) + [d_pool_in]
    gw_in_t = _matmul_tn(dproj, h1, name="grad_w_in")
    grad_x, gg_mix_pre, (r_in_t,) = _dgrad_norm(dproj, w_in_t, dx2, x, g_mix_pre, None, [gw_in_t], name="proj_dgrad")
    received = (r_in_t, r_out, r_up_t, r_down)
    return grad_x, received, gg_mix_pre, small_gathered


_SMALL = ("g_mix_pre", "g_mix_post", "g_ffn_pre", "g_ffn_post", "pool_scale", "conv_b", "pool_w")
LANES = 128


def _pack_rows(arrays):
    parts = []
    for a in arrays:
        a2 = a.reshape(-1, LANES)
        parts.append(jnp.pad(a2, ((0, (-a2.shape[0]) % 8), (0, 0))))
    return jnp.concatenate(parts, axis=0)


def _unpack_rows(packed, shapes):
    out, row = [], 0
    for shape in shapes:
        rows = math.prod(shape) // LANES
        out.append(packed[row:row + rows].reshape(shape))
        row += -(-rows // 8) * 8
    return out


def kernel(x, g_mix_pre, w_in, pool_w, pool_scale, w_out, g_mix_post, g_ffn_pre, w_up, conv_w, conv_b, w_down, g_ffn_post, loss_target, m_g_mix_pre, m_w_in, m_pool_w, m_pool_scale, m_w_out, m_g_mix_post, m_g_ffn_pre, m_w_up, m_conv_w, m_conv_b, m_w_down, m_g_ffn_post, v_g_mix_pre, v_w_in, v_pool_w, v_pool_scale, v_w_out, v_g_mix_post, v_g_ffn_pre, v_w_up, v_conv_w, v_conv_b, v_w_down, v_g_ffn_post):
    me = 4 * lax.axis_index("x") + 2 * lax.axis_index("y") + lax.axis_index("c")
    grad_x, recv, gg_mix_pre, small_gathered = _device_step(
        x[0], loss_target[0], g_mix_pre, w_in[0].T.astype(BF16),
        _rest_payload(w_out[0], w_up[0], w_down[0], conv_w[0]),
        pool_w[0], pool_scale, g_mix_post, g_ffn_pre, conv_b, g_ffn_post)

    g_in_t, g_out, g_up_t, g_down = (
        _sum_partials(r, name=f"sum_partials_{k}", tr=r.shape[1] // 2) for k, r in enumerate(recv))
    grads = {"w_in": g_in_t.T, "w_out": g_out, "w_up": g_up_t.T, "w_down": g_down}

    given = dict(g_mix_pre=g_mix_pre, g_mix_post=g_mix_post, g_ffn_pre=g_ffn_pre, g_ffn_post=g_ffn_post,
                 pool_scale=pool_scale, conv_b=conv_b, pool_w=pool_w)
    small_shapes = [given[k].shape for k in _SMALL]
    total = _all_reduce_small(_pack_rows([gg_mix_pre]), small_gathered, name="all_reduce_small")
    *small_grads, g_conv_w_all, loss_all = _unpack_rows(total, small_shapes + [(3, 2 * D_FF), (8, LANES)])
    loss = loss_all[0, 0]
    grads.update(zip(_SMALL, small_grads))
    width = 2 * D_FF // N_DEV
    grads["conv_w"] = lax.dynamic_slice_in_dim(g_conv_w_all, me * width, width, axis=1)[None]

    weights = dict(g_mix_pre=g_mix_pre, w_in=w_in, pool_w=pool_w, pool_scale=pool_scale, w_out=w_out,
                   g_mix_post=g_mix_post, g_ffn_pre=g_ffn_pre, w_up=w_up, conv_w=conv_w, conv_b=conv_b,
                   w_down=w_down, g_ffn_post=g_ffn_post)
    m_in = dict(g_mix_pre=m_g_mix_pre, w_in=m_w_in, pool_w=m_pool_w, pool_scale=m_pool_scale, w_out=m_w_out,
                g_mix_post=m_g_mix_post, g_ffn_pre=m_g_ffn_pre, w_up=m_w_up, conv_w=m_conv_w, conv_b=m_conv_b,
                w_down=m_w_down, g_ffn_post=m_g_ffn_post)
    v_in = dict(g_mix_pre=v_g_mix_pre, w_in=v_w_in, pool_w=v_pool_w, pool_scale=v_pool_scale, w_out=v_w_out,
                g_mix_post=v_g_mix_post, g_ffn_pre=v_g_ffn_pre, w_up=v_w_up, conv_w=v_conv_w, conv_b=v_conv_b,
                w_down=v_w_down, g_ffn_post=v_g_ffn_post)
    delta, new_m, new_v = {}, {}, {}
    for k in ("w_in", "w_out", "w_up", "w_down"):
        g = grads[k]
        d, nm, nv = _adamw(weights[k][0], g, m_in[k][0], v_in[k][0], name=f"adamw_{k}", tr=g.shape[0] // 2)
        grads[k], delta[k], new_m[k], new_v[k] = g[None], d[None], nm[None], nv[None]
    d, nm, nv = _adamw(weights["conv_w"][0], grads["conv_w"][0], m_in["conv_w"][0], v_in["conv_w"][0],
                       name="adamw_conv_w", tr=3)
    delta["conv_w"], new_m["conv_w"], new_v["conv_w"] = d[None], nm[None], nv[None]
    packed_w = _pack_rows([weights[k] for k in _SMALL])
    small_rows = packed_w.shape[0]
    d, nm, nv = _adamw(packed_w, total[:small_rows], _pack_rows([m_in[k] for k in _SMALL]),
                       _pack_rows([v_in[k] for k in _SMALL]), name="adamw_small", tr=small_rows)
    for k, dk, mk, vk in zip(_SMALL, _unpack_rows(d, small_shapes), _unpack_rows(nm, small_shapes),
                             _unpack_rows(nv, small_shapes)):
        delta[k], new_m[k], new_v[k] = dk, mk, vk

    order = ("g_mix_pre", "w_in", "pool_w", "pool_scale", "w_out", "g_mix_post", "g_ffn_pre", "w_up",
             "conv_w", "conv_b", "w_down", "g_ffn_post")
    return (loss, grad_x[None], *[grads[k] for k in order], *[delta[k] for k in order],
            *[new_m[k] for k in order], *[new_v[k] for k in order])
```
